```python
import jax, jax.numpy as jnp
from jax import lax
import numpy as np

D_MODEL = 1024
BATCH = 8
SEQ = 4096
DEPTH = 1

D_MIX = D_MODEL
D_POOL = D_MIX // 2
D_DN = D_MIX - D_POOL
POOL_WINDOWS = (2, 4, 8, 16)
N_POOL_GROUPS = len(POOL_WINDOWS)
POOL_GROUP = D_POOL // N_POOL_GROUPS
DN_HEAD_DIM = 128
DN_HEADS = D_DN // DN_HEAD_DIM
CONV_WIDTH = 4
CHUNK = 64
NORM_EPS = 1e-6
SPLIT_SIZES = (D_POOL, D_POOL, D_DN, D_DN, D_DN, D_DN, DN_HEADS, DN_HEADS)
D_IN = sum(SPLIT_SIZES)

kernel_name = "hymba_pool_gated_deltanet_block"


def rms_norm(x, w):
    xf = x.astype(jnp.float32)
    y = xf * lax.rsqrt(jnp.mean(xf * xf, axis=-1, keepdims=True) + NORM_EPS)
    return (y * w.astype(jnp.float32)).astype(x.dtype)


def l2_normalize(t):
    return t * lax.rsqrt(jnp.sum(t * t, axis=-1, keepdims=True) + NORM_EPS)


def pool_mixer(u, z, pool_w, pool_scale):
    B, S, _ = u.shape
    uf = u.astype(jnp.float32).reshape(B, S, N_POOL_GROUPS, POOL_GROUP)
    csum = jnp.cumsum(uf, axis=1)
    counts = jnp.arange(1, S + 1, dtype=jnp.float32)
    outs = []
    for gi, w in enumerate(POOL_WINDOWS):
        c = csum[:, :, gi]
        prev = jnp.pad(c, ((0, 0), (w, 0), (0, 0)))[:, :S]
        cnt = jnp.minimum(counts, float(w))[None, :, None]
        outs.append((c - prev) / cnt)
    mix = jnp.stack(outs, axis=2) - uf
    mix = jnp.einsum('bsgc,gcd->bsgd', mix, pool_w.astype(jnp.float32)).reshape(B, S, D_POOL)
    out = mix * pool_scale.astype(jnp.float32) * jax.nn.silu(z.astype(jnp.float32))
    return out.astype(u.dtype)


def causal_depthwise_conv(u, w):
    K, C = w.shape
    return lax.conv_general_dilated(
        u, w[:, None, :], window_strides=(1,), padding=[(K - 1, 0)],
        dimension_numbers=('NWC', 'WIO', 'NWC'), feature_group_count=C)


def gated_delta_rule(q, k, v, g, beta):
    B, H, S, dk = q.shape
    dv = v.shape[-1]
    n = S // CHUNK
    q = q * (dk ** -0.5)
    k_beta = k * beta[..., None]
    v_beta = v * beta[..., None]
    chunked = lambda t: t.reshape(B, H, n, CHUNK, t.shape[-1])
    q, k, k_beta, v_beta = chunked(q), chunked(k), chunked(k_beta), chunked(v_beta)
    gc = jnp.cumsum(g.reshape(B, H, n, CHUNK), axis=-1)
    causal = jnp.tril(jnp.ones((CHUNK, CHUNK), dtype=bool))
    strict = jnp.tril(jnp.ones((CHUNK, CHUNK), dtype=bool), k=-1)
    diff = gc[..., :, None] - gc[..., None, :]
    decay = jnp.exp(jnp.where(causal, diff, -jnp.inf))
    A = jnp.where(strict, jnp.einsum('bhncd,bhnmd->bhncm', k_beta, k) * decay, 0.0)
    eye = jnp.eye(CHUNK, dtype=jnp.float32)
    T = lax.linalg.triangular_solve(A + eye, jnp.broadcast_to(eye, A.shape),
                                    left_side=True, lower=True, unit_diagonal=True)
    u = jnp.einsum('bhncm,bhnmd->bhncd', T, v_beta)
    w = jnp.einsum('bhncm,bhnmd->bhncd', T, k_beta * jnp.exp(gc)[..., None])
    qk = jnp.einsum('bhncd,bhnmd->bhncm', q, k) * decay
    q_dec = q * jnp.exp(gc)[..., None]
    k_dec = k * jnp.exp(gc[..., -1:] - gc)[..., None]
    chunk_decay = jnp.exp(gc[..., -1])

    def step(state, xs):
        qk_i, q_dec_i, k_dec_i, u_i, w_i, dec_i = xs
        v_new = u_i - jnp.einsum('bhcd,bhde->bhce', w_i, state)
        o = jnp.einsum('bhcd,bhde->bhce', q_dec_i, state) + jnp.einsum('bhcm,bhme->bhce', qk_i, v_new)
        state = state * dec_i[..., None, None] + jnp.einsum('bhcd,bhce->bhde', k_dec_i, v_new)
        return state, o

    to_scan = lambda t: jnp.moveaxis(t, 2, 0)
    xs = (to_scan(qk), to_scan(q_dec), to_scan(k_dec), to_scan(u), to_scan(w), to_scan(chunk_decay))
    state0 = jnp.zeros((B, H, dk, dv), dtype=jnp.float32)
    _, o = lax.scan(step, state0, xs)
    return jnp.moveaxis(o, 0, 2).reshape(B, H, S, dv)


def deltanet_mixer(q, k, v, z, b, a, conv_w, a_log, dt_bias, norm_w):
    B, S, _ = q.shape
    out_dtype = q.dtype
    qkv = jnp.concatenate([q, k, v], axis=-1).astype(jnp.float32)
    qkv = jax.nn.silu(causal_depthwise_conv(qkv, conv_w.astype(jnp.float32)))
    q, k, v = jnp.split(qkv, 3, axis=-1)
    heads = lambda t: t.reshape(B, S, DN_HEADS, DN_HEAD_DIM).transpose(0, 2, 1, 3)
    q, k, v = l2_normalize(heads(q)), l2_normalize(heads(k)), heads(v)
    beta = jax.nn.sigmoid(b.astype(jnp.float32)).transpose(0, 2, 1)
    g = (-jnp.exp(a_log.astype(jnp.float32))
         * jax.nn.softplus(a.astype(jnp.float32) + dt_bias.astype(jnp.float32))).transpose(0, 2, 1)
    o = gated_delta_rule(q, k, v, g, beta).transpose(0, 2, 1, 3)
    o = o * lax.rsqrt(jnp.mean(o * o, axis=-1, keepdims=True) + NORM_EPS) * norm_w.astype(jnp.float32)
    o = o * jax.nn.silu(z.astype(jnp.float32).reshape(B, S, DN_HEADS, DN_HEAD_DIM))
    return o.reshape(B, S, D_DN).astype(out_dtype)


def _fwd_setup_inputs(seed: int = 0) -> dict:
    key = jax.random.key(seed)
    ks = jax.random.split(key, 12)
    f32 = jnp.float32
    x = jax.random.normal(ks[0], (BATCH, SEQ, D_MODEL), f32)
    norm_w = 1.0 + 0.02 * jax.random.normal(ks[1], (DEPTH, D_MODEL), f32)
    w_in = jax.random.normal(ks[2], (DEPTH, D_MODEL, D_IN), f32) * D_MODEL ** -0.5
    pool_w = jax.random.normal(ks[3], (DEPTH, N_POOL_GROUPS, POOL_GROUP, POOL_GROUP), f32) * POOL_GROUP ** -0.5
    pool_scale = 1.0 + 0.1 * jax.random.normal(ks[4], (DEPTH, D_POOL), f32)
    conv_w = jax.random.normal(ks[5], (DEPTH, CONV_WIDTH, 3 * D_DN), f32) * CONV_WIDTH ** -0.5
    a_log = jnp.log(jax.random.uniform(ks[6], (DEPTH, DN_HEADS), f32, 1.0, 16.0))
    dt = jnp.exp(jax.random.uniform(ks[7], (DEPTH, DN_HEADS), f32, np.log(1e-3), np.log(1e-1)))
    dt_bias = dt + jnp.log(-jnp.expm1(-dt))
    dn_norm_w = 1.0 + 0.02 * jax.random.normal(ks[8], (DEPTH, DN_HEAD_DIM), f32)
    w_out = jax.random.normal(ks[9], (DEPTH, D_MIX, D_MODEL), f32) * D_MIX ** -0.5
    final_norm_w = 1.0 + 0.02 * jax.random.normal(ks[10], (D_MODEL,), f32)
    return {"x": x, "norm_w": norm_w, "w_in": w_in, "pool_w": pool_w, "pool_scale": pool_scale,
            "conv_w": conv_w, "a_log": a_log, "dt_bias": dt_bias, "dn_norm_w": dn_norm_w,
            "w_out": w_out, "final_norm_w": final_norm_w}


def _fwd_reference(x, norm_w, w_in, pool_w, pool_scale, conv_w, a_log, dt_bias, dn_norm_w, w_out, final_norm_w):
    offsets = np.cumsum((0,) + SPLIT_SIZES)
    h = x
    for layer in range(DEPTH):
        n = rms_norm(h, norm_w[layer])
        proj = jnp.einsum('bsd,de->bse', n, w_in[layer])
        pu, pz, q, k, v, dz, b, a = [proj[..., int(offsets[i]):int(offsets[i + 1])]
                                     for i in range(len(SPLIT_SIZES))]
        y_pool = pool_mixer(pu, pz, pool_w[layer], pool_scale[layer])
        y_dn = deltanet_mixer(q, k, v, dz, b, a, conv_w[layer], a_log[layer],
                              dt_bias[layer], dn_norm_w[layer])
        y = jnp.concatenate([y_pool, y_dn], axis=-1)
        h = h + jnp.einsum('bse,ed->bsd', y, w_out[layer])
    return rms_norm(h, final_norm_w)


import jax as _jax
import jax.numpy as _jnp

TWIN_FORMAT = 'train_step'
FWD_PARAMS = ['x', 'norm_w', 'w_in', 'pool_w', 'pool_scale', 'conv_w', 'a_log', 'dt_bias', 'dn_norm_w', 'w_out', 'final_norm_w']
TWIN_WEIGHTS = ['norm_w', 'w_in', 'pool_w', 'pool_scale', 'conv_w', 'a_log', 'dt_bias', 'dn_norm_w', 'w_out', 'final_norm_w']
TWIN_DIFF_INPUT = 'x'
TWIN_INPUTS = ['x', 'norm_w', 'w_in', 'pool_w', 'pool_scale', 'conv_w', 'a_log', 'dt_bias', 'dn_norm_w', 'w_out', 'final_norm_w', 'loss_target', 'm_norm_w', 'm_w_in', 'm_pool_w', 'm_pool_scale', 'm_conv_w', 'm_a_log', 'm_dt_bias', 'm_dn_norm_w', 'm_w_out', 'm_final_norm_w', 'v_norm_w', 'v_w_in', 'v_pool_w', 'v_pool_scale', 'v_conv_w', 'v_a_log', 'v_dt_bias', 'v_dn_norm_w', 'v_w_out', 'v_final_norm_w']
TWIN_OUTPUTS = ['loss', 'grad_x', 'grad_norm_w', 'grad_w_in', 'grad_pool_w', 'grad_pool_scale', 'grad_conv_w', 'grad_a_log', 'grad_dt_bias', 'grad_dn_norm_w', 'grad_w_out', 'grad_final_norm_w', 'delta_norm_w', 'delta_w_in', 'delta_pool_w', 'delta_pool_scale', 'delta_conv_w', 'delta_a_log', 'delta_dt_bias', 'delta_dn_norm_w', 'delta_w_out', 'delta_final_norm_w', 'new_m_norm_w', 'new_m_w_in', 'new_m_pool_w', 'new_m_pool_scale', 'new_m_conv_w', 'new_m_a_log', 'new_m_dt_bias', 'new_m_dn_norm_w', 'new_m_w_out', 'new_m_final_norm_w', 'new_v_norm_w', 'new_v_w_in', 'new_v_pool_w', 'new_v_pool_scale', 'new_v_conv_w', 'new_v_a_log', 'new_v_dt_bias', 'new_v_dn_norm_w', 'new_v_w_out', 'new_v_final_norm_w']
TWIN_LEAF_KINDS = {'loss': 'loss', 'grad_x': 'grad_x', 'grad_norm_w': 'grad_w', 'grad_w_in': 'grad_w', 'grad_pool_w': 'grad_w', 'grad_pool_scale': 'grad_w', 'grad_conv_w': 'grad_w', 'grad_a_log': 'grad_w', 'grad_dt_bias': 'grad_w', 'grad_dn_norm_w': 'grad_w', 'grad_w_out': 'grad_w', 'grad_final_norm_w': 'grad_w', 'delta_norm_w': 'delta_w', 'delta_w_in': 'delta_w', 'delta_pool_w': 'delta_w', 'delta_pool_scale': 'delta_w', 'delta_conv_w': 'delta_w', 'delta_a_log': 'delta_w', 'delta_dt_bias': 'delta_w', 'delta_dn_norm_w': 'delta_w', 'delta_w_out': 'delta_w', 'delta_final_norm_w': 'delta_w', 'new_m_norm_w': 'new_m', 'new_m_w_in': 'new_m', 'new_m_pool_w': 'new_m', 'new_m_pool_scale': 'new_m', 'new_m_conv_w': 'new_m', 'new_m_a_log': 'new_m', 'new_m_dt_bias': 'new_m', 'new_m_dn_norm_w': 'new_m', 'new_m_w_out': 'new_m', 'new_m_final_norm_w': 'new_m', 'new_v_norm_w': 'new_v', 'new_v_w_in': 'new_v', 'new_v_pool_w': 'new_v', 'new_v_pool_scale': 'new_v', 'new_v_conv_w': 'new_v', 'new_v_a_log': 'new_v', 'new_v_dt_bias': 'new_v', 'new_v_dn_norm_w': 'new_v', 'new_v_w_out': 'new_v', 'new_v_final_norm_w': 'new_v'}


def _forward(args):
    return _fwd_reference(*[args[k] for k in FWD_PARAMS])


def _output_shape():
    out = _jax.eval_shape(lambda: _forward(_fwd_setup_inputs(0)))
    return out.shape, out.dtype

N_MICROBATCH = 1
ADAM_LR = 0.001
ADAM_B1 = 0.9
ADAM_B2 = 0.999
ADAM_EPS = 1e-08
ADAM_WD = 0.01
ADAM_STEP = 10
PER_EXAMPLE_BATCH_AXIS = {'x': 0, 'loss_target': 0}
SHARED_INPUTS = []
_WEIGHT_DTYPES = {'norm_w': _jnp.float32, 'w_in': _jnp.float32, 'pool_w': _jnp.float32, 'pool_scale': _jnp.float32, 'conv_w': _jnp.float32, 'a_log': _jnp.float32, 'dt_bias': _jnp.float32, 'dn_norm_w': _jnp.float32, 'w_out': _jnp.float32, 'final_norm_w': _jnp.float32}
MOMENT_SCALE = {'norm_w': 1.407220e-01, 'w_in': 7.817475e-02, 'pool_w': 8.081874e-02, 'pool_scale': 8.257985e-02, 'conv_w': 7.131172e-02, 'a_log': 3.863155e-01, 'dt_bias': 3.757178e-01, 'dn_norm_w': 1.898766e-01, 'w_out': 8.636790e-02, 'final_norm_w': 3.200005e+01}


def _to_microbatches(a, axis):
    t = _jnp.moveaxis(a, axis, 0)
    t = t.reshape((N_MICROBATCH, t.shape[0] // N_MICROBATCH) + t.shape[1:])
    return _jnp.moveaxis(t, 1, axis + 1)


def setup_inputs(seed: int = 0) -> dict:
    inp = _fwd_setup_inputs(seed)
    key = _jax.random.fold_in(_jax.random.key(seed), 7919)
    shape, _ = _output_shape()
    out = dict(inp)
    out["loss_target"] = _jax.random.normal(_jax.random.fold_in(key, 0), shape, _jnp.float32)
    for i, name in enumerate(TWIN_WEIGHTS):
        w = inp[name].astype(_jnp.float32)
        if MOMENT_SCALE is None:
            s = _jnp.sqrt(_jnp.mean(_jnp.square(w)) + 1e-30)
        else:
            s = MOMENT_SCALE[name]
        km, kv = _jax.random.split(_jax.random.fold_in(key, i + 1))
        out[name] = w
        out["m_" + name] = s * _jax.random.normal(km, w.shape, _jnp.float32)
        out["v_" + name] = (s * s) * _jax.random.uniform(kv, w.shape, _jnp.float32, 0.5, 1.5)
    if N_MICROBATCH > 1:
        for name, axis in PER_EXAMPLE_BATCH_AXIS.items():
            out[name] = _to_microbatches(out[name], axis)
    return {'x': out['x'], 'norm_w': out['norm_w'], 'w_in': out['w_in'], 'pool_w': out['pool_w'], 'pool_scale': out['pool_scale'], 'conv_w': out['conv_w'], 'a_log': out['a_log'], 'dt_bias': out['dt_bias'], 'dn_norm_w': out['dn_norm_w'], 'w_out': out['w_out'], 'final_norm_w': out['final_norm_w'], 'loss_target': out['loss_target'], 'm_norm_w': out['m_norm_w'], 'm_w_in': out['m_w_in'], 'm_pool_w': out['m_pool_w'], 'm_pool_scale': out['m_pool_scale'], 'm_conv_w': out['m_conv_w'], 'm_a_log': out['m_a_log'], 'm_dt_bias': out['m_dt_bias'], 'm_dn_norm_w': out['m_dn_norm_w'], 'm_w_out': out['m_w_out'], 'm_final_norm_w': out['m_final_norm_w'], 'v_norm_w': out['v_norm_w'], 'v_w_in': out['v_w_in'], 'v_pool_w': out['v_pool_w'], 'v_pool_scale': out['v_pool_scale'], 'v_conv_w': out['v_conv_w'], 'v_a_log': out['v_a_log'], 'v_dt_bias': out['v_dt_bias'], 'v_dn_norm_w': out['v_dn_norm_w'], 'v_w_out': out['v_w_out'], 'v_final_norm_w': out['v_final_norm_w']}


def _loss(weights, diff, rest, loss_target):
    with _jax.named_scope("forward"):
        args = {**rest, TWIN_DIFF_INPUT: diff, **{k: w.astype(_WEIGHT_DTYPES[k]) for k, w in weights.items()}}
        y = _forward(args)
    with _jax.named_scope("loss_head"):
        err = _jnp.square(y.astype(_jnp.float32) - loss_target)
        return 0.5 * _jnp.sum(_jnp.mean(err, axis=-1)) if err.ndim else 0.5 * err


def _adamw(w, g, m, v):
    m = ADAM_B1 * m + (1.0 - ADAM_B1) * g
    v = ADAM_B2 * v + (1.0 - ADAM_B2) * _jnp.square(g)
    m_hat = m / (1.0 - ADAM_B1 ** ADAM_STEP)
    v_hat = v / (1.0 - ADAM_B2 ** ADAM_STEP)
    delta = -ADAM_LR * (m_hat / (_jnp.sqrt(v_hat) + ADAM_EPS) + ADAM_WD * w)
    return delta, m, v


def reference(x, norm_w, w_in, pool_w, pool_scale, conv_w, a_log, dt_bias, dn_norm_w, w_out, final_norm_w, loss_target, m_norm_w, m_w_in, m_pool_w, m_pool_scale, m_conv_w, m_a_log, m_dt_bias, m_dn_norm_w, m_w_out, m_final_norm_w, v_norm_w, v_w_in, v_pool_w, v_pool_scale, v_conv_w, v_a_log, v_dt_bias, v_dn_norm_w, v_w_out, v_final_norm_w):
    given = dict(x=x, norm_w=norm_w, w_in=w_in, pool_w=pool_w, pool_scale=pool_scale, conv_w=conv_w, a_log=a_log, dt_bias=dt_bias, dn_norm_w=dn_norm_w, w_out=w_out, final_norm_w=final_norm_w, loss_target=loss_target, m_norm_w=m_norm_w, m_w_in=m_w_in, m_pool_w=m_pool_w, m_pool_scale=m_pool_scale, m_conv_w=m_conv_w, m_a_log=m_a_log, m_dt_bias=m_dt_bias, m_dn_norm_w=m_dn_norm_w, m_w_out=m_w_out, m_final_norm_w=m_final_norm_w, v_norm_w=v_norm_w, v_w_in=v_w_in, v_pool_w=v_pool_w, v_pool_scale=v_pool_scale, v_conv_w=v_conv_w, v_a_log=v_a_log, v_dt_bias=v_dt_bias, v_dn_norm_w=v_dn_norm_w, v_w_out=v_w_out, v_final_norm_w=v_final_norm_w)
    weights = {n: given[n] for n in TWIN_WEIGHTS}
    shared = {n: given[n] for n in SHARED_INPUTS}
    per_example = {n: given[n] for n in ['x']}
    grad_fn = _jax.value_and_grad(_loss, argnums=(0, 1))

    def one_microbatch(ex, loss_target):
        ex = dict(ex)
        diff = ex.pop(TWIN_DIFF_INPUT)
        return grad_fn(weights, diff, {**shared, **ex}, loss_target)

    if N_MICROBATCH == 1:
        loss, (grad_w, grad_x) = one_microbatch(per_example, given["loss_target"])
    else:
        def body(carry, xs):
            loss_sum, grad_sum = carry
            l_k, (gw_k, gx_k) = one_microbatch(xs[0], xs[1])
            with _jax.named_scope("update"):
                return (loss_sum + l_k, _jax.tree.map(_jnp.add, grad_sum, gw_k)), gx_k

        init = (_jnp.zeros((), _jnp.float32), _jax.tree.map(_jnp.zeros_like, weights))
        (loss, grad_w), grad_x = _jax.lax.scan(body, init, (per_example, given["loss_target"]))
    with _jax.named_scope("update"):
        delta_w, new_m, new_v = {}, {}, {}
        for n in TWIN_WEIGHTS:
            delta_w[n], new_m[n], new_v[n] = _adamw(weights[n], grad_w[n], given["m_" + n], given["v_" + n])
    return (loss, grad_x, *[grad_w[n] for n in TWIN_WEIGHTS], *[delta_w[n] for n in TWIN_WEIGHTS],
            *[new_m[n] for n in TWIN_WEIGHTS], *[new_v[n] for n in TWIN_WEIGHTS])
```

```python
import functools

import jax
import jax.numpy as jnp
from jax import lax
from jax.experimental import pallas as pl
from jax.experimental.pallas import tpu as pltpu

F32 = jnp.float32
BF16 = jnp.bfloat16
HI = lax.Precision.HIGHEST
MESH = pl.DeviceIdType.MESH

D_MODEL = 1024
D_POOL = 512
D_DN = 512
POOL_WINDOWS = (2, 4, 8, 16)
POOL_GROUP = 128
DN_HEADS = 4
DN_HEAD_DIM = 128
CONV_WIDTH = 4
CHUNK = 64
NORM_EPS = 1e-6
D_IN = 3080
D_MAIN = 3072
N_DEV = 8
W_IN_SHARD = D_IN // N_DEV
HALO = 16

ADAM_LR = 0.001
ADAM_B1 = 0.9
ADAM_B2 = 0.999
ADAM_EPS = 1e-08
ADAM_WD = 0.01
ADAM_STEP = 10

VMEM_LIMIT = 56 * 1024 * 1024


def _cp(sem=None, vmem=VMEM_LIMIT):
    kw = {"vmem_limit_bytes": vmem}
    if sem is not None:
        kw["dimension_semantics"] = sem
    return pltpu.CompilerParams(**kw)


def _dot_hi(a, b):
    return jnp.dot(a, b, precision=HI, preferred_element_type=F32)


def _dot_bf(a, b):
    return jnp.dot(a.astype(BF16), b.astype(BF16), preferred_element_type=F32)


def _dot_nt_bf(a, b):
    return lax.dot_general(a.astype(BF16), b.astype(BF16), (((1,), (1,)), ((), ())), preferred_element_type=F32)


def _dot_tn_bf(a, b):
    return lax.dot_general(a.astype(BF16), b.astype(BF16), (((0,), (0,)), ((), ())), preferred_element_type=F32)


def _dot_tn_hi(a, b):
    return lax.dot_general(a, b, (((0,), (0,)), ((), ())), precision=HI, preferred_element_type=F32)


def _silu(x):
    return x * jax.nn.sigmoid(x)


def _softplus(x):
    pos = x > 0.0
    return jnp.where(pos, x, 0.0) + jnp.log1p(jnp.exp(jnp.where(pos, -x, x)))


def _mesh_pos():
    return lax.axis_index("x"), lax.axis_index("y"), lax.axis_index("c")


def _dev_index(x, y, c):
    return 4 * x + 2 * y + c


def _all_gather_blocks(outs, send_sems, recv_sems):
    x, y, c = _mesh_pos()
    me = (x, y, c)
    sibling = (x, y, 1 - c)
    chips = [(1 - x, y), (x, 1 - y), (1 - x, 1 - y)]

    def copy(a, k, block, to):
        rows = outs[a].at[_dev_index(*block)]
        return pltpu.make_async_remote_copy(src_ref=rows, dst_ref=rows, send_sem=send_sems.at[a, k],
                                            recv_sem=recv_sems.at[a, k], device_id=to, device_id_type=MESH)

    n = len(outs)
    first = []
    for a in range(n):
        first.append(copy(a, 0, me, sibling))
        for j, chip in enumerate(chips):
            first.append(copy(a, 1 + j, me, (*chip, c)))
    for cp in first:
        cp.start()
    passed = []
    for j, chip in enumerate(chips):
        for a in range(n):
            copy(a, 1 + j, (*chip, c), me).wait_recv()
            fwd = copy(a, 4 + j, (*chip, c), sibling)
            fwd.start()
            passed.append(fwd)
    for a in range(n):
        copy(a, 0, sibling, me).wait_recv()
        for j, chip in enumerate(chips):
            copy(a, 4 + j, (*chip, 1 - c), me).wait_recv()
    for cp in first + passed:
        cp.wait_send()


def _gather_weights(w_in_blk, w_out_blk, conv_blk):
    def body(win_ref, wout_ref, conv_ref, gin_ref, gout_ref, gconv_ref, send_sems, recv_sems):
        x, y, c = _mesh_pos()
        me = _dev_index(x, y, c)
        gin_ref[me] = win_ref[...].astype(BF16)
        gout_ref[me] = wout_ref[...].astype(BF16)
        gconv_ref[me] = conv_ref[...]
        _all_gather_blocks((gin_ref, gout_ref, gconv_ref), send_sems, recv_sems)

    vm = pl.BlockSpec(memory_space=pltpu.VMEM)
    return pl.pallas_call(
        body, name="gather_weights",
        out_shape=(jax.ShapeDtypeStruct((N_DEV,) + w_in_blk.shape, BF16),
                   jax.ShapeDtypeStruct((N_DEV,) + w_out_blk.shape, BF16),
                   jax.ShapeDtypeStruct((N_DEV,) + conv_blk.shape, F32)),
        in_specs=[vm, vm, vm], out_specs=(vm, vm, vm),
        scratch_shapes=[pltpu.SemaphoreType.DMA((3, 7)), pltpu.SemaphoreType.DMA((3, 7))],
        compiler_params=_cp(),
    )(w_in_blk, w_out_blk, conv_blk)


def _reduce_grads(p_in, p_out, p_conv, pack_a, pack_b):
    big = (p_in, p_out, p_conv)

    def body(pin_ref, pout_ref, pconv_ref, pa_ref, pb_ref,
             oin_ref, oout_ref, oconv_ref, ga_ref, gb_ref,
             r1_in, r1_out, r1_conv, r2_in, r2_out, r2_conv, st_in, st_out, st_conv,
             s1_send, s1_recv, s2_send, s2_recv, ag_send, ag_recv, st_sem):
        x, y, c = _mesh_pos()
        me = (x, y, c)
        sibling = (x, y, 1 - c)
        rel = [(x, y), (1 - x, y), (x, 1 - y), (1 - x, 1 - y)]
        srcs = (pin_ref, pout_ref, pconv_ref)
        r1s = (r1_in, r1_out, r1_conv)
        r2s = (r2_in, r2_out, r2_conv)
        sts = (st_in, st_out, st_conv)
        outs = (oin_ref, oout_ref, oconv_ref)

        ga_ref[_dev_index(*me)] = pa_ref[...]
        gb_ref[_dev_index(*me)] = pb_ref[...]

        def p1(a, r, to):
            return pltpu.make_async_remote_copy(
                src_ref=srcs[a].at[_dev_index(*rel[r], 1 - c)], dst_ref=r1s[a].at[r],
                send_sem=s1_send.at[a, r], recv_sem=s1_recv.at[a, r], device_id=to, device_id_type=MESH)

        def p2(a, r, to):
            return pltpu.make_async_remote_copy(
                src_ref=r1s[a].at[r], dst_ref=r2s[a].at[r - 1],
                send_sem=s2_send.at[a, r - 1], recv_sem=s2_recv.at[a, r - 1], device_id=to, device_id_type=MESH)

        def stage(a, r):
            return pltpu.make_async_copy(srcs[a].at[_dev_index(*rel[r], c)], sts[a].at[r % 2], st_sem.at[a, r % 2])

        sends1 = [p1(a, r, sibling) for a in range(3) for r in range(4)]
        for cp in sends1:
            cp.start()
        sends2 = []
        for a in range(3):
            stage(a, 1).start()
            for r in (1, 2, 3, 0):
                nxt = {1: 2, 2: 3, 3: 0, 0: None}[r]
                if nxt is not None:
                    stage(a, nxt).start()
                stage(a, r).wait()
                p1(a, r, me).wait_recv()
                r1s[a][r] = r1s[a][r] + sts[a][r % 2]
                if r != 0:
                    cp = p2(a, r, (*rel[r], c))
                    cp.start()
                    sends2.append(cp)
        _all_gather_blocks((ga_ref, gb_ref), ag_send, ag_recv)
        for a in range(3):
            for r in (1, 2, 3):
                p2(a, r, me).wait_recv()
            outs[a][...] = ((r1s[a][0] + r2s[a][0]) + r2s[a][1]) + r2s[a][2]
        for cp in sends1 + sends2:
            cp.wait_send()

    vm = pl.BlockSpec(memory_space=pltpu.VMEM)
    hbm = pl.BlockSpec(memory_space=pl.ANY)
    blk = [p.shape[1:] for p in big]
    scratch = ([pltpu.VMEM((4,) + b, F32) for b in blk] + [pltpu.VMEM((3,) + b, F32) for b in blk]
               + [pltpu.VMEM((2,) + b, F32) for b in blk]
               + [pltpu.SemaphoreType.DMA((3, 4)), pltpu.SemaphoreType.DMA((3, 4)),
                  pltpu.SemaphoreType.DMA((3, 3)), pltpu.SemaphoreType.DMA((3, 3)),
                  pltpu.SemaphoreType.DMA((2, 7)), pltpu.SemaphoreType.DMA((2, 7)),
                  pltpu.SemaphoreType.DMA((3, 2))])
    return pl.pallas_call(
        body, name="reduce_grads",
        out_shape=tuple(jax.ShapeDtypeStruct(b, F32) for b in blk)
        + (jax.ShapeDtypeStruct((N_DEV,) + pack_a.shape, F32), jax.ShapeDtypeStruct((N_DEV,) + pack_b.shape, F32)),
        in_specs=[hbm, hbm, hbm, vm, vm], out_specs=(vm, vm, vm, vm, vm),
        scratch_shapes=scratch,
        compiler_params=_cp(),
    )(p_in, p_out, p_conv, pack_a, pack_b)


def _rms_hat(xf):
    r = lax.rsqrt(jnp.mean(xf * xf, axis=-1, keepdims=True) + NORM_EPS)
    return xf * r, r


def _in_proj(x2, norm_w, w_main, w_ba, tm):
    s = x2.shape[0]

    def body(x_ref, nw_ref, wm_hbm, wb_ref, pm_ref, pb_ref, wm_vmem, sem):
        @pl.when(pl.program_id(0) == 0)
        def _():
            cp = pltpu.make_async_copy(wm_hbm, wm_vmem, sem)
            cp.start()
            cp.wait()
        xhat, _ = _rms_hat(x_ref[...])
        n = (xhat * nw_ref[...]).astype(BF16)
        pm_ref[...] = jnp.dot(n, wm_vmem[...], preferred_element_type=F32)
        pb_ref[...] = jnp.dot(n, wb_ref[...], preferred_element_type=F32)

    return pl.pallas_call(
        body, name="in_proj", grid=(s // tm,),
        out_shape=(jax.ShapeDtypeStruct((s, D_MAIN), F32), jax.ShapeDtypeStruct((s, 128), F32)),
        in_specs=[pl.BlockSpec((tm, D_MODEL), lambda i: (i, 0)),
                  pl.BlockSpec((1, D_MODEL), lambda i: (0, 0)),
                  pl.BlockSpec(memory_space=pl.ANY),
                  pl.BlockSpec((D_MODEL, 128), lambda i: (0, 0))],
        out_specs=(pl.BlockSpec((tm, D_MAIN), lambda i: (i, 0)), pl.BlockSpec((tm, 128), lambda i: (i, 0))),
        scratch_shapes=[pltpu.VMEM((D_MODEL, D_MAIN), BF16), pltpu.SemaphoreType.DMA],
        compiler_params=_cp(("arbitrary",)),
    )(x2, norm_w, w_main, w_ba)


def _shift_down(cur, prev_tail, s):
    ext = jnp.concatenate([prev_tail, cur], axis=0)
    return pltpu.roll(ext, s, 0)[HALO:, :]


def _shift_up(cur, next_head, s):
    ext = jnp.concatenate([cur, next_head], axis=0)
    n = ext.shape[0]
    return pltpu.roll(ext, n - s, 0)[:cur.shape[0], :]


def _pool_counts(i, tp, w):
    t = i * tp + lax.broadcasted_iota(jnp.int32, (tp, 1), 0)
    return jnp.minimum(t + 1, w).astype(F32)


def _pool_mix(u, u_prev_tail, i, tp):
    ext = jnp.concatenate([u_prev_tail, u], axis=0)
    w2 = ext + pltpu.roll(ext, 1, 0)
    w4 = w2 + pltpu.roll(w2, 2, 0)
    w8 = w4 + pltpu.roll(w4, 4, 0)
    w16 = w8 + pltpu.roll(w8, 8, 0)
    mixes = []
    for gi, (w, win) in enumerate(zip(POOL_WINDOWS, (w2, w4, w8, w16))):
        cols = slice(gi * POOL_GROUP, (gi + 1) * POOL_GROUP)
        mixes.append(win[HALO:, cols] / _pool_counts(i, tp, w) - u[:, cols])
    return mixes


def _prev_tail(ref, i):
    tail = ref[ref.shape[0] - HALO:, :]
    return jnp.where(i > 0, tail, 0.0)


def _next_head(ref, i, n):
    head = ref[:HALO, :]
    return jnp.where(i < n - 1, head, 0.0)


def _pool_fwd(proj_main, pool_w, pool_scale, tp):
    s = proj_main.shape[0]

    def body(u_ref, up_ref, z_ref, pw_ref, ps_ref, y_ref):
        i = pl.program_id(0)
        u = u_ref[...]
        mixes = _pool_mix(u, _prev_tail(up_ref, i), i, tp)
        gate = ps_ref[...] * _silu(z_ref[...])
        for gi in range(4):
            cols = slice(gi * POOL_GROUP, (gi + 1) * POOL_GROUP)
            y_ref[:, cols] = _dot_hi(mixes[gi], pw_ref[gi]) * gate[:, cols]

    return pl.pallas_call(
        body, name="pool_fwd", grid=(s // tp,),
        out_shape=jax.ShapeDtypeStruct((s, D_POOL), F32),
        in_specs=[pl.BlockSpec((tp, D_POOL), lambda i: (i, 0)),
                  pl.BlockSpec((tp, D_POOL), lambda i: (jnp.maximum(i - 1, 0), 0)),
                  pl.BlockSpec((tp, D_POOL), lambda i: (i, 1)),
                  pl.BlockSpec((4, POOL_GROUP, POOL_GROUP), lambda i: (0, 0, 0)),
                  pl.BlockSpec((1, D_POOL), lambda i: (0, 0))],
        out_specs=pl.BlockSpec((tp, D_POOL), lambda i: (i, 0)),
        compiler_params=_cp(("parallel",)),
    )(proj_main, proj_main, proj_main, pool_w, pool_scale)


def _conv_fwd(cur, prev_tail, w4):
    ext = jnp.concatenate([prev_tail, cur], axis=0)
    y = ext * w4[CONV_WIDTH - 1:CONV_WIDTH, :]
    for sft in range(1, CONV_WIDTH):
        y = y + pltpu.roll(ext, sft, 0) * w4[CONV_WIDTH - 1 - sft:CONV_WIDTH - sft, :]
    return y[HALO:, :]


def _l2n_heads(t):
    parts = []
    for h in range(DN_HEADS):
        th = t[:, h * DN_HEAD_DIM:(h + 1) * DN_HEAD_DIM]
        parts.append(th * lax.rsqrt(jnp.sum(th * th, axis=-1, keepdims=True) + NORM_EPS))
    return jnp.concatenate(parts, axis=1)


def _post_conv(yq, yk, yv):
    return _l2n_heads(_silu(yq)), _l2n_heads(_silu(yk)), _silu(yv)


def _gates(ba, alog_lane, dtb_lane):
    lane = lax.broadcasted_iota(jnp.int32, ba.shape, 1)
    beta = jax.nn.sigmoid(ba)
    g = -jnp.exp(alog_lane) * _softplus(ba + dtb_lane)
    return jnp.where(lane < DN_HEADS, beta, jnp.where(lane < 2 * DN_HEADS, g, 0.0))


def _qkv_specs(tp, which):
    def spec(col, n=None):
        if which == 0:
            return pl.BlockSpec((tp, D_DN), lambda i: (i, col))
        if which < 0:
            return pl.BlockSpec((tp, D_DN), lambda i: (jnp.maximum(i - 1, 0), col))
        return pl.BlockSpec((tp, D_DN), lambda i: (jnp.minimum(i + 1, n - 1), col))
    return spec


def _dn_pre(proj_main, proj_ba, conv_full, alog_lane, dtb_lane, tp):
    s = proj_main.shape[0]

    def body(q_ref, k_ref, v_ref, qp_ref, kp_ref, vp_ref, cw_ref, ba_ref, al_ref, db_ref,
             qn_ref, kn_ref, vv_ref, gb_ref):
        i = pl.program_id(0)
        ys = []
        for j, (cur, prev) in enumerate(((q_ref, qp_ref), (k_ref, kp_ref), (v_ref, vp_ref))):
            ys.append(_conv_fwd(cur[...], _prev_tail(prev, i), cw_ref[:, j * D_DN:(j + 1) * D_DN]))
        qn, kn, vv = _post_conv(*ys)
        qn_ref[...] = qn
        kn_ref[...] = kn
        vv_ref[...] = vv
        gb_ref[...] = _gates(ba_ref[...], al_ref[...], db_ref[...])

    cur, prev = _qkv_specs(tp, 0), _qkv_specs(tp, -1)
    row = pl.BlockSpec((1, 128), lambda i: (0, 0))
    tile = pl.BlockSpec((tp, D_DN), lambda i: (i, 0))
    return pl.pallas_call(
        body, name="dn_pre", grid=(s // tp,),
        out_shape=(jax.ShapeDtypeStruct((s, D_DN), F32),) * 3 + (jax.ShapeDtypeStruct((s, 128), F32),),
        in_specs=[cur(2), cur(3), cur(4), prev(2), prev(3), prev(4),
                  pl.BlockSpec((CONV_WIDTH, 3 * D_DN), lambda i: (0, 0)),
                  pl.BlockSpec((tp, 128), lambda i: (i, 0)), row, row],
        out_specs=(tile, tile, tile, pl.BlockSpec((tp, 128), lambda i: (i, 0))),
        compiler_params=_cp(("parallel",)),
    )(proj_main, proj_main, proj_main, proj_main, proj_main, proj_main, conv_full, proj_ba, alog_lane, dtb_lane)


def _dn_chunk(q, k, v, gcol, bcol, state, dz, nw):
    n = q.shape[0]
    ii = lax.broadcasted_iota(jnp.int32, (n, n), 0)
    jj = lax.broadcasted_iota(jnp.int32, (n, n), 1)
    lower = ii >= jj
    eye = (ii == jj).astype(F32)
    g_row = jnp.sum(eye * gcol, axis=0, keepdims=True)
    gc_col = jnp.sum(jnp.where(lower, g_row, 0.0), axis=1, keepdims=True)
    gc_row = jnp.sum(eye * gc_col, axis=0, keepdims=True)
    decay = jnp.where(lower, jnp.exp(jnp.where(lower, gc_col - gc_row, 0.0)), 0.0)
    kb = k * bcol
    vb = v * bcol
    a = jnp.where(ii > jj, _dot_hi(kb, k.T) * decay, 0.0)
    t = eye - a
    apow = a
    for _ in range(5):
        apow = _dot_hi(apow, apow)
        t = t + _dot_hi(t, apow)
    egc = jnp.exp(gc_col)
    u = _dot_hi(t, vb)
    w = _dot_hi(t, kb * egc)
    qs = q * (DN_HEAD_DIM ** -0.5)
    qk = _dot_hi(qs, k.T) * decay
    q_dec = qs * egc
    g_last = gc_col[n - 1:n, :]
    k_dec = k * jnp.exp(g_last - gc_col)
    v_new = u - _dot_hi(w, state)
    o = _dot_hi(q_dec, state) + _dot_hi(qk, v_new)
    new_state = state * jnp.exp(g_last) + _dot_hi(k_dec.T, v_new)
    y = o * lax.rsqrt(jnp.mean(o * o, axis=-1, keepdims=True) + NORM_EPS) * nw * _silu(dz)
    return y, new_state


def _dn_scan_fwd(qn, kn, vv, gb, proj_main, dn_norm_w, gc):
    s = qn.shape[0]
    nchunk = s // CHUNK
    rows = gc * CHUNK

    def body(q_ref, k_ref, v_ref, gb_ref, dz_ref, nw_ref, y_ref, ss_ref, state):
        @pl.when(pl.program_id(0) == 0)
        def _():
            state[...] = jnp.zeros_like(state)
        nw = nw_ref[...]
        for cc in range(gc):
            r = slice(cc * CHUNK, (cc + 1) * CHUNK)
            gbv = gb_ref[r, :]
            for h in range(DN_HEADS):
                cols = slice(h * DN_HEAD_DIM, (h + 1) * DN_HEAD_DIM)
                st = state[h]
                ss_ref[cc, h] = st
                y, new = _dn_chunk(q_ref[r, cols], k_ref[r, cols], v_ref[r, cols],
                                   gbv[:, DN_HEADS + h:DN_HEADS + h + 1], gbv[:, h:h + 1], st, dz_ref[r, cols], nw)
                y_ref[r, cols] = y
                state[h] = new

    tile = pl.BlockSpec((rows, D_DN), lambda i: (i, 0))
    return pl.pallas_call(
        body, name="dn_scan_fwd", grid=(nchunk // gc,),
        out_shape=(jax.ShapeDtypeStruct((s, D_DN), F32),
                   jax.ShapeDtypeStruct((nchunk, DN_HEADS, DN_HEAD_DIM, DN_HEAD_DIM), F32)),
        in_specs=[tile, tile, tile, pl.BlockSpec((rows, 128), lambda i: (i, 0)),
                  pl.BlockSpec((rows, D_DN), lambda i: (i, 5)), pl.BlockSpec((1, 128), lambda i: (0, 0))],
        out_specs=(tile, pl.BlockSpec((gc, DN_HEADS, DN_HEAD_DIM, DN_HEAD_DIM), lambda i: (i, 0, 0, 0))),
        scratch_shapes=[pltpu.VMEM((DN_HEADS, DN_HEAD_DIM, DN_HEAD_DIM), F32)],
        compiler_params=_cp(("arbitrary",)),
    )(qn, kn, vv, gb, proj_main, dn_norm_w)


def _out_proj_loss(y_pool, y_dn, x2, tgt, w_out_full, fnw, tm):
    s = x2.shape[0]

    def body(yp_ref, yd_ref, x_ref, t_ref, wo_ref, fw_ref,
             dh_ref, dyp_ref, dyd_ref, gwo_ref, gfw_ref, loss_ref):
        @pl.when(pl.program_id(0) == 0)
        def _():
            gwo_ref[...] = jnp.zeros_like(gwo_ref)
            gfw_ref[...] = jnp.zeros_like(gfw_ref)
            loss_ref[...] = jnp.zeros_like(loss_ref)
        y = jnp.concatenate([yp_ref[...], yd_ref[...]], axis=1).astype(BF16)
        wo = wo_ref[...]
        h = x_ref[...] + jnp.dot(y, wo, preferred_element_type=F32)
        hn, r = _rms_hat(h)
        fw = fw_ref[...]
        err = hn * fw - t_ref[...]
        loss_ref[...] += 0.5 * jnp.sum(jnp.sum(err * err, axis=-1, keepdims=True) / D_MODEL, axis=0, keepdims=True)
        dout = err / D_MODEL
        gfw_ref[...] += jnp.sum(dout * hn, axis=0, keepdims=True)
        dhn = dout * fw
        dh = r * (dhn - hn * jnp.mean(dhn * hn, axis=-1, keepdims=True))
        dh_ref[...] = dh
        dhb = dh.astype(BF16)
        dy = _dot_nt_bf(dhb, wo)
        dyp_ref[...] = dy[:, :D_POOL]
        dyd_ref[...] = dy[:, D_POOL:]
        gwo_ref[...] += _dot_tn_bf(y, dhb)

    half = pl.BlockSpec((tm, D_POOL), lambda i: (i, 0))
    full = pl.BlockSpec((tm, D_MODEL), lambda i: (i, 0))
    return pl.pallas_call(
        body, name="out_proj_loss", grid=(s // tm,),
        out_shape=(jax.ShapeDtypeStruct((s, D_MODEL), F32), jax.ShapeDtypeStruct((s, D_POOL), F32),
                   jax.ShapeDtypeStruct((s, D_DN), F32), jax.ShapeDtypeStruct((D_MODEL, D_MODEL), F32),
                   jax.ShapeDtypeStruct((1, D_MODEL), F32), jax.ShapeDtypeStruct((1, 128), F32)),
        in_specs=[half, half, full, full, pl.BlockSpec((D_MODEL, D_MODEL), lambda i: (0, 0)),
                  pl.BlockSpec((1, D_MODEL), lambda i: (0, 0))],
        out_specs=(full, half, half, pl.BlockSpec((D_MODEL, D_MODEL), lambda i: (0, 0)),
                   pl.BlockSpec((1, D_MODEL), lambda i: (0, 0)), pl.BlockSpec((1, 128), lambda i: (0, 0))),
        compiler_params=_cp(("arbitrary",)),
    )(y_pool, y_dn, x2, tgt, w_out_full, fnw)


def _dn_scan_bwd(qn, kn, vv, gb, proj_main, dn_norm_w, states, dy_dn, gc):
    s = qn.shape[0]
    nchunk = s // CHUNK
    nstep = nchunk // gc
    rows = gc * CHUNK

    def body(q_ref, k_ref, v_ref, gb_ref, dz_ref, nw_ref, ss_ref, dy_ref,
             dq_ref, dk_ref, dv_ref, dgb_ref, ddz_ref, dnw_ref, dstate):
        @pl.when(pl.program_id(0) == 0)
        def _():
            dstate[...] = jnp.zeros_like(dstate)
            dnw_ref[...] = jnp.zeros_like(dnw_ref)
        nw = nw_ref[...]
        lane = lax.broadcasted_iota(jnp.int32, (CHUNK, 128), 1)
        for cc in reversed(range(gc)):
            r = slice(cc * CHUNK, (cc + 1) * CHUNK)
            gbv = gb_ref[r, :]
            dgb = jnp.zeros((CHUNK, 128), F32)
            dnw = jnp.zeros((1, DN_HEAD_DIM), F32)
            for h in range(DN_HEADS):
                cols = slice(h * DN_HEAD_DIM, (h + 1) * DN_HEAD_DIM)
                _, vjp = jax.vjp(_dn_chunk, q_ref[r, cols], k_ref[r, cols], v_ref[r, cols],
                                 gbv[:, DN_HEADS + h:DN_HEADS + h + 1], gbv[:, h:h + 1], ss_ref[cc, h],
                                 dz_ref[r, cols], nw)
                dq, dk, dv, dg, db, ds, ddz, dnw_h = vjp((dy_ref[r, cols], dstate[h]))
                dq_ref[r, cols] = dq
                dk_ref[r, cols] = dk
                dv_ref[r, cols] = dv
                ddz_ref[r, cols] = ddz
                dstate[h] = ds
                dgb = dgb + jnp.where(lane == h, db, 0.0) + jnp.where(lane == DN_HEADS + h, dg, 0.0)
                dnw = dnw + dnw_h
            dgb_ref[r, :] = dgb
            dnw_ref[...] += dnw

    rev = lambda i: (nstep - 1 - i, 0)
    tile = pl.BlockSpec((rows, D_DN), rev)
    lanes = pl.BlockSpec((rows, 128), rev)
    return pl.pallas_call(
        body, name="dn_scan_bwd", grid=(nstep,),
        out_shape=(jax.ShapeDtypeStruct((s, D_DN), F32),) * 3
        + (jax.ShapeDtypeStruct((s, 128), F32), jax.ShapeDtypeStruct((s, D_DN), F32),
           jax.ShapeDtypeStruct((1, 128), F32)),
        in_specs=[tile, tile, tile, lanes, pl.BlockSpec((rows, D_DN), lambda i: (nstep - 1 - i, 5)),
                  pl.BlockSpec((1, 128), lambda i: (0, 0)),
                  pl.BlockSpec((gc, DN_HEADS, DN_HEAD_DIM, DN_HEAD_DIM), lambda i: (nstep - 1 - i, 0, 0, 0)), tile],
        out_specs=(tile, tile, tile, lanes, tile, pl.BlockSpec((1, 128), lambda i: (0, 0))),
        scratch_shapes=[pltpu.VMEM((DN_HEADS, DN_HEAD_DIM, DN_HEAD_DIM), F32)],
        compiler_params=_cp(("arbitrary",)),
    )(qn, kn, vv, gb, proj_main, dn_norm_w, states, dy_dn)


def _dn_pre_bwd1(proj_main, proj_ba, conv_full, alog_lane, dtb_lane, dqn, dkn, dvv, dgb, tp):
    s = proj_main.shape[0]

    def body(q_ref, k_ref, v_ref, qp_ref, kp_ref, vp_ref, cw_ref, ba_ref, al_ref, db_ref,
             dqn_ref, dkn_ref, dvv_ref, dgb_ref,
             dcq_ref, dck_ref, dcv_ref, dba_ref, dcw_ref, dal_ref, ddb_ref):
        i = pl.program_id(0)

        @pl.when(i == 0)
        def _():
            dcw_ref[...] = jnp.zeros_like(dcw_ref)
            dal_ref[...] = jnp.zeros_like(dal_ref)
            ddb_ref[...] = jnp.zeros_like(ddb_ref)
        curs = (q_ref[...], k_ref[...], v_ref[...])
        tails = (_prev_tail(qp_ref, i), _prev_tail(kp_ref, i), _prev_tail(vp_ref, i))
        ys = [_conv_fwd(curs[j], tails[j], cw_ref[:, j * D_DN:(j + 1) * D_DN]) for j in range(3)]
        _, vjp = jax.vjp(_post_conv, *ys)
        dys = vjp((dqn_ref[...], dkn_ref[...], dvv_ref[...]))
        for j, (dy, out) in enumerate(zip(dys, (dcq_ref, dck_ref, dcv_ref))):
            out[...] = dy
            for sft in range(CONV_WIDTH):
                xs = curs[j] if sft == 0 else _shift_down(curs[j], tails[j], sft)
                row = CONV_WIDTH - 1 - sft
                dcw_ref[row:row + 1, j * D_DN:(j + 1) * D_DN] += jnp.sum(dy * xs, axis=0, keepdims=True)
        _, gvjp = jax.vjp(_gates, ba_ref[...], al_ref[...], db_ref[...])
        dba, dal, ddb = gvjp(dgb_ref[...])
        dba_ref[...] = dba
        dal_ref[...] += dal
        ddb_ref[...] += ddb

    cur, prev = _qkv_specs(tp, 0), _qkv_specs(tp, -1)
    row = pl.BlockSpec((1, 128), lambda i: (0, 0))
    tile = pl.BlockSpec((tp, D_DN), lambda i: (i, 0))
    lanes = pl.BlockSpec((tp, 128), lambda i: (i, 0))
    cw = pl.BlockSpec((CONV_WIDTH, 3 * D_DN), lambda i: (0, 0))
    return pl.pallas_call(
        body, name="dn_pre_bwd1", grid=(s // tp,),
        out_shape=(jax.ShapeDtypeStruct((s, D_DN), F32),) * 3
        + (jax.ShapeDtypeStruct((s, 128), F32), jax.ShapeDtypeStruct((CONV_WIDTH, 3 * D_DN), F32),
           jax.ShapeDtypeStruct((1, 128), F32), jax.ShapeDtypeStruct((1, 128), F32)),
        in_specs=[cur(2), cur(3), cur(4), prev(2), prev(3), prev(4), cw, lanes, row, row, tile, tile, tile, lanes],
        out_specs=(tile, tile, tile, lanes, cw, row, row),
        compiler_params=_cp(("arbitrary",)),
    )(proj_main, proj_main, proj_main, proj_main, proj_main, proj_main, conv_full, proj_ba, alog_lane, dtb_lane,
      dqn, dkn, dvv, dgb)


def _conv_bwd_x(dcq, dck, dcv, conv_full, tp):
    s = dcq.shape[0]
    n = s // tp

    def body(q_ref, k_ref, v_ref, qn_ref, kn_ref, vn_ref, cw_ref, dq_ref, dk_ref, dv_ref):
        i = pl.program_id(0)
        for j, (cur, nxt, out) in enumerate(((q_ref, qn_ref, dq_ref), (k_ref, kn_ref, dk_ref), (v_ref, vn_ref, dv_ref))):
            w4 = cw_ref[:, j * D_DN:(j + 1) * D_DN]
            dy = cur[...]
            head = _next_head(nxt, i, n)
            dx = dy * w4[CONV_WIDTH - 1:CONV_WIDTH, :]
            for sft in range(1, CONV_WIDTH):
                dx = dx + _shift_up(dy, head, sft) * w4[CONV_WIDTH - 1 - sft:CONV_WIDTH - sft, :]
            out[...] = dx

    tile = pl.BlockSpec((tp, D_DN), lambda i: (i, 0))
    nxt = pl.BlockSpec((tp, D_DN), lambda i: (jnp.minimum(i + 1, n - 1), 0))
    return pl.pallas_call(
        body, name="conv_bwd_x", grid=(n,),
        out_shape=(jax.ShapeDtypeStruct((s, D_DN), F32),) * 3,
        in_specs=[tile, tile, tile, nxt, nxt, nxt, pl.BlockSpec((CONV_WIDTH, 3 * D_DN), lambda i: (0, 0))],
        out_specs=(tile, tile, tile),
        compiler_params=_cp(("parallel",)),
    )(dcq, dck, dcv, dcq, dck, dcv, conv_full)


def _pool_bwd1(proj_main, pool_w, pool_scale, dyp, tp):
    s = proj_main.shape[0]

    def body(u_ref, up_ref, z_ref, pw_ref, ps_ref, dy_ref, dz_ref, dwin_ref, dpw_ref, dps_ref):
        i = pl.program_id(0)

        @pl.when(i == 0)
        def _():
            dpw_ref[...] = jnp.zeros_like(dpw_ref)
            dps_ref[...] = jnp.zeros_like(dps_ref)
        u = u_ref[...]
        z = z_ref[...]
        dy = dy_ref[...]
        ps = ps_ref[...]
        mixes = _pool_mix(u, _prev_tail(up_ref, i), i, tp)
        sg = jax.nn.sigmoid(z)
        sz = z * sg
        dsz = sg * (1.0 + z * (1.0 - sg))
        for gi, w in enumerate(POOL_WINDOWS):
            cols = slice(gi * POOL_GROUP, (gi + 1) * POOL_GROUP)
            mixw = _dot_hi(mixes[gi], pw_ref[gi])
            dmixw = dy[:, cols] * ps[:, cols] * sz[:, cols]
            dps_ref[:, cols] += jnp.sum(dy[:, cols] * mixw * sz[:, cols], axis=0, keepdims=True)
            dz_ref[:, cols] = dy[:, cols] * mixw * ps[:, cols] * dsz[:, cols]
            dpw_ref[gi] += _dot_tn_hi(mixes[gi], dmixw)
            dmix = lax.dot_general(dmixw, pw_ref[gi], (((1,), (1,)), ((), ())), precision=HI,
                                   preferred_element_type=F32)
            dwin_ref[:, cols] = dmix / _pool_counts(i, tp, w)

    tile = pl.BlockSpec((tp, D_POOL), lambda i: (i, 0))
    pw = pl.BlockSpec((4, POOL_GROUP, POOL_GROUP), lambda i: (0, 0, 0))
    ps = pl.BlockSpec((1, D_POOL), lambda i: (0, 0))
    return pl.pallas_call(
        body, name="pool_bwd1", grid=(s // tp,),
        out_shape=(jax.ShapeDtypeStruct((s, D_POOL), F32), jax.ShapeDtypeStruct((s, D_POOL), F32),
                   jax.ShapeDtypeStruct((4, POOL_GROUP, POOL_GROUP), F32), jax.ShapeDtypeStruct((1, D_POOL), F32)),
        in_specs=[tile, pl.BlockSpec((tp, D_POOL), lambda i: (jnp.maximum(i - 1, 0), 0)),
                  pl.BlockSpec((tp, D_POOL), lambda i: (i, 1)), pw, ps, tile],
        out_specs=(tile, tile, pw, ps),
        compiler_params=_cp(("arbitrary",)),
    )(proj_main, proj_main, proj_main, pool_w, pool_scale, dyp)


def _pool_bwd2(dwin, tp):
    s = dwin.shape[0]
    n = s // tp

    def body(d_ref, dn_ref, du_ref):
        i = pl.program_id(0)
        ext = jnp.concatenate([d_ref[...], _next_head(dn_ref, i, n)], axis=0)
        m = ext.shape[0]
        a2 = ext + pltpu.roll(ext, m - 1, 0)
        a4 = a2 + pltpu.roll(a2, m - 2, 0)
        a8 = a4 + pltpu.roll(a4, m - 4, 0)
        a16 = a8 + pltpu.roll(a8, m - 8, 0)
        for gi, (w, acc) in enumerate(zip(POOL_WINDOWS, (a2, a4, a8, a16))):
            cols = slice(gi * POOL_GROUP, (gi + 1) * POOL_GROUP)
            du_ref[:, cols] = acc[:tp, cols] - d_ref[:, cols] * _pool_counts(i, tp, w)

    tile = pl.BlockSpec((tp, D_POOL), lambda i: (i, 0))
    return pl.pallas_call(
        body, name="pool_bwd2", grid=(n,),
        out_shape=jax.ShapeDtypeStruct((s, D_POOL), F32),
        in_specs=[tile, pl.BlockSpec((tp, D_POOL), lambda i: (jnp.minimum(i + 1, n - 1), 0))],
        out_specs=tile,
        compiler_params=_cp(("parallel",)),
    )(dwin, dwin)


def _in_proj_bwd(dparts, dba, x2, dh, norm_w, w_main, w_ba, tm):
    s = x2.shape[0]
    nstep = s // tm

    def body(d0, d1, d2, d3, d4, d5, dba_ref, x_ref, dh_ref, nw_ref, wm_hbm, wb_ref,
             gx_ref, gwm_hbm, gwb_ref, gnw_ref, wm_vmem, gwm_acc, sem):
        i = pl.program_id(0)

        @pl.when(i == 0)
        def _():
            cp = pltpu.make_async_copy(wm_hbm, wm_vmem, sem)
            cp.start()
            gwm_acc[...] = jnp.zeros_like(gwm_acc)
            gwb_ref[...] = jnp.zeros_like(gwb_ref)
            gnw_ref[...] = jnp.zeros_like(gnw_ref)
            cp.wait()
        dbab = dba_ref[...].astype(BF16)
        xhat, r = _rms_hat(x_ref[...])
        nw = nw_ref[...]
        n = (xhat * nw).astype(BF16)
        gwb_ref[...] += _dot_tn_bf(n, dbab)
        dn = _dot_nt_bf(dbab, wb_ref[...])
        for cb, d in enumerate((d0, d1, d2, d3, d4, d5)):
            cols = slice(cb * D_POOL, (cb + 1) * D_POOL)
            dpart = d[...].astype(BF16)
            gwm_acc[:, cols] += _dot_tn_bf(n, dpart)
            dn = dn + _dot_nt_bf(dpart, wm_vmem[:, cols])
        gnw_ref[...] += jnp.sum(dn * xhat, axis=0, keepdims=True)
        dxh = dn * nw
        gx_ref[...] = dh_ref[...] + r * (dxh - xhat * jnp.mean(dxh * xhat, axis=-1, keepdims=True))

        @pl.when(i == nstep - 1)
        def _():
            cp = pltpu.make_async_copy(gwm_acc, gwm_hbm, sem)
            cp.start()
            cp.wait()

    part = pl.BlockSpec((tm, D_POOL), lambda i: (i, 0))
    full = pl.BlockSpec((tm, D_MODEL), lambda i: (i, 0))
    row = pl.BlockSpec((1, D_MODEL), lambda i: (0, 0))
    wb = pl.BlockSpec((D_MODEL, 128), lambda i: (0, 0))
    return pl.pallas_call(
        body, name="in_proj_bwd", grid=(nstep,),
        out_shape=(jax.ShapeDtypeStruct((s, D_MODEL), F32), jax.ShapeDtypeStruct((D_MODEL, D_MAIN), F32),
                   jax.ShapeDtypeStruct((D_MODEL, 128), F32), jax.ShapeDtypeStruct((1, D_MODEL), F32)),
        in_specs=[part] * 6 + [pl.BlockSpec((tm, 128), lambda i: (i, 0)), full, full, row,
                               pl.BlockSpec(memory_space=pl.ANY), wb],
        out_specs=(full, pl.BlockSpec(memory_space=pl.ANY), wb, row),
        scratch_shapes=[pltpu.VMEM((D_MODEL, D_MAIN), BF16), pltpu.VMEM((D_MODEL, D_MAIN), F32),
                        pltpu.SemaphoreType.DMA],
        compiler_params=_cp(("arbitrary",)),
    )(*dparts, dba, x2, dh, norm_w, w_main, w_ba)


def _adamw_math(w, g, m, v):
    m = ADAM_B1 * m + (1.0 - ADAM_B1) * g
    v = ADAM_B2 * v + (1.0 - ADAM_B2) * (g * g)
    m_hat = m / (1.0 - ADAM_B1 ** ADAM_STEP)
    v_hat = v / (1.0 - ADAM_B2 ** ADAM_STEP)
    delta = -ADAM_LR * (m_hat / (jnp.sqrt(v_hat) + ADAM_EPS) + ADAM_WD * w)
    return delta, m, v


def _adamw_sharded(params):
    k = len(params)

    def body(*refs):
        ins, outs = refs[:4 * k], refs[4 * k:]
        for p in range(k):
            w, g, m, v = (r[...] for r in ins[4 * p:4 * p + 4])
            d, nm, nv = _adamw_math(w, g, m, v)
            outs[3 * p][...] = d
            outs[3 * p + 1][...] = nm
            outs[3 * p + 2][...] = nv

    flat = [a for p in params for a in p]
    out_shape = tuple(jax.ShapeDtypeStruct(p[0].shape, F32) for p in params for _ in range(3))
    res = pl.pallas_call(body, name="adamw_sharded", out_shape=out_shape, compiler_params=_cp())(*flat)
    return [tuple(res[3 * p:3 * p + 3]) for p in range(k)]


def _adamw_replicated(gath_a, gath_b, wa, ma, va, wb, mb, vb):
    def body(ga_ref, gb_ref, wa_ref, ma_ref, va_ref, wb_ref, mb_ref, vb_ref, *outs):
        for g_ref, w_ref, m_ref, v_ref, o in ((ga_ref, wa_ref, ma_ref, va_ref, outs[:4]),
                                              (gb_ref, wb_ref, mb_ref, vb_ref, outs[4:])):
            g = g_ref[0]
            for d in range(1, N_DEV):
                g = g + g_ref[d]
            dl, nm, nv = _adamw_math(w_ref[...], g, m_ref[...], v_ref[...])
            o[0][...] = g
            o[1][...] = dl
            o[2][...] = nm
            o[3][...] = nv

    out_shape = (jax.ShapeDtypeStruct(wa.shape, F32),) * 4 + (jax.ShapeDtypeStruct(wb.shape, F32),) * 4
    return pl.pallas_call(body, name="adamw_replicated", out_shape=out_shape, compiler_params=_cp())(
        gath_a, gath_b, wa, ma, va, wb, mb, vb)


_ROW_ORDER = ("norm_w", "final_norm_w", "pool_scale", "dn_norm_w", "a_log", "dt_bias")
_ROW_LEN = {"norm_w": 1024, "final_norm_w": 1024, "pool_scale": 512, "dn_norm_w": 128, "a_log": 4, "dt_bias": 4}


def _pack_rows(rows):
    out = [jnp.pad(rows[nm].reshape(-1), (0, D_MODEL - _ROW_LEN[nm])) for nm in _ROW_ORDER]
    extra = rows.get("extra")
    out.append(jnp.zeros((D_MODEL,), F32) if extra is None else jnp.pad(extra.reshape(-1), (0, D_MODEL - 1)))
    out.append(jnp.zeros((D_MODEL,), F32))
    return jnp.stack(out, axis=0)


def _lane_row(vec4, start):
    return jnp.pad(vec4.reshape(-1), (start, 128 - start - vec4.size)).reshape(1, 128)


def kernel(x, norm_w, w_in, pool_w, pool_scale, conv_w, a_log, dt_bias, dn_norm_w, w_out, final_norm_w, loss_target, m_norm_w, m_w_in, m_pool_w, m_pool_scale, m_conv_w, m_a_log, m_dt_bias, m_dn_norm_w, m_w_out, m_final_norm_w, v_norm_w, v_w_in, v_pool_w, v_pool_scale, v_conv_w, v_a_log, v_dt_bias, v_dn_norm_w, v_w_out, v_final_norm_w):
    s = x.shape[1]
    tm = min(512, s)
    tmb = min(256, s)
    tp = min(512, s)
    x2 = x[0]
    tgt = loss_target[0]

    g_in, g_out, g_conv = _gather_weights(w_in[0], w_out[0], conv_w[0])
    w_full = g_in.transpose(1, 0, 2).reshape(D_MODEL, D_IN)
    w_main = w_full[:, :D_MAIN]
    w_ba = jnp.pad(w_full[:, D_MAIN:], ((0, 0), (0, 128 - (D_IN - D_MAIN))))
    w_out_full = g_out.reshape(D_MODEL, D_MODEL)
    conv_full = g_conv.transpose(1, 0, 2).reshape(CONV_WIDTH, 3 * D_DN)
    alog_lane = _lane_row(a_log, DN_HEADS)
    dtb_lane = _lane_row(dt_bias, DN_HEADS)
    fnw = final_norm_w.reshape(1, D_MODEL)

    proj_main, proj_ba = _in_proj(x2, norm_w, w_main, w_ba, tm)
    y_pool = _pool_fwd(proj_main, pool_w[0], pool_scale, tp)
    qn, kn, vv, gb = _dn_pre(proj_main, proj_ba, conv_full, alog_lane, dtb_lane, tp)
    y_dn, states = _dn_scan_fwd(qn, kn, vv, gb, proj_main, dn_norm_w, 2)

    dh, dyp, dyd, g_wout, g_fnw, loss_part = _out_proj_loss(y_pool, y_dn, x2, tgt, w_out_full, fnw, tm)
    dqn, dkn, dvv, dgb, ddz, g_dnw = _dn_scan_bwd(qn, kn, vv, gb, proj_main, dn_norm_w, states, dyd, 1)
    dcq, dck, dcv, dba, g_conv_full, g_al, g_db = _dn_pre_bwd1(
        proj_main, proj_ba, conv_full, alog_lane, dtb_lane, dqn, dkn, dvv, dgb, tp)
    dq, dk, dv = _conv_bwd_x(dcq, dck, dcv, conv_full, tp)
    dzp, dwin, g_pw, g_ps = _pool_bwd1(proj_main, pool_w[0], pool_scale, dyp, tp)
    dup = _pool_bwd2(dwin, tp)
    grad_x2, g_wmain, g_wba, g_nw = _in_proj_bwd((dup, dzp, dq, dk, dv, ddz), dba, x2, dh, norm_w, w_main, w_ba, tmb)

    g_win_full = jnp.concatenate([g_wmain, g_wba[:, :D_IN - D_MAIN]], axis=1)
    p_in = g_win_full.reshape(D_MODEL, N_DEV, W_IN_SHARD).transpose(1, 0, 2)
    p_out = g_wout.reshape(N_DEV, D_MODEL // N_DEV, D_MODEL)
    p_conv = g_conv_full.reshape(CONV_WIDTH, N_DEV, 3 * D_DN // N_DEV).transpose(1, 0, 2)
    pack_a = g_pw.reshape(4 * POOL_GROUP, POOL_GROUP)
    pack_b = _pack_rows({"norm_w": g_nw, "final_norm_w": g_fnw, "pool_scale": g_ps, "dn_norm_w": g_dnw,
                         "a_log": g_al[0, DN_HEADS:2 * DN_HEADS], "dt_bias": g_db[0, DN_HEADS:2 * DN_HEADS],
                         "extra": loss_part[0, :1]})
    gr_in, gr_out, gr_conv, gath_a, gath_b = _reduce_grads(p_in, p_out, p_conv, pack_a, pack_b)

    (d_in, nm_in, nv_in), (d_out, nm_out, nv_out), (d_conv, nm_conv, nv_conv) = _adamw_sharded([
        (w_in[0], gr_in, m_w_in[0], v_w_in[0]), (w_out[0], gr_out, m_w_out[0], v_w_out[0]),
        (conv_w[0], gr_conv, m_conv_w[0], v_conv_w[0])])
    rep = {"w": dict(norm_w=norm_w, final_norm_w=final_norm_w, pool_scale=pool_scale, dn_norm_w=dn_norm_w,
                     a_log=a_log, dt_bias=dt_bias),
           "m": dict(norm_w=m_norm_w, final_norm_w=m_final_norm_w, pool_scale=m_pool_scale, dn_norm_w=m_dn_norm_w,
                     a_log=m_a_log, dt_bias=m_dt_bias),
           "v": dict(norm_w=v_norm_w, final_norm_w=v_final_norm_w, pool_scale=v_pool_scale, dn_norm_w=v_dn_norm_w,
                     a_log=v_a_log, dt_bias=v_dt_bias)}
    flat = lambda a: a.reshape(4 * POOL_GROUP, POOL_GROUP)
    res = _adamw_replicated(gath_a, gath_b, flat(pool_w), flat(m_pool_w), flat(v_pool_w),
                            _pack_rows(rep["w"]), _pack_rows(rep["m"]), _pack_rows(rep["v"]))
    ga, da, nma, nva, gbk, dbk, nmb, nvb = res
    loss = gbk[6, 0]

    def rows_of(pack):
        out = {}
        for r, nm in enumerate(_ROW_ORDER):
            out[nm] = pack[r, :_ROW_LEN[nm]].reshape(rep["w"][nm].shape)
        return out

    def group(sharded_in, sharded_out, sharded_conv, pack_a_out, pack_b_out):
        rw = rows_of(pack_b_out)
        return (rw["norm_w"], sharded_in[None], pack_a_out.reshape(pool_w.shape), rw["pool_scale"], sharded_conv[None],
                rw["a_log"], rw["dt_bias"], rw["dn_norm_w"], sharded_out[None], rw["final_norm_w"])

    return (loss, grad_x2[None],
            *group(gr_in, gr_out, gr_conv, ga, gbk),
            *group(d_in, d_out, d_conv, da, dbk),
            *group(nm_in, nm_out, nm_conv, nma, nmb),
            *group(nv_in, nv_out, nv_conv, nva, nvb))
```

```python
import functools

import jax
import jax.numpy as jnp
from jax import lax
from jax.experimental import pallas as pl
from jax.experimental.pallas import tpu as pltpu

F32 = jnp.float32
BF16 = jnp.bfloat16
HI = lax.Precision.HIGHEST
MESH = pl.DeviceIdType.MESH

D_MODEL = 1024
D_POOL = 512
D_DN = 512
POOL_WINDOWS = (2, 4, 8, 16)
POOL_GROUP = 128
DN_HEADS = 4
DN_HEAD_DIM = 128
CONV_WIDTH = 4
CHUNK = 64
NORM_EPS = 1e-6
D_IN = 3080
D_MAIN = 3072
N_DEV = 8
W_IN_SHARD = D_IN // N_DEV
HALO = 16
DN_CHUNKS_PER_STEP = 4

ADAM_LR = 0.001
ADAM_B1 = 0.9
ADAM_B2 = 0.999
ADAM_EPS = 1e-08
ADAM_WD = 0.01
ADAM_STEP = 10

VMEM_LIMIT = 56 * 1024 * 1024


def _cp(sem=None, vmem=VMEM_LIMIT):
    kw = {"vmem_limit_bytes": vmem}
    if sem is not None:
        kw["dimension_semantics"] = sem
    return pltpu.CompilerParams(**kw)


def _dot_hi(a, b):
    return jnp.dot(a, b, precision=HI, preferred_element_type=F32)


def _dot_bf(a, b):
    return jnp.dot(a.astype(BF16), b.astype(BF16), preferred_element_type=F32)


def _dot_nt_bf(a, b):
    return lax.dot_general(a.astype(BF16), b.astype(BF16), (((1,), (1,)), ((), ())), preferred_element_type=F32)


def _dot_tn_bf(a, b):
    return lax.dot_general(a.astype(BF16), b.astype(BF16), (((0,), (0,)), ((), ())), preferred_element_type=F32)


def _dot_tn_hi(a, b):
    return lax.dot_general(a, b, (((0,), (0,)), ((), ())), precision=HI, preferred_element_type=F32)


def _mm_raw(a, b, ca, cb, prec):
    off = a.ndim - 2
    dn = (((ca + off,), (cb + off,)), ((0,), (0,)) if off else ((), ()))
    if prec == "hi":
        return lax.dot_general(a, b, dn, precision=HI, preferred_element_type=F32)
    ah, bh = a.astype(BF16), b.astype(BF16)
    out = lax.dot_general(ah, bh, dn, preferred_element_type=F32)
    if prec == "x3":
        al = (a - ah.astype(F32)).astype(BF16)
        bl = (b - bh.astype(F32)).astype(BF16)
        out = out + lax.dot_general(ah, bl, dn, preferred_element_type=F32)
        out = out + lax.dot_general(al, bh, dn, preferred_element_type=F32)
    return out


@functools.partial(jax.custom_vjp, nondiff_argnums=(2, 3, 4, 5))
def _mm(a, b, ca, cb, prec, bprec):
    return _mm_raw(a, b, ca, cb, prec)


def _mm_fwd(a, b, ca, cb, prec, bprec):
    return _mm_raw(a, b, ca, cb, prec), (a, b)


def _mm_bwd(ca, cb, prec, bprec, res, dc):
    a, b = res
    da = _mm_raw(dc, b, 1, 1 - cb, bprec) if ca == 1 else _mm_raw(b, dc, 1 - cb, 1, bprec)
    db = _mm_raw(a, dc, 1 - ca, 0, bprec) if cb == 0 else _mm_raw(dc, a, 0, 1 - ca, bprec)
    return da, db


_mm.defvjp(_mm_fwd, _mm_bwd)


@functools.partial(jax.custom_vjp, nondiff_argnums=(1, 2))
def _tri_inv(a, prec, bprec):
    n = a.shape[-1]
    ii = lax.broadcasted_iota(jnp.int32, (n, n), 0)
    jj = lax.broadcasted_iota(jnp.int32, (n, n), 1)
    p = (ii == jj).astype(F32) - a
    b = _mm_raw(a, a, 1, 0, prec)
    for _ in range(4):
        pb = _mm_raw(jnp.concatenate([p, b], axis=-2), b, 1, 0, prec)
        p = p + pb[..., :n, :]
        b = pb[..., n:, :]
    return p + _mm_raw(p, b, 1, 0, prec)


def _tri_inv_fwd(a, prec, bprec):
    t = _tri_inv(a, prec, bprec)
    return t, t


def _tri_inv_bwd(prec, bprec, t, dt):
    return (-_mm_raw(_mm_raw(t, dt, 0, 0, bprec), t, 1, 1, bprec),)


_tri_inv.defvjp(_tri_inv_fwd, _tri_inv_bwd)

_DN_PREC = {"akq": ("bf16", "bf16"), "inv": ("x3", "x3"), "uw": ("bf16", "bf16"), "ws": ("bf16", "bf16"),
            "ov": ("bf16", "bf16"), "st": ("bf16", "bf16")}


def _silu(x):
    return x * jax.nn.sigmoid(x)


def _softplus(x):
    pos = x > 0.0
    return jnp.where(pos, x, 0.0) + jnp.log1p(jnp.exp(jnp.where(pos, -x, x)))


def _mesh_pos():
    return lax.axis_index("x"), lax.axis_index("y"), lax.axis_index("c")


def _dev_index(x, y, c):
    return 4 * x + 2 * y + c


def _all_gather_blocks(outs, send_sems, recv_sems):
    x, y, c = _mesh_pos()
    me = (x, y, c)
    sibling = (x, y, 1 - c)
    chips = [(1 - x, y), (x, 1 - y), (1 - x, 1 - y)]

    def copy(a, k, block, to):
        rows = outs[a].at[_dev_index(*block)]
        return pltpu.make_async_remote_copy(src_ref=rows, dst_ref=rows, send_sem=send_sems.at[a, k],
                                            recv_sem=recv_sems.at[a, k], device_id=to, device_id_type=MESH)

    n = len(outs)
    first = []
    for a in range(n):
        first.append(copy(a, 0, me, sibling))
        for j, chip in enumerate(chips):
            first.append(copy(a, 1 + j, me, (*chip, c)))
    for cp in first:
        cp.start()
    passed = []
    for j, chip in enumerate(chips):
        for a in range(n):
            copy(a, 1 + j, (*chip, c), me).wait_recv()
            fwd = copy(a, 4 + j, (*chip, c), sibling)
            fwd.start()
            passed.append(fwd)
    for a in range(n):
        copy(a, 0, sibling, me).wait_recv()
        for j, chip in enumerate(chips):
            copy(a, 4 + j, (*chip, 1 - c), me).wait_recv()
    for cp in first + passed:
        cp.wait_send()


def _gather_weights(w_in_blk, w_out_blk, conv_blk):
    def body(win_ref, wout_ref, conv_ref, gin_ref, gout_ref, gconv_ref, send_sems, recv_sems):
        x, y, c = _mesh_pos()
        me = _dev_index(x, y, c)
        gin_ref[me] = win_ref[...].astype(BF16)
        gout_ref[me] = wout_ref[...].astype(BF16)
        gconv_ref[me] = conv_ref[...]
        _all_gather_blocks((gin_ref, gout_ref, gconv_ref), send_sems, recv_sems)

    vm = pl.BlockSpec(memory_space=pltpu.VMEM)
    return pl.pallas_call(
        body, name="gather_weights",
        out_shape=(jax.ShapeDtypeStruct((N_DEV,) + w_in_blk.shape, BF16),
                   jax.ShapeDtypeStruct((N_DEV,) + w_out_blk.shape, BF16),
                   jax.ShapeDtypeStruct((N_DEV,) + conv_blk.shape, F32)),
        in_specs=[vm, vm, vm], out_specs=(vm, vm, vm),
        scratch_shapes=[pltpu.SemaphoreType.DMA((3, 7)), pltpu.SemaphoreType.DMA((3, 7))],
        compiler_params=_cp(),
    )(w_in_blk, w_out_blk, conv_blk)


def _reduce_grads(p_in, p_out, p_conv, pack_a, pack_b):
    big = (p_in, p_out, p_conv)

    def body(pin_ref, pout_ref, pconv_ref, pa_ref, pb_ref,
             oin_ref, oout_ref, oconv_ref, ga_ref, gb_ref,
             r1_in, r1_out, r1_conv, r2_in, r2_out, r2_conv, st_in, st_out, st_conv,
             s1_send, s1_recv, s2_send, s2_recv, ag_send, ag_recv, st_sem):
        x, y, c = _mesh_pos()
        me = (x, y, c)
        sibling = (x, y, 1 - c)
        rel = [(x, y), (1 - x, y), (x, 1 - y), (1 - x, 1 - y)]
        srcs = (pin_ref, pout_ref, pconv_ref)
        r1s = (r1_in, r1_out, r1_conv)
        r2s = (r2_in, r2_out, r2_conv)
        sts = (st_in, st_out, st_conv)
        outs = (oin_ref, oout_ref, oconv_ref)

        ga_ref[_dev_index(*me)] = pa_ref[...]
        gb_ref[_dev_index(*me)] = pb_ref[...]

        def p1(a, r, to):
            return pltpu.make_async_remote_copy(
                src_ref=srcs[a].at[_dev_index(*rel[r], 1 - c)], dst_ref=r1s[a].at[r],
                send_sem=s1_send.at[a, r], recv_sem=s1_recv.at[a, r], device_id=to, device_id_type=MESH)

        def p2(a, r, to):
            return pltpu.make_async_remote_copy(
                src_ref=r1s[a].at[r], dst_ref=r2s[a].at[r - 1],
                send_sem=s2_send.at[a, r - 1], recv_sem=s2_recv.at[a, r - 1], device_id=to, device_id_type=MESH)

        def stage(a, r):
            return pltpu.make_async_copy(srcs[a].at[_dev_index(*rel[r], c)], sts[a].at[r % 2], st_sem.at[a, r % 2])

        sends1 = [p1(a, r, sibling) for a in range(3) for r in range(4)]
        for cp in sends1:
            cp.start()
        sends2 = []
        for a in range(3):
            stage(a, 1).start()
            for r in (1, 2, 3, 0):
                nxt = {1: 2, 2: 3, 3: 0, 0: None}[r]
                if nxt is not None:
                    stage(a, nxt).start()
                stage(a, r).wait()
                p1(a, r, me).wait_recv()
                r1s[a][r] = r1s[a][r] + sts[a][r % 2]
                if r != 0:
                    cp = p2(a, r, (*rel[r], c))
                    cp.start()
                    sends2.append(cp)
        _all_gather_blocks((ga_ref, gb_ref), ag_send, ag_recv)
        for a in range(3):
            for r in (1, 2, 3):
                p2(a, r, me).wait_recv()
            outs[a][...] = ((r1s[a][0] + r2s[a][0]) + r2s[a][1]) + r2s[a][2]
        for cp in sends1 + sends2:
            cp.wait_send()

    vm = pl.BlockSpec(memory_space=pltpu.VMEM)
    hbm = pl.BlockSpec(memory_space=pl.ANY)
    blk = [p.shape[1:] for p in big]
    scratch = ([pltpu.VMEM((4,) + b, F32) for b in blk] + [pltpu.VMEM((3,) + b, F32) for b in blk]
               + [pltpu.VMEM((2,) + b, F32) for b in blk]
               + [pltpu.SemaphoreType.DMA((3, 4)), pltpu.SemaphoreType.DMA((3, 4)),
                  pltpu.SemaphoreType.DMA((3, 3)), pltpu.SemaphoreType.DMA((3, 3)),
                  pltpu.SemaphoreType.DMA((2, 7)), pltpu.SemaphoreType.DMA((2, 7)),
                  pltpu.SemaphoreType.DMA((3, 2))])
    return pl.pallas_call(
        body, name="reduce_grads",
        out_shape=tuple(jax.ShapeDtypeStruct(b, F32) for b in blk)
        + (jax.ShapeDtypeStruct((N_DEV,) + pack_a.shape, F32), jax.ShapeDtypeStruct((N_DEV,) + pack_b.shape, F32)),
        in_specs=[hbm, hbm, hbm, vm, vm], out_specs=(vm, vm, vm, vm, vm),
        scratch_shapes=scratch,
        compiler_params=_cp(),
    )(p_in, p_out, p_conv, pack_a, pack_b)


def _rms_hat(xf):
    r = lax.rsqrt(jnp.mean(xf * xf, axis=-1, keepdims=True) + NORM_EPS)
    return xf * r, r


def _in_proj(x2, norm_w, w_main, w_ba, tm):
    s = x2.shape[0]

    def body(x_ref, nw_ref, wm_hbm, wb_ref, pm_ref, pb_ref, wm_vmem, sem):
        @pl.when(pl.program_id(0) == 0)
        def _():
            cp = pltpu.make_async_copy(wm_hbm, wm_vmem, sem)
            cp.start()
            cp.wait()
        xhat, _ = _rms_hat(x_ref[...])
        n = (xhat * nw_ref[...]).astype(BF16)
        pm_ref[...] = jnp.dot(n, wm_vmem[...], preferred_element_type=F32)
        pb_ref[...] = jnp.dot(n, wb_ref[...], preferred_element_type=F32)

    return pl.pallas_call(
        body, name="in_proj", grid=(s // tm,),
        out_shape=(jax.ShapeDtypeStruct((s, D_MAIN), F32), jax.ShapeDtypeStruct((s, 128), F32)),
        in_specs=[pl.BlockSpec((tm, D_MODEL), lambda i: (i, 0)),
                  pl.BlockSpec((1, D_MODEL), lambda i: (0, 0)),
                  pl.BlockSpec(memory_space=pl.ANY),
                  pl.BlockSpec((D_MODEL, 128), lambda i: (0, 0))],
        out_specs=(pl.BlockSpec((tm, D_MAIN), lambda i: (i, 0)), pl.BlockSpec((tm, 128), lambda i: (i, 0))),
        scratch_shapes=[pltpu.VMEM((D_MODEL, D_MAIN), BF16), pltpu.SemaphoreType.DMA],
        compiler_params=_cp(("arbitrary",)),
    )(x2, norm_w, w_main, w_ba)


def _shift_down(cur, prev_tail, s):
    ext = jnp.concatenate([prev_tail, cur], axis=0)
    return pltpu.roll(ext, s, 0)[HALO:, :]


def _shift_up(cur, next_head, s):
    ext = jnp.concatenate([cur, next_head], axis=0)
    n = ext.shape[0]
    return pltpu.roll(ext, n - s, 0)[:cur.shape[0], :]


def _pool_counts(i, tp, w):
    t = i * tp + lax.broadcasted_iota(jnp.int32, (tp, 1), 0)
    return jnp.minimum(t + 1, w).astype(F32)


def _pool_mix(u, u_prev_tail, i, tp):
    ext = jnp.concatenate([u_prev_tail, u], axis=0)
    w2 = ext + pltpu.roll(ext, 1, 0)
    w4 = w2 + pltpu.roll(w2, 2, 0)
    w8 = w4 + pltpu.roll(w4, 4, 0)
    w16 = w8 + pltpu.roll(w8, 8, 0)
    mixes = []
    for gi, (w, win) in enumerate(zip(POOL_WINDOWS, (w2, w4, w8, w16))):
        cols = slice(gi * POOL_GROUP, (gi + 1) * POOL_GROUP)
        mixes.append(win[HALO:, cols] / _pool_counts(i, tp, w) - u[:, cols])
    return mixes


def _prev_tail(ref, i):
    tail = ref[ref.shape[0] - HALO:, :]
    return jnp.where(i > 0, tail, 0.0)


def _next_head(ref, i, n):
    head = ref[:HALO, :]
    return jnp.where(i < n - 1, head, 0.0)


def _pool_fwd(proj_main, pool_w, pool_scale, tp):
    s = proj_main.shape[0]

    def body(u_ref, up_ref, z_ref, pw_ref, ps_ref, y_ref):
        i = pl.program_id(0)
        u = u_ref[...]
        mixes = _pool_mix(u, _prev_tail(up_ref, i), i, tp)
        gate = ps_ref[...] * _silu(z_ref[...])
        for gi in range(4):
            cols = slice(gi * POOL_GROUP, (gi + 1) * POOL_GROUP)
            y_ref[:, cols] = _dot_hi(mixes[gi], pw_ref[gi]) * gate[:, cols]

    return pl.pallas_call(
        body, name="pool_fwd", grid=(s // tp,),
        out_shape=jax.ShapeDtypeStruct((s, D_POOL), F32),
        in_specs=[pl.BlockSpec((tp, D_POOL), lambda i: (i, 0)),
                  pl.BlockSpec((tp, D_POOL), lambda i: (jnp.maximum(i - 1, 0), 0)),
                  pl.BlockSpec((tp, D_POOL), lambda i: (i, 1)),
                  pl.BlockSpec((4, POOL_GROUP, POOL_GROUP), lambda i: (0, 0, 0)),
                  pl.BlockSpec((1, D_POOL), lambda i: (0, 0))],
        out_specs=pl.BlockSpec((tp, D_POOL), lambda i: (i, 0)),
        compiler_params=_cp(("parallel",)),
    )(proj_main, proj_main, proj_main, pool_w, pool_scale)


def _conv_fwd(cur, prev_tail, w4):
    ext = jnp.concatenate([prev_tail, cur], axis=0)
    y = ext * w4[CONV_WIDTH - 1:CONV_WIDTH, :]
    for sft in range(1, CONV_WIDTH):
        y = y + pltpu.roll(ext, sft, 0) * w4[CONV_WIDTH - 1 - sft:CONV_WIDTH - sft, :]
    return y[HALO:, :]


def _l2n_heads(t):
    parts = []
    for h in range(DN_HEADS):
        th = t[:, h * DN_HEAD_DIM:(h + 1) * DN_HEAD_DIM]
        parts.append(th * lax.rsqrt(jnp.sum(th * th, axis=-1, keepdims=True) + NORM_EPS))
    return jnp.concatenate(parts, axis=1)


def _post_conv(yq, yk, yv):
    return _l2n_heads(_silu(yq)), _l2n_heads(_silu(yk)), _silu(yv)


def _gates(ba, alog_lane, dtb_lane):
    lane = lax.broadcasted_iota(jnp.int32, ba.shape, 1)
    beta = jax.nn.sigmoid(ba)
    g = -jnp.exp(alog_lane) * _softplus(ba + dtb_lane)
    return jnp.where(lane < DN_HEADS, beta, jnp.where(lane < 2 * DN_HEADS, g, 0.0))


def _qkv_specs(tp, which):
    def spec(col, n=None):
        if which == 0:
            return pl.BlockSpec((tp, D_DN), lambda i: (i, col))
        if which < 0:
            return pl.BlockSpec((tp, D_DN), lambda i: (jnp.maximum(i - 1, 0), col))
        return pl.BlockSpec((tp, D_DN), lambda i: (jnp.minimum(i + 1, n - 1), col))
    return spec


def _dn_pre(proj_main, proj_ba, conv_full, alog_lane, dtb_lane, tp):
    s = proj_main.shape[0]

    def body(q_ref, k_ref, v_ref, qp_ref, kp_ref, vp_ref, cw_ref, ba_ref, al_ref, db_ref,
             qn_ref, kn_ref, vv_ref, gb_ref):
        i = pl.program_id(0)
        ys = []
        for j, (cur, prev) in enumerate(((q_ref, qp_ref), (k_ref, kp_ref), (v_ref, vp_ref))):
            ys.append(_conv_fwd(cur[...], _prev_tail(prev, i), cw_ref[:, j * D_DN:(j + 1) * D_DN]))
        qn, kn, vv = _post_conv(*ys)
        qn_ref[...] = qn
        kn_ref[...] = kn
        vv_ref[...] = vv
        gb_ref[...] = _gates(ba_ref[...], al_ref[...], db_ref[...])

    cur, prev = _qkv_specs(tp, 0), _qkv_specs(tp, -1)
    row = pl.BlockSpec((1, 128), lambda i: (0, 0))
    tile = pl.BlockSpec((tp, D_DN), lambda i: (i, 0))
    return pl.pallas_call(
        body, name="dn_pre", grid=(s // tp,),
        out_shape=(jax.ShapeDtypeStruct((s, D_DN), F32),) * 3 + (jax.ShapeDtypeStruct((s, 128), F32),),
        in_specs=[cur(2), cur(3), cur(4), prev(2), prev(3), prev(4),
                  pl.BlockSpec((CONV_WIDTH, 3 * D_DN), lambda i: (0, 0)),
                  pl.BlockSpec((tp, 128), lambda i: (i, 0)), row, row],
        out_specs=(tile, tile, tile, pl.BlockSpec((tp, 128), lambda i: (i, 0))),
        compiler_params=_cp(("parallel",)),
    )(proj_main, proj_main, proj_main, proj_main, proj_main, proj_main, conv_full, proj_ba, alog_lane, dtb_lane)


def _dn_block(q, k, v, gcol, bcol, state, dz, nw):
    nb, n, d = q.shape
    ii = lax.broadcasted_iota(jnp.int32, (n, n), 0)
    jj = lax.broadcasted_iota(jnp.int32, (n, n), 1)
    lower = ii >= jj
    eye = (ii == jj).astype(F32)
    g_row = jnp.sum(eye * gcol, axis=1, keepdims=True)
    gc_col = jnp.sum(jnp.where(lower, g_row, 0.0), axis=2, keepdims=True)
    gc_row = jnp.sum(eye * gc_col, axis=1, keepdims=True)
    decay = jnp.where(lower, jnp.exp(jnp.where(lower, gc_col - gc_row, 0.0)), 0.0)
    kb = k * bcol
    vb = v * bcol
    qs = q * (DN_HEAD_DIM ** -0.5)
    egc = jnp.exp(gc_col)
    akq = _mm(jnp.concatenate([kb, qs], axis=1), k, 1, 1, *_DN_PREC["akq"])
    a = jnp.where(ii > jj, akq[:, :n] * decay, 0.0)
    qk = akq[:, n:] * decay
    t = _tri_inv(a, *_DN_PREC["inv"])
    uw = _mm(t, jnp.concatenate([vb, kb * egc], axis=2), 1, 0, *_DN_PREC["uw"])
    wq = jnp.concatenate([uw[:, :, d:], qs * egc], axis=1)
    g_last = gc_col[:, n - 1:n, :]
    k_dec = k * jnp.exp(g_last - gc_col)
    e_last = jnp.exp(g_last)
    os_ = []
    for c in range(nb // DN_HEADS):
        sl = slice(c * DN_HEADS, (c + 1) * DN_HEADS)
        ws = _mm(wq[sl], state, 1, 0, *_DN_PREC["ws"])
        v_new = uw[sl, :, :d] - ws[:, :n]
        os_.append(ws[:, n:] + _mm(qk[sl], v_new, 1, 0, *_DN_PREC["ov"]))
        state = state * e_last[sl] + _mm(k_dec[sl], v_new, 0, 0, *_DN_PREC["st"])
    o = jnp.concatenate(os_, axis=0)
    y = o * lax.rsqrt(jnp.mean(o * o, axis=-1, keepdims=True) + NORM_EPS) * nw * _silu(dz)
    return y, state


def _dn_block_args(gc, q_ref, k_ref, v_ref, gb_ref, dz_ref):
    qs, ks, vs, gs, bs, zs = [], [], [], [], [], []
    for cc in range(gc):
        r = slice(cc * CHUNK, (cc + 1) * CHUNK)
        gbv = gb_ref[r, :]
        for h in range(DN_HEADS):
            cols = slice(h * DN_HEAD_DIM, (h + 1) * DN_HEAD_DIM)
            qs.append(q_ref[r, cols])
            ks.append(k_ref[r, cols])
            vs.append(v_ref[r, cols])
            zs.append(dz_ref[r, cols])
            gs.append(gbv[:, DN_HEADS + h:DN_HEADS + h + 1])
            bs.append(gbv[:, h:h + 1])
    return tuple(jnp.stack(t, axis=0) for t in (qs, ks, vs, gs, bs, zs))


def _dn_scan_fwd(qn, kn, vv, gb, proj_main, dn_norm_w, gc):
    s = qn.shape[0]
    nchunk = s // CHUNK
    rows = gc * CHUNK

    def body(q_ref, k_ref, v_ref, gb_ref, dz_ref, nw_ref, y_ref, ss_ref, state):
        @pl.when(pl.program_id(0) == 0)
        def _():
            state[...] = jnp.zeros_like(state)
        q, k, v, gcol, bcol, dz = _dn_block_args(gc, q_ref, k_ref, v_ref, gb_ref, dz_ref)
        st = state[...]
        ss_ref[0] = st
        y, new = _dn_block(q, k, v, gcol, bcol, st, dz, nw_ref[...])
        state[...] = new
        for cc in range(gc):
            for h in range(DN_HEADS):
                y_ref[cc * CHUNK:(cc + 1) * CHUNK, h * DN_HEAD_DIM:(h + 1) * DN_HEAD_DIM] = y[cc * DN_HEADS + h]

    tile = pl.BlockSpec((rows, D_DN), lambda i: (i, 0))
    return pl.pallas_call(
        body, name="dn_scan_fwd", grid=(nchunk // gc,),
        out_shape=(jax.ShapeDtypeStruct((s, D_DN), F32),
                   jax.ShapeDtypeStruct((nchunk // gc, DN_HEADS, DN_HEAD_DIM, DN_HEAD_DIM), F32)),
        in_specs=[tile, tile, tile, pl.BlockSpec((rows, 128), lambda i: (i, 0)),
                  pl.BlockSpec((rows, D_DN), lambda i: (i, 5)), pl.BlockSpec((1, 128), lambda i: (0, 0))],
        out_specs=(tile, pl.BlockSpec((1, DN_HEADS, DN_HEAD_DIM, DN_HEAD_DIM), lambda i: (i, 0, 0, 0))),
        scratch_shapes=[pltpu.VMEM((DN_HEADS, DN_HEAD_DIM, DN_HEAD_DIM), F32)],
        compiler_params=_cp(("arbitrary",)),
    )(qn, kn, vv, gb, proj_main, dn_norm_w)


def _out_proj_loss(y_pool, y_dn, x2, tgt, w_out_full, fnw, tm):
    s = x2.shape[0]

    def body(yp_ref, yd_ref, x_ref, t_ref, wo_ref, fw_ref,
             dh_ref, dyp_ref, dyd_ref, gwo_ref, gfw_ref, loss_ref):
        @pl.when(pl.program_id(0) == 0)
        def _():
            gwo_ref[...] = jnp.zeros_like(gwo_ref)
            gfw_ref[...] = jnp.zeros_like(gfw_ref)
            loss_ref[...] = jnp.zeros_like(loss_ref)
        y = jnp.concatenate([yp_ref[...], yd_ref[...]], axis=1).astype(BF16)
        wo = wo_ref[...]
        h = x_ref[...] + jnp.dot(y, wo, preferred_element_type=F32)
        hn, r = _rms_hat(h)
        fw = fw_ref[...]
        err = hn * fw - t_ref[...]
        loss_ref[...] += 0.5 * jnp.sum(jnp.sum(err * err, axis=-1, keepdims=True) / D_MODEL, axis=0, keepdims=True)
        dout = err / D_MODEL
        gfw_ref[...] += jnp.sum(dout * hn, axis=0, keepdims=True)
        dhn = dout * fw
        dh = r * (dhn - hn * jnp.mean(dhn * hn, axis=-1, keepdims=True))
        dh_ref[...] = dh
        dhb = dh.astype(BF16)
        dy = _dot_nt_bf(dhb, wo)
        dyp_ref[...] = dy[:, :D_POOL]
        dyd_ref[...] = dy[:, D_POOL:]
        gwo_ref[...] += _dot_tn_bf(y, dhb)

    half = pl.BlockSpec((tm, D_POOL), lambda i: (i, 0))
    full = pl.BlockSpec((tm, D_MODEL), lambda i: (i, 0))
    return pl.pallas_call(
        body, name="out_proj_loss", grid=(s // tm,),
        out_shape=(jax.ShapeDtypeStruct((s, D_MODEL), F32), jax.ShapeDtypeStruct((s, D_POOL), F32),
                   jax.ShapeDtypeStruct((s, D_DN), F32), jax.ShapeDtypeStruct((D_MODEL, D_MODEL), F32),
                   jax.ShapeDtypeStruct((1, D_MODEL), F32), jax.ShapeDtypeStruct((1, 128), F32)),
        in_specs=[half, half, full, full, pl.BlockSpec((D_MODEL, D_MODEL), lambda i: (0, 0)),
                  pl.BlockSpec((1, D_MODEL), lambda i: (0, 0))],
        out_specs=(full, half, half, pl.BlockSpec((D_MODEL, D_MODEL), lambda i: (0, 0)),
                   pl.BlockSpec((1, D_MODEL), lambda i: (0, 0)), pl.BlockSpec((1, 128), lambda i: (0, 0))),
        compiler_params=_cp(("arbitrary",)),
    )(y_pool, y_dn, x2, tgt, w_out_full, fnw)


def _dn_scan_bwd(qn, kn, vv, gb, proj_main, dn_norm_w, states, dy_dn, gc):
    s = qn.shape[0]
    nchunk = s // CHUNK
    nstep = nchunk // gc
    rows = gc * CHUNK

    def body(q_ref, k_ref, v_ref, gb_ref, dz_ref, nw_ref, ss_ref, dy_ref,
             dq_ref, dk_ref, dv_ref, dgb_ref, ddz_ref, dnw_ref, dstate):
        @pl.when(pl.program_id(0) == 0)
        def _():
            dstate[...] = jnp.zeros_like(dstate)
            dnw_ref[...] = jnp.zeros_like(dnw_ref)
        lane = lax.broadcasted_iota(jnp.int32, (CHUNK, 128), 1)
        q, k, v, gcol, bcol, dz = _dn_block_args(gc, q_ref, k_ref, v_ref, gb_ref, dz_ref)
        dy = jnp.stack([dy_ref[cc * CHUNK:(cc + 1) * CHUNK, h * DN_HEAD_DIM:(h + 1) * DN_HEAD_DIM]
                        for cc in range(gc) for h in range(DN_HEADS)], axis=0)
        _, vjp = jax.vjp(_dn_block, q, k, v, gcol, bcol, ss_ref[0], dz, nw_ref[...])
        dq, dk, dv, dg, db, dst, ddz, dnw = vjp((dy, dstate[...]))
        dstate[...] = dst
        dnw_ref[...] += dnw
        for cc in range(gc):
            r = slice(cc * CHUNK, (cc + 1) * CHUNK)
            dgb = jnp.zeros((CHUNK, 128), F32)
            for h in range(DN_HEADS):
                b = cc * DN_HEADS + h
                cols = slice(h * DN_HEAD_DIM, (h + 1) * DN_HEAD_DIM)
                for ref, val in zip((dq_ref, dk_ref, dv_ref, ddz_ref), (dq, dk, dv, ddz)):
                    ref[r, cols] = val[b]
                dgb = dgb + jnp.where(lane == h, db[b], 0.0) + jnp.where(lane == DN_HEADS + h, dg[b], 0.0)
            dgb_ref[r, :] = dgb

    rev = lambda i: (nstep - 1 - i, 0)
    tile = pl.BlockSpec((rows, D_DN), rev)
    lanes = pl.BlockSpec((rows, 128), rev)
    return pl.pallas_call(
        body, name="dn_scan_bwd", grid=(nstep,),
        out_shape=(jax.ShapeDtypeStruct((s, D_DN), F32),) * 3
        + (jax.ShapeDtypeStruct((s, 128), F32), jax.ShapeDtypeStruct((s, D_DN), F32),
           jax.ShapeDtypeStruct((1, 128), F32)),
        in_specs=[tile, tile, tile, lanes, pl.BlockSpec((rows, D_DN), lambda i: (nstep - 1 - i, 5)),
                  pl.BlockSpec((1, 128), lambda i: (0, 0)),
                  pl.BlockSpec((1, DN_HEADS, DN_HEAD_DIM, DN_HEAD_DIM), lambda i: (nstep - 1 - i, 0, 0, 0)), tile],
        out_specs=(tile, tile, tile, lanes, tile, pl.BlockSpec((1, 128), lambda i: (0, 0))),
        scratch_shapes=[pltpu.VMEM((DN_HEADS, DN_HEAD_DIM, DN_HEAD_DIM), F32)],
        compiler_params=_cp(("arbitrary",)),
    )(qn, kn, vv, gb, proj_main, dn_norm_w, states, dy_dn)


def _dn_pre_bwd1(proj_main, proj_ba, conv_full, alog_lane, dtb_lane, dqn, dkn, dvv, dgb, tp):
    s = proj_main.shape[0]

    def body(q_ref, k_ref, v_ref, qp_ref, kp_ref, vp_ref, cw_ref, ba_ref, al_ref, db_ref,
             dqn_ref, dkn_ref, dvv_ref, dgb_ref,
             dcq_ref, dck_ref, dcv_ref, dba_ref, dcw_ref, dal_ref, ddb_ref):
        i = pl.program_id(0)

        @pl.when(i == 0)
        def _():
            dcw_ref[...] = jnp.zeros_like(dcw_ref)
            dal_ref[...] = jnp.zeros_like(dal_ref)
            ddb_ref[...] = jnp.zeros_like(ddb_ref)
        curs = (q_ref[...], k_ref[...], v_ref[...])
        tails = (_prev_tail(qp_ref, i), _prev_tail(kp_ref, i), _prev_tail(vp_ref, i))
        ys = [_conv_fwd(curs[j], tails[j], cw_ref[:, j * D_DN:(j + 1) * D_DN]) for j in range(3)]
        _, vjp = jax.vjp(_post_conv, *ys)
        dys = vjp((dqn_ref[...], dkn_ref[...], dvv_ref[...]))
        for j, (dy, out) in enumerate(zip(dys, (dcq_ref, dck_ref, dcv_ref))):
            out[...] = dy
            for sft in range(CONV_WIDTH):
                xs = curs[j] if sft == 0 else _shift_down(curs[j], tails[j], sft)
                row = CONV_WIDTH - 1 - sft
                dcw_ref[row:row + 1, j * D_DN:(j + 1) * D_DN] += jnp.sum(dy * xs, axis=0, keepdims=True)
        _, gvjp = jax.vjp(_gates, ba_ref[...], al_ref[...], db_ref[...])
        dba, dal, ddb = gvjp(dgb_ref[...])
        dba_ref[...] = dba
        dal_ref[...] += dal
        ddb_ref[...] += ddb

    cur, prev = _qkv_specs(tp, 0), _qkv_specs(tp, -1)
    row = pl.BlockSpec((1, 128), lambda i: (0, 0))
    tile = pl.BlockSpec((tp, D_DN), lambda i: (i, 0))
    lanes = pl.BlockSpec((tp, 128), lambda i: (i, 0))
    cw = pl.BlockSpec((CONV_WIDTH, 3 * D_DN), lambda i: (0, 0))
    return pl.pallas_call(
        body, name="dn_pre_bwd1", grid=(s // tp,),
        out_shape=(jax.ShapeDtypeStruct((s, D_DN), F32),) * 3
        + (jax.ShapeDtypeStruct((s, 128), F32), jax.ShapeDtypeStruct((CONV_WIDTH, 3 * D_DN), F32),
           jax.ShapeDtypeStruct((1, 128), F32), jax.ShapeDtypeStruct((1, 128), F32)),
        in_specs=[cur(2), cur(3), cur(4), prev(2), prev(3), prev(4), cw, lanes, row, row, tile, tile, tile, lanes],
        out_specs=(tile, tile, tile, lanes, cw, row, row),
        compiler_params=_cp(("arbitrary",)),
    )(proj_main, proj_main, proj_main, proj_main, proj_main, proj_main, conv_full, proj_ba, alog_lane, dtb_lane,
      dqn, dkn, dvv, dgb)


def _conv_bwd_x(dcq, dck, dcv, conv_full, tp):
    s = dcq.shape[0]
    n = s // tp

    def body(q_ref, k_ref, v_ref, qn_ref, kn_ref, vn_ref, cw_ref, dq_ref, dk_ref, dv_ref):
        i = pl.program_id(0)
        for j, (cur, nxt, out) in enumerate(((q_ref, qn_ref, dq_ref), (k_ref, kn_ref, dk_ref), (v_ref, vn_ref, dv_ref))):
            w4 = cw_ref[:, j * D_DN:(j + 1) * D_DN]
            dy = cur[...]
            head = _next_head(nxt, i, n)
            dx = dy * w4[CONV_WIDTH - 1:CONV_WIDTH, :]
            for sft in range(1, CONV_WIDTH):
                dx = dx + _shift_up(dy, head, sft) * w4[CONV_WIDTH - 1 - sft:CONV_WIDTH - sft, :]
            out[...] = dx

    tile = pl.BlockSpec((tp, D_DN), lambda i: (i, 0))
    nxt = pl.BlockSpec((tp, D_DN), lambda i: (jnp.minimum(i + 1, n - 1), 0))
    return pl.pallas_call(
        body, name="conv_bwd_x", grid=(n,),
        out_shape=(jax.ShapeDtypeStruct((s, D_DN), F32),) * 3,
        in_specs=[tile, tile, tile, nxt, nxt, nxt, pl.BlockSpec((CONV_WIDTH, 3 * D_DN), lambda i: (0, 0))],
        out_specs=(tile, tile, tile),
        compiler_params=_cp(("parallel",)),
    )(dcq, dck, dcv, dcq, dck, dcv, conv_full)


def _pool_bwd1(proj_main, pool_w, pool_scale, dyp, tp):
    s = proj_main.shape[0]

    def body(u_ref, up_ref, z_ref, pw_ref, ps_ref, dy_ref, dz_ref, dwin_ref, dpw_ref, dps_ref):
        i = pl.program_id(0)

        @pl.when(i == 0)
        def _():
            dpw_ref[...] = jnp.zeros_like(dpw_ref)
            dps_ref[...] = jnp.zeros_like(dps_ref)
        u = u_ref[...]
        z = z_ref[...]
        dy = dy_ref[...]
        ps = ps_ref[...]
        mixes = _pool_mix(u, _prev_tail(up_ref, i), i, tp)
        sg = jax.nn.sigmoid(z)
        sz = z * sg
        dsz = sg * (1.0 + z * (1.0 - sg))
        for gi, w in enumerate(POOL_WINDOWS):
            cols = slice(gi * POOL_GROUP, (gi + 1) * POOL_GROUP)
            mixw = _dot_hi(mixes[gi], pw_ref[gi])
            dmixw = dy[:, cols] * ps[:, cols] * sz[:, cols]
            dps_ref[:, cols] += jnp.sum(dy[:, cols] * mixw * sz[:, cols], axis=0, keepdims=True)
            dz_ref[:, cols] = dy[:, cols] * mixw * ps[:, cols] * dsz[:, cols]
            dpw_ref[gi] += _dot_tn_hi(mixes[gi], dmixw)
            dmix = lax.dot_general(dmixw, pw_ref[gi], (((1,), (1,)), ((), ())), precision=HI,
                                   preferred_element_type=F32)
            dwin_ref[:, cols] = dmix / _pool_counts(i, tp, w)

    tile = pl.BlockSpec((tp, D_POOL), lambda i: (i, 0))
    pw = pl.BlockSpec((4, POOL_GROUP, POOL_GROUP), lambda i: (0, 0, 0))
    ps = pl.BlockSpec((1, D_POOL), lambda i: (0, 0))
    return pl.pallas_call(
        body, name="pool_bwd1", grid=(s // tp,),
        out_shape=(jax.ShapeDtypeStruct((s, D_POOL), F32), jax.ShapeDtypeStruct((s, D_POOL), F32),
                   jax.ShapeDtypeStruct((4, POOL_GROUP, POOL_GROUP), F32), jax.ShapeDtypeStruct((1, D_POOL), F32)),
        in_specs=[tile, pl.BlockSpec((tp, D_POOL), lambda i: (jnp.maximum(i - 1, 0), 0)),
                  pl.BlockSpec((tp, D_POOL), lambda i: (i, 1)), pw, ps, tile],
        out_specs=(tile, tile, pw, ps),
        compiler_params=_cp(("arbitrary",)),
    )(proj_main, proj_main, proj_main, pool_w, pool_scale, dyp)


def _pool_bwd2(dwin, tp):
    s = dwin.shape[0]
    n = s // tp

    def body(d_ref, dn_ref, du_ref):
        i = pl.program_id(0)
        ext = jnp.concatenate([d_ref[...], _next_head(dn_ref, i, n)], axis=0)
        m = ext.shape[0]
        a2 = ext + pltpu.roll(ext, m - 1, 0)
        a4 = a2 + pltpu.roll(a2, m - 2, 0)
        a8 = a4 + pltpu.roll(a4, m - 4, 0)
        a16 = a8 + pltpu.roll(a8, m - 8, 0)
        for gi, (w, acc) in enumerate(zip(POOL_WINDOWS, (a2, a4, a8, a16))):
            cols = slice(gi * POOL_GROUP, (gi + 1) * POOL_GROUP)
            du_ref[:, cols] = acc[:tp, cols] - d_ref[:, cols] * _pool_counts(i, tp, w)

    tile = pl.BlockSpec((tp, D_POOL), lambda i: (i, 0))
    return pl.pallas_call(
        body, name="pool_bwd2", grid=(n,),
        out_shape=jax.ShapeDtypeStruct((s, D_POOL), F32),
        in_specs=[tile, pl.BlockSpec((tp, D_POOL), lambda i: (jnp.minimum(i + 1, n - 1), 0))],
        out_specs=tile,
        compiler_params=_cp(("parallel",)),
    )(dwin, dwin)


def _in_proj_bwd(dparts, dba, x2, dh, norm_w, w_main, w_ba, tm):
    s = x2.shape[0]
    nstep = s // tm

    def body(d0, d1, d2, d3, d4, d5, dba_ref, x_ref, dh_ref, nw_ref, wm_hbm, wb_ref,
             gx_ref, gwm_hbm, gwb_ref, gnw_ref, wm_vmem, gwm_acc, sem):
        i = pl.program_id(0)

        @pl.when(i == 0)
        def _():
            cp = pltpu.make_async_copy(wm_hbm, wm_vmem, sem)
            cp.start()
            gwm_acc[...] = jnp.zeros_like(gwm_acc)
            gwb_ref[...] = jnp.zeros_like(gwb_ref)
            gnw_ref[...] = jnp.zeros_like(gnw_ref)
            cp.wait()
        dbab = dba_ref[...].astype(BF16)
        xhat, r = _rms_hat(x_ref[...])
        nw = nw_ref[...]
        n = (xhat * nw).astype(BF16)
        gwb_ref[...] += _dot_tn_bf(n, dbab)
        dn = _dot_nt_bf(dbab, wb_ref[...])
        for cb, d in enumerate((d0, d1, d2, d3, d4, d5)):
            cols = slice(cb * D_POOL, (cb + 1) * D_POOL)
            dpart = d[...].astype(BF16)
            gwm_acc[:, cols] += _dot_tn_bf(n, dpart)
            dn = dn + _dot_nt_bf(dpart, wm_vmem[:, cols])
        gnw_ref[...] += jnp.sum(dn * xhat, axis=0, keepdims=True)
        dxh = dn * nw
        gx_ref[...] = dh_ref[...] + r * (dxh - xhat * jnp.mean(dxh * xhat, axis=-1, keepdims=True))

        @pl.when(i == nstep - 1)
        def _():
            cp = pltpu.make_async_copy(gwm_acc, gwm_hbm, sem)
            cp.start()
            cp.wait()

    part = pl.BlockSpec((tm, D_POOL), lambda i: (i, 0))
    full = pl.BlockSpec((tm, D_MODEL), lambda i: (i, 0))
    row = pl.BlockSpec((1, D_MODEL), lambda i: (0, 0))
    wb = pl.BlockSpec((D_MODEL, 128), lambda i: (0, 0))
    return pl.pallas_call(
        body, name="in_proj_bwd", grid=(nstep,),
        out_shape=(jax.ShapeDtypeStruct((s, D_MODEL), F32), jax.ShapeDtypeStruct((D_MODEL, D_MAIN), F32),
                   jax.ShapeDtypeStruct((D_MODEL, 128), F32), jax.ShapeDtypeStruct((1, D_MODEL), F32)),
        in_specs=[part] * 6 + [pl.BlockSpec((tm, 128), lambda i: (i, 0)), full, full, row,
                               pl.BlockSpec(memory_space=pl.ANY), wb],
        out_specs=(full, pl.BlockSpec(memory_space=pl.ANY), wb, row),
        scratch_shapes=[pltpu.VMEM((D_MODEL, D_MAIN), BF16), pltpu.VMEM((D_MODEL, D_MAIN), F32),
                        pltpu.SemaphoreType.DMA],
        compiler_params=_cp(("arbitrary",)),
    )(*dparts, dba, x2, dh, norm_w, w_main, w_ba)


def _adamw_math(w, g, m, v):
    m = ADAM_B1 * m + (1.0 - ADAM_B1) * g
    v = ADAM_B2 * v + (1.0 - ADAM_B2) * (g * g)
    m_hat = m / (1.0 - ADAM_B1 ** ADAM_STEP)
    v_hat = v / (1.0 - ADAM_B2 ** ADAM_STEP)
    delta = -ADAM_LR * (m_hat / (jnp.sqrt(v_hat) + ADAM_EPS) + ADAM_WD * w)
    return delta, m, v


def _adamw_sharded(params):
    k = len(params)

    def body(*refs):
        ins, outs = refs[:4 * k], refs[4 * k:]
        for p in range(k):
            w, g, m, v = (r[...] for r in ins[4 * p:4 * p + 4])
            d, nm, nv = _adamw_math(w, g, m, v)
            outs[3 * p][...] = d
            outs[3 * p + 1][...] = nm
            outs[3 * p + 2][...] = nv

    flat = [a for p in params for a in p]
    out_shape = tuple(jax.ShapeDtypeStruct(p[0].shape, F32) for p in params for _ in range(3))
    res = pl.pallas_call(body, name="adamw_sharded", out_shape=out_shape, compiler_params=_cp())(*flat)
    return [tuple(res[3 * p:3 * p + 3]) for p in range(k)]


def _adamw_replicated(gath_a, gath_b, wa, ma, va, wb, mb, vb):
    def body(ga_ref, gb_ref, wa_ref, ma_ref, va_ref, wb_ref, mb_ref, vb_ref, *outs):
        for g_ref, w_ref, m_ref, v_ref, o in ((ga_ref, wa_ref, ma_ref, va_ref, outs[:4]),
                                              (gb_ref, wb_ref, mb_ref, vb_ref, outs[4:])):
            g = g_ref[0]
            for d in range(1, N_DEV):
                g = g + g_ref[d]
            dl, nm, nv = _adamw_math(w_ref[...], g, m_ref[...], v_ref[...])
            o[0][...] = g
            o[1][...] = dl
            o[2][...] = nm
            o[3][...] = nv

    out_shape = (jax.ShapeDtypeStruct(wa.shape, F32),) * 4 + (jax.ShapeDtypeStruct(wb.shape, F32),) * 4
    return pl.pallas_call(body, name="adamw_replicated", out_shape=out_shape, compiler_params=_cp())(
        gath_a, gath_b, wa, ma, va, wb, mb, vb)


_ROW_ORDER = ("norm_w", "final_norm_w", "pool_scale", "dn_norm_w", "a_log", "dt_bias")
_ROW_LEN = {"norm_w": 1024, "final_norm_w": 1024, "pool_scale": 512, "dn_norm_w": 128, "a_log": 4, "dt_bias": 4}


def _pack_rows(rows):
    out = [jnp.pad(rows[nm].reshape(-1), (0, D_MODEL - _ROW_LEN[nm])) for nm in _ROW_ORDER]
    extra = rows.get("extra")
    out.append(jnp.zeros((D_MODEL,), F32) if extra is None else jnp.pad(extra.reshape(-1), (0, D_MODEL - 1)))
    out.append(jnp.zeros((D_MODEL,), F32))
    return jnp.stack(out, axis=0)


def _lane_row(vec4, start):
    return jnp.pad(vec4.reshape(-1), (start, 128 - start - vec4.size)).reshape(1, 128)


def kernel(x, norm_w, w_in, pool_w, pool_scale, conv_w, a_log, dt_bias, dn_norm_w, w_out, final_norm_w, loss_target, m_norm_w, m_w_in, m_pool_w, m_pool_scale, m_conv_w, m_a_log, m_dt_bias, m_dn_norm_w, m_w_out, m_final_norm_w, v_norm_w, v_w_in, v_pool_w, v_pool_scale, v_conv_w, v_a_log, v_dt_bias, v_dn_norm_w, v_w_out, v_final_norm_w):
    s = x.shape[1]
    tm = min(512, s)
    tmb = min(256, s)
    tp = min(512, s)
    x2 = x[0]
    tgt = loss_target[0]

    g_in, g_out, g_conv = _gather_weights(w_in[0], w_out[0], conv_w[0])
    w_full = g_in.transpose(1, 0, 2).reshape(D_MODEL, D_IN)
    w_main = w_full[:, :D_MAIN]
    w_ba = jnp.pad(w_full[:, D_MAIN:], ((0, 0), (0, 128 - (D_IN - D_MAIN))))
    w_out_full = g_out.reshape(D_MODEL, D_MODEL)
    conv_full = g_conv.transpose(1, 0, 2).reshape(CONV_WIDTH, 3 * D_DN)
    alog_lane = _lane_row(a_log, DN_HEADS)
    dtb_lane = _lane_row(dt_bias, DN_HEADS)
    fnw = final_norm_w.reshape(1, D_MODEL)

    proj_main, proj_ba = _in_proj(x2, norm_w, w_main, w_ba, tm)
    y_pool = _pool_fwd(proj_main, pool_w[0], pool_scale, tp)
    qn, kn, vv, gb = _dn_pre(proj_main, proj_ba, conv_full, alog_lane, dtb_lane, tp)
    y_dn, states = _dn_scan_fwd(qn, kn, vv, gb, proj_main, dn_norm_w, DN_CHUNKS_PER_STEP)

    dh, dyp, dyd, g_wout, g_fnw, loss_part = _out_proj_loss(y_pool, y_dn, x2, tgt, w_out_full, fnw, tm)
    dqn, dkn, dvv, dgb, ddz, g_dnw = _dn_scan_bwd(qn, kn, vv, gb, proj_main, dn_norm_w, states, dyd,
                                                  DN_CHUNKS_PER_STEP)
    dcq, dck, dcv, dba, g_conv_full, g_al, g_db = _dn_pre_bwd1(
        proj_main, proj_ba, conv_full, alog_lane, dtb_lane, dqn, dkn, dvv, dgb, tp)
    dq, dk, dv = _conv_bwd_x(dcq, dck, dcv, conv_full, tp)
    dzp, dwin, g_pw, g_ps = _pool_bwd1(proj_main, pool_w[0], pool_scale, dyp, tp)
    dup = _pool_bwd2(dwin, tp)
    grad_x2, g_wmain, g_wba, g_nw = _in_proj_bwd((dup, dzp, dq, dk, dv, ddz), dba, x2, dh, norm_w, w_main, w_ba, tmb)

    g_win_full = jnp.concatenate([g_wmain, g_wba[:, :D_IN - D_MAIN]], axis=1)
    p_in = g_win_full.reshape(D_MODEL, N_DEV, W_IN_SHARD).transpose(1, 0, 2)
    p_out = g_wout.reshape(N_DEV, D_MODEL // N_DEV, D_MODEL)
    p_conv = g_conv_full.reshape(CONV_WIDTH, N_DEV, 3 * D_DN // N_DEV).transpose(1, 0, 2)
    pack_a = g_pw.reshape(4 * POOL_GROUP, POOL_GROUP)
    pack_b = _pack_rows({"norm_w": g_nw, "final_norm_w": g_fnw, "pool_scale": g_ps, "dn_norm_w": g_dnw,
                         "a_log": g_al[0, DN_HEADS:2 * DN_HEADS], "dt_bias": g_db[0, DN_HEADS:2 * DN_HEADS],
                         "extra": loss_part[0, :1]})
    gr_in, gr_out, gr_conv, gath_a, gath_b = _reduce_grads(p_in, p_out, p_conv, pack_a, pack_b)

    (d_in, nm_in, nv_in), (d_out, nm_out, nv_out), (d_conv, nm_conv, nv_conv) = _adamw_sharded([
        (w_in[0], gr_in, m_w_in[0], v_w_in[0]), (w_out[0], gr_out, m_w_out[0], v_w_out[0]),
        (conv_w[0], gr_conv, m_conv_w[0], v_conv_w[0])])
    rep = {"w": dict(norm_w=norm_w, final_norm_w=final_norm_w, pool_scale=pool_scale, dn_norm_w=dn_norm_w,
                     a_log=a_log, dt_bias=dt_bias),
           "m": dict(norm_w=m_norm_w, final_norm_w=m_final_norm_w, pool_scale=m_pool_scale, dn_norm_w=m_dn_norm_w,
                     a_log=m_a_log, dt_bias=m_dt_bias),
           "v": dict(norm_w=v_norm_w, final_norm_w=v_final_norm_w, pool_scale=v_pool_scale, dn_norm_w=v_dn_norm_w,
                     a_log=v_a_log, dt_bias=v_dt_bias)}
    flat = lambda a: a.reshape(4 * POOL_GROUP, POOL_GROUP)
    res = _adamw_replicated(gath_a, gath_b, flat(pool_w), flat(m_pool_w), flat(v_pool_w),
                            _pack_rows(rep["w"]), _pack_rows(rep["m"]), _pack_rows(rep["v"]))
    ga, da, nma, nva, gbk, dbk, nmb, nvb = res
    loss = gbk[6, 0]

    def rows_of(pack):
        out = {}
        for r, nm in enumerate(_ROW_ORDER):
            out[nm] = pack[r, :_ROW_LEN[nm]].reshape(rep["w"][nm].shape)
        return out

    def group(sharded_in, sharded_out, sharded_conv, pack_a_out, pack_b_out):
        rw = rows_of(pack_b_out)
        return (rw["norm_w"], sharded_in[None], pack_a_out.reshape(pool_w.shape), rw["pool_scale"], sharded_conv[None],
                rw["a_log"], rw["dt_bias"], rw["dn_norm_w"], sharded_out[None], rw["final_norm_w"])

    return (loss, grad_x2[None],
            *group(gr_in, gr_out, gr_conv, ga, gbk),
            *group(d_in, d_out, d_conv, da, dbk),
            *group(nm_in, nm_out, nm_conv, nma, nmb),
            *group(nv_in, nv_out, nv_conv, nva, nvb))
```

```python
import functools

import jax
import jax.numpy as jnp
from jax import lax
from jax.experimental import pallas as pl
from jax.experimental.pallas import tpu as pltpu

F32 = jnp.float32
BF16 = jnp.bfloat16
HI = lax.Precision.HIGHEST
MESH = pl.DeviceIdType.MESH

D_MODEL = 1024
D_POOL = 512
D_DN = 512
POOL_WINDOWS = (2, 4, 8, 16)
POOL_GROUP = 128
DN_HEADS = 4
DN_HEAD_DIM = 128
CONV_WIDTH = 4
CHUNK = 64
NORM_EPS = 1e-6
D_IN = 3080
D_MAIN = 3072
N_DEV = 8
W_IN_SHARD = D_IN // N_DEV
HALO = 16
DN_CHUNKS_PER_STEP = 8

ADAM_LR = 0.001
ADAM_B1 = 0.9
ADAM_B2 = 0.999
ADAM_EPS = 1e-08
ADAM_WD = 0.01
ADAM_STEP = 10

VMEM_LIMIT = 56 * 1024 * 1024


def _cp(sem=None, vmem=VMEM_LIMIT):
    kw = {"vmem_limit_bytes": vmem}
    if sem is not None:
        kw["dimension_semantics"] = sem
    return pltpu.CompilerParams(**kw)


def _dot_hi(a, b):
    return jnp.dot(a, b, precision=HI, preferred_element_type=F32)


def _dot_bf(a, b):
    return jnp.dot(a.astype(BF16), b.astype(BF16), preferred_element_type=F32)


def _dot_nt_bf(a, b):
    return lax.dot_general(a.astype(BF16), b.astype(BF16), (((1,), (1,)), ((), ())), preferred_element_type=F32)


def _dot_tn_bf(a, b):
    return lax.dot_general(a.astype(BF16), b.astype(BF16), (((0,), (0,)), ((), ())), preferred_element_type=F32)


def _dot_tn_hi(a, b):
    return lax.dot_general(a, b, (((0,), (0,)), ((), ())), precision=HI, preferred_element_type=F32)


def _mm_raw(a, b, ca, cb, prec):
    off = a.ndim - 2
    dn = (((ca + off,), (cb + off,)), ((0,), (0,)) if off else ((), ()))
    if prec == "hi":
        return lax.dot_general(a, b, dn, precision=HI, preferred_element_type=F32)
    ah, bh = a.astype(BF16), b.astype(BF16)
    out = lax.dot_general(ah, bh, dn, preferred_element_type=F32)
    if prec == "x3":
        al = (a - ah.astype(F32)).astype(BF16)
        bl = (b - bh.astype(F32)).astype(BF16)
        out = out + lax.dot_general(ah, bl, dn, preferred_element_type=F32)
        out = out + lax.dot_general(al, bh, dn, preferred_element_type=F32)
    return out


@functools.partial(jax.custom_vjp, nondiff_argnums=(2, 3, 4, 5))
def _mm(a, b, ca, cb, prec, bprec):
    return _mm_raw(a, b, ca, cb, prec)


def _mm_fwd(a, b, ca, cb, prec, bprec):
    return _mm_raw(a, b, ca, cb, prec), (a, b)


def _mm_bwd(ca, cb, prec, bprec, res, dc):
    a, b = res
    da = _mm_raw(dc, b, 1, 1 - cb, bprec) if ca == 1 else _mm_raw(b, dc, 1 - cb, 1, bprec)
    db = _mm_raw(a, dc, 1 - ca, 0, bprec) if cb == 0 else _mm_raw(dc, a, 0, 1 - ca, bprec)
    return da, db


_mm.defvjp(_mm_fwd, _mm_bwd)


@functools.partial(jax.custom_vjp, nondiff_argnums=(1, 2))
def _tri_inv(a, prec, bprec):
    n = a.shape[-1]
    ii = lax.broadcasted_iota(jnp.int32, (n, n), 0)
    jj = lax.broadcasted_iota(jnp.int32, (n, n), 1)
    p = (ii == jj).astype(F32) - a
    b = _mm_raw(a, a, 1, 0, prec)
    for _ in range(4):
        pb = _mm_raw(jnp.concatenate([p, b], axis=-2), b, 1, 0, prec)
        p = p + pb[..., :n, :]
        b = pb[..., n:, :]
    return p + _mm_raw(p, b, 1, 0, prec)


def _tri_inv_fwd(a, prec, bprec):
    t = _tri_inv(a, prec, bprec)
    return t, t


def _tri_inv_bwd(prec, bprec, t, dt):
    return (-_mm_raw(_mm_raw(t, dt, 0, 0, bprec), t, 1, 1, bprec),)


_tri_inv.defvjp(_tri_inv_fwd, _tri_inv_bwd)

_DN_PREC = {"akq": ("bf16", "bf16"), "inv": ("bf16", "bf16"), "uw": ("bf16", "bf16"), "ws": ("bf16", "bf16"),
            "ov": ("bf16", "bf16"), "st": ("bf16", "bf16")}


def _silu(x):
    return x * jax.nn.sigmoid(x)


def _softplus(x):
    pos = x > 0.0
    return jnp.where(pos, x, 0.0) + jnp.log1p(jnp.exp(jnp.where(pos, -x, x)))


def _mesh_pos():
    return lax.axis_index("x"), lax.axis_index("y"), lax.axis_index("c")


def _dev_index(x, y, c):
    return 4 * x + 2 * y + c


def _all_gather_blocks(outs, send_sems, recv_sems):
    x, y, c = _mesh_pos()
    me = (x, y, c)
    sibling = (x, y, 1 - c)
    chips = [(1 - x, y), (x, 1 - y), (1 - x, 1 - y)]

    def copy(a, k, block, to):
        rows = outs[a].at[_dev_index(*block)]
        return pltpu.make_async_remote_copy(src_ref=rows, dst_ref=rows, send_sem=send_sems.at[a, k],
                                            recv_sem=recv_sems.at[a, k], device_id=to, device_id_type=MESH)

    n = len(outs)
    first = []
    for a in range(n):
        first.append(copy(a, 0, me, sibling))
        for j, chip in enumerate(chips):
            first.append(copy(a, 1 + j, me, (*chip, c)))
    for cp in first:
        cp.start()
    passed = []
    for j, chip in enumerate(chips):
        for a in range(n):
            copy(a, 1 + j, (*chip, c), me).wait_recv()
            fwd = copy(a, 4 + j, (*chip, c), sibling)
            fwd.start()
            passed.append(fwd)
    for a in range(n):
        copy(a, 0, sibling, me).wait_recv()
        for j, chip in enumerate(chips):
            copy(a, 4 + j, (*chip, 1 - c), me).wait_recv()
    for cp in first + passed:
        cp.wait_send()


def _gather_weights(w_in_blk, w_out_blk, conv_blk):
    def body(win_ref, wout_ref, conv_ref, gin_ref, gout_ref, gconv_ref, send_sems, recv_sems):
        x, y, c = _mesh_pos()
        me = _dev_index(x, y, c)
        gin_ref[me] = win_ref[...].astype(BF16)
        gout_ref[me] = wout_ref[...].astype(BF16)
        gconv_ref[me] = conv_ref[...]
        _all_gather_blocks((gin_ref, gout_ref, gconv_ref), send_sems, recv_sems)

    vm = pl.BlockSpec(memory_space=pltpu.VMEM)
    return pl.pallas_call(
        body, name="gather_weights",
        out_shape=(jax.ShapeDtypeStruct((N_DEV,) + w_in_blk.shape, BF16),
                   jax.ShapeDtypeStruct((N_DEV,) + w_out_blk.shape, BF16),
                   jax.ShapeDtypeStruct((N_DEV,) + conv_blk.shape, F32)),
        in_specs=[vm, vm, vm], out_specs=(vm, vm, vm),
        scratch_shapes=[pltpu.SemaphoreType.DMA((3, 7)), pltpu.SemaphoreType.DMA((3, 7))],
        compiler_params=_cp(),
    )(w_in_blk, w_out_blk, conv_blk)


def _reduce_grads(p_in, p_out, p_conv, pack_a, pack_b):
    big = (p_in, p_out, p_conv)

    def body(pin_ref, pout_ref, pconv_ref, pa_ref, pb_ref,
             oin_ref, oout_ref, oconv_ref, ga_ref, gb_ref,
             r1_in, r1_out, r1_conv, r2_in, r2_out, r2_conv, sb_in, sb_out, sb_conv, st_in, st_out, st_conv,
             s1_send, s1_recv, s2_send, s2_recv, ag_send, ag_recv, st_sem):
        x, y, c = _mesh_pos()
        me = (x, y, c)
        sibling = (x, y, 1 - c)
        rel = [(x, y), (1 - x, y), (x, 1 - y), (1 - x, 1 - y)]
        srcs = (pin_ref, pout_ref, pconv_ref)
        r1s = (r1_in, r1_out, r1_conv)
        r2s = (r2_in, r2_out, r2_conv)
        sbs = (sb_in, sb_out, sb_conv)
        sts = (st_in, st_out, st_conv)
        outs = (oin_ref, oout_ref, oconv_ref)

        ga_ref[_dev_index(*me)] = pa_ref[...]
        gb_ref[_dev_index(*me)] = pb_ref[...]

        def p1(a, r, to):
            return pltpu.make_async_remote_copy(
                src_ref=srcs[a].at[_dev_index(*rel[r], 1 - c)], dst_ref=r1s[a].at[r],
                send_sem=s1_send.at[a, r], recv_sem=s1_recv.at[a, r], device_id=to, device_id_type=MESH)

        def p2(a, r, to):
            return pltpu.make_async_remote_copy(
                src_ref=sbs[a].at[r - 1], dst_ref=r2s[a].at[r - 1],
                send_sem=s2_send.at[a, r - 1], recv_sem=s2_recv.at[a, r - 1], device_id=to, device_id_type=MESH)

        def stage(a, r):
            return pltpu.make_async_copy(srcs[a].at[_dev_index(*rel[r], c)], sts[a].at[r % 2], st_sem.at[a, r % 2])

        sends1 = [p1(a, r, sibling) for a in range(3) for r in range(4)]
        for cp in sends1:
            cp.start()
        sends2 = []
        for a in range(3):
            stage(a, 1).start()
            for r in (1, 2, 3, 0):
                nxt = {1: 2, 2: 3, 3: 0, 0: None}[r]
                if nxt is not None:
                    stage(a, nxt).start()
                stage(a, r).wait()
                p1(a, r, me).wait_recv()
                chip_sum = r1s[a][r] + sts[a][r % 2]
                if r == 0:
                    r1s[a][0] = chip_sum
                else:
                    sbs[a][r - 1] = chip_sum.astype(BF16)
                    cp = p2(a, r, (*rel[r], c))
                    cp.start()
                    sends2.append(cp)
        _all_gather_blocks((ga_ref, gb_ref), ag_send, ag_recv)
        for a in range(3):
            for r in (1, 2, 3):
                p2(a, r, me).wait_recv()
            outs[a][...] = ((r1s[a][0] + r2s[a][0].astype(F32)) + r2s[a][1].astype(F32)) + r2s[a][2].astype(F32)
        for cp in sends1 + sends2:
            cp.wait_send()

    vm = pl.BlockSpec(memory_space=pltpu.VMEM)
    hbm = pl.BlockSpec(memory_space=pl.ANY)
    blk = [p.shape[1:] for p in big]
    scratch = ([pltpu.VMEM((4,) + b, F32) for b in blk] + [pltpu.VMEM((3,) + b, BF16) for b in blk]
               + [pltpu.VMEM((3,) + b, BF16) for b in blk] + [pltpu.VMEM((2,) + b, F32) for b in blk]
               + [pltpu.SemaphoreType.DMA((3, 4)), pltpu.SemaphoreType.DMA((3, 4)),
                  pltpu.SemaphoreType.DMA((3, 3)), pltpu.SemaphoreType.DMA((3, 3)),
                  pltpu.SemaphoreType.DMA((2, 7)), pltpu.SemaphoreType.DMA((2, 7)),
                  pltpu.SemaphoreType.DMA((3, 2))])
    return pl.pallas_call(
        body, name="reduce_grads",
        out_shape=tuple(jax.ShapeDtypeStruct(b, F32) for b in blk)
        + (jax.ShapeDtypeStruct((N_DEV,) + pack_a.shape, F32), jax.ShapeDtypeStruct((N_DEV,) + pack_b.shape, F32)),
        in_specs=[hbm, hbm, hbm, vm, vm], out_specs=(vm, vm, vm, vm, vm),
        scratch_shapes=scratch,
        compiler_params=_cp(),
    )(p_in, p_out, p_conv, pack_a, pack_b)


def _rms_hat(xf):
    r = lax.rsqrt(jnp.mean(xf * xf, axis=-1, keepdims=True) + NORM_EPS)
    return xf * r, r


def _in_proj(x2, norm_w, w_main, w_ba, tm):
    s = x2.shape[0]

    def body(x_ref, nw_ref, wm_hbm, wb_ref, pm_ref, pb_ref, wm_vmem, sem):
        @pl.when(pl.program_id(0) == 0)
        def _():
            cp = pltpu.make_async_copy(wm_hbm, wm_vmem, sem)
            cp.start()
            cp.wait()
        xhat, _ = _rms_hat(x_ref[...])
        n = (xhat * nw_ref[...]).astype(BF16)
        pm_ref[...] = jnp.dot(n, wm_vmem[...], preferred_element_type=F32)
        pb_ref[...] = jnp.dot(n, wb_ref[...], preferred_element_type=F32)

    return pl.pallas_call(
        body, name="in_proj", grid=(s // tm,),
        out_shape=(jax.ShapeDtypeStruct((s, D_MAIN), F32), jax.ShapeDtypeStruct((s, 128), F32)),
        in_specs=[pl.BlockSpec((tm, D_MODEL), lambda i: (i, 0)),
                  pl.BlockSpec((1, D_MODEL), lambda i: (0, 0)),
                  pl.BlockSpec(memory_space=pl.ANY),
                  pl.BlockSpec((D_MODEL, 128), lambda i: (0, 0))],
        out_specs=(pl.BlockSpec((tm, D_MAIN), lambda i: (i, 0)), pl.BlockSpec((tm, 128), lambda i: (i, 0))),
        scratch_shapes=[pltpu.VMEM((D_MODEL, D_MAIN), BF16), pltpu.SemaphoreType.DMA],
        compiler_params=_cp(("arbitrary",)),
    )(x2, norm_w, w_main, w_ba)


def _shift_down(cur, prev_tail, s):
    ext = jnp.concatenate([prev_tail, cur], axis=0)
    return pltpu.roll(ext, s, 0)[HALO:, :]


def _shift_up(cur, next_head, s):
    ext = jnp.concatenate([cur, next_head], axis=0)
    n = ext.shape[0]
    return pltpu.roll(ext, n - s, 0)[:cur.shape[0], :]


def _pool_counts(i, tp, w):
    t = i * tp + lax.broadcasted_iota(jnp.int32, (tp, 1), 0)
    return jnp.minimum(t + 1, w).astype(F32)


def _pool_mix(u, u_prev_tail, i, tp):
    ext = jnp.concatenate([u_prev_tail, u], axis=0)
    w2 = ext + pltpu.roll(ext, 1, 0)
    w4 = w2 + pltpu.roll(w2, 2, 0)
    w8 = w4 + pltpu.roll(w4, 4, 0)
    w16 = w8 + pltpu.roll(w8, 8, 0)
    mixes = []
    for gi, (w, win) in enumerate(zip(POOL_WINDOWS, (w2, w4, w8, w16))):
        cols = slice(gi * POOL_GROUP, (gi + 1) * POOL_GROUP)
        mixes.append(win[HALO:, cols] / _pool_counts(i, tp, w) - u[:, cols])
    return mixes


def _prev_tail(ref, i):
    tail = ref[ref.shape[0] - HALO:, :]
    return jnp.where(i > 0, tail, 0.0)


def _next_head(ref, i, n):
    head = ref[:HALO, :]
    return jnp.where(i < n - 1, head, 0.0)


def _pool_fwd(proj_main, pool_w, pool_scale, tp):
    s = proj_main.shape[0]

    def body(u_ref, up_ref, z_ref, pw_ref, ps_ref, y_ref):
        i = pl.program_id(0)
        u = u_ref[...]
        mixes = _pool_mix(u, _prev_tail(up_ref, i), i, tp)
        gate = ps_ref[...] * _silu(z_ref[...])
        for gi in range(4):
            cols = slice(gi * POOL_GROUP, (gi + 1) * POOL_GROUP)
            y_ref[:, cols] = _dot_hi(mixes[gi], pw_ref[gi]) * gate[:, cols]

    return pl.pallas_call(
        body, name="pool_fwd", grid=(s // tp,),
        out_shape=jax.ShapeDtypeStruct((s, D_POOL), F32),
        in_specs=[pl.BlockSpec((tp, D_POOL), lambda i: (i, 0)),
                  pl.BlockSpec((tp, D_POOL), lambda i: (jnp.maximum(i - 1, 0), 0)),
                  pl.BlockSpec((tp, D_POOL), lambda i: (i, 1)),
                  pl.BlockSpec((4, POOL_GROUP, POOL_GROUP), lambda i: (0, 0, 0)),
                  pl.BlockSpec((1, D_POOL), lambda i: (0, 0))],
        out_specs=pl.BlockSpec((tp, D_POOL), lambda i: (i, 0)),
        compiler_params=_cp(("parallel",)),
    )(proj_main, proj_main, proj_main, pool_w, pool_scale)


def _conv_fwd(cur, prev_tail, w4):
    ext = jnp.concatenate([prev_tail, cur], axis=0)
    y = ext * w4[CONV_WIDTH - 1:CONV_WIDTH, :]
    for sft in range(1, CONV_WIDTH):
        y = y + pltpu.roll(ext, sft, 0) * w4[CONV_WIDTH - 1 - sft:CONV_WIDTH - sft, :]
    return y[HALO:, :]


def _l2n_heads(t):
    parts = []
    for h in range(DN_HEADS):
        th = t[:, h * DN_HEAD_DIM:(h + 1) * DN_HEAD_DIM]
        parts.append(th * lax.rsqrt(jnp.sum(th * th, axis=-1, keepdims=True) + NORM_EPS))
    return jnp.concatenate(parts, axis=1)


def _post_conv(yq, yk, yv):
    return _l2n_heads(_silu(yq)), _l2n_heads(_silu(yk)), _silu(yv)


def _gates(ba, alog_lane, dtb_lane):
    lane = lax.broadcasted_iota(jnp.int32, ba.shape, 1)
    beta = jax.nn.sigmoid(ba)
    g = -jnp.exp(alog_lane) * _softplus(ba + dtb_lane)
    return jnp.where(lane < DN_HEADS, beta, jnp.where(lane < 2 * DN_HEADS, g, 0.0))


def _qkv_specs(tp, which):
    def spec(col, n=None):
        if which == 0:
            return pl.BlockSpec((tp, D_DN), lambda i: (i, col))
        if which < 0:
            return pl.BlockSpec((tp, D_DN), lambda i: (jnp.maximum(i - 1, 0), col))
        return pl.BlockSpec((tp, D_DN), lambda i: (jnp.minimum(i + 1, n - 1), col))
    return spec


def _dn_pre(proj_main, proj_ba, conv_full, alog_lane, dtb_lane, tp):
    s = proj_main.shape[0]

    def body(q_ref, k_ref, v_ref, qp_ref, kp_ref, vp_ref, cw_ref, ba_ref, al_ref, db_ref,
             qn_ref, kn_ref, vv_ref, gb_ref):
        i = pl.program_id(0)
        ys = []
        for j, (cur, prev) in enumerate(((q_ref, qp_ref), (k_ref, kp_ref), (v_ref, vp_ref))):
            ys.append(_conv_fwd(cur[...], _prev_tail(prev, i), cw_ref[:, j * D_DN:(j + 1) * D_DN]))
        qn, kn, vv = _post_conv(*ys)
        qn_ref[...] = qn
        kn_ref[...] = kn
        vv_ref[...] = vv
        gb_ref[...] = _gates(ba_ref[...], al_ref[...], db_ref[...])

    cur, prev = _qkv_specs(tp, 0), _qkv_specs(tp, -1)
    row = pl.BlockSpec((1, 128), lambda i: (0, 0))
    tile = pl.BlockSpec((tp, D_DN), lambda i: (i, 0))
    return pl.pallas_call(
        body, name="dn_pre", grid=(s // tp,),
        out_shape=(jax.ShapeDtypeStruct((s, D_DN), F32),) * 3 + (jax.ShapeDtypeStruct((s, 128), F32),),
        in_specs=[cur(2), cur(3), cur(4), prev(2), prev(3), prev(4),
                  pl.BlockSpec((CONV_WIDTH, 3 * D_DN), lambda i: (0, 0)),
                  pl.BlockSpec((tp, 128), lambda i: (i, 0)), row, row],
        out_specs=(tile, tile, tile, pl.BlockSpec((tp, 128), lambda i: (i, 0))),
        compiler_params=_cp(("parallel",)),
    )(proj_main, proj_main, proj_main, proj_main, proj_main, proj_main, conv_full, proj_ba, alog_lane, dtb_lane)


def _dn_block(q, k, v, gcol, bcol, state, dz, nw):
    nb, n, d = q.shape
    ii = lax.broadcasted_iota(jnp.int32, (n, n), 0)
    jj = lax.broadcasted_iota(jnp.int32, (n, n), 1)
    lower = ii >= jj
    eye = (ii == jj).astype(F32)
    g_row = jnp.sum(eye * gcol, axis=1, keepdims=True)
    gc_col = jnp.sum(jnp.where(lower, g_row, 0.0), axis=2, keepdims=True)
    gc_row = jnp.sum(eye * gc_col, axis=1, keepdims=True)
    decay = jnp.where(lower, jnp.exp(jnp.where(lower, gc_col - gc_row, 0.0)), 0.0)
    kb = k * bcol
    vb = v * bcol
    qs = q * (DN_HEAD_DIM ** -0.5)
    egc = jnp.exp(gc_col)
    akq = _mm(jnp.concatenate([kb, qs], axis=1), k, 1, 1, *_DN_PREC["akq"])
    a = jnp.where(ii > jj, akq[:, :n] * decay, 0.0)
    qk = akq[:, n:] * decay
    t = _tri_inv(a, *_DN_PREC["inv"])
    uw = _mm(t, jnp.concatenate([vb, kb * egc], axis=2), 1, 0, *_DN_PREC["uw"])
    wq = jnp.concatenate([uw[:, :, d:], qs * egc], axis=1)
    g_last = gc_col[:, n - 1:n, :]
    k_dec = k * jnp.exp(g_last - gc_col)
    e_last = jnp.exp(g_last)
    os_ = []
    for c in range(nb // DN_HEADS):
        sl = slice(c * DN_HEADS, (c + 1) * DN_HEADS)
        ws = _mm(wq[sl], state, 1, 0, *_DN_PREC["ws"])
        v_new = uw[sl, :, :d] - ws[:, :n]
        os_.append(ws[:, n:] + _mm(qk[sl], v_new, 1, 0, *_DN_PREC["ov"]))
        state = state * e_last[sl] + _mm(k_dec[sl], v_new, 0, 0, *_DN_PREC["st"])
    o = jnp.concatenate(os_, axis=0)
    y = o * lax.rsqrt(jnp.mean(o * o, axis=-1, keepdims=True) + NORM_EPS) * nw * _silu(dz)
    return y, state


def _dn_block_args(gc, q_ref, k_ref, v_ref, gb_ref, dz_ref):
    qs, ks, vs, gs, bs, zs = [], [], [], [], [], []
    for cc in range(gc):
        r = slice(cc * CHUNK, (cc + 1) * CHUNK)
        gbv = gb_ref[r, :]
        for h in range(DN_HEADS):
            cols = slice(h * DN_HEAD_DIM, (h + 1) * DN_HEAD_DIM)
            qs.append(q_ref[r, cols])
            ks.append(k_ref[r, cols])
            vs.append(v_ref[r, cols])
            zs.append(dz_ref[r, cols])
            gs.append(gbv[:, DN_HEADS + h:DN_HEADS + h + 1])
            bs.append(gbv[:, h:h + 1])
    return tuple(jnp.stack(t, axis=0) for t in (qs, ks, vs, gs, bs, zs))


def _dn_scan_fwd(qn, kn, vv, gb, proj_main, dn_norm_w, gc):
    s = qn.shape[0]
    nchunk = s // CHUNK
    rows = gc * CHUNK

    def body(q_ref, k_ref, v_ref, gb_ref, dz_ref, nw_ref, y_ref, ss_ref, state):
        @pl.when(pl.program_id(0) == 0)
        def _():
            state[...] = jnp.zeros_like(state)
        q, k, v, gcol, bcol, dz = _dn_block_args(gc, q_ref, k_ref, v_ref, gb_ref, dz_ref)
        st = state[...]
        ss_ref[0] = st
        y, new = _dn_block(q, k, v, gcol, bcol, st, dz, nw_ref[...])
        state[...] = new
        for cc in range(gc):
            for h in range(DN_HEADS):
                y_ref[cc * CHUNK:(cc + 1) * CHUNK, h * DN_HEAD_DIM:(h + 1) * DN_HEAD_DIM] = y[cc * DN_HEADS + h]

    tile = pl.BlockSpec((rows, D_DN), lambda i: (i, 0))
    return pl.pallas_call(
        body, name="dn_scan_fwd", grid=(nchunk // gc,),
        out_shape=(jax.ShapeDtypeStruct((s, D_DN), F32),
                   jax.ShapeDtypeStruct((nchunk // gc, DN_HEADS, DN_HEAD_DIM, DN_HEAD_DIM), F32)),
        in_specs=[tile, tile, tile, pl.BlockSpec((rows, 128), lambda i: (i, 0)),
                  pl.BlockSpec((rows, D_DN), lambda i: (i, 5)), pl.BlockSpec((1, 128), lambda i: (0, 0))],
        out_specs=(tile, pl.BlockSpec((1, DN_HEADS, DN_HEAD_DIM, DN_HEAD_DIM), lambda i: (i, 0, 0, 0))),
        scratch_shapes=[pltpu.VMEM((DN_HEADS, DN_HEAD_DIM, DN_HEAD_DIM), F32)],
        compiler_params=_cp(("arbitrary",)),
    )(qn, kn, vv, gb, proj_main, dn_norm_w)


def _out_proj_loss(y_pool, y_dn, x2, tgt, w_out_full, fnw, tm):
    s = x2.shape[0]

    def body(yp_ref, yd_ref, x_ref, t_ref, wo_ref, fw_ref,
             dh_ref, dyp_ref, dyd_ref, gwo_ref, gfw_ref, loss_ref):
        @pl.when(pl.program_id(0) == 0)
        def _():
            gwo_ref[...] = jnp.zeros_like(gwo_ref)
            gfw_ref[...] = jnp.zeros_like(gfw_ref)
            loss_ref[...] = jnp.zeros_like(loss_ref)
        y = jnp.concatenate([yp_ref[...], yd_ref[...]], axis=1).astype(BF16)
        wo = wo_ref[...]
        h = x_ref[...] + jnp.dot(y, wo, preferred_element_type=F32)
        hn, r = _rms_hat(h)
        fw = fw_ref[...]
        err = hn * fw - t_ref[...]
        loss_ref[...] += 0.5 * jnp.sum(jnp.sum(err * err, axis=-1, keepdims=True) / D_MODEL, axis=0, keepdims=True)
        dout = err / D_MODEL
        gfw_ref[...] += jnp.sum(dout * hn, axis=0, keepdims=True)
        dhn = dout * fw
        dh = r * (dhn - hn * jnp.mean(dhn * hn, axis=-1, keepdims=True))
        dh_ref[...] = dh
        dhb = dh.astype(BF16)
        dy = _dot_nt_bf(dhb, wo)
        dyp_ref[...] = dy[:, :D_POOL]
        dyd_ref[...] = dy[:, D_POOL:]
        gwo_ref[...] += _dot_tn_bf(y, dhb)

    half = pl.BlockSpec((tm, D_POOL), lambda i: (i, 0))
    full = pl.BlockSpec((tm, D_MODEL), lambda i: (i, 0))
    return pl.pallas_call(
        body, name="out_proj_loss", grid=(s // tm,),
        out_shape=(jax.ShapeDtypeStruct((s, D_MODEL), F32), jax.ShapeDtypeStruct((s, D_POOL), F32),
                   jax.ShapeDtypeStruct((s, D_DN), F32), jax.ShapeDtypeStruct((D_MODEL, D_MODEL), F32),
                   jax.ShapeDtypeStruct((1, D_MODEL), F32), jax.ShapeDtypeStruct((1, 128), F32)),
        in_specs=[half, half, full, full, pl.BlockSpec((D_MODEL, D_MODEL), lambda i: (0, 0)),
                  pl.BlockSpec((1, D_MODEL), lambda i: (0, 0))],
        out_specs=(full, half, half, pl.BlockSpec((D_MODEL, D_MODEL), lambda i: (0, 0)),
                   pl.BlockSpec((1, D_MODEL), lambda i: (0, 0)), pl.BlockSpec((1, 128), lambda i: (0, 0))),
        compiler_params=_cp(("arbitrary",)),
    )(y_pool, y_dn, x2, tgt, w_out_full, fnw)


def _dn_scan_bwd(qn, kn, vv, gb, proj_main, dn_norm_w, states, dy_dn, gc):
    s = qn.shape[0]
    nchunk = s // CHUNK
    nstep = nchunk // gc
    rows = gc * CHUNK

    def body(q_ref, k_ref, v_ref, gb_ref, dz_ref, nw_ref, ss_ref, dy_ref,
             dq_ref, dk_ref, dv_ref, dgb_ref, ddz_ref, dnw_ref, dstate):
        @pl.when(pl.program_id(0) == 0)
        def _():
            dstate[...] = jnp.zeros_like(dstate)
            dnw_ref[...] = jnp.zeros_like(dnw_ref)
        lane = lax.broadcasted_iota(jnp.int32, (CHUNK, 128), 1)
        q, k, v, gcol, bcol, dz = _dn_block_args(gc, q_ref, k_ref, v_ref, gb_ref, dz_ref)
        dy = jnp.stack([dy_ref[cc * CHUNK:(cc + 1) * CHUNK, h * DN_HEAD_DIM:(h + 1) * DN_HEAD_DIM]
                        for cc in range(gc) for h in range(DN_HEADS)], axis=0)
        _, vjp = jax.vjp(_dn_block, q, k, v, gcol, bcol, ss_ref[0], dz, nw_ref[...])
        dq, dk, dv, dg, db, dst, ddz, dnw = vjp((dy, dstate[...]))
        dstate[...] = dst
        dnw_ref[...] += dnw
        for cc in range(gc):
            r = slice(cc * CHUNK, (cc + 1) * CHUNK)
            dgb = jnp.zeros((CHUNK, 128), F32)
            for h in range(DN_HEADS):
                b = cc * DN_HEADS + h
                cols = slice(h * DN_HEAD_DIM, (h + 1) * DN_HEAD_DIM)
                for ref, val in zip((dq_ref, dk_ref, dv_ref, ddz_ref), (dq, dk, dv, ddz)):
                    ref[r, cols] = val[b]
                dgb = dgb + jnp.where(lane == h, db[b], 0.0) + jnp.where(lane == DN_HEADS + h, dg[b], 0.0)
            dgb_ref[r, :] = dgb

    rev = lambda i: (nstep - 1 - i, 0)
    tile = pl.BlockSpec((rows, D_DN), rev)
    lanes = pl.BlockSpec((rows, 128), rev)
    return pl.pallas_call(
        body, name="dn_scan_bwd", grid=(nstep,),
        out_shape=(jax.ShapeDtypeStruct((s, D_DN), F32),) * 3
        + (jax.ShapeDtypeStruct((s, 128), F32), jax.ShapeDtypeStruct((s, D_DN), F32),
           jax.ShapeDtypeStruct((1, 128), F32)),
        in_specs=[tile, tile, tile, lanes, pl.BlockSpec((rows, D_DN), lambda i: (nstep - 1 - i, 5)),
                  pl.BlockSpec((1, 128), lambda i: (0, 0)),
                  pl.BlockSpec((1, DN_HEADS, DN_HEAD_DIM, DN_HEAD_DIM), lambda i: (nstep - 1 - i, 0, 0, 0)), tile],
        out_specs=(tile, tile, tile, lanes, tile, pl.BlockSpec((1, 128), lambda i: (0, 0))),
        scratch_shapes=[pltpu.VMEM((DN_HEADS, DN_HEAD_DIM, DN_HEAD_DIM), F32)],
        compiler_params=_cp(("arbitrary",)),
    )(qn, kn, vv, gb, proj_main, dn_norm_w, states, dy_dn)


def _dn_pre_bwd1(proj_main, proj_ba, conv_full, alog_lane, dtb_lane, dqn, dkn, dvv, dgb, tp):
    s = proj_main.shape[0]

    def body(q_ref, k_ref, v_ref, qp_ref, kp_ref, vp_ref, cw_ref, ba_ref, al_ref, db_ref,
             dqn_ref, dkn_ref, dvv_ref, dgb_ref,
             dcq_ref, dck_ref, dcv_ref, dba_ref, dcw_ref, dal_ref, ddb_ref):
        i = pl.program_id(0)

        @pl.when(i == 0)
        def _():
            dcw_ref[...] = jnp.zeros_like(dcw_ref)
            dal_ref[...] = jnp.zeros_like(dal_ref)
            ddb_ref[...] = jnp.zeros_like(ddb_ref)
        curs = (q_ref[...], k_ref[...], v_ref[...])
        tails = (_prev_tail(qp_ref, i), _prev_tail(kp_ref, i), _prev_tail(vp_ref, i))
        ys = [_conv_fwd(curs[j], tails[j], cw_ref[:, j * D_DN:(j + 1) * D_DN]) for j in range(3)]
        _, vjp = jax.vjp(_post_conv, *ys)
        dys = vjp((dqn_ref[...], dkn_ref[...], dvv_ref[...]))
        for j, (dy, out) in enumerate(zip(dys, (dcq_ref, dck_ref, dcv_ref))):
            out[...] = dy
            for sft in range(CONV_WIDTH):
                xs = curs[j] if sft == 0 else _shift_down(curs[j], tails[j], sft)
                row = CONV_WIDTH - 1 - sft
                dcw_ref[row:row + 1, j * D_DN:(j + 1) * D_DN] += jnp.sum(dy * xs, axis=0, keepdims=True)
        _, gvjp = jax.vjp(_gates, ba_ref[...], al_ref[...], db_ref[...])
        dba, dal, ddb = gvjp(dgb_ref[...])
        dba_ref[...] = dba
        dal_ref[...] += dal
        ddb_ref[...] += ddb

    cur, prev = _qkv_specs(tp, 0), _qkv_specs(tp, -1)
    row = pl.BlockSpec((1, 128), lambda i: (0, 0))
    tile = pl.BlockSpec((tp, D_DN), lambda i: (i, 0))
    lanes = pl.BlockSpec((tp, 128), lambda i: (i, 0))
    cw = pl.BlockSpec((CONV_WIDTH, 3 * D_DN), lambda i: (0, 0))
    return pl.pallas_call(
        body, name="dn_pre_bwd1", grid=(s // tp,),
        out_shape=(jax.ShapeDtypeStruct((s, D_DN), F32),) * 3
        + (jax.ShapeDtypeStruct((s, 128), F32), jax.ShapeDtypeStruct((CONV_WIDTH, 3 * D_DN), F32),
           jax.ShapeDtypeStruct((1, 128), F32), jax.ShapeDtypeStruct((1, 128), F32)),
        in_specs=[cur(2), cur(3), cur(4), prev(2), prev(3), prev(4), cw, lanes, row, row, tile, tile, tile, lanes],
        out_specs=(tile, tile, tile, lanes, cw, row, row),
        compiler_params=_cp(("arbitrary",)),
    )(proj_main, proj_main, proj_main, proj_main, proj_main, proj_main, conv_full, proj_ba, alog_lane, dtb_lane,
      dqn, dkn, dvv, dgb)


def _conv_bwd_x(dcq, dck, dcv, conv_full, tp):
    s = dcq.shape[0]
    n = s // tp

    def body(q_ref, k_ref, v_ref, qn_ref, kn_ref, vn_ref, cw_ref, dq_ref, dk_ref, dv_ref):
        i = pl.program_id(0)
        for j, (cur, nxt, out) in enumerate(((q_ref, qn_ref, dq_ref), (k_ref, kn_ref, dk_ref), (v_ref, vn_ref, dv_ref))):
            w4 = cw_ref[:, j * D_DN:(j + 1) * D_DN]
            dy = cur[...]
            head = _next_head(nxt, i, n)
            dx = dy * w4[CONV_WIDTH - 1:CONV_WIDTH, :]
            for sft in range(1, CONV_WIDTH):
                dx = dx + _shift_up(dy, head, sft) * w4[CONV_WIDTH - 1 - sft:CONV_WIDTH - sft, :]
            out[...] = dx

    tile = pl.BlockSpec((tp, D_DN), lambda i: (i, 0))
    nxt = pl.BlockSpec((tp, D_DN), lambda i: (jnp.minimum(i + 1, n - 1), 0))
    return pl.pallas_call(
        body, name="conv_bwd_x", grid=(n,),
        out_shape=(jax.ShapeDtypeStruct((s, D_DN), F32),) * 3,
        in_specs=[tile, tile, tile, nxt, nxt, nxt, pl.BlockSpec((CONV_WIDTH, 3 * D_DN), lambda i: (0, 0))],
        out_specs=(tile, tile, tile),
        compiler_params=_cp(("parallel",)),
    )(dcq, dck, dcv, dcq, dck, dcv, conv_full)


def _pool_bwd1(proj_main, pool_w, pool_scale, dyp, tp):
    s = proj_main.shape[0]

    def body(u_ref, up_ref, z_ref, pw_ref, ps_ref, dy_ref, dz_ref, dwin_ref, dpw_ref, dps_ref):
        i = pl.program_id(0)

        @pl.when(i == 0)
        def _():
            dpw_ref[...] = jnp.zeros_like(dpw_ref)
            dps_ref[...] = jnp.zeros_like(dps_ref)
        u = u_ref[...]
        z = z_ref[...]
        dy = dy_ref[...]
        ps = ps_ref[...]
        mixes = _pool_mix(u, _prev_tail(up_ref, i), i, tp)
        sg = jax.nn.sigmoid(z)
        sz = z * sg
        dsz = sg * (1.0 + z * (1.0 - sg))
        for gi, w in enumerate(POOL_WINDOWS):
            cols = slice(gi * POOL_GROUP, (gi + 1) * POOL_GROUP)
            mixw = _dot_hi(mixes[gi], pw_ref[gi])
            dmixw = dy[:, cols] * ps[:, cols] * sz[:, cols]
            dps_ref[:, cols] += jnp.sum(dy[:, cols] * mixw * sz[:, cols], axis=0, keepdims=True)
            dz_ref[:, cols] = dy[:, cols] * mixw * ps[:, cols] * dsz[:, cols]
            dpw_ref[gi] += _dot_tn_hi(mixes[gi], dmixw)
            dmix = lax.dot_general(dmixw, pw_ref[gi], (((1,), (1,)), ((), ())), precision=HI,
                                   preferred_element_type=F32)
            dwin_ref[:, cols] = dmix / _pool_counts(i, tp, w)

    tile = pl.BlockSpec((tp, D_POOL), lambda i: (i, 0))
    pw = pl.BlockSpec((4, POOL_GROUP, POOL_GROUP), lambda i: (0, 0, 0))
    ps = pl.BlockSpec((1, D_POOL), lambda i: (0, 0))
    return pl.pallas_call(
        body, name="pool_bwd1", grid=(s // tp,),
        out_shape=(jax.ShapeDtypeStruct((s, D_POOL), F32), jax.ShapeDtypeStruct((s, D_POOL), F32),
                   jax.ShapeDtypeStruct((4, POOL_GROUP, POOL_GROUP), F32), jax.ShapeDtypeStruct((1, D_POOL), F32)),
        in_specs=[tile, pl.BlockSpec((tp, D_POOL), lambda i: (jnp.maximum(i - 1, 0), 0)),
                  pl.BlockSpec((tp, D_POOL), lambda i: (i, 1)), pw, ps, tile],
        out_specs=(tile, tile, pw, ps),
        compiler_params=_cp(("arbitrary",)),
    )(proj_main, proj_main, proj_main, pool_w, pool_scale, dyp)


def _pool_bwd2(dwin, tp):
    s = dwin.shape[0]
    n = s // tp

    def body(d_ref, dn_ref, du_ref):
        i = pl.program_id(0)
        ext = jnp.concatenate([d_ref[...], _next_head(dn_ref, i, n)], axis=0)
        m = ext.shape[0]
        a2 = ext + pltpu.roll(ext, m - 1, 0)
        a4 = a2 + pltpu.roll(a2, m - 2, 0)
        a8 = a4 + pltpu.roll(a4, m - 4, 0)
        a16 = a8 + pltpu.roll(a8, m - 8, 0)
        for gi, (w, acc) in enumerate(zip(POOL_WINDOWS, (a2, a4, a8, a16))):
            cols = slice(gi * POOL_GROUP, (gi + 1) * POOL_GROUP)
            du_ref[:, cols] = acc[:tp, cols] - d_ref[:, cols] * _pool_counts(i, tp, w)

    tile = pl.BlockSpec((tp, D_POOL), lambda i: (i, 0))
    return pl.pallas_call(
        body, name="pool_bwd2", grid=(n,),
        out_shape=jax.ShapeDtypeStruct((s, D_POOL), F32),
        in_specs=[tile, pl.BlockSpec((tp, D_POOL), lambda i: (jnp.minimum(i + 1, n - 1), 0))],
        out_specs=tile,
        compiler_params=_cp(("parallel",)),
    )(dwin, dwin)


def _in_proj_bwd(dparts, dba, x2, dh, norm_w, w_main, w_ba, tm):
    s = x2.shape[0]
    nstep = s // tm

    def body(d0, d1, d2, d3, d4, d5, dba_ref, x_ref, dh_ref, nw_ref, wm_hbm, wb_ref,
             gx_ref, gwm_hbm, gwb_ref, gnw_ref, wm_vmem, gwm_acc, sem):
        i = pl.program_id(0)

        @pl.when(i == 0)
        def _():
            cp = pltpu.make_async_copy(wm_hbm, wm_vmem, sem)
            cp.start()
            gwm_acc[...] = jnp.zeros_like(gwm_acc)
            gwb_ref[...] = jnp.zeros_like(gwb_ref)
            gnw_ref[...] = jnp.zeros_like(gnw_ref)
            cp.wait()
        dbab = dba_ref[...].astype(BF16)
        xhat, r = _rms_hat(x_ref[...])
        nw = nw_ref[...]
        n = (xhat * nw).astype(BF16)
        gwb_ref[...] += _dot_tn_bf(n, dbab)
        dn = _dot_nt_bf(dbab, wb_ref[...])
        for cb, d in enumerate((d0, d1, d2, d3, d4, d5)):
            cols = slice(cb * D_POOL, (cb + 1) * D_POOL)
            dpart = d[...].astype(BF16)
            gwm_acc[:, cols] += _dot_tn_bf(n, dpart)
            dn = dn + _dot_nt_bf(dpart, wm_vmem[:, cols])
        gnw_ref[...] += jnp.sum(dn * xhat, axis=0, keepdims=True)
        dxh = dn * nw
        gx_ref[...] = dh_ref[...] + r * (dxh - xhat * jnp.mean(dxh * xhat, axis=-1, keepdims=True))

        @pl.when(i == nstep - 1)
        def _():
            cp = pltpu.make_async_copy(gwm_acc, gwm_hbm, sem)
            cp.start()
            cp.wait()

    part = pl.BlockSpec((tm, D_POOL), lambda i: (i, 0))
    full = pl.BlockSpec((tm, D_MODEL), lambda i: (i, 0))
    row = pl.BlockSpec((1, D_MODEL), lambda i: (0, 0))
    wb = pl.BlockSpec((D_MODEL, 128), lambda i: (0, 0))
    return pl.pallas_call(
        body, name="in_proj_bwd", grid=(nstep,),
        out_shape=(jax.ShapeDtypeStruct((s, D_MODEL), F32), jax.ShapeDtypeStruct((D_MODEL, D_MAIN), F32),
                   jax.ShapeDtypeStruct((D_MODEL, 128), F32), jax.ShapeDtypeStruct((1, D_MODEL), F32)),
        in_specs=[part] * 6 + [pl.BlockSpec((tm, 128), lambda i: (i, 0)), full, full, row,
                               pl.BlockSpec(memory_space=pl.ANY), wb],
        out_specs=(full, pl.BlockSpec(memory_space=pl.ANY), wb, row),
        scratch_shapes=[pltpu.VMEM((D_MODEL, D_MAIN), BF16), pltpu.VMEM((D_MODEL, D_MAIN), F32),
                        pltpu.SemaphoreType.DMA],
        compiler_params=_cp(("arbitrary",)),
    )(*dparts, dba, x2, dh, norm_w, w_main, w_ba)


def _adamw_math(w, g, m, v):
    m = ADAM_B1 * m + (1.0 - ADAM_B1) * g
    v = ADAM_B2 * v + (1.0 - ADAM_B2) * (g * g)
    m_hat = m / (1.0 - ADAM_B1 ** ADAM_STEP)
    v_hat = v / (1.0 - ADAM_B2 ** ADAM_STEP)
    delta = -ADAM_LR * (m_hat / (jnp.sqrt(v_hat) + ADAM_EPS) + ADAM_WD * w)
    return delta, m, v


def _adamw_sharded(params):
    k = len(params)

    def body(*refs):
        ins, outs = refs[:4 * k], refs[4 * k:]
        for p in range(k):
            w, g, m, v = (r[...] for r in ins[4 * p:4 * p + 4])
            d, nm, nv = _adamw_math(w, g, m, v)
            outs[3 * p][...] = d
            outs[3 * p + 1][...] = nm
            outs[3 * p + 2][...] = nv

    flat = [a for p in params for a in p]
    out_shape = tuple(jax.ShapeDtypeStruct(p[0].shape, F32) for p in params for _ in range(3))
    res = pl.pallas_call(body, name="adamw_sharded", out_shape=out_shape, compiler_params=_cp())(*flat)
    return [tuple(res[3 * p:3 * p + 3]) for p in range(k)]


def _adamw_replicated(gath_a, gath_b, wa, ma, va, wb, mb, vb):
    def body(ga_ref, gb_ref, wa_ref, ma_ref, va_ref, wb_ref, mb_ref, vb_ref, *outs):
        for g_ref, w_ref, m_ref, v_ref, o in ((ga_ref, wa_ref, ma_ref, va_ref, outs[:4]),
                                              (gb_ref, wb_ref, mb_ref, vb_ref, outs[4:])):
            g = g_ref[0]
            for d in range(1, N_DEV):
                g = g + g_ref[d]
            dl, nm, nv = _adamw_math(w_ref[...], g, m_ref[...], v_ref[...])
            o[0][...] = g
            o[1][...] = dl
            o[2][...] = nm
            o[3][...] = nv

    out_shape = (jax.ShapeDtypeStruct(wa.shape, F32),) * 4 + (jax.ShapeDtypeStruct(wb.shape, F32),) * 4
    return pl.pallas_call(body, name="adamw_replicated", out_shape=out_shape, compiler_params=_cp())(
        gath_a, gath_b, wa, ma, va, wb, mb, vb)


_ROW_ORDER = ("norm_w", "final_norm_w", "pool_scale", "dn_norm_w", "a_log", "dt_bias")
_ROW_LEN = {"norm_w": 1024, "final_norm_w": 1024, "pool_scale": 512, "dn_norm_w": 128, "a_log": 4, "dt_bias": 4}


def _pack_rows(rows):
    out = [jnp.pad(rows[nm].reshape(-1), (0, D_MODEL - _ROW_LEN[nm])) for nm in _ROW_ORDER]
    extra = rows.get("extra")
    out.append(jnp.zeros((D_MODEL,), F32) if extra is None else jnp.pad(extra.reshape(-1), (0, D_MODEL - 1)))
    out.append(jnp.zeros((D_MODEL,), F32))
    return jnp.stack(out, axis=0)


def _lane_row(vec4, start):
    return jnp.pad(vec4.reshape(-1), (start, 128 - start - vec4.size)).reshape(1, 128)


def kernel(x, norm_w, w_in, pool_w, pool_scale, conv_w, a_log, dt_bias, dn_norm_w, w_out, final_norm_w, loss_target, m_norm_w, m_w_in, m_pool_w, m_pool_scale, m_conv_w, m_a_log, m_dt_bias, m_dn_norm_w, m_w_out, m_final_norm_w, v_norm_w, v_w_in, v_pool_w, v_pool_scale, v_conv_w, v_a_log, v_dt_bias, v_dn_norm_w, v_w_out, v_final_norm_w):
    s = x.shape[1]
    tm = min(512, s)
    tmb = min(256, s)
    tp = min(512, s)
    x2 = x[0]
    tgt = loss_target[0]

    g_in, g_out, g_conv = _gather_weights(w_in[0], w_out[0], conv_w[0])
    w_full = g_in.transpose(1, 0, 2).reshape(D_MODEL, D_IN)
    w_main = w_full[:, :D_MAIN]
    w_ba = jnp.pad(w_full[:, D_MAIN:], ((0, 0), (0, 128 - (D_IN - D_MAIN))))
    w_out_full = g_out.reshape(D_MODEL, D_MODEL)
    conv_full = g_conv.transpose(1, 0, 2).reshape(CONV_WIDTH, 3 * D_DN)
    alog_lane = _lane_row(a_log, DN_HEADS)
    dtb_lane = _lane_row(dt_bias, DN_HEADS)
    fnw = final_norm_w.reshape(1, D_MODEL)

    proj_main, proj_ba = _in_proj(x2, norm_w, w_main, w_ba, tm)
    y_pool = _pool_fwd(proj_main, pool_w[0], pool_scale, tp)
    qn, kn, vv, gb = _dn_pre(proj_main, proj_ba, conv_full, alog_lane, dtb_lane, tp)
    y_dn, states = _dn_scan_fwd(qn, kn, vv, gb, proj_main, dn_norm_w, DN_CHUNKS_PER_STEP)

    dh, dyp, dyd, g_wout, g_fnw, loss_part = _out_proj_loss(y_pool, y_dn, x2, tgt, w_out_full, fnw, tm)
    dqn, dkn, dvv, dgb, ddz, g_dnw = _dn_scan_bwd(qn, kn, vv, gb, proj_main, dn_norm_w, states, dyd,
                                                  DN_CHUNKS_PER_STEP)
    dcq, dck, dcv, dba, g_conv_full, g_al, g_db = _dn_pre_bwd1(
        proj_main, proj_ba, conv_full, alog_lane, dtb_lane, dqn, dkn, dvv, dgb, tp)
    dq, dk, dv = _conv_bwd_x(dcq, dck, dcv, conv_full, tp)
    dzp, dwin, g_pw, g_ps = _pool_bwd1(proj_main, pool_w[0], pool_scale, dyp, tp)
    dup = _pool_bwd2(dwin, tp)
    grad_x2, g_wmain, g_wba, g_nw = _in_proj_bwd((dup, dzp, dq, dk, dv, ddz), dba, x2, dh, norm_w, w_main, w_ba, tmb)

    g_win_full = jnp.concatenate([g_wmain, g_wba[:, :D_IN - D_MAIN]], axis=1)
    p_in = g_win_full.reshape(D_MODEL, N_DEV, W_IN_SHARD).transpose(1, 0, 2)
    p_out = g_wout.reshape(N_DEV, D_MODEL // N_DEV, D_MODEL)
    p_conv = g_conv_full.reshape(CONV_WIDTH, N_DEV, 3 * D_DN // N_DEV).transpose(1, 0, 2)
    pack_a = g_pw.reshape(4 * POOL_GROUP, POOL_GROUP)
    pack_b = _pack_rows({"norm_w": g_nw, "final_norm_w": g_fnw, "pool_scale": g_ps, "dn_norm_w": g_dnw,
                         "a_log": g_al[0, DN_HEADS:2 * DN_HEADS], "dt_bias": g_db[0, DN_HEADS:2 * DN_HEADS],
                         "extra": loss_part[0, :1]})
    gr_in, gr_out, gr_conv, gath_a, gath_b = _reduce_grads(p_in, p_out, p_conv, pack_a, pack_b)

    (d_in, nm_in, nv_in), (d_out, nm_out, nv_out), (d_conv, nm_conv, nv_conv) = _adamw_sharded([
        (w_in[0], gr_in, m_w_in[0], v_w_in[0]), (w_out[0], gr_out, m_w_out[0], v_w_out[0]),
        (conv_w[0], gr_conv, m_conv_w[0], v_conv_w[0])])
    rep = {"w": dict(norm_w=norm_w, final_norm_w=final_norm_w, pool_scale=pool_scale, dn_norm_w=dn_norm_w,
                     a_log=a_log, dt_bias=dt_bias),
           "m": dict(norm_w=m_norm_w, final_norm_w=m_final_norm_w, pool_scale=m_pool_scale, dn_norm_w=m_dn_norm_w,
                     a_log=m_a_log, dt_bias=m_dt_bias),
           "v": dict(norm_w=v_norm_w, final_norm_w=v_final_norm_w, pool_scale=v_pool_scale, dn_norm_w=v_dn_norm_w,
                     a_log=v_a_log, dt_bias=v_dt_bias)}
    flat = lambda a: a.reshape(4 * POOL_GROUP, POOL_GROUP)
    res = _adamw_replicated(gath_a, gath_b, flat(pool_w), flat(m_pool_w), flat(v_pool_w),
                            _pack_rows(rep["w"]), _pack_rows(rep["m"]), _pack_rows(rep["v"]))
    ga, da, nma, nva, gbk, dbk, nmb, nvb = res
    loss = gbk[6, 0]

    def rows_of(pack):
        out = {}
        for r, nm in enumerate(_ROW_ORDER):
            out[nm] = pack[r, :_ROW_LEN[nm]].reshape(rep["w"][nm].shape)
        return out

    def group(sharded_in, sharded_out, sharded_conv, pack_a_out, pack_b_out):
        rw = rows_of(pack_b_out)
        return (rw["norm_w"], sharded_in[None], pack_a_out.reshape(pool_w.shape), rw["pool_scale"], sharded_conv[None],
                rw["a_log"], rw["dt_bias"], rw["dn_norm_w"], sharded_out[None], rw["final_norm_w"])

    return (loss, grad_x2[None],
            *group(gr_in, gr_out, gr_conv, ga, gbk),
            *group(d_in, d_out, d_conv, da, dbk),
            *group(nm_in, nm_out, nm_conv, nma, nmb),
            *group(nv_in, nv_out, nv_conv, nva, nvb))
```

```python
import functools

import jax
import jax.numpy as jnp
from jax import lax
from jax.experimental import pallas as pl
from jax.experimental.pallas import tpu as pltpu

F32 = jnp.float32
BF16 = jnp.bfloat16
HI = lax.Precision.HIGHEST
MESH = pl.DeviceIdType.MESH

D_MODEL = 1024
D_POOL = 512
D_DN = 512
POOL_WINDOWS = (2, 4, 8, 16)
POOL_GROUP = 128
DN_HEADS = 4
DN_HEAD_DIM = 128
CONV_WIDTH = 4
CHUNK = 64
NORM_EPS = 1e-6
D_IN = 3080
D_MAIN = 3072
D_IN_PAD = D_MAIN + 128
N_DEV = 8
W_IN_SHARD = D_IN // N_DEV
HALO = 16
DN_CHUNKS_PER_STEP = 8

ADAM_LR = 0.001
ADAM_B1 = 0.9
ADAM_B2 = 0.999
ADAM_EPS = 1e-08
ADAM_WD = 0.01
ADAM_STEP = 10

VMEM_LIMIT = 56 * 1024 * 1024


def _cp(sem=None, vmem=VMEM_LIMIT):
    kw = {"vmem_limit_bytes": vmem}
    if sem is not None:
        kw["dimension_semantics"] = sem
    return pltpu.CompilerParams(**kw)


def _dot_hi(a, b):
    return jnp.dot(a, b, precision=HI, preferred_element_type=F32)


def _dot_bf(a, b):
    return jnp.dot(a.astype(BF16), b.astype(BF16), preferred_element_type=F32)


def _dot_nt_bf(a, b):
    return lax.dot_general(a.astype(BF16), b.astype(BF16), (((1,), (1,)), ((), ())), preferred_element_type=F32)


def _dot_tn_bf(a, b):
    return lax.dot_general(a.astype(BF16), b.astype(BF16), (((0,), (0,)), ((), ())), preferred_element_type=F32)


def _dot_tn_hi(a, b):
    return lax.dot_general(a, b, (((0,), (0,)), ((), ())), precision=HI, preferred_element_type=F32)


def _mm_raw(a, b, ca, cb, prec):
    off = a.ndim - 2
    dn = (((ca + off,), (cb + off,)), ((0,), (0,)) if off else ((), ()))
    if prec == "hi":
        return lax.dot_general(a, b, dn, precision=HI, preferred_element_type=F32)
    ah, bh = a.astype(BF16), b.astype(BF16)
    out = lax.dot_general(ah, bh, dn, preferred_element_type=F32)
    if prec == "x3":
        al = (a - ah.astype(F32)).astype(BF16)
        bl = (b - bh.astype(F32)).astype(BF16)
        out = out + lax.dot_general(ah, bl, dn, preferred_element_type=F32)
        out = out + lax.dot_general(al, bh, dn, preferred_element_type=F32)
    return out


@functools.partial(jax.custom_vjp, nondiff_argnums=(2, 3, 4, 5))
def _mm(a, b, ca, cb, prec, bprec):
    return _mm_raw(a, b, ca, cb, prec)


def _mm_fwd(a, b, ca, cb, prec, bprec):
    return _mm_raw(a, b, ca, cb, prec), (a, b)


def _mm_bwd(ca, cb, prec, bprec, res, dc):
    a, b = res
    da = _mm_raw(dc, b, 1, 1 - cb, bprec) if ca == 1 else _mm_raw(b, dc, 1 - cb, 1, bprec)
    db = _mm_raw(a, dc, 1 - ca, 0, bprec) if cb == 0 else _mm_raw(dc, a, 0, 1 - ca, bprec)
    return da, db


_mm.defvjp(_mm_fwd, _mm_bwd)


@functools.partial(jax.custom_vjp, nondiff_argnums=(1, 2))
def _tri_inv(a, prec, bprec):
    n = a.shape[-1]
    ii = lax.broadcasted_iota(jnp.int32, (n, n), 0)
    jj = lax.broadcasted_iota(jnp.int32, (n, n), 1)
    p = (ii == jj).astype(F32) - a
    b = _mm_raw(a, a, 1, 0, prec)
    for _ in range(4):
        pb = _mm_raw(jnp.concatenate([p, b], axis=-2), b, 1, 0, prec)
        p = p + pb[..., :n, :]
        b = pb[..., n:, :]
    return p + _mm_raw(p, b, 1, 0, prec)


def _tri_inv_fwd(a, prec, bprec):
    t = _tri_inv(a, prec, bprec)
    return t, t


def _tri_inv_bwd(prec, bprec, t, dt):
    return (-_mm_raw(_mm_raw(t, dt, 0, 0, bprec), t, 1, 1, bprec),)


_tri_inv.defvjp(_tri_inv_fwd, _tri_inv_bwd)

_DN_PREC = {"akq": ("bf16", "bf16"), "inv": ("bf16", "bf16"), "uw": ("bf16", "bf16"), "ws": ("bf16", "bf16"),
            "ov": ("bf16", "bf16"), "st": ("bf16", "bf16")}


def _silu(x):
    return x * jax.nn.sigmoid(x)


def _softplus(x):
    pos = x > 0.0
    return jnp.where(pos, x, 0.0) + jnp.log1p(jnp.exp(jnp.where(pos, -x, x)))


def _mesh_pos():
    return lax.axis_index("x"), lax.axis_index("y"), lax.axis_index("c")


def _dev_index(x, y, c):
    return 4 * x + 2 * y + c


def _all_gather_blocks(outs, send_sems, recv_sems):
    x, y, c = _mesh_pos()
    me = (x, y, c)
    sibling = (x, y, 1 - c)
    chips = [(1 - x, y), (x, 1 - y), (1 - x, 1 - y)]

    def copy(a, k, block, to):
        rows = outs[a].at[_dev_index(*block)]
        return pltpu.make_async_remote_copy(src_ref=rows, dst_ref=rows, send_sem=send_sems.at[a, k],
                                            recv_sem=recv_sems.at[a, k], device_id=to, device_id_type=MESH)

    n = len(outs)
    first = []
    for a in range(n):
        first.append(copy(a, 0, me, sibling))
        for j, chip in enumerate(chips):
            first.append(copy(a, 1 + j, me, (*chip, c)))
    for cp in first:
        cp.start()
    passed = []
    for j, chip in enumerate(chips):
        for a in range(n):
            copy(a, 1 + j, (*chip, c), me).wait_recv()
            fwd = copy(a, 4 + j, (*chip, c), sibling)
            fwd.start()
            passed.append(fwd)
    for a in range(n):
        copy(a, 0, sibling, me).wait_recv()
        for j, chip in enumerate(chips):
            copy(a, 4 + j, (*chip, 1 - c), me).wait_recv()
    for cp in first + passed:
        cp.wait_send()


def _gather_weights(w_in_blk, w_out_blk, conv_blk):
    def body(win_ref, wout_ref, conv_ref, gin_ref, gout_ref, gconv_ref, send_sems, recv_sems):
        x, y, c = _mesh_pos()
        me = _dev_index(x, y, c)
        gin_ref[me] = win_ref[...].astype(BF16)
        gout_ref[me] = wout_ref[...].astype(BF16)
        gconv_ref[me] = conv_ref[...]
        _all_gather_blocks((gin_ref, gout_ref, gconv_ref), send_sems, recv_sems)

    vm = pl.BlockSpec(memory_space=pltpu.VMEM)
    return pl.pallas_call(
        body, name="gather_weights",
        out_shape=(jax.ShapeDtypeStruct((N_DEV,) + w_in_blk.shape, BF16),
                   jax.ShapeDtypeStruct((N_DEV,) + w_out_blk.shape, BF16),
                   jax.ShapeDtypeStruct((N_DEV,) + conv_blk.shape, F32)),
        in_specs=[vm, vm, vm], out_specs=(vm, vm, vm),
        scratch_shapes=[pltpu.SemaphoreType.DMA((3, 7)), pltpu.SemaphoreType.DMA((3, 7))],
        compiler_params=_cp(),
    )(w_in_blk, w_out_blk, conv_blk)


def _reduce_grads(p_in, p_out, p_conv, pack_a, pack_b):
    big = (p_in, p_out, p_conv)

    def body(pin_ref, pout_ref, pconv_ref, pa_ref, pb_ref,
             oin_ref, oout_ref, oconv_ref, ga_ref, gb_ref,
             r1_in, r1_out, r1_conv, r2_in, r2_out, r2_conv, sb_in, sb_out, sb_conv, st_in, st_out, st_conv,
             s1_send, s1_recv, s2_send, s2_recv, ag_send, ag_recv, st_sem):
        x, y, c = _mesh_pos()
        me = (x, y, c)
        sibling = (x, y, 1 - c)
        rel = [(x, y), (1 - x, y), (x, 1 - y), (1 - x, 1 - y)]
        srcs = (pin_ref, pout_ref, pconv_ref)
        r1s = (r1_in, r1_out, r1_conv)
        r2s = (r2_in, r2_out, r2_conv)
        sbs = (sb_in, sb_out, sb_conv)
        sts = (st_in, st_out, st_conv)
        outs = (oin_ref, oout_ref, oconv_ref)

        ga_ref[_dev_index(*me)] = pa_ref[...]
        gb_ref[_dev_index(*me)] = pb_ref[...]

        def p1(a, r, to):
            return pltpu.make_async_remote_copy(
                src_ref=srcs[a].at[_dev_index(*rel[r], 1 - c)], dst_ref=r1s[a].at[r],
                send_sem=s1_send.at[a, r], recv_sem=s1_recv.at[a, r], device_id=to, device_id_type=MESH)

        def p2(a, r, to):
            return pltpu.make_async_remote_copy(
                src_ref=sbs[a].at[r - 1], dst_ref=r2s[a].at[r - 1],
                send_sem=s2_send.at[a, r - 1], recv_sem=s2_recv.at[a, r - 1], device_id=to, device_id_type=MESH)

        def stage(a, r):
            return pltpu.make_async_copy(srcs[a].at[_dev_index(*rel[r], c)], sts[a].at[r % 2], st_sem.at[a, r % 2])

        sends1 = [p1(a, r, sibling) for a in range(3) for r in range(4)]
        for cp in sends1:
            cp.start()
        sends2 = []
        for a in range(3):
            stage(a, 1).start()
            for r in (1, 2, 3, 0):
                nxt = {1: 2, 2: 3, 3: 0, 0: None}[r]
                if nxt is not None:
                    stage(a, nxt).start()
                stage(a, r).wait()
                p1(a, r, me).wait_recv()
                chip_sum = r1s[a][r] + sts[a][r % 2]
                if r == 0:
                    r1s[a][0] = chip_sum
                else:
                    sbs[a][r - 1] = chip_sum.astype(BF16)
                    cp = p2(a, r, (*rel[r], c))
                    cp.start()
                    sends2.append(cp)
        _all_gather_blocks((ga_ref, gb_ref), ag_send, ag_recv)
        for a in range(3):
            for r in (1, 2, 3):
                p2(a, r, me).wait_recv()
            outs[a][...] = ((r1s[a][0] + r2s[a][0].astype(F32)) + r2s[a][1].astype(F32)) + r2s[a][2].astype(F32)
        for cp in sends1 + sends2:
            cp.wait_send()

    vm = pl.BlockSpec(memory_space=pltpu.VMEM)
    hbm = pl.BlockSpec(memory_space=pl.ANY)
    blk = [p.shape[1:] for p in big]
    scratch = ([pltpu.VMEM((4,) + b, F32) for b in blk] + [pltpu.VMEM((3,) + b, BF16) for b in blk]
               + [pltpu.VMEM((3,) + b, BF16) for b in blk] + [pltpu.VMEM((2,) + b, F32) for b in blk]
               + [pltpu.SemaphoreType.DMA((3, 4)), pltpu.SemaphoreType.DMA((3, 4)),
                  pltpu.SemaphoreType.DMA((3, 3)), pltpu.SemaphoreType.DMA((3, 3)),
                  pltpu.SemaphoreType.DMA((2, 7)), pltpu.SemaphoreType.DMA((2, 7)),
                  pltpu.SemaphoreType.DMA((3, 2))])
    return pl.pallas_call(
        body, name="reduce_grads",
        out_shape=tuple(jax.ShapeDtypeStruct(b, F32) for b in blk)
        + (jax.ShapeDtypeStruct((N_DEV,) + pack_a.shape, F32), jax.ShapeDtypeStruct((N_DEV,) + pack_b.shape, F32)),
        in_specs=[hbm, hbm, hbm, vm, vm], out_specs=(vm, vm, vm, vm, vm),
        scratch_shapes=scratch,
        compiler_params=_cp(),
    )(p_in, p_out, p_conv, pack_a, pack_b)


def _rms_hat(xf):
    r = lax.rsqrt(jnp.mean(xf * xf, axis=-1, keepdims=True) + NORM_EPS)
    return xf * r, r


def _in_proj(x2, norm_w, g_in, tm):
    s = x2.shape[0]

    def body(x_ref, nw_ref, g_hbm, pm_ref, pb_ref, wt_hbm, g_vmem, wt_vmem, sem):
        @pl.when(pl.program_id(0) == 0)
        def _():
            cp = pltpu.make_async_copy(g_hbm, g_vmem, sem)
            cp.start()
            wt_vmem[D_MAIN:, :] = jnp.zeros((D_IN_PAD - D_MAIN, D_MODEL), BF16)
            cp.wait()
            for d in range(N_DEV):
                wt_vmem[W_IN_SHARD * d:W_IN_SHARD * (d + 1), :] = g_vmem[d]
            out = pltpu.make_async_copy(wt_vmem, wt_hbm, sem)
            out.start()
            out.wait()
        xhat, _ = _rms_hat(x_ref[...])
        n = (xhat * nw_ref[...]).astype(BF16)
        pm_ref[...] = _dot_nt_bf(n, wt_vmem[:D_MAIN, :])
        pb_ref[...] = _dot_nt_bf(n, wt_vmem[D_MAIN:, :])

    return pl.pallas_call(
        body, name="in_proj", grid=(s // tm,),
        out_shape=(jax.ShapeDtypeStruct((s, D_MAIN), F32), jax.ShapeDtypeStruct((s, 128), F32),
                   jax.ShapeDtypeStruct((D_IN_PAD, D_MODEL), BF16)),
        in_specs=[pl.BlockSpec((tm, D_MODEL), lambda i: (i, 0)),
                  pl.BlockSpec((1, D_MODEL), lambda i: (0, 0)),
                  pl.BlockSpec(memory_space=pl.ANY)],
        out_specs=(pl.BlockSpec((tm, D_MAIN), lambda i: (i, 0)), pl.BlockSpec((tm, 128), lambda i: (i, 0)),
                   pl.BlockSpec(memory_space=pl.ANY)),
        scratch_shapes=[pltpu.VMEM((N_DEV, W_IN_SHARD, D_MODEL), BF16), pltpu.VMEM((D_IN_PAD, D_MODEL), BF16),
                        pltpu.SemaphoreType.DMA],
        compiler_params=_cp(("arbitrary",)),
    )(x2, norm_w, g_in)


def _shift_down(cur, prev_tail, s):
    ext = jnp.concatenate([prev_tail, cur], axis=0)
    return pltpu.roll(ext, s, 0)[HALO:, :]


def _shift_up(cur, next_head, s):
    ext = jnp.concatenate([cur, next_head], axis=0)
    n = ext.shape[0]
    return pltpu.roll(ext, n - s, 0)[:cur.shape[0], :]


def _pool_counts(i, tp, w):
    t = i * tp + lax.broadcasted_iota(jnp.int32, (tp, 1), 0)
    return jnp.minimum(t + 1, w).astype(F32)


def _pool_mix(u, u_prev_tail, i, tp):
    ext = jnp.concatenate([u_prev_tail, u], axis=0)
    w2 = ext + pltpu.roll(ext, 1, 0)
    w4 = w2 + pltpu.roll(w2, 2, 0)
    w8 = w4 + pltpu.roll(w4, 4, 0)
    w16 = w8 + pltpu.roll(w8, 8, 0)
    mixes = []
    for gi, (w, win) in enumerate(zip(POOL_WINDOWS, (w2, w4, w8, w16))):
        cols = slice(gi * POOL_GROUP, (gi + 1) * POOL_GROUP)
        mixes.append(win[HALO:, cols] / _pool_counts(i, tp, w) - u[:, cols])
    return mixes


def _prev_tail(ref, i):
    tail = ref[ref.shape[0] - HALO:, :]
    return jnp.where(i > 0, tail, 0.0)


def _next_head(ref, i, n):
    head = ref[:HALO, :]
    return jnp.where(i < n - 1, head, 0.0)


def _pool_fwd(proj_main, pool_w, pool_scale, tp):
    s = proj_main.shape[0]

    def body(u_ref, up_ref, z_ref, pw_ref, ps_ref, y_ref):
        i = pl.program_id(0)
        u = u_ref[...]
        mixes = _pool_mix(u, _prev_tail(up_ref, i), i, tp)
        gate = ps_ref[...] * _silu(z_ref[...])
        for gi in range(4):
            cols = slice(gi * POOL_GROUP, (gi + 1) * POOL_GROUP)
            y_ref[:, cols] = _dot_hi(mixes[gi], pw_ref[gi]) * gate[:, cols]

    return pl.pallas_call(
        body, name="pool_fwd", grid=(s // tp,),
        out_shape=jax.ShapeDtypeStruct((s, D_POOL), F32),
        in_specs=[pl.BlockSpec((tp, D_POOL), lambda i: (i, 0)),
                  pl.BlockSpec((tp, D_POOL), lambda i: (jnp.maximum(i - 1, 0), 0)),
                  pl.BlockSpec((tp, D_POOL), lambda i: (i, 1)),
                  pl.BlockSpec((4, POOL_GROUP, POOL_GROUP), lambda i: (0, 0, 0)),
                  pl.BlockSpec((1, D_POOL), lambda i: (0, 0))],
        out_specs=pl.BlockSpec((tp, D_POOL), lambda i: (i, 0)),
        compiler_params=_cp(("parallel",)),
    )(proj_main, proj_main, proj_main, pool_w, pool_scale)


def _conv_fwd(cur, prev_tail, w4):
    ext = jnp.concatenate([prev_tail, cur], axis=0)
    y = ext * w4[CONV_WIDTH - 1:CONV_WIDTH, :]
    for sft in range(1, CONV_WIDTH):
        y = y + pltpu.roll(ext, sft, 0) * w4[CONV_WIDTH - 1 - sft:CONV_WIDTH - sft, :]
    return y[HALO:, :]


def _l2n_heads(t):
    parts = []
    for h in range(DN_HEADS):
        th = t[:, h * DN_HEAD_DIM:(h + 1) * DN_HEAD_DIM]
        parts.append(th * lax.rsqrt(jnp.sum(th * th, axis=-1, keepdims=True) + NORM_EPS))
    return jnp.concatenate(parts, axis=1)


def _post_conv(yq, yk, yv):
    return _l2n_heads(_silu(yq)), _l2n_heads(_silu(yk)), _silu(yv)


def _gates(ba, alog_lane, dtb_lane):
    lane = lax.broadcasted_iota(jnp.int32, ba.shape, 1)
    beta = jax.nn.sigmoid(ba)
    g = -jnp.exp(alog_lane) * _softplus(ba + dtb_lane)
    return jnp.where(lane < DN_HEADS, beta, jnp.where(lane < 2 * DN_HEADS, g, 0.0))


def _qkv_specs(tp, which):
    def spec(col, n=None):
        if which == 0:
            return pl.BlockSpec((tp, D_DN), lambda i: (i, col))
        if which < 0:
            return pl.BlockSpec((tp, D_DN), lambda i: (jnp.maximum(i - 1, 0), col))
        return pl.BlockSpec((tp, D_DN), lambda i: (jnp.minimum(i + 1, n - 1), col))
    return spec


def _dn_pre(proj_main, proj_ba, conv_full, alog_lane, dtb_lane, tp):
    s = proj_main.shape[0]

    def body(q_ref, k_ref, v_ref, qp_ref, kp_ref, vp_ref, cw_ref, ba_ref, al_ref, db_ref,
             qn_ref, kn_ref, vv_ref, gb_ref):
        i = pl.program_id(0)
        ys = []
        for j, (cur, prev) in enumerate(((q_ref, qp_ref), (k_ref, kp_ref), (v_ref, vp_ref))):
            ys.append(_conv_fwd(cur[...], _prev_tail(prev, i), cw_ref[:, j * D_DN:(j + 1) * D_DN]))
        qn, kn, vv = _post_conv(*ys)
        qn_ref[...] = qn
        kn_ref[...] = kn
        vv_ref[...] = vv
        gb_ref[...] = _gates(ba_ref[...], al_ref[...], db_ref[...])

    cur, prev = _qkv_specs(tp, 0), _qkv_specs(tp, -1)
    row = pl.BlockSpec((1, 128), lambda i: (0, 0))
    tile = pl.BlockSpec((tp, D_DN), lambda i: (i, 0))
    return pl.pallas_call(
        body, name="dn_pre", grid=(s // tp,),
        out_shape=(jax.ShapeDtypeStruct((s, D_DN), F32),) * 3 + (jax.ShapeDtypeStruct((s, 128), F32),),
        in_specs=[cur(2), cur(3), cur(4), prev(2), prev(3), prev(4),
                  pl.BlockSpec((CONV_WIDTH, 3 * D_DN), lambda i: (0, 0)),
                  pl.BlockSpec((tp, 128), lambda i: (i, 0)), row, row],
        out_specs=(tile, tile, tile, pl.BlockSpec((tp, 128), lambda i: (i, 0))),
        compiler_params=_cp(("parallel",)),
    )(proj_main, proj_main, proj_main, proj_main, proj_main, proj_main, conv_full, proj_ba, alog_lane, dtb_lane)


def _dn_block(q, k, v, gcol, bcol, state, dz, nw):
    nb, n, d = q.shape
    ii = lax.broadcasted_iota(jnp.int32, (n, n), 0)
    jj = lax.broadcasted_iota(jnp.int32, (n, n), 1)
    lower = ii >= jj
    eye = (ii == jj).astype(F32)
    g_row = jnp.sum(eye * gcol, axis=1, keepdims=True)
    gc_col = jnp.sum(jnp.where(lower, g_row, 0.0), axis=2, keepdims=True)
    gc_row = jnp.sum(eye * gc_col, axis=1, keepdims=True)
    decay = jnp.where(lower, jnp.exp(jnp.where(lower, gc_col - gc_row, 0.0)), 0.0)
    kb = k * bcol
    vb = v * bcol
    qs = q * (DN_HEAD_DIM ** -0.5)
    egc = jnp.exp(gc_col)
    akq = _mm(jnp.concatenate([kb, qs], axis=1), k, 1, 1, *_DN_PREC["akq"])
    a = jnp.where(ii > jj, akq[:, :n] * decay, 0.0)
    qk = akq[:, n:] * decay
    t = _tri_inv(a, *_DN_PREC["inv"])
    uw = _mm(t, jnp.concatenate([vb, kb * egc], axis=2), 1, 0, *_DN_PREC["uw"])
    wq = jnp.concatenate([uw[:, :, d:], qs * egc], axis=1)
    g_last = gc_col[:, n - 1:n, :]
    k_dec = k * jnp.exp(g_last - gc_col)
    e_last = jnp.exp(g_last)
    os_ = []
    for c in range(nb // DN_HEADS):
        sl = slice(c * DN_HEADS, (c + 1) * DN_HEADS)
        ws = _mm(wq[sl], state, 1, 0, *_DN_PREC["ws"])
        v_new = uw[sl, :, :d] - ws[:, :n]
        os_.append(ws[:, n:] + _mm(qk[sl], v_new, 1, 0, *_DN_PREC["ov"]))
        state = state * e_last[sl] + _mm(k_dec[sl], v_new, 0, 0, *_DN_PREC["st"])
    o = jnp.concatenate(os_, axis=0)
    y = o * lax.rsqrt(jnp.mean(o * o, axis=-1, keepdims=True) + NORM_EPS) * nw * _silu(dz)
    return y, state


def _dn_block_args(gc, q_ref, k_ref, v_ref, gb_ref, dz_ref):
    qs, ks, vs, gs, bs, zs = [], [], [], [], [], []
    for cc in range(gc):
        r = slice(cc * CHUNK, (cc + 1) * CHUNK)
        gbv = gb_ref[r, :]
        for h in range(DN_HEADS):
            cols = slice(h * DN_HEAD_DIM, (h + 1) * DN_HEAD_DIM)
            qs.append(q_ref[r, cols])
            ks.append(k_ref[r, cols])
            vs.append(v_ref[r, cols])
            zs.append(dz_ref[r, cols])
            gs.append(gbv[:, DN_HEADS + h:DN_HEADS + h + 1])
            bs.append(gbv[:, h:h + 1])
    return tuple(jnp.stack(t, axis=0) for t in (qs, ks, vs, gs, bs, zs))


def _dn_scan_fwd(qn, kn, vv, gb, proj_main, dn_norm_w, gc):
    s = qn.shape[0]
    nchunk = s // CHUNK
    rows = gc * CHUNK

    def body(q_ref, k_ref, v_ref, gb_ref, dz_ref, nw_ref, y_ref, ss_ref, state):
        @pl.when(pl.program_id(0) == 0)
        def _():
            state[...] = jnp.zeros_like(state)
        q, k, v, gcol, bcol, dz = _dn_block_args(gc, q_ref, k_ref, v_ref, gb_ref, dz_ref)
        st = state[...]
        ss_ref[0] = st
        y, new = _dn_block(q, k, v, gcol, bcol, st, dz, nw_ref[...])
        state[...] = new
        for cc in range(gc):
            for h in range(DN_HEADS):
                y_ref[cc * CHUNK:(cc + 1) * CHUNK, h * DN_HEAD_DIM:(h + 1) * DN_HEAD_DIM] = y[cc * DN_HEADS + h]

    tile = pl.BlockSpec((rows, D_DN), lambda i: (i, 0))
    return pl.pallas_call(
        body, name="dn_scan_fwd", grid=(nchunk // gc,),
        out_shape=(jax.ShapeDtypeStruct((s, D_DN), F32),
                   jax.ShapeDtypeStruct((nchunk // gc, DN_HEADS, DN_HEAD_DIM, DN_HEAD_DIM), F32)),
        in_specs=[tile, tile, tile, pl.BlockSpec((rows, 128), lambda i: (i, 0)),
                  pl.BlockSpec((rows, D_DN), lambda i: (i, 5)), pl.BlockSpec((1, 128), lambda i: (0, 0))],
        out_specs=(tile, pl.BlockSpec((1, DN_HEADS, DN_HEAD_DIM, DN_HEAD_DIM), lambda i: (i, 0, 0, 0))),
        scratch_shapes=[pltpu.VMEM((DN_HEADS, DN_HEAD_DIM, DN_HEAD_DIM), F32)],
        compiler_params=_cp(("arbitrary",)),
    )(qn, kn, vv, gb, proj_main, dn_norm_w)


def _out_proj_loss(y_pool, y_dn, x2, tgt, w_out_full, fnw, tm):
    s = x2.shape[0]

    def body(yp_ref, yd_ref, x_ref, t_ref, wo_ref, fw_ref,
             dh_ref, dyp_ref, dyd_ref, gwo_ref, gfw_ref, loss_ref):
        @pl.when(pl.program_id(0) == 0)
        def _():
            gwo_ref[...] = jnp.zeros_like(gwo_ref)
            gfw_ref[...] = jnp.zeros_like(gfw_ref)
            loss_ref[...] = jnp.zeros_like(loss_ref)
        y = jnp.concatenate([yp_ref[...], yd_ref[...]], axis=1).astype(BF16)
        wo = wo_ref[...]
        h = x_ref[...] + jnp.dot(y, wo, preferred_element_type=F32)
        hn, r = _rms_hat(h)
        fw = fw_ref[...]
        err = hn * fw - t_ref[...]
        loss_ref[...] += 0.5 * jnp.sum(jnp.sum(err * err, axis=-1, keepdims=True) / D_MODEL, axis=0, keepdims=True)
        dout = err / D_MODEL
        gfw_ref[...] += jnp.sum(dout * hn, axis=0, keepdims=True)
        dhn = dout * fw
        dh = r * (dhn - hn * jnp.mean(dhn * hn, axis=-1, keepdims=True))
        dh_ref[...] = dh
        dhb = dh.astype(BF16)
        dy = _dot_nt_bf(dhb, wo)
        dyp_ref[...] = dy[:, :D_POOL]
        dyd_ref[...] = dy[:, D_POOL:]
        gwo_ref[...] += _dot_tn_bf(y, dhb)

    half = pl.BlockSpec((tm, D_POOL), lambda i: (i, 0))
    full = pl.BlockSpec((tm, D_MODEL), lambda i: (i, 0))
    return pl.pallas_call(
        body, name="out_proj_loss", grid=(s // tm,),
        out_shape=(jax.ShapeDtypeStruct((s, D_MODEL), F32), jax.ShapeDtypeStruct((s, D_POOL), F32),
                   jax.ShapeDtypeStruct((s, D_DN), F32), jax.ShapeDtypeStruct((D_MODEL, D_MODEL), F32),
                   jax.ShapeDtypeStruct((1, D_MODEL), F32), jax.ShapeDtypeStruct((1, 128), F32)),
        in_specs=[half, half, full, full, pl.BlockSpec((D_MODEL, D_MODEL), lambda i: (0, 0)),
                  pl.BlockSpec((1, D_MODEL), lambda i: (0, 0))],
        out_specs=(full, half, half, pl.BlockSpec((D_MODEL, D_MODEL), lambda i: (0, 0)),
                   pl.BlockSpec((1, D_MODEL), lambda i: (0, 0)), pl.BlockSpec((1, 128), lambda i: (0, 0))),
        compiler_params=_cp(("arbitrary",)),
    )(y_pool, y_dn, x2, tgt, w_out_full, fnw)


def _dn_scan_bwd(qn, kn, vv, gb, proj_main, dn_norm_w, states, dy_dn, gc):
    s = qn.shape[0]
    nchunk = s // CHUNK
    nstep = nchunk // gc
    rows = gc * CHUNK

    def body(q_ref, k_ref, v_ref, gb_ref, dz_ref, nw_ref, ss_ref, dy_ref,
             dq_ref, dk_ref, dv_ref, dgb_ref, ddz_ref, dnw_ref, dstate):
        @pl.when(pl.program_id(0) == 0)
        def _():
            dstate[...] = jnp.zeros_like(dstate)
            dnw_ref[...] = jnp.zeros_like(dnw_ref)
        lane = lax.broadcasted_iota(jnp.int32, (CHUNK, 128), 1)
        q, k, v, gcol, bcol, dz = _dn_block_args(gc, q_ref, k_ref, v_ref, gb_ref, dz_ref)
        dy = jnp.stack([dy_ref[cc * CHUNK:(cc + 1) * CHUNK, h * DN_HEAD_DIM:(h + 1) * DN_HEAD_DIM]
                        for cc in range(gc) for h in range(DN_HEADS)], axis=0)
        _, vjp = jax.vjp(_dn_block, q, k, v, gcol, bcol, ss_ref[0], dz, nw_ref[...])
        dq, dk, dv, dg, db, dst, ddz, dnw = vjp((dy, dstate[...]))
        dstate[...] = dst
        dnw_ref[...] += dnw
        for cc in range(gc):
            r = slice(cc * CHUNK, (cc + 1) * CHUNK)
            dgb = jnp.zeros((CHUNK, 128), F32)
            for h in range(DN_HEADS):
                b = cc * DN_HEADS + h
                cols = slice(h * DN_HEAD_DIM, (h + 1) * DN_HEAD_DIM)
                for ref, val in zip((dq_ref, dk_ref, dv_ref, ddz_ref), (dq, dk, dv, ddz)):
                    ref[r, cols] = val[b]
                dgb = dgb + jnp.where(lane == h, db[b], 0.0) + jnp.where(lane == DN_HEADS + h, dg[b], 0.0)
            dgb_ref[r, :] = dgb

    rev = lambda i: (nstep - 1 - i, 0)
    tile = pl.BlockSpec((rows, D_DN), rev)
    lanes = pl.BlockSpec((rows, 128), rev)
    return pl.pallas_call(
        body, name="dn_scan_bwd", grid=(nstep,),
        out_shape=(jax.ShapeDtypeStruct((s, D_DN), F32),) * 3
        + (jax.ShapeDtypeStruct((s, 128), F32), jax.ShapeDtypeStruct((s, D_DN), F32),
           jax.ShapeDtypeStruct((1, 128), F32)),
        in_specs=[tile, tile, tile, lanes, pl.BlockSpec((rows, D_DN), lambda i: (nstep - 1 - i, 5)),
                  pl.BlockSpec((1, 128), lambda i: (0, 0)),
                  pl.BlockSpec((1, DN_HEADS, DN_HEAD_DIM, DN_HEAD_DIM), lambda i: (nstep - 1 - i, 0, 0, 0)), tile],
        out_specs=(tile, tile, tile, lanes, tile, pl.BlockSpec((1, 128), lambda i: (0, 0))),
        scratch_shapes=[pltpu.VMEM((DN_HEADS, DN_HEAD_DIM, DN_HEAD_DIM), F32)],
        compiler_params=_cp(("arbitrary",)),
    )(qn, kn, vv, gb, proj_main, dn_norm_w, states, dy_dn)


def _dn_pre_bwd1(proj_main, proj_ba, conv_full, alog_lane, dtb_lane, dqn, dkn, dvv, dgb, tp):
    s = proj_main.shape[0]

    def body(q_ref, k_ref, v_ref, qp_ref, kp_ref, vp_ref, cw_ref, ba_ref, al_ref, db_ref,
             dqn_ref, dkn_ref, dvv_ref, dgb_ref,
             dcq_ref, dck_ref, dcv_ref, dba_ref, dcw_ref, dal_ref, ddb_ref):
        i = pl.program_id(0)

        @pl.when(i == 0)
        def _():
            dcw_ref[...] = jnp.zeros_like(dcw_ref)
            dal_ref[...] = jnp.zeros_like(dal_ref)
            ddb_ref[...] = jnp.zeros_like(ddb_ref)
        curs = (q_ref[...], k_ref[...], v_ref[...])
        tails = (_prev_tail(qp_ref, i), _prev_tail(kp_ref, i), _prev_tail(vp_ref, i))
        ys = [_conv_fwd(curs[j], tails[j], cw_ref[:, j * D_DN:(j + 1) * D_DN]) for j in range(3)]
        _, vjp = jax.vjp(_post_conv, *ys)
        dys = vjp((dqn_ref[...], dkn_ref[...], dvv_ref[...]))
        for j, (dy, out) in enumerate(zip(dys, (dcq_ref, dck_ref, dcv_ref))):
            out[...] = dy
            for sft in range(CONV_WIDTH):
                xs = curs[j] if sft == 0 else _shift_down(curs[j], tails[j], sft)
                row = CONV_WIDTH - 1 - sft
                dcw_ref[row:row + 1, j * D_DN:(j + 1) * D_DN] += jnp.sum(dy * xs, axis=0, keepdims=True)
        _, gvjp = jax.vjp(_gates, ba_ref[...], al_ref[...], db_ref[...])
        dba, dal, ddb = gvjp(dgb_ref[...])
        dba_ref[...] = dba
        dal_ref[...] += dal
        ddb_ref[...] += ddb

    cur, prev = _qkv_specs(tp, 0), _qkv_specs(tp, -1)
    row = pl.BlockSpec((1, 128), lambda i: (0, 0))
    tile = pl.BlockSpec((tp, D_DN), lambda i: (i, 0))
    lanes = pl.BlockSpec((tp, 128), lambda i: (i, 0))
    cw = pl.BlockSpec((CONV_WIDTH, 3 * D_DN), lambda i: (0, 0))
    return pl.pallas_call(
        body, name="dn_pre_bwd1", grid=(s // tp,),
        out_shape=(jax.ShapeDtypeStruct((s, D_DN), F32),) * 3
        + (jax.ShapeDtypeStruct((s, 128), F32), jax.ShapeDtypeStruct((CONV_WIDTH, 3 * D_DN), F32),
           jax.ShapeDtypeStruct((1, 128), F32), jax.ShapeDtypeStruct((1, 128), F32)),
        in_specs=[cur(2), cur(3), cur(4), prev(2), prev(3), prev(4), cw, lanes, row, row, tile, tile, tile, lanes],
        out_specs=(tile, tile, tile, lanes, cw, row, row),
        compiler_params=_cp(("arbitrary",)),
    )(proj_main, proj_main, proj_main, proj_main, proj_main, proj_main, conv_full, proj_ba, alog_lane, dtb_lane,
      dqn, dkn, dvv, dgb)


def _conv_bwd_x(dcq, dck, dcv, conv_full, tp):
    s = dcq.shape[0]
    n = s // tp

    def body(q_ref, k_ref, v_ref, qn_ref, kn_ref, vn_ref, cw_ref, dq_ref, dk_ref, dv_ref):
        i = pl.program_id(0)
        for j, (cur, nxt, out) in enumerate(((q_ref, qn_ref, dq_ref), (k_ref, kn_ref, dk_ref), (v_ref, vn_ref, dv_ref))):
            w4 = cw_ref[:, j * D_DN:(j + 1) * D_DN]
            dy = cur[...]
            head = _next_head(nxt, i, n)
            dx = dy * w4[CONV_WIDTH - 1:CONV_WIDTH, :]
            for sft in range(1, CONV_WIDTH):
                dx = dx + _shift_up(dy, head, sft) * w4[CONV_WIDTH - 1 - sft:CONV_WIDTH - sft, :]
            out[...] = dx

    tile = pl.BlockSpec((tp, D_DN), lambda i: (i, 0))
    nxt = pl.BlockSpec((tp, D_DN), lambda i: (jnp.minimum(i + 1, n - 1), 0))
    return pl.pallas_call(
        body, name="conv_bwd_x", grid=(n,),
        out_shape=(jax.ShapeDtypeStruct((s, D_DN), F32),) * 3,
        in_specs=[tile, tile, tile, nxt, nxt, nxt, pl.BlockSpec((CONV_WIDTH, 3 * D_DN), lambda i: (0, 0))],
        out_specs=(tile, tile, tile),
        compiler_params=_cp(("parallel",)),
    )(dcq, dck, dcv, dcq, dck, dcv, conv_full)


def _pool_bwd1(proj_main, pool_w, pool_scale, dyp, tp):
    s = proj_main.shape[0]

    def body(u_ref, up_ref, z_ref, pw_ref, ps_ref, dy_ref, dz_ref, dwin_ref, dpw_ref, dps_ref):
        i = pl.program_id(0)

        @pl.when(i == 0)
        def _():
            dpw_ref[...] = jnp.zeros_like(dpw_ref)
            dps_ref[...] = jnp.zeros_like(dps_ref)
        u = u_ref[...]
        z = z_ref[...]
        dy = dy_ref[...]
        ps = ps_ref[...]
        mixes = _pool_mix(u, _prev_tail(up_ref, i), i, tp)
        sg = jax.nn.sigmoid(z)
        sz = z * sg
        dsz = sg * (1.0 + z * (1.0 - sg))
        for gi, w in enumerate(POOL_WINDOWS):
            cols = slice(gi * POOL_GROUP, (gi + 1) * POOL_GROUP)
            mixw = _dot_hi(mixes[gi], pw_ref[gi])
            dmixw = dy[:, cols] * ps[:, cols] * sz[:, cols]
            dps_ref[:, cols] += jnp.sum(dy[:, cols] * mixw * sz[:, cols], axis=0, keepdims=True)
            dz_ref[:, cols] = dy[:, cols] * mixw * ps[:, cols] * dsz[:, cols]
            dpw_ref[gi] += _dot_tn_hi(mixes[gi], dmixw)
            dmix = lax.dot_general(dmixw, pw_ref[gi], (((1,), (1,)), ((), ())), precision=HI,
                                   preferred_element_type=F32)
            dwin_ref[:, cols] = dmix / _pool_counts(i, tp, w)

    tile = pl.BlockSpec((tp, D_POOL), lambda i: (i, 0))
    pw = pl.BlockSpec((4, POOL_GROUP, POOL_GROUP), lambda i: (0, 0, 0))
    ps = pl.BlockSpec((1, D_POOL), lambda i: (0, 0))
    return pl.pallas_call(
        body, name="pool_bwd1", grid=(s // tp,),
        out_shape=(jax.ShapeDtypeStruct((s, D_POOL), F32), jax.ShapeDtypeStruct((s, D_POOL), F32),
                   jax.ShapeDtypeStruct((4, POOL_GROUP, POOL_GROUP), F32), jax.ShapeDtypeStruct((1, D_POOL), F32)),
        in_specs=[tile, pl.BlockSpec((tp, D_POOL), lambda i: (jnp.maximum(i - 1, 0), 0)),
                  pl.BlockSpec((tp, D_POOL), lambda i: (i, 1)), pw, ps, tile],
        out_specs=(tile, tile, pw, ps),
        compiler_params=_cp(("arbitrary",)),
    )(proj_main, proj_main, proj_main, pool_w, pool_scale, dyp)


def _pool_bwd2(dwin, tp):
    s = dwin.shape[0]
    n = s // tp

    def body(d_ref, dn_ref, du_ref):
        i = pl.program_id(0)
        ext = jnp.concatenate([d_ref[...], _next_head(dn_ref, i, n)], axis=0)
        m = ext.shape[0]
        a2 = ext + pltpu.roll(ext, m - 1, 0)
        a4 = a2 + pltpu.roll(a2, m - 2, 0)
        a8 = a4 + pltpu.roll(a4, m - 4, 0)
        a16 = a8 + pltpu.roll(a8, m - 8, 0)
        for gi, (w, acc) in enumerate(zip(POOL_WINDOWS, (a2, a4, a8, a16))):
            cols = slice(gi * POOL_GROUP, (gi + 1) * POOL_GROUP)
            du_ref[:, cols] = acc[:tp, cols] - d_ref[:, cols] * _pool_counts(i, tp, w)

    tile = pl.BlockSpec((tp, D_POOL), lambda i: (i, 0))
    return pl.pallas_call(
        body, name="pool_bwd2", grid=(n,),
        out_shape=jax.ShapeDtypeStruct((s, D_POOL), F32),
        in_specs=[tile, pl.BlockSpec((tp, D_POOL), lambda i: (jnp.minimum(i + 1, n - 1), 0))],
        out_specs=tile,
        compiler_params=_cp(("parallel",)),
    )(dwin, dwin)


def _in_proj_bwd(dparts, dba, x2, dh, norm_w, wt_full, tm):
    s = x2.shape[0]
    nstep = s // tm

    def body(d0, d1, d2, d3, d4, d5, dba_ref, x_ref, dh_ref, nw_ref, wt_hbm,
             gx_ref, p_hbm, gnw_ref, wt_vmem, acc, blk, sem, osem):
        i = pl.program_id(0)

        @pl.when(i == 0)
        def _():
            cp = pltpu.make_async_copy(wt_hbm, wt_vmem, sem)
            cp.start()
            acc[...] = jnp.zeros_like(acc)
            gnw_ref[...] = jnp.zeros_like(gnw_ref)
            cp.wait()
        dbab = dba_ref[...].astype(BF16)
        xhat, r = _rms_hat(x_ref[...])
        nw = nw_ref[...]
        n = (xhat * nw).astype(BF16)
        acc[D_MAIN:, :] += _dot_tn_bf(dbab, n)
        dn = jnp.dot(dbab, wt_vmem[D_MAIN:, :], preferred_element_type=F32)
        for cb, d in enumerate((d0, d1, d2, d3, d4, d5)):
            rows = slice(cb * D_POOL, (cb + 1) * D_POOL)
            dpart = d[...].astype(BF16)
            acc[rows, :] += _dot_tn_bf(dpart, n)
            dn = dn + jnp.dot(dpart, wt_vmem[rows, :], preferred_element_type=F32)
        gnw_ref[...] += jnp.sum(dn * xhat, axis=0, keepdims=True)
        dxh = dn * nw
        gx_ref[...] = dh_ref[...] + r * (dxh - xhat * jnp.mean(dxh * xhat, axis=-1, keepdims=True))

        @pl.when(i == nstep - 1)
        def _():
            def out(d):
                return pltpu.make_async_copy(blk.at[d % 2], p_hbm.at[d], osem.at[d % 2])
            for d in range(N_DEV):
                if d >= 2:
                    out(d - 2).wait()
                blk[d % 2] = acc[W_IN_SHARD * d:W_IN_SHARD * (d + 1), :]
                out(d).start()
            out(N_DEV - 2).wait()
            out(N_DEV - 1).wait()

    part = pl.BlockSpec((tm, D_POOL), lambda i: (i, 0))
    full = pl.BlockSpec((tm, D_MODEL), lambda i: (i, 0))
    row = pl.BlockSpec((1, D_MODEL), lambda i: (0, 0))
    return pl.pallas_call(
        body, name="in_proj_bwd", grid=(nstep,),
        out_shape=(jax.ShapeDtypeStruct((s, D_MODEL), F32),
                   jax.ShapeDtypeStruct((N_DEV, W_IN_SHARD, D_MODEL), F32), jax.ShapeDtypeStruct((1, D_MODEL), F32)),
        in_specs=[part] * 6 + [pl.BlockSpec((tm, 128), lambda i: (i, 0)), full, full, row,
                               pl.BlockSpec(memory_space=pl.ANY)],
        out_specs=(full, pl.BlockSpec(memory_space=pl.ANY), row),
        scratch_shapes=[pltpu.VMEM((D_IN_PAD, D_MODEL), BF16), pltpu.VMEM((D_IN_PAD, D_MODEL), F32),
                        pltpu.VMEM((2, W_IN_SHARD, D_MODEL), F32), pltpu.SemaphoreType.DMA,
                        pltpu.SemaphoreType.DMA((2,))],
        compiler_params=_cp(("arbitrary",)),
    )(*dparts, dba, x2, dh, norm_w, wt_full)


def _adamw_math(w, g, m, v):
    m = ADAM_B1 * m + (1.0 - ADAM_B1) * g
    v = ADAM_B2 * v + (1.0 - ADAM_B2) * (g * g)
    m_hat = m / (1.0 - ADAM_B1 ** ADAM_STEP)
    v_hat = v / (1.0 - ADAM_B2 ** ADAM_STEP)
    delta = -ADAM_LR * (m_hat / (jnp.sqrt(v_hat) + ADAM_EPS) + ADAM_WD * w)
    return delta, m, v


def _adamw_sharded(params):
    k = len(params)

    def body(*refs):
        ins, outs = refs[:4 * k], refs[4 * k:]
        for p in range(k):
            w, g, m, v = (r[...] for r in ins[4 * p:4 * p + 4])
            d, nm, nv = _adamw_math(w, g, m, v)
            outs[3 * p][...] = d
            outs[3 * p + 1][...] = nm
            outs[3 * p + 2][...] = nv

    flat = [a for p in params for a in p]
    out_shape = tuple(jax.ShapeDtypeStruct(p[0].shape, F32) for p in params for _ in range(3))
    res = pl.pallas_call(body, name="adamw_sharded", out_shape=out_shape, compiler_params=_cp())(*flat)
    return [tuple(res[3 * p:3 * p + 3]) for p in range(k)]


def _adamw_replicated(gath_a, gath_b, pool, rows):
    nrow = len(rows)

    def body(*refs):
        ga_ref, gb_ref = refs[:2]
        ins = refs[2:2 + 3 * (nrow + 1)]
        outs = refs[2 + 3 * (nrow + 1):]

        def total(ref):
            g = ref[0]
            for d in range(1, N_DEV):
                g = g + ref[d]
            return g

        def update(g, wmv, o):
            w, m, v = (r[...] for r in wmv)
            dl, nm, nv = _adamw_math(w, g, m, v)
            o[0][...] = g
            o[1][...] = dl
            o[2][...] = nm
            o[3][...] = nv

        update(total(ga_ref), ins[:3], outs[:4])
        gb = total(gb_ref)
        for r in range(nrow):
            n = ins[3 * (r + 1)].shape[1]
            update(gb[r:r + 1, :n], ins[3 * (r + 1):3 * (r + 2)], outs[4 * (r + 1):4 * (r + 2)])
        outs[4 * (nrow + 1)][...] = gb[nrow:nrow + 1, 0:1]

    flat = list(pool) + [a for wmv in rows for a in wmv]
    out_shape = ((jax.ShapeDtypeStruct(pool[0].shape, F32),) * 4
                 + tuple(jax.ShapeDtypeStruct(wmv[0].shape, F32) for wmv in rows for _ in range(4))
                 + (jax.ShapeDtypeStruct((1, 1), F32),))
    res = pl.pallas_call(body, name="adamw_replicated", out_shape=out_shape, compiler_params=_cp())(
        gath_a, gath_b, *flat)
    return [res[4 * k:4 * k + 4] for k in range(nrow + 1)], res[-1]


_ROW_ORDER = ("norm_w", "final_norm_w", "pool_scale", "dn_norm_w", "a_log", "dt_bias")


def _pack_rows(vectors):
    out = [jnp.pad(v.reshape(-1), (0, D_MODEL - v.size)) for v in vectors]
    out += [jnp.zeros((D_MODEL,), F32)] * (8 - len(out))
    return jnp.stack(out, axis=0)


def _lane_row(vec4, start):
    return jnp.pad(vec4.reshape(-1), (start, 128 - start - vec4.size)).reshape(1, 128)


def kernel(x, norm_w, w_in, pool_w, pool_scale, conv_w, a_log, dt_bias, dn_norm_w, w_out, final_norm_w, loss_target, m_norm_w, m_w_in, m_pool_w, m_pool_scale, m_conv_w, m_a_log, m_dt_bias, m_dn_norm_w, m_w_out, m_final_norm_w, v_norm_w, v_w_in, v_pool_w, v_pool_scale, v_conv_w, v_a_log, v_dt_bias, v_dn_norm_w, v_w_out, v_final_norm_w):
    s = x.shape[1]
    tm = min(512, s)
    tmb = min(256, s)
    tp = min(512, s)
    x2 = x[0]
    tgt = loss_target[0]
    wt, m_wt, v_wt = w_in[0].T, m_w_in[0].T, v_w_in[0].T

    g_in, g_out, g_conv = _gather_weights(wt, w_out[0], conv_w[0])
    w_out_full = g_out.reshape(D_MODEL, D_MODEL)
    conv_full = g_conv.transpose(1, 0, 2).reshape(CONV_WIDTH, 3 * D_DN)
    alog_lane = _lane_row(a_log, DN_HEADS)
    dtb_lane = _lane_row(dt_bias, DN_HEADS)
    fnw = final_norm_w.reshape(1, D_MODEL)

    proj_main, proj_ba, wt_full = _in_proj(x2, norm_w, g_in, tm)
    y_pool = _pool_fwd(proj_main, pool_w[0], pool_scale, tp)
    qn, kn, vv, gb = _dn_pre(proj_main, proj_ba, conv_full, alog_lane, dtb_lane, tp)
    y_dn, states = _dn_scan_fwd(qn, kn, vv, gb, proj_main, dn_norm_w, DN_CHUNKS_PER_STEP)

    dh, dyp, dyd, g_wout, g_fnw, loss_part = _out_proj_loss(y_pool, y_dn, x2, tgt, w_out_full, fnw, tm)
    dqn, dkn, dvv, dgb, ddz, g_dnw = _dn_scan_bwd(qn, kn, vv, gb, proj_main, dn_norm_w, states, dyd,
                                                  DN_CHUNKS_PER_STEP)
    dcq, dck, dcv, dba, g_conv_full, g_al, g_db = _dn_pre_bwd1(
        proj_main, proj_ba, conv_full, alog_lane, dtb_lane, dqn, dkn, dvv, dgb, tp)
    dq, dk, dv = _conv_bwd_x(dcq, dck, dcv, conv_full, tp)
    dzp, dwin, g_pw, g_ps = _pool_bwd1(proj_main, pool_w[0], pool_scale, dyp, tp)
    dup = _pool_bwd2(dwin, tp)
    grad_x2, p_in, g_nw = _in_proj_bwd((dup, dzp, dq, dk, dv, ddz), dba, x2, dh, norm_w, wt_full, tmb)

    p_out = g_wout.reshape(N_DEV, D_MODEL // N_DEV, D_MODEL)
    p_conv = g_conv_full.reshape(CONV_WIDTH, N_DEV, 3 * D_DN // N_DEV).transpose(1, 0, 2)
    pack_a = g_pw.reshape(4 * POOL_GROUP, POOL_GROUP)
    pack_b = _pack_rows([g_nw, g_fnw, g_ps, g_dnw, g_al[0, DN_HEADS:2 * DN_HEADS], g_db[0, DN_HEADS:2 * DN_HEADS],
                         loss_part[0, :1]])
    gr_in, gr_out, gr_conv, gath_a, gath_b = _reduce_grads(p_in, p_out, p_conv, pack_a, pack_b)

    sharded = _adamw_sharded([(wt, gr_in, m_wt, v_wt), (w_out[0], gr_out, m_w_out[0], v_w_out[0]),
                              (conv_w[0], gr_conv, m_conv_w[0], v_conv_w[0])])
    r_in, r_out, r_conv = ((g,) + dmv for g, dmv in zip((gr_in, gr_out, gr_conv), sharded))
    flat = lambda a: a.reshape(4 * POOL_GROUP, POOL_GROUP)
    row = lambda a: a.reshape(1, -1)
    vecs = {"norm_w": (norm_w, m_norm_w, v_norm_w), "final_norm_w": (final_norm_w, m_final_norm_w, v_final_norm_w),
            "pool_scale": (pool_scale, m_pool_scale, v_pool_scale), "dn_norm_w": (dn_norm_w, m_dn_norm_w, v_dn_norm_w),
            "a_log": (a_log, m_a_log, v_a_log), "dt_bias": (dt_bias, m_dt_bias, v_dt_bias)}
    res, loss = _adamw_replicated(gath_a, gath_b, (flat(pool_w), flat(m_pool_w), flat(v_pool_w)),
                                  [tuple(row(a) for a in vecs[nm]) for nm in _ROW_ORDER])
    r_pool = res[0]
    r_vec = dict(zip(_ROW_ORDER, res[1:]))

    def group(k):
        vec = lambda nm: r_vec[nm][k].reshape(vecs[nm][0].shape)
        return (vec("norm_w"), r_in[k].T[None], r_pool[k].reshape(pool_w.shape), vec("pool_scale"), r_conv[k][None],
                vec("a_log"), vec("dt_bias"), vec("dn_norm_w"), r_out[k][None], vec("final_norm_w"))

    return (loss[0, 0], grad_x2[None], *group(0), *group(1), *group(2), *group(3))
```

```python
import functools

import jax
import jax.numpy as jnp
from jax import lax
from jax.experimental import pallas as pl
from jax.experimental.pallas import tpu as pltpu

F32 = jnp.float32
BF16 = jnp.bfloat16
HI = lax.Precision.HIGHEST
MESH = pl.DeviceIdType.MESH

D_MODEL = 1024
D_POOL = 512
D_DN = 512
POOL_WINDOWS = (2, 4, 8, 16)
POOL_GROUP = 128
DN_HEADS = 4
DN_HEAD_DIM = 128
CONV_WIDTH = 4
CHUNK = 64
NORM_EPS = 1e-6
D_IN = 3080
D_MAIN = 3072
D_IN_PAD = D_MAIN + 128
N_DEV = 8
W_IN_SHARD = D_IN // N_DEV
HALO = 16
DN_CHUNKS_PER_STEP = 8

ADAM_LR = 0.001
ADAM_B1 = 0.9
ADAM_B2 = 0.999
ADAM_EPS = 1e-08
ADAM_WD = 0.01
ADAM_STEP = 10

VMEM_LIMIT = 56 * 1024 * 1024


def _cp(sem=None, vmem=VMEM_LIMIT):
    kw = {"vmem_limit_bytes": vmem}
    if sem is not None:
        kw["dimension_semantics"] = sem
    return pltpu.CompilerParams(**kw)


def _dot_bf(a, b):
    return jnp.dot(a.astype(BF16), b.astype(BF16), preferred_element_type=F32)


def _dot_nt_bf(a, b):
    return lax.dot_general(a.astype(BF16), b.astype(BF16), (((1,), (1,)), ((), ())), preferred_element_type=F32)


def _dot_tn_bf(a, b):
    return lax.dot_general(a.astype(BF16), b.astype(BF16), (((0,), (0,)), ((), ())), preferred_element_type=F32)


def _mm_raw(a, b, ca, cb, prec):
    off = a.ndim - 2
    dn = (((ca + off,), (cb + off,)), ((0,), (0,)) if off else ((), ()))
    if prec == "hi":
        return lax.dot_general(a, b, dn, precision=HI, preferred_element_type=F32)
    ah, bh = a.astype(BF16), b.astype(BF16)
    out = lax.dot_general(ah, bh, dn, preferred_element_type=F32)
    if prec == "x3":
        al = (a - ah.astype(F32)).astype(BF16)
        bl = (b - bh.astype(F32)).astype(BF16)
        out = out + lax.dot_general(ah, bl, dn, preferred_element_type=F32)
        out = out + lax.dot_general(al, bh, dn, preferred_element_type=F32)
    return out


@functools.partial(jax.custom_vjp, nondiff_argnums=(2, 3, 4, 5))
def _mm(a, b, ca, cb, prec, bprec):
    return _mm_raw(a, b, ca, cb, prec)


def _mm_fwd(a, b, ca, cb, prec, bprec):
    return _mm_raw(a, b, ca, cb, prec), (a, b)


def _mm_bwd(ca, cb, prec, bprec, res, dc):
    a, b = res
    da = _mm_raw(dc, b, 1, 1 - cb, bprec) if ca == 1 else _mm_raw(b, dc, 1 - cb, 1, bprec)
    db = _mm_raw(a, dc, 1 - ca, 0, bprec) if cb == 0 else _mm_raw(dc, a, 0, 1 - ca, bprec)
    return da, db


_mm.defvjp(_mm_fwd, _mm_bwd)


@functools.partial(jax.custom_vjp, nondiff_argnums=(1, 2))
def _tri_inv(a, prec, bprec):
    n = a.shape[-1]
    ii = lax.broadcasted_iota(jnp.int32, (n, n), 0)
    jj = lax.broadcasted_iota(jnp.int32, (n, n), 1)
    p = (ii == jj).astype(F32) - a
    b = _mm_raw(a, a, 1, 0, prec)
    for _ in range(4):
        pb = _mm_raw(jnp.concatenate([p, b], axis=-2), b, 1, 0, prec)
        p = p + pb[..., :n, :]
        b = pb[..., n:, :]
    return p + _mm_raw(p, b, 1, 0, prec)


def _tri_inv_fwd(a, prec, bprec):
    t = _tri_inv(a, prec, bprec)
    return t, t


def _tri_inv_bwd(prec, bprec, t, dt):
    return (-_mm_raw(_mm_raw(t, dt, 0, 0, bprec), t, 1, 1, bprec),)


_tri_inv.defvjp(_tri_inv_fwd, _tri_inv_bwd)

_DN_PREC = {"akq": ("bf16", "bf16"), "inv": ("bf16", "bf16"), "uw": ("bf16", "bf16"), "ws": ("bf16", "bf16"),
            "ov": ("bf16", "bf16"), "st": ("bf16", "bf16")}


def _silu(x):
    return x * jax.nn.sigmoid(x)


def _softplus(x):
    pos = x > 0.0
    return jnp.where(pos, x, 0.0) + jnp.log1p(jnp.exp(jnp.where(pos, -x, x)))


def _mesh_pos():
    return lax.axis_index("x"), lax.axis_index("y"), lax.axis_index("c")


def _dev_index(x, y, c):
    return 4 * x + 2 * y + c


def _all_gather_blocks(outs, send_sems, recv_sems):
    x, y, c = _mesh_pos()
    me = (x, y, c)
    sibling = (x, y, 1 - c)
    chips = [(1 - x, y), (x, 1 - y), (1 - x, 1 - y)]

    def copy(a, k, block, to):
        rows = outs[a].at[_dev_index(*block)]
        return pltpu.make_async_remote_copy(src_ref=rows, dst_ref=rows, send_sem=send_sems.at[a, k],
                                            recv_sem=recv_sems.at[a, k], device_id=to, device_id_type=MESH)

    n = len(outs)
    first = []
    for a in range(n):
        first.append(copy(a, 0, me, sibling))
        for j, chip in enumerate(chips):
            first.append(copy(a, 1 + j, me, (*chip, c)))
    for cp in first:
        cp.start()
    passed = []
    for j, chip in enumerate(chips):
        for a in range(n):
            copy(a, 1 + j, (*chip, c), me).wait_recv()
            fwd = copy(a, 4 + j, (*chip, c), sibling)
            fwd.start()
            passed.append(fwd)
    for a in range(n):
        copy(a, 0, sibling, me).wait_recv()
        for j, chip in enumerate(chips):
            copy(a, 4 + j, (*chip, 1 - c), me).wait_recv()
    for cp in first + passed:
        cp.wait_send()


def _gather_weights(w_in_blk, w_out_blk, conv_blk):
    def body(win_ref, wout_ref, conv_ref, gin_ref, gout_ref, gconv_ref, send_sems, recv_sems):
        x, y, c = _mesh_pos()
        me = _dev_index(x, y, c)
        gin_ref[me] = win_ref[...].astype(BF16)
        gout_ref[me] = wout_ref[...].astype(BF16)
        gconv_ref[me] = conv_ref[...]
        _all_gather_blocks((gin_ref, gout_ref, gconv_ref), send_sems, recv_sems)

    vm = pl.BlockSpec(memory_space=pltpu.VMEM)
    return pl.pallas_call(
        body, name="gather_weights",
        out_shape=(jax.ShapeDtypeStruct((N_DEV,) + w_in_blk.shape, BF16),
                   jax.ShapeDtypeStruct((N_DEV,) + w_out_blk.shape, BF16),
                   jax.ShapeDtypeStruct((N_DEV,) + conv_blk.shape, F32)),
        in_specs=[vm, vm, vm], out_specs=(vm, vm, vm),
        scratch_shapes=[pltpu.SemaphoreType.DMA((3, 7)), pltpu.SemaphoreType.DMA((3, 7))],
        compiler_params=_cp(),
    )(w_in_blk, w_out_blk, conv_blk)


def _reduce_grads(p_in, p_out, p_conv, pack_a, pack_b):
    big = (p_in, p_out, p_conv)

    def body(pin_ref, pout_ref, pconv_ref, pa_ref, pb_ref,
             oin_ref, oout_ref, oconv_ref, ga_ref, gb_ref,
             r1_in, r1_out, r1_conv, r2_in, r2_out, r2_conv, sb_in, sb_out, sb_conv, st_in, st_out, st_conv,
             s1_send, s1_recv, s2_send, s2_recv, ag_send, ag_recv, st_sem):
        x, y, c = _mesh_pos()
        me = (x, y, c)
        sibling = (x, y, 1 - c)
        rel = [(x, y), (1 - x, y), (x, 1 - y), (1 - x, 1 - y)]
        srcs = (pin_ref, pout_ref, pconv_ref)
        r1s = (r1_in, r1_out, r1_conv)
        r2s = (r2_in, r2_out, r2_conv)
        sbs = (sb_in, sb_out, sb_conv)
        sts = (st_in, st_out, st_conv)
        outs = (oin_ref, oout_ref, oconv_ref)

        ga_ref[_dev_index(*me)] = pa_ref[...]
        gb_ref[_dev_index(*me)] = pb_ref[...]

        def p1(a, r, to):
            return pltpu.make_async_remote_copy(
                src_ref=srcs[a].at[_dev_index(*rel[r], 1 - c)], dst_ref=r1s[a].at[r],
                send_sem=s1_send.at[a, r], recv_sem=s1_recv.at[a, r], device_id=to, device_id_type=MESH)

        def p2(a, r, to):
            return pltpu.make_async_remote_copy(
                src_ref=sbs[a].at[r - 1], dst_ref=r2s[a].at[r - 1],
                send_sem=s2_send.at[a, r - 1], recv_sem=s2_recv.at[a, r - 1], device_id=to, device_id_type=MESH)

        def stage(a, r):
            return pltpu.make_async_copy(srcs[a].at[_dev_index(*rel[r], c)], sts[a].at[r % 2], st_sem.at[a, r % 2])

        sends1 = [p1(a, r, sibling) for a in range(3) for r in range(4)]
        for cp in sends1:
            cp.start()
        sends2 = []
        for a in range(3):
            stage(a, 1).start()
            for r in (1, 2, 3, 0):
                nxt = {1: 2, 2: 3, 3: 0, 0: None}[r]
                if nxt is not None:
                    stage(a, nxt).start()
                stage(a, r).wait()
                p1(a, r, me).wait_recv()
                chip_sum = r1s[a][r] + sts[a][r % 2]
                if r == 0:
                    r1s[a][0] = chip_sum
                else:
                    sbs[a][r - 1] = chip_sum.astype(BF16)
                    cp = p2(a, r, (*rel[r], c))
                    cp.start()
                    sends2.append(cp)
        _all_gather_blocks((ga_ref, gb_ref), ag_send, ag_recv)
        for a in range(3):
            for r in (1, 2, 3):
                p2(a, r, me).wait_recv()
            outs[a][...] = ((r1s[a][0] + r2s[a][0].astype(F32)) + r2s[a][1].astype(F32)) + r2s[a][2].astype(F32)
        for cp in sends1 + sends2:
            cp.wait_send()

    vm = pl.BlockSpec(memory_space=pltpu.VMEM)
    hbm = pl.BlockSpec(memory_space=pl.ANY)
    blk = [p.shape[1:] for p in big]
    scratch = ([pltpu.VMEM((4,) + b, F32) for b in blk] + [pltpu.VMEM((3,) + b, BF16) for b in blk]
               + [pltpu.VMEM((3,) + b, BF16) for b in blk] + [pltpu.VMEM((2,) + b, F32) for b in blk]
               + [pltpu.SemaphoreType.DMA((3, 4)), pltpu.SemaphoreType.DMA((3, 4)),
                  pltpu.SemaphoreType.DMA((3, 3)), pltpu.SemaphoreType.DMA((3, 3)),
                  pltpu.SemaphoreType.DMA((2, 7)), pltpu.SemaphoreType.DMA((2, 7)),
                  pltpu.SemaphoreType.DMA((3, 2))])
    return pl.pallas_call(
        body, name="reduce_grads",
        out_shape=tuple(jax.ShapeDtypeStruct(b, F32) for b in blk)
        + (jax.ShapeDtypeStruct((N_DEV,) + pack_a.shape, F32), jax.ShapeDtypeStruct((N_DEV,) + pack_b.shape, F32)),
        in_specs=[hbm, hbm, hbm, vm, vm], out_specs=(vm, vm, vm, vm, vm),
        scratch_shapes=scratch,
        compiler_params=_cp(),
    )(p_in, p_out, p_conv, pack_a, pack_b)


def _rms_hat(xf):
    r = lax.rsqrt(jnp.mean(xf * xf, axis=-1, keepdims=True) + NORM_EPS)
    return xf * r, r


def _in_proj(x2, norm_w, g_in, tm):
    s = x2.shape[0]

    def body(x_ref, nw_ref, g_hbm, pm_ref, pb_ref, wt_hbm, g_vmem, wt_vmem, sem):
        @pl.when(pl.program_id(0) == 0)
        def _():
            cp = pltpu.make_async_copy(g_hbm, g_vmem, sem)
            cp.start()
            wt_vmem[D_MAIN:, :] = jnp.zeros((D_IN_PAD - D_MAIN, D_MODEL), BF16)
            cp.wait()
            for d in range(N_DEV):
                wt_vmem[W_IN_SHARD * d:W_IN_SHARD * (d + 1), :] = g_vmem[d]
            out = pltpu.make_async_copy(wt_vmem, wt_hbm, sem)
            out.start()
            out.wait()
        xhat, _ = _rms_hat(x_ref[...])
        n = (xhat * nw_ref[...]).astype(BF16)
        pm_ref[...] = _dot_nt_bf(n, wt_vmem[:D_MAIN, :])
        pb_ref[...] = _dot_nt_bf(n, wt_vmem[D_MAIN:, :])

    return pl.pallas_call(
        body, name="in_proj", grid=(s // tm,),
        out_shape=(jax.ShapeDtypeStruct((s, D_MAIN), F32), jax.ShapeDtypeStruct((s, 128), F32),
                   jax.ShapeDtypeStruct((D_IN_PAD, D_MODEL), BF16)),
        in_specs=[pl.BlockSpec((tm, D_MODEL), lambda i: (i, 0)),
                  pl.BlockSpec((1, D_MODEL), lambda i: (0, 0)),
                  pl.BlockSpec(memory_space=pl.ANY)],
        out_specs=(pl.BlockSpec((tm, D_MAIN), lambda i: (i, 0)), pl.BlockSpec((tm, 128), lambda i: (i, 0)),
                   pl.BlockSpec(memory_space=pl.ANY)),
        scratch_shapes=[pltpu.VMEM((N_DEV, W_IN_SHARD, D_MODEL), BF16), pltpu.VMEM((D_IN_PAD, D_MODEL), BF16),
                        pltpu.SemaphoreType.DMA],
        compiler_params=_cp(("arbitrary",)),
    )(x2, norm_w, g_in)


def _shift_down(cur, prev_tail, s):
    ext = jnp.concatenate([prev_tail, cur], axis=0)
    return pltpu.roll(ext, s, 0)[HALO:, :]


def _shift_up(cur, next_head, s):
    ext = jnp.concatenate([cur, next_head], axis=0)
    n = ext.shape[0]
    return pltpu.roll(ext, n - s, 0)[:cur.shape[0], :]


def _pool_counts(i, tp, w):
    t = i * tp + lax.broadcasted_iota(jnp.int32, (tp, 1), 0)
    return jnp.minimum(t + 1, w).astype(F32)


def _pool_mix(u, u_prev_tail, i, tp):
    ext = jnp.concatenate([u_prev_tail, u], axis=0)
    w2 = ext + pltpu.roll(ext, 1, 0)
    w4 = w2 + pltpu.roll(w2, 2, 0)
    w8 = w4 + pltpu.roll(w4, 4, 0)
    w16 = w8 + pltpu.roll(w8, 8, 0)
    mixes = []
    for gi, (w, win) in enumerate(zip(POOL_WINDOWS, (w2, w4, w8, w16))):
        cols = slice(gi * POOL_GROUP, (gi + 1) * POOL_GROUP)
        mixes.append(win[HALO:, cols] / _pool_counts(i, tp, w) - u[:, cols])
    return mixes


def _prev_halo_spec(tp, width, col):
    per = tp // HALO
    return pl.BlockSpec((HALO, width), lambda i: (jnp.maximum(i * per - 1, 0), col))


def _next_halo_spec(tp, width, col, n):
    per = tp // HALO
    return pl.BlockSpec((HALO, width), lambda i: (jnp.minimum((i + 1) * per, n * per - 1), col))


def _prev_tail(ref, i):
    return jnp.where(i > 0, ref[...], 0.0)


def _next_head(ref, i, n):
    return jnp.where(i < n - 1, ref[...], 0.0)


def _pool_fwd(proj_main, pool_w, pool_scale, tp):
    s = proj_main.shape[0]

    def body(u_ref, up_ref, z_ref, pw_ref, ps_ref, y_ref):
        i = pl.program_id(0)
        u = u_ref[...]
        mixes = _pool_mix(u, _prev_tail(up_ref, i), i, tp)
        gate = ps_ref[...] * _silu(z_ref[...])
        for gi in range(4):
            cols = slice(gi * POOL_GROUP, (gi + 1) * POOL_GROUP)
            y_ref[:, cols] = _dot_bf(mixes[gi], pw_ref[gi]) * gate[:, cols]

    return pl.pallas_call(
        body, name="pool_fwd", grid=(s // tp,),
        out_shape=jax.ShapeDtypeStruct((s, D_POOL), F32),
        in_specs=[pl.BlockSpec((tp, D_POOL), lambda i: (i, 0)),
                  _prev_halo_spec(tp, D_POOL, 0),
                  pl.BlockSpec((tp, D_POOL), lambda i: (i, 1)),
                  pl.BlockSpec((4, POOL_GROUP, POOL_GROUP), lambda i: (0, 0, 0)),
                  pl.BlockSpec((1, D_POOL), lambda i: (0, 0))],
        out_specs=pl.BlockSpec((tp, D_POOL), lambda i: (i, 0)),
        compiler_params=_cp(("parallel",)),
    )(proj_main, proj_main, proj_main, pool_w, pool_scale)


def _conv_fwd(cur, prev_tail, w4):
    ext = jnp.concatenate([prev_tail, cur], axis=0)
    y = ext * w4[CONV_WIDTH - 1:CONV_WIDTH, :]
    for sft in range(1, CONV_WIDTH):
        y = y + pltpu.roll(ext, sft, 0) * w4[CONV_WIDTH - 1 - sft:CONV_WIDTH - sft, :]
    return y[HALO:, :]


def _l2n_heads(t):
    parts = []
    for h in range(DN_HEADS):
        th = t[:, h * DN_HEAD_DIM:(h + 1) * DN_HEAD_DIM]
        parts.append(th * lax.rsqrt(jnp.sum(th * th, axis=-1, keepdims=True) + NORM_EPS))
    return jnp.concatenate(parts, axis=1)


def _post_conv(yq, yk, yv):
    return _l2n_heads(_silu(yq)), _l2n_heads(_silu(yk)), _silu(yv)


def _gates(ba, alog_lane, dtb_lane):
    lane = lax.broadcasted_iota(jnp.int32, ba.shape, 1)
    beta = jax.nn.sigmoid(ba)
    g = -jnp.exp(alog_lane) * _softplus(ba + dtb_lane)
    return jnp.where(lane < DN_HEADS, beta, jnp.where(lane < 2 * DN_HEADS, g, 0.0))


def _qkv_specs(tp, which):
    def spec(col, n=None):
        if which == 0:
            return pl.BlockSpec((tp, D_DN), lambda i: (i, col))
        if which < 0:
            return _prev_halo_spec(tp, D_DN, col)
        return _next_halo_spec(tp, D_DN, col, n)
    return spec


def _dn_pre(proj_main, proj_ba, conv_full, alog_lane, dtb_lane, tp):
    s = proj_main.shape[0]

    def body(q_ref, k_ref, v_ref, qp_ref, kp_ref, vp_ref, cw_ref, ba_ref, al_ref, db_ref,
             qn_ref, kn_ref, vv_ref, gb_ref):
        i = pl.program_id(0)
        ys = []
        for j, (cur, prev) in enumerate(((q_ref, qp_ref), (k_ref, kp_ref), (v_ref, vp_ref))):
            ys.append(_conv_fwd(cur[...], _prev_tail(prev, i), cw_ref[:, j * D_DN:(j + 1) * D_DN]))
        qn, kn, vv = _post_conv(*ys)
        qn_ref[...] = qn
        kn_ref[...] = kn
        vv_ref[...] = vv
        gb_ref[...] = _gates(ba_ref[...], al_ref[...], db_ref[...])

    cur, prev = _qkv_specs(tp, 0), _qkv_specs(tp, -1)
    row = pl.BlockSpec((1, 128), lambda i: (0, 0))
    tile = pl.BlockSpec((tp, D_DN), lambda i: (i, 0))
    return pl.pallas_call(
        body, name="dn_pre", grid=(s // tp,),
        out_shape=(jax.ShapeDtypeStruct((s, D_DN), F32),) * 3 + (jax.ShapeDtypeStruct((s, 128), F32),),
        in_specs=[cur(2), cur(3), cur(4), prev(2), prev(3), prev(4),
                  pl.BlockSpec((CONV_WIDTH, 3 * D_DN), lambda i: (0, 0)),
                  pl.BlockSpec((tp, 128), lambda i: (i, 0)), row, row],
        out_specs=(tile, tile, tile, pl.BlockSpec((tp, 128), lambda i: (i, 0))),
        compiler_params=_cp(("parallel",)),
    )(proj_main, proj_main, proj_main, proj_main, proj_main, proj_main, conv_full, proj_ba, alog_lane, dtb_lane)


def _dn_block(q, k, v, gcol, bcol, state, dz, nw):
    nb, n, d = q.shape
    ii = lax.broadcasted_iota(jnp.int32, (n, n), 0)
    jj = lax.broadcasted_iota(jnp.int32, (n, n), 1)
    lower = ii >= jj
    eye = (ii == jj).astype(F32)
    g_row = jnp.sum(eye * gcol, axis=1, keepdims=True)
    gc_col = jnp.sum(jnp.where(lower, g_row, 0.0), axis=2, keepdims=True)
    gc_row = jnp.sum(eye * gc_col, axis=1, keepdims=True)
    decay = jnp.where(lower, jnp.exp(jnp.where(lower, gc_col - gc_row, 0.0)), 0.0)
    kb = k * bcol
    vb = v * bcol
    qs = q * (DN_HEAD_DIM ** -0.5)
    egc = jnp.exp(gc_col)
    akq = _mm(jnp.concatenate([kb, qs], axis=1), k, 1, 1, *_DN_PREC["akq"])
    a = jnp.where(ii > jj, akq[:, :n] * decay, 0.0)
    qk = akq[:, n:] * decay
    t = _tri_inv(a, *_DN_PREC["inv"])
    uw = _mm(t, jnp.concatenate([vb, kb * egc], axis=2), 1, 0, *_DN_PREC["uw"])
    wq = jnp.concatenate([uw[:, :, d:], qs * egc], axis=1)
    g_last = gc_col[:, n - 1:n, :]
    k_dec = k * jnp.exp(g_last - gc_col)
    e_last = jnp.exp(g_last)
    os_ = []
    for c in range(nb // DN_HEADS):
        sl = slice(c * DN_HEADS, (c + 1) * DN_HEADS)
        ws = _mm(wq[sl], state, 1, 0, *_DN_PREC["ws"])
        v_new = uw[sl, :, :d] - ws[:, :n]
        os_.append(ws[:, n:] + _mm(qk[sl], v_new, 1, 0, *_DN_PREC["ov"]))
        state = state * e_last[sl] + _mm(k_dec[sl], v_new, 0, 0, *_DN_PREC["st"])
    o = jnp.concatenate(os_, axis=0)
    y = o * lax.rsqrt(jnp.mean(o * o, axis=-1, keepdims=True) + NORM_EPS) * nw * _silu(dz)
    return y, state


def _dn_block_args(gc, q_ref, k_ref, v_ref, gb_ref, dz_ref):
    qs, ks, vs, gs, bs, zs = [], [], [], [], [], []
    for cc in range(gc):
        r = slice(cc * CHUNK, (cc + 1) * CHUNK)
        gbv = gb_ref[r, :]
        for h in range(DN_HEADS):
            cols = slice(h * DN_HEAD_DIM, (h + 1) * DN_HEAD_DIM)
            qs.append(q_ref[r, cols])
            ks.append(k_ref[r, cols])
            vs.append(v_ref[r, cols])
            zs.append(dz_ref[r, cols])
            gs.append(gbv[:, DN_HEADS + h:DN_HEADS + h + 1])
            bs.append(gbv[:, h:h + 1])
    return tuple(jnp.stack(t, axis=0) for t in (qs, ks, vs, gs, bs, zs))


def _dn_scan_fwd(qn, kn, vv, gb, proj_main, dn_norm_w, gc):
    s = qn.shape[0]
    nchunk = s // CHUNK
    rows = gc * CHUNK

    def body(q_ref, k_ref, v_ref, gb_ref, dz_ref, nw_ref, y_ref, ss_ref, state):
        @pl.when(pl.program_id(0) == 0)
        def _():
            state[...] = jnp.zeros_like(state)
        q, k, v, gcol, bcol, dz = _dn_block_args(gc, q_ref, k_ref, v_ref, gb_ref, dz_ref)
        st = state[...]
        ss_ref[0] = st
        y, new = _dn_block(q, k, v, gcol, bcol, st, dz, nw_ref[...])
        state[...] = new
        for cc in range(gc):
            for h in range(DN_HEADS):
                y_ref[cc * CHUNK:(cc + 1) * CHUNK, h * DN_HEAD_DIM:(h + 1) * DN_HEAD_DIM] = y[cc * DN_HEADS + h]

    tile = pl.BlockSpec((rows, D_DN), lambda i: (i, 0))
    return pl.pallas_call(
        body, name="dn_scan_fwd", grid=(nchunk // gc,),
        out_shape=(jax.ShapeDtypeStruct((s, D_DN), F32),
                   jax.ShapeDtypeStruct((nchunk // gc, DN_HEADS, DN_HEAD_DIM, DN_HEAD_DIM), F32)),
        in_specs=[tile, tile, tile, pl.BlockSpec((rows, 128), lambda i: (i, 0)),
                  pl.BlockSpec((rows, D_DN), lambda i: (i, 5)), pl.BlockSpec((1, 128), lambda i: (0, 0))],
        out_specs=(tile, pl.BlockSpec((1, DN_HEADS, DN_HEAD_DIM, DN_HEAD_DIM), lambda i: (i, 0, 0, 0))),
        scratch_shapes=[pltpu.VMEM((DN_HEADS, DN_HEAD_DIM, DN_HEAD_DIM), F32)],
        compiler_params=_cp(("arbitrary",)),
    )(qn, kn, vv, gb, proj_main, dn_norm_w)


def _out_proj_loss(y_pool, y_dn, x2, tgt, w_out_full, fnw, tm):
    s = x2.shape[0]

    def body(yp_ref, yd_ref, x_ref, t_ref, wo_ref, fw_ref,
             dh_ref, dyp_ref, dyd_ref, gwo_ref, gfw_ref, loss_ref):
        @pl.when(pl.program_id(0) == 0)
        def _():
            gwo_ref[...] = jnp.zeros_like(gwo_ref)
            gfw_ref[...] = jnp.zeros_like(gfw_ref)
            loss_ref[...] = jnp.zeros_like(loss_ref)
        y = jnp.concatenate([yp_ref[...], yd_ref[...]], axis=1).astype(BF16)
        wo = wo_ref[...]
        h = x_ref[...] + jnp.dot(y, wo, preferred_element_type=F32)
        hn, r = _rms_hat(h)
        fw = fw_ref[...]
        err = hn * fw - t_ref[...]
        loss_ref[...] += 0.5 * jnp.sum(jnp.sum(err * err, axis=-1, keepdims=True) / D_MODEL, axis=0, keepdims=True)
        dout = err / D_MODEL
        gfw_ref[...] += jnp.sum(dout * hn, axis=0, keepdims=True)
        dhn = dout * fw
        dh = r * (dhn - hn * jnp.mean(dhn * hn, axis=-1, keepdims=True))
        dh_ref[...] = dh
        dhb = dh.astype(BF16)
        dy = _dot_nt_bf(dhb, wo)
        dyp_ref[...] = dy[:, :D_POOL]
        dyd_ref[...] = dy[:, D_POOL:]
        gwo_ref[...] += _dot_tn_bf(y, dhb)

    half = pl.BlockSpec((tm, D_POOL), lambda i: (i, 0))
    full = pl.BlockSpec((tm, D_MODEL), lambda i: (i, 0))
    return pl.pallas_call(
        body, name="out_proj_loss", grid=(s // tm,),
        out_shape=(jax.ShapeDtypeStruct((s, D_MODEL), F32), jax.ShapeDtypeStruct((s, D_POOL), F32),
                   jax.ShapeDtypeStruct((s, D_DN), F32), jax.ShapeDtypeStruct((D_MODEL, D_MODEL), F32),
                   jax.ShapeDtypeStruct((1, D_MODEL), F32), jax.ShapeDtypeStruct((1, 128), F32)),
        in_specs=[half, half, full, full, pl.BlockSpec((D_MODEL, D_MODEL), lambda i: (0, 0)),
                  pl.BlockSpec((1, D_MODEL), lambda i: (0, 0))],
        out_specs=(full, half, half, pl.BlockSpec((D_MODEL, D_MODEL), lambda i: (0, 0)),
                   pl.BlockSpec((1, D_MODEL), lambda i: (0, 0)), pl.BlockSpec((1, 128), lambda i: (0, 0))),
        compiler_params=_cp(("arbitrary",)),
    )(y_pool, y_dn, x2, tgt, w_out_full, fnw)


def _dn_scan_bwd(qn, kn, vv, gb, proj_main, dn_norm_w, states, dy_dn, gc):
    s = qn.shape[0]
    nchunk = s // CHUNK
    nstep = nchunk // gc
    rows = gc * CHUNK

    def body(q_ref, k_ref, v_ref, gb_ref, dz_ref, nw_ref, ss_ref, dy_ref,
             dq_ref, dk_ref, dv_ref, dgb_ref, ddz_ref, dnw_ref, dstate):
        @pl.when(pl.program_id(0) == 0)
        def _():
            dstate[...] = jnp.zeros_like(dstate)
            dnw_ref[...] = jnp.zeros_like(dnw_ref)
        lane = lax.broadcasted_iota(jnp.int32, (CHUNK, 128), 1)
        q, k, v, gcol, bcol, dz = _dn_block_args(gc, q_ref, k_ref, v_ref, gb_ref, dz_ref)
        dy = jnp.stack([dy_ref[cc * CHUNK:(cc + 1) * CHUNK, h * DN_HEAD_DIM:(h + 1) * DN_HEAD_DIM]
                        for cc in range(gc) for h in range(DN_HEADS)], axis=0)
        _, vjp = jax.vjp(_dn_block, q, k, v, gcol, bcol, ss_ref[0], dz, nw_ref[...])
        dq, dk, dv, dg, db, dst, ddz, dnw = vjp((dy, dstate[...]))
        dstate[...] = dst
        dnw_ref[...] += dnw
        for cc in range(gc):
            r = slice(cc * CHUNK, (cc + 1) * CHUNK)
            dgb = jnp.zeros((CHUNK, 128), F32)
            for h in range(DN_HEADS):
                b = cc * DN_HEADS + h
                cols = slice(h * DN_HEAD_DIM, (h + 1) * DN_HEAD_DIM)
                for ref, val in zip((dq_ref, dk_ref, dv_ref, ddz_ref), (dq, dk, dv, ddz)):
                    ref[r, cols] = val[b]
                dgb = dgb + jnp.where(lane == h, db[b], 0.0) + jnp.where(lane == DN_HEADS + h, dg[b], 0.0)
            dgb_ref[r, :] = dgb

    rev = lambda i: (nstep - 1 - i, 0)
    tile = pl.BlockSpec((rows, D_DN), rev)
    lanes = pl.BlockSpec((rows, 128), rev)
    return pl.pallas_call(
        body, name="dn_scan_bwd", grid=(nstep,),
        out_shape=(jax.ShapeDtypeStruct((s, D_DN), F32),) * 3
        + (jax.ShapeDtypeStruct((s, 128), F32), jax.ShapeDtypeStruct((s, D_DN), F32),
           jax.ShapeDtypeStruct((1, 128), F32)),
        in_specs=[tile, tile, tile, lanes, pl.BlockSpec((rows, D_DN), lambda i: (nstep - 1 - i, 5)),
                  pl.BlockSpec((1, 128), lambda i: (0, 0)),
                  pl.BlockSpec((1, DN_HEADS, DN_HEAD_DIM, DN_HEAD_DIM), lambda i: (nstep - 1 - i, 0, 0, 0)), tile],
        out_specs=(tile, tile, tile, lanes, tile, pl.BlockSpec((1, 128), lambda i: (0, 0))),
        scratch_shapes=[pltpu.VMEM((DN_HEADS, DN_HEAD_DIM, DN_HEAD_DIM), F32)],
        compiler_params=_cp(("arbitrary",)),
    )(qn, kn, vv, gb, proj_main, dn_norm_w, states, dy_dn)


def _dn_pre_bwd1(proj_main, proj_ba, conv_full, alog_lane, dtb_lane, dqn, dkn, dvv, dgb, tp):
    s = proj_main.shape[0]

    def body(q_ref, k_ref, v_ref, qp_ref, kp_ref, vp_ref, cw_ref, ba_ref, al_ref, db_ref,
             dqn_ref, dkn_ref, dvv_ref, dgb_ref,
             dcq_ref, dck_ref, dcv_ref, dba_ref, dcw_ref, dal_ref, ddb_ref):
        i = pl.program_id(0)

        @pl.when(i == 0)
        def _():
            dcw_ref[...] = jnp.zeros_like(dcw_ref)
            dal_ref[...] = jnp.zeros_like(dal_ref)
            ddb_ref[...] = jnp.zeros_like(ddb_ref)
        curs = (q_ref[...], k_ref[...], v_ref[...])
        tails = (_prev_tail(qp_ref, i), _prev_tail(kp_ref, i), _prev_tail(vp_ref, i))
        ys = [_conv_fwd(curs[j], tails[j], cw_ref[:, j * D_DN:(j + 1) * D_DN]) for j in range(3)]
        _, vjp = jax.vjp(_post_conv, *ys)
        dys = vjp((dqn_ref[...], dkn_ref[...], dvv_ref[...]))
        for j, (dy, out) in enumerate(zip(dys, (dcq_ref, dck_ref, dcv_ref))):
            out[...] = dy
            for sft in range(CONV_WIDTH):
                xs = curs[j] if sft == 0 else _shift_down(curs[j], tails[j], sft)
                row = CONV_WIDTH - 1 - sft
                dcw_ref[row:row + 1, j * D_DN:(j + 1) * D_DN] += jnp.sum(dy * xs, axis=0, keepdims=True)
        _, gvjp = jax.vjp(_gates, ba_ref[...], al_ref[...], db_ref[...])
        dba, dal, ddb = gvjp(dgb_ref[...])
        dba_ref[...] = dba
        dal_ref[...] += dal
        ddb_ref[...] += ddb

    cur, prev = _qkv_specs(tp, 0), _qkv_specs(tp, -1)
    row = pl.BlockSpec((1, 128), lambda i: (0, 0))
    tile = pl.BlockSpec((tp, D_DN), lambda i: (i, 0))
    lanes = pl.BlockSpec((tp, 128), lambda i: (i, 0))
    cw = pl.BlockSpec((CONV_WIDTH, 3 * D_DN), lambda i: (0, 0))
    return pl.pallas_call(
        body, name="dn_pre_bwd1", grid=(s // tp,),
        out_shape=(jax.ShapeDtypeStruct((s, D_DN), F32),) * 3
        + (jax.ShapeDtypeStruct((s, 128), F32), jax.ShapeDtypeStruct((CONV_WIDTH, 3 * D_DN), F32),
           jax.ShapeDtypeStruct((1, 128), F32), jax.ShapeDtypeStruct((1, 128), F32)),
        in_specs=[cur(2), cur(3), cur(4), prev(2), prev(3), prev(4), cw, lanes, row, row, tile, tile, tile, lanes],
        out_specs=(tile, tile, tile, lanes, cw, row, row),
        compiler_params=_cp(("arbitrary",)),
    )(proj_main, proj_main, proj_main, proj_main, proj_main, proj_main, conv_full, proj_ba, alog_lane, dtb_lane,
      dqn, dkn, dvv, dgb)


def _conv_bwd_x(dcq, dck, dcv, conv_full, tp):
    s = dcq.shape[0]
    n = s // tp

    def body(q_ref, k_ref, v_ref, qn_ref, kn_ref, vn_ref, cw_ref, dq_ref, dk_ref, dv_ref):
        i = pl.program_id(0)
        for j, (cur, nxt, out) in enumerate(((q_ref, qn_ref, dq_ref), (k_ref, kn_ref, dk_ref), (v_ref, vn_ref, dv_ref))):
            w4 = cw_ref[:, j * D_DN:(j + 1) * D_DN]
            dy = cur[...]
            head = _next_head(nxt, i, n)
            dx = dy * w4[CONV_WIDTH - 1:CONV_WIDTH, :]
            for sft in range(1, CONV_WIDTH):
                dx = dx + _shift_up(dy, head, sft) * w4[CONV_WIDTH - 1 - sft:CONV_WIDTH - sft, :]
            out[...] = dx

    tile = pl.BlockSpec((tp, D_DN), lambda i: (i, 0))
    nxt = _next_halo_spec(tp, D_DN, 0, n)
    return pl.pallas_call(
        body, name="conv_bwd_x", grid=(n,),
        out_shape=(jax.ShapeDtypeStruct((s, D_DN), F32),) * 3,
        in_specs=[tile, tile, tile, nxt, nxt, nxt, pl.BlockSpec((CONV_WIDTH, 3 * D_DN), lambda i: (0, 0))],
        out_specs=(tile, tile, tile),
        compiler_params=_cp(("parallel",)),
    )(dcq, dck, dcv, dcq, dck, dcv, conv_full)


def _pool_bwd1(proj_main, pool_w, pool_scale, dyp, tp):
    s = proj_main.shape[0]

    def body(u_ref, up_ref, z_ref, pw_ref, ps_ref, dy_ref, dz_ref, dwin_ref, dpw_ref, dps_ref):
        i = pl.program_id(0)

        @pl.when(i == 0)
        def _():
            dpw_ref[...] = jnp.zeros_like(dpw_ref)
            dps_ref[...] = jnp.zeros_like(dps_ref)
        u = u_ref[...]
        z = z_ref[...]
        dy = dy_ref[...]
        ps = ps_ref[...]
        mixes = _pool_mix(u, _prev_tail(up_ref, i), i, tp)
        sg = jax.nn.sigmoid(z)
        sz = z * sg
        dsz = sg * (1.0 + z * (1.0 - sg))
        for gi, w in enumerate(POOL_WINDOWS):
            cols = slice(gi * POOL_GROUP, (gi + 1) * POOL_GROUP)
            mixw = _dot_bf(mixes[gi], pw_ref[gi])
            dmixw = dy[:, cols] * ps[:, cols] * sz[:, cols]
            dps_ref[:, cols] += jnp.sum(dy[:, cols] * mixw * sz[:, cols], axis=0, keepdims=True)
            dz_ref[:, cols] = dy[:, cols] * mixw * ps[:, cols] * dsz[:, cols]
            dpw_ref[gi] += _dot_tn_bf(mixes[gi], dmixw)
            dmix = _dot_nt_bf(dmixw, pw_ref[gi])
            dwin_ref[:, cols] = dmix / _pool_counts(i, tp, w)

    tile = pl.BlockSpec((tp, D_POOL), lambda i: (i, 0))
    pw = pl.BlockSpec((4, POOL_GROUP, POOL_GROUP), lambda i: (0, 0, 0))
    ps = pl.BlockSpec((1, D_POOL), lambda i: (0, 0))
    return pl.pallas_call(
        body, name="pool_bwd1", grid=(s // tp,),
        out_shape=(jax.ShapeDtypeStruct((s, D_POOL), F32), jax.ShapeDtypeStruct((s, D_POOL), F32),
                   jax.ShapeDtypeStruct((4, POOL_GROUP, POOL_GROUP), F32), jax.ShapeDtypeStruct((1, D_POOL), F32)),
        in_specs=[tile, _prev_halo_spec(tp, D_POOL, 0),
                  pl.BlockSpec((tp, D_POOL), lambda i: (i, 1)), pw, ps, tile],
        out_specs=(tile, tile, pw, ps),
        compiler_params=_cp(("arbitrary",)),
    )(proj_main, proj_main, proj_main, pool_w, pool_scale, dyp)


def _pool_bwd2(dwin, tp):
    s = dwin.shape[0]
    n = s // tp

    def body(d_ref, dn_ref, du_ref):
        i = pl.program_id(0)
        ext = jnp.concatenate([d_ref[...], _next_head(dn_ref, i, n)], axis=0)
        m = ext.shape[0]
        a2 = ext + pltpu.roll(ext, m - 1, 0)
        a4 = a2 + pltpu.roll(a2, m - 2, 0)
        a8 = a4 + pltpu.roll(a4, m - 4, 0)
        a16 = a8 + pltpu.roll(a8, m - 8, 0)
        for gi, (w, acc) in enumerate(zip(POOL_WINDOWS, (a2, a4, a8, a16))):
            cols = slice(gi * POOL_GROUP, (gi + 1) * POOL_GROUP)
            du_ref[:, cols] = acc[:tp, cols] - d_ref[:, cols] * _pool_counts(i, tp, w)

    tile = pl.BlockSpec((tp, D_POOL), lambda i: (i, 0))
    return pl.pallas_call(
        body, name="pool_bwd2", grid=(n,),
        out_shape=jax.ShapeDtypeStruct((s, D_POOL), F32),
        in_specs=[tile, _next_halo_spec(tp, D_POOL, 0, n)],
        out_specs=tile,
        compiler_params=_cp(("parallel",)),
    )(dwin, dwin)


def _in_proj_bwd(dparts, dba, x2, dh, norm_w, wt_full, tm):
    s = x2.shape[0]
    nstep = s // tm

    def body(d0, d1, d2, d3, d4, d5, dba_ref, x_ref, dh_ref, nw_ref, wt_hbm,
             gx_ref, p_hbm, gnw_ref, wt_vmem, acc, blk, sem, osem):
        i = pl.program_id(0)

        @pl.when(i == 0)
        def _():
            cp = pltpu.make_async_copy(wt_hbm, wt_vmem, sem)
            cp.start()
            acc[...] = jnp.zeros_like(acc)
            gnw_ref[...] = jnp.zeros_like(gnw_ref)
            cp.wait()
        dbab = dba_ref[...].astype(BF16)
        xhat, r = _rms_hat(x_ref[...])
        nw = nw_ref[...]
        n = (xhat * nw).astype(BF16)
        acc[D_MAIN:, :] += _dot_tn_bf(dbab, n)
        dn = jnp.dot(dbab, wt_vmem[D_MAIN:, :], preferred_element_type=F32)
        for cb, d in enumerate((d0, d1, d2, d3, d4, d5)):
            rows = slice(cb * D_POOL, (cb + 1) * D_POOL)
            dpart = d[...].astype(BF16)
            acc[rows, :] += _dot_tn_bf(dpart, n)
            dn = dn + jnp.dot(dpart, wt_vmem[rows, :], preferred_element_type=F32)
        gnw_ref[...] += jnp.sum(dn * xhat, axis=0, keepdims=True)
        dxh = dn * nw
        gx_ref[...] = dh_ref[...] + r * (dxh - xhat * jnp.mean(dxh * xhat, axis=-1, keepdims=True))

        @pl.when(i == nstep - 1)
        def _():
            def out(d):
                return pltpu.make_async_copy(blk.at[d % 2], p_hbm.at[d], osem.at[d % 2])
            for d in range(N_DEV):
                if d >= 2:
                    out(d - 2).wait()
                blk[d % 2] = acc[W_IN_SHARD * d:W_IN_SHARD * (d + 1), :]
                out(d).start()
            out(N_DEV - 2).wait()
            out(N_DEV - 1).wait()

    part = pl.BlockSpec((tm, D_POOL), lambda i: (i, 0))
    full = pl.BlockSpec((tm, D_MODEL), lambda i: (i, 0))
    row = pl.BlockSpec((1, D_MODEL), lambda i: (0, 0))
    return pl.pallas_call(
        body, name="in_proj_bwd", grid=(nstep,),
        out_shape=(jax.ShapeDtypeStruct((s, D_MODEL), F32),
                   jax.ShapeDtypeStruct((N_DEV, W_IN_SHARD, D_MODEL), F32), jax.ShapeDtypeStruct((1, D_MODEL), F32)),
        in_specs=[part] * 6 + [pl.BlockSpec((tm, 128), lambda i: (i, 0)), full, full, row,
                               pl.BlockSpec(memory_space=pl.ANY)],
        out_specs=(full, pl.BlockSpec(memory_space=pl.ANY), row),
        scratch_shapes=[pltpu.VMEM((D_IN_PAD, D_MODEL), BF16), pltpu.VMEM((D_IN_PAD, D_MODEL), F32),
                        pltpu.VMEM((2, W_IN_SHARD, D_MODEL), F32), pltpu.SemaphoreType.DMA,
                        pltpu.SemaphoreType.DMA((2,))],
        compiler_params=_cp(("arbitrary",)),
    )(*dparts, dba, x2, dh, norm_w, wt_full)


def _adamw_math(w, g, m, v):
    m = ADAM_B1 * m + (1.0 - ADAM_B1) * g
    v = ADAM_B2 * v + (1.0 - ADAM_B2) * (g * g)
    m_hat = m / (1.0 - ADAM_B1 ** ADAM_STEP)
    v_hat = v / (1.0 - ADAM_B2 ** ADAM_STEP)
    delta = -ADAM_LR * (m_hat / (jnp.sqrt(v_hat) + ADAM_EPS) + ADAM_WD * w)
    return delta, m, v


def _adamw_sharded(params):
    k = len(params)

    def body(*refs):
        ins, outs = refs[:4 * k], refs[4 * k:]
        for p in range(k):
            w, g, m, v = (r[...] for r in ins[4 * p:4 * p + 4])
            d, nm, nv = _adamw_math(w, g, m, v)
            outs[3 * p][...] = d
            outs[3 * p + 1][...] = nm
            outs[3 * p + 2][...] = nv

    flat = [a for p in params for a in p]
    out_shape = tuple(jax.ShapeDtypeStruct(p[0].shape, F32) for p in params for _ in range(3))
    res = pl.pallas_call(body, name="adamw_sharded", out_shape=out_shape, compiler_params=_cp())(*flat)
    return [tuple(res[3 * p:3 * p + 3]) for p in range(k)]


def _adamw_replicated(gath_a, gath_b, pool, rows):
    nrow = len(rows)

    def body(*refs):
        ga_ref, gb_ref = refs[:2]
        ins = refs[2:2 + 3 * (nrow + 1)]
        outs = refs[2 + 3 * (nrow + 1):]

        def total(ref):
            g = ref[0]
            for d in range(1, N_DEV):
                g = g + ref[d]
            return g

        def update(g, wmv, o):
            w, m, v = (r[...] for r in wmv)
            dl, nm, nv = _adamw_math(w, g, m, v)
            o[0][...] = g
            o[1][...] = dl
            o[2][...] = nm
            o[3][...] = nv

        update(total(ga_ref), ins[:3], outs[:4])
        gb = total(gb_ref)
        for r in range(nrow):
            n = ins[3 * (r + 1)].shape[1]
            update(gb[r:r + 1, :n], ins[3 * (r + 1):3 * (r + 2)], outs[4 * (r + 1):4 * (r + 2)])
        outs[4 * (nrow + 1)][...] = gb[nrow:nrow + 1, 0:1]

    flat = list(pool) + [a for wmv in rows for a in wmv]
    out_shape = ((jax.ShapeDtypeStruct(pool[0].shape, F32),) * 4
                 + tuple(jax.ShapeDtypeStruct(wmv[0].shape, F32) for wmv in rows for _ in range(4))
                 + (jax.ShapeDtypeStruct((1, 1), F32),))
    res = pl.pallas_call(body, name="adamw_replicated", out_shape=out_shape, compiler_params=_cp())(
        gath_a, gath_b, *flat)
    return [res[4 * k:4 * k + 4] for k in range(nrow + 1)], res[-1]


_ROW_ORDER = ("norm_w", "final_norm_w", "pool_scale", "dn_norm_w", "a_log", "dt_bias")


def _pack_rows(vectors):
    out = [jnp.pad(v.reshape(-1), (0, D_MODEL - v.size)) for v in vectors]
    out += [jnp.zeros((D_MODEL,), F32)] * (8 - len(out))
    return jnp.stack(out, axis=0)


def _lane_row(vec4, start):
    return jnp.pad(vec4.reshape(-1), (start, 128 - start - vec4.size)).reshape(1, 128)


def kernel(x, norm_w, w_in, pool_w, pool_scale, conv_w, a_log, dt_bias, dn_norm_w, w_out, final_norm_w, loss_target, m_norm_w, m_w_in, m_pool_w, m_pool_scale, m_conv_w, m_a_log, m_dt_bias, m_dn_norm_w, m_w_out, m_final_norm_w, v_norm_w, v_w_in, v_pool_w, v_pool_scale, v_conv_w, v_a_log, v_dt_bias, v_dn_norm_w, v_w_out, v_final_norm_w):
    s = x.shape[1]
    tm = min(512, s)
    tmb = min(512, s)
    tp = min(512, s)
    x2 = x[0]
    tgt = loss_target[0]
    wt, m_wt, v_wt = w_in[0].T, m_w_in[0].T, v_w_in[0].T

    g_in, g_out, g_conv = _gather_weights(wt, w_out[0], conv_w[0])
    w_out_full = g_out.reshape(D_MODEL, D_MODEL)
    conv_full = g_conv.transpose(1, 0, 2).reshape(CONV_WIDTH, 3 * D_DN)
    alog_lane = _lane_row(a_log, DN_HEADS)
    dtb_lane = _lane_row(dt_bias, DN_HEADS)
    fnw = final_norm_w.reshape(1, D_MODEL)

    proj_main, proj_ba, wt_full = _in_proj(x2, norm_w, g_in, tm)
    y_pool = _pool_fwd(proj_main, pool_w[0], pool_scale, tp)
    qn, kn, vv, gb = _dn_pre(proj_main, proj_ba, conv_full, alog_lane, dtb_lane, tp)
    y_dn, states = _dn_scan_fwd(qn, kn, vv, gb, proj_main, dn_norm_w, DN_CHUNKS_PER_STEP)

    dh, dyp, dyd, g_wout, g_fnw, loss_part = _out_proj_loss(y_pool, y_dn, x2, tgt, w_out_full, fnw, tm)
    dqn, dkn, dvv, dgb, ddz, g_dnw = _dn_scan_bwd(qn, kn, vv, gb, proj_main, dn_norm_w, states, dyd,
                                                  DN_CHUNKS_PER_STEP)
    dcq, dck, dcv, dba, g_conv_full, g_al, g_db = _dn_pre_bwd1(
        proj_main, proj_ba, conv_full, alog_lane, dtb_lane, dqn, dkn, dvv, dgb, tp)
    dq, dk, dv = _conv_bwd_x(dcq, dck, dcv, conv_full, tp)
    dzp, dwin, g_pw, g_ps = _pool_bwd1(proj_main, pool_w[0], pool_scale, dyp, tp)
    dup = _pool_bwd2(dwin, tp)
    grad_x2, p_in, g_nw = _in_proj_bwd((dup, dzp, dq, dk, dv, ddz), dba, x2, dh, norm_w, wt_full, tmb)

    p_out = g_wout.reshape(N_DEV, D_MODEL // N_DEV, D_MODEL)
    p_conv = g_conv_full.reshape(CONV_WIDTH, N_DEV, 3 * D_DN // N_DEV).transpose(1, 0, 2)
    pack_a = g_pw.reshape(4 * POOL_GROUP, POOL_GROUP)
    pack_b = _pack_rows([g_nw, g_fnw, g_ps, g_dnw, g_al[0, DN_HEADS:2 * DN_HEADS], g_db[0, DN_HEADS:2 * DN_HEADS],
                         loss_part[0, :1]])
    gr_in, gr_out, gr_conv, gath_a, gath_b = _reduce_grads(p_in, p_out, p_conv, pack_a, pack_b)

    sharded = _adamw_sharded([(wt, gr_in, m_wt, v_wt), (w_out[0], gr_out, m_w_out[0], v_w_out[0]),
                              (conv_w[0], gr_conv, m_conv_w[0], v_conv_w[0])])
    r_in, r_out, r_conv = ((g,) + dmv for g, dmv in zip((gr_in, gr_out, gr_conv), sharded))
    flat = lambda a: a.reshape(4 * POOL_GROUP, POOL_GROUP)
    row = lambda a: a.reshape(1, -1)
    vecs = {"norm_w": (norm_w, m_norm_w, v_norm_w), "final_norm_w": (final_norm_w, m_final_norm_w, v_final_norm_w),
            "pool_scale": (pool_scale, m_pool_scale, v_pool_scale), "dn_norm_w": (dn_norm_w, m_dn_norm_w, v_dn_norm_w),
            "a_log": (a_log, m_a_log, v_a_log), "dt_bias": (dt_bias, m_dt_bias, v_dt_bias)}
    res, loss = _adamw_replicated(gath_a, gath_b, (flat(pool_w), flat(m_pool_w), flat(v_pool_w)),
                                  [tuple(row(a) for a in vecs[nm]) for nm in _ROW_ORDER])
    r_pool = res[0]
    r_vec = dict(zip(_ROW_ORDER, res[1:]))

    def group(k):
        vec = lambda nm: r_vec[nm][k].reshape(vecs[nm][0].shape)
        return (vec("norm_w"), r_in[k].T[None], r_pool[k].reshape(pool_w.shape), vec("pool_scale"), r_conv[k][None],
                vec("a_log"), vec("dt_bias"), vec("dn_norm_w"), r_out[k][None], vec("final_norm_w"))

    return (loss[0, 0], grad_x2[None], *group(0), *group(1), *group(2), *group(3))
```

```python
import functools

import jax
import jax.numpy as jnp
from jax import lax
from jax.experimental import pallas as pl
from jax.experimental.pallas import tpu as pltpu

F32 = jnp.float32
BF16 = jnp.bfloat16
HI = lax.Precision.HIGHEST
MESH = pl.DeviceIdType.MESH

D_MODEL = 1024
D_POOL = 512
D_DN = 512
POOL_WINDOWS = (2, 4, 8, 16)
POOL_GROUP = 128
DN_HEADS = 4
DN_HEAD_DIM = 128
CONV_WIDTH = 4
CHUNK = 64
NORM_EPS = 1e-6
D_IN = 3080
D_MAIN = 3072
D_IN_PAD = D_MAIN + 128
N_DEV = 8
W_IN_SHARD = D_IN // N_DEV
HALO = 16
DN_CHUNKS_PER_STEP = 8

ADAM_LR = 0.001
ADAM_B1 = 0.9
ADAM_B2 = 0.999
ADAM_EPS = 1e-08
ADAM_WD = 0.01
ADAM_STEP = 10

VMEM_LIMIT = 56 * 1024 * 1024


def _cp(sem=None, vmem=VMEM_LIMIT):
    kw = {"vmem_limit_bytes": vmem}
    if sem is not None:
        kw["dimension_semantics"] = sem
    return pltpu.CompilerParams(**kw)


def _dot_bf(a, b):
    return jnp.dot(a.astype(BF16), b.astype(BF16), preferred_element_type=F32)


def _dot_nt_bf(a, b):
    return lax.dot_general(a.astype(BF16), b.astype(BF16), (((1,), (1,)), ((), ())), preferred_element_type=F32)


def _dot_tn_bf(a, b):
    return lax.dot_general(a.astype(BF16), b.astype(BF16), (((0,), (0,)), ((), ())), preferred_element_type=F32)


def _mm_raw(a, b, ca, cb, prec):
    off = a.ndim - 2
    dn = (((ca + off,), (cb + off,)), ((0,), (0,)) if off else ((), ()))
    if prec == "hi":
        return lax.dot_general(a, b, dn, precision=HI, preferred_element_type=F32)
    ah, bh = a.astype(BF16), b.astype(BF16)
    out = lax.dot_general(ah, bh, dn, preferred_element_type=F32)
    if prec == "x3":
        al = (a - ah.astype(F32)).astype(BF16)
        bl = (b - bh.astype(F32)).astype(BF16)
        out = out + lax.dot_general(ah, bl, dn, preferred_element_type=F32)
        out = out + lax.dot_general(al, bh, dn, preferred_element_type=F32)
    return out


@functools.partial(jax.custom_vjp, nondiff_argnums=(2, 3, 4, 5))
def _mm(a, b, ca, cb, prec, bprec):
    return _mm_raw(a, b, ca, cb, prec)


def _mm_fwd(a, b, ca, cb, prec, bprec):
    return _mm_raw(a, b, ca, cb, prec), (a, b)


def _mm_bwd(ca, cb, prec, bprec, res, dc):
    a, b = res
    da = _mm_raw(dc, b, 1, 1 - cb, bprec) if ca == 1 else _mm_raw(b, dc, 1 - cb, 1, bprec)
    db = _mm_raw(a, dc, 1 - ca, 0, bprec) if cb == 0 else _mm_raw(dc, a, 0, 1 - ca, bprec)
    return da, db


_mm.defvjp(_mm_fwd, _mm_bwd)


@functools.partial(jax.custom_vjp, nondiff_argnums=(1, 2))
def _tri_inv(a, prec, bprec):
    n = a.shape[-1]
    ii = lax.broadcasted_iota(jnp.int32, (n, n), 0)
    jj = lax.broadcasted_iota(jnp.int32, (n, n), 1)
    p = (ii == jj).astype(F32) - a
    b = _mm_raw(a, a, 1, 0, prec)
    for _ in range(4):
        pb = _mm_raw(jnp.concatenate([p, b], axis=-2), b, 1, 0, prec)
        p = p + pb[..., :n, :]
        b = pb[..., n:, :]
    return p + _mm_raw(p, b, 1, 0, prec)


def _tri_inv_fwd(a, prec, bprec):
    t = _tri_inv(a, prec, bprec)
    return t, t


def _tri_inv_bwd(prec, bprec, t, dt):
    return (-_mm_raw(_mm_raw(t, dt, 0, 0, bprec), t, 1, 1, bprec),)


_tri_inv.defvjp(_tri_inv_fwd, _tri_inv_bwd)

_DN_PREC = {"akq": ("bf16", "bf16"), "inv": ("bf16", "bf16"), "uw": ("bf16", "bf16"), "ws": ("bf16", "bf16"),
            "ov": ("bf16", "bf16"), "st": ("bf16", "bf16")}


def _silu(x):
    return x * jax.nn.sigmoid(x)


def _softplus(x):
    pos = x > 0.0
    return jnp.where(pos, x, 0.0) + jnp.log1p(jnp.exp(jnp.where(pos, -x, x)))


def _mesh_pos():
    return lax.axis_index("x"), lax.axis_index("y"), lax.axis_index("c")


def _dev_index(x, y, c):
    return 4 * x + 2 * y + c


def _all_gather_blocks(outs, send_sems, recv_sems):
    x, y, c = _mesh_pos()
    me = (x, y, c)
    sibling = (x, y, 1 - c)
    chips = [(1 - x, y), (x, 1 - y), (1 - x, 1 - y)]

    def copy(a, k, block, to):
        rows = outs[a].at[_dev_index(*block)]
        return pltpu.make_async_remote_copy(src_ref=rows, dst_ref=rows, send_sem=send_sems.at[a, k],
                                            recv_sem=recv_sems.at[a, k], device_id=to, device_id_type=MESH)

    n = len(outs)
    first = []
    for a in range(n):
        first.append(copy(a, 0, me, sibling))
        for j, chip in enumerate(chips):
            first.append(copy(a, 1 + j, me, (*chip, c)))
    for cp in first:
        cp.start()
    passed = []
    for j, chip in enumerate(chips):
        for a in range(n):
            copy(a, 1 + j, (*chip, c), me).wait_recv()
            fwd = copy(a, 4 + j, (*chip, c), sibling)
            fwd.start()
            passed.append(fwd)
    for a in range(n):
        copy(a, 0, sibling, me).wait_recv()
        for j, chip in enumerate(chips):
            copy(a, 4 + j, (*chip, 1 - c), me).wait_recv()
    for cp in first + passed:
        cp.wait_send()


def _gather_weights(w_in_blk, w_out_blk, conv_blk):
    def body(win_ref, wout_ref, conv_ref, gin_ref, gout_ref, gconv_ref, send_sems, recv_sems):
        x, y, c = _mesh_pos()
        me = _dev_index(x, y, c)
        gin_ref[me] = win_ref[...].astype(BF16)
        gout_ref[me] = wout_ref[...].astype(BF16)
        gconv_ref[me] = conv_ref[...]
        _all_gather_blocks((gin_ref, gout_ref, gconv_ref), send_sems, recv_sems)

    vm = pl.BlockSpec(memory_space=pltpu.VMEM)
    return pl.pallas_call(
        body, name="gather_weights",
        out_shape=(jax.ShapeDtypeStruct((N_DEV,) + w_in_blk.shape, BF16),
                   jax.ShapeDtypeStruct((N_DEV,) + w_out_blk.shape, BF16),
                   jax.ShapeDtypeStruct((N_DEV,) + conv_blk.shape, F32)),
        in_specs=[vm, vm, vm], out_specs=(vm, vm, vm),
        scratch_shapes=[pltpu.SemaphoreType.DMA((3, 7)), pltpu.SemaphoreType.DMA((3, 7))],
        compiler_params=_cp(),
    )(w_in_blk, w_out_blk, conv_blk)


def _reduce_grads(p_in, p_out, p_conv, pack_a, pack_b):
    big = (p_in, p_out, p_conv)

    def body(pin_ref, pout_ref, pconv_ref, pa_ref, pb_ref,
             oin_ref, oout_ref, oconv_ref, ga_ref, gb_ref,
             r1_in, r1_out, r1_conv, r2_in, r2_out, r2_conv, sb_in, sb_out, sb_conv, st_in, st_out, st_conv,
             s1_send, s1_recv, s2_send, s2_recv, ag_send, ag_recv, st_sem):
        x, y, c = _mesh_pos()
        me = (x, y, c)
        sibling = (x, y, 1 - c)
        rel = [(x, y), (1 - x, y), (x, 1 - y), (1 - x, 1 - y)]
        srcs = (pin_ref, pout_ref, pconv_ref)
        r1s = (r1_in, r1_out, r1_conv)
        r2s = (r2_in, r2_out, r2_conv)
        sbs = (sb_in, sb_out, sb_conv)
        sts = (st_in, st_out, st_conv)
        outs = (oin_ref, oout_ref, oconv_ref)

        ga_ref[_dev_index(*me)] = pa_ref[...]
        gb_ref[_dev_index(*me)] = pb_ref[...]

        def p1(a, r, to):
            return pltpu.make_async_remote_copy(
                src_ref=srcs[a].at[_dev_index(*rel[r], 1 - c)], dst_ref=r1s[a].at[r],
                send_sem=s1_send.at[a, r], recv_sem=s1_recv.at[a, r], device_id=to, device_id_type=MESH)

        def p2(a, r, to):
            return pltpu.make_async_remote_copy(
                src_ref=sbs[a].at[r - 1], dst_ref=r2s[a].at[r - 1],
                send_sem=s2_send.at[a, r - 1], recv_sem=s2_recv.at[a, r - 1], device_id=to, device_id_type=MESH)

        def stage(a, r):
            return pltpu.make_async_copy(srcs[a].at[_dev_index(*rel[r], c)], sts[a].at[r % 2], st_sem.at[a, r % 2])

        sends1 = [p1(a, r, sibling) for a in range(3) for r in range(4)]
        for cp in sends1:
            cp.start()
        sends2 = []
        for a in range(3):
            stage(a, 1).start()
            for r in (1, 2, 3, 0):
                nxt = {1: 2, 2: 3, 3: 0, 0: None}[r]
                if nxt is not None:
                    stage(a, nxt).start()
                stage(a, r).wait()
                p1(a, r, me).wait_recv()
                chip_sum = r1s[a][r] + sts[a][r % 2]
                if r == 0:
                    r1s[a][0] = chip_sum
                else:
                    sbs[a][r - 1] = chip_sum.astype(BF16)
                    cp = p2(a, r, (*rel[r], c))
                    cp.start()
                    sends2.append(cp)
        _all_gather_blocks((ga_ref, gb_ref), ag_send, ag_recv)
        for a in range(3):
            for r in (1, 2, 3):
                p2(a, r, me).wait_recv()
            outs[a][...] = ((r1s[a][0] + r2s[a][0].astype(F32)) + r2s[a][1].astype(F32)) + r2s[a][2].astype(F32)
        for cp in sends1 + sends2:
            cp.wait_send()

    vm = pl.BlockSpec(memory_space=pltpu.VMEM)
    hbm = pl.BlockSpec(memory_space=pl.ANY)
    blk = [p.shape[1:] for p in big]
    scratch = ([pltpu.VMEM((4,) + b, F32) for b in blk] + [pltpu.VMEM((3,) + b, BF16) for b in blk]
               + [pltpu.VMEM((3,) + b, BF16) for b in blk] + [pltpu.VMEM((2,) + b, F32) for b in blk]
               + [pltpu.SemaphoreType.DMA((3, 4)), pltpu.SemaphoreType.DMA((3, 4)),
                  pltpu.SemaphoreType.DMA((3, 3)), pltpu.SemaphoreType.DMA((3, 3)),
                  pltpu.SemaphoreType.DMA((2, 7)), pltpu.SemaphoreType.DMA((2, 7)),
                  pltpu.SemaphoreType.DMA((3, 2))])
    return pl.pallas_call(
        body, name="reduce_grads",
        out_shape=tuple(jax.ShapeDtypeStruct(b, F32) for b in blk)
        + (jax.ShapeDtypeStruct((N_DEV,) + pack_a.shape, F32), jax.ShapeDtypeStruct((N_DEV,) + pack_b.shape, F32)),
        in_specs=[hbm, hbm, hbm, vm, vm], out_specs=(vm, vm, vm, vm, vm),
        scratch_shapes=scratch,
        compiler_params=_cp(),
    )(p_in, p_out, p_conv, pack_a, pack_b)


def _rms_hat(xf):
    r = lax.rsqrt(jnp.mean(xf * xf, axis=-1, keepdims=True) + NORM_EPS)
    return xf * r, r


def _in_proj(x2, norm_w, g_in, tm):
    s = x2.shape[0]

    def body(x_ref, nw_ref, g_hbm, pm_ref, pb_ref, wt_hbm, g_vmem, wt_vmem, sem):
        @pl.when(pl.program_id(0) == 0)
        def _():
            cp = pltpu.make_async_copy(g_hbm, g_vmem, sem)
            cp.start()
            wt_vmem[D_MAIN:, :] = jnp.zeros((D_IN_PAD - D_MAIN, D_MODEL), BF16)
            cp.wait()
            for d in range(N_DEV):
                wt_vmem[W_IN_SHARD * d:W_IN_SHARD * (d + 1), :] = g_vmem[d]
            out = pltpu.make_async_copy(wt_vmem, wt_hbm, sem)
            out.start()
            out.wait()
        xhat, _ = _rms_hat(x_ref[...])
        n = (xhat * nw_ref[...]).astype(BF16)
        pm_ref[...] = _dot_nt_bf(n, wt_vmem[:D_MAIN, :])
        pb_ref[...] = _dot_nt_bf(n, wt_vmem[D_MAIN:, :])

    return pl.pallas_call(
        body, name="in_proj", grid=(s // tm,),
        out_shape=(jax.ShapeDtypeStruct((s, D_MAIN), F32), jax.ShapeDtypeStruct((s, 128), F32),
                   jax.ShapeDtypeStruct((D_IN_PAD, D_MODEL), BF16)),
        in_specs=[pl.BlockSpec((tm, D_MODEL), lambda i: (i, 0)),
                  pl.BlockSpec((1, D_MODEL), lambda i: (0, 0)),
                  pl.BlockSpec(memory_space=pl.ANY)],
        out_specs=(pl.BlockSpec((tm, D_MAIN), lambda i: (i, 0)), pl.BlockSpec((tm, 128), lambda i: (i, 0)),
                   pl.BlockSpec(memory_space=pl.ANY)),
        scratch_shapes=[pltpu.VMEM((N_DEV, W_IN_SHARD, D_MODEL), BF16), pltpu.VMEM((D_IN_PAD, D_MODEL), BF16),
                        pltpu.SemaphoreType.DMA],
        compiler_params=_cp(("arbitrary",)),
    )(x2, norm_w, g_in)


def _shift_down(cur, prev_tail, s):
    ext = jnp.concatenate([prev_tail, cur], axis=0)
    return pltpu.roll(ext, s, 0)[HALO:, :]


def _shift_up(cur, next_head, s):
    ext = jnp.concatenate([cur, next_head], axis=0)
    n = ext.shape[0]
    return pltpu.roll(ext, n - s, 0)[:cur.shape[0], :]


def _pool_counts(i, tp, w):
    t = i * tp + lax.broadcasted_iota(jnp.int32, (tp, 1), 0)
    return jnp.minimum(t + 1, w).astype(F32)


def _pool_mix(u, u_prev_tail, i, tp):
    ext = jnp.concatenate([u_prev_tail, u], axis=0)
    w2 = ext + pltpu.roll(ext, 1, 0)
    w4 = w2 + pltpu.roll(w2, 2, 0)
    w8 = w4 + pltpu.roll(w4, 4, 0)
    w16 = w8 + pltpu.roll(w8, 8, 0)
    mixes = []
    for gi, (w, win) in enumerate(zip(POOL_WINDOWS, (w2, w4, w8, w16))):
        cols = slice(gi * POOL_GROUP, (gi + 1) * POOL_GROUP)
        mixes.append(win[HALO:, cols] / _pool_counts(i, tp, w) - u[:, cols])
    return mixes


def _prev_halo_spec(tp, width, col):
    per = tp // HALO
    return pl.BlockSpec((HALO, width), lambda i: (jnp.maximum(i * per - 1, 0), col))


def _next_halo_spec(tp, width, col, n):
    per = tp // HALO
    return pl.BlockSpec((HALO, width), lambda i: (jnp.minimum((i + 1) * per, n * per - 1), col))


def _prev_tail(ref, i):
    return jnp.where(i > 0, ref[...], 0.0)


def _next_head(ref, i, n):
    return jnp.where(i < n - 1, ref[...], 0.0)


def _pool_fwd(proj_main, pool_w, pool_scale, tp):
    s = proj_main.shape[0]

    def body(u_ref, up_ref, z_ref, pw_ref, ps_ref, y_ref):
        i = pl.program_id(0)
        u = u_ref[...]
        mixes = _pool_mix(u, _prev_tail(up_ref, i), i, tp)
        gate = ps_ref[...] * _silu(z_ref[...])
        for gi in range(4):
            cols = slice(gi * POOL_GROUP, (gi + 1) * POOL_GROUP)
            y_ref[:, cols] = _dot_bf(mixes[gi], pw_ref[gi]) * gate[:, cols]

    return pl.pallas_call(
        body, name="pool_fwd", grid=(s // tp,),
        out_shape=jax.ShapeDtypeStruct((s, D_POOL), F32),
        in_specs=[pl.BlockSpec((tp, D_POOL), lambda i: (i, 0)),
                  _prev_halo_spec(tp, D_POOL, 0),
                  pl.BlockSpec((tp, D_POOL), lambda i: (i, 1)),
                  pl.BlockSpec((4, POOL_GROUP, POOL_GROUP), lambda i: (0, 0, 0)),
                  pl.BlockSpec((1, D_POOL), lambda i: (0, 0))],
        out_specs=pl.BlockSpec((tp, D_POOL), lambda i: (i, 0)),
        compiler_params=_cp(("parallel",)),
    )(proj_main, proj_main, proj_main, pool_w, pool_scale)


def _conv_fwd(cur, prev_tail, w4):
    ext = jnp.concatenate([prev_tail, cur], axis=0)
    y = ext * w4[CONV_WIDTH - 1:CONV_WIDTH, :]
    for sft in range(1, CONV_WIDTH):
        y = y + pltpu.roll(ext, sft, 0) * w4[CONV_WIDTH - 1 - sft:CONV_WIDTH - sft, :]
    return y[HALO:, :]


def _l2n_heads(t):
    parts = []
    for h in range(DN_HEADS):
        th = t[:, h * DN_HEAD_DIM:(h + 1) * DN_HEAD_DIM]
        parts.append(th * lax.rsqrt(jnp.sum(th * th, axis=-1, keepdims=True) + NORM_EPS))
    return jnp.concatenate(parts, axis=1)


def _post_conv(yq, yk, yv):
    return _l2n_heads(_silu(yq)), _l2n_heads(_silu(yk)), _silu(yv)


def _gates(ba, alog_lane, dtb_lane):
    lane = lax.broadcasted_iota(jnp.int32, ba.shape, 1)
    beta = jax.nn.sigmoid(ba)
    g = -jnp.exp(alog_lane) * _softplus(ba + dtb_lane)
    return jnp.where(lane < DN_HEADS, beta, jnp.where(lane < 2 * DN_HEADS, g, 0.0))


def _front(x2, norm_w, g_in, pool_w, pool_scale, conv_full, alog_lane, dtb_lane, tm):
    s = x2.shape[0]

    def body(x_ref, nw_ref, pw_ref, ps_ref, cw_ref, al_ref, db_ref, g_hbm,
             pm_ref, pb_ref, yp_ref, qn_ref, kn_ref, vv_ref, gb_ref, wt_hbm,
             g_vmem, wt_vmem, tail_u, tail_qkv, sem):
        i = pl.program_id(0)

        @pl.when(i == 0)
        def _():
            cp = pltpu.make_async_copy(g_hbm, g_vmem, sem)
            cp.start()
            wt_vmem[D_MAIN:, :] = jnp.zeros((D_IN_PAD - D_MAIN, D_MODEL), BF16)
            tail_u[...] = jnp.zeros_like(tail_u)
            tail_qkv[...] = jnp.zeros_like(tail_qkv)
            cp.wait()
            for d in range(N_DEV):
                wt_vmem[W_IN_SHARD * d:W_IN_SHARD * (d + 1), :] = g_vmem[d]
            out = pltpu.make_async_copy(wt_vmem, wt_hbm, sem)
            out.start()
            out.wait()
        xhat, _ = _rms_hat(x_ref[...])
        n = (xhat * nw_ref[...]).astype(BF16)
        pm_ref[...] = _dot_nt_bf(n, wt_vmem[:D_MAIN, :])
        pb = _dot_nt_bf(n, wt_vmem[D_MAIN:, :])
        pb_ref[...] = pb
        u = pm_ref[:, :D_POOL]
        mixes = _pool_mix(u, tail_u[...], i, tm)
        tail_u[...] = u[tm - HALO:, :]
        gate = ps_ref[...] * _silu(pm_ref[:, D_POOL:2 * D_POOL])
        for gi in range(4):
            cols = slice(gi * POOL_GROUP, (gi + 1) * POOL_GROUP)
            yp_ref[:, cols] = _dot_bf(mixes[gi], pw_ref[gi]) * gate[:, cols]
        ys = []
        for c in range(3):
            cols = slice(c * D_DN, (c + 1) * D_DN)
            cur = pm_ref[:, 2 * D_POOL + c * D_DN:2 * D_POOL + (c + 1) * D_DN]
            ys.append(_conv_fwd(cur, tail_qkv[:, cols], cw_ref[:, cols]))
            tail_qkv[:, cols] = cur[tm - HALO:, :]
        qn, kn, vv = _post_conv(*ys)
        qn_ref[...] = qn
        kn_ref[...] = kn
        vv_ref[...] = vv
        gb_ref[...] = _gates(pb, al_ref[...], db_ref[...])

    tile = pl.BlockSpec((tm, D_DN), lambda i: (i, 0))
    lanes = pl.BlockSpec((tm, 128), lambda i: (i, 0))
    row = pl.BlockSpec((1, 128), lambda i: (0, 0))
    return pl.pallas_call(
        body, name="front", grid=(s // tm,),
        out_shape=(jax.ShapeDtypeStruct((s, D_MAIN), F32), jax.ShapeDtypeStruct((s, 128), F32),
                   jax.ShapeDtypeStruct((s, D_POOL), F32), jax.ShapeDtypeStruct((s, D_DN), F32),
                   jax.ShapeDtypeStruct((s, D_DN), F32), jax.ShapeDtypeStruct((s, D_DN), F32),
                   jax.ShapeDtypeStruct((s, 128), F32), jax.ShapeDtypeStruct((D_IN_PAD, D_MODEL), BF16)),
        in_specs=[pl.BlockSpec((tm, D_MODEL), lambda i: (i, 0)),
                  pl.BlockSpec((1, D_MODEL), lambda i: (0, 0)),
                  pl.BlockSpec((4, POOL_GROUP, POOL_GROUP), lambda i: (0, 0, 0)),
                  pl.BlockSpec((1, D_POOL), lambda i: (0, 0)),
                  pl.BlockSpec((CONV_WIDTH, 3 * D_DN), lambda i: (0, 0)), row, row,
                  pl.BlockSpec(memory_space=pl.ANY)],
        out_specs=(pl.BlockSpec((tm, D_MAIN), lambda i: (i, 0)), lanes, tile, tile, tile, tile, lanes,
                   pl.BlockSpec(memory_space=pl.ANY)),
        scratch_shapes=[pltpu.VMEM((N_DEV, W_IN_SHARD, D_MODEL), BF16), pltpu.VMEM((D_IN_PAD, D_MODEL), BF16),
                        pltpu.VMEM((HALO, D_POOL), F32), pltpu.VMEM((HALO, 3 * D_DN), F32),
                        pltpu.SemaphoreType.DMA],
        compiler_params=_cp(("arbitrary",)),
    )(x2, norm_w, pool_w, pool_scale, conv_full, alog_lane, dtb_lane, g_in)


def _qkv_specs(tp, which):
    def spec(col, n=None):
        if which == 0:
            return pl.BlockSpec((tp, D_DN), lambda i: (i, col))
        if which < 0:
            return _prev_halo_spec(tp, D_DN, col)
        return _next_halo_spec(tp, D_DN, col, n)
    return spec


def _dn_pre(proj_main, proj_ba, conv_full, alog_lane, dtb_lane, tp):
    s = proj_main.shape[0]

    def body(q_ref, k_ref, v_ref, qp_ref, kp_ref, vp_ref, cw_ref, ba_ref, al_ref, db_ref,
             qn_ref, kn_ref, vv_ref, gb_ref):
        i = pl.program_id(0)
        ys = []
        for j, (cur, prev) in enumerate(((q_ref, qp_ref), (k_ref, kp_ref), (v_ref, vp_ref))):
            ys.append(_conv_fwd(cur[...], _prev_tail(prev, i), cw_ref[:, j * D_DN:(j + 1) * D_DN]))
        qn, kn, vv = _post_conv(*ys)
        qn_ref[...] = qn
        kn_ref[...] = kn
        vv_ref[...] = vv
        gb_ref[...] = _gates(ba_ref[...], al_ref[...], db_ref[...])

    cur, prev = _qkv_specs(tp, 0), _qkv_specs(tp, -1)
    row = pl.BlockSpec((1, 128), lambda i: (0, 0))
    tile = pl.BlockSpec((tp, D_DN), lambda i: (i, 0))
    return pl.pallas_call(
        body, name="dn_pre", grid=(s // tp,),
        out_shape=(jax.ShapeDtypeStruct((s, D_DN), F32),) * 3 + (jax.ShapeDtypeStruct((s, 128), F32),),
        in_specs=[cur(2), cur(3), cur(4), prev(2), prev(3), prev(4),
                  pl.BlockSpec((CONV_WIDTH, 3 * D_DN), lambda i: (0, 0)),
                  pl.BlockSpec((tp, 128), lambda i: (i, 0)), row, row],
        out_specs=(tile, tile, tile, pl.BlockSpec((tp, 128), lambda i: (i, 0))),
        compiler_params=_cp(("parallel",)),
    )(proj_main, proj_main, proj_main, proj_main, proj_main, proj_main, conv_full, proj_ba, alog_lane, dtb_lane)


def _dn_block(q, k, v, gcol, bcol, state, dz, nw):
    nb, n, d = q.shape
    ii = lax.broadcasted_iota(jnp.int32, (n, n), 0)
    jj = lax.broadcasted_iota(jnp.int32, (n, n), 1)
    lower = ii >= jj
    eye = (ii == jj).astype(F32)
    g_row = jnp.sum(eye * gcol, axis=1, keepdims=True)
    gc_col = jnp.sum(jnp.where(lower, g_row, 0.0), axis=2, keepdims=True)
    gc_row = jnp.sum(eye * gc_col, axis=1, keepdims=True)
    decay = jnp.where(lower, jnp.exp(jnp.where(lower, gc_col - gc_row, 0.0)), 0.0)
    kb = k * bcol
    vb = v * bcol
    qs = q * (DN_HEAD_DIM ** -0.5)
    egc = jnp.exp(gc_col)
    akq = _mm(jnp.concatenate([kb, qs], axis=1), k, 1, 1, *_DN_PREC["akq"])
    a = jnp.where(ii > jj, akq[:, :n] * decay, 0.0)
    qk = akq[:, n:] * decay
    t = _tri_inv(a, *_DN_PREC["inv"])
    uw = _mm(t, jnp.concatenate([vb, kb * egc], axis=2), 1, 0, *_DN_PREC["uw"])
    wq = jnp.concatenate([uw[:, :, d:], qs * egc], axis=1)
    g_last = gc_col[:, n - 1:n, :]
    k_dec = k * jnp.exp(g_last - gc_col)
    e_last = jnp.exp(g_last)
    os_ = []
    for c in range(nb // DN_HEADS):
        sl = slice(c * DN_HEADS, (c + 1) * DN_HEADS)
        ws = _mm(wq[sl], state, 1, 0, *_DN_PREC["ws"])
        v_new = uw[sl, :, :d] - ws[:, :n]
        os_.append(ws[:, n:] + _mm(qk[sl], v_new, 1, 0, *_DN_PREC["ov"]))
        state = state * e_last[sl] + _mm(k_dec[sl], v_new, 0, 0, *_DN_PREC["st"])
    o = jnp.concatenate(os_, axis=0)
    y = o * lax.rsqrt(jnp.mean(o * o, axis=-1, keepdims=True) + NORM_EPS) * nw * _silu(dz)
    return y, state


def _dn_block_args(gc, q_ref, k_ref, v_ref, gb_ref, dz_ref):
    qs, ks, vs, gs, bs, zs = [], [], [], [], [], []
    for cc in range(gc):
        r = slice(cc * CHUNK, (cc + 1) * CHUNK)
        gbv = gb_ref[r, :]
        for h in range(DN_HEADS):
            cols = slice(h * DN_HEAD_DIM, (h + 1) * DN_HEAD_DIM)
            qs.append(q_ref[r, cols])
            ks.append(k_ref[r, cols])
            vs.append(v_ref[r, cols])
            zs.append(dz_ref[r, cols])
            gs.append(gbv[:, DN_HEADS + h:DN_HEADS + h + 1])
            bs.append(gbv[:, h:h + 1])
    return tuple(jnp.stack(t, axis=0) for t in (qs, ks, vs, gs, bs, zs))


def _dn_scan_fwd(qn, kn, vv, gb, proj_main, dn_norm_w, gc):
    s = qn.shape[0]
    nchunk = s // CHUNK
    rows = gc * CHUNK

    def body(q_ref, k_ref, v_ref, gb_ref, dz_ref, nw_ref, y_ref, ss_ref, state):
        @pl.when(pl.program_id(0) == 0)
        def _():
            state[...] = jnp.zeros_like(state)
        q, k, v, gcol, bcol, dz = _dn_block_args(gc, q_ref, k_ref, v_ref, gb_ref, dz_ref)
        st = state[...]
        ss_ref[0] = st
        y, new = _dn_block(q, k, v, gcol, bcol, st, dz, nw_ref[...])
        state[...] = new
        for cc in range(gc):
            for h in range(DN_HEADS):
                y_ref[cc * CHUNK:(cc + 1) * CHUNK, h * DN_HEAD_DIM:(h + 1) * DN_HEAD_DIM] = y[cc * DN_HEADS + h]

    tile = pl.BlockSpec((rows, D_DN), lambda i: (i, 0))
    return pl.pallas_call(
        body, name="dn_scan_fwd", grid=(nchunk // gc,),
        out_shape=(jax.ShapeDtypeStruct((s, D_DN), F32),
                   jax.ShapeDtypeStruct((nchunk // gc, DN_HEADS, DN_HEAD_DIM, DN_HEAD_DIM), F32)),
        in_specs=[tile, tile, tile, pl.BlockSpec((rows, 128), lambda i: (i, 0)),
                  pl.BlockSpec((rows, D_DN), lambda i: (i, 5)), pl.BlockSpec((1, 128), lambda i: (0, 0))],
        out_specs=(tile, pl.BlockSpec((1, DN_HEADS, DN_HEAD_DIM, DN_HEAD_DIM), lambda i: (i, 0, 0, 0))),
        scratch_shapes=[pltpu.VMEM((DN_HEADS, DN_HEAD_DIM, DN_HEAD_DIM), F32)],
        compiler_params=_cp(("arbitrary",)),
    )(qn, kn, vv, gb, proj_main, dn_norm_w)


def _out_proj_loss(y_pool, y_dn, x2, tgt, w_out_full, fnw, tm):
    s = x2.shape[0]

    def body(yp_ref, yd_ref, x_ref, t_ref, wo_ref, fw_ref,
             dh_ref, dyp_ref, dyd_ref, gwo_ref, gfw_ref, loss_ref):
        @pl.when(pl.program_id(0) == 0)
        def _():
            gwo_ref[...] = jnp.zeros_like(gwo_ref)
            gfw_ref[...] = jnp.zeros_like(gfw_ref)
            loss_ref[...] = jnp.zeros_like(loss_ref)
        y = jnp.concatenate([yp_ref[...], yd_ref[...]], axis=1).astype(BF16)
        wo = wo_ref[...]
        h = x_ref[...] + jnp.dot(y, wo, preferred_element_type=F32)
        hn, r = _rms_hat(h)
        fw = fw_ref[...]
        err = hn * fw - t_ref[...]
        loss_ref[...] += 0.5 * jnp.sum(jnp.sum(err * err, axis=-1, keepdims=True) / D_MODEL, axis=0, keepdims=True)
        dout = err / D_MODEL
        gfw_ref[...] += jnp.sum(dout * hn, axis=0, keepdims=True)
        dhn = dout * fw
        dh = r * (dhn - hn * jnp.mean(dhn * hn, axis=-1, keepdims=True))
        dh_ref[...] = dh
        dhb = dh.astype(BF16)
        dy = _dot_nt_bf(dhb, wo)
        dyp_ref[...] = dy[:, :D_POOL]
        dyd_ref[...] = dy[:, D_POOL:]
        gwo_ref[...] += _dot_tn_bf(y, dhb)

    half = pl.BlockSpec((tm, D_POOL), lambda i: (i, 0))
    full = pl.BlockSpec((tm, D_MODEL), lambda i: (i, 0))
    return pl.pallas_call(
        body, name="out_proj_loss", grid=(s // tm,),
        out_shape=(jax.ShapeDtypeStruct((s, D_MODEL), F32), jax.ShapeDtypeStruct((s, D_POOL), F32),
                   jax.ShapeDtypeStruct((s, D_DN), F32), jax.ShapeDtypeStruct((D_MODEL, D_MODEL), F32),
                   jax.ShapeDtypeStruct((1, D_MODEL), F32), jax.ShapeDtypeStruct((1, 128), F32)),
        in_specs=[half, half, full, full, pl.BlockSpec((D_MODEL, D_MODEL), lambda i: (0, 0)),
                  pl.BlockSpec((1, D_MODEL), lambda i: (0, 0))],
        out_specs=(full, half, half, pl.BlockSpec((D_MODEL, D_MODEL), lambda i: (0, 0)),
                   pl.BlockSpec((1, D_MODEL), lambda i: (0, 0)), pl.BlockSpec((1, 128), lambda i: (0, 0))),
        compiler_params=_cp(("arbitrary",)),
    )(y_pool, y_dn, x2, tgt, w_out_full, fnw)


def _dn_scan_bwd(qn, kn, vv, gb, proj_main, dn_norm_w, states, dy_dn, gc):
    s = qn.shape[0]
    nchunk = s // CHUNK
    nstep = nchunk // gc
    rows = gc * CHUNK

    def body(q_ref, k_ref, v_ref, gb_ref, dz_ref, nw_ref, ss_ref, dy_ref,
             dq_ref, dk_ref, dv_ref, dgb_ref, ddz_ref, dnw_ref, dstate):
        @pl.when(pl.program_id(0) == 0)
        def _():
            dstate[...] = jnp.zeros_like(dstate)
            dnw_ref[...] = jnp.zeros_like(dnw_ref)
        lane = lax.broadcasted_iota(jnp.int32, (CHUNK, 128), 1)
        q, k, v, gcol, bcol, dz = _dn_block_args(gc, q_ref, k_ref, v_ref, gb_ref, dz_ref)
        dy = jnp.stack([dy_ref[cc * CHUNK:(cc + 1) * CHUNK, h * DN_HEAD_DIM:(h + 1) * DN_HEAD_DIM]
                        for cc in range(gc) for h in range(DN_HEADS)], axis=0)
        _, vjp = jax.vjp(_dn_block, q, k, v, gcol, bcol, ss_ref[0], dz, nw_ref[...])
        dq, dk, dv, dg, db, dst, ddz, dnw = vjp((dy, dstate[...]))
        dstate[...] = dst
        dnw_ref[...] += dnw
        for cc in range(gc):
            r = slice(cc * CHUNK, (cc + 1) * CHUNK)
            dgb = jnp.zeros((CHUNK, 128), F32)
            for h in range(DN_HEADS):
                b = cc * DN_HEADS + h
                cols = slice(h * DN_HEAD_DIM, (h + 1) * DN_HEAD_DIM)
                for ref, val in zip((dq_ref, dk_ref, dv_ref, ddz_ref), (dq, dk, dv, ddz)):
                    ref[r, cols] = val[b]
                dgb = dgb + jnp.where(lane == h, db[b], 0.0) + jnp.where(lane == DN_HEADS + h, dg[b], 0.0)
            dgb_ref[r, :] = dgb

    rev = lambda i: (nstep - 1 - i, 0)
    tile = pl.BlockSpec((rows, D_DN), rev)
    lanes = pl.BlockSpec((rows, 128), rev)
    return pl.pallas_call(
        body, name="dn_scan_bwd", grid=(nstep,),
        out_shape=(jax.ShapeDtypeStruct((s, D_DN), F32),) * 3
        + (jax.ShapeDtypeStruct((s, 128), F32), jax.ShapeDtypeStruct((s, D_DN), F32),
           jax.ShapeDtypeStruct((1, 128), F32)),
        in_specs=[tile, tile, tile, lanes, pl.BlockSpec((rows, D_DN), lambda i: (nstep - 1 - i, 5)),
                  pl.BlockSpec((1, 128), lambda i: (0, 0)),
                  pl.BlockSpec((1, DN_HEADS, DN_HEAD_DIM, DN_HEAD_DIM), lambda i: (nstep - 1 - i, 0, 0, 0)), tile],
        out_specs=(tile, tile, tile, lanes, tile, pl.BlockSpec((1, 128), lambda i: (0, 0))),
        scratch_shapes=[pltpu.VMEM((DN_HEADS, DN_HEAD_DIM, DN_HEAD_DIM), F32)],
        compiler_params=_cp(("arbitrary",)),
    )(qn, kn, vv, gb, proj_main, dn_norm_w, states, dy_dn)


def _dn_pre_bwd1(proj_main, proj_ba, conv_full, alog_lane, dtb_lane, dqn, dkn, dvv, dgb, tp):
    s = proj_main.shape[0]

    def body(q_ref, k_ref, v_ref, qp_ref, kp_ref, vp_ref, cw_ref, ba_ref, al_ref, db_ref,
             dqn_ref, dkn_ref, dvv_ref, dgb_ref,
             dcq_ref, dck_ref, dcv_ref, dba_ref, dcw_ref, dal_ref, ddb_ref):
        i = pl.program_id(0)

        @pl.when(i == 0)
        def _():
            dcw_ref[...] = jnp.zeros_like(dcw_ref)
            dal_ref[...] = jnp.zeros_like(dal_ref)
            ddb_ref[...] = jnp.zeros_like(ddb_ref)
        curs = (q_ref[...], k_ref[...], v_ref[...])
        tails = (_prev_tail(qp_ref, i), _prev_tail(kp_ref, i), _prev_tail(vp_ref, i))
        ys = [_conv_fwd(curs[j], tails[j], cw_ref[:, j * D_DN:(j + 1) * D_DN]) for j in range(3)]
        _, vjp = jax.vjp(_post_conv, *ys)
        dys = vjp((dqn_ref[...], dkn_ref[...], dvv_ref[...]))
        for j, (dy, out) in enumerate(zip(dys, (dcq_ref, dck_ref, dcv_ref))):
            out[...] = dy
            for sft in range(CONV_WIDTH):
                xs = curs[j] if sft == 0 else _shift_down(curs[j], tails[j], sft)
                row = CONV_WIDTH - 1 - sft
                dcw_ref[row:row + 1, j * D_DN:(j + 1) * D_DN] += jnp.sum(dy * xs, axis=0, keepdims=True)
        _, gvjp = jax.vjp(_gates, ba_ref[...], al_ref[...], db_ref[...])
        dba, dal, ddb = gvjp(dgb_ref[...])
        dba_ref[...] = dba
        dal_ref[...] += dal
        ddb_ref[...] += ddb

    cur, prev = _qkv_specs(tp, 0), _qkv_specs(tp, -1)
    row = pl.BlockSpec((1, 128), lambda i: (0, 0))
    tile = pl.BlockSpec((tp, D_DN), lambda i: (i, 0))
    lanes = pl.BlockSpec((tp, 128), lambda i: (i, 0))
    cw = pl.BlockSpec((CONV_WIDTH, 3 * D_DN), lambda i: (0, 0))
    return pl.pallas_call(
        body, name="dn_pre_bwd1", grid=(s // tp,),
        out_shape=(jax.ShapeDtypeStruct((s, D_DN), F32),) * 3
        + (jax.ShapeDtypeStruct((s, 128), F32), jax.ShapeDtypeStruct((CONV_WIDTH, 3 * D_DN), F32),
           jax.ShapeDtypeStruct((1, 128), F32), jax.ShapeDtypeStruct((1, 128), F32)),
        in_specs=[cur(2), cur(3), cur(4), prev(2), prev(3), prev(4), cw, lanes, row, row, tile, tile, tile, lanes],
        out_specs=(tile, tile, tile, lanes, cw, row, row),
        compiler_params=_cp(("arbitrary",)),
    )(proj_main, proj_main, proj_main, proj_main, proj_main, proj_main, conv_full, proj_ba, alog_lane, dtb_lane,
      dqn, dkn, dvv, dgb)


def _pool_bwd1(proj_main, pool_w, pool_scale, dyp, tp):
    s = proj_main.shape[0]

    def body(u_ref, up_ref, z_ref, pw_ref, ps_ref, dy_ref, dz_ref, dwin_ref, dpw_ref, dps_ref):
        i = pl.program_id(0)

        @pl.when(i == 0)
        def _():
            dpw_ref[...] = jnp.zeros_like(dpw_ref)
            dps_ref[...] = jnp.zeros_like(dps_ref)
        u = u_ref[...]
        z = z_ref[...]
        dy = dy_ref[...]
        ps = ps_ref[...]
        mixes = _pool_mix(u, _prev_tail(up_ref, i), i, tp)
        sg = jax.nn.sigmoid(z)
        sz = z * sg
        dsz = sg * (1.0 + z * (1.0 - sg))
        for gi, w in enumerate(POOL_WINDOWS):
            cols = slice(gi * POOL_GROUP, (gi + 1) * POOL_GROUP)
            mixw = _dot_bf(mixes[gi], pw_ref[gi])
            dmixw = dy[:, cols] * ps[:, cols] * sz[:, cols]
            dps_ref[:, cols] += jnp.sum(dy[:, cols] * mixw * sz[:, cols], axis=0, keepdims=True)
            dz_ref[:, cols] = dy[:, cols] * mixw * ps[:, cols] * dsz[:, cols]
            dpw_ref[gi] += _dot_tn_bf(mixes[gi], dmixw)
            dmix = _dot_nt_bf(dmixw, pw_ref[gi])
            dwin_ref[:, cols] = dmix / _pool_counts(i, tp, w)

    tile = pl.BlockSpec((tp, D_POOL), lambda i: (i, 0))
    pw = pl.BlockSpec((4, POOL_GROUP, POOL_GROUP), lambda i: (0, 0, 0))
    ps = pl.BlockSpec((1, D_POOL), lambda i: (0, 0))
    return pl.pallas_call(
        body, name="pool_bwd1", grid=(s // tp,),
        out_shape=(jax.ShapeDtypeStruct((s, D_POOL), F32), jax.ShapeDtypeStruct((s, D_POOL), F32),
                   jax.ShapeDtypeStruct((4, POOL_GROUP, POOL_GROUP), F32), jax.ShapeDtypeStruct((1, D_POOL), F32)),
        in_specs=[tile, _prev_halo_spec(tp, D_POOL, 0),
                  pl.BlockSpec((tp, D_POOL), lambda i: (i, 1)), pw, ps, tile],
        out_specs=(tile, tile, pw, ps),
        compiler_params=_cp(("arbitrary",)),
    )(proj_main, proj_main, proj_main, pool_w, pool_scale, dyp)


def _in_proj_bwd(dwin, dzp, dcq, dck, dcv, ddz, dba, x2, dh, norm_w, wt_full, conv_full, tm):
    s = x2.shape[0]
    nstep = s // tm

    def body(dwin_ref, dzp_ref, dcq_ref, dck_ref, dcv_ref, ddz_ref, dba_ref, x_ref, dh_ref, nw_ref, cw_ref, wt_hbm,
             gx_ref, p_hbm, gnw_ref, wt_vmem, acc, blk, head_dc, head_dw, sem, osem):
        j = pl.program_id(0)
        i = nstep - 1 - j

        @pl.when(j == 0)
        def _():
            cp = pltpu.make_async_copy(wt_hbm, wt_vmem, sem)
            cp.start()
            acc[...] = jnp.zeros_like(acc)
            gnw_ref[...] = jnp.zeros_like(gnw_ref)
            head_dc[...] = jnp.zeros_like(head_dc)
            head_dw[...] = jnp.zeros_like(head_dw)
            cp.wait()
        dw = dwin_ref[...]
        ext = jnp.concatenate([dw, head_dw[...]], axis=0)
        m = ext.shape[0]
        a2 = ext + pltpu.roll(ext, m - 1, 0)
        a4 = a2 + pltpu.roll(a2, m - 2, 0)
        a8 = a4 + pltpu.roll(a4, m - 4, 0)
        a16 = a8 + pltpu.roll(a8, m - 8, 0)
        dup = jnp.concatenate(
            [acc_w[:tm, gi * POOL_GROUP:(gi + 1) * POOL_GROUP]
             - dw[:, gi * POOL_GROUP:(gi + 1) * POOL_GROUP] * _pool_counts(i, tm, w)
             for gi, (w, acc_w) in enumerate(zip(POOL_WINDOWS, (a2, a4, a8, a16)))], axis=1)
        head_dw[...] = dw[:HALO, :]
        dxs = []
        for c, ref in enumerate((dcq_ref, dck_ref, dcv_ref)):
            cols = slice(c * D_DN, (c + 1) * D_DN)
            w4 = cw_ref[:, cols]
            dy = ref[...]
            head = head_dc[:, cols]
            dx = dy * w4[CONV_WIDTH - 1:CONV_WIDTH, :]
            for sft in range(1, CONV_WIDTH):
                dx = dx + _shift_up(dy, head, sft) * w4[CONV_WIDTH - 1 - sft:CONV_WIDTH - sft, :]
            dxs.append(dx)
            head_dc[:, cols] = dy[:HALO, :]
        dbab = dba_ref[...].astype(BF16)
        xhat, r = _rms_hat(x_ref[...])
        nw = nw_ref[...]
        n = (xhat * nw).astype(BF16)
        acc[D_MAIN:, :] += _dot_tn_bf(dbab, n)
        dn = jnp.dot(dbab, wt_vmem[D_MAIN:, :], preferred_element_type=F32)
        for cb, d in enumerate((dup, dzp_ref[...], dxs[0], dxs[1], dxs[2], ddz_ref[...])):
            rows = slice(cb * D_POOL, (cb + 1) * D_POOL)
            dpart = d.astype(BF16)
            acc[rows, :] += _dot_tn_bf(dpart, n)
            dn = dn + jnp.dot(dpart, wt_vmem[rows, :], preferred_element_type=F32)
        gnw_ref[...] += jnp.sum(dn * xhat, axis=0, keepdims=True)
        dxh = dn * nw
        gx_ref[...] = dh_ref[...] + r * (dxh - xhat * jnp.mean(dxh * xhat, axis=-1, keepdims=True))

        @pl.when(j == nstep - 1)
        def _():
            def out(d):
                return pltpu.make_async_copy(blk.at[d % 2], p_hbm.at[d], osem.at[d % 2])
            for d in range(N_DEV):
                if d >= 2:
                    out(d - 2).wait()
                blk[d % 2] = acc[W_IN_SHARD * d:W_IN_SHARD * (d + 1), :]
                out(d).start()
            out(N_DEV - 2).wait()
            out(N_DEV - 1).wait()

    rev = lambda j: (nstep - 1 - j, 0)
    part = pl.BlockSpec((tm, D_POOL), rev)
    full = pl.BlockSpec((tm, D_MODEL), rev)
    row = pl.BlockSpec((1, D_MODEL), lambda j: (0, 0))
    return pl.pallas_call(
        body, name="in_proj_bwd", grid=(nstep,),
        out_shape=(jax.ShapeDtypeStruct((s, D_MODEL), F32),
                   jax.ShapeDtypeStruct((N_DEV, W_IN_SHARD, D_MODEL), F32), jax.ShapeDtypeStruct((1, D_MODEL), F32)),
        in_specs=[part] * 6 + [pl.BlockSpec((tm, 128), rev), full, full, row,
                               pl.BlockSpec((CONV_WIDTH, 3 * D_DN), lambda j: (0, 0)),
                               pl.BlockSpec(memory_space=pl.ANY)],
        out_specs=(full, pl.BlockSpec(memory_space=pl.ANY), row),
        scratch_shapes=[pltpu.VMEM((D_IN_PAD, D_MODEL), BF16), pltpu.VMEM((D_IN_PAD, D_MODEL), F32),
                        pltpu.VMEM((2, W_IN_SHARD, D_MODEL), F32),
                        pltpu.VMEM((HALO, 3 * D_DN), F32), pltpu.VMEM((HALO, D_POOL), F32),
                        pltpu.SemaphoreType.DMA, pltpu.SemaphoreType.DMA((2,))],
        compiler_params=_cp(("arbitrary",)),
    )(dwin, dzp, dcq, dck, dcv, ddz, dba, x2, dh, norm_w, conv_full, wt_full)


def _adamw_math(w, g, m, v):
    m = ADAM_B1 * m + (1.0 - ADAM_B1) * g
    v = ADAM_B2 * v + (1.0 - ADAM_B2) * (g * g)
    m_hat = m / (1.0 - ADAM_B1 ** ADAM_STEP)
    v_hat = v / (1.0 - ADAM_B2 ** ADAM_STEP)
    delta = -ADAM_LR * (m_hat / (jnp.sqrt(v_hat) + ADAM_EPS) + ADAM_WD * w)
    return delta, m, v


def _adamw_sharded(params):
    k = len(params)

    def body(*refs):
        ins, outs = refs[:4 * k], refs[4 * k:]
        for p in range(k):
            w, g, m, v = (r[...] for r in ins[4 * p:4 * p + 4])
            d, nm, nv = _adamw_math(w, g, m, v)
            outs[3 * p][...] = d
            outs[3 * p + 1][...] = nm
            outs[3 * p + 2][...] = nv

    flat = [a for p in params for a in p]
    out_shape = tuple(jax.ShapeDtypeStruct(p[0].shape, F32) for p in params for _ in range(3))
    res = pl.pallas_call(body, name="adamw_sharded", out_shape=out_shape, compiler_params=_cp())(*flat)
    return [tuple(res[3 * p:3 * p + 3]) for p in range(k)]


def _adamw_replicated(gath_a, gath_b, pool, rows):
    nrow = len(rows)

    def body(*refs):
        ga_ref, gb_ref = refs[:2]
        ins = refs[2:2 + 3 * (nrow + 1)]
        outs = refs[2 + 3 * (nrow + 1):]

        def total(ref):
            g = ref[0]
            for d in range(1, N_DEV):
                g = g + ref[d]
            return g

        def update(g, wmv, o):
            w, m, v = (r[...] for r in wmv)
            dl, nm, nv = _adamw_math(w, g, m, v)
            o[0][...] = g
            o[1][...] = dl
            o[2][...] = nm
            o[3][...] = nv

        update(total(ga_ref), ins[:3], outs[:4])
        gb = total(gb_ref)
        for r in range(nrow):
            n = ins[3 * (r + 1)].shape[1]
            update(gb[r:r + 1, :n], ins[3 * (r + 1):3 * (r + 2)], outs[4 * (r + 1):4 * (r + 2)])
        outs[4 * (nrow + 1)][...] = gb[nrow:nrow + 1, 0:1]

    flat = list(pool) + [a for wmv in rows for a in wmv]
    out_shape = ((jax.ShapeDtypeStruct(pool[0].shape, F32),) * 4
                 + tuple(jax.ShapeDtypeStruct(wmv[0].shape, F32) for wmv in rows for _ in range(4))
                 + (jax.ShapeDtypeStruct((1, 1), F32),))
    res = pl.pallas_call(body, name="adamw_replicated", out_shape=out_shape, compiler_params=_cp())(
        gath_a, gath_b, *flat)
    return [res[4 * k:4 * k + 4] for k in range(nrow + 1)], res[-1]


_ROW_ORDER = ("norm_w", "final_norm_w", "pool_scale", "dn_norm_w", "a_log", "dt_bias")


def _pack_rows(vectors):
    out = [jnp.pad(v.reshape(-1), (0, D_MODEL - v.size)) for v in vectors]
    out += [jnp.zeros((D_MODEL,), F32)] * (8 - len(out))
    return jnp.stack(out, axis=0)


def _lane_row(vec4, start):
    return jnp.pad(vec4.reshape(-1), (start, 128 - start - vec4.size)).reshape(1, 128)


def kernel(x, norm_w, w_in, pool_w, pool_scale, conv_w, a_log, dt_bias, dn_norm_w, w_out, final_norm_w, loss_target, m_norm_w, m_w_in, m_pool_w, m_pool_scale, m_conv_w, m_a_log, m_dt_bias, m_dn_norm_w, m_w_out, m_final_norm_w, v_norm_w, v_w_in, v_pool_w, v_pool_scale, v_conv_w, v_a_log, v_dt_bias, v_dn_norm_w, v_w_out, v_final_norm_w):
    s = x.shape[1]
    tm = min(512, s)
    tmb = min(512, s)
    tp = min(512, s)
    x2 = x[0]
    tgt = loss_target[0]
    wt, m_wt, v_wt = w_in[0].T, m_w_in[0].T, v_w_in[0].T

    g_in, g_out, g_conv = _gather_weights(wt, w_out[0], conv_w[0])
    w_out_full = g_out.reshape(D_MODEL, D_MODEL)
    conv_full = g_conv.transpose(1, 0, 2).reshape(CONV_WIDTH, 3 * D_DN)
    alog_lane = _lane_row(a_log, DN_HEADS)
    dtb_lane = _lane_row(dt_bias, DN_HEADS)
    fnw = final_norm_w.reshape(1, D_MODEL)

    proj_main, proj_ba, y_pool, qn, kn, vv, gb, wt_full = _front(
        x2, norm_w, g_in, pool_w[0], pool_scale, conv_full, alog_lane, dtb_lane, tm)
    y_dn, states = _dn_scan_fwd(qn, kn, vv, gb, proj_main, dn_norm_w, DN_CHUNKS_PER_STEP)

    dh, dyp, dyd, g_wout, g_fnw, loss_part = _out_proj_loss(y_pool, y_dn, x2, tgt, w_out_full, fnw, tm)
    dqn, dkn, dvv, dgb, ddz, g_dnw = _dn_scan_bwd(qn, kn, vv, gb, proj_main, dn_norm_w, states, dyd,
                                                  DN_CHUNKS_PER_STEP)
    dcq, dck, dcv, dba, g_conv_full, g_al, g_db = _dn_pre_bwd1(
        proj_main, proj_ba, conv_full, alog_lane, dtb_lane, dqn, dkn, dvv, dgb, tp)
    dzp, dwin, g_pw, g_ps = _pool_bwd1(proj_main, pool_w[0], pool_scale, dyp, tp)
    grad_x2, p_in, g_nw = _in_proj_bwd(dwin, dzp, dcq, dck, dcv, ddz, dba, x2, dh, norm_w, wt_full, conv_full, tmb)

    p_out = g_wout.reshape(N_DEV, D_MODEL // N_DEV, D_MODEL)
    p_conv = g_conv_full.reshape(CONV_WIDTH, N_DEV, 3 * D_DN // N_DEV).transpose(1, 0, 2)
    pack_a = g_pw.reshape(4 * POOL_GROUP, POOL_GROUP)
    pack_b = _pack_rows([g_nw, g_fnw, g_ps, g_dnw, g_al[0, DN_HEADS:2 * DN_HEADS], g_db[0, DN_HEADS:2 * DN_HEADS],
                         loss_part[0, :1]])
    gr_in, gr_out, gr_conv, gath_a, gath_b = _reduce_grads(p_in, p_out, p_conv, pack_a, pack_b)

    sharded = _adamw_sharded([(wt, gr_in, m_wt, v_wt), (w_out[0], gr_out, m_w_out[0], v_w_out[0]),
                              (conv_w[0], gr_conv, m_conv_w[0], v_conv_w[0])])
    r_in, r_out, r_conv = ((g,) + dmv for g, dmv in zip((gr_in, gr_out, gr_conv), sharded))
    flat = lambda a: a.reshape(4 * POOL_GROUP, POOL_GROUP)
    row = lambda a: a.reshape(1, -1)
    vecs = {"norm_w": (norm_w, m_norm_w, v_norm_w), "final_norm_w": (final_norm_w, m_final_norm_w, v_final_norm_w),
            "pool_scale": (pool_scale, m_pool_scale, v_pool_scale), "dn_norm_w": (dn_norm_w, m_dn_norm_w, v_dn_norm_w),
            "a_log": (a_log, m_a_log, v_a_log), "dt_bias": (dt_bias, m_dt_bias, v_dt_bias)}
    res, loss = _adamw_replicated(gath_a, gath_b, (flat(pool_w), flat(m_pool_w), flat(v_pool_w)),
                                  [tuple(row(a) for a in vecs[nm]) for nm in _ROW_ORDER])
    r_pool = res[0]
    r_vec = dict(zip(_ROW_ORDER, res[1:]))

    def group(k):
        vec = lambda nm: r_vec[nm][k].reshape(vecs[nm][0].shape)
        return (vec("norm_w"), r_in[k].T[None], r_pool[k].reshape(pool_w.shape), vec("pool_scale"), r_conv[k][None],
                vec("a_log"), vec("dt_bias"), vec("dn_norm_w"), r_out[k][None], vec("final_norm_w"))

    return (loss[0, 0], grad_x2[None], *group(0), *group(1), *group(2), *group(3))
```

```python
import functools

import jax
import jax.numpy as jnp
from jax import lax
from jax.experimental import pallas as pl
from jax.experimental.pallas import tpu as pltpu

F32 = jnp.float32
BF16 = jnp.bfloat16
HI = lax.Precision.HIGHEST
MESH = pl.DeviceIdType.MESH

D_MODEL = 1024
D_POOL = 512
D_DN = 512
POOL_WINDOWS = (2, 4, 8, 16)
POOL_GROUP = 128
DN_HEADS = 4
DN_HEAD_DIM = 128
CONV_WIDTH = 4
CHUNK = 64
NORM_EPS = 1e-6
D_IN = 3080
D_MAIN = 3072
D_IN_PAD = D_MAIN + 128
N_DEV = 8
W_IN_SHARD = D_IN // N_DEV
HALO = 16
DN_CHUNKS_PER_STEP = 8

ADAM_LR = 0.001
ADAM_B1 = 0.9
ADAM_B2 = 0.999
ADAM_EPS = 1e-08
ADAM_WD = 0.01
ADAM_STEP = 10

VMEM_LIMIT = 56 * 1024 * 1024


def _cp(sem=None, vmem=VMEM_LIMIT):
    kw = {"vmem_limit_bytes": vmem}
    if sem is not None:
        kw["dimension_semantics"] = sem
    return pltpu.CompilerParams(**kw)


def _dot_bf(a, b):
    return jnp.dot(a.astype(BF16), b.astype(BF16), preferred_element_type=F32)


def _dot_nt_bf(a, b):
    return lax.dot_general(a.astype(BF16), b.astype(BF16), (((1,), (1,)), ((), ())), preferred_element_type=F32)


def _dot_tn_bf(a, b):
    return lax.dot_general(a.astype(BF16), b.astype(BF16), (((0,), (0,)), ((), ())), preferred_element_type=F32)


def _mm_raw(a, b, ca, cb, prec):
    off = a.ndim - 2
    dn = (((ca + off,), (cb + off,)), ((0,), (0,)) if off else ((), ()))
    if prec == "hi":
        return lax.dot_general(a, b, dn, precision=HI, preferred_element_type=F32)
    ah, bh = a.astype(BF16), b.astype(BF16)
    out = lax.dot_general(ah, bh, dn, preferred_element_type=F32)
    if prec == "x3":
        al = (a - ah.astype(F32)).astype(BF16)
        bl = (b - bh.astype(F32)).astype(BF16)
        out = out + lax.dot_general(ah, bl, dn, preferred_element_type=F32)
        out = out + lax.dot_general(al, bh, dn, preferred_element_type=F32)
    return out


@functools.partial(jax.custom_vjp, nondiff_argnums=(2, 3, 4, 5))
def _mm(a, b, ca, cb, prec, bprec):
    return _mm_raw(a, b, ca, cb, prec)


def _mm_fwd(a, b, ca, cb, prec, bprec):
    return _mm_raw(a, b, ca, cb, prec), (a, b)


def _mm_bwd(ca, cb, prec, bprec, res, dc):
    a, b = res
    da = _mm_raw(dc, b, 1, 1 - cb, bprec) if ca == 1 else _mm_raw(b, dc, 1 - cb, 1, bprec)
    db = _mm_raw(a, dc, 1 - ca, 0, bprec) if cb == 0 else _mm_raw(dc, a, 0, 1 - ca, bprec)
    return da, db


_mm.defvjp(_mm_fwd, _mm_bwd)


@functools.partial(jax.custom_vjp, nondiff_argnums=(1, 2))
def _tri_inv(a, prec, bprec):
    n = a.shape[-1]
    ii = lax.broadcasted_iota(jnp.int32, (n, n), 0)
    jj = lax.broadcasted_iota(jnp.int32, (n, n), 1)
    p = (ii == jj).astype(F32) - a
    b = _mm_raw(a, a, 1, 0, prec)
    for _ in range(4):
        pb = _mm_raw(jnp.concatenate([p, b], axis=-2), b, 1, 0, prec)
        p = p + pb[..., :n, :]
        b = pb[..., n:, :]
    return p + _mm_raw(p, b, 1, 0, prec)


def _tri_inv_fwd(a, prec, bprec):
    t = _tri_inv(a, prec, bprec)
    return t, t


def _tri_inv_bwd(prec, bprec, t, dt):
    return (-_mm_raw(_mm_raw(t, dt, 0, 0, bprec), t, 1, 1, bprec),)


_tri_inv.defvjp(_tri_inv_fwd, _tri_inv_bwd)

_DN_PREC = {"akq": ("bf16", "bf16"), "inv": ("bf16", "bf16"), "uw": ("bf16", "bf16"), "ws": ("bf16", "bf16"),
            "ov": ("bf16", "bf16"), "st": ("bf16", "bf16")}


def _silu(x):
    return x * jax.nn.sigmoid(x)


def _softplus(x):
    pos = x > 0.0
    return jnp.where(pos, x, 0.0) + jnp.log1p(jnp.exp(jnp.where(pos, -x, x)))


def _mesh_pos():
    return lax.axis_index("x"), lax.axis_index("y"), lax.axis_index("c")


def _dev_index(x, y, c):
    return 4 * x + 2 * y + c


def _all_gather_blocks(outs, send_sems, recv_sems):
    x, y, c = _mesh_pos()
    me = (x, y, c)
    sibling = (x, y, 1 - c)
    chips = [(1 - x, y), (x, 1 - y), (1 - x, 1 - y)]

    def copy(a, k, block, to):
        rows = outs[a].at[_dev_index(*block)]
        return pltpu.make_async_remote_copy(src_ref=rows, dst_ref=rows, send_sem=send_sems.at[a, k],
                                            recv_sem=recv_sems.at[a, k], device_id=to, device_id_type=MESH)

    n = len(outs)
    first = []
    for a in range(n):
        first.append(copy(a, 0, me, sibling))
        for j, chip in enumerate(chips):
            first.append(copy(a, 1 + j, me, (*chip, c)))
    for cp in first:
        cp.start()
    passed = []
    for j, chip in enumerate(chips):
        for a in range(n):
            copy(a, 1 + j, (*chip, c), me).wait_recv()
            fwd = copy(a, 4 + j, (*chip, c), sibling)
            fwd.start()
            passed.append(fwd)
    for a in range(n):
        copy(a, 0, sibling, me).wait_recv()
        for j, chip in enumerate(chips):
            copy(a, 4 + j, (*chip, 1 - c), me).wait_recv()
    for cp in first + passed:
        cp.wait_send()


def _gather_weights(w_in_blk, w_out_blk, conv_blk):
    def body(win_ref, wout_ref, conv_ref, gin_ref, gout_ref, gconv_ref, send_sems, recv_sems):
        x, y, c = _mesh_pos()
        me = _dev_index(x, y, c)
        gin_ref[me] = win_ref[...].astype(BF16)
        gout_ref[me] = wout_ref[...].astype(BF16)
        gconv_ref[me] = conv_ref[...]
        _all_gather_blocks((gin_ref, gout_ref, gconv_ref), send_sems, recv_sems)

    vm = pl.BlockSpec(memory_space=pltpu.VMEM)
    return pl.pallas_call(
        body, name="gather_weights",
        out_shape=(jax.ShapeDtypeStruct((N_DEV,) + w_in_blk.shape, BF16),
                   jax.ShapeDtypeStruct((N_DEV,) + w_out_blk.shape, BF16),
                   jax.ShapeDtypeStruct((N_DEV,) + conv_blk.shape, F32)),
        in_specs=[vm, vm, vm], out_specs=(vm, vm, vm),
        scratch_shapes=[pltpu.SemaphoreType.DMA((3, 7)), pltpu.SemaphoreType.DMA((3, 7))],
        compiler_params=_cp(),
    )(w_in_blk, w_out_blk, conv_blk)


def _reduce_grads(p_in, p_out, p_conv, pack_a, pack_b):
    big = (p_in, p_out, p_conv)

    def body(pin_ref, pout_ref, pconv_ref, pa_ref, pb_ref,
             oin_ref, oout_ref, oconv_ref, ga_ref, gb_ref,
             r1_in, r1_out, r1_conv, r2_in, r2_out, r2_conv, sb_in, sb_out, sb_conv, st_in, st_out, st_conv,
             s1_send, s1_recv, s2_send, s2_recv, ag_send, ag_recv, st_sem):
        x, y, c = _mesh_pos()
        me = (x, y, c)
        sibling = (x, y, 1 - c)
        rel = [(x, y), (1 - x, y), (x, 1 - y), (1 - x, 1 - y)]
        srcs = (pin_ref, pout_ref, pconv_ref)
        r1s = (r1_in, r1_out, r1_conv)
        r2s = (r2_in, r2_out, r2_conv)
        sbs = (sb_in, sb_out, sb_conv)
        sts = (st_in, st_out, st_conv)
        outs = (oin_ref, oout_ref, oconv_ref)

        ga_ref[_dev_index(*me)] = pa_ref[...]
        gb_ref[_dev_index(*me)] = pb_ref[...]

        def p1(a, r, to):
            return pltpu.make_async_remote_copy(
                src_ref=srcs[a].at[_dev_index(*rel[r], 1 - c)], dst_ref=r1s[a].at[r],
                send_sem=s1_send.at[a, r], recv_sem=s1_recv.at[a, r], device_id=to, device_id_type=MESH)

        def p2(a, r, to):
            return pltpu.make_async_remote_copy(
                src_ref=sbs[a].at[r - 1], dst_ref=r2s[a].at[r - 1],
                send_sem=s2_send.at[a, r - 1], recv_sem=s2_recv.at[a, r - 1], device_id=to, device_id_type=MESH)

        def stage(a, r):
            return pltpu.make_async_copy(srcs[a].at[_dev_index(*rel[r], c)], sts[a].at[r % 2], st_sem.at[a, r % 2])

        sends1 = [p1(a, r, sibling) for a in range(3) for r in range(4)]
        for cp in sends1:
            cp.start()
        sends2 = []
        for a in range(3):
            stage(a, 1).start()
            for r in (1, 2, 3, 0):
                nxt = {1: 2, 2: 3, 3: 0, 0: None}[r]
                if nxt is not None:
                    stage(a, nxt).start()
                stage(a, r).wait()
                p1(a, r, me).wait_recv()
                chip_sum = r1s[a][r] + sts[a][r % 2]
                if r == 0:
                    r1s[a][0] = chip_sum
                else:
                    sbs[a][r - 1] = chip_sum.astype(BF16)
                    cp = p2(a, r, (*rel[r], c))
                    cp.start()
                    sends2.append(cp)
        _all_gather_blocks((ga_ref, gb_ref), ag_send, ag_recv)
        for a in range(3):
            for r in (1, 2, 3):
                p2(a, r, me).wait_recv()
            outs[a][...] = ((r1s[a][0] + r2s[a][0].astype(F32)) + r2s[a][1].astype(F32)) + r2s[a][2].astype(F32)
        for cp in sends1 + sends2:
            cp.wait_send()

    vm = pl.BlockSpec(memory_space=pltpu.VMEM)
    hbm = pl.BlockSpec(memory_space=pl.ANY)
    blk = [p.shape[1:] for p in big]
    scratch = ([pltpu.VMEM((4,) + b, F32) for b in blk] + [pltpu.VMEM((3,) + b, BF16) for b in blk]
               + [pltpu.VMEM((3,) + b, BF16) for b in blk] + [pltpu.VMEM((2,) + b, F32) for b in blk]
               + [pltpu.SemaphoreType.DMA((3, 4)), pltpu.SemaphoreType.DMA((3, 4)),
                  pltpu.SemaphoreType.DMA((3, 3)), pltpu.SemaphoreType.DMA((3, 3)),
                  pltpu.SemaphoreType.DMA((2, 7)), pltpu.SemaphoreType.DMA((2, 7)),
                  pltpu.SemaphoreType.DMA((3, 2))])
    return pl.pallas_call(
        body, name="reduce_grads",
        out_shape=tuple(jax.ShapeDtypeStruct(b, F32) for b in blk)
        + (jax.ShapeDtypeStruct((N_DEV,) + pack_a.shape, F32), jax.ShapeDtypeStruct((N_DEV,) + pack_b.shape, F32)),
        in_specs=[hbm, hbm, hbm, vm, vm], out_specs=(vm, vm, vm, vm, vm),
        scratch_shapes=scratch,
        compiler_params=_cp(),
    )(p_in, p_out, p_conv, pack_a, pack_b)


def _rms_hat(xf):
    r = lax.rsqrt(jnp.mean(xf * xf, axis=-1, keepdims=True) + NORM_EPS)
    return xf * r, r


def _in_proj(x2, norm_w, g_in, tm):
    s = x2.shape[0]

    def body(x_ref, nw_ref, g_hbm, pm_ref, pb_ref, wt_hbm, g_vmem, wt_vmem, sem):
        @pl.when(pl.program_id(0) == 0)
        def _():
            cp = pltpu.make_async_copy(g_hbm, g_vmem, sem)
            cp.start()
            wt_vmem[D_MAIN:, :] = jnp.zeros((D_IN_PAD - D_MAIN, D_MODEL), BF16)
            cp.wait()
            for d in range(N_DEV):
                wt_vmem[W_IN_SHARD * d:W_IN_SHARD * (d + 1), :] = g_vmem[d]
            out = pltpu.make_async_copy(wt_vmem, wt_hbm, sem)
            out.start()
            out.wait()
        xhat, _ = _rms_hat(x_ref[...])
        n = (xhat * nw_ref[...]).astype(BF16)
        pm_ref[...] = _dot_nt_bf(n, wt_vmem[:D_MAIN, :])
        pb_ref[...] = _dot_nt_bf(n, wt_vmem[D_MAIN:, :])

    return pl.pallas_call(
        body, name="in_proj", grid=(s // tm,),
        out_shape=(jax.ShapeDtypeStruct((s, D_MAIN), F32), jax.ShapeDtypeStruct((s, 128), F32),
                   jax.ShapeDtypeStruct((D_IN_PAD, D_MODEL), BF16)),
        in_specs=[pl.BlockSpec((tm, D_MODEL), lambda i: (i, 0)),
                  pl.BlockSpec((1, D_MODEL), lambda i: (0, 0)),
                  pl.BlockSpec(memory_space=pl.ANY)],
        out_specs=(pl.BlockSpec((tm, D_MAIN), lambda i: (i, 0)), pl.BlockSpec((tm, 128), lambda i: (i, 0)),
                   pl.BlockSpec(memory_space=pl.ANY)),
        scratch_shapes=[pltpu.VMEM((N_DEV, W_IN_SHARD, D_MODEL), BF16), pltpu.VMEM((D_IN_PAD, D_MODEL), BF16),
                        pltpu.SemaphoreType.DMA],
        compiler_params=_cp(("arbitrary",)),
    )(x2, norm_w, g_in)


def _shift_down(cur, prev_tail, s):
    ext = jnp.concatenate([prev_tail, cur], axis=0)
    return pltpu.roll(ext, s, 0)[HALO:, :]


def _shift_up(cur, next_head, s):
    ext = jnp.concatenate([cur, next_head], axis=0)
    n = ext.shape[0]
    return pltpu.roll(ext, n - s, 0)[:cur.shape[0], :]


def _pool_counts(i, tp, w):
    t = i * tp + lax.broadcasted_iota(jnp.int32, (tp, 1), 0)
    return jnp.minimum(t + 1, w).astype(F32)


def _pool_mix(u, u_prev_tail, i, tp):
    ext = jnp.concatenate([u_prev_tail, u], axis=0)
    w2 = ext + pltpu.roll(ext, 1, 0)
    w4 = w2 + pltpu.roll(w2, 2, 0)
    w8 = w4 + pltpu.roll(w4, 4, 0)
    w16 = w8 + pltpu.roll(w8, 8, 0)
    mixes = []
    for gi, (w, win) in enumerate(zip(POOL_WINDOWS, (w2, w4, w8, w16))):
        cols = slice(gi * POOL_GROUP, (gi + 1) * POOL_GROUP)
        mixes.append(win[HALO:, cols] / _pool_counts(i, tp, w) - u[:, cols])
    return mixes


def _prev_halo_spec(tp, width, col):
    per = tp // HALO
    return pl.BlockSpec((HALO, width), lambda i: (jnp.maximum(i * per - 1, 0), col))


def _next_halo_spec(tp, width, col, n):
    per = tp // HALO
    return pl.BlockSpec((HALO, width), lambda i: (jnp.minimum((i + 1) * per, n * per - 1), col))


def _prev_tail(ref, i):
    return jnp.where(i > 0, ref[...], 0.0)


def _next_head(ref, i, n):
    return jnp.where(i < n - 1, ref[...], 0.0)


def _pool_fwd(proj_main, pool_w, pool_scale, tp):
    s = proj_main.shape[0]

    def body(u_ref, up_ref, z_ref, pw_ref, ps_ref, y_ref):
        i = pl.program_id(0)
        u = u_ref[...]
        mixes = _pool_mix(u, _prev_tail(up_ref, i), i, tp)
        gate = ps_ref[...] * _silu(z_ref[...])
        for gi in range(4):
            cols = slice(gi * POOL_GROUP, (gi + 1) * POOL_GROUP)
            y_ref[:, cols] = _dot_bf(mixes[gi], pw_ref[gi]) * gate[:, cols]

    return pl.pallas_call(
        body, name="pool_fwd", grid=(s // tp,),
        out_shape=jax.ShapeDtypeStruct((s, D_POOL), F32),
        in_specs=[pl.BlockSpec((tp, D_POOL), lambda i: (i, 0)),
                  _prev_halo_spec(tp, D_POOL, 0),
                  pl.BlockSpec((tp, D_POOL), lambda i: (i, 1)),
                  pl.BlockSpec((4, POOL_GROUP, POOL_GROUP), lambda i: (0, 0, 0)),
                  pl.BlockSpec((1, D_POOL), lambda i: (0, 0))],
        out_specs=pl.BlockSpec((tp, D_POOL), lambda i: (i, 0)),
        compiler_params=_cp(("parallel",)),
    )(proj_main, proj_main, proj_main, pool_w, pool_scale)


def _conv_fwd(cur, prev_tail, w4):
    ext = jnp.concatenate([prev_tail, cur], axis=0)
    y = ext * w4[CONV_WIDTH - 1:CONV_WIDTH, :]
    for sft in range(1, CONV_WIDTH):
        y = y + pltpu.roll(ext, sft, 0) * w4[CONV_WIDTH - 1 - sft:CONV_WIDTH - sft, :]
    return y[HALO:, :]


def _l2n_heads(t):
    parts = []
    for h in range(DN_HEADS):
        th = t[:, h * DN_HEAD_DIM:(h + 1) * DN_HEAD_DIM]
        parts.append(th * lax.rsqrt(jnp.sum(th * th, axis=-1, keepdims=True) + NORM_EPS))
    return jnp.concatenate(parts, axis=1)


def _post_conv(yq, yk, yv):
    return _l2n_heads(_silu(yq)), _l2n_heads(_silu(yk)), _silu(yv)


def _gates(ba, alog_lane, dtb_lane):
    lane = lax.broadcasted_iota(jnp.int32, ba.shape, 1)
    beta = jax.nn.sigmoid(ba)
    g = -jnp.exp(alog_lane) * _softplus(ba + dtb_lane)
    return jnp.where(lane < DN_HEADS, beta, jnp.where(lane < 2 * DN_HEADS, g, 0.0))


def _front(x2, norm_w, g_in, pool_w, pool_scale, conv_full, alog_lane, dtb_lane, tm):
    s = x2.shape[0]

    def body(x_ref, nw_ref, pw_ref, ps_ref, cw_ref, al_ref, db_ref, g_hbm,
             pm_ref, pb_ref, yp_ref, qn_ref, kn_ref, vv_ref, gb_ref, wt_hbm,
             g_vmem, wt_vmem, tail_u, tail_qkv, sem):
        i = pl.program_id(0)

        @pl.when(i == 0)
        def _():
            cp = pltpu.make_async_copy(g_hbm, g_vmem, sem)
            cp.start()
            wt_vmem[D_MAIN:, :] = jnp.zeros((D_IN_PAD - D_MAIN, D_MODEL), BF16)
            tail_u[...] = jnp.zeros_like(tail_u)
            tail_qkv[...] = jnp.zeros_like(tail_qkv)
            cp.wait()
            for d in range(N_DEV):
                wt_vmem[W_IN_SHARD * d:W_IN_SHARD * (d + 1), :] = g_vmem[d]
            out = pltpu.make_async_copy(wt_vmem, wt_hbm, sem)
            out.start()
            out.wait()
        xhat, _ = _rms_hat(x_ref[...])
        n = (xhat * nw_ref[...]).astype(BF16)
        pm_ref[...] = _dot_nt_bf(n, wt_vmem[:D_MAIN, :])
        pb = _dot_nt_bf(n, wt_vmem[D_MAIN:, :])
        pb_ref[...] = pb
        u = pm_ref[:, :D_POOL]
        mixes = _pool_mix(u, tail_u[...], i, tm)
        tail_u[...] = u[tm - HALO:, :]
        gate = ps_ref[...] * _silu(pm_ref[:, D_POOL:2 * D_POOL])
        for gi in range(4):
            cols = slice(gi * POOL_GROUP, (gi + 1) * POOL_GROUP)
            yp_ref[:, cols] = _dot_bf(mixes[gi], pw_ref[gi]) * gate[:, cols]
        ys = []
        for c in range(3):
            cols = slice(c * D_DN, (c + 1) * D_DN)
            cur = pm_ref[:, 2 * D_POOL + c * D_DN:2 * D_POOL + (c + 1) * D_DN]
            ys.append(_conv_fwd(cur, tail_qkv[:, cols], cw_ref[:, cols]))
            tail_qkv[:, cols] = cur[tm - HALO:, :]
        qn, kn, vv = _post_conv(*ys)
        qn_ref[...] = qn
        kn_ref[...] = kn
        vv_ref[...] = vv
        gb_ref[...] = _gates(pb, al_ref[...], db_ref[...])

    tile = pl.BlockSpec((tm, D_DN), lambda i: (i, 0))
    lanes = pl.BlockSpec((tm, 128), lambda i: (i, 0))
    row = pl.BlockSpec((1, 128), lambda i: (0, 0))
    return pl.pallas_call(
        body, name="front", grid=(s // tm,),
        out_shape=(jax.ShapeDtypeStruct((s, D_MAIN), F32), jax.ShapeDtypeStruct((s, 128), F32),
                   jax.ShapeDtypeStruct((s, D_POOL), F32), jax.ShapeDtypeStruct((s, D_DN), F32),
                   jax.ShapeDtypeStruct((s, D_DN), F32), jax.ShapeDtypeStruct((s, D_DN), F32),
                   jax.ShapeDtypeStruct((s, 128), F32), jax.ShapeDtypeStruct((D_IN_PAD, D_MODEL), BF16)),
        in_specs=[pl.BlockSpec((tm, D_MODEL), lambda i: (i, 0)),
                  pl.BlockSpec((1, D_MODEL), lambda i: (0, 0)),
                  pl.BlockSpec((4, POOL_GROUP, POOL_GROUP), lambda i: (0, 0, 0)),
                  pl.BlockSpec((1, D_POOL), lambda i: (0, 0)),
                  pl.BlockSpec((CONV_WIDTH, 3 * D_DN), lambda i: (0, 0)), row, row,
                  pl.BlockSpec(memory_space=pl.ANY)],
        out_specs=(pl.BlockSpec((tm, D_MAIN), lambda i: (i, 0)), lanes, tile, tile, tile, tile, lanes,
                   pl.BlockSpec(memory_space=pl.ANY)),
        scratch_shapes=[pltpu.VMEM((N_DEV, W_IN_SHARD, D_MODEL), BF16), pltpu.VMEM((D_IN_PAD, D_MODEL), BF16),
                        pltpu.VMEM((HALO, D_POOL), F32), pltpu.VMEM((HALO, 3 * D_DN), F32),
                        pltpu.SemaphoreType.DMA],
        compiler_params=_cp(("arbitrary",)),
    )(x2, norm_w, pool_w, pool_scale, conv_full, alog_lane, dtb_lane, g_in)


def _qkv_specs(tp, which):
    def spec(col, n=None):
        if which == 0:
            return pl.BlockSpec((tp, D_DN), lambda i: (i, col))
        if which < 0:
            return _prev_halo_spec(tp, D_DN, col)
        return _next_halo_spec(tp, D_DN, col, n)
    return spec


def _dn_pre(proj_main, proj_ba, conv_full, alog_lane, dtb_lane, tp):
    s = proj_main.shape[0]

    def body(q_ref, k_ref, v_ref, qp_ref, kp_ref, vp_ref, cw_ref, ba_ref, al_ref, db_ref,
             qn_ref, kn_ref, vv_ref, gb_ref):
        i = pl.program_id(0)
        ys = []
        for j, (cur, prev) in enumerate(((q_ref, qp_ref), (k_ref, kp_ref), (v_ref, vp_ref))):
            ys.append(_conv_fwd(cur[...], _prev_tail(prev, i), cw_ref[:, j * D_DN:(j + 1) * D_DN]))
        qn, kn, vv = _post_conv(*ys)
        qn_ref[...] = qn
        kn_ref[...] = kn
        vv_ref[...] = vv
        gb_ref[...] = _gates(ba_ref[...], al_ref[...], db_ref[...])

    cur, prev = _qkv_specs(tp, 0), _qkv_specs(tp, -1)
    row = pl.BlockSpec((1, 128), lambda i: (0, 0))
    tile = pl.BlockSpec((tp, D_DN), lambda i: (i, 0))
    return pl.pallas_call(
        body, name="dn_pre", grid=(s // tp,),
        out_shape=(jax.ShapeDtypeStruct((s, D_DN), F32),) * 3 + (jax.ShapeDtypeStruct((s, 128), F32),),
        in_specs=[cur(2), cur(3), cur(4), prev(2), prev(3), prev(4),
                  pl.BlockSpec((CONV_WIDTH, 3 * D_DN), lambda i: (0, 0)),
                  pl.BlockSpec((tp, 128), lambda i: (i, 0)), row, row],
        out_specs=(tile, tile, tile, pl.BlockSpec((tp, 128), lambda i: (i, 0))),
        compiler_params=_cp(("parallel",)),
    )(proj_main, proj_main, proj_main, proj_main, proj_main, proj_main, conv_full, proj_ba, alog_lane, dtb_lane)


def _dn_block(q, k, v, gcol, bcol, state, dz, nw):
    nb, n, d = q.shape
    ii = lax.broadcasted_iota(jnp.int32, (n, n), 0)
    jj = lax.broadcasted_iota(jnp.int32, (n, n), 1)
    lower = ii >= jj
    eye = (ii == jj).astype(F32)
    g_row = jnp.sum(eye * gcol, axis=1, keepdims=True)
    gc_col = jnp.sum(jnp.where(lower, g_row, 0.0), axis=2, keepdims=True)
    gc_row = jnp.sum(eye * gc_col, axis=1, keepdims=True)
    decay = jnp.where(lower, jnp.exp(jnp.where(lower, gc_col - gc_row, 0.0)), 0.0)
    kb = k * bcol
    vb = v * bcol
    qs = q * (DN_HEAD_DIM ** -0.5)
    egc = jnp.exp(gc_col)
    akq = _mm(jnp.concatenate([kb, qs], axis=1), k, 1, 1, *_DN_PREC["akq"])
    a = jnp.where(ii > jj, akq[:, :n] * decay, 0.0)
    qk = akq[:, n:] * decay
    t = _tri_inv(a, *_DN_PREC["inv"])
    uw = _mm(t, jnp.concatenate([vb, kb * egc], axis=2), 1, 0, *_DN_PREC["uw"])
    wq = jnp.concatenate([uw[:, :, d:], qs * egc], axis=1)
    g_last = gc_col[:, n - 1:n, :]
    k_dec = k * jnp.exp(g_last - gc_col)
    e_last = jnp.exp(g_last)
    os_ = []
    for c in range(nb // DN_HEADS):
        sl = slice(c * DN_HEADS, (c + 1) * DN_HEADS)
        ws = _mm(wq[sl], state, 1, 0, *_DN_PREC["ws"])
        v_new = uw[sl, :, :d] - ws[:, :n]
        os_.append(ws[:, n:] + _mm(qk[sl], v_new, 1, 0, *_DN_PREC["ov"]))
        state = state * e_last[sl] + _mm(k_dec[sl], v_new, 0, 0, *_DN_PREC["st"])
    o = jnp.concatenate(os_, axis=0)
    y = o * lax.rsqrt(jnp.mean(o * o, axis=-1, keepdims=True) + NORM_EPS) * nw * _silu(dz)
    return y, state


def _dn_block_args(gc, q_ref, k_ref, v_ref, gb_ref, dz_ref):
    qs, ks, vs, gs, bs, zs = [], [], [], [], [], []
    for cc in range(gc):
        r = slice(cc * CHUNK, (cc + 1) * CHUNK)
        gbv = gb_ref[r, :]
        for h in range(DN_HEADS):
            cols = slice(h * DN_HEAD_DIM, (h + 1) * DN_HEAD_DIM)
            qs.append(q_ref[r, cols])
            ks.append(k_ref[r, cols])
            vs.append(v_ref[r, cols])
            zs.append(dz_ref[r, cols])
            gs.append(gbv[:, DN_HEADS + h:DN_HEADS + h + 1])
            bs.append(gbv[:, h:h + 1])
    return tuple(jnp.stack(t, axis=0) for t in (qs, ks, vs, gs, bs, zs))


def _dn_scan_fwd(qn, kn, vv, gb, proj_main, dn_norm_w, gc):
    s = qn.shape[0]
    nchunk = s // CHUNK
    rows = gc * CHUNK

    def body(q_ref, k_ref, v_ref, gb_ref, dz_ref, nw_ref, y_ref, ss_ref, state):
        @pl.when(pl.program_id(0) == 0)
        def _():
            state[...] = jnp.zeros_like(state)
        q, k, v, gcol, bcol, dz = _dn_block_args(gc, q_ref, k_ref, v_ref, gb_ref, dz_ref)
        st = state[...]
        ss_ref[0] = st
        y, new = _dn_block(q, k, v, gcol, bcol, st, dz, nw_ref[...])
        state[...] = new
        for cc in range(gc):
            for h in range(DN_HEADS):
                y_ref[cc * CHUNK:(cc + 1) * CHUNK, h * DN_HEAD_DIM:(h + 1) * DN_HEAD_DIM] = y[cc * DN_HEADS + h]

    tile = pl.BlockSpec((rows, D_DN), lambda i: (i, 0))
    return pl.pallas_call(
        body, name="dn_scan_fwd", grid=(nchunk // gc,),
        out_shape=(jax.ShapeDtypeStruct((s, D_DN), F32),
                   jax.ShapeDtypeStruct((nchunk // gc, DN_HEADS, DN_HEAD_DIM, DN_HEAD_DIM), F32)),
        in_specs=[tile, tile, tile, pl.BlockSpec((rows, 128), lambda i: (i, 0)),
                  pl.BlockSpec((rows, D_DN), lambda i: (i, 5)), pl.BlockSpec((1, 128), lambda i: (0, 0))],
        out_specs=(tile, pl.BlockSpec((1, DN_HEADS, DN_HEAD_DIM, DN_HEAD_DIM), lambda i: (i, 0, 0, 0))),
        scratch_shapes=[pltpu.VMEM((DN_HEADS, DN_HEAD_DIM, DN_HEAD_DIM), F32)],
        compiler_params=_cp(("arbitrary",)),
    )(qn, kn, vv, gb, proj_main, dn_norm_w)


def _out_proj_loss(y_pool, y_dn, x2, tgt, w_out_full, fnw, tm):
    s = x2.shape[0]

    def body(yp_ref, yd_ref, x_ref, t_ref, wo_ref, fw_ref,
             dh_ref, dyp_ref, dyd_ref, gwo_ref, gfw_ref, loss_ref):
        @pl.when(pl.program_id(0) == 0)
        def _():
            gwo_ref[...] = jnp.zeros_like(gwo_ref)
            gfw_ref[...] = jnp.zeros_like(gfw_ref)
            loss_ref[...] = jnp.zeros_like(loss_ref)
        y = jnp.concatenate([yp_ref[...], yd_ref[...]], axis=1).astype(BF16)
        wo = wo_ref[...]
        h = x_ref[...] + jnp.dot(y, wo, preferred_element_type=F32)
        hn, r = _rms_hat(h)
        fw = fw_ref[...]
        err = hn * fw - t_ref[...]
        loss_ref[...] += 0.5 * jnp.sum(jnp.sum(err * err, axis=-1, keepdims=True) / D_MODEL, axis=0, keepdims=True)
        dout = err / D_MODEL
        gfw_ref[...] += jnp.sum(dout * hn, axis=0, keepdims=True)
        dhn = dout * fw
        dh = r * (dhn - hn * jnp.mean(dhn * hn, axis=-1, keepdims=True))
        dh_ref[...] = dh
        dhb = dh.astype(BF16)
        dy = _dot_nt_bf(dhb, wo)
        dyp_ref[...] = dy[:, :D_POOL]
        dyd_ref[...] = dy[:, D_POOL:]
        gwo_ref[...] += _dot_tn_bf(y, dhb)

    half = pl.BlockSpec((tm, D_POOL), lambda i: (i, 0))
    full = pl.BlockSpec((tm, D_MODEL), lambda i: (i, 0))
    return pl.pallas_call(
        body, name="out_proj_loss", grid=(s // tm,),
        out_shape=(jax.ShapeDtypeStruct((s, D_MODEL), F32), jax.ShapeDtypeStruct((s, D_POOL), F32),
                   jax.ShapeDtypeStruct((s, D_DN), F32), jax.ShapeDtypeStruct((D_MODEL, D_MODEL), F32),
                   jax.ShapeDtypeStruct((1, D_MODEL), F32), jax.ShapeDtypeStruct((1, 128), F32)),
        in_specs=[half, half, full, full, pl.BlockSpec((D_MODEL, D_MODEL), lambda i: (0, 0)),
                  pl.BlockSpec((1, D_MODEL), lambda i: (0, 0))],
        out_specs=(full, half, half, pl.BlockSpec((D_MODEL, D_MODEL), lambda i: (0, 0)),
                   pl.BlockSpec((1, D_MODEL), lambda i: (0, 0)), pl.BlockSpec((1, 128), lambda i: (0, 0))),
        compiler_params=_cp(("arbitrary",)),
    )(y_pool, y_dn, x2, tgt, w_out_full, fnw)


def _dn_scan_bwd(qn, kn, vv, gb, proj_main, dn_norm_w, states, dy_dn, gc):
    s = qn.shape[0]
    nchunk = s // CHUNK
    nstep = nchunk // gc
    rows = gc * CHUNK

    def body(q_ref, k_ref, v_ref, gb_ref, dz_ref, nw_ref, ss_ref, dy_ref,
             dq_ref, dk_ref, dv_ref, dgb_ref, ddz_ref, dnw_ref, dstate):
        @pl.when(pl.program_id(0) == 0)
        def _():
            dstate[...] = jnp.zeros_like(dstate)
            dnw_ref[...] = jnp.zeros_like(dnw_ref)
        lane = lax.broadcasted_iota(jnp.int32, (CHUNK, 128), 1)
        q, k, v, gcol, bcol, dz = _dn_block_args(gc, q_ref, k_ref, v_ref, gb_ref, dz_ref)
        dy = jnp.stack([dy_ref[cc * CHUNK:(cc + 1) * CHUNK, h * DN_HEAD_DIM:(h + 1) * DN_HEAD_DIM]
                        for cc in range(gc) for h in range(DN_HEADS)], axis=0)
        _, vjp = jax.vjp(_dn_block, q, k, v, gcol, bcol, ss_ref[0], dz, nw_ref[...])
        dq, dk, dv, dg, db, dst, ddz, dnw = vjp((dy, dstate[...]))
        dstate[...] = dst
        dnw_ref[...] += dnw
        for cc in range(gc):
            r = slice(cc * CHUNK, (cc + 1) * CHUNK)
            dgb = jnp.zeros((CHUNK, 128), F32)
            for h in range(DN_HEADS):
                b = cc * DN_HEADS + h
                cols = slice(h * DN_HEAD_DIM, (h + 1) * DN_HEAD_DIM)
                for ref, val in zip((dq_ref, dk_ref, dv_ref, ddz_ref), (dq, dk, dv, ddz)):
                    ref[r, cols] = val[b]
                dgb = dgb + jnp.where(lane == h, db[b], 0.0) + jnp.where(lane == DN_HEADS + h, dg[b], 0.0)
            dgb_ref[r, :] = dgb

    rev = lambda i: (nstep - 1 - i, 0)
    tile = pl.BlockSpec((rows, D_DN), rev)
    lanes = pl.BlockSpec((rows, 128), rev)
    return pl.pallas_call(
        body, name="dn_scan_bwd", grid=(nstep,),
        out_shape=(jax.ShapeDtypeStruct((s, D_DN), F32),) * 3
        + (jax.ShapeDtypeStruct((s, 128), F32), jax.ShapeDtypeStruct((s, D_DN), F32),
           jax.ShapeDtypeStruct((1, 128), F32)),
        in_specs=[tile, tile, tile, lanes, pl.BlockSpec((rows, D_DN), lambda i: (nstep - 1 - i, 5)),
                  pl.BlockSpec((1, 128), lambda i: (0, 0)),
                  pl.BlockSpec((1, DN_HEADS, DN_HEAD_DIM, DN_HEAD_DIM), lambda i: (nstep - 1 - i, 0, 0, 0)), tile],
        out_specs=(tile, tile, tile, lanes, tile, pl.BlockSpec((1, 128), lambda i: (0, 0))),
        scratch_shapes=[pltpu.VMEM((DN_HEADS, DN_HEAD_DIM, DN_HEAD_DIM), F32)],
        compiler_params=_cp(("arbitrary",)),
    )(qn, kn, vv, gb, proj_main, dn_norm_w, states, dy_dn)


def _dn_pre_bwd1(proj_main, proj_ba, conv_full, alog_lane, dtb_lane, dqn, dkn, dvv, dgb, tp):
    s = proj_main.shape[0]

    def body(q_ref, k_ref, v_ref, qp_ref, kp_ref, vp_ref, cw_ref, ba_ref, al_ref, db_ref,
             dqn_ref, dkn_ref, dvv_ref, dgb_ref,
             dcq_ref, dck_ref, dcv_ref, dba_ref, dcw_ref, dal_ref, ddb_ref):
        i = pl.program_id(0)

        @pl.when(i == 0)
        def _():
            dcw_ref[...] = jnp.zeros_like(dcw_ref)
            dal_ref[...] = jnp.zeros_like(dal_ref)
            ddb_ref[...] = jnp.zeros_like(ddb_ref)
        curs = (q_ref[...], k_ref[...], v_ref[...])
        tails = (_prev_tail(qp_ref, i), _prev_tail(kp_ref, i), _prev_tail(vp_ref, i))
        ys = [_conv_fwd(curs[j], tails[j], cw_ref[:, j * D_DN:(j + 1) * D_DN]) for j in range(3)]
        _, vjp = jax.vjp(_post_conv, *ys)
        dys = vjp((dqn_ref[...], dkn_ref[...], dvv_ref[...]))
        for j, (dy, out) in enumerate(zip(dys, (dcq_ref, dck_ref, dcv_ref))):
            out[...] = dy
            for sft in range(CONV_WIDTH):
                xs = curs[j] if sft == 0 else _shift_down(curs[j], tails[j], sft)
                row = CONV_WIDTH - 1 - sft
                dcw_ref[row:row + 1, j * D_DN:(j + 1) * D_DN] += jnp.sum(dy * xs, axis=0, keepdims=True)
        _, gvjp = jax.vjp(_gates, ba_ref[...], al_ref[...], db_ref[...])
        dba, dal, ddb = gvjp(dgb_ref[...])
        dba_ref[...] = dba
        dal_ref[...] += dal
        ddb_ref[...] += ddb

    cur, prev = _qkv_specs(tp, 0), _qkv_specs(tp, -1)
    row = pl.BlockSpec((1, 128), lambda i: (0, 0))
    tile = pl.BlockSpec((tp, D_DN), lambda i: (i, 0))
    lanes = pl.BlockSpec((tp, 128), lambda i: (i, 0))
    cw = pl.BlockSpec((CONV_WIDTH, 3 * D_DN), lambda i: (0, 0))
    return pl.pallas_call(
        body, name="dn_pre_bwd1", grid=(s // tp,),
        out_shape=(jax.ShapeDtypeStruct((s, D_DN), F32),) * 3
        + (jax.ShapeDtypeStruct((s, 128), F32), jax.ShapeDtypeStruct((CONV_WIDTH, 3 * D_DN), F32),
           jax.ShapeDtypeStruct((1, 128), F32), jax.ShapeDtypeStruct((1, 128), F32)),
        in_specs=[cur(2), cur(3), cur(4), prev(2), prev(3), prev(4), cw, lanes, row, row, tile, tile, tile, lanes],
        out_specs=(tile, tile, tile, lanes, cw, row, row),
        compiler_params=_cp(("arbitrary",)),
    )(proj_main, proj_main, proj_main, proj_main, proj_main, proj_main, conv_full, proj_ba, alog_lane, dtb_lane,
      dqn, dkn, dvv, dgb)


def _pool_bwd1(proj_main, pool_w, pool_scale, dyp, tp):
    s = proj_main.shape[0]

    def body(u_ref, up_ref, z_ref, pw_ref, ps_ref, dy_ref, dz_ref, dwin_ref, dpw_ref, dps_ref):
        i = pl.program_id(0)

        @pl.when(i == 0)
        def _():
            dpw_ref[...] = jnp.zeros_like(dpw_ref)
            dps_ref[...] = jnp.zeros_like(dps_ref)
        u = u_ref[...]
        z = z_ref[...]
        dy = dy_ref[...]
        ps = ps_ref[...]
        mixes = _pool_mix(u, _prev_tail(up_ref, i), i, tp)
        sg = jax.nn.sigmoid(z)
        sz = z * sg
        dsz = sg * (1.0 + z * (1.0 - sg))
        for gi, w in enumerate(POOL_WINDOWS):
            cols = slice(gi * POOL_GROUP, (gi + 1) * POOL_GROUP)
            mixw = _dot_bf(mixes[gi], pw_ref[gi])
            dmixw = dy[:, cols] * ps[:, cols] * sz[:, cols]
            dps_ref[:, cols] += jnp.sum(dy[:, cols] * mixw * sz[:, cols], axis=0, keepdims=True)
            dz_ref[:, cols] = dy[:, cols] * mixw * ps[:, cols] * dsz[:, cols]
            dpw_ref[gi] += _dot_tn_bf(mixes[gi], dmixw)
            dmix = _dot_nt_bf(dmixw, pw_ref[gi])
            dwin_ref[:, cols] = dmix / _pool_counts(i, tp, w)

    tile = pl.BlockSpec((tp, D_POOL), lambda i: (i, 0))
    pw = pl.BlockSpec((4, POOL_GROUP, POOL_GROUP), lambda i: (0, 0, 0))
    ps = pl.BlockSpec((1, D_POOL), lambda i: (0, 0))
    return pl.pallas_call(
        body, name="pool_bwd1", grid=(s // tp,),
        out_shape=(jax.ShapeDtypeStruct((s, D_POOL), F32), jax.ShapeDtypeStruct((s, D_POOL), F32),
                   jax.ShapeDtypeStruct((4, POOL_GROUP, POOL_GROUP), F32), jax.ShapeDtypeStruct((1, D_POOL), F32)),
        in_specs=[tile, _prev_halo_spec(tp, D_POOL, 0),
                  pl.BlockSpec((tp, D_POOL), lambda i: (i, 1)), pw, ps, tile],
        out_specs=(tile, tile, pw, ps),
        compiler_params=_cp(("arbitrary",)),
    )(proj_main, proj_main, proj_main, pool_w, pool_scale, dyp)


def _back(proj_main, proj_ba, dyp, dqn, dkn, dvv, dgb, ddz, x2, dh, norm_w, pool_w, pool_scale, conv_full,
          alog_lane, dtb_lane, wt_full, tm):
    s = x2.shape[0]
    nstep = s // tm
    per = tm // HALO

    def body(u_ref, z_ref, q_ref, k_ref, v_ref, up_ref, qp_ref, kp_ref, vp_ref, ba_ref,
             dyp_ref, dqn_ref, dkn_ref, dvv_ref, dgb_ref, ddz_ref, x_ref, dh_ref,
             nw_ref, pw_ref, ps_ref, cw_ref, al_ref, db_ref, wt_hbm,
             gx_ref, p_hbm, gnw_ref, dpw_ref, dps_ref, dcw_ref, dal_ref, ddb_ref,
             wt_vmem, acc, blk, head_dc, head_dw, sem, osem):
        j = pl.program_id(0)
        i = nstep - 1 - j

        @pl.when(j == 0)
        def _():
            cp = pltpu.make_async_copy(wt_hbm, wt_vmem, sem)
            cp.start()
            acc[...] = jnp.zeros_like(acc)
            for ref in (gnw_ref, dpw_ref, dps_ref, dcw_ref, dal_ref, ddb_ref, head_dc, head_dw):
                ref[...] = jnp.zeros_like(ref)
            cp.wait()

        u = u_ref[...]
        z = z_ref[...]
        dy = dyp_ref[...]
        ps = ps_ref[...]
        mixes = _pool_mix(u, _prev_tail(up_ref, i), i, tm)
        sg = jax.nn.sigmoid(z)
        sz = z * sg
        dsz = sg * (1.0 + z * (1.0 - sg))
        dzs, dwins = [], []
        for gi, w in enumerate(POOL_WINDOWS):
            cols = slice(gi * POOL_GROUP, (gi + 1) * POOL_GROUP)
            mixw = _dot_bf(mixes[gi], pw_ref[gi])
            dmixw = dy[:, cols] * ps[:, cols] * sz[:, cols]
            dps_ref[:, cols] += jnp.sum(dy[:, cols] * mixw * sz[:, cols], axis=0, keepdims=True)
            dzs.append(dy[:, cols] * mixw * ps[:, cols] * dsz[:, cols])
            dpw_ref[gi] += _dot_tn_bf(mixes[gi], dmixw)
            dwins.append(_dot_nt_bf(dmixw, pw_ref[gi]) / _pool_counts(i, tm, w))
        dzp = jnp.concatenate(dzs, axis=1)
        dw = jnp.concatenate(dwins, axis=1)
        ext = jnp.concatenate([dw, head_dw[...]], axis=0)
        m = ext.shape[0]
        a2 = ext + pltpu.roll(ext, m - 1, 0)
        a4 = a2 + pltpu.roll(a2, m - 2, 0)
        a8 = a4 + pltpu.roll(a4, m - 4, 0)
        a16 = a8 + pltpu.roll(a8, m - 8, 0)
        dup = jnp.concatenate(
            [acc_w[:tm, gi * POOL_GROUP:(gi + 1) * POOL_GROUP]
             - dw[:, gi * POOL_GROUP:(gi + 1) * POOL_GROUP] * _pool_counts(i, tm, w)
             for gi, (w, acc_w) in enumerate(zip(POOL_WINDOWS, (a2, a4, a8, a16)))], axis=1)
        head_dw[...] = dw[:HALO, :]

        curs = (q_ref[...], k_ref[...], v_ref[...])
        tails = (_prev_tail(qp_ref, i), _prev_tail(kp_ref, i), _prev_tail(vp_ref, i))
        ys = [_conv_fwd(curs[c], tails[c], cw_ref[:, c * D_DN:(c + 1) * D_DN]) for c in range(3)]
        _, vjp = jax.vjp(_post_conv, *ys)
        dys = vjp((dqn_ref[...], dkn_ref[...], dvv_ref[...]))
        dxs = []
        for c, dyc in enumerate(dys):
            cols = slice(c * D_DN, (c + 1) * D_DN)
            w4 = cw_ref[:, cols]
            for sft in range(CONV_WIDTH):
                xs = curs[c] if sft == 0 else _shift_down(curs[c], tails[c], sft)
                row = CONV_WIDTH - 1 - sft
                dcw_ref[row:row + 1, cols] += jnp.sum(dyc * xs, axis=0, keepdims=True)
            head = head_dc[:, cols]
            dx = dyc * w4[CONV_WIDTH - 1:CONV_WIDTH, :]
            for sft in range(1, CONV_WIDTH):
                dx = dx + _shift_up(dyc, head, sft) * w4[CONV_WIDTH - 1 - sft:CONV_WIDTH - sft, :]
            dxs.append(dx)
            head_dc[:, cols] = dyc[:HALO, :]
        _, gvjp = jax.vjp(_gates, ba_ref[...], al_ref[...], db_ref[...])
        dba, dal, ddb = gvjp(dgb_ref[...])
        dal_ref[...] += dal
        ddb_ref[...] += ddb

        dbab = dba.astype(BF16)
        xhat, r = _rms_hat(x_ref[...])
        nw = nw_ref[...]
        n = (xhat * nw).astype(BF16)
        acc[D_MAIN:, :] += _dot_tn_bf(dbab, n)
        dn = jnp.dot(dbab, wt_vmem[D_MAIN:, :], preferred_element_type=F32)
        for cb, d in enumerate((dup, dzp, dxs[0], dxs[1], dxs[2], ddz_ref[...])):
            rows = slice(cb * D_POOL, (cb + 1) * D_POOL)
            dpart = d.astype(BF16)
            acc[rows, :] += _dot_tn_bf(dpart, n)
            dn = dn + jnp.dot(dpart, wt_vmem[rows, :], preferred_element_type=F32)
        gnw_ref[...] += jnp.sum(dn * xhat, axis=0, keepdims=True)
        dxh = dn * nw
        gx_ref[...] = dh_ref[...] + r * (dxh - xhat * jnp.mean(dxh * xhat, axis=-1, keepdims=True))

        @pl.when(j == nstep - 1)
        def _():
            def out(d):
                return pltpu.make_async_copy(blk.at[d % 2], p_hbm.at[d], osem.at[d % 2])
            for d in range(N_DEV):
                if d >= 2:
                    out(d - 2).wait()
                blk[d % 2] = acc[W_IN_SHARD * d:W_IN_SHARD * (d + 1), :]
                out(d).start()
            out(N_DEV - 2).wait()
            out(N_DEV - 1).wait()

    def col(c):
        return pl.BlockSpec((tm, D_POOL), lambda j: (nstep - 1 - j, c))

    def halo(c):
        return pl.BlockSpec((HALO, D_POOL), lambda j: (jnp.maximum((nstep - 1 - j) * per - 1, 0), c))

    rev = lambda j: (nstep - 1 - j, 0)
    part = pl.BlockSpec((tm, D_POOL), rev)
    lanes = pl.BlockSpec((tm, 128), rev)
    full = pl.BlockSpec((tm, D_MODEL), rev)
    row = pl.BlockSpec((1, D_MODEL), lambda j: (0, 0))
    lrow = pl.BlockSpec((1, 128), lambda j: (0, 0))
    pw = pl.BlockSpec((4, POOL_GROUP, POOL_GROUP), lambda j: (0, 0, 0))
    psp = pl.BlockSpec((1, D_POOL), lambda j: (0, 0))
    cw = pl.BlockSpec((CONV_WIDTH, 3 * D_DN), lambda j: (0, 0))
    return pl.pallas_call(
        body, name="back", grid=(nstep,),
        out_shape=(jax.ShapeDtypeStruct((s, D_MODEL), F32),
                   jax.ShapeDtypeStruct((N_DEV, W_IN_SHARD, D_MODEL), F32), jax.ShapeDtypeStruct((1, D_MODEL), F32),
                   jax.ShapeDtypeStruct((4, POOL_GROUP, POOL_GROUP), F32), jax.ShapeDtypeStruct((1, D_POOL), F32),
                   jax.ShapeDtypeStruct((CONV_WIDTH, 3 * D_DN), F32),
                   jax.ShapeDtypeStruct((1, 128), F32), jax.ShapeDtypeStruct((1, 128), F32)),
        in_specs=[col(0), col(1), col(2), col(3), col(4), halo(0), halo(2), halo(3), halo(4), lanes,
                  part, part, part, part, lanes, part, full, full,
                  row, pw, psp, cw, lrow, lrow, pl.BlockSpec(memory_space=pl.ANY)],
        out_specs=(full, pl.BlockSpec(memory_space=pl.ANY), row, pw, psp, cw, lrow, lrow),
        scratch_shapes=[pltpu.VMEM((D_IN_PAD, D_MODEL), BF16), pltpu.VMEM((D_IN_PAD, D_MODEL), F32),
                        pltpu.VMEM((2, W_IN_SHARD, D_MODEL), F32),
                        pltpu.VMEM((HALO, 3 * D_DN), F32), pltpu.VMEM((HALO, D_POOL), F32),
                        pltpu.SemaphoreType.DMA, pltpu.SemaphoreType.DMA((2,))],
        compiler_params=_cp(("arbitrary",)),
    )(proj_main, proj_main, proj_main, proj_main, proj_main, proj_main, proj_main, proj_main, proj_main, proj_ba,
      dyp, dqn, dkn, dvv, dgb, ddz, x2, dh, norm_w, pool_w, pool_scale, conv_full, alog_lane, dtb_lane, wt_full)


def _in_proj_bwd(dwin, dzp, dcq, dck, dcv, ddz, dba, x2, dh, norm_w, wt_full, conv_full, tm):
    s = x2.shape[0]
    nstep = s // tm

    def body(dwin_ref, dzp_ref, dcq_ref, dck_ref, dcv_ref, ddz_ref, dba_ref, x_ref, dh_ref, nw_ref, cw_ref, wt_hbm,
             gx_ref, p_hbm, gnw_ref, wt_vmem, acc, blk, head_dc, head_dw, sem, osem):
        j = pl.program_id(0)
        i = nstep - 1 - j

        @pl.when(j == 0)
        def _():
            cp = pltpu.make_async_copy(wt_hbm, wt_vmem, sem)
            cp.start()
            acc[...] = jnp.zeros_like(acc)
            gnw_ref[...] = jnp.zeros_like(gnw_ref)
            head_dc[...] = jnp.zeros_like(head_dc)
            head_dw[...] = jnp.zeros_like(head_dw)
            cp.wait()
        dw = dwin_ref[...]
        ext = jnp.concatenate([dw, head_dw[...]], axis=0)
        m = ext.shape[0]
        a2 = ext + pltpu.roll(ext, m - 1, 0)
        a4 = a2 + pltpu.roll(a2, m - 2, 0)
        a8 = a4 + pltpu.roll(a4, m - 4, 0)
        a16 = a8 + pltpu.roll(a8, m - 8, 0)
        dup = jnp.concatenate(
            [acc_w[:tm, gi * POOL_GROUP:(gi + 1) * POOL_GROUP]
             - dw[:, gi * POOL_GROUP:(gi + 1) * POOL_GROUP] * _pool_counts(i, tm, w)
             for gi, (w, acc_w) in enumerate(zip(POOL_WINDOWS, (a2, a4, a8, a16)))], axis=1)
        head_dw[...] = dw[:HALO, :]
        dxs = []
        for c, ref in enumerate((dcq_ref, dck_ref, dcv_ref)):
            cols = slice(c * D_DN, (c + 1) * D_DN)
            w4 = cw_ref[:, cols]
            dy = ref[...]
            head = head_dc[:, cols]
            dx = dy * w4[CONV_WIDTH - 1:CONV_WIDTH, :]
            for sft in range(1, CONV_WIDTH):
                dx = dx + _shift_up(dy, head, sft) * w4[CONV_WIDTH - 1 - sft:CONV_WIDTH - sft, :]
            dxs.append(dx)
            head_dc[:, cols] = dy[:HALO, :]
        dbab = dba_ref[...].astype(BF16)
        xhat, r = _rms_hat(x_ref[...])
        nw = nw_ref[...]
        n = (xhat * nw).astype(BF16)
        acc[D_MAIN:, :] += _dot_tn_bf(dbab, n)
        dn = jnp.dot(dbab, wt_vmem[D_MAIN:, :], preferred_element_type=F32)
        for cb, d in enumerate((dup, dzp_ref[...], dxs[0], dxs[1], dxs[2], ddz_ref[...])):
            rows = slice(cb * D_POOL, (cb + 1) * D_POOL)
            dpart = d.astype(BF16)
            acc[rows, :] += _dot_tn_bf(dpart, n)
            dn = dn + jnp.dot(dpart, wt_vmem[rows, :], preferred_element_type=F32)
        gnw_ref[...] += jnp.sum(dn * xhat, axis=0, keepdims=True)
        dxh = dn * nw
        gx_ref[...] = dh_ref[...] + r * (dxh - xhat * jnp.mean(dxh * xhat, axis=-1, keepdims=True))

        @pl.when(j == nstep - 1)
        def _():
            def out(d):
                return pltpu.make_async_copy(blk.at[d % 2], p_hbm.at[d], osem.at[d % 2])
            for d in range(N_DEV):
                if d >= 2:
                    out(d - 2).wait()
                blk[d % 2] = acc[W_IN_SHARD * d:W_IN_SHARD * (d + 1), :]
                out(d).start()
            out(N_DEV - 2).wait()
            out(N_DEV - 1).wait()

    rev = lambda j: (nstep - 1 - j, 0)
    part = pl.BlockSpec((tm, D_POOL), rev)
    full = pl.BlockSpec((tm, D_MODEL), rev)
    row = pl.BlockSpec((1, D_MODEL), lambda j: (0, 0))
    return pl.pallas_call(
        body, name="in_proj_bwd", grid=(nstep,),
        out_shape=(jax.ShapeDtypeStruct((s, D_MODEL), F32),
                   jax.ShapeDtypeStruct((N_DEV, W_IN_SHARD, D_MODEL), F32), jax.ShapeDtypeStruct((1, D_MODEL), F32)),
        in_specs=[part] * 6 + [pl.BlockSpec((tm, 128), rev), full, full, row,
                               pl.BlockSpec((CONV_WIDTH, 3 * D_DN), lambda j: (0, 0)),
                               pl.BlockSpec(memory_space=pl.ANY)],
        out_specs=(full, pl.BlockSpec(memory_space=pl.ANY), row),
        scratch_shapes=[pltpu.VMEM((D_IN_PAD, D_MODEL), BF16), pltpu.VMEM((D_IN_PAD, D_MODEL), F32),
                        pltpu.VMEM((2, W_IN_SHARD, D_MODEL), F32),
                        pltpu.VMEM((HALO, 3 * D_DN), F32), pltpu.VMEM((HALO, D_POOL), F32),
                        pltpu.SemaphoreType.DMA, pltpu.SemaphoreType.DMA((2,))],
        compiler_params=_cp(("arbitrary",)),
    )(dwin, dzp, dcq, dck, dcv, ddz, dba, x2, dh, norm_w, conv_full, wt_full)


def _adamw_math(w, g, m, v):
    m = ADAM_B1 * m + (1.0 - ADAM_B1) * g
    v = ADAM_B2 * v + (1.0 - ADAM_B2) * (g * g)
    m_hat = m / (1.0 - ADAM_B1 ** ADAM_STEP)
    v_hat = v / (1.0 - ADAM_B2 ** ADAM_STEP)
    delta = -ADAM_LR * (m_hat / (jnp.sqrt(v_hat) + ADAM_EPS) + ADAM_WD * w)
    return delta, m, v


def _adamw_sharded(params):
    k = len(params)

    def body(*refs):
        ins, outs = refs[:4 * k], refs[4 * k:]
        for p in range(k):
            w, g, m, v = (r[...] for r in ins[4 * p:4 * p + 4])
            d, nm, nv = _adamw_math(w, g, m, v)
            outs[3 * p][...] = d
            outs[3 * p + 1][...] = nm
            outs[3 * p + 2][...] = nv

    flat = [a for p in params for a in p]
    out_shape = tuple(jax.ShapeDtypeStruct(p[0].shape, F32) for p in params for _ in range(3))
    res = pl.pallas_call(body, name="adamw_sharded", out_shape=out_shape, compiler_params=_cp())(*flat)
    return [tuple(res[3 * p:3 * p + 3]) for p in range(k)]


def _adamw_replicated(gath_a, gath_b, pool, rows):
    nrow = len(rows)

    def body(*refs):
        ga_ref, gb_ref = refs[:2]
        ins = refs[2:2 + 3 * (nrow + 1)]
        outs = refs[2 + 3 * (nrow + 1):]

        def total(ref):
            g = ref[0]
            for d in range(1, N_DEV):
                g = g + ref[d]
            return g

        def update(g, wmv, o):
            w, m, v = (r[...] for r in wmv)
            dl, nm, nv = _adamw_math(w, g, m, v)
            o[0][...] = g
            o[1][...] = dl
            o[2][...] = nm
            o[3][...] = nv

        update(total(ga_ref), ins[:3], outs[:4])
        gb = total(gb_ref)
        for r in range(nrow):
            n = ins[3 * (r + 1)].shape[1]
            update(gb[r:r + 1, :n], ins[3 * (r + 1):3 * (r + 2)], outs[4 * (r + 1):4 * (r + 2)])
        outs[4 * (nrow + 1)][...] = gb[nrow:nrow + 1, 0:1]

    flat = list(pool) + [a for wmv in rows for a in wmv]
    out_shape = ((jax.ShapeDtypeStruct(pool[0].shape, F32),) * 4
                 + tuple(jax.ShapeDtypeStruct(wmv[0].shape, F32) for wmv in rows for _ in range(4))
                 + (jax.ShapeDtypeStruct((1, 1), F32),))
    res = pl.pallas_call(body, name="adamw_replicated", out_shape=out_shape, compiler_params=_cp())(
        gath_a, gath_b, *flat)
    return [res[4 * k:4 * k + 4] for k in range(nrow + 1)], res[-1]


_ROW_ORDER = ("norm_w", "final_norm_w", "pool_scale", "dn_norm_w", "a_log", "dt_bias")


def _pack_rows(vectors):
    out = [jnp.pad(v.reshape(-1), (0, D_MODEL - v.size)) for v in vectors]
    out += [jnp.zeros((D_MODEL,), F32)] * (8 - len(out))
    return jnp.stack(out, axis=0)


def _lane_row(vec4, start):
    return jnp.pad(vec4.reshape(-1), (start, 128 - start - vec4.size)).reshape(1, 128)


def kernel(x, norm_w, w_in, pool_w, pool_scale, conv_w, a_log, dt_bias, dn_norm_w, w_out, final_norm_w, loss_target, m_norm_w, m_w_in, m_pool_w, m_pool_scale, m_conv_w, m_a_log, m_dt_bias, m_dn_norm_w, m_w_out, m_final_norm_w, v_norm_w, v_w_in, v_pool_w, v_pool_scale, v_conv_w, v_a_log, v_dt_bias, v_dn_norm_w, v_w_out, v_final_norm_w):
    s = x.shape[1]
    tm = min(512, s)
    tmb = min(256, s)
    tp = min(512, s)
    x2 = x[0]
    tgt = loss_target[0]
    wt, m_wt, v_wt = w_in[0].T, m_w_in[0].T, v_w_in[0].T

    g_in, g_out, g_conv = _gather_weights(wt, w_out[0], conv_w[0])
    w_out_full = g_out.reshape(D_MODEL, D_MODEL)
    conv_full = g_conv.transpose(1, 0, 2).reshape(CONV_WIDTH, 3 * D_DN)
    alog_lane = _lane_row(a_log, DN_HEADS)
    dtb_lane = _lane_row(dt_bias, DN_HEADS)
    fnw = final_norm_w.reshape(1, D_MODEL)

    proj_main, proj_ba, y_pool, qn, kn, vv, gb, wt_full = _front(
        x2, norm_w, g_in, pool_w[0], pool_scale, conv_full, alog_lane, dtb_lane, tm)
    y_dn, states = _dn_scan_fwd(qn, kn, vv, gb, proj_main, dn_norm_w, DN_CHUNKS_PER_STEP)

    dh, dyp, dyd, g_wout, g_fnw, loss_part = _out_proj_loss(y_pool, y_dn, x2, tgt, w_out_full, fnw, tm)
    dqn, dkn, dvv, dgb, ddz, g_dnw = _dn_scan_bwd(qn, kn, vv, gb, proj_main, dn_norm_w, states, dyd,
                                                  DN_CHUNKS_PER_STEP)
    grad_x2, p_in, g_nw, g_pw, g_ps, g_conv_full, g_al, g_db = _back(
        proj_main, proj_ba, dyp, dqn, dkn, dvv, dgb, ddz, x2, dh, norm_w, pool_w[0], pool_scale, conv_full,
        alog_lane, dtb_lane, wt_full, tmb)

    p_out = g_wout.reshape(N_DEV, D_MODEL // N_DEV, D_MODEL)
    p_conv = g_conv_full.reshape(CONV_WIDTH, N_DEV, 3 * D_DN // N_DEV).transpose(1, 0, 2)
    pack_a = g_pw.reshape(4 * POOL_GROUP, POOL_GROUP)
    pack_b = _pack_rows([g_nw, g_fnw, g_ps, g_dnw, g_al[0, DN_HEADS:2 * DN_HEADS], g_db[0, DN_HEADS:2 * DN_HEADS],
                         loss_part[0, :1]])
    gr_in, gr_out, gr_conv, gath_a, gath_b = _reduce_grads(p_in, p_out, p_conv, pack_a, pack_b)

    sharded = _adamw_sharded([(wt, gr_in, m_wt, v_wt), (w_out[0], gr_out, m_w_out[0], v_w_out[0]),
                              (conv_w[0], gr_conv, m_conv_w[0], v_conv_w[0])])
    r_in, r_out, r_conv = ((g,) + dmv for g, dmv in zip((gr_in, gr_out, gr_conv), sharded))
    flat = lambda a: a.reshape(4 * POOL_GROUP, POOL_GROUP)
    row = lambda a: a.reshape(1, -1)
    vecs = {"norm_w": (norm_w, m_norm_w, v_norm_w), "final_norm_w": (final_norm_w, m_final_norm_w, v_final_norm_w),
            "pool_scale": (pool_scale, m_pool_scale, v_pool_scale), "dn_norm_w": (dn_norm_w, m_dn_norm_w, v_dn_norm_w),
            "a_log": (a_log, m_a_log, v_a_log), "dt_bias": (dt_bias, m_dt_bias, v_dt_bias)}
    res, loss = _adamw_replicated(gath_a, gath_b, (flat(pool_w), flat(m_pool_w), flat(v_pool_w)),
                                  [tuple(row(a) for a in vecs[nm]) for nm in _ROW_ORDER])
    r_pool = res[0]
    r_vec = dict(zip(_ROW_ORDER, res[1:]))

    def group(k):
        vec = lambda nm: r_vec[nm][k].reshape(vecs[nm][0].shape)
        return (vec("norm_w"), r_in[k].T[None], r_pool[k].reshape(pool_w.shape), vec("pool_scale"), r_conv[k][None],
                vec("a_log"), vec("dt_bias"), vec("dn_norm_w"), r_out[k][None], vec("final_norm_w"))

    return (loss[0, 0], grad_x2[None], *group(0), *group(1), *group(2), *group(3))
```

```python
import functools

import jax
import jax.numpy as jnp
from jax import lax
from jax.experimental import pallas as pl
from jax.experimental.pallas import tpu as pltpu

F32 = jnp.float32
BF16 = jnp.bfloat16
HI = lax.Precision.HIGHEST
MESH = pl.DeviceIdType.MESH

D_MODEL = 1024
D_POOL = 512
D_DN = 512
POOL_WINDOWS = (2, 4, 8, 16)
POOL_GROUP = 128
DN_HEADS = 4
DN_HEAD_DIM = 128
CONV_WIDTH = 4
CHUNK = 64
NORM_EPS = 1e-6
D_IN = 3080
D_MAIN = 3072
FLAT_ROWS = D_MODEL // 128
D_IN_PAD = D_MAIN + 128
N_DEV = 8
W_IN_SHARD = D_IN // N_DEV
HALO = 16
DN_CHUNKS_PER_STEP = 8

ADAM_LR = 0.001
ADAM_B1 = 0.9
ADAM_B2 = 0.999
ADAM_EPS = 1e-08
ADAM_WD = 0.01
ADAM_STEP = 10

VMEM_LIMIT = 56 * 1024 * 1024


def _cp(sem=None, vmem=VMEM_LIMIT):
    kw = {"vmem_limit_bytes": vmem}
    if sem is not None:
        kw["dimension_semantics"] = sem
    return pltpu.CompilerParams(**kw)


def _dot_bf(a, b):
    return jnp.dot(a.astype(BF16), b.astype(BF16), preferred_element_type=F32)


def _dot_nt_bf(a, b):
    return lax.dot_general(a.astype(BF16), b.astype(BF16), (((1,), (1,)), ((), ())), preferred_element_type=F32)


def _dot_tn_bf(a, b):
    return lax.dot_general(a.astype(BF16), b.astype(BF16), (((0,), (0,)), ((), ())), preferred_element_type=F32)


def _mm_raw(a, b, ca, cb, prec):
    off = a.ndim - 2
    dn = (((ca + off,), (cb + off,)), ((0,), (0,)) if off else ((), ()))
    if prec == "hi":
        return lax.dot_general(a, b, dn, precision=HI, preferred_element_type=F32)
    ah, bh = a.astype(BF16), b.astype(BF16)
    out = lax.dot_general(ah, bh, dn, preferred_element_type=F32)
    if prec == "x3":
        al = (a - ah.astype(F32)).astype(BF16)
        bl = (b - bh.astype(F32)).astype(BF16)
        out = out + lax.dot_general(ah, bl, dn, preferred_element_type=F32)
        out = out + lax.dot_general(al, bh, dn, preferred_element_type=F32)
    return out


@functools.partial(jax.custom_vjp, nondiff_argnums=(2, 3, 4, 5))
def _mm(a, b, ca, cb, prec, bprec):
    return _mm_raw(a, b, ca, cb, prec)


def _mm_fwd(a, b, ca, cb, prec, bprec):
    return _mm_raw(a, b, ca, cb, prec), (a, b)


def _mm_bwd(ca, cb, prec, bprec, res, dc):
    a, b = res
    da = _mm_raw(dc, b, 1, 1 - cb, bprec) if ca == 1 else _mm_raw(b, dc, 1 - cb, 1, bprec)
    db = _mm_raw(a, dc, 1 - ca, 0, bprec) if cb == 0 else _mm_raw(dc, a, 0, 1 - ca, bprec)
    return da, db


_mm.defvjp(_mm_fwd, _mm_bwd)


@functools.partial(jax.custom_vjp, nondiff_argnums=(1, 2))
def _tri_inv(a, prec, bprec):
    n = a.shape[-1]
    ii = lax.broadcasted_iota(jnp.int32, (n, n), 0)
    jj = lax.broadcasted_iota(jnp.int32, (n, n), 1)
    p = (ii == jj).astype(F32) - a
    b = _mm_raw(a, a, 1, 0, prec)
    for _ in range(4):
        pb = _mm_raw(jnp.concatenate([p, b], axis=-2), b, 1, 0, prec)
        p = p + pb[..., :n, :]
        b = pb[..., n:, :]
    return p + _mm_raw(p, b, 1, 0, prec)


def _tri_inv_fwd(a, prec, bprec):
    t = _tri_inv(a, prec, bprec)
    return t, t


def _tri_inv_bwd(prec, bprec, t, dt):
    return (-_mm_raw(_mm_raw(t, dt, 0, 0, bprec), t, 1, 1, bprec),)


_tri_inv.defvjp(_tri_inv_fwd, _tri_inv_bwd)

_DN_PREC = {"akq": ("bf16", "bf16"), "inv": ("bf16", "bf16"), "uw": ("bf16", "bf16"), "ws": ("bf16", "bf16"),
            "ov": ("bf16", "bf16"), "st": ("bf16", "bf16")}


def _silu(x):
    return x * jax.nn.sigmoid(x)


def _softplus(x):
    pos = x > 0.0
    return jnp.where(pos, x, 0.0) + jnp.log1p(jnp.exp(jnp.where(pos, -x, x)))


def _mesh_pos():
    return lax.axis_index("x"), lax.axis_index("y"), lax.axis_index("c")


def _dev_index(x, y, c):
    return 4 * x + 2 * y + c


def _all_gather_blocks(outs, send_sems, recv_sems):
    x, y, c = _mesh_pos()
    me = (x, y, c)
    sibling = (x, y, 1 - c)
    chips = [(1 - x, y), (x, 1 - y), (1 - x, 1 - y)]

    def copy(a, k, block, to):
        rows = outs[a].at[_dev_index(*block)]
        return pltpu.make_async_remote_copy(src_ref=rows, dst_ref=rows, send_sem=send_sems.at[a, k],
                                            recv_sem=recv_sems.at[a, k], device_id=to, device_id_type=MESH)

    n = len(outs)
    first = []
    for a in range(n):
        first.append(copy(a, 0, me, sibling))
        for j, chip in enumerate(chips):
            first.append(copy(a, 1 + j, me, (*chip, c)))
    for cp in first:
        cp.start()
    passed = []
    for j, chip in enumerate(chips):
        for a in range(n):
            copy(a, 1 + j, (*chip, c), me).wait_recv()
            fwd = copy(a, 4 + j, (*chip, c), sibling)
            fwd.start()
            passed.append(fwd)
    for a in range(n):
        copy(a, 0, sibling, me).wait_recv()
        for j, chip in enumerate(chips):
            copy(a, 4 + j, (*chip, 1 - c), me).wait_recv()
    for cp in first + passed:
        cp.wait_send()


def _gather_weights(w_in_flat, w_out_blk, conv_blk):
    def body(win_ref, wout_ref, conv_ref, gin_ref, gout_ref, gconv_ref, send_sems, recv_sems):
        x, y, c = _mesh_pos()
        me = _dev_index(x, y, c)
        for j in range(FLAT_ROWS):
            gin_ref[me, :, 128 * j:128 * (j + 1)] = win_ref[pl.ds(j, W_IN_SHARD, stride=FLAT_ROWS), :].astype(BF16)
        gout_ref[me] = wout_ref[...].astype(BF16)
        gconv_ref[me] = conv_ref[...]
        _all_gather_blocks((gin_ref, gout_ref, gconv_ref), send_sems, recv_sems)

    vm = pl.BlockSpec(memory_space=pltpu.VMEM)
    return pl.pallas_call(
        body, name="gather_weights",
        out_shape=(jax.ShapeDtypeStruct((N_DEV, W_IN_SHARD, D_MODEL), BF16),
                   jax.ShapeDtypeStruct((N_DEV,) + w_out_blk.shape, BF16),
                   jax.ShapeDtypeStruct((N_DEV,) + conv_blk.shape, F32)),
        in_specs=[vm, vm, vm], out_specs=(vm, vm, vm),
        scratch_shapes=[pltpu.SemaphoreType.DMA((3, 7)), pltpu.SemaphoreType.DMA((3, 7))],
        compiler_params=_cp(),
    )(w_in_flat, w_out_blk, conv_blk)


def _reduce_grads(p_in, p_out, p_conv, pack_a, pack_b):
    big = (p_in, p_out, p_conv)

    def body(pin_ref, pout_ref, pconv_ref, pa_ref, pb_ref,
             oin_ref, oout_ref, oconv_ref, ga_ref, gb_ref,
             r1_in, r1_out, r1_conv, r2_in, r2_out, r2_conv, sb_in, sb_out, sb_conv, st_in, st_out, st_conv,
             s1_send, s1_recv, s2_send, s2_recv, ag_send, ag_recv, st_sem):
        x, y, c = _mesh_pos()
        me = (x, y, c)
        sibling = (x, y, 1 - c)
        rel = [(x, y), (1 - x, y), (x, 1 - y), (1 - x, 1 - y)]
        srcs = (pin_ref, pout_ref, pconv_ref)
        r1s = (r1_in, r1_out, r1_conv)
        r2s = (r2_in, r2_out, r2_conv)
        sbs = (sb_in, sb_out, sb_conv)
        sts = (st_in, st_out, st_conv)
        outs = (oin_ref, oout_ref, oconv_ref)

        ga_ref[_dev_index(*me)] = pa_ref[...]
        gb_ref[_dev_index(*me)] = pb_ref[...]

        def p1(a, r, to):
            return pltpu.make_async_remote_copy(
                src_ref=srcs[a].at[_dev_index(*rel[r], 1 - c)], dst_ref=r1s[a].at[r],
                send_sem=s1_send.at[a, r], recv_sem=s1_recv.at[a, r], device_id=to, device_id_type=MESH)

        def p2(a, r, to):
            return pltpu.make_async_remote_copy(
                src_ref=sbs[a].at[r - 1], dst_ref=r2s[a].at[r - 1],
                send_sem=s2_send.at[a, r - 1], recv_sem=s2_recv.at[a, r - 1], device_id=to, device_id_type=MESH)

        def stage(a, r):
            return pltpu.make_async_copy(srcs[a].at[_dev_index(*rel[r], c)], sts[a].at[r % 2], st_sem.at[a, r % 2])

        sends1 = [p1(a, r, sibling) for a in range(3) for r in range(4)]
        for cp in sends1:
            cp.start()
        sends2 = []
        for a in range(3):
            stage(a, 1).start()
            for r in (1, 2, 3, 0):
                nxt = {1: 2, 2: 3, 3: 0, 0: None}[r]
                if nxt is not None:
                    stage(a, nxt).start()
                stage(a, r).wait()
                p1(a, r, me).wait_recv()
                chip_sum = r1s[a][r] + sts[a][r % 2]
                if r == 0:
                    r1s[a][0] = chip_sum
                else:
                    sbs[a][r - 1] = chip_sum.astype(BF16)
                    cp = p2(a, r, (*rel[r], c))
                    cp.start()
                    sends2.append(cp)
        _all_gather_blocks((ga_ref, gb_ref), ag_send, ag_recv)
        for a in range(3):
            for r in (1, 2, 3):
                p2(a, r, me).wait_recv()
            outs[a][...] = ((r1s[a][0] + r2s[a][0].astype(F32)) + r2s[a][1].astype(F32)) + r2s[a][2].astype(F32)
        for cp in sends1 + sends2:
            cp.wait_send()

    vm = pl.BlockSpec(memory_space=pltpu.VMEM)
    hbm = pl.BlockSpec(memory_space=pl.ANY)
    blk = [p.shape[1:] for p in big]
    scratch = ([pltpu.VMEM((4,) + b, F32) for b in blk] + [pltpu.VMEM((3,) + b, BF16) for b in blk]
               + [pltpu.VMEM((3,) + b, BF16) for b in blk] + [pltpu.VMEM((2,) + b, F32) for b in blk]
               + [pltpu.SemaphoreType.DMA((3, 4)), pltpu.SemaphoreType.DMA((3, 4)),
                  pltpu.SemaphoreType.DMA((3, 3)), pltpu.SemaphoreType.DMA((3, 3)),
                  pltpu.SemaphoreType.DMA((2, 7)), pltpu.SemaphoreType.DMA((2, 7)),
                  pltpu.SemaphoreType.DMA((3, 2))])
    return pl.pallas_call(
        body, name="reduce_grads",
        out_shape=tuple(jax.ShapeDtypeStruct(b, F32) for b in blk)
        + (jax.ShapeDtypeStruct((N_DEV,) + pack_a.shape, F32), jax.ShapeDtypeStruct((N_DEV,) + pack_b.shape, F32)),
        in_specs=[hbm, hbm, hbm, vm, vm], out_specs=(vm, vm, vm, vm, vm),
        scratch_shapes=scratch,
        compiler_params=_cp(),
    )(p_in, p_out, p_conv, pack_a, pack_b)


def _rms_hat(xf):
    r = lax.rsqrt(jnp.mean(xf * xf, axis=-1, keepdims=True) + NORM_EPS)
    return xf * r, r


def _in_proj(x2, norm_w, g_in, tm):
    s = x2.shape[0]

    def body(x_ref, nw_ref, g_hbm, pm_ref, pb_ref, wt_hbm, g_vmem, wt_vmem, sem):
        @pl.when(pl.program_id(0) == 0)
        def _():
            cp = pltpu.make_async_copy(g_hbm, g_vmem, sem)
            cp.start()
            wt_vmem[D_MAIN:, :] = jnp.zeros((D_IN_PAD - D_MAIN, D_MODEL), BF16)
            cp.wait()
            for d in range(N_DEV):
                wt_vmem[W_IN_SHARD * d:W_IN_SHARD * (d + 1), :] = g_vmem[d]
            out = pltpu.make_async_copy(wt_vmem, wt_hbm, sem)
            out.start()
            out.wait()
        xhat, _ = _rms_hat(x_ref[...])
        n = (xhat * nw_ref[...]).astype(BF16)
        pm_ref[...] = _dot_nt_bf(n, wt_vmem[:D_MAIN, :])
        pb_ref[...] = _dot_nt_bf(n, wt_vmem[D_MAIN:, :])

    return pl.pallas_call(
        body, name="in_proj", grid=(s // tm,),
        out_shape=(jax.ShapeDtypeStruct((s, D_MAIN), F32), jax.ShapeDtypeStruct((s, 128), F32),
                   jax.ShapeDtypeStruct((D_IN_PAD, D_MODEL), BF16)),
        in_specs=[pl.BlockSpec((tm, D_MODEL), lambda i: (i, 0)),
                  pl.BlockSpec((1, D_MODEL), lambda i: (0, 0)),
                  pl.BlockSpec(memory_space=pl.ANY)],
        out_specs=(pl.BlockSpec((tm, D_MAIN), lambda i: (i, 0)), pl.BlockSpec((tm, 128), lambda i: (i, 0)),
                   pl.BlockSpec(memory_space=pl.ANY)),
        scratch_shapes=[pltpu.VMEM((N_DEV, W_IN_SHARD, D_MODEL), BF16), pltpu.VMEM((D_IN_PAD, D_MODEL), BF16),
                        pltpu.SemaphoreType.DMA],
        compiler_params=_cp(("arbitrary",)),
    )(x2, norm_w, g_in)


def _shift_down(cur, prev_tail, s):
    ext = jnp.concatenate([prev_tail, cur], axis=0)
    return pltpu.roll(ext, s, 0)[HALO:, :]


def _shift_up(cur, next_head, s):
    ext = jnp.concatenate([cur, next_head], axis=0)
    n = ext.shape[0]
    return pltpu.roll(ext, n - s, 0)[:cur.shape[0], :]


def _pool_counts(i, tp, w):
    t = i * tp + lax.broadcasted_iota(jnp.int32, (tp, 1), 0)
    return jnp.minimum(t + 1, w).astype(F32)


def _pool_mix(u, u_prev_tail, i, tp):
    ext = jnp.concatenate([u_prev_tail, u], axis=0)
    w2 = ext + pltpu.roll(ext, 1, 0)
    w4 = w2 + pltpu.roll(w2, 2, 0)
    w8 = w4 + pltpu.roll(w4, 4, 0)
    w16 = w8 + pltpu.roll(w8, 8, 0)
    mixes = []
    for gi, (w, win) in enumerate(zip(POOL_WINDOWS, (w2, w4, w8, w16))):
        cols = slice(gi * POOL_GROUP, (gi + 1) * POOL_GROUP)
        mixes.append(win[HALO:, cols] / _pool_counts(i, tp, w) - u[:, cols])
    return mixes


def _prev_halo_spec(tp, width, col):
    per = tp // HALO
    return pl.BlockSpec((HALO, width), lambda i: (jnp.maximum(i * per - 1, 0), col))


def _next_halo_spec(tp, width, col, n):
    per = tp // HALO
    return pl.BlockSpec((HALO, width), lambda i: (jnp.minimum((i + 1) * per, n * per - 1), col))


def _prev_tail(ref, i):
    return jnp.where(i > 0, ref[...], 0.0)


def _next_head(ref, i, n):
    return jnp.where(i < n - 1, ref[...], 0.0)


def _pool_fwd(proj_main, pool_w, pool_scale, tp):
    s = proj_main.shape[0]

    def body(u_ref, up_ref, z_ref, pw_ref, ps_ref, y_ref):
        i = pl.program_id(0)
        u = u_ref[...]
        mixes = _pool_mix(u, _prev_tail(up_ref, i), i, tp)
        gate = ps_ref[...] * _silu(z_ref[...])
        for gi in range(4):
            cols = slice(gi * POOL_GROUP, (gi + 1) * POOL_GROUP)
            y_ref[:, cols] = _dot_bf(mixes[gi], pw_ref[gi]) * gate[:, cols]

    return pl.pallas_call(
        body, name="pool_fwd", grid=(s // tp,),
        out_shape=jax.ShapeDtypeStruct((s, D_POOL), F32),
        in_specs=[pl.BlockSpec((tp, D_POOL), lambda i: (i, 0)),
                  _prev_halo_spec(tp, D_POOL, 0),
                  pl.BlockSpec((tp, D_POOL), lambda i: (i, 1)),
                  pl.BlockSpec((4, POOL_GROUP, POOL_GROUP), lambda i: (0, 0, 0)),
                  pl.BlockSpec((1, D_POOL), lambda i: (0, 0))],
        out_specs=pl.BlockSpec((tp, D_POOL), lambda i: (i, 0)),
        compiler_params=_cp(("parallel",)),
    )(proj_main, proj_main, proj_main, pool_w, pool_scale)


def _conv_fwd(cur, prev_tail, w4):
    ext = jnp.concatenate([prev_tail, cur], axis=0)
    y = ext * w4[CONV_WIDTH - 1:CONV_WIDTH, :]
    for sft in range(1, CONV_WIDTH):
        y = y + pltpu.roll(ext, sft, 0) * w4[CONV_WIDTH - 1 - sft:CONV_WIDTH - sft, :]
    return y[HALO:, :]


def _l2n_heads(t):
    parts = []
    for h in range(DN_HEADS):
        th = t[:, h * DN_HEAD_DIM:(h + 1) * DN_HEAD_DIM]
        parts.append(th * lax.rsqrt(jnp.sum(th * th, axis=-1, keepdims=True) + NORM_EPS))
    return jnp.concatenate(parts, axis=1)


def _post_conv(yq, yk, yv):
    return _l2n_heads(_silu(yq)), _l2n_heads(_silu(yk)), _silu(yv)


def _gates(ba, alog_lane, dtb_lane):
    lane = lax.broadcasted_iota(jnp.int32, ba.shape, 1)
    beta = jax.nn.sigmoid(ba)
    g = -jnp.exp(alog_lane) * _softplus(ba + dtb_lane)
    return jnp.where(lane < DN_HEADS, beta, jnp.where(lane < 2 * DN_HEADS, g, 0.0))


def _front(x2, norm_w, g_in, pool_w, pool_scale, conv_full, alog_lane, dtb_lane, tm):
    s = x2.shape[0]

    def body(x_ref, nw_ref, pw_ref, ps_ref, cw_ref, al_ref, db_ref, g_hbm,
             pm_ref, pb_ref, yp_ref, qn_ref, kn_ref, vv_ref, gb_ref, wt_hbm,
             g_vmem, wt_vmem, tail_u, tail_qkv, sem):
        i = pl.program_id(0)

        @pl.when(i == 0)
        def _():
            cp = pltpu.make_async_copy(g_hbm, g_vmem, sem)
            cp.start()
            wt_vmem[D_MAIN:, :] = jnp.zeros((D_IN_PAD - D_MAIN, D_MODEL), BF16)
            tail_u[...] = jnp.zeros_like(tail_u)
            tail_qkv[...] = jnp.zeros_like(tail_qkv)
            cp.wait()
            for d in range(N_DEV):
                wt_vmem[W_IN_SHARD * d:W_IN_SHARD * (d + 1), :] = g_vmem[d]
            out = pltpu.make_async_copy(wt_vmem, wt_hbm, sem)
            out.start()
            out.wait()
        xhat, _ = _rms_hat(x_ref[...])
        n = (xhat * nw_ref[...]).astype(BF16)
        pm_ref[...] = _dot_nt_bf(n, wt_vmem[:D_MAIN, :])
        pb = _dot_nt_bf(n, wt_vmem[D_MAIN:, :])
        pb_ref[...] = pb
        u = pm_ref[:, :D_POOL]
        mixes = _pool_mix(u, tail_u[...], i, tm)
        tail_u[...] = u[tm - HALO:, :]
        gate = ps_ref[...] * _silu(pm_ref[:, D_POOL:2 * D_POOL])
        for gi in range(4):
            cols = slice(gi * POOL_GROUP, (gi + 1) * POOL_GROUP)
            yp_ref[:, cols] = _dot_bf(mixes[gi], pw_ref[gi]) * gate[:, cols]
        ys = []
        for c in range(3):
            cols = slice(c * D_DN, (c + 1) * D_DN)
            cur = pm_ref[:, 2 * D_POOL + c * D_DN:2 * D_POOL + (c + 1) * D_DN]
            ys.append(_conv_fwd(cur, tail_qkv[:, cols], cw_ref[:, cols]))
            tail_qkv[:, cols] = cur[tm - HALO:, :]
        qn, kn, vv = _post_conv(*ys)
        qn_ref[...] = qn
        kn_ref[...] = kn
        vv_ref[...] = vv
        gb_ref[...] = _gates(pb, al_ref[...], db_ref[...])

    tile = pl.BlockSpec((tm, D_DN), lambda i: (i, 0))
    lanes = pl.BlockSpec((tm, 128), lambda i: (i, 0))
    row = pl.BlockSpec((1, 128), lambda i: (0, 0))
    return pl.pallas_call(
        body, name="front", grid=(s // tm,),
        out_shape=(jax.ShapeDtypeStruct((s, D_MAIN), F32), jax.ShapeDtypeStruct((s, 128), F32),
                   jax.ShapeDtypeStruct((s, D_POOL), F32), jax.ShapeDtypeStruct((s, D_DN), F32),
                   jax.ShapeDtypeStruct((s, D_DN), F32), jax.ShapeDtypeStruct((s, D_DN), F32),
                   jax.ShapeDtypeStruct((s, 128), F32), jax.ShapeDtypeStruct((D_IN_PAD, D_MODEL), BF16)),
        in_specs=[pl.BlockSpec((tm, D_MODEL), lambda i: (i, 0)),
                  pl.BlockSpec((1, D_MODEL), lambda i: (0, 0)),
                  pl.BlockSpec((4, POOL_GROUP, POOL_GROUP), lambda i: (0, 0, 0)),
                  pl.BlockSpec((1, D_POOL), lambda i: (0, 0)),
                  pl.BlockSpec((CONV_WIDTH, 3 * D_DN), lambda i: (0, 0)), row, row,
                  pl.BlockSpec(memory_space=pl.ANY)],
        out_specs=(pl.BlockSpec((tm, D_MAIN), lambda i: (i, 0)), lanes, tile, tile, tile, tile, lanes,
                   pl.BlockSpec(memory_space=pl.ANY)),
        scratch_shapes=[pltpu.VMEM((N_DEV, W_IN_SHARD, D_MODEL), BF16), pltpu.VMEM((D_IN_PAD, D_MODEL), BF16),
                        pltpu.VMEM((HALO, D_POOL), F32), pltpu.VMEM((HALO, 3 * D_DN), F32),
                        pltpu.SemaphoreType.DMA],
        compiler_params=_cp(("arbitrary",)),
    )(x2, norm_w, pool_w, pool_scale, conv_full, alog_lane, dtb_lane, g_in)


def _qkv_specs(tp, which):
    def spec(col, n=None):
        if which == 0:
            return pl.BlockSpec((tp, D_DN), lambda i: (i, col))
        if which < 0:
            return _prev_halo_spec(tp, D_DN, col)
        return _next_halo_spec(tp, D_DN, col, n)
    return spec


def _dn_pre(proj_main, proj_ba, conv_full, alog_lane, dtb_lane, tp):
    s = proj_main.shape[0]

    def body(q_ref, k_ref, v_ref, qp_ref, kp_ref, vp_ref, cw_ref, ba_ref, al_ref, db_ref,
             qn_ref, kn_ref, vv_ref, gb_ref):
        i = pl.program_id(0)
        ys = []
        for j, (cur, prev) in enumerate(((q_ref, qp_ref), (k_ref, kp_ref), (v_ref, vp_ref))):
            ys.append(_conv_fwd(cur[...], _prev_tail(prev, i), cw_ref[:, j * D_DN:(j + 1) * D_DN]))
        qn, kn, vv = _post_conv(*ys)
        qn_ref[...] = qn
        kn_ref[...] = kn
        vv_ref[...] = vv
        gb_ref[...] = _gates(ba_ref[...], al_ref[...], db_ref[...])

    cur, prev = _qkv_specs(tp, 0), _qkv_specs(tp, -1)
    row = pl.BlockSpec((1, 128), lambda i: (0, 0))
    tile = pl.BlockSpec((tp, D_DN), lambda i: (i, 0))
    return pl.pallas_call(
        body, name="dn_pre", grid=(s // tp,),
        out_shape=(jax.ShapeDtypeStruct((s, D_DN), F32),) * 3 + (jax.ShapeDtypeStruct((s, 128), F32),),
        in_specs=[cur(2), cur(3), cur(4), prev(2), prev(3), prev(4),
                  pl.BlockSpec((CONV_WIDTH, 3 * D_DN), lambda i: (0, 0)),
                  pl.BlockSpec((tp, 128), lambda i: (i, 0)), row, row],
        out_specs=(tile, tile, tile, pl.BlockSpec((tp, 128), lambda i: (i, 0))),
        compiler_params=_cp(("parallel",)),
    )(proj_main, proj_main, proj_main, proj_main, proj_main, proj_main, conv_full, proj_ba, alog_lane, dtb_lane)


def _dn_block(q, k, v, gcol, bcol, state, dz, nw):
    nb, n, d = q.shape
    ii = lax.broadcasted_iota(jnp.int32, (n, n), 0)
    jj = lax.broadcasted_iota(jnp.int32, (n, n), 1)
    lower = ii >= jj
    eye = (ii == jj).astype(F32)
    g_row = jnp.sum(eye * gcol, axis=1, keepdims=True)
    gc_col = jnp.sum(jnp.where(lower, g_row, 0.0), axis=2, keepdims=True)
    gc_row = jnp.sum(eye * gc_col, axis=1, keepdims=True)
    decay = jnp.where(lower, jnp.exp(jnp.where(lower, gc_col - gc_row, 0.0)), 0.0)
    kb = k * bcol
    vb = v * bcol
    qs = q * (DN_HEAD_DIM ** -0.5)
    egc = jnp.exp(gc_col)
    akq = _mm(jnp.concatenate([kb, qs], axis=1), k, 1, 1, *_DN_PREC["akq"])
    a = jnp.where(ii > jj, akq[:, :n] * decay, 0.0)
    qk = akq[:, n:] * decay
    t = _tri_inv(a, *_DN_PREC["inv"])
    uw = _mm(t, jnp.concatenate([vb, kb * egc], axis=2), 1, 0, *_DN_PREC["uw"])
    wq = jnp.concatenate([uw[:, :, d:], qs * egc], axis=1)
    g_last = gc_col[:, n - 1:n, :]
    k_dec = k * jnp.exp(g_last - gc_col)
    e_last = jnp.exp(g_last)
    os_ = []
    for c in range(nb // DN_HEADS):
        sl = slice(c * DN_HEADS, (c + 1) * DN_HEADS)
        ws = _mm(wq[sl], state, 1, 0, *_DN_PREC["ws"])
        v_new = uw[sl, :, :d] - ws[:, :n]
        os_.append(ws[:, n:] + _mm(qk[sl], v_new, 1, 0, *_DN_PREC["ov"]))
        state = state * e_last[sl] + _mm(k_dec[sl], v_new, 0, 0, *_DN_PREC["st"])
    o = jnp.concatenate(os_, axis=0)
    y = o * lax.rsqrt(jnp.mean(o * o, axis=-1, keepdims=True) + NORM_EPS) * nw * _silu(dz)
    return y, state


def _dn_block_args(gc, q_ref, k_ref, v_ref, gb_ref, dz_ref):
    qs, ks, vs, gs, bs, zs = [], [], [], [], [], []
    for cc in range(gc):
        r = slice(cc * CHUNK, (cc + 1) * CHUNK)
        gbv = gb_ref[r, :]
        for h in range(DN_HEADS):
            cols = slice(h * DN_HEAD_DIM, (h + 1) * DN_HEAD_DIM)
            qs.append(q_ref[r, cols])
            ks.append(k_ref[r, cols])
            vs.append(v_ref[r, cols])
            zs.append(dz_ref[r, cols])
            gs.append(gbv[:, DN_HEADS + h:DN_HEADS + h + 1])
            bs.append(gbv[:, h:h + 1])
    return tuple(jnp.stack(t, axis=0) for t in (qs, ks, vs, gs, bs, zs))


def _dn_scan_fwd(qn, kn, vv, gb, proj_main, dn_norm_w, gc):
    s = qn.shape[0]
    nchunk = s // CHUNK
    rows = gc * CHUNK

    def body(q_ref, k_ref, v_ref, gb_ref, dz_ref, nw_ref, y_ref, ss_ref, state):
        @pl.when(pl.program_id(0) == 0)
        def _():
            state[...] = jnp.zeros_like(state)
        q, k, v, gcol, bcol, dz = _dn_block_args(gc, q_ref, k_ref, v_ref, gb_ref, dz_ref)
        st = state[...]
        ss_ref[0] = st
        y, new = _dn_block(q, k, v, gcol, bcol, st, dz, nw_ref[...])
        state[...] = new
        for cc in range(gc):
            for h in range(DN_HEADS):
                y_ref[cc * CHUNK:(cc + 1) * CHUNK, h * DN_HEAD_DIM:(h + 1) * DN_HEAD_DIM] = y[cc * DN_HEADS + h]

    tile = pl.BlockSpec((rows, D_DN), lambda i: (i, 0))
    return pl.pallas_call(
        body, name="dn_scan_fwd", grid=(nchunk // gc,),
        out_shape=(jax.ShapeDtypeStruct((s, D_DN), F32),
                   jax.ShapeDtypeStruct((nchunk // gc, DN_HEADS, DN_HEAD_DIM, DN_HEAD_DIM), F32)),
        in_specs=[tile, tile, tile, pl.BlockSpec((rows, 128), lambda i: (i, 0)),
                  pl.BlockSpec((rows, D_DN), lambda i: (i, 5)), pl.BlockSpec((1, 128), lambda i: (0, 0))],
        out_specs=(tile, pl.BlockSpec((1, DN_HEADS, DN_HEAD_DIM, DN_HEAD_DIM), lambda i: (i, 0, 0, 0))),
        scratch_shapes=[pltpu.VMEM((DN_HEADS, DN_HEAD_DIM, DN_HEAD_DIM), F32)],
        compiler_params=_cp(("arbitrary",)),
    )(qn, kn, vv, gb, proj_main, dn_norm_w)


def _out_proj_loss(y_pool, y_dn, x2, tgt, w_out_full, fnw, tm):
    s = x2.shape[0]

    def body(yp_ref, yd_ref, x_ref, t_ref, wo_ref, fw_ref,
             dh_ref, dyp_ref, dyd_ref, gwo_ref, gfw_ref, loss_ref):
        @pl.when(pl.program_id(0) == 0)
        def _():
            gwo_ref[...] = jnp.zeros_like(gwo_ref)
            gfw_ref[...] = jnp.zeros_like(gfw_ref)
            loss_ref[...] = jnp.zeros_like(loss_ref)
        y = jnp.concatenate([yp_ref[...], yd_ref[...]], axis=1).astype(BF16)
        wo = wo_ref[...]
        h = x_ref[...] + jnp.dot(y, wo, preferred_element_type=F32)
        hn, r = _rms_hat(h)
        fw = fw_ref[...]
        err = hn * fw - t_ref[...]
        loss_ref[...] += 0.5 * jnp.sum(jnp.sum(err * err, axis=-1, keepdims=True) / D_MODEL, axis=0, keepdims=True)
        dout = err / D_MODEL
        gfw_ref[...] += jnp.sum(dout * hn, axis=0, keepdims=True)
        dhn = dout * fw
        dh = r * (dhn - hn * jnp.mean(dhn * hn, axis=-1, keepdims=True))
        dh_ref[...] = dh
        dhb = dh.astype(BF16)
        dy = _dot_nt_bf(dhb, wo)
        dyp_ref[...] = dy[:, :D_POOL]
        dyd_ref[...] = dy[:, D_POOL:]
        gwo_ref[...] += _dot_tn_bf(y, dhb)

    half = pl.BlockSpec((tm, D_POOL), lambda i: (i, 0))
    full = pl.BlockSpec((tm, D_MODEL), lambda i: (i, 0))
    return pl.pallas_call(
        body, name="out_proj_loss", grid=(s // tm,),
        out_shape=(jax.ShapeDtypeStruct((s, D_MODEL), F32), jax.ShapeDtypeStruct((s, D_POOL), F32),
                   jax.ShapeDtypeStruct((s, D_DN), F32), jax.ShapeDtypeStruct((D_MODEL, D_MODEL), F32),
                   jax.ShapeDtypeStruct((1, D_MODEL), F32), jax.ShapeDtypeStruct((1, 128), F32)),
        in_specs=[half, half, full, full, pl.BlockSpec((D_MODEL, D_MODEL), lambda i: (0, 0)),
                  pl.BlockSpec((1, D_MODEL), lambda i: (0, 0))],
        out_specs=(full, half, half, pl.BlockSpec((D_MODEL, D_MODEL), lambda i: (0, 0)),
                   pl.BlockSpec((1, D_MODEL), lambda i: (0, 0)), pl.BlockSpec((1, 128), lambda i: (0, 0))),
        compiler_params=_cp(("arbitrary",)),
    )(y_pool, y_dn, x2, tgt, w_out_full, fnw)


def _dn_scan_bwd(qn, kn, vv, gb, proj_main, dn_norm_w, states, dy_dn, gc):
    s = qn.shape[0]
    nchunk = s // CHUNK
    nstep = nchunk // gc
    rows = gc * CHUNK

    def body(q_ref, k_ref, v_ref, gb_ref, dz_ref, nw_ref, ss_ref, dy_ref,
             dq_ref, dk_ref, dv_ref, dgb_ref, ddz_ref, dnw_ref, dstate):
        @pl.when(pl.program_id(0) == 0)
        def _():
            dstate[...] = jnp.zeros_like(dstate)
            dnw_ref[...] = jnp.zeros_like(dnw_ref)
        lane = lax.broadcasted_iota(jnp.int32, (CHUNK, 128), 1)
        q, k, v, gcol, bcol, dz = _dn_block_args(gc, q_ref, k_ref, v_ref, gb_ref, dz_ref)
        dy = jnp.stack([dy_ref[cc * CHUNK:(cc + 1) * CHUNK, h * DN_HEAD_DIM:(h + 1) * DN_HEAD_DIM]
                        for cc in range(gc) for h in range(DN_HEADS)], axis=0)
        _, vjp = jax.vjp(_dn_block, q, k, v, gcol, bcol, ss_ref[0], dz, nw_ref[...])
        dq, dk, dv, dg, db, dst, ddz, dnw = vjp((dy, dstate[...]))
        dstate[...] = dst
        dnw_ref[...] += dnw
        for cc in range(gc):
            r = slice(cc * CHUNK, (cc + 1) * CHUNK)
            dgb = jnp.zeros((CHUNK, 128), F32)
            for h in range(DN_HEADS):
                b = cc * DN_HEADS + h
                cols = slice(h * DN_HEAD_DIM, (h + 1) * DN_HEAD_DIM)
                for ref, val in zip((dq_ref, dk_ref, dv_ref, ddz_ref), (dq, dk, dv, ddz)):
                    ref[r, cols] = val[b]
                dgb = dgb + jnp.where(lane == h, db[b], 0.0) + jnp.where(lane == DN_HEADS + h, dg[b], 0.0)
            dgb_ref[r, :] = dgb

    rev = lambda i: (nstep - 1 - i, 0)
    tile = pl.BlockSpec((rows, D_DN), rev)
    lanes = pl.BlockSpec((rows, 128), rev)
    return pl.pallas_call(
        body, name="dn_scan_bwd", grid=(nstep,),
        out_shape=(jax.ShapeDtypeStruct((s, D_DN), F32),) * 3
        + (jax.ShapeDtypeStruct((s, 128), F32), jax.ShapeDtypeStruct((s, D_DN), F32),
           jax.ShapeDtypeStruct((1, 128), F32)),
        in_specs=[tile, tile, tile, lanes, pl.BlockSpec((rows, D_DN), lambda i: (nstep - 1 - i, 5)),
                  pl.BlockSpec((1, 128), lambda i: (0, 0)),
                  pl.BlockSpec((1, DN_HEADS, DN_HEAD_DIM, DN_HEAD_DIM), lambda i: (nstep - 1 - i, 0, 0, 0)), tile],
        out_specs=(tile, tile, tile, lanes, tile, pl.BlockSpec((1, 128), lambda i: (0, 0))),
        scratch_shapes=[pltpu.VMEM((DN_HEADS, DN_HEAD_DIM, DN_HEAD_DIM), F32)],
        compiler_params=_cp(("arbitrary",)),
    )(qn, kn, vv, gb, proj_main, dn_norm_w, states, dy_dn)


def _dn_pre_bwd1(proj_main, proj_ba, conv_full, alog_lane, dtb_lane, dqn, dkn, dvv, dgb, tp):
    s = proj_main.shape[0]

    def body(q_ref, k_ref, v_ref, qp_ref, kp_ref, vp_ref, cw_ref, ba_ref, al_ref, db_ref,
             dqn_ref, dkn_ref, dvv_ref, dgb_ref,
             dcq_ref, dck_ref, dcv_ref, dba_ref, dcw_ref, dal_ref, ddb_ref):
        i = pl.program_id(0)

        @pl.when(i == 0)
        def _():
            dcw_ref[...] = jnp.zeros_like(dcw_ref)
            dal_ref[...] = jnp.zeros_like(dal_ref)
            ddb_ref[...] = jnp.zeros_like(ddb_ref)
        curs = (q_ref[...], k_ref[...], v_ref[...])
        tails = (_prev_tail(qp_ref, i), _prev_tail(kp_ref, i), _prev_tail(vp_ref, i))
        ys = [_conv_fwd(curs[j], tails[j], cw_ref[:, j * D_DN:(j + 1) * D_DN]) for j in range(3)]
        _, vjp = jax.vjp(_post_conv, *ys)
        dys = vjp((dqn_ref[...], dkn_ref[...], dvv_ref[...]))
        for j, (dy, out) in enumerate(zip(dys, (dcq_ref, dck_ref, dcv_ref))):
            out[...] = dy
            for sft in range(CONV_WIDTH):
                xs = curs[j] if sft == 0 else _shift_down(curs[j], tails[j], sft)
                row = CONV_WIDTH - 1 - sft
                dcw_ref[row:row + 1, j * D_DN:(j + 1) * D_DN] += jnp.sum(dy * xs, axis=0, keepdims=True)
        _, gvjp = jax.vjp(_gates, ba_ref[...], al_ref[...], db_ref[...])
        dba, dal, ddb = gvjp(dgb_ref[...])
        dba_ref[...] = dba
        dal_ref[...] += dal
        ddb_ref[...] += ddb

    cur, prev = _qkv_specs(tp, 0), _qkv_specs(tp, -1)
    row = pl.BlockSpec((1, 128), lambda i: (0, 0))
    tile = pl.BlockSpec((tp, D_DN), lambda i: (i, 0))
    lanes = pl.BlockSpec((tp, 128), lambda i: (i, 0))
    cw = pl.BlockSpec((CONV_WIDTH, 3 * D_DN), lambda i: (0, 0))
    return pl.pallas_call(
        body, name="dn_pre_bwd1", grid=(s // tp,),
        out_shape=(jax.ShapeDtypeStruct((s, D_DN), F32),) * 3
        + (jax.ShapeDtypeStruct((s, 128), F32), jax.ShapeDtypeStruct((CONV_WIDTH, 3 * D_DN), F32),
           jax.ShapeDtypeStruct((1, 128), F32), jax.ShapeDtypeStruct((1, 128), F32)),
        in_specs=[cur(2), cur(3), cur(4), prev(2), prev(3), prev(4), cw, lanes, row, row, tile, tile, tile, lanes],
        out_specs=(tile, tile, tile, lanes, cw, row, row),
        compiler_params=_cp(("arbitrary",)),
    )(proj_main, proj_main, proj_main, proj_main, proj_main, proj_main, conv_full, proj_ba, alog_lane, dtb_lane,
      dqn, dkn, dvv, dgb)


def _pool_bwd1(proj_main, pool_w, pool_scale, dyp, tp):
    s = proj_main.shape[0]

    def body(u_ref, up_ref, z_ref, pw_ref, ps_ref, dy_ref, dz_ref, dwin_ref, dpw_ref, dps_ref):
        i = pl.program_id(0)

        @pl.when(i == 0)
        def _():
            dpw_ref[...] = jnp.zeros_like(dpw_ref)
            dps_ref[...] = jnp.zeros_like(dps_ref)
        u = u_ref[...]
        z = z_ref[...]
        dy = dy_ref[...]
        ps = ps_ref[...]
        mixes = _pool_mix(u, _prev_tail(up_ref, i), i, tp)
        sg = jax.nn.sigmoid(z)
        sz = z * sg
        dsz = sg * (1.0 + z * (1.0 - sg))
        for gi, w in enumerate(POOL_WINDOWS):
            cols = slice(gi * POOL_GROUP, (gi + 1) * POOL_GROUP)
            mixw = _dot_bf(mixes[gi], pw_ref[gi])
            dmixw = dy[:, cols] * ps[:, cols] * sz[:, cols]
            dps_ref[:, cols] += jnp.sum(dy[:, cols] * mixw * sz[:, cols], axis=0, keepdims=True)
            dz_ref[:, cols] = dy[:, cols] * mixw * ps[:, cols] * dsz[:, cols]
            dpw_ref[gi] += _dot_tn_bf(mixes[gi], dmixw)
            dmix = _dot_nt_bf(dmixw, pw_ref[gi])
            dwin_ref[:, cols] = dmix / _pool_counts(i, tp, w)

    tile = pl.BlockSpec((tp, D_POOL), lambda i: (i, 0))
    pw = pl.BlockSpec((4, POOL_GROUP, POOL_GROUP), lambda i: (0, 0, 0))
    ps = pl.BlockSpec((1, D_POOL), lambda i: (0, 0))
    return pl.pallas_call(
        body, name="pool_bwd1", grid=(s // tp,),
        out_shape=(jax.ShapeDtypeStruct((s, D_POOL), F32), jax.ShapeDtypeStruct((s, D_POOL), F32),
                   jax.ShapeDtypeStruct((4, POOL_GROUP, POOL_GROUP), F32), jax.ShapeDtypeStruct((1, D_POOL), F32)),
        in_specs=[tile, _prev_halo_spec(tp, D_POOL, 0),
                  pl.BlockSpec((tp, D_POOL), lambda i: (i, 1)), pw, ps, tile],
        out_specs=(tile, tile, pw, ps),
        compiler_params=_cp(("arbitrary",)),
    )(proj_main, proj_main, proj_main, pool_w, pool_scale, dyp)


def _back(proj_main, proj_ba, dyp, dqn, dkn, dvv, dgb, ddz, x2, dh, norm_w, pool_w, pool_scale, conv_full,
          alog_lane, dtb_lane, wt_full, tm):
    s = x2.shape[0]
    nstep = s // tm
    per = tm // HALO

    def body(u_ref, z_ref, q_ref, k_ref, v_ref, up_ref, qp_ref, kp_ref, vp_ref, ba_ref,
             dyp_ref, dqn_ref, dkn_ref, dvv_ref, dgb_ref, ddz_ref, x_ref, dh_ref,
             nw_ref, pw_ref, ps_ref, cw_ref, al_ref, db_ref, wt_hbm,
             gx_ref, p_hbm, gnw_ref, dpw_ref, dps_ref, dcw_ref, dal_ref, ddb_ref,
             wt_vmem, acc, blk, head_dc, head_dw, sem, osem):
        j = pl.program_id(0)
        i = nstep - 1 - j

        @pl.when(j == 0)
        def _():
            cp = pltpu.make_async_copy(wt_hbm, wt_vmem, sem)
            cp.start()
            acc[...] = jnp.zeros_like(acc)
            for ref in (gnw_ref, dpw_ref, dps_ref, dcw_ref, dal_ref, ddb_ref, head_dc, head_dw):
                ref[...] = jnp.zeros_like(ref)
            cp.wait()

        u = u_ref[...]
        z = z_ref[...]
        dy = dyp_ref[...]
        ps = ps_ref[...]
        mixes = _pool_mix(u, _prev_tail(up_ref, i), i, tm)
        sg = jax.nn.sigmoid(z)
        sz = z * sg
        dsz = sg * (1.0 + z * (1.0 - sg))
        dzs, dwins = [], []
        for gi, w in enumerate(POOL_WINDOWS):
            cols = slice(gi * POOL_GROUP, (gi + 1) * POOL_GROUP)
            mixw = _dot_bf(mixes[gi], pw_ref[gi])
            dmixw = dy[:, cols] * ps[:, cols] * sz[:, cols]
            dps_ref[:, cols] += jnp.sum(dy[:, cols] * mixw * sz[:, cols], axis=0, keepdims=True)
            dzs.append(dy[:, cols] * mixw * ps[:, cols] * dsz[:, cols])
            dpw_ref[gi] += _dot_tn_bf(mixes[gi], dmixw)
            dwins.append(_dot_nt_bf(dmixw, pw_ref[gi]) / _pool_counts(i, tm, w))
        dzp = jnp.concatenate(dzs, axis=1)
        dw = jnp.concatenate(dwins, axis=1)
        ext = jnp.concatenate([dw, head_dw[...]], axis=0)
        m = ext.shape[0]
        a2 = ext + pltpu.roll(ext, m - 1, 0)
        a4 = a2 + pltpu.roll(a2, m - 2, 0)
        a8 = a4 + pltpu.roll(a4, m - 4, 0)
        a16 = a8 + pltpu.roll(a8, m - 8, 0)
        dup = jnp.concatenate(
            [acc_w[:tm, gi * POOL_GROUP:(gi + 1) * POOL_GROUP]
             - dw[:, gi * POOL_GROUP:(gi + 1) * POOL_GROUP] * _pool_counts(i, tm, w)
             for gi, (w, acc_w) in enumerate(zip(POOL_WINDOWS, (a2, a4, a8, a16)))], axis=1)
        head_dw[...] = dw[:HALO, :]

        curs = (q_ref[...], k_ref[...], v_ref[...])
        tails = (_prev_tail(qp_ref, i), _prev_tail(kp_ref, i), _prev_tail(vp_ref, i))
        ys = [_conv_fwd(curs[c], tails[c], cw_ref[:, c * D_DN:(c + 1) * D_DN]) for c in range(3)]
        _, vjp = jax.vjp(_post_conv, *ys)
        dys = vjp((dqn_ref[...], dkn_ref[...], dvv_ref[...]))
        dxs = []
        for c, dyc in enumerate(dys):
            cols = slice(c * D_DN, (c + 1) * D_DN)
            w4 = cw_ref[:, cols]
            for sft in range(CONV_WIDTH):
                xs = curs[c] if sft == 0 else _shift_down(curs[c], tails[c], sft)
                row = CONV_WIDTH - 1 - sft
                dcw_ref[row:row + 1, cols] += jnp.sum(dyc * xs, axis=0, keepdims=True)
            head = head_dc[:, cols]
            dx = dyc * w4[CONV_WIDTH - 1:CONV_WIDTH, :]
            for sft in range(1, CONV_WIDTH):
                dx = dx + _shift_up(dyc, head, sft) * w4[CONV_WIDTH - 1 - sft:CONV_WIDTH - sft, :]
            dxs.append(dx)
            head_dc[:, cols] = dyc[:HALO, :]
        _, gvjp = jax.vjp(_gates, ba_ref[...], al_ref[...], db_ref[...])
        dba, dal, ddb = gvjp(dgb_ref[...])
        dal_ref[...] += dal
        ddb_ref[...] += ddb

        dbab = dba.astype(BF16)
        xhat, r = _rms_hat(x_ref[...])
        nw = nw_ref[...]
        n = (xhat * nw).astype(BF16)
        acc[D_MAIN:, :] += _dot_tn_bf(dbab, n)
        dn = jnp.dot(dbab, wt_vmem[D_MAIN:, :], preferred_element_type=F32)
        for cb, d in enumerate((dup, dzp, dxs[0], dxs[1], dxs[2], ddz_ref[...])):
            rows = slice(cb * D_POOL, (cb + 1) * D_POOL)
            dpart = d.astype(BF16)
            acc[rows, :] += _dot_tn_bf(dpart, n)
            dn = dn + jnp.dot(dpart, wt_vmem[rows, :], preferred_element_type=F32)
        gnw_ref[...] += jnp.sum(dn * xhat, axis=0, keepdims=True)
        dxh = dn * nw
        gx_ref[...] = dh_ref[...] + r * (dxh - xhat * jnp.mean(dxh * xhat, axis=-1, keepdims=True))

        @pl.when(j == nstep - 1)
        def _():
            def out(d):
                return pltpu.make_async_copy(blk.at[d % 2], p_hbm.at[d], osem.at[d % 2])
            for d in range(N_DEV):
                if d >= 2:
                    out(d - 2).wait()
                blk[d % 2] = acc[W_IN_SHARD * d:W_IN_SHARD * (d + 1), :]
                out(d).start()
            out(N_DEV - 2).wait()
            out(N_DEV - 1).wait()

    def col(c):
        return pl.BlockSpec((tm, D_POOL), lambda j: (nstep - 1 - j, c))

    def halo(c):
        return pl.BlockSpec((HALO, D_POOL), lambda j: (jnp.maximum((nstep - 1 - j) * per - 1, 0), c))

    rev = lambda j: (nstep - 1 - j, 0)
    part = pl.BlockSpec((tm, D_POOL), rev)
    lanes = pl.BlockSpec((tm, 128), rev)
    full = pl.BlockSpec((tm, D_MODEL), rev)
    row = pl.BlockSpec((1, D_MODEL), lambda j: (0, 0))
    lrow = pl.BlockSpec((1, 128), lambda j: (0, 0))
    pw = pl.BlockSpec((4, POOL_GROUP, POOL_GROUP), lambda j: (0, 0, 0))
    psp = pl.BlockSpec((1, D_POOL), lambda j: (0, 0))
    cw = pl.BlockSpec((CONV_WIDTH, 3 * D_DN), lambda j: (0, 0))
    return pl.pallas_call(
        body, name="back", grid=(nstep,),
        out_shape=(jax.ShapeDtypeStruct((s, D_MODEL), F32),
                   jax.ShapeDtypeStruct((N_DEV, W_IN_SHARD, D_MODEL), F32), jax.ShapeDtypeStruct((1, D_MODEL), F32),
                   jax.ShapeDtypeStruct((4, POOL_GROUP, POOL_GROUP), F32), jax.ShapeDtypeStruct((1, D_POOL), F32),
                   jax.ShapeDtypeStruct((CONV_WIDTH, 3 * D_DN), F32),
                   jax.ShapeDtypeStruct((1, 128), F32), jax.ShapeDtypeStruct((1, 128), F32)),
        in_specs=[col(0), col(1), col(2), col(3), col(4), halo(0), halo(2), halo(3), halo(4), lanes,
                  part, part, part, part, lanes, part, full, full,
                  row, pw, psp, cw, lrow, lrow, pl.BlockSpec(memory_space=pl.ANY)],
        out_specs=(full, pl.BlockSpec(memory_space=pl.ANY), row, pw, psp, cw, lrow, lrow),
        scratch_shapes=[pltpu.VMEM((D_IN_PAD, D_MODEL), BF16), pltpu.VMEM((D_IN_PAD, D_MODEL), F32),
                        pltpu.VMEM((2, W_IN_SHARD, D_MODEL), F32),
                        pltpu.VMEM((HALO, 3 * D_DN), F32), pltpu.VMEM((HALO, D_POOL), F32),
                        pltpu.SemaphoreType.DMA, pltpu.SemaphoreType.DMA((2,))],
        compiler_params=_cp(("arbitrary",)),
    )(proj_main, proj_main, proj_main, proj_main, proj_main, proj_main, proj_main, proj_main, proj_main, proj_ba,
      dyp, dqn, dkn, dvv, dgb, ddz, x2, dh, norm_w, pool_w, pool_scale, conv_full, alog_lane, dtb_lane, wt_full)


def _in_proj_bwd(dwin, dzp, dcq, dck, dcv, ddz, dba, x2, dh, norm_w, wt_full, conv_full, tm):
    s = x2.shape[0]
    nstep = s // tm

    def body(dwin_ref, dzp_ref, dcq_ref, dck_ref, dcv_ref, ddz_ref, dba_ref, x_ref, dh_ref, nw_ref, cw_ref, wt_hbm,
             gx_ref, p_hbm, gnw_ref, wt_vmem, acc, blk, head_dc, head_dw, sem, osem):
        j = pl.program_id(0)
        i = nstep - 1 - j

        @pl.when(j == 0)
        def _():
            cp = pltpu.make_async_copy(wt_hbm, wt_vmem, sem)
            cp.start()
            acc[...] = jnp.zeros_like(acc)
            gnw_ref[...] = jnp.zeros_like(gnw_ref)
            head_dc[...] = jnp.zeros_like(head_dc)
            head_dw[...] = jnp.zeros_like(head_dw)
            cp.wait()
        dw = dwin_ref[...]
        ext = jnp.concatenate([dw, head_dw[...]], axis=0)
        m = ext.shape[0]
        a2 = ext + pltpu.roll(ext, m - 1, 0)
        a4 = a2 + pltpu.roll(a2, m - 2, 0)
        a8 = a4 + pltpu.roll(a4, m - 4, 0)
        a16 = a8 + pltpu.roll(a8, m - 8, 0)
        dup = jnp.concatenate(
            [acc_w[:tm, gi * POOL_GROUP:(gi + 1) * POOL_GROUP]
             - dw[:, gi * POOL_GROUP:(gi + 1) * POOL_GROUP] * _pool_counts(i, tm, w)
             for gi, (w, acc_w) in enumerate(zip(POOL_WINDOWS, (a2, a4, a8, a16)))], axis=1)
        head_dw[...] = dw[:HALO, :]
        dxs = []
        for c, ref in enumerate((dcq_ref, dck_ref, dcv_ref)):
            cols = slice(c * D_DN, (c + 1) * D_DN)
            w4 = cw_ref[:, cols]
            dy = ref[...]
            head = head_dc[:, cols]
            dx = dy * w4[CONV_WIDTH - 1:CONV_WIDTH, :]
            for sft in range(1, CONV_WIDTH):
                dx = dx + _shift_up(dy, head, sft) * w4[CONV_WIDTH - 1 - sft:CONV_WIDTH - sft, :]
            dxs.append(dx)
            head_dc[:, cols] = dy[:HALO, :]
        dbab = dba_ref[...].astype(BF16)
        xhat, r = _rms_hat(x_ref[...])
        nw = nw_ref[...]
        n = (xhat * nw).astype(BF16)
        acc[D_MAIN:, :] += _dot_tn_bf(dbab, n)
        dn = jnp.dot(dbab, wt_vmem[D_MAIN:, :], preferred_element_type=F32)
        for cb, d in enumerate((dup, dzp_ref[...], dxs[0], dxs[1], dxs[2], ddz_ref[...])):
            rows = slice(cb * D_POOL, (cb + 1) * D_POOL)
            dpart = d.astype(BF16)
            acc[rows, :] += _dot_tn_bf(dpart, n)
            dn = dn + jnp.dot(dpart, wt_vmem[rows, :], preferred_element_type=F32)
        gnw_ref[...] += jnp.sum(dn * xhat, axis=0, keepdims=True)
        dxh = dn * nw
        gx_ref[...] = dh_ref[...] + r * (dxh - xhat * jnp.mean(dxh * xhat, axis=-1, keepdims=True))

        @pl.when(j == nstep - 1)
        def _():
            def out(d):
                return pltpu.make_async_copy(blk.at[d % 2], p_hbm.at[d], osem.at[d % 2])
            for d in range(N_DEV):
                if d >= 2:
                    out(d - 2).wait()
                blk[d % 2] = acc[W_IN_SHARD * d:W_IN_SHARD * (d + 1), :]
                out(d).start()
            out(N_DEV - 2).wait()
            out(N_DEV - 1).wait()

    rev = lambda j: (nstep - 1 - j, 0)
    part = pl.BlockSpec((tm, D_POOL), rev)
    full = pl.BlockSpec((tm, D_MODEL), rev)
    row = pl.BlockSpec((1, D_MODEL), lambda j: (0, 0))
    return pl.pallas_call(
        body, name="in_proj_bwd", grid=(nstep,),
        out_shape=(jax.ShapeDtypeStruct((s, D_MODEL), F32),
                   jax.ShapeDtypeStruct((N_DEV, W_IN_SHARD, D_MODEL), F32), jax.ShapeDtypeStruct((1, D_MODEL), F32)),
        in_specs=[part] * 6 + [pl.BlockSpec((tm, 128), rev), full, full, row,
                               pl.BlockSpec((CONV_WIDTH, 3 * D_DN), lambda j: (0, 0)),
                               pl.BlockSpec(memory_space=pl.ANY)],
        out_specs=(full, pl.BlockSpec(memory_space=pl.ANY), row),
        scratch_shapes=[pltpu.VMEM((D_IN_PAD, D_MODEL), BF16), pltpu.VMEM((D_IN_PAD, D_MODEL), F32),
                        pltpu.VMEM((2, W_IN_SHARD, D_MODEL), F32),
                        pltpu.VMEM((HALO, 3 * D_DN), F32), pltpu.VMEM((HALO, D_POOL), F32),
                        pltpu.SemaphoreType.DMA, pltpu.SemaphoreType.DMA((2,))],
        compiler_params=_cp(("arbitrary",)),
    )(dwin, dzp, dcq, dck, dcv, ddz, dba, x2, dh, norm_w, conv_full, wt_full)


def _adamw_math(w, g, m, v):
    m = ADAM_B1 * m + (1.0 - ADAM_B1) * g
    v = ADAM_B2 * v + (1.0 - ADAM_B2) * (g * g)
    m_hat = m / (1.0 - ADAM_B1 ** ADAM_STEP)
    v_hat = v / (1.0 - ADAM_B2 ** ADAM_STEP)
    delta = -ADAM_LR * (m_hat / (jnp.sqrt(v_hat) + ADAM_EPS) + ADAM_WD * w)
    return delta, m, v


def _adamw_sharded(params):
    k = len(params)

    def body(*refs):
        ins, outs = refs[:4 * k], refs[4 * k:]
        for p in range(k):
            w_ref, g_ref, m_ref, v_ref = ins[4 * p:4 * p + 4]
            go_ref = outs[4 * p]
            if g_ref.shape == w_ref.shape:
                go_ref[...] = g_ref[...]
            else:
                for j in range(FLAT_ROWS):
                    go_ref[pl.ds(j, W_IN_SHARD, stride=FLAT_ROWS), :] = g_ref[:, 128 * j:128 * (j + 1)]
            d, nm, nv = _adamw_math(w_ref[...], go_ref[...], m_ref[...], v_ref[...])
            outs[4 * p + 1][...] = d
            outs[4 * p + 2][...] = nm
            outs[4 * p + 3][...] = nv

    flat = [a for p in params for a in p]
    out_shape = tuple(jax.ShapeDtypeStruct(p[0].shape, F32) for p in params for _ in range(4))
    res = pl.pallas_call(body, name="adamw_sharded", out_shape=out_shape, compiler_params=_cp())(*flat)
    return [tuple(res[4 * p:4 * p + 4]) for p in range(k)]


def _adamw_replicated(gath_a, gath_b, pool, rows):
    nrow = len(rows)

    def body(*refs):
        ga_ref, gb_ref = refs[:2]
        ins = refs[2:2 + 3 * (nrow + 1)]
        outs = refs[2 + 3 * (nrow + 1):]

        def total(ref):
            g = ref[0]
            for d in range(1, N_DEV):
                g = g + ref[d]
            return g

        def update(g, wmv, o):
            w, m, v = (r[...] for r in wmv)
            dl, nm, nv = _adamw_math(w, g, m, v)
            o[0][...] = g
            o[1][...] = dl
            o[2][...] = nm
            o[3][...] = nv

        update(total(ga_ref), ins[:3], outs[:4])
        gb = total(gb_ref)
        for r in range(nrow):
            n = ins[3 * (r + 1)].shape[1]
            update(gb[r:r + 1, :n], ins[3 * (r + 1):3 * (r + 2)], outs[4 * (r + 1):4 * (r + 2)])
        outs[4 * (nrow + 1)][...] = gb[nrow:nrow + 1, 0:1]

    flat = list(pool) + [a for wmv in rows for a in wmv]
    out_shape = ((jax.ShapeDtypeStruct(pool[0].shape, F32),) * 4
                 + tuple(jax.ShapeDtypeStruct(wmv[0].shape, F32) for wmv in rows for _ in range(4))
                 + (jax.ShapeDtypeStruct((1, 1), F32),))
    res = pl.pallas_call(body, name="adamw_replicated", out_shape=out_shape, compiler_params=_cp())(
        gath_a, gath_b, *flat)
    return [res[4 * k:4 * k + 4] for k in range(nrow + 1)], res[-1]


_ROW_ORDER = ("norm_w", "final_norm_w", "pool_scale", "dn_norm_w", "a_log", "dt_bias")


def _pack_rows(vectors):
    out = [jnp.pad(v.reshape(-1), (0, D_MODEL - v.size)) for v in vectors]
    out += [jnp.zeros((D_MODEL,), F32)] * (8 - len(out))
    return jnp.stack(out, axis=0)


def _lane_row(vec4, start):
    return jnp.pad(vec4.reshape(-1), (start, 128 - start - vec4.size)).reshape(1, 128)


def kernel(x, norm_w, w_in, pool_w, pool_scale, conv_w, a_log, dt_bias, dn_norm_w, w_out, final_norm_w, loss_target, m_norm_w, m_w_in, m_pool_w, m_pool_scale, m_conv_w, m_a_log, m_dt_bias, m_dn_norm_w, m_w_out, m_final_norm_w, v_norm_w, v_w_in, v_pool_w, v_pool_scale, v_conv_w, v_a_log, v_dt_bias, v_dn_norm_w, v_w_out, v_final_norm_w):
    s = x.shape[1]
    tm = min(512, s)
    tmb = min(256, s)
    tp = min(512, s)
    x2 = x[0]
    tgt = loss_target[0]
    def to_flat(a):
        return a[0].reshape(FLAT_ROWS, 128, W_IN_SHARD).transpose(2, 0, 1).reshape(W_IN_SHARD * FLAT_ROWS, 128)

    def from_flat(f):
        return f.reshape(W_IN_SHARD, FLAT_ROWS, 128).transpose(1, 2, 0).reshape(1, D_MODEL, W_IN_SHARD)

    wf, m_wf, v_wf = to_flat(w_in), to_flat(m_w_in), to_flat(v_w_in)

    g_in, g_out, g_conv = _gather_weights(wf, w_out[0], conv_w[0])
    w_out_full = g_out.reshape(D_MODEL, D_MODEL)
    conv_full = g_conv.transpose(1, 0, 2).reshape(CONV_WIDTH, 3 * D_DN)
    alog_lane = _lane_row(a_log, DN_HEADS)
    dtb_lane = _lane_row(dt_bias, DN_HEADS)
    fnw = final_norm_w.reshape(1, D_MODEL)

    proj_main, proj_ba, y_pool, qn, kn, vv, gb, wt_full = _front(
        x2, norm_w, g_in, pool_w[0], pool_scale, conv_full, alog_lane, dtb_lane, tm)
    y_dn, states = _dn_scan_fwd(qn, kn, vv, gb, proj_main, dn_norm_w, DN_CHUNKS_PER_STEP)

    dh, dyp, dyd, g_wout, g_fnw, loss_part = _out_proj_loss(y_pool, y_dn, x2, tgt, w_out_full, fnw, tm)
    dqn, dkn, dvv, dgb, ddz, g_dnw = _dn_scan_bwd(qn, kn, vv, gb, proj_main, dn_norm_w, states, dyd,
                                                  DN_CHUNKS_PER_STEP)
    grad_x2, p_in, g_nw, g_pw, g_ps, g_conv_full, g_al, g_db = _back(
        proj_main, proj_ba, dyp, dqn, dkn, dvv, dgb, ddz, x2, dh, norm_w, pool_w[0], pool_scale, conv_full,
        alog_lane, dtb_lane, wt_full, tmb)

    p_out = g_wout.reshape(N_DEV, D_MODEL // N_DEV, D_MODEL)
    p_conv = g_conv_full.reshape(CONV_WIDTH, N_DEV, 3 * D_DN // N_DEV).transpose(1, 0, 2)
    pack_a = g_pw.reshape(4 * POOL_GROUP, POOL_GROUP)
    pack_b = _pack_rows([g_nw, g_fnw, g_ps, g_dnw, g_al[0, DN_HEADS:2 * DN_HEADS], g_db[0, DN_HEADS:2 * DN_HEADS],
                         loss_part[0, :1]])
    gr_in, gr_out, gr_conv, gath_a, gath_b = _reduce_grads(p_in, p_out, p_conv, pack_a, pack_b)

    r_in, r_out, r_conv = _adamw_sharded([(wf, gr_in, m_wf, v_wf), (w_out[0], gr_out, m_w_out[0], v_w_out[0]),
                                          (conv_w[0], gr_conv, m_conv_w[0], v_conv_w[0])])
    flat = lambda a: a.reshape(4 * POOL_GROUP, POOL_GROUP)
    row = lambda a: a.reshape(1, -1)
    vecs = {"norm_w": (norm_w, m_norm_w, v_norm_w), "final_norm_w": (final_norm_w, m_final_norm_w, v_final_norm_w),
            "pool_scale": (pool_scale, m_pool_scale, v_pool_scale), "dn_norm_w": (dn_norm_w, m_dn_norm_w, v_dn_norm_w),
            "a_log": (a_log, m_a_log, v_a_log), "dt_bias": (dt_bias, m_dt_bias, v_dt_bias)}
    res, loss = _adamw_replicated(gath_a, gath_b, (flat(pool_w), flat(m_pool_w), flat(v_pool_w)),
                                  [tuple(row(a) for a in vecs[nm]) for nm in _ROW_ORDER])
    r_pool = res[0]
    r_vec = dict(zip(_ROW_ORDER, res[1:]))

    def group(k):
        vec = lambda nm: r_vec[nm][k].reshape(vecs[nm][0].shape)
        return (vec("norm_w"), from_flat(r_in[k]), r_pool[k].reshape(pool_w.shape), vec("pool_scale"), r_conv[k][None],
                vec("a_log"), vec("dt_bias"), vec("dn_norm_w"), r_out[k][None], vec("final_norm_w"))

    return (loss[0, 0], grad_x2[None], *group(0), *group(1), *group(2), *group(3))
```

```python
import functools

import jax
import jax.numpy as jnp
from jax import lax
from jax.experimental import pallas as pl
from jax.experimental.pallas import tpu as pltpu

F32 = jnp.float32
BF16 = jnp.bfloat16
HI = lax.Precision.HIGHEST
MESH = pl.DeviceIdType.MESH

D_MODEL = 1024
D_POOL = 512
D_DN = 512
POOL_WINDOWS = (2, 4, 8, 16)
POOL_GROUP = 128
DN_HEADS = 4
DN_HEAD_DIM = 128
CONV_WIDTH = 4
CHUNK = 64
NORM_EPS = 1e-6
D_IN = 3080
D_MAIN = 3072
FLAT_ROWS = D_MODEL // 128
D_IN_PAD = D_MAIN + 128
N_DEV = 8
W_IN_SHARD = D_IN // N_DEV
HALO = 16
DN_CHUNKS_PER_STEP = 8

ADAM_LR = 0.001
ADAM_B1 = 0.9
ADAM_B2 = 0.999
ADAM_EPS = 1e-08
ADAM_WD = 0.01
ADAM_STEP = 10

VMEM_LIMIT = 56 * 1024 * 1024


def _cp(sem=None, vmem=VMEM_LIMIT):
    kw = {"vmem_limit_bytes": vmem}
    if sem is not None:
        kw["dimension_semantics"] = sem
    return pltpu.CompilerParams(**kw)


def _dot_bf(a, b):
    return jnp.dot(a.astype(BF16), b.astype(BF16), preferred_element_type=F32)


def _dot_nt_bf(a, b):
    return lax.dot_general(a.astype(BF16), b.astype(BF16), (((1,), (1,)), ((), ())), preferred_element_type=F32)


def _dot_tn_bf(a, b):
    return lax.dot_general(a.astype(BF16), b.astype(BF16), (((0,), (0,)), ((), ())), preferred_element_type=F32)


def _mm_raw(a, b, ca, cb, prec):
    off = a.ndim - 2
    dn = (((ca + off,), (cb + off,)), ((0,), (0,)) if off else ((), ()))
    if prec == "hi":
        return lax.dot_general(a, b, dn, precision=HI, preferred_element_type=F32)
    ah, bh = a.astype(BF16), b.astype(BF16)
    out = lax.dot_general(ah, bh, dn, preferred_element_type=F32)
    if prec == "x3":
        al = (a - ah.astype(F32)).astype(BF16)
        bl = (b - bh.astype(F32)).astype(BF16)
        out = out + lax.dot_general(ah, bl, dn, preferred_element_type=F32)
        out = out + lax.dot_general(al, bh, dn, preferred_element_type=F32)
    return out


@functools.partial(jax.custom_vjp, nondiff_argnums=(2, 3, 4, 5))
def _mm(a, b, ca, cb, prec, bprec):
    return _mm_raw(a, b, ca, cb, prec)


def _mm_fwd(a, b, ca, cb, prec, bprec):
    return _mm_raw(a, b, ca, cb, prec), (a, b)


def _mm_bwd(ca, cb, prec, bprec, res, dc):
    a, b = res
    da = _mm_raw(dc, b, 1, 1 - cb, bprec) if ca == 1 else _mm_raw(b, dc, 1 - cb, 1, bprec)
    db = _mm_raw(a, dc, 1 - ca, 0, bprec) if cb == 0 else _mm_raw(dc, a, 0, 1 - ca, bprec)
    return da, db


_mm.defvjp(_mm_fwd, _mm_bwd)


@functools.partial(jax.custom_vjp, nondiff_argnums=(1, 2))
def _tri_inv(a, prec, bprec):
    n = a.shape[-1]
    ii = lax.broadcasted_iota(jnp.int32, (n, n), 0)
    jj = lax.broadcasted_iota(jnp.int32, (n, n), 1)
    p = (ii == jj).astype(F32) - a
    b = _mm_raw(a, a, 1, 0, prec)
    for _ in range(4):
        pb = _mm_raw(jnp.concatenate([p, b], axis=-2), b, 1, 0, prec)
        p = p + pb[..., :n, :]
        b = pb[..., n:, :]
    return p + _mm_raw(p, b, 1, 0, prec)


def _tri_inv_fwd(a, prec, bprec):
    t = _tri_inv(a, prec, bprec)
    return t, t


def _tri_inv_bwd(prec, bprec, t, dt):
    return (-_mm_raw(_mm_raw(t, dt, 0, 0, bprec), t, 1, 1, bprec),)


_tri_inv.defvjp(_tri_inv_fwd, _tri_inv_bwd)

_DN_PREC = {"akq": ("bf16", "bf16"), "inv": ("bf16", "bf16"), "uw": ("bf16", "bf16"), "ws": ("bf16", "bf16"),
            "ov": ("bf16", "bf16"), "st": ("bf16", "bf16")}


def _silu(x):
    return x * jax.nn.sigmoid(x)


def _softplus(x):
    pos = x > 0.0
    return jnp.where(pos, x, 0.0) + jnp.log1p(jnp.exp(jnp.where(pos, -x, x)))


def _mesh_pos():
    return lax.axis_index("x"), lax.axis_index("y"), lax.axis_index("c")


def _dev_index(x, y, c):
    return 4 * x + 2 * y + c


def _all_gather_blocks(outs, send_sems, recv_sems):
    x, y, c = _mesh_pos()
    me = (x, y, c)
    sibling = (x, y, 1 - c)
    chips = [(1 - x, y), (x, 1 - y), (1 - x, 1 - y)]

    def copy(a, k, block, to):
        rows = outs[a].at[_dev_index(*block)]
        return pltpu.make_async_remote_copy(src_ref=rows, dst_ref=rows, send_sem=send_sems.at[a, k],
                                            recv_sem=recv_sems.at[a, k], device_id=to, device_id_type=MESH)

    n = len(outs)
    first = []
    for a in range(n):
        first.append(copy(a, 0, me, sibling))
        for j, chip in enumerate(chips):
            first.append(copy(a, 1 + j, me, (*chip, c)))
    for cp in first:
        cp.start()
    passed = []
    for j, chip in enumerate(chips):
        for a in range(n):
            copy(a, 1 + j, (*chip, c), me).wait_recv()
            fwd = copy(a, 4 + j, (*chip, c), sibling)
            fwd.start()
            passed.append(fwd)
    for a in range(n):
        copy(a, 0, sibling, me).wait_recv()
        for j, chip in enumerate(chips):
            copy(a, 4 + j, (*chip, 1 - c), me).wait_recv()
    for cp in first + passed:
        cp.wait_send()


def _peer_relations():
    x, y, c = _mesh_pos()
    flips = [(fx, fy, fc) for fx in (0, 1) for fy in (0, 1) for fc in (0, 1)][1:]
    peers = [(1 - x if fx else x, 1 - y if fy else y, 1 - c if fc else c) for fx, fy, fc in flips]
    return (x, y, c), peers


def _direct_gather_start(out_ref, send_sems, recv_sems):
    me, peers = _peer_relations()
    rows = out_ref.at[_dev_index(*me)]
    for k, peer in enumerate(peers):
        pltpu.make_async_remote_copy(src_ref=rows, dst_ref=rows, send_sem=send_sems.at[k], recv_sem=recv_sems.at[k],
                                     device_id=peer, device_id_type=MESH).start()


def _direct_gather_wait(out_ref, send_sems, recv_sems):
    me, peers = _peer_relations()
    for k, peer in enumerate(peers):
        rows = out_ref.at[_dev_index(*peer)]
        cp = pltpu.make_async_remote_copy(src_ref=rows, dst_ref=rows, send_sem=send_sems.at[k],
                                          recv_sem=recv_sems.at[k], device_id=peer, device_id_type=MESH)
        cp.wait_recv()
        cp.wait_send()


def _direct_scatter_start(send_ref, recv_ref, send_sems, recv_sems):
    me, peers = _peer_relations()
    for k, peer in enumerate(peers):
        pltpu.make_async_remote_copy(src_ref=send_ref.at[_dev_index(*peer)], dst_ref=recv_ref.at[k],
                                     send_sem=send_sems.at[k], recv_sem=recv_sems.at[k],
                                     device_id=peer, device_id_type=MESH).start()


def _direct_scatter_wait(send_ref, recv_ref, send_sems, recv_sems):
    me, peers = _peer_relations()
    for k, peer in enumerate(peers):
        cp = pltpu.make_async_remote_copy(src_ref=send_ref.at[_dev_index(*peer)], dst_ref=recv_ref.at[k],
                                          send_sem=send_sems.at[k], recv_sem=recv_sems.at[k],
                                          device_id=peer, device_id_type=MESH)
        cp.wait_recv()
        cp.wait_send()


def _gather_weights(w_in_flat, conv_blk):
    def body(win_ref, conv_ref, gin_ref, gconv_ref, send_sems, recv_sems):
        x, y, c = _mesh_pos()
        me = _dev_index(x, y, c)
        for j in range(FLAT_ROWS):
            gin_ref[me, :, 128 * j:128 * (j + 1)] = win_ref[pl.ds(j, W_IN_SHARD, stride=FLAT_ROWS), :].astype(BF16)
        gconv_ref[me] = conv_ref[...]
        _all_gather_blocks((gin_ref, gconv_ref), send_sems, recv_sems)

    vm = pl.BlockSpec(memory_space=pltpu.VMEM)
    return pl.pallas_call(
        body, name="gather_weights",
        out_shape=(jax.ShapeDtypeStruct((N_DEV, W_IN_SHARD, D_MODEL), BF16),
                   jax.ShapeDtypeStruct((N_DEV,) + conv_blk.shape, F32)),
        in_specs=[vm, vm], out_specs=(vm, vm),
        scratch_shapes=[pltpu.SemaphoreType.DMA((2, 7)), pltpu.SemaphoreType.DMA((2, 7))],
        compiler_params=_cp(),
    )(w_in_flat, conv_blk)


def _reduce_grads(big, pack_a, pack_b):
    nb = len(big)

    def body(*refs):
        srcs, (pa_ref, pb_ref), outs, (ga_ref, gb_ref) = (
            refs[:nb], refs[nb:nb + 2], refs[nb + 2:2 * nb + 2], refs[2 * nb + 2:2 * nb + 4])
        scr = refs[2 * nb + 4:]
        r1s, r2s, sbs, sts = (scr[k * nb:(k + 1) * nb] for k in range(4))
        s1_send, s1_recv, s2_send, s2_recv, ag_send, ag_recv, st_sem = scr[4 * nb:]
        x, y, c = _mesh_pos()
        me = (x, y, c)
        sibling = (x, y, 1 - c)
        rel = [(x, y), (1 - x, y), (x, 1 - y), (1 - x, 1 - y)]

        ga_ref[_dev_index(*me)] = pa_ref[...]
        gb_ref[_dev_index(*me)] = pb_ref[...]

        def p1(a, r, to):
            return pltpu.make_async_remote_copy(
                src_ref=srcs[a].at[_dev_index(*rel[r], 1 - c)], dst_ref=r1s[a].at[r],
                send_sem=s1_send.at[a, r], recv_sem=s1_recv.at[a, r], device_id=to, device_id_type=MESH)

        def p2(a, r, to):
            return pltpu.make_async_remote_copy(
                src_ref=sbs[a].at[r - 1], dst_ref=r2s[a].at[r - 1],
                send_sem=s2_send.at[a, r - 1], recv_sem=s2_recv.at[a, r - 1], device_id=to, device_id_type=MESH)

        def stage(a, r):
            return pltpu.make_async_copy(srcs[a].at[_dev_index(*rel[r], c)], sts[a].at[r % 2], st_sem.at[a, r % 2])

        sends1 = [p1(a, r, sibling) for a in range(nb) for r in range(4)]
        for cp in sends1:
            cp.start()
        sends2 = []
        for a in range(nb):
            stage(a, 1).start()
            for r in (1, 2, 3, 0):
                nxt = {1: 2, 2: 3, 3: 0, 0: None}[r]
                if nxt is not None:
                    stage(a, nxt).start()
                stage(a, r).wait()
                p1(a, r, me).wait_recv()
                chip_sum = r1s[a][r] + sts[a][r % 2]
                if r == 0:
                    r1s[a][0] = chip_sum
                else:
                    sbs[a][r - 1] = chip_sum.astype(BF16)
                    cp = p2(a, r, (*rel[r], c))
                    cp.start()
                    sends2.append(cp)
        _all_gather_blocks((ga_ref, gb_ref), ag_send, ag_recv)
        for a in range(nb):
            for r in (1, 2, 3):
                p2(a, r, me).wait_recv()
            outs[a][...] = ((r1s[a][0] + r2s[a][0].astype(F32)) + r2s[a][1].astype(F32)) + r2s[a][2].astype(F32)
        for cp in sends1 + sends2:
            cp.wait_send()

    vm = pl.BlockSpec(memory_space=pltpu.VMEM)
    hbm = pl.BlockSpec(memory_space=pl.ANY)
    blk = [p.shape[1:] for p in big]
    scratch = ([pltpu.VMEM((4,) + b, F32) for b in blk] + [pltpu.VMEM((3,) + b, BF16) for b in blk]
               + [pltpu.VMEM((3,) + b, BF16) for b in blk] + [pltpu.VMEM((2,) + b, F32) for b in blk]
               + [pltpu.SemaphoreType.DMA((nb, 4)), pltpu.SemaphoreType.DMA((nb, 4)),
                  pltpu.SemaphoreType.DMA((nb, 3)), pltpu.SemaphoreType.DMA((nb, 3)),
                  pltpu.SemaphoreType.DMA((2, 7)), pltpu.SemaphoreType.DMA((2, 7)),
                  pltpu.SemaphoreType.DMA((nb, 2))])
    return pl.pallas_call(
        body, name="reduce_grads",
        out_shape=tuple(jax.ShapeDtypeStruct(b, F32) for b in blk)
        + (jax.ShapeDtypeStruct((N_DEV,) + pack_a.shape, F32), jax.ShapeDtypeStruct((N_DEV,) + pack_b.shape, F32)),
        in_specs=[hbm] * nb + [vm, vm], out_specs=(vm,) * (nb + 2),
        scratch_shapes=scratch,
        compiler_params=_cp(),
    )(*big, pack_a, pack_b)


def _rms_hat(xf):
    r = lax.rsqrt(jnp.mean(xf * xf, axis=-1, keepdims=True) + NORM_EPS)
    return xf * r, r


def _in_proj(x2, norm_w, g_in, tm):
    s = x2.shape[0]

    def body(x_ref, nw_ref, g_hbm, pm_ref, pb_ref, wt_hbm, g_vmem, wt_vmem, sem):
        @pl.when(pl.program_id(0) == 0)
        def _():
            cp = pltpu.make_async_copy(g_hbm, g_vmem, sem)
            cp.start()
            wt_vmem[D_MAIN:, :] = jnp.zeros((D_IN_PAD - D_MAIN, D_MODEL), BF16)
            cp.wait()
            for d in range(N_DEV):
                wt_vmem[W_IN_SHARD * d:W_IN_SHARD * (d + 1), :] = g_vmem[d]
            out = pltpu.make_async_copy(wt_vmem, wt_hbm, sem)
            out.start()
            out.wait()
        xhat, _ = _rms_hat(x_ref[...])
        n = (xhat * nw_ref[...]).astype(BF16)
        pm_ref[...] = _dot_nt_bf(n, wt_vmem[:D_MAIN, :])
        pb_ref[...] = _dot_nt_bf(n, wt_vmem[D_MAIN:, :])

    return pl.pallas_call(
        body, name="in_proj", grid=(s // tm,),
        out_shape=(jax.ShapeDtypeStruct((s, D_MAIN), F32), jax.ShapeDtypeStruct((s, 128), F32),
                   jax.ShapeDtypeStruct((D_IN_PAD, D_MODEL), BF16)),
        in_specs=[pl.BlockSpec((tm, D_MODEL), lambda i: (i, 0)),
                  pl.BlockSpec((1, D_MODEL), lambda i: (0, 0)),
                  pl.BlockSpec(memory_space=pl.ANY)],
        out_specs=(pl.BlockSpec((tm, D_MAIN), lambda i: (i, 0)), pl.BlockSpec((tm, 128), lambda i: (i, 0)),
                   pl.BlockSpec(memory_space=pl.ANY)),
        scratch_shapes=[pltpu.VMEM((N_DEV, W_IN_SHARD, D_MODEL), BF16), pltpu.VMEM((D_IN_PAD, D_MODEL), BF16),
                        pltpu.SemaphoreType.DMA],
        compiler_params=_cp(("arbitrary",)),
    )(x2, norm_w, g_in)


def _shift_down(cur, prev_tail, s):
    ext = jnp.concatenate([prev_tail, cur], axis=0)
    return pltpu.roll(ext, s, 0)[HALO:, :]


def _shift_up(cur, next_head, s):
    ext = jnp.concatenate([cur, next_head], axis=0)
    n = ext.shape[0]
    return pltpu.roll(ext, n - s, 0)[:cur.shape[0], :]


def _pool_counts(i, tp, w):
    t = i * tp + lax.broadcasted_iota(jnp.int32, (tp, 1), 0)
    return jnp.minimum(t + 1, w).astype(F32)


def _pool_mix(u, u_prev_tail, i, tp):
    ext = jnp.concatenate([u_prev_tail, u], axis=0)
    w2 = ext + pltpu.roll(ext, 1, 0)
    w4 = w2 + pltpu.roll(w2, 2, 0)
    w8 = w4 + pltpu.roll(w4, 4, 0)
    w16 = w8 + pltpu.roll(w8, 8, 0)
    mixes = []
    for gi, (w, win) in enumerate(zip(POOL_WINDOWS, (w2, w4, w8, w16))):
        cols = slice(gi * POOL_GROUP, (gi + 1) * POOL_GROUP)
        mixes.append(win[HALO:, cols] / _pool_counts(i, tp, w) - u[:, cols])
    return mixes


def _prev_halo_spec(tp, width, col):
    per = tp // HALO
    return pl.BlockSpec((HALO, width), lambda i: (jnp.maximum(i * per - 1, 0), col))


def _next_halo_spec(tp, width, col, n):
    per = tp // HALO
    return pl.BlockSpec((HALO, width), lambda i: (jnp.minimum((i + 1) * per, n * per - 1), col))


def _prev_tail(ref, i):
    return jnp.where(i > 0, ref[...], 0.0)


def _next_head(ref, i, n):
    return jnp.where(i < n - 1, ref[...], 0.0)


def _pool_fwd(proj_main, pool_w, pool_scale, tp):
    s = proj_main.shape[0]

    def body(u_ref, up_ref, z_ref, pw_ref, ps_ref, y_ref):
        i = pl.program_id(0)
        u = u_ref[...]
        mixes = _pool_mix(u, _prev_tail(up_ref, i), i, tp)
        gate = ps_ref[...] * _silu(z_ref[...])
        for gi in range(4):
            cols = slice(gi * POOL_GROUP, (gi + 1) * POOL_GROUP)
            y_ref[:, cols] = _dot_bf(mixes[gi], pw_ref[gi]) * gate[:, cols]

    return pl.pallas_call(
        body, name="pool_fwd", grid=(s // tp,),
        out_shape=jax.ShapeDtypeStruct((s, D_POOL), F32),
        in_specs=[pl.BlockSpec((tp, D_POOL), lambda i: (i, 0)),
                  _prev_halo_spec(tp, D_POOL, 0),
                  pl.BlockSpec((tp, D_POOL), lambda i: (i, 1)),
                  pl.BlockSpec((4, POOL_GROUP, POOL_GROUP), lambda i: (0, 0, 0)),
                  pl.BlockSpec((1, D_POOL), lambda i: (0, 0))],
        out_specs=pl.BlockSpec((tp, D_POOL), lambda i: (i, 0)),
        compiler_params=_cp(("parallel",)),
    )(proj_main, proj_main, proj_main, pool_w, pool_scale)


def _conv_fwd(cur, prev_tail, w4):
    ext = jnp.concatenate([prev_tail, cur], axis=0)
    y = ext * w4[CONV_WIDTH - 1:CONV_WIDTH, :]
    for sft in range(1, CONV_WIDTH):
        y = y + pltpu.roll(ext, sft, 0) * w4[CONV_WIDTH - 1 - sft:CONV_WIDTH - sft, :]
    return y[HALO:, :]


def _l2n_heads(t):
    parts = []
    for h in range(DN_HEADS):
        th = t[:, h * DN_HEAD_DIM:(h + 1) * DN_HEAD_DIM]
        parts.append(th * lax.rsqrt(jnp.sum(th * th, axis=-1, keepdims=True) + NORM_EPS))
    return jnp.concatenate(parts, axis=1)


def _post_conv(yq, yk, yv):
    return _l2n_heads(_silu(yq)), _l2n_heads(_silu(yk)), _silu(yv)


def _gates(ba, alog_lane, dtb_lane):
    lane = lax.broadcasted_iota(jnp.int32, ba.shape, 1)
    beta = jax.nn.sigmoid(ba)
    g = -jnp.exp(alog_lane) * _softplus(ba + dtb_lane)
    return jnp.where(lane < DN_HEADS, beta, jnp.where(lane < 2 * DN_HEADS, g, 0.0))


def _front(x2, norm_w, g_in, pool_w, pool_scale, conv_full, alog_lane, dtb_lane, w_out_blk, tm):
    s = x2.shape[0]

    def body(x_ref, nw_ref, pw_ref, ps_ref, cw_ref, al_ref, db_ref, wo_ref, g_hbm,
             pm_ref, pb_ref, yp_ref, qn_ref, kn_ref, vv_ref, gb_ref, gwo_hbm, wt_hbm,
             g_vmem, wt_vmem, tail_u, tail_qkv, gwo_ref, sem, wo_send, wo_recv):
        i = pl.program_id(0)

        @pl.when(i == 0)
        def _():
            gwo_ref[_dev_index(*_mesh_pos())] = wo_ref[...].astype(BF16)
            _direct_gather_start(gwo_ref, wo_send, wo_recv)
            cp = pltpu.make_async_copy(g_hbm, g_vmem, sem)
            cp.start()
            wt_vmem[D_MAIN:, :] = jnp.zeros((D_IN_PAD - D_MAIN, D_MODEL), BF16)
            tail_u[...] = jnp.zeros_like(tail_u)
            tail_qkv[...] = jnp.zeros_like(tail_qkv)
            cp.wait()
            for d in range(N_DEV):
                wt_vmem[W_IN_SHARD * d:W_IN_SHARD * (d + 1), :] = g_vmem[d]
            out = pltpu.make_async_copy(wt_vmem, wt_hbm, sem)
            out.start()
            out.wait()
        xhat, _ = _rms_hat(x_ref[...])
        n = (xhat * nw_ref[...]).astype(BF16)
        pm_ref[...] = _dot_nt_bf(n, wt_vmem[:D_MAIN, :])
        pb = _dot_nt_bf(n, wt_vmem[D_MAIN:, :])
        pb_ref[...] = pb
        u = pm_ref[:, :D_POOL]
        mixes = _pool_mix(u, tail_u[...], i, tm)
        tail_u[...] = u[tm - HALO:, :]
        gate = ps_ref[...] * _silu(pm_ref[:, D_POOL:2 * D_POOL])
        for gi in range(4):
            cols = slice(gi * POOL_GROUP, (gi + 1) * POOL_GROUP)
            yp_ref[:, cols] = _dot_bf(mixes[gi], pw_ref[gi]) * gate[:, cols]
        ys = []
        for c in range(3):
            cols = slice(c * D_DN, (c + 1) * D_DN)
            cur = pm_ref[:, 2 * D_POOL + c * D_DN:2 * D_POOL + (c + 1) * D_DN]
            ys.append(_conv_fwd(cur, tail_qkv[:, cols], cw_ref[:, cols]))
            tail_qkv[:, cols] = cur[tm - HALO:, :]
        qn, kn, vv = _post_conv(*ys)
        qn_ref[...] = qn
        kn_ref[...] = kn
        vv_ref[...] = vv
        gb_ref[...] = _gates(pb, al_ref[...], db_ref[...])

        @pl.when(i == s // tm - 1)
        def _():
            _direct_gather_wait(gwo_ref, wo_send, wo_recv)
            out = pltpu.make_async_copy(gwo_ref, gwo_hbm, sem)
            out.start()
            out.wait()

    tile = pl.BlockSpec((tm, D_DN), lambda i: (i, 0))
    lanes = pl.BlockSpec((tm, 128), lambda i: (i, 0))
    row = pl.BlockSpec((1, 128), lambda i: (0, 0))
    return pl.pallas_call(
        body, name="front", grid=(s // tm,),
        out_shape=(jax.ShapeDtypeStruct((s, D_MAIN), F32), jax.ShapeDtypeStruct((s, 128), F32),
                   jax.ShapeDtypeStruct((s, D_POOL), F32), jax.ShapeDtypeStruct((s, D_DN), F32),
                   jax.ShapeDtypeStruct((s, D_DN), F32), jax.ShapeDtypeStruct((s, D_DN), F32),
                   jax.ShapeDtypeStruct((s, 128), F32), jax.ShapeDtypeStruct((N_DEV,) + w_out_blk.shape, BF16),
                   jax.ShapeDtypeStruct((D_IN_PAD, D_MODEL), BF16)),
        in_specs=[pl.BlockSpec((tm, D_MODEL), lambda i: (i, 0)),
                  pl.BlockSpec((1, D_MODEL), lambda i: (0, 0)),
                  pl.BlockSpec((4, POOL_GROUP, POOL_GROUP), lambda i: (0, 0, 0)),
                  pl.BlockSpec((1, D_POOL), lambda i: (0, 0)),
                  pl.BlockSpec((CONV_WIDTH, 3 * D_DN), lambda i: (0, 0)), row, row,
                  pl.BlockSpec(memory_space=pltpu.VMEM), pl.BlockSpec(memory_space=pl.ANY)],
        out_specs=(pl.BlockSpec((tm, D_MAIN), lambda i: (i, 0)), lanes, tile, tile, tile, tile, lanes,
                   pl.BlockSpec(memory_space=pl.ANY), pl.BlockSpec(memory_space=pl.ANY)),
        scratch_shapes=[pltpu.VMEM((N_DEV, W_IN_SHARD, D_MODEL), BF16), pltpu.VMEM((D_IN_PAD, D_MODEL), BF16),
                        pltpu.VMEM((HALO, D_POOL), F32), pltpu.VMEM((HALO, 3 * D_DN), F32),
                        pltpu.VMEM((N_DEV,) + w_out_blk.shape, BF16),
                        pltpu.SemaphoreType.DMA, pltpu.SemaphoreType.DMA((7,)), pltpu.SemaphoreType.DMA((7,))],
        compiler_params=_cp(("arbitrary",)),
    )(x2, norm_w, pool_w, pool_scale, conv_full, alog_lane, dtb_lane, w_out_blk, g_in)


def _qkv_specs(tp, which):
    def spec(col, n=None):
        if which == 0:
            return pl.BlockSpec((tp, D_DN), lambda i: (i, col))
        if which < 0:
            return _prev_halo_spec(tp, D_DN, col)
        return _next_halo_spec(tp, D_DN, col, n)
    return spec


def _dn_pre(proj_main, proj_ba, conv_full, alog_lane, dtb_lane, tp):
    s = proj_main.shape[0]

    def body(q_ref, k_ref, v_ref, qp_ref, kp_ref, vp_ref, cw_ref, ba_ref, al_ref, db_ref,
             qn_ref, kn_ref, vv_ref, gb_ref):
        i = pl.program_id(0)
        ys = []
        for j, (cur, prev) in enumerate(((q_ref, qp_ref), (k_ref, kp_ref), (v_ref, vp_ref))):
            ys.append(_conv_fwd(cur[...], _prev_tail(prev, i), cw_ref[:, j * D_DN:(j + 1) * D_DN]))
        qn, kn, vv = _post_conv(*ys)
        qn_ref[...] = qn
        kn_ref[...] = kn
        vv_ref[...] = vv
        gb_ref[...] = _gates(ba_ref[...], al_ref[...], db_ref[...])

    cur, prev = _qkv_specs(tp, 0), _qkv_specs(tp, -1)
    row = pl.BlockSpec((1, 128), lambda i: (0, 0))
    tile = pl.BlockSpec((tp, D_DN), lambda i: (i, 0))
    return pl.pallas_call(
        body, name="dn_pre", grid=(s // tp,),
        out_shape=(jax.ShapeDtypeStruct((s, D_DN), F32),) * 3 + (jax.ShapeDtypeStruct((s, 128), F32),),
        in_specs=[cur(2), cur(3), cur(4), prev(2), prev(3), prev(4),
                  pl.BlockSpec((CONV_WIDTH, 3 * D_DN), lambda i: (0, 0)),
                  pl.BlockSpec((tp, 128), lambda i: (i, 0)), row, row],
        out_specs=(tile, tile, tile, pl.BlockSpec((tp, 128), lambda i: (i, 0))),
        compiler_params=_cp(("parallel",)),
    )(proj_main, proj_main, proj_main, proj_main, proj_main, proj_main, conv_full, proj_ba, alog_lane, dtb_lane)


def _dn_block(q, k, v, gcol, bcol, state, dz, nw):
    nb, n, d = q.shape
    ii = lax.broadcasted_iota(jnp.int32, (n, n), 0)
    jj = lax.broadcasted_iota(jnp.int32, (n, n), 1)
    lower = ii >= jj
    eye = (ii == jj).astype(F32)
    g_row = jnp.sum(eye * gcol, axis=1, keepdims=True)
    gc_col = jnp.sum(jnp.where(lower, g_row, 0.0), axis=2, keepdims=True)
    gc_row = jnp.sum(eye * gc_col, axis=1, keepdims=True)
    decay = jnp.where(lower, jnp.exp(jnp.where(lower, gc_col - gc_row, 0.0)), 0.0)
    kb = k * bcol
    vb = v * bcol
    qs = q * (DN_HEAD_DIM ** -0.5)
    egc = jnp.exp(gc_col)
    akq = _mm(jnp.concatenate([kb, qs], axis=1), k, 1, 1, *_DN_PREC["akq"])
    a = jnp.where(ii > jj, akq[:, :n] * decay, 0.0)
    qk = akq[:, n:] * decay
    t = _tri_inv(a, *_DN_PREC["inv"])
    uw = _mm(t, jnp.concatenate([vb, kb * egc], axis=2), 1, 0, *_DN_PREC["uw"])
    wq = jnp.concatenate([uw[:, :, d:], qs * egc], axis=1)
    g_last = gc_col[:, n - 1:n, :]
    k_dec = k * jnp.exp(g_last - gc_col)
    e_last = jnp.exp(g_last)
    os_ = []
    for c in range(nb // DN_HEADS):
        sl = slice(c * DN_HEADS, (c + 1) * DN_HEADS)
        ws = _mm(wq[sl], state, 1, 0, *_DN_PREC["ws"])
        v_new = uw[sl, :, :d] - ws[:, :n]
        os_.append(ws[:, n:] + _mm(qk[sl], v_new, 1, 0, *_DN_PREC["ov"]))
        state = state * e_last[sl] + _mm(k_dec[sl], v_new, 0, 0, *_DN_PREC["st"])
    o = jnp.concatenate(os_, axis=0)
    y = o * lax.rsqrt(jnp.mean(o * o, axis=-1, keepdims=True) + NORM_EPS) * nw * _silu(dz)
    return y, state


def _dn_block_args(gc, q_ref, k_ref, v_ref, gb_ref, dz_ref):
    qs, ks, vs, gs, bs, zs = [], [], [], [], [], []
    for cc in range(gc):
        r = slice(cc * CHUNK, (cc + 1) * CHUNK)
        gbv = gb_ref[r, :]
        for h in range(DN_HEADS):
            cols = slice(h * DN_HEAD_DIM, (h + 1) * DN_HEAD_DIM)
            qs.append(q_ref[r, cols])
            ks.append(k_ref[r, cols])
            vs.append(v_ref[r, cols])
            zs.append(dz_ref[r, cols])
            gs.append(gbv[:, DN_HEADS + h:DN_HEADS + h + 1])
            bs.append(gbv[:, h:h + 1])
    return tuple(jnp.stack(t, axis=0) for t in (qs, ks, vs, gs, bs, zs))


def _dn_scan_fwd(qn, kn, vv, gb, proj_main, dn_norm_w, gc):
    s = qn.shape[0]
    nchunk = s // CHUNK
    rows = gc * CHUNK

    def body(q_ref, k_ref, v_ref, gb_ref, dz_ref, nw_ref, y_ref, ss_ref, state):
        @pl.when(pl.program_id(0) == 0)
        def _():
            state[...] = jnp.zeros_like(state)
        q, k, v, gcol, bcol, dz = _dn_block_args(gc, q_ref, k_ref, v_ref, gb_ref, dz_ref)
        st = state[...]
        ss_ref[0] = st
        y, new = _dn_block(q, k, v, gcol, bcol, st, dz, nw_ref[...])
        state[...] = new
        for cc in range(gc):
            for h in range(DN_HEADS):
                y_ref[cc * CHUNK:(cc + 1) * CHUNK, h * DN_HEAD_DIM:(h + 1) * DN_HEAD_DIM] = y[cc * DN_HEADS + h]

    tile = pl.BlockSpec((rows, D_DN), lambda i: (i, 0))
    return pl.pallas_call(
        body, name="dn_scan_fwd", grid=(nchunk // gc,),
        out_shape=(jax.ShapeDtypeStruct((s, D_DN), F32),
                   jax.ShapeDtypeStruct((nchunk // gc, DN_HEADS, DN_HEAD_DIM, DN_HEAD_DIM), F32)),
        in_specs=[tile, tile, tile, pl.BlockSpec((rows, 128), lambda i: (i, 0)),
                  pl.BlockSpec((rows, D_DN), lambda i: (i, 5)), pl.BlockSpec((1, 128), lambda i: (0, 0))],
        out_specs=(tile, pl.BlockSpec((1, DN_HEADS, DN_HEAD_DIM, DN_HEAD_DIM), lambda i: (i, 0, 0, 0))),
        scratch_shapes=[pltpu.VMEM((DN_HEADS, DN_HEAD_DIM, DN_HEAD_DIM), F32)],
        compiler_params=_cp(("arbitrary",)),
    )(qn, kn, vv, gb, proj_main, dn_norm_w)


def _out_proj_loss(y_pool, y_dn, x2, tgt, w_out_full, fnw, tm):
    s = x2.shape[0]

    def body(yp_ref, yd_ref, x_ref, t_ref, wo_ref, fw_ref,
             dh_ref, dyp_ref, dyd_ref, gwo_ref, gfw_ref, loss_ref):
        @pl.when(pl.program_id(0) == 0)
        def _():
            gwo_ref[...] = jnp.zeros_like(gwo_ref)
            gfw_ref[...] = jnp.zeros_like(gfw_ref)
            loss_ref[...] = jnp.zeros_like(loss_ref)
        y = jnp.concatenate([yp_ref[...], yd_ref[...]], axis=1).astype(BF16)
        wo = wo_ref[...]
        h = x_ref[...] + jnp.dot(y, wo, preferred_element_type=F32)
        hn, r = _rms_hat(h)
        fw = fw_ref[...]
        err = hn * fw - t_ref[...]
        loss_ref[...] += 0.5 * jnp.sum(jnp.sum(err * err, axis=-1, keepdims=True) / D_MODEL, axis=0, keepdims=True)
        dout = err / D_MODEL
        gfw_ref[...] += jnp.sum(dout * hn, axis=0, keepdims=True)
        dhn = dout * fw
        dh = r * (dhn - hn * jnp.mean(dhn * hn, axis=-1, keepdims=True))
        dh_ref[...] = dh
        dhb = dh.astype(BF16)
        dy = _dot_nt_bf(dhb, wo)
        dyp_ref[...] = dy[:, :D_POOL]
        dyd_ref[...] = dy[:, D_POOL:]
        gwo_ref[...] += _dot_tn_bf(y, dhb)

    half = pl.BlockSpec((tm, D_POOL), lambda i: (i, 0))
    full = pl.BlockSpec((tm, D_MODEL), lambda i: (i, 0))
    return pl.pallas_call(
        body, name="out_proj_loss", grid=(s // tm,),
        out_shape=(jax.ShapeDtypeStruct((s, D_MODEL), F32), jax.ShapeDtypeStruct((s, D_POOL), F32),
                   jax.ShapeDtypeStruct((s, D_DN), F32), jax.ShapeDtypeStruct((D_MODEL, D_MODEL), F32),
                   jax.ShapeDtypeStruct((1, D_MODEL), F32), jax.ShapeDtypeStruct((1, 128), F32)),
        in_specs=[half, half, full, full, pl.BlockSpec((D_MODEL, D_MODEL), lambda i: (0, 0)),
                  pl.BlockSpec((1, D_MODEL), lambda i: (0, 0))],
        out_specs=(full, half, half, pl.BlockSpec((D_MODEL, D_MODEL), lambda i: (0, 0)),
                   pl.BlockSpec((1, D_MODEL), lambda i: (0, 0)), pl.BlockSpec((1, 128), lambda i: (0, 0))),
        compiler_params=_cp(("arbitrary",)),
    )(y_pool, y_dn, x2, tgt, w_out_full, fnw)


def _dn_scan_bwd(qn, kn, vv, gb, proj_main, dn_norm_w, states, dy_dn, p_out, gc):
    s = qn.shape[0]
    nchunk = s // CHUNK
    nstep = nchunk // gc
    rows = gc * CHUNK

    def body(q_ref, k_ref, v_ref, gb_ref, dz_ref, nw_ref, ss_ref, dy_ref, po_ref,
             dq_ref, dk_ref, dv_ref, dgb_ref, ddz_ref, dnw_ref, gro_ref, dstate, po_send, po_recv, rs_send, rs_recv):
        @pl.when(pl.program_id(0) == 0)
        def _():
            dstate[...] = jnp.zeros_like(dstate)
            dnw_ref[...] = jnp.zeros_like(dnw_ref)
            po_send[...] = po_ref[...].astype(BF16)
            _direct_scatter_start(po_send, po_recv, rs_send, rs_recv)

        @pl.when(pl.program_id(0) == nstep - 1)
        def _():
            _direct_scatter_wait(po_send, po_recv, rs_send, rs_recv)
            total = po_ref[_dev_index(*_mesh_pos())]
            for k in range(N_DEV - 1):
                total = total + po_recv[k].astype(F32)
            gro_ref[...] = total
        lane = lax.broadcasted_iota(jnp.int32, (CHUNK, 128), 1)
        q, k, v, gcol, bcol, dz = _dn_block_args(gc, q_ref, k_ref, v_ref, gb_ref, dz_ref)
        dy = jnp.stack([dy_ref[cc * CHUNK:(cc + 1) * CHUNK, h * DN_HEAD_DIM:(h + 1) * DN_HEAD_DIM]
                        for cc in range(gc) for h in range(DN_HEADS)], axis=0)
        _, vjp = jax.vjp(_dn_block, q, k, v, gcol, bcol, ss_ref[0], dz, nw_ref[...])
        dq, dk, dv, dg, db, dst, ddz, dnw = vjp((dy, dstate[...]))
        dstate[...] = dst
        dnw_ref[...] += dnw
        for cc in range(gc):
            r = slice(cc * CHUNK, (cc + 1) * CHUNK)
            dgb = jnp.zeros((CHUNK, 128), F32)
            for h in range(DN_HEADS):
                b = cc * DN_HEADS + h
                cols = slice(h * DN_HEAD_DIM, (h + 1) * DN_HEAD_DIM)
                for ref, val in zip((dq_ref, dk_ref, dv_ref, ddz_ref), (dq, dk, dv, ddz)):
                    ref[r, cols] = val[b]
                dgb = dgb + jnp.where(lane == h, db[b], 0.0) + jnp.where(lane == DN_HEADS + h, dg[b], 0.0)
            dgb_ref[r, :] = dgb

    rev = lambda i: (nstep - 1 - i, 0)
    tile = pl.BlockSpec((rows, D_DN), rev)
    lanes = pl.BlockSpec((rows, 128), rev)
    return pl.pallas_call(
        body, name="dn_scan_bwd", grid=(nstep,),
        out_shape=(jax.ShapeDtypeStruct((s, D_DN), F32),) * 3
        + (jax.ShapeDtypeStruct((s, 128), F32), jax.ShapeDtypeStruct((s, D_DN), F32),
           jax.ShapeDtypeStruct((1, 128), F32), jax.ShapeDtypeStruct(p_out.shape[1:], F32)),
        in_specs=[tile, tile, tile, lanes, pl.BlockSpec((rows, D_DN), lambda i: (nstep - 1 - i, 5)),
                  pl.BlockSpec((1, 128), lambda i: (0, 0)),
                  pl.BlockSpec((1, DN_HEADS, DN_HEAD_DIM, DN_HEAD_DIM), lambda i: (nstep - 1 - i, 0, 0, 0)), tile,
                  pl.BlockSpec(memory_space=pltpu.VMEM)],
        out_specs=(tile, tile, tile, lanes, tile, pl.BlockSpec((1, 128), lambda i: (0, 0)),
                   pl.BlockSpec(memory_space=pltpu.VMEM)),
        scratch_shapes=[pltpu.VMEM((DN_HEADS, DN_HEAD_DIM, DN_HEAD_DIM), F32),
                        pltpu.VMEM(p_out.shape, BF16), pltpu.VMEM((N_DEV - 1,) + p_out.shape[1:], BF16),
                        pltpu.SemaphoreType.DMA((7,)), pltpu.SemaphoreType.DMA((7,))],
        compiler_params=_cp(("arbitrary",)),
    )(qn, kn, vv, gb, proj_main, dn_norm_w, states, dy_dn, p_out)


def _dn_pre_bwd1(proj_main, proj_ba, conv_full, alog_lane, dtb_lane, dqn, dkn, dvv, dgb, tp):
    s = proj_main.shape[0]

    def body(q_ref, k_ref, v_ref, qp_ref, kp_ref, vp_ref, cw_ref, ba_ref, al_ref, db_ref,
             dqn_ref, dkn_ref, dvv_ref, dgb_ref,
             dcq_ref, dck_ref, dcv_ref, dba_ref, dcw_ref, dal_ref, ddb_ref):
        i = pl.program_id(0)

        @pl.when(i == 0)
        def _():
            dcw_ref[...] = jnp.zeros_like(dcw_ref)
            dal_ref[...] = jnp.zeros_like(dal_ref)
            ddb_ref[...] = jnp.zeros_like(ddb_ref)
        curs = (q_ref[...], k_ref[...], v_ref[...])
        tails = (_prev_tail(qp_ref, i), _prev_tail(kp_ref, i), _prev_tail(vp_ref, i))
        ys = [_conv_fwd(curs[j], tails[j], cw_ref[:, j * D_DN:(j + 1) * D_DN]) for j in range(3)]
        _, vjp = jax.vjp(_post_conv, *ys)
        dys = vjp((dqn_ref[...], dkn_ref[...], dvv_ref[...]))
        for j, (dy, out) in enumerate(zip(dys, (dcq_ref, dck_ref, dcv_ref))):
            out[...] = dy
            for sft in range(CONV_WIDTH):
                xs = curs[j] if sft == 0 else _shift_down(curs[j], tails[j], sft)
                row = CONV_WIDTH - 1 - sft
                dcw_ref[row:row + 1, j * D_DN:(j + 1) * D_DN] += jnp.sum(dy * xs, axis=0, keepdims=True)
        _, gvjp = jax.vjp(_gates, ba_ref[...], al_ref[...], db_ref[...])
        dba, dal, ddb = gvjp(dgb_ref[...])
        dba_ref[...] = dba
        dal_ref[...] += dal
        ddb_ref[...] += ddb

    cur, prev = _qkv_specs(tp, 0), _qkv_specs(tp, -1)
    row = pl.BlockSpec((1, 128), lambda i: (0, 0))
    tile = pl.BlockSpec((tp, D_DN), lambda i: (i, 0))
    lanes = pl.BlockSpec((tp, 128), lambda i: (i, 0))
    cw = pl.BlockSpec((CONV_WIDTH, 3 * D_DN), lambda i: (0, 0))
    return pl.pallas_call(
        body, name="dn_pre_bwd1", grid=(s // tp,),
        out_shape=(jax.ShapeDtypeStruct((s, D_DN), F32),) * 3
        + (jax.ShapeDtypeStruct((s, 128), F32), jax.ShapeDtypeStruct((CONV_WIDTH, 3 * D_DN), F32),
           jax.ShapeDtypeStruct((1, 128), F32), jax.ShapeDtypeStruct((1, 128), F32)),
        in_specs=[cur(2), cur(3), cur(4), prev(2), prev(3), prev(4), cw, lanes, row, row, tile, tile, tile, lanes],
        out_specs=(tile, tile, tile, lanes, cw, row, row),
        compiler_params=_cp(("arbitrary",)),
    )(proj_main, proj_main, proj_main, proj_main, proj_main, proj_main, conv_full, proj_ba, alog_lane, dtb_lane,
      dqn, dkn, dvv, dgb)


def _pool_bwd1(proj_main, pool_w, pool_scale, dyp, tp):
    s = proj_main.shape[0]

    def body(u_ref, up_ref, z_ref, pw_ref, ps_ref, dy_ref, dz_ref, dwin_ref, dpw_ref, dps_ref):
        i = pl.program_id(0)

        @pl.when(i == 0)
        def _():
            dpw_ref[...] = jnp.zeros_like(dpw_ref)
            dps_ref[...] = jnp.zeros_like(dps_ref)
        u = u_ref[...]
        z = z_ref[...]
        dy = dy_ref[...]
        ps = ps_ref[...]
        mixes = _pool_mix(u, _prev_tail(up_ref, i), i, tp)
        sg = jax.nn.sigmoid(z)
        sz = z * sg
        dsz = sg * (1.0 + z * (1.0 - sg))
        for gi, w in enumerate(POOL_WINDOWS):
            cols = slice(gi * POOL_GROUP, (gi + 1) * POOL_GROUP)
            mixw = _dot_bf(mixes[gi], pw_ref[gi])
            dmixw = dy[:, cols] * ps[:, cols] * sz[:, cols]
            dps_ref[:, cols] += jnp.sum(dy[:, cols] * mixw * sz[:, cols], axis=0, keepdims=True)
            dz_ref[:, cols] = dy[:, cols] * mixw * ps[:, cols] * dsz[:, cols]
            dpw_ref[gi] += _dot_tn_bf(mixes[gi], dmixw)
            dmix = _dot_nt_bf(dmixw, pw_ref[gi])
            dwin_ref[:, cols] = dmix / _pool_counts(i, tp, w)

    tile = pl.BlockSpec((tp, D_POOL), lambda i: (i, 0))
    pw = pl.BlockSpec((4, POOL_GROUP, POOL_GROUP), lambda i: (0, 0, 0))
    ps = pl.BlockSpec((1, D_POOL), lambda i: (0, 0))
    return pl.pallas_call(
        body, name="pool_bwd1", grid=(s // tp,),
        out_shape=(jax.ShapeDtypeStruct((s, D_POOL), F32), jax.ShapeDtypeStruct((s, D_POOL), F32),
                   jax.ShapeDtypeStruct((4, POOL_GROUP, POOL_GROUP), F32), jax.ShapeDtypeStruct((1, D_POOL), F32)),
        in_specs=[tile, _prev_halo_spec(tp, D_POOL, 0),
                  pl.BlockSpec((tp, D_POOL), lambda i: (i, 1)), pw, ps, tile],
        out_specs=(tile, tile, pw, ps),
        compiler_params=_cp(("arbitrary",)),
    )(proj_main, proj_main, proj_main, pool_w, pool_scale, dyp)


def _back(proj_main, proj_ba, dyp, dqn, dkn, dvv, dgb, ddz, x2, dh, norm_w, pool_w, pool_scale, conv_full,
          alog_lane, dtb_lane, wt_full, tm):
    s = x2.shape[0]
    nstep = s // tm
    per = tm // HALO

    def body(u_ref, z_ref, q_ref, k_ref, v_ref, up_ref, qp_ref, kp_ref, vp_ref, ba_ref,
             dyp_ref, dqn_ref, dkn_ref, dvv_ref, dgb_ref, ddz_ref, x_ref, dh_ref,
             nw_ref, pw_ref, ps_ref, cw_ref, al_ref, db_ref, wt_hbm,
             gx_ref, p_hbm, gnw_ref, dpw_ref, dps_ref, dcw_ref, dal_ref, ddb_ref,
             wt_vmem, acc, blk, head_dc, head_dw, sem, osem):
        j = pl.program_id(0)
        i = nstep - 1 - j

        @pl.when(j == 0)
        def _():
            cp = pltpu.make_async_copy(wt_hbm, wt_vmem, sem)
            cp.start()
            acc[...] = jnp.zeros_like(acc)
            for ref in (gnw_ref, dpw_ref, dps_ref, dcw_ref, dal_ref, ddb_ref, head_dc, head_dw):
                ref[...] = jnp.zeros_like(ref)
            cp.wait()

        u = u_ref[...]
        z = z_ref[...]
        dy = dyp_ref[...]
        ps = ps_ref[...]
        mixes = _pool_mix(u, _prev_tail(up_ref, i), i, tm)
        sg = jax.nn.sigmoid(z)
        sz = z * sg
        dsz = sg * (1.0 + z * (1.0 - sg))
        dzs, dwins = [], []
        for gi, w in enumerate(POOL_WINDOWS):
            cols = slice(gi * POOL_GROUP, (gi + 1) * POOL_GROUP)
            mixw = _dot_bf(mixes[gi], pw_ref[gi])
            dmixw = dy[:, cols] * ps[:, cols] * sz[:, cols]
            dps_ref[:, cols] += jnp.sum(dy[:, cols] * mixw * sz[:, cols], axis=0, keepdims=True)
            dzs.append(dy[:, cols] * mixw * ps[:, cols] * dsz[:, cols])
            dpw_ref[gi] += _dot_tn_bf(mixes[gi], dmixw)
            dwins.append(_dot_nt_bf(dmixw, pw_ref[gi]) / _pool_counts(i, tm, w))
        dzp = jnp.concatenate(dzs, axis=1)
        dw = jnp.concatenate(dwins, axis=1)
        ext = jnp.concatenate([dw, head_dw[...]], axis=0)
        m = ext.shape[0]
        a2 = ext + pltpu.roll(ext, m - 1, 0)
        a4 = a2 + pltpu.roll(a2, m - 2, 0)
        a8 = a4 + pltpu.roll(a4, m - 4, 0)
        a16 = a8 + pltpu.roll(a8, m - 8, 0)
        dup = jnp.concatenate(
            [acc_w[:tm, gi * POOL_GROUP:(gi + 1) * POOL_GROUP]
             - dw[:, gi * POOL_GROUP:(gi + 1) * POOL_GROUP] * _pool_counts(i, tm, w)
             for gi, (w, acc_w) in enumerate(zip(POOL_WINDOWS, (a2, a4, a8, a16)))], axis=1)
        head_dw[...] = dw[:HALO, :]

        curs = (q_ref[...], k_ref[...], v_ref[...])
        tails = (_prev_tail(qp_ref, i), _prev_tail(kp_ref, i), _prev_tail(vp_ref, i))
        ys = [_conv_fwd(curs[c], tails[c], cw_ref[:, c * D_DN:(c + 1) * D_DN]) for c in range(3)]
        _, vjp = jax.vjp(_post_conv, *ys)
        dys = vjp((dqn_ref[...], dkn_ref[...], dvv_ref[...]))
        dxs = []
        for c, dyc in enumerate(dys):
            cols = slice(c * D_DN, (c + 1) * D_DN)
            w4 = cw_ref[:, cols]
            for sft in range(CONV_WIDTH):
                xs = curs[c] if sft == 0 else _shift_down(curs[c], tails[c], sft)
                row = CONV_WIDTH - 1 - sft
                dcw_ref[row:row + 1, cols] += jnp.sum(dyc * xs, axis=0, keepdims=True)
            head = head_dc[:, cols]
            dx = dyc * w4[CONV_WIDTH - 1:CONV_WIDTH, :]
            for sft in range(1, CONV_WIDTH):
                dx = dx + _shift_up(dyc, head, sft) * w4[CONV_WIDTH - 1 - sft:CONV_WIDTH - sft, :]
            dxs.append(dx)
            head_dc[:, cols] = dyc[:HALO, :]
        _, gvjp = jax.vjp(_gates, ba_ref[...], al_ref[...], db_ref[...])
        dba, dal, ddb = gvjp(dgb_ref[...])
        dal_ref[...] += dal
        ddb_ref[...] += ddb

        dbab = dba.astype(BF16)
        xhat, r = _rms_hat(x_ref[...])
        nw = nw_ref[...]
        n = (xhat * nw).astype(BF16)
        acc[D_MAIN:, :] += _dot_tn_bf(dbab, n)
        dn = jnp.dot(dbab, wt_vmem[D_MAIN:, :], preferred_element_type=F32)
        for cb, d in enumerate((dup, dzp, dxs[0], dxs[1], dxs[2], ddz_ref[...])):
            rows = slice(cb * D_POOL, (cb + 1) * D_POOL)
            dpart = d.astype(BF16)
            acc[rows, :] += _dot_tn_bf(dpart, n)
            dn = dn + jnp.dot(dpart, wt_vmem[rows, :], preferred_element_type=F32)
        gnw_ref[...] += jnp.sum(dn * xhat, axis=0, keepdims=True)
        dxh = dn * nw
        gx_ref[...] = dh_ref[...] + r * (dxh - xhat * jnp.mean(dxh * xhat, axis=-1, keepdims=True))

        @pl.when(j == nstep - 1)
        def _():
            def out(d):
                return pltpu.make_async_copy(blk.at[d % 2], p_hbm.at[d], osem.at[d % 2])
            for d in range(N_DEV):
                if d >= 2:
                    out(d - 2).wait()
                blk[d % 2] = acc[W_IN_SHARD * d:W_IN_SHARD * (d + 1), :]
                out(d).start()
            out(N_DEV - 2).wait()
            out(N_DEV - 1).wait()

    def col(c):
        return pl.BlockSpec((tm, D_POOL), lambda j: (nstep - 1 - j, c))

    def halo(c):
        return pl.BlockSpec((HALO, D_POOL), lambda j: (jnp.maximum((nstep - 1 - j) * per - 1, 0), c))

    rev = lambda j: (nstep - 1 - j, 0)
    part = pl.BlockSpec((tm, D_POOL), rev)
    lanes = pl.BlockSpec((tm, 128), rev)
    full = pl.BlockSpec((tm, D_MODEL), rev)
    row = pl.BlockSpec((1, D_MODEL), lambda j: (0, 0))
    lrow = pl.BlockSpec((1, 128), lambda j: (0, 0))
    pw = pl.BlockSpec((4, POOL_GROUP, POOL_GROUP), lambda j: (0, 0, 0))
    psp = pl.BlockSpec((1, D_POOL), lambda j: (0, 0))
    cw = pl.BlockSpec((CONV_WIDTH, 3 * D_DN), lambda j: (0, 0))
    return pl.pallas_call(
        body, name="back", grid=(nstep,),
        out_shape=(jax.ShapeDtypeStruct((s, D_MODEL), F32),
                   jax.ShapeDtypeStruct((N_DEV, W_IN_SHARD, D_MODEL), F32), jax.ShapeDtypeStruct((1, D_MODEL), F32),
                   jax.ShapeDtypeStruct((4, POOL_GROUP, POOL_GROUP), F32), jax.ShapeDtypeStruct((1, D_POOL), F32),
                   jax.ShapeDtypeStruct((CONV_WIDTH, 3 * D_DN), F32),
                   jax.ShapeDtypeStruct((1, 128), F32), jax.ShapeDtypeStruct((1, 128), F32)),
        in_specs=[col(0), col(1), col(2), col(3), col(4), halo(0), halo(2), halo(3), halo(4), lanes,
                  part, part, part, part, lanes, part, full, full,
                  row, pw, psp, cw, lrow, lrow, pl.BlockSpec(memory_space=pl.ANY)],
        out_specs=(full, pl.BlockSpec(memory_space=pl.ANY), row, pw, psp, cw, lrow, lrow),
        scratch_shapes=[pltpu.VMEM((D_IN_PAD, D_MODEL), BF16), pltpu.VMEM((D_IN_PAD, D_MODEL), F32),
                        pltpu.VMEM((2, W_IN_SHARD, D_MODEL), F32),
                        pltpu.VMEM((HALO, 3 * D_DN), F32), pltpu.VMEM((HALO, D_POOL), F32),
                        pltpu.SemaphoreType.DMA, pltpu.SemaphoreType.DMA((2,))],
        compiler_params=_cp(("arbitrary",)),
    )(proj_main, proj_main, proj_main, proj_main, proj_main, proj_main, proj_main, proj_main, proj_main, proj_ba,
      dyp, dqn, dkn, dvv, dgb, ddz, x2, dh, norm_w, pool_w, pool_scale, conv_full, alog_lane, dtb_lane, wt_full)


def _in_proj_bwd(dwin, dzp, dcq, dck, dcv, ddz, dba, x2, dh, norm_w, wt_full, conv_full, tm):
    s = x2.shape[0]
    nstep = s // tm

    def body(dwin_ref, dzp_ref, dcq_ref, dck_ref, dcv_ref, ddz_ref, dba_ref, x_ref, dh_ref, nw_ref, cw_ref, wt_hbm,
             gx_ref, p_hbm, gnw_ref, wt_vmem, acc, blk, head_dc, head_dw, sem, osem):
        j = pl.program_id(0)
        i = nstep - 1 - j

        @pl.when(j == 0)
        def _():
            cp = pltpu.make_async_copy(wt_hbm, wt_vmem, sem)
            cp.start()
            acc[...] = jnp.zeros_like(acc)
            gnw_ref[...] = jnp.zeros_like(gnw_ref)
            head_dc[...] = jnp.zeros_like(head_dc)
            head_dw[...] = jnp.zeros_like(head_dw)
            cp.wait()
        dw = dwin_ref[...]
        ext = jnp.concatenate([dw, head_dw[...]], axis=0)
        m = ext.shape[0]
        a2 = ext + pltpu.roll(ext, m - 1, 0)
        a4 = a2 + pltpu.roll(a2, m - 2, 0)
        a8 = a4 + pltpu.roll(a4, m - 4, 0)
        a16 = a8 + pltpu.roll(a8, m - 8, 0)
        dup = jnp.concatenate(
            [acc_w[:tm, gi * POOL_GROUP:(gi + 1) * POOL_GROUP]
             - dw[:, gi * POOL_GROUP:(gi + 1) * POOL_GROUP] * _pool_counts(i, tm, w)
             for gi, (w, acc_w) in enumerate(zip(POOL_WINDOWS, (a2, a4, a8, a16)))], axis=1)
        head_dw[...] = dw[:HALO, :]
        dxs = []
        for c, ref in enumerate((dcq_ref, dck_ref, dcv_ref)):
            cols = slice(c * D_DN, (c + 1) * D_DN)
            w4 = cw_ref[:, cols]
            dy = ref[...]
            head = head_dc[:, cols]
            dx = dy * w4[CONV_WIDTH - 1:CONV_WIDTH, :]
            for sft in range(1, CONV_WIDTH):
                dx = dx + _shift_up(dy, head, sft) * w4[CONV_WIDTH - 1 - sft:CONV_WIDTH - sft, :]
            dxs.append(dx)
            head_dc[:, cols] = dy[:HALO, :]
        dbab = dba_ref[...].astype(BF16)
        xhat, r = _rms_hat(x_ref[...])
        nw = nw_ref[...]
        n = (xhat * nw).astype(BF16)
        acc[D_MAIN:, :] += _dot_tn_bf(dbab, n)
        dn = jnp.dot(dbab, wt_vmem[D_MAIN:, :], preferred_element_type=F32)
        for cb, d in enumerate((dup, dzp_ref[...], dxs[0], dxs[1], dxs[2], ddz_ref[...])):
            rows = slice(cb * D_POOL, (cb + 1) * D_POOL)
            dpart = d.astype(BF16)
            acc[rows, :] += _dot_tn_bf(dpart, n)
            dn = dn + jnp.dot(dpart, wt_vmem[rows, :], preferred_element_type=F32)
        gnw_ref[...] += jnp.sum(dn * xhat, axis=0, keepdims=True)
        dxh = dn * nw
        gx_ref[...] = dh_ref[...] + r * (dxh - xhat * jnp.mean(dxh * xhat, axis=-1, keepdims=True))

        @pl.when(j == nstep - 1)
        def _():
            def out(d):
                return pltpu.make_async_copy(blk.at[d % 2], p_hbm.at[d], osem.at[d % 2])
            for d in range(N_DEV):
                if d >= 2:
                    out(d - 2).wait()
                blk[d % 2] = acc[W_IN_SHARD * d:W_IN_SHARD * (d + 1), :]
                out(d).start()
            out(N_DEV - 2).wait()
            out(N_DEV - 1).wait()

    rev = lambda j: (nstep - 1 - j, 0)
    part = pl.BlockSpec((tm, D_POOL), rev)
    full = pl.BlockSpec((tm, D_MODEL), rev)
    row = pl.BlockSpec((1, D_MODEL), lambda j: (0, 0))
    return pl.pallas_call(
        body, name="in_proj_bwd", grid=(nstep,),
        out_shape=(jax.ShapeDtypeStruct((s, D_MODEL), F32),
                   jax.ShapeDtypeStruct((N_DEV, W_IN_SHARD, D_MODEL), F32), jax.ShapeDtypeStruct((1, D_MODEL), F32)),
        in_specs=[part] * 6 + [pl.BlockSpec((tm, 128), rev), full, full, row,
                               pl.BlockSpec((CONV_WIDTH, 3 * D_DN), lambda j: (0, 0)),
                               pl.BlockSpec(memory_space=pl.ANY)],
        out_specs=(full, pl.BlockSpec(memory_space=pl.ANY), row),
        scratch_shapes=[pltpu.VMEM((D_IN_PAD, D_MODEL), BF16), pltpu.VMEM((D_IN_PAD, D_MODEL), F32),
                        pltpu.VMEM((2, W_IN_SHARD, D_MODEL), F32),
                        pltpu.VMEM((HALO, 3 * D_DN), F32), pltpu.VMEM((HALO, D_POOL), F32),
                        pltpu.SemaphoreType.DMA, pltpu.SemaphoreType.DMA((2,))],
        compiler_params=_cp(("arbitrary",)),
    )(dwin, dzp, dcq, dck, dcv, ddz, dba, x2, dh, norm_w, conv_full, wt_full)


def _adamw_math(w, g, m, v):
    m = ADAM_B1 * m + (1.0 - ADAM_B1) * g
    v = ADAM_B2 * v + (1.0 - ADAM_B2) * (g * g)
    m_hat = m / (1.0 - ADAM_B1 ** ADAM_STEP)
    v_hat = v / (1.0 - ADAM_B2 ** ADAM_STEP)
    delta = -ADAM_LR * (m_hat / (jnp.sqrt(v_hat) + ADAM_EPS) + ADAM_WD * w)
    return delta, m, v


def _adamw_sharded(params):
    k = len(params)

    def body(*refs):
        ins, outs = refs[:4 * k], refs[4 * k:]
        for p in range(k):
            w_ref, g_ref, m_ref, v_ref = ins[4 * p:4 * p + 4]
            go_ref = outs[4 * p]
            if g_ref.shape == w_ref.shape:
                go_ref[...] = g_ref[...]
            else:
                for j in range(FLAT_ROWS):
                    go_ref[pl.ds(j, W_IN_SHARD, stride=FLAT_ROWS), :] = g_ref[:, 128 * j:128 * (j + 1)]
            d, nm, nv = _adamw_math(w_ref[...], go_ref[...], m_ref[...], v_ref[...])
            outs[4 * p + 1][...] = d
            outs[4 * p + 2][...] = nm
            outs[4 * p + 3][...] = nv

    flat = [a for p in params for a in p]
    out_shape = tuple(jax.ShapeDtypeStruct(p[0].shape, F32) for p in params for _ in range(4))
    res = pl.pallas_call(body, name="adamw_sharded", out_shape=out_shape, compiler_params=_cp())(*flat)
    return [tuple(res[4 * p:4 * p + 4]) for p in range(k)]


def _adamw_replicated(gath_a, gath_b, pool, rows):
    nrow = len(rows)

    def body(*refs):
        ga_ref, gb_ref = refs[:2]
        ins = refs[2:2 + 3 * (nrow + 1)]
        outs = refs[2 + 3 * (nrow + 1):]

        def total(ref):
            g = ref[0]
            for d in range(1, N_DEV):
                g = g + ref[d]
            return g

        def update(g, wmv, o):
            w, m, v = (r[...] for r in wmv)
            dl, nm, nv = _adamw_math(w, g, m, v)
            o[0][...] = g
            o[1][...] = dl
            o[2][...] = nm
            o[3][...] = nv

        update(total(ga_ref), ins[:3], outs[:4])
        gb = total(gb_ref)
        for r in range(nrow):
            n = ins[3 * (r + 1)].shape[1]
            update(gb[r:r + 1, :n], ins[3 * (r + 1):3 * (r + 2)], outs[4 * (r + 1):4 * (r + 2)])
        outs[4 * (nrow + 1)][...] = gb[nrow:nrow + 1, 0:1]

    flat = list(pool) + [a for wmv in rows for a in wmv]
    out_shape = ((jax.ShapeDtypeStruct(pool[0].shape, F32),) * 4
                 + tuple(jax.ShapeDtypeStruct(wmv[0].shape, F32) for wmv in rows for _ in range(4))
                 + (jax.ShapeDtypeStruct((1, 1), F32),))
    res = pl.pallas_call(body, name="adamw_replicated", out_shape=out_shape, compiler_params=_cp())(
        gath_a, gath_b, *flat)
    return [res[4 * k:4 * k + 4] for k in range(nrow + 1)], res[-1]


_ROW_ORDER = ("norm_w", "final_norm_w", "pool_scale", "dn_norm_w", "a_log", "dt_bias")


def _pack_rows(vectors):
    out = [jnp.pad(v.reshape(-1), (0, D_MODEL - v.size)) for v in vectors]
    out += [jnp.zeros((D_MODEL,), F32)] * (8 - len(out))
    return jnp.stack(out, axis=0)


def _lane_row(vec4, start):
    return jnp.pad(vec4.reshape(-1), (start, 128 - start - vec4.size)).reshape(1, 128)


def kernel(x, norm_w, w_in, pool_w, pool_scale, conv_w, a_log, dt_bias, dn_norm_w, w_out, final_norm_w, loss_target, m_norm_w, m_w_in, m_pool_w, m_pool_scale, m_conv_w, m_a_log, m_dt_bias, m_dn_norm_w, m_w_out, m_final_norm_w, v_norm_w, v_w_in, v_pool_w, v_pool_scale, v_conv_w, v_a_log, v_dt_bias, v_dn_norm_w, v_w_out, v_final_norm_w):
    s = x.shape[1]
    tm = min(512, s)
    tmb = min(256, s)
    tp = min(512, s)
    x2 = x[0]
    tgt = loss_target[0]
    def to_flat(a):
        return a[0].reshape(FLAT_ROWS, 128, W_IN_SHARD).transpose(2, 0, 1).reshape(W_IN_SHARD * FLAT_ROWS, 128)

    def from_flat(f):
        return f.reshape(W_IN_SHARD, FLAT_ROWS, 128).transpose(1, 2, 0).reshape(1, D_MODEL, W_IN_SHARD)

    wf, m_wf, v_wf = to_flat(w_in), to_flat(m_w_in), to_flat(v_w_in)

    g_in, g_conv = _gather_weights(wf, conv_w[0])
    conv_full = g_conv.transpose(1, 0, 2).reshape(CONV_WIDTH, 3 * D_DN)
    alog_lane = _lane_row(a_log, DN_HEADS)
    dtb_lane = _lane_row(dt_bias, DN_HEADS)
    fnw = final_norm_w.reshape(1, D_MODEL)

    proj_main, proj_ba, y_pool, qn, kn, vv, gb, g_out, wt_full = _front(
        x2, norm_w, g_in, pool_w[0], pool_scale, conv_full, alog_lane, dtb_lane, w_out[0], tm)
    w_out_full = g_out.reshape(D_MODEL, D_MODEL)
    y_dn, states = _dn_scan_fwd(qn, kn, vv, gb, proj_main, dn_norm_w, DN_CHUNKS_PER_STEP)

    dh, dyp, dyd, g_wout, g_fnw, loss_part = _out_proj_loss(y_pool, y_dn, x2, tgt, w_out_full, fnw, tm)
    p_out = g_wout.reshape(N_DEV, D_MODEL // N_DEV, D_MODEL)
    dqn, dkn, dvv, dgb, ddz, g_dnw, gr_out = _dn_scan_bwd(qn, kn, vv, gb, proj_main, dn_norm_w, states, dyd, p_out,
                                                          DN_CHUNKS_PER_STEP)
    grad_x2, p_in, g_nw, g_pw, g_ps, g_conv_full, g_al, g_db = _back(
        proj_main, proj_ba, dyp, dqn, dkn, dvv, dgb, ddz, x2, dh, norm_w, pool_w[0], pool_scale, conv_full,
        alog_lane, dtb_lane, wt_full, tmb)

    p_conv = g_conv_full.reshape(CONV_WIDTH, N_DEV, 3 * D_DN // N_DEV).transpose(1, 0, 2)
    pack_a = g_pw.reshape(4 * POOL_GROUP, POOL_GROUP)
    pack_b = _pack_rows([g_nw, g_fnw, g_ps, g_dnw, g_al[0, DN_HEADS:2 * DN_HEADS], g_db[0, DN_HEADS:2 * DN_HEADS],
                         loss_part[0, :1]])
    gr_in, gr_conv, gath_a, gath_b = _reduce_grads((p_in, p_conv), pack_a, pack_b)

    r_in, r_out, r_conv = _adamw_sharded([(wf, gr_in, m_wf, v_wf), (w_out[0], gr_out, m_w_out[0], v_w_out[0]),
                                          (conv_w[0], gr_conv, m_conv_w[0], v_conv_w[0])])
    flat = lambda a: a.reshape(4 * POOL_GROUP, POOL_GROUP)
    row = lambda a: a.reshape(1, -1)
    vecs = {"norm_w": (norm_w, m_norm_w, v_norm_w), "final_norm_w": (final_norm_w, m_final_norm_w, v_final_norm_w),
            "pool_scale": (pool_scale, m_pool_scale, v_pool_scale), "dn_norm_w": (dn_norm_w, m_dn_norm_w, v_dn_norm_w),
            "a_log": (a_log, m_a_log, v_a_log), "dt_bias": (dt_bias, m_dt_bias, v_dt_bias)}
    res, loss = _adamw_replicated(gath_a, gath_b, (flat(pool_w), flat(m_pool_w), flat(v_pool_w)),
                                  [tuple(row(a) for a in vecs[nm]) for nm in _ROW_ORDER])
    r_pool = res[0]
    r_vec = dict(zip(_ROW_ORDER, res[1:]))

    def group(k):
        vec = lambda nm: r_vec[nm][k].reshape(vecs[nm][0].shape)
        return (vec("norm_w"), from_flat(r_in[k]), r_pool[k].reshape(pool_w.shape), vec("pool_scale"), r_conv[k][None],
                vec("a_log"), vec("dt_bias"), vec("dn_norm_w"), r_out[k][None], vec("final_norm_w"))

    return (loss[0, 0], grad_x2[None], *group(0), *group(1), *group(2), *group(3))
```

```python
import functools

import jax
import jax.numpy as jnp
from jax import lax
from jax.experimental import pallas as pl
from jax.experimental.pallas import tpu as pltpu

F32 = jnp.float32
BF16 = jnp.bfloat16
HI = lax.Precision.HIGHEST
MESH = pl.DeviceIdType.MESH

D_MODEL = 1024
D_POOL = 512
D_DN = 512
POOL_WINDOWS = (2, 4, 8, 16)
POOL_GROUP = 128
DN_HEADS = 4
DN_HEAD_DIM = 128
CONV_WIDTH = 4
CHUNK = 64
NORM_EPS = 1e-6
D_IN = 3080
D_MAIN = 3072
FLAT_ROWS = D_MODEL // 128
D_IN_PAD = D_MAIN + 128
N_DEV = 8
W_IN_SHARD = D_IN // N_DEV
HALO = 16
DN_CHUNKS_PER_STEP = 8

ADAM_LR = 0.001
ADAM_B1 = 0.9
ADAM_B2 = 0.999
ADAM_EPS = 1e-08
ADAM_WD = 0.01
ADAM_STEP = 10

VMEM_LIMIT = 56 * 1024 * 1024


def _cp(sem=None, vmem=VMEM_LIMIT):
    kw = {"vmem_limit_bytes": vmem}
    if sem is not None:
        kw["dimension_semantics"] = sem
    return pltpu.CompilerParams(**kw)


def _dot_bf(a, b):
    return jnp.dot(a.astype(BF16), b.astype(BF16), preferred_element_type=F32)


def _dot_nt_bf(a, b):
    return lax.dot_general(a.astype(BF16), b.astype(BF16), (((1,), (1,)), ((), ())), preferred_element_type=F32)


def _dot_tn_bf(a, b):
    return lax.dot_general(a.astype(BF16), b.astype(BF16), (((0,), (0,)), ((), ())), preferred_element_type=F32)


def _mm_raw(a, b, ca, cb, prec):
    off = a.ndim - 2
    dn = (((ca + off,), (cb + off,)), ((0,), (0,)) if off else ((), ()))
    if prec == "hi":
        return lax.dot_general(a, b, dn, precision=HI, preferred_element_type=F32)
    ah, bh = a.astype(BF16), b.astype(BF16)
    out = lax.dot_general(ah, bh, dn, preferred_element_type=F32)
    if prec == "x3":
        al = (a - ah.astype(F32)).astype(BF16)
        bl = (b - bh.astype(F32)).astype(BF16)
        out = out + lax.dot_general(ah, bl, dn, preferred_element_type=F32)
        out = out + lax.dot_general(al, bh, dn, preferred_element_type=F32)
    return out


@functools.partial(jax.custom_vjp, nondiff_argnums=(2, 3, 4, 5))
def _mm(a, b, ca, cb, prec, bprec):
    return _mm_raw(a, b, ca, cb, prec)


def _mm_fwd(a, b, ca, cb, prec, bprec):
    return _mm_raw(a, b, ca, cb, prec), (a, b)


def _mm_bwd(ca, cb, prec, bprec, res, dc):
    a, b = res
    da = _mm_raw(dc, b, 1, 1 - cb, bprec) if ca == 1 else _mm_raw(b, dc, 1 - cb, 1, bprec)
    db = _mm_raw(a, dc, 1 - ca, 0, bprec) if cb == 0 else _mm_raw(dc, a, 0, 1 - ca, bprec)
    return da, db


_mm.defvjp(_mm_fwd, _mm_bwd)


@functools.partial(jax.custom_vjp, nondiff_argnums=(1, 2))
def _tri_inv(a, prec, bprec):
    n = a.shape[-1]
    ii = lax.broadcasted_iota(jnp.int32, (n, n), 0)
    jj = lax.broadcasted_iota(jnp.int32, (n, n), 1)
    p = (ii == jj).astype(F32) - a
    b = _mm_raw(a, a, 1, 0, prec)
    for _ in range(4):
        pb = _mm_raw(jnp.concatenate([p, b], axis=-2), b, 1, 0, prec)
        p = p + pb[..., :n, :]
        b = pb[..., n:, :]
    return p + _mm_raw(p, b, 1, 0, prec)


def _tri_inv_fwd(a, prec, bprec):
    t = _tri_inv(a, prec, bprec)
    return t, t


def _tri_inv_bwd(prec, bprec, t, dt):
    return (-_mm_raw(_mm_raw(t, dt, 0, 0, bprec), t, 1, 1, bprec),)


_tri_inv.defvjp(_tri_inv_fwd, _tri_inv_bwd)

@functools.partial(jax.custom_vjp, nondiff_argnums=(3, 4, 5))
def _mm_known(a, b, out, ca, cb, bprec):
    return out


def _mm_known_fwd(a, b, out, ca, cb, bprec):
    return out, (a, b)


def _mm_known_bwd(ca, cb, bprec, res, dc):
    return _mm_bwd(ca, cb, None, bprec, res, dc) + (jnp.zeros_like(dc),)


_mm_known.defvjp(_mm_known_fwd, _mm_known_bwd)


@jax.custom_vjp
def _use_known(x, known):
    return known


_use_known.defvjp(lambda x, known: (known, None), lambda _, g: (g, jnp.zeros_like(g)))


@functools.partial(jax.custom_vjp, nondiff_argnums=(2,))
def _tri_inv_known(a, t, bprec):
    return t


def _tri_inv_known_fwd(a, t, bprec):
    return t, t


def _tri_inv_known_bwd(bprec, t, dt):
    return _tri_inv_bwd(None, bprec, t, dt) + (jnp.zeros_like(dt),)


_tri_inv_known.defvjp(_tri_inv_known_fwd, _tri_inv_known_bwd)

_DN_PREC = {"akq": ("bf16", "bf16"), "inv": ("bf16", "bf16"), "uw": ("bf16", "bf16"), "ws": ("bf16", "bf16"),
            "ov": ("bf16", "bf16"), "st": ("bf16", "bf16")}


def _silu(x):
    return x * jax.nn.sigmoid(x)


def _softplus(x):
    pos = x > 0.0
    return jnp.where(pos, x, 0.0) + jnp.log1p(jnp.exp(jnp.where(pos, -x, x)))


def _mesh_pos():
    return lax.axis_index("x"), lax.axis_index("y"), lax.axis_index("c")


def _dev_index(x, y, c):
    return 4 * x + 2 * y + c


def _all_gather_blocks(outs, send_sems, recv_sems):
    x, y, c = _mesh_pos()
    me = (x, y, c)
    sibling = (x, y, 1 - c)
    chips = [(1 - x, y), (x, 1 - y), (1 - x, 1 - y)]

    def copy(a, k, block, to):
        rows = outs[a].at[_dev_index(*block)]
        return pltpu.make_async_remote_copy(src_ref=rows, dst_ref=rows, send_sem=send_sems.at[a, k],
                                            recv_sem=recv_sems.at[a, k], device_id=to, device_id_type=MESH)

    n = len(outs)
    first = []
    for a in range(n):
        first.append(copy(a, 0, me, sibling))
        for j, chip in enumerate(chips):
            first.append(copy(a, 1 + j, me, (*chip, c)))
    for cp in first:
        cp.start()
    passed = []
    for j, chip in enumerate(chips):
        for a in range(n):
            copy(a, 1 + j, (*chip, c), me).wait_recv()
            fwd = copy(a, 4 + j, (*chip, c), sibling)
            fwd.start()
            passed.append(fwd)
    for a in range(n):
        copy(a, 0, sibling, me).wait_recv()
        for j, chip in enumerate(chips):
            copy(a, 4 + j, (*chip, 1 - c), me).wait_recv()
    for cp in first + passed:
        cp.wait_send()


def _peer_relations():
    x, y, c = _mesh_pos()
    flips = [(fx, fy, fc) for fx in (0, 1) for fy in (0, 1) for fc in (0, 1)][1:]
    peers = [(1 - x if fx else x, 1 - y if fy else y, 1 - c if fc else c) for fx, fy, fc in flips]
    return (x, y, c), peers


def _direct_gather_start(out_ref, send_sems, recv_sems):
    me, peers = _peer_relations()
    rows = out_ref.at[_dev_index(*me)]
    for k, peer in enumerate(peers):
        pltpu.make_async_remote_copy(src_ref=rows, dst_ref=rows, send_sem=send_sems.at[k], recv_sem=recv_sems.at[k],
                                     device_id=peer, device_id_type=MESH).start()


def _direct_gather_wait(out_ref, send_sems, recv_sems):
    me, peers = _peer_relations()
    for k, peer in enumerate(peers):
        rows = out_ref.at[_dev_index(*peer)]
        cp = pltpu.make_async_remote_copy(src_ref=rows, dst_ref=rows, send_sem=send_sems.at[k],
                                          recv_sem=recv_sems.at[k], device_id=peer, device_id_type=MESH)
        cp.wait_recv()
        cp.wait_send()


def _direct_scatter_start(send_ref, recv_ref, send_sems, recv_sems):
    me, peers = _peer_relations()
    for k, peer in enumerate(peers):
        pltpu.make_async_remote_copy(src_ref=send_ref.at[_dev_index(*peer)], dst_ref=recv_ref.at[k],
                                     send_sem=send_sems.at[k], recv_sem=recv_sems.at[k],
                                     device_id=peer, device_id_type=MESH).start()


def _direct_scatter_wait(send_ref, recv_ref, send_sems, recv_sems):
    me, peers = _peer_relations()
    for k, peer in enumerate(peers):
        cp = pltpu.make_async_remote_copy(src_ref=send_ref.at[_dev_index(*peer)], dst_ref=recv_ref.at[k],
                                          send_sem=send_sems.at[k], recv_sem=recv_sems.at[k],
                                          device_id=peer, device_id_type=MESH)
        cp.wait_recv()
        cp.wait_send()


def _gather_weights(w_in_flat, conv_blk):
    def body(win_ref, conv_ref, gin_ref, gconv_ref, send_sems, recv_sems):
        x, y, c = _mesh_pos()
        me = _dev_index(x, y, c)
        for j in range(FLAT_ROWS):
            gin_ref[me, :, 128 * j:128 * (j + 1)] = win_ref[pl.ds(j, W_IN_SHARD, stride=FLAT_ROWS), :].astype(BF16)
        gconv_ref[me] = conv_ref[...]
        _all_gather_blocks((gin_ref, gconv_ref), send_sems, recv_sems)

    vm = pl.BlockSpec(memory_space=pltpu.VMEM)
    return pl.pallas_call(
        body, name="gather_weights",
        out_shape=(jax.ShapeDtypeStruct((N_DEV, W_IN_SHARD, D_MODEL), BF16),
                   jax.ShapeDtypeStruct((N_DEV,) + conv_blk.shape, F32)),
        in_specs=[vm, vm], out_specs=(vm, vm),
        scratch_shapes=[pltpu.SemaphoreType.DMA((2, 7)), pltpu.SemaphoreType.DMA((2, 7))],
        compiler_params=_cp(),
    )(w_in_flat, conv_blk)


def _reduce_grads(big, pack_a, pack_b):
    nb = len(big)

    def body(*refs):
        srcs, (pa_ref, pb_ref), outs, (ga_ref, gb_ref) = (
            refs[:nb], refs[nb:nb + 2], refs[nb + 2:2 * nb + 2], refs[2 * nb + 2:2 * nb + 4])
        scr = refs[2 * nb + 4:]
        r1s, r2s, sbs, sts = (scr[k * nb:(k + 1) * nb] for k in range(4))
        s1_send, s1_recv, s2_send, s2_recv, ag_send, ag_recv, st_sem = scr[4 * nb:]
        x, y, c = _mesh_pos()
        me = (x, y, c)
        sibling = (x, y, 1 - c)
        rel = [(x, y), (1 - x, y), (x, 1 - y), (1 - x, 1 - y)]

        ga_ref[_dev_index(*me)] = pa_ref[...]
        gb_ref[_dev_index(*me)] = pb_ref[...]

        def p1(a, r, to):
            return pltpu.make_async_remote_copy(
                src_ref=srcs[a].at[_dev_index(*rel[r], 1 - c)], dst_ref=r1s[a].at[r],
                send_sem=s1_send.at[a, r], recv_sem=s1_recv.at[a, r], device_id=to, device_id_type=MESH)

        def p2(a, r, to):
            return pltpu.make_async_remote_copy(
                src_ref=sbs[a].at[r - 1], dst_ref=r2s[a].at[r - 1],
                send_sem=s2_send.at[a, r - 1], recv_sem=s2_recv.at[a, r - 1], device_id=to, device_id_type=MESH)

        def stage(a, r):
            return pltpu.make_async_copy(srcs[a].at[_dev_index(*rel[r], c)], sts[a].at[r % 2], st_sem.at[a, r % 2])

        sends1 = [p1(a, r, sibling) for a in range(nb) for r in range(4)]
        for cp in sends1:
            cp.start()
        sends2 = []
        for a in range(nb):
            stage(a, 1).start()
            for r in (1, 2, 3, 0):
                nxt = {1: 2, 2: 3, 3: 0, 0: None}[r]
                if nxt is not None:
                    stage(a, nxt).start()
                stage(a, r).wait()
                p1(a, r, me).wait_recv()
                chip_sum = r1s[a][r] + sts[a][r % 2]
                if r == 0:
                    r1s[a][0] = chip_sum
                else:
                    sbs[a][r - 1] = chip_sum.astype(BF16)
                    cp = p2(a, r, (*rel[r], c))
                    cp.start()
                    sends2.append(cp)
        _all_gather_blocks((ga_ref, gb_ref), ag_send, ag_recv)
        for a in range(nb):
            for r in (1, 2, 3):
                p2(a, r, me).wait_recv()
            outs[a][...] = ((r1s[a][0] + r2s[a][0].astype(F32)) + r2s[a][1].astype(F32)) + r2s[a][2].astype(F32)
        for cp in sends1 + sends2:
            cp.wait_send()

    vm = pl.BlockSpec(memory_space=pltpu.VMEM)
    hbm = pl.BlockSpec(memory_space=pl.ANY)
    blk = [p.shape[1:] for p in big]
    scratch = ([pltpu.VMEM((4,) + b, F32) for b in blk] + [pltpu.VMEM((3,) + b, BF16) for b in blk]
               + [pltpu.VMEM((3,) + b, BF16) for b in blk] + [pltpu.VMEM((2,) + b, F32) for b in blk]
               + [pltpu.SemaphoreType.DMA((nb, 4)), pltpu.SemaphoreType.DMA((nb, 4)),
                  pltpu.SemaphoreType.DMA((nb, 3)), pltpu.SemaphoreType.DMA((nb, 3)),
                  pltpu.SemaphoreType.DMA((2, 7)), pltpu.SemaphoreType.DMA((2, 7)),
                  pltpu.SemaphoreType.DMA((nb, 2))])
    return pl.pallas_call(
        body, name="reduce_grads",
        out_shape=tuple(jax.ShapeDtypeStruct(b, F32) for b in blk)
        + (jax.ShapeDtypeStruct((N_DEV,) + pack_a.shape, F32), jax.ShapeDtypeStruct((N_DEV,) + pack_b.shape, F32)),
        in_specs=[hbm] * nb + [vm, vm], out_specs=(vm,) * (nb + 2),
        scratch_shapes=scratch,
        compiler_params=_cp(),
    )(*big, pack_a, pack_b)


def _rms_hat(xf):
    r = lax.rsqrt(jnp.mean(xf * xf, axis=-1, keepdims=True) + NORM_EPS)
    return xf * r, r


def _in_proj(x2, norm_w, g_in, tm):
    s = x2.shape[0]

    def body(x_ref, nw_ref, g_hbm, pm_ref, pb_ref, wt_hbm, g_vmem, wt_vmem, sem):
        @pl.when(pl.program_id(0) == 0)
        def _():
            cp = pltpu.make_async_copy(g_hbm, g_vmem, sem)
            cp.start()
            wt_vmem[D_MAIN:, :] = jnp.zeros((D_IN_PAD - D_MAIN, D_MODEL), BF16)
            cp.wait()
            for d in range(N_DEV):
                wt_vmem[W_IN_SHARD * d:W_IN_SHARD * (d + 1), :] = g_vmem[d]
            out = pltpu.make_async_copy(wt_vmem, wt_hbm, sem)
            out.start()
            out.wait()
        xhat, _ = _rms_hat(x_ref[...])
        n = (xhat * nw_ref[...]).astype(BF16)
        pm_ref[...] = _dot_nt_bf(n, wt_vmem[:D_MAIN, :])
        pb_ref[...] = _dot_nt_bf(n, wt_vmem[D_MAIN:, :])

    return pl.pallas_call(
        body, name="in_proj", grid=(s // tm,),
        out_shape=(jax.ShapeDtypeStruct((s, D_MAIN), F32), jax.ShapeDtypeStruct((s, 128), F32),
                   jax.ShapeDtypeStruct((D_IN_PAD, D_MODEL), BF16)),
        in_specs=[pl.BlockSpec((tm, D_MODEL), lambda i: (i, 0)),
                  pl.BlockSpec((1, D_MODEL), lambda i: (0, 0)),
                  pl.BlockSpec(memory_space=pl.ANY)],
        out_specs=(pl.BlockSpec((tm, D_MAIN), lambda i: (i, 0)), pl.BlockSpec((tm, 128), lambda i: (i, 0)),
                   pl.BlockSpec(memory_space=pl.ANY)),
        scratch_shapes=[pltpu.VMEM((N_DEV, W_IN_SHARD, D_MODEL), BF16), pltpu.VMEM((D_IN_PAD, D_MODEL), BF16),
                        pltpu.SemaphoreType.DMA],
        compiler_params=_cp(("arbitrary",)),
    )(x2, norm_w, g_in)


def _shift_down(cur, prev_tail, s):
    ext = jnp.concatenate([prev_tail, cur], axis=0)
    return pltpu.roll(ext, s, 0)[HALO:, :]


def _shift_up(cur, next_head, s):
    ext = jnp.concatenate([cur, next_head], axis=0)
    n = ext.shape[0]
    return pltpu.roll(ext, n - s, 0)[:cur.shape[0], :]


def _pool_counts(i, tp, w):
    t = i * tp + lax.broadcasted_iota(jnp.int32, (tp, 1), 0)
    return jnp.minimum(t + 1, w).astype(F32)


def _pool_mix(u, u_prev_tail, i, tp):
    ext = jnp.concatenate([u_prev_tail, u], axis=0)
    w2 = ext + pltpu.roll(ext, 1, 0)
    w4 = w2 + pltpu.roll(w2, 2, 0)
    w8 = w4 + pltpu.roll(w4, 4, 0)
    w16 = w8 + pltpu.roll(w8, 8, 0)
    mixes = []
    for gi, (w, win) in enumerate(zip(POOL_WINDOWS, (w2, w4, w8, w16))):
        cols = slice(gi * POOL_GROUP, (gi + 1) * POOL_GROUP)
        mixes.append(win[HALO:, cols] / _pool_counts(i, tp, w) - u[:, cols])
    return mixes


def _prev_halo_spec(tp, width, col):
    per = tp // HALO
    return pl.BlockSpec((HALO, width), lambda i: (jnp.maximum(i * per - 1, 0), col))


def _next_halo_spec(tp, width, col, n):
    per = tp // HALO
    return pl.BlockSpec((HALO, width), lambda i: (jnp.minimum((i + 1) * per, n * per - 1), col))


def _prev_tail(ref, i):
    return jnp.where(i > 0, ref[...], 0.0)


def _next_head(ref, i, n):
    return jnp.where(i < n - 1, ref[...], 0.0)


def _pool_fwd(proj_main, pool_w, pool_scale, tp):
    s = proj_main.shape[0]

    def body(u_ref, up_ref, z_ref, pw_ref, ps_ref, y_ref):
        i = pl.program_id(0)
        u = u_ref[...]
        mixes = _pool_mix(u, _prev_tail(up_ref, i), i, tp)
        gate = ps_ref[...] * _silu(z_ref[...])
        for gi in range(4):
            cols = slice(gi * POOL_GROUP, (gi + 1) * POOL_GROUP)
            y_ref[:, cols] = _dot_bf(mixes[gi], pw_ref[gi]) * gate[:, cols]

    return pl.pallas_call(
        body, name="pool_fwd", grid=(s // tp,),
        out_shape=jax.ShapeDtypeStruct((s, D_POOL), F32),
        in_specs=[pl.BlockSpec((tp, D_POOL), lambda i: (i, 0)),
                  _prev_halo_spec(tp, D_POOL, 0),
                  pl.BlockSpec((tp, D_POOL), lambda i: (i, 1)),
                  pl.BlockSpec((4, POOL_GROUP, POOL_GROUP), lambda i: (0, 0, 0)),
                  pl.BlockSpec((1, D_POOL), lambda i: (0, 0))],
        out_specs=pl.BlockSpec((tp, D_POOL), lambda i: (i, 0)),
        compiler_params=_cp(("parallel",)),
    )(proj_main, proj_main, proj_main, pool_w, pool_scale)


def _conv_fwd(cur, prev_tail, w4):
    ext = jnp.concatenate([prev_tail, cur], axis=0)
    y = ext * w4[CONV_WIDTH - 1:CONV_WIDTH, :]
    for sft in range(1, CONV_WIDTH):
        y = y + pltpu.roll(ext, sft, 0) * w4[CONV_WIDTH - 1 - sft:CONV_WIDTH - sft, :]
    return y[HALO:, :]


def _l2n_heads(t):
    parts = []
    for h in range(DN_HEADS):
        th = t[:, h * DN_HEAD_DIM:(h + 1) * DN_HEAD_DIM]
        parts.append(th * lax.rsqrt(jnp.sum(th * th, axis=-1, keepdims=True) + NORM_EPS))
    return jnp.concatenate(parts, axis=1)


def _post_conv(yq, yk, yv):
    return _l2n_heads(_silu(yq)), _l2n_heads(_silu(yk)), _silu(yv)


def _gates(ba, alog_lane, dtb_lane):
    lane = lax.broadcasted_iota(jnp.int32, ba.shape, 1)
    beta = jax.nn.sigmoid(ba)
    g = -jnp.exp(alog_lane) * _softplus(ba + dtb_lane)
    return jnp.where(lane < DN_HEADS, beta, jnp.where(lane < 2 * DN_HEADS, g, 0.0))


def _front(x2, norm_w, g_in, pool_w, pool_scale, conv_full, alog_lane, dtb_lane, w_out_blk, tm):
    s = x2.shape[0]

    def body(x_ref, nw_ref, pw_ref, ps_ref, cw_ref, al_ref, db_ref, wo_ref, g_hbm,
             pm_ref, pb_ref, yp_ref, qn_ref, kn_ref, vv_ref, gb_ref, gwo_hbm, wt_hbm,
             g_vmem, wt_vmem, tail_u, tail_qkv, gwo_ref, sem, wo_send, wo_recv):
        i = pl.program_id(0)

        @pl.when(i == 0)
        def _():
            gwo_ref[_dev_index(*_mesh_pos())] = wo_ref[...].astype(BF16)
            _direct_gather_start(gwo_ref, wo_send, wo_recv)
            cp = pltpu.make_async_copy(g_hbm, g_vmem, sem)
            cp.start()
            wt_vmem[D_MAIN:, :] = jnp.zeros((D_IN_PAD - D_MAIN, D_MODEL), BF16)
            tail_u[...] = jnp.zeros_like(tail_u)
            tail_qkv[...] = jnp.zeros_like(tail_qkv)
            cp.wait()
            for d in range(N_DEV):
                wt_vmem[W_IN_SHARD * d:W_IN_SHARD * (d + 1), :] = g_vmem[d]
            out = pltpu.make_async_copy(wt_vmem, wt_hbm, sem)
            out.start()
            out.wait()
        xhat, _ = _rms_hat(x_ref[...])
        n = (xhat * nw_ref[...]).astype(BF16)
        pm_ref[...] = _dot_nt_bf(n, wt_vmem[:D_MAIN, :])
        pb = _dot_nt_bf(n, wt_vmem[D_MAIN:, :])
        pb_ref[...] = pb
        u = pm_ref[:, :D_POOL]
        mixes = _pool_mix(u, tail_u[...], i, tm)
        tail_u[...] = u[tm - HALO:, :]
        gate = ps_ref[...] * _silu(pm_ref[:, D_POOL:2 * D_POOL])
        for gi in range(4):
            cols = slice(gi * POOL_GROUP, (gi + 1) * POOL_GROUP)
            yp_ref[:, cols] = _dot_bf(mixes[gi], pw_ref[gi]) * gate[:, cols]
        ys = []
        for c in range(3):
            cols = slice(c * D_DN, (c + 1) * D_DN)
            cur = pm_ref[:, 2 * D_POOL + c * D_DN:2 * D_POOL + (c + 1) * D_DN]
            ys.append(_conv_fwd(cur, tail_qkv[:, cols], cw_ref[:, cols]))
            tail_qkv[:, cols] = cur[tm - HALO:, :]
        qn, kn, vv = _post_conv(*ys)
        qn_ref[...] = qn
        kn_ref[...] = kn
        vv_ref[...] = vv
        gb_ref[...] = _gates(pb, al_ref[...], db_ref[...])

        @pl.when(i == s // tm - 1)
        def _():
            _direct_gather_wait(gwo_ref, wo_send, wo_recv)
            out = pltpu.make_async_copy(gwo_ref, gwo_hbm, sem)
            out.start()
            out.wait()

    tile = pl.BlockSpec((tm, D_DN), lambda i: (i, 0))
    lanes = pl.BlockSpec((tm, 128), lambda i: (i, 0))
    row = pl.BlockSpec((1, 128), lambda i: (0, 0))
    return pl.pallas_call(
        body, name="front", grid=(s // tm,),
        out_shape=(jax.ShapeDtypeStruct((s, D_MAIN), F32), jax.ShapeDtypeStruct((s, 128), F32),
                   jax.ShapeDtypeStruct((s, D_POOL), F32), jax.ShapeDtypeStruct((s, D_DN), F32),
                   jax.ShapeDtypeStruct((s, D_DN), F32), jax.ShapeDtypeStruct((s, D_DN), F32),
                   jax.ShapeDtypeStruct((s, 128), F32), jax.ShapeDtypeStruct((N_DEV,) + w_out_blk.shape, BF16),
                   jax.ShapeDtypeStruct((D_IN_PAD, D_MODEL), BF16)),
        in_specs=[pl.BlockSpec((tm, D_MODEL), lambda i: (i, 0)),
                  pl.BlockSpec((1, D_MODEL), lambda i: (0, 0)),
                  pl.BlockSpec((4, POOL_GROUP, POOL_GROUP), lambda i: (0, 0, 0)),
                  pl.BlockSpec((1, D_POOL), lambda i: (0, 0)),
                  pl.BlockSpec((CONV_WIDTH, 3 * D_DN), lambda i: (0, 0)), row, row,
                  pl.BlockSpec(memory_space=pltpu.VMEM), pl.BlockSpec(memory_space=pl.ANY)],
        out_specs=(pl.BlockSpec((tm, D_MAIN), lambda i: (i, 0)), lanes, tile, tile, tile, tile, lanes,
                   pl.BlockSpec(memory_space=pl.ANY), pl.BlockSpec(memory_space=pl.ANY)),
        scratch_shapes=[pltpu.VMEM((N_DEV, W_IN_SHARD, D_MODEL), BF16), pltpu.VMEM((D_IN_PAD, D_MODEL), BF16),
                        pltpu.VMEM((HALO, D_POOL), F32), pltpu.VMEM((HALO, 3 * D_DN), F32),
                        pltpu.VMEM((N_DEV,) + w_out_blk.shape, BF16),
                        pltpu.SemaphoreType.DMA, pltpu.SemaphoreType.DMA((7,)), pltpu.SemaphoreType.DMA((7,))],
        compiler_params=_cp(("arbitrary",)),
    )(x2, norm_w, pool_w, pool_scale, conv_full, alog_lane, dtb_lane, w_out_blk, g_in)


def _qkv_specs(tp, which):
    def spec(col, n=None):
        if which == 0:
            return pl.BlockSpec((tp, D_DN), lambda i: (i, col))
        if which < 0:
            return _prev_halo_spec(tp, D_DN, col)
        return _next_halo_spec(tp, D_DN, col, n)
    return spec


def _dn_pre(proj_main, proj_ba, conv_full, alog_lane, dtb_lane, tp):
    s = proj_main.shape[0]

    def body(q_ref, k_ref, v_ref, qp_ref, kp_ref, vp_ref, cw_ref, ba_ref, al_ref, db_ref,
             qn_ref, kn_ref, vv_ref, gb_ref):
        i = pl.program_id(0)
        ys = []
        for j, (cur, prev) in enumerate(((q_ref, qp_ref), (k_ref, kp_ref), (v_ref, vp_ref))):
            ys.append(_conv_fwd(cur[...], _prev_tail(prev, i), cw_ref[:, j * D_DN:(j + 1) * D_DN]))
        qn, kn, vv = _post_conv(*ys)
        qn_ref[...] = qn
        kn_ref[...] = kn
        vv_ref[...] = vv
        gb_ref[...] = _gates(ba_ref[...], al_ref[...], db_ref[...])

    cur, prev = _qkv_specs(tp, 0), _qkv_specs(tp, -1)
    row = pl.BlockSpec((1, 128), lambda i: (0, 0))
    tile = pl.BlockSpec((tp, D_DN), lambda i: (i, 0))
    return pl.pallas_call(
        body, name="dn_pre", grid=(s // tp,),
        out_shape=(jax.ShapeDtypeStruct((s, D_DN), F32),) * 3 + (jax.ShapeDtypeStruct((s, 128), F32),),
        in_specs=[cur(2), cur(3), cur(4), prev(2), prev(3), prev(4),
                  pl.BlockSpec((CONV_WIDTH, 3 * D_DN), lambda i: (0, 0)),
                  pl.BlockSpec((tp, 128), lambda i: (i, 0)), row, row],
        out_specs=(tile, tile, tile, pl.BlockSpec((tp, 128), lambda i: (i, 0))),
        compiler_params=_cp(("parallel",)),
    )(proj_main, proj_main, proj_main, proj_main, proj_main, proj_main, conv_full, proj_ba, alog_lane, dtb_lane)


def _dn_block(q, k, v, gcol, bcol, state, dz, nw, known=None):
    nb, n, d = q.shape
    ii = lax.broadcasted_iota(jnp.int32, (n, n), 0)
    jj = lax.broadcasted_iota(jnp.int32, (n, n), 1)
    lower = ii >= jj
    eye = (ii == jj).astype(F32)
    g_row = jnp.sum(eye * gcol, axis=1, keepdims=True)
    gc_col = jnp.sum(jnp.where(lower, g_row, 0.0), axis=2, keepdims=True)
    gc_row = jnp.sum(eye * gc_col, axis=1, keepdims=True)
    decay = jnp.where(lower, jnp.exp(jnp.where(lower, gc_col - gc_row, 0.0)), 0.0)
    kb = k * bcol
    vb = v * bcol
    qs = q * (DN_HEAD_DIM ** -0.5)
    egc = jnp.exp(gc_col)
    kq = jnp.concatenate([kb, qs], axis=1)
    vk = jnp.concatenate([vb, kb * egc], axis=2)
    if known is None:
        akq = _mm(kq, k, 1, 1, *_DN_PREC["akq"])
    else:
        akq = _mm_known(kq, k, known[0][:, :, :n].astype(F32), 1, 1, _DN_PREC["akq"][1])
    a = jnp.where(ii > jj, akq[:, :n] * decay, 0.0)
    qk = akq[:, n:] * decay
    if known is None:
        t = _tri_inv(a, *_DN_PREC["inv"])
        uw = _mm(t, vk, 1, 0, *_DN_PREC["uw"])
    else:
        t = _tri_inv_known(a, known[0][:, :n, n:].astype(F32), _DN_PREC["inv"][1])
        uw = _mm_known(t, vk, known[1], 1, 0, _DN_PREC["uw"][1])
    pre = jnp.concatenate([akq, jnp.concatenate([t, jnp.zeros_like(t)], axis=1)], axis=2).astype(BF16)
    wq = jnp.concatenate([uw[:, :, d:], qs * egc], axis=1)
    g_last = gc_col[:, n - 1:n, :]
    k_dec = k * jnp.exp(g_last - gc_col)
    e_last = jnp.exp(g_last)
    os_, starts = [], []
    for c in range(nb // DN_HEADS):
        sl = slice(c * DN_HEADS, (c + 1) * DN_HEADS)
        if known is not None and c > 0:
            state = _use_known(state, known[2][sl])
        starts.append(state)
        ws = _mm(wq[sl], state, 1, 0, *_DN_PREC["ws"])
        v_new = uw[sl, :, :d] - ws[:, :n]
        os_.append(ws[:, n:] + _mm(qk[sl], v_new, 1, 0, *_DN_PREC["ov"]))
        state = state * e_last[sl] + _mm(k_dec[sl], v_new, 0, 0, *_DN_PREC["st"])
    o = jnp.concatenate(os_, axis=0)
    y = o * lax.rsqrt(jnp.mean(o * o, axis=-1, keepdims=True) + NORM_EPS) * nw * _silu(dz)
    return y, state, (pre, uw, jnp.concatenate(starts, axis=0))


def _dn_block_args(gc, q_ref, k_ref, v_ref, gb_ref, dz_ref):
    qs, ks, vs, gs, bs, zs = [], [], [], [], [], []
    for cc in range(gc):
        r = slice(cc * CHUNK, (cc + 1) * CHUNK)
        gbv = gb_ref[r, :]
        for h in range(DN_HEADS):
            cols = slice(h * DN_HEAD_DIM, (h + 1) * DN_HEAD_DIM)
            qs.append(q_ref[r, cols])
            ks.append(k_ref[r, cols])
            vs.append(v_ref[r, cols])
            zs.append(dz_ref[r, cols])
            gs.append(gbv[:, DN_HEADS + h:DN_HEADS + h + 1])
            bs.append(gbv[:, h:h + 1])
    return tuple(jnp.stack(t, axis=0) for t in (qs, ks, vs, gs, bs, zs))


def _dn_scan_fwd(qn, kn, vv, gb, proj_main, dn_norm_w, gc):
    s = qn.shape[0]
    nchunk = s // CHUNK
    rows = gc * CHUNK

    def body(q_ref, k_ref, v_ref, gb_ref, dz_ref, nw_ref, y_ref, ss_ref, pre_ref, uw_ref, state):
        @pl.when(pl.program_id(0) == 0)
        def _():
            state[...] = jnp.zeros_like(state)
        q, k, v, gcol, bcol, dz = _dn_block_args(gc, q_ref, k_ref, v_ref, gb_ref, dz_ref)
        y, new, (pre, uw, starts) = _dn_block(q, k, v, gcol, bcol, state[...], dz, nw_ref[...])
        state[...] = new
        ss_ref[...] = starts
        pre_ref[...] = pre
        uw_ref[...] = uw
        for cc in range(gc):
            for h in range(DN_HEADS):
                y_ref[cc * CHUNK:(cc + 1) * CHUNK, h * DN_HEAD_DIM:(h + 1) * DN_HEAD_DIM] = y[cc * DN_HEADS + h]

    tile = pl.BlockSpec((rows, D_DN), lambda i: (i, 0))
    return pl.pallas_call(
        body, name="dn_scan_fwd", grid=(nchunk // gc,),
        out_shape=(jax.ShapeDtypeStruct((s, D_DN), F32),
                   jax.ShapeDtypeStruct((nchunk * DN_HEADS, DN_HEAD_DIM, DN_HEAD_DIM), F32),
                   jax.ShapeDtypeStruct((nchunk * DN_HEADS, 2 * CHUNK, 2 * CHUNK), BF16),
                   jax.ShapeDtypeStruct((nchunk * DN_HEADS, CHUNK, 2 * DN_HEAD_DIM), F32)),
        in_specs=[tile, tile, tile, pl.BlockSpec((rows, 128), lambda i: (i, 0)),
                  pl.BlockSpec((rows, D_DN), lambda i: (i, 5)), pl.BlockSpec((1, 128), lambda i: (0, 0))],
        out_specs=(tile, pl.BlockSpec((gc * DN_HEADS, DN_HEAD_DIM, DN_HEAD_DIM), lambda i: (i, 0, 0)),
                   pl.BlockSpec((gc * DN_HEADS, 2 * CHUNK, 2 * CHUNK), lambda i: (i, 0, 0)),
                   pl.BlockSpec((gc * DN_HEADS, CHUNK, 2 * DN_HEAD_DIM), lambda i: (i, 0, 0))),
        scratch_shapes=[pltpu.VMEM((DN_HEADS, DN_HEAD_DIM, DN_HEAD_DIM), F32)],
        compiler_params=_cp(("arbitrary",)),
    )(qn, kn, vv, gb, proj_main, dn_norm_w)


def _out_proj_loss(y_pool, y_dn, x2, tgt, w_out_full, fnw, tm):
    s = x2.shape[0]

    def body(yp_ref, yd_ref, x_ref, t_ref, wo_ref, fw_ref,
             dh_ref, dyp_ref, dyd_ref, gwo_ref, gfw_ref, loss_ref):
        @pl.when(pl.program_id(0) == 0)
        def _():
            gwo_ref[...] = jnp.zeros_like(gwo_ref)
            gfw_ref[...] = jnp.zeros_like(gfw_ref)
            loss_ref[...] = jnp.zeros_like(loss_ref)
        y = jnp.concatenate([yp_ref[...], yd_ref[...]], axis=1).astype(BF16)
        wo = wo_ref[...]
        h = x_ref[...] + jnp.dot(y, wo, preferred_element_type=F32)
        hn, r = _rms_hat(h)
        fw = fw_ref[...]
        err = hn * fw - t_ref[...]
        loss_ref[...] += 0.5 * jnp.sum(jnp.sum(err * err, axis=-1, keepdims=True) / D_MODEL, axis=0, keepdims=True)
        dout = err / D_MODEL
        gfw_ref[...] += jnp.sum(dout * hn, axis=0, keepdims=True)
        dhn = dout * fw
        dh = r * (dhn - hn * jnp.mean(dhn * hn, axis=-1, keepdims=True))
        dh_ref[...] = dh
        dhb = dh.astype(BF16)
        dy = _dot_nt_bf(dhb, wo)
        dyp_ref[...] = dy[:, :D_POOL]
        dyd_ref[...] = dy[:, D_POOL:]
        gwo_ref[...] += _dot_tn_bf(y, dhb)

    half = pl.BlockSpec((tm, D_POOL), lambda i: (i, 0))
    full = pl.BlockSpec((tm, D_MODEL), lambda i: (i, 0))
    return pl.pallas_call(
        body, name="out_proj_loss", grid=(s // tm,),
        out_shape=(jax.ShapeDtypeStruct((s, D_MODEL), F32), jax.ShapeDtypeStruct((s, D_POOL), F32),
                   jax.ShapeDtypeStruct((s, D_DN), F32), jax.ShapeDtypeStruct((D_MODEL, D_MODEL), F32),
                   jax.ShapeDtypeStruct((1, D_MODEL), F32), jax.ShapeDtypeStruct((1, 128), F32)),
        in_specs=[half, half, full, full, pl.BlockSpec((D_MODEL, D_MODEL), lambda i: (0, 0)),
                  pl.BlockSpec((1, D_MODEL), lambda i: (0, 0))],
        out_specs=(full, half, half, pl.BlockSpec((D_MODEL, D_MODEL), lambda i: (0, 0)),
                   pl.BlockSpec((1, D_MODEL), lambda i: (0, 0)), pl.BlockSpec((1, 128), lambda i: (0, 0))),
        compiler_params=_cp(("arbitrary",)),
    )(y_pool, y_dn, x2, tgt, w_out_full, fnw)


def _dn_scan_bwd(qn, kn, vv, gb, proj_main, dn_norm_w, states, pre, uw, dy_dn, p_out, gc):
    s = qn.shape[0]
    nchunk = s // CHUNK
    nstep = nchunk // gc
    rows = gc * CHUNK

    def body(q_ref, k_ref, v_ref, gb_ref, dz_ref, nw_ref, ss_ref, pre_ref, uw_ref, dy_ref, po_ref,
             dq_ref, dk_ref, dv_ref, dgb_ref, ddz_ref, dnw_ref, gro_ref, dstate, po_send, po_recv, rs_send, rs_recv):
        @pl.when(pl.program_id(0) == 0)
        def _():
            dstate[...] = jnp.zeros_like(dstate)
            dnw_ref[...] = jnp.zeros_like(dnw_ref)
            po_send[...] = po_ref[...].astype(BF16)
            _direct_scatter_start(po_send, po_recv, rs_send, rs_recv)

        @pl.when(pl.program_id(0) == nstep - 1)
        def _():
            _direct_scatter_wait(po_send, po_recv, rs_send, rs_recv)
            total = po_ref[_dev_index(*_mesh_pos())]
            for k in range(N_DEV - 1):
                total = total + po_recv[k].astype(F32)
            gro_ref[...] = total
        lane = lax.broadcasted_iota(jnp.int32, (CHUNK, 128), 1)
        q, k, v, gcol, bcol, dz = _dn_block_args(gc, q_ref, k_ref, v_ref, gb_ref, dz_ref)
        dy = jnp.stack([dy_ref[cc * CHUNK:(cc + 1) * CHUNK, h * DN_HEAD_DIM:(h + 1) * DN_HEAD_DIM]
                        for cc in range(gc) for h in range(DN_HEADS)], axis=0)
        known = (pre_ref[...], uw_ref[...], ss_ref[...])
        _, vjp = jax.vjp(lambda *a: _dn_block(*a, known=known)[:2], q, k, v, gcol, bcol, ss_ref[:DN_HEADS], dz,
                         nw_ref[...])
        dq, dk, dv, dg, db, dst, ddz, dnw = vjp((dy, dstate[...]))
        dstate[...] = dst
        dnw_ref[...] += dnw
        for cc in range(gc):
            r = slice(cc * CHUNK, (cc + 1) * CHUNK)
            dgb = jnp.zeros((CHUNK, 128), F32)
            for h in range(DN_HEADS):
                b = cc * DN_HEADS + h
                cols = slice(h * DN_HEAD_DIM, (h + 1) * DN_HEAD_DIM)
                for ref, val in zip((dq_ref, dk_ref, dv_ref, ddz_ref), (dq, dk, dv, ddz)):
                    ref[r, cols] = val[b]
                dgb = dgb + jnp.where(lane == h, db[b], 0.0) + jnp.where(lane == DN_HEADS + h, dg[b], 0.0)
            dgb_ref[r, :] = dgb

    rev = lambda i: (nstep - 1 - i, 0)
    tile = pl.BlockSpec((rows, D_DN), rev)
    lanes = pl.BlockSpec((rows, 128), rev)
    return pl.pallas_call(
        body, name="dn_scan_bwd", grid=(nstep,),
        out_shape=(jax.ShapeDtypeStruct((s, D_DN), F32),) * 3
        + (jax.ShapeDtypeStruct((s, 128), F32), jax.ShapeDtypeStruct((s, D_DN), F32),
           jax.ShapeDtypeStruct((1, 128), F32), jax.ShapeDtypeStruct(p_out.shape[1:], F32)),
        in_specs=[tile, tile, tile, lanes, pl.BlockSpec((rows, D_DN), lambda i: (nstep - 1 - i, 5)),
                  pl.BlockSpec((1, 128), lambda i: (0, 0)),
                  pl.BlockSpec((gc * DN_HEADS, DN_HEAD_DIM, DN_HEAD_DIM), lambda i: (nstep - 1 - i, 0, 0)),
                  pl.BlockSpec((gc * DN_HEADS, 2 * CHUNK, 2 * CHUNK), lambda i: (nstep - 1 - i, 0, 0)),
                  pl.BlockSpec((gc * DN_HEADS, CHUNK, 2 * DN_HEAD_DIM), lambda i: (nstep - 1 - i, 0, 0)), tile,
                  pl.BlockSpec(memory_space=pltpu.VMEM)],
        out_specs=(tile, tile, tile, lanes, tile, pl.BlockSpec((1, 128), lambda i: (0, 0)),
                   pl.BlockSpec(memory_space=pltpu.VMEM)),
        scratch_shapes=[pltpu.VMEM((DN_HEADS, DN_HEAD_DIM, DN_HEAD_DIM), F32),
                        pltpu.VMEM(p_out.shape, BF16), pltpu.VMEM((N_DEV - 1,) + p_out.shape[1:], BF16),
                        pltpu.SemaphoreType.DMA((7,)), pltpu.SemaphoreType.DMA((7,))],
        compiler_params=_cp(("arbitrary",)),
    )(qn, kn, vv, gb, proj_main, dn_norm_w, states, pre, uw, dy_dn, p_out)


def _dn_pre_bwd1(proj_main, proj_ba, conv_full, alog_lane, dtb_lane, dqn, dkn, dvv, dgb, tp):
    s = proj_main.shape[0]

    def body(q_ref, k_ref, v_ref, qp_ref, kp_ref, vp_ref, cw_ref, ba_ref, al_ref, db_ref,
             dqn_ref, dkn_ref, dvv_ref, dgb_ref,
             dcq_ref, dck_ref, dcv_ref, dba_ref, dcw_ref, dal_ref, ddb_ref):
        i = pl.program_id(0)

        @pl.when(i == 0)
        def _():
            dcw_ref[...] = jnp.zeros_like(dcw_ref)
            dal_ref[...] = jnp.zeros_like(dal_ref)
            ddb_ref[...] = jnp.zeros_like(ddb_ref)
        curs = (q_ref[...], k_ref[...], v_ref[...])
        tails = (_prev_tail(qp_ref, i), _prev_tail(kp_ref, i), _prev_tail(vp_ref, i))
        ys = [_conv_fwd(curs[j], tails[j], cw_ref[:, j * D_DN:(j + 1) * D_DN]) for j in range(3)]
        _, vjp = jax.vjp(_post_conv, *ys)
        dys = vjp((dqn_ref[...], dkn_ref[...], dvv_ref[...]))
        for j, (dy, out) in enumerate(zip(dys, (dcq_ref, dck_ref, dcv_ref))):
            out[...] = dy
            for sft in range(CONV_WIDTH):
                xs = curs[j] if sft == 0 else _shift_down(curs[j], tails[j], sft)
                row = CONV_WIDTH - 1 - sft
                dcw_ref[row:row + 1, j * D_DN:(j + 1) * D_DN] += jnp.sum(dy * xs, axis=0, keepdims=True)
        _, gvjp = jax.vjp(_gates, ba_ref[...], al_ref[...], db_ref[...])
        dba, dal, ddb = gvjp(dgb_ref[...])
        dba_ref[...] = dba
        dal_ref[...] += dal
        ddb_ref[...] += ddb

    cur, prev = _qkv_specs(tp, 0), _qkv_specs(tp, -1)
    row = pl.BlockSpec((1, 128), lambda i: (0, 0))
    tile = pl.BlockSpec((tp, D_DN), lambda i: (i, 0))
    lanes = pl.BlockSpec((tp, 128), lambda i: (i, 0))
    cw = pl.BlockSpec((CONV_WIDTH, 3 * D_DN), lambda i: (0, 0))
    return pl.pallas_call(
        body, name="dn_pre_bwd1", grid=(s // tp,),
        out_shape=(jax.ShapeDtypeStruct((s, D_DN), F32),) * 3
        + (jax.ShapeDtypeStruct((s, 128), F32), jax.ShapeDtypeStruct((CONV_WIDTH, 3 * D_DN), F32),
           jax.ShapeDtypeStruct((1, 128), F32), jax.ShapeDtypeStruct((1, 128), F32)),
        in_specs=[cur(2), cur(3), cur(4), prev(2), prev(3), prev(4), cw, lanes, row, row, tile, tile, tile, lanes],
        out_specs=(tile, tile, tile, lanes, cw, row, row),
        compiler_params=_cp(("arbitrary",)),
    )(proj_main, proj_main, proj_main, proj_main, proj_main, proj_main, conv_full, proj_ba, alog_lane, dtb_lane,
      dqn, dkn, dvv, dgb)


def _pool_bwd1(proj_main, pool_w, pool_scale, dyp, tp):
    s = proj_main.shape[0]

    def body(u_ref, up_ref, z_ref, pw_ref, ps_ref, dy_ref, dz_ref, dwin_ref, dpw_ref, dps_ref):
        i = pl.program_id(0)

        @pl.when(i == 0)
        def _():
            dpw_ref[...] = jnp.zeros_like(dpw_ref)
            dps_ref[...] = jnp.zeros_like(dps_ref)
        u = u_ref[...]
        z = z_ref[...]
        dy = dy_ref[...]
        ps = ps_ref[...]
        mixes = _pool_mix(u, _prev_tail(up_ref, i), i, tp)
        sg = jax.nn.sigmoid(z)
        sz = z * sg
        dsz = sg * (1.0 + z * (1.0 - sg))
        for gi, w in enumerate(POOL_WINDOWS):
            cols = slice(gi * POOL_GROUP, (gi + 1) * POOL_GROUP)
            mixw = _dot_bf(mixes[gi], pw_ref[gi])
            dmixw = dy[:, cols] * ps[:, cols] * sz[:, cols]
            dps_ref[:, cols] += jnp.sum(dy[:, cols] * mixw * sz[:, cols], axis=0, keepdims=True)
            dz_ref[:, cols] = dy[:, cols] * mixw * ps[:, cols] * dsz[:, cols]
            dpw_ref[gi] += _dot_tn_bf(mixes[gi], dmixw)
            dmix = _dot_nt_bf(dmixw, pw_ref[gi])
            dwin_ref[:, cols] = dmix / _pool_counts(i, tp, w)

    tile = pl.BlockSpec((tp, D_POOL), lambda i: (i, 0))
    pw = pl.BlockSpec((4, POOL_GROUP, POOL_GROUP), lambda i: (0, 0, 0))
    ps = pl.BlockSpec((1, D_POOL), lambda i: (0, 0))
    return pl.pallas_call(
        body, name="pool_bwd1", grid=(s // tp,),
        out_shape=(jax.ShapeDtypeStruct((s, D_POOL), F32), jax.ShapeDtypeStruct((s, D_POOL), F32),
                   jax.ShapeDtypeStruct((4, POOL_GROUP, POOL_GROUP), F32), jax.ShapeDtypeStruct((1, D_POOL), F32)),
        in_specs=[tile, _prev_halo_spec(tp, D_POOL, 0),
                  pl.BlockSpec((tp, D_POOL), lambda i: (i, 1)), pw, ps, tile],
        out_specs=(tile, tile, pw, ps),
        compiler_params=_cp(("arbitrary",)),
    )(proj_main, proj_main, proj_main, pool_w, pool_scale, dyp)


def _back(proj_main, proj_ba, dyp, dqn, dkn, dvv, dgb, ddz, x2, dh, norm_w, pool_w, pool_scale, conv_full,
          alog_lane, dtb_lane, wt_full, tm):
    s = x2.shape[0]
    nstep = s // tm
    per = tm // HALO

    def body(u_ref, z_ref, q_ref, k_ref, v_ref, up_ref, qp_ref, kp_ref, vp_ref, ba_ref,
             dyp_ref, dqn_ref, dkn_ref, dvv_ref, dgb_ref, ddz_ref, x_ref, dh_ref,
             nw_ref, pw_ref, ps_ref, cw_ref, al_ref, db_ref, wt_hbm,
             gx_ref, p_hbm, gnw_ref, dpw_ref, dps_ref, dcw_ref, dal_ref, ddb_ref,
             wt_vmem, acc, blk, head_dc, head_dw, sem, osem):
        j = pl.program_id(0)
        i = nstep - 1 - j

        @pl.when(j == 0)
        def _():
            cp = pltpu.make_async_copy(wt_hbm, wt_vmem, sem)
            cp.start()
            acc[...] = jnp.zeros_like(acc)
            for ref in (gnw_ref, dpw_ref, dps_ref, dcw_ref, dal_ref, ddb_ref, head_dc, head_dw):
                ref[...] = jnp.zeros_like(ref)
            cp.wait()

        u = u_ref[...]
        z = z_ref[...]
        dy = dyp_ref[...]
        ps = ps_ref[...]
        mixes = _pool_mix(u, _prev_tail(up_ref, i), i, tm)
        sg = jax.nn.sigmoid(z)
        sz = z * sg
        dsz = sg * (1.0 + z * (1.0 - sg))
        dzs, dwins = [], []
        for gi, w in enumerate(POOL_WINDOWS):
            cols = slice(gi * POOL_GROUP, (gi + 1) * POOL_GROUP)
            mixw = _dot_bf(mixes[gi], pw_ref[gi])
            dmixw = dy[:, cols] * ps[:, cols] * sz[:, cols]
            dps_ref[:, cols] += jnp.sum(dy[:, cols] * mixw * sz[:, cols], axis=0, keepdims=True)
            dzs.append(dy[:, cols] * mixw * ps[:, cols] * dsz[:, cols])
            dpw_ref[gi] += _dot_tn_bf(mixes[gi], dmixw)
            dwins.append(_dot_nt_bf(dmixw, pw_ref[gi]) / _pool_counts(i, tm, w))
        dzp = jnp.concatenate(dzs, axis=1)
        dw = jnp.concatenate(dwins, axis=1)
        ext = jnp.concatenate([dw, head_dw[...]], axis=0)
        m = ext.shape[0]
        a2 = ext + pltpu.roll(ext, m - 1, 0)
        a4 = a2 + pltpu.roll(a2, m - 2, 0)
        a8 = a4 + pltpu.roll(a4, m - 4, 0)
        a16 = a8 + pltpu.roll(a8, m - 8, 0)
        dup = jnp.concatenate(
            [acc_w[:tm, gi * POOL_GROUP:(gi + 1) * POOL_GROUP]
             - dw[:, gi * POOL_GROUP:(gi + 1) * POOL_GROUP] * _pool_counts(i, tm, w)
             for gi, (w, acc_w) in enumerate(zip(POOL_WINDOWS, (a2, a4, a8, a16)))], axis=1)
        head_dw[...] = dw[:HALO, :]

        curs = (q_ref[...], k_ref[...], v_ref[...])
        tails = (_prev_tail(qp_ref, i), _prev_tail(kp_ref, i), _prev_tail(vp_ref, i))
        ys = [_conv_fwd(curs[c], tails[c], cw_ref[:, c * D_DN:(c + 1) * D_DN]) for c in range(3)]
        _, vjp = jax.vjp(_post_conv, *ys)
        dys = vjp((dqn_ref[...], dkn_ref[...], dvv_ref[...]))
        dxs = []
        for c, dyc in enumerate(dys):
            cols = slice(c * D_DN, (c + 1) * D_DN)
            w4 = cw_ref[:, cols]
            for sft in range(CONV_WIDTH):
                xs = curs[c] if sft == 0 else _shift_down(curs[c], tails[c], sft)
                row = CONV_WIDTH - 1 - sft
                dcw_ref[row:row + 1, cols] += jnp.sum(dyc * xs, axis=0, keepdims=True)
            head = head_dc[:, cols]
            dx = dyc * w4[CONV_WIDTH - 1:CONV_WIDTH, :]
            for sft in range(1, CONV_WIDTH):
                dx = dx + _shift_up(dyc, head, sft) * w4[CONV_WIDTH - 1 - sft:CONV_WIDTH - sft, :]
            dxs.append(dx)
            head_dc[:, cols] = dyc[:HALO, :]
        _, gvjp = jax.vjp(_gates, ba_ref[...], al_ref[...], db_ref[...])
        dba, dal, ddb = gvjp(dgb_ref[...])
        dal_ref[...] += dal
        ddb_ref[...] += ddb

        dbab = dba.astype(BF16)
        xhat, r = _rms_hat(x_ref[...])
        nw = nw_ref[...]
        n = (xhat * nw).astype(BF16)
        acc[D_MAIN:, :] += _dot_tn_bf(dbab, n)
        dn = jnp.dot(dbab, wt_vmem[D_MAIN:, :], preferred_element_type=F32)
        for cb, d in enumerate((dup, dzp, dxs[0], dxs[1], dxs[2], ddz_ref[...])):
            rows = slice(cb * D_POOL, (cb + 1) * D_POOL)
            dpart = d.astype(BF16)
            acc[rows, :] += _dot_tn_bf(dpart, n)
            dn = dn + jnp.dot(dpart, wt_vmem[rows, :], preferred_element_type=F32)
        gnw_ref[...] += jnp.sum(dn * xhat, axis=0, keepdims=True)
        dxh = dn * nw
        gx_ref[...] = dh_ref[...] + r * (dxh - xhat * jnp.mean(dxh * xhat, axis=-1, keepdims=True))

        @pl.when(j == nstep - 1)
        def _():
            def out(d):
                return pltpu.make_async_copy(blk.at[d % 2], p_hbm.at[d], osem.at[d % 2])
            for d in range(N_DEV):
                if d >= 2:
                    out(d - 2).wait()
                blk[d % 2] = acc[W_IN_SHARD * d:W_IN_SHARD * (d + 1), :]
                out(d).start()
            out(N_DEV - 2).wait()
            out(N_DEV - 1).wait()

    def col(c):
        return pl.BlockSpec((tm, D_POOL), lambda j: (nstep - 1 - j, c))

    def halo(c):
        return pl.BlockSpec((HALO, D_POOL), lambda j: (jnp.maximum((nstep - 1 - j) * per - 1, 0), c))

    rev = lambda j: (nstep - 1 - j, 0)
    part = pl.BlockSpec((tm, D_POOL), rev)
    lanes = pl.BlockSpec((tm, 128), rev)
    full = pl.BlockSpec((tm, D_MODEL), rev)
    row = pl.BlockSpec((1, D_MODEL), lambda j: (0, 0))
    lrow = pl.BlockSpec((1, 128), lambda j: (0, 0))
    pw = pl.BlockSpec((4, POOL_GROUP, POOL_GROUP), lambda j: (0, 0, 0))
    psp = pl.BlockSpec((1, D_POOL), lambda j: (0, 0))
    cw = pl.BlockSpec((CONV_WIDTH, 3 * D_DN), lambda j: (0, 0))
    return pl.pallas_call(
        body, name="back", grid=(nstep,),
        out_shape=(jax.ShapeDtypeStruct((s, D_MODEL), F32),
                   jax.ShapeDtypeStruct((N_DEV, W_IN_SHARD, D_MODEL), F32), jax.ShapeDtypeStruct((1, D_MODEL), F32),
                   jax.ShapeDtypeStruct((4, POOL_GROUP, POOL_GROUP), F32), jax.ShapeDtypeStruct((1, D_POOL), F32),
                   jax.ShapeDtypeStruct((CONV_WIDTH, 3 * D_DN), F32),
                   jax.ShapeDtypeStruct((1, 128), F32), jax.ShapeDtypeStruct((1, 128), F32)),
        in_specs=[col(0), col(1), col(2), col(3), col(4), halo(0), halo(2), halo(3), halo(4), lanes,
                  part, part, part, part, lanes, part, full, full,
                  row, pw, psp, cw, lrow, lrow, pl.BlockSpec(memory_space=pl.ANY)],
        out_specs=(full, pl.BlockSpec(memory_space=pl.ANY), row, pw, psp, cw, lrow, lrow),
        scratch_shapes=[pltpu.VMEM((D_IN_PAD, D_MODEL), BF16), pltpu.VMEM((D_IN_PAD, D_MODEL), F32),
                        pltpu.VMEM((2, W_IN_SHARD, D_MODEL), F32),
                        pltpu.VMEM((HALO, 3 * D_DN), F32), pltpu.VMEM((HALO, D_POOL), F32),
                        pltpu.SemaphoreType.DMA, pltpu.SemaphoreType.DMA((2,))],
        compiler_params=_cp(("arbitrary",)),
    )(proj_main, proj_main, proj_main, proj_main, proj_main, proj_main, proj_main, proj_main, proj_main, proj_ba,
      dyp, dqn, dkn, dvv, dgb, ddz, x2, dh, norm_w, pool_w, pool_scale, conv_full, alog_lane, dtb_lane, wt_full)


def _in_proj_bwd(dwin, dzp, dcq, dck, dcv, ddz, dba, x2, dh, norm_w, wt_full, conv_full, tm):
    s = x2.shape[0]
    nstep = s // tm

    def body(dwin_ref, dzp_ref, dcq_ref, dck_ref, dcv_ref, ddz_ref, dba_ref, x_ref, dh_ref, nw_ref, cw_ref, wt_hbm,
             gx_ref, p_hbm, gnw_ref, wt_vmem, acc, blk, head_dc, head_dw, sem, osem):
        j = pl.program_id(0)
        i = nstep - 1 - j

        @pl.when(j == 0)
        def _():
            cp = pltpu.make_async_copy(wt_hbm, wt_vmem, sem)
            cp.start()
            acc[...] = jnp.zeros_like(acc)
            gnw_ref[...] = jnp.zeros_like(gnw_ref)
            head_dc[...] = jnp.zeros_like(head_dc)
            head_dw[...] = jnp.zeros_like(head_dw)
            cp.wait()
        dw = dwin_ref[...]
        ext = jnp.concatenate([dw, head_dw[...]], axis=0)
        m = ext.shape[0]
        a2 = ext + pltpu.roll(ext, m - 1, 0)
        a4 = a2 + pltpu.roll(a2, m - 2, 0)
        a8 = a4 + pltpu.roll(a4, m - 4, 0)
        a16 = a8 + pltpu.roll(a8, m - 8, 0)
        dup = jnp.concatenate(
            [acc_w[:tm, gi * POOL_GROUP:(gi + 1) * POOL_GROUP]
             - dw[:, gi * POOL_GROUP:(gi + 1) * POOL_GROUP] * _pool_counts(i, tm, w)
             for gi, (w, acc_w) in enumerate(zip(POOL_WINDOWS, (a2, a4, a8, a16)))], axis=1)
        head_dw[...] = dw[:HALO, :]
        dxs = []
        for c, ref in enumerate((dcq_ref, dck_ref, dcv_ref)):
            cols = slice(c * D_DN, (c + 1) * D_DN)
            w4 = cw_ref[:, cols]
            dy = ref[...]
            head = head_dc[:, cols]
            dx = dy * w4[CONV_WIDTH - 1:CONV_WIDTH, :]
            for sft in range(1, CONV_WIDTH):
                dx = dx + _shift_up(dy, head, sft) * w4[CONV_WIDTH - 1 - sft:CONV_WIDTH - sft, :]
            dxs.append(dx)
            head_dc[:, cols] = dy[:HALO, :]
        dbab = dba_ref[...].astype(BF16)
        xhat, r = _rms_hat(x_ref[...])
        nw = nw_ref[...]
        n = (xhat * nw).astype(BF16)
        acc[D_MAIN:, :] += _dot_tn_bf(dbab, n)
        dn = jnp.dot(dbab, wt_vmem[D_MAIN:, :], preferred_element_type=F32)
        for cb, d in enumerate((dup, dzp_ref[...], dxs[0], dxs[1], dxs[2], ddz_ref[...])):
            rows = slice(cb * D_POOL, (cb + 1) * D_POOL)
            dpart = d.astype(BF16)
            acc[rows, :] += _dot_tn_bf(dpart, n)
            dn = dn + jnp.dot(dpart, wt_vmem[rows, :], preferred_element_type=F32)
        gnw_ref[...] += jnp.sum(dn * xhat, axis=0, keepdims=True)
        dxh = dn * nw
        gx_ref[...] = dh_ref[...] + r * (dxh - xhat * jnp.mean(dxh * xhat, axis=-1, keepdims=True))

        @pl.when(j == nstep - 1)
        def _():
            def out(d):
                return pltpu.make_async_copy(blk.at[d % 2], p_hbm.at[d], osem.at[d % 2])
            for d in range(N_DEV):
                if d >= 2:
                    out(d - 2).wait()
                blk[d % 2] = acc[W_IN_SHARD * d:W_IN_SHARD * (d + 1), :]
                out(d).start()
            out(N_DEV - 2).wait()
            out(N_DEV - 1).wait()

    rev = lambda j: (nstep - 1 - j, 0)
    part = pl.BlockSpec((tm, D_POOL), rev)
    full = pl.BlockSpec((tm, D_MODEL), rev)
    row = pl.BlockSpec((1, D_MODEL), lambda j: (0, 0))
    return pl.pallas_call(
        body, name="in_proj_bwd", grid=(nstep,),
        out_shape=(jax.ShapeDtypeStruct((s, D_MODEL), F32),
                   jax.ShapeDtypeStruct((N_DEV, W_IN_SHARD, D_MODEL), F32), jax.ShapeDtypeStruct((1, D_MODEL), F32)),
        in_specs=[part] * 6 + [pl.BlockSpec((tm, 128), rev), full, full, row,
                               pl.BlockSpec((CONV_WIDTH, 3 * D_DN), lambda j: (0, 0)),
                               pl.BlockSpec(memory_space=pl.ANY)],
        out_specs=(full, pl.BlockSpec(memory_space=pl.ANY), row),
        scratch_shapes=[pltpu.VMEM((D_IN_PAD, D_MODEL), BF16), pltpu.VMEM((D_IN_PAD, D_MODEL), F32),
                        pltpu.VMEM((2, W_IN_SHARD, D_MODEL), F32),
                        pltpu.VMEM((HALO, 3 * D_DN), F32), pltpu.VMEM((HALO, D_POOL), F32),
                        pltpu.SemaphoreType.DMA, pltpu.SemaphoreType.DMA((2,))],
        compiler_params=_cp(("arbitrary",)),
    )(dwin, dzp, dcq, dck, dcv, ddz, dba, x2, dh, norm_w, conv_full, wt_full)


def _adamw_math(w, g, m, v):
    m = ADAM_B1 * m + (1.0 - ADAM_B1) * g
    v = ADAM_B2 * v + (1.0 - ADAM_B2) * (g * g)
    m_hat = m / (1.0 - ADAM_B1 ** ADAM_STEP)
    v_hat = v / (1.0 - ADAM_B2 ** ADAM_STEP)
    delta = -ADAM_LR * (m_hat / (jnp.sqrt(v_hat) + ADAM_EPS) + ADAM_WD * w)
    return delta, m, v


def _adamw_sharded(params):
    k = len(params)

    def body(*refs):
        ins, outs = refs[:4 * k], refs[4 * k:]
        for p in range(k):
            w_ref, g_ref, m_ref, v_ref = ins[4 * p:4 * p + 4]
            go_ref = outs[4 * p]
            if g_ref.shape == w_ref.shape:
                go_ref[...] = g_ref[...]
            else:
                for j in range(FLAT_ROWS):
                    go_ref[pl.ds(j, W_IN_SHARD, stride=FLAT_ROWS), :] = g_ref[:, 128 * j:128 * (j + 1)]
            d, nm, nv = _adamw_math(w_ref[...], go_ref[...], m_ref[...], v_ref[...])
            outs[4 * p + 1][...] = d
            outs[4 * p + 2][...] = nm
            outs[4 * p + 3][...] = nv

    flat = [a for p in params for a in p]
    out_shape = tuple(jax.ShapeDtypeStruct(p[0].shape, F32) for p in params for _ in range(4))
    res = pl.pallas_call(body, name="adamw_sharded", out_shape=out_shape, compiler_params=_cp())(*flat)
    return [tuple(res[4 * p:4 * p + 4]) for p in range(k)]


def _adamw_replicated(gath_a, gath_b, pool, rows):
    nrow = len(rows)

    def body(*refs):
        ga_ref, gb_ref = refs[:2]
        ins = refs[2:2 + 3 * (nrow + 1)]
        outs = refs[2 + 3 * (nrow + 1):]

        def total(ref):
            g = ref[0]
            for d in range(1, N_DEV):
                g = g + ref[d]
            return g

        def update(g, wmv, o):
            w, m, v = (r[...] for r in wmv)
            dl, nm, nv = _adamw_math(w, g, m, v)
            o[0][...] = g
            o[1][...] = dl
            o[2][...] = nm
            o[3][...] = nv

        update(total(ga_ref), ins[:3], outs[:4])
        gb = total(gb_ref)
        for r in range(nrow):
            n = ins[3 * (r + 1)].shape[1]
            update(gb[r:r + 1, :n], ins[3 * (r + 1):3 * (r + 2)], outs[4 * (r + 1):4 * (r + 2)])
        outs[4 * (nrow + 1)][...] = gb[nrow:nrow + 1, 0:1]

    flat = list(pool) + [a for wmv in rows for a in wmv]
    out_shape = ((jax.ShapeDtypeStruct(pool[0].shape, F32),) * 4
                 + tuple(jax.ShapeDtypeStruct(wmv[0].shape, F32) for wmv in rows for _ in range(4))
                 + (jax.ShapeDtypeStruct((1, 1), F32),))
    res = pl.pallas_call(body, name="adamw_replicated", out_shape=out_shape, compiler_params=_cp())(
        gath_a, gath_b, *flat)
    return [res[4 * k:4 * k + 4] for k in range(nrow + 1)], res[-1]


_ROW_ORDER = ("norm_w", "final_norm_w", "pool_scale", "dn_norm_w", "a_log", "dt_bias")


def _pack_rows(vectors):
    out = [jnp.pad(v.reshape(-1), (0, D_MODEL - v.size)) for v in vectors]
    out += [jnp.zeros((D_MODEL,), F32)] * (8 - len(out))
    return jnp.stack(out, axis=0)


def _lane_row(vec4, start):
    return jnp.pad(vec4.reshape(-1), (start, 128 - start - vec4.size)).reshape(1, 128)


def kernel(x, norm_w, w_in, pool_w, pool_scale, conv_w, a_log, dt_bias, dn_norm_w, w_out, final_norm_w, loss_target, m_norm_w, m_w_in, m_pool_w, m_pool_scale, m_conv_w, m_a_log, m_dt_bias, m_dn_norm_w, m_w_out, m_final_norm_w, v_norm_w, v_w_in, v_pool_w, v_pool_scale, v_conv_w, v_a_log, v_dt_bias, v_dn_norm_w, v_w_out, v_final_norm_w):
    s = x.shape[1]
    tm = min(512, s)
    tmb = min(256, s)
    tp = min(512, s)
    x2 = x[0]
    tgt = loss_target[0]
    def to_flat(a):
        return a[0].reshape(FLAT_ROWS, 128, W_IN_SHARD).transpose(2, 0, 1).reshape(W_IN_SHARD * FLAT_ROWS, 128)

    def from_flat(f):
        return f.reshape(W_IN_SHARD, FLAT_ROWS, 128).transpose(1, 2, 0).reshape(1, D_MODEL, W_IN_SHARD)

    wf, m_wf, v_wf = to_flat(w_in), to_flat(m_w_in), to_flat(v_w_in)

    g_in, g_conv = _gather_weights(wf, conv_w[0])
    conv_full = g_conv.transpose(1, 0, 2).reshape(CONV_WIDTH, 3 * D_DN)
    alog_lane = _lane_row(a_log, DN_HEADS)
    dtb_lane = _lane_row(dt_bias, DN_HEADS)
    fnw = final_norm_w.reshape(1, D_MODEL)

    proj_main, proj_ba, y_pool, qn, kn, vv, gb, g_out, wt_full = _front(
        x2, norm_w, g_in, pool_w[0], pool_scale, conv_full, alog_lane, dtb_lane, w_out[0], tm)
    w_out_full = g_out.reshape(D_MODEL, D_MODEL)
    y_dn, states, dn_pre, dn_uw = _dn_scan_fwd(qn, kn, vv, gb, proj_main, dn_norm_w, DN_CHUNKS_PER_STEP)

    dh, dyp, dyd, g_wout, g_fnw, loss_part = _out_proj_loss(y_pool, y_dn, x2, tgt, w_out_full, fnw, tm)
    p_out = g_wout.reshape(N_DEV, D_MODEL // N_DEV, D_MODEL)
    dqn, dkn, dvv, dgb, ddz, g_dnw, gr_out = _dn_scan_bwd(qn, kn, vv, gb, proj_main, dn_norm_w, states, dn_pre, dn_uw,
                                                          dyd, p_out, DN_CHUNKS_PER_STEP)
    grad_x2, p_in, g_nw, g_pw, g_ps, g_conv_full, g_al, g_db = _back(
        proj_main, proj_ba, dyp, dqn, dkn, dvv, dgb, ddz, x2, dh, norm_w, pool_w[0], pool_scale, conv_full,
        alog_lane, dtb_lane, wt_full, tmb)

    p_conv = g_conv_full.reshape(CONV_WIDTH, N_DEV, 3 * D_DN // N_DEV).transpose(1, 0, 2)
    pack_a = g_pw.reshape(4 * POOL_GROUP, POOL_GROUP)
    pack_b = _pack_rows([g_nw, g_fnw, g_ps, g_dnw, g_al[0, DN_HEADS:2 * DN_HEADS], g_db[0, DN_HEADS:2 * DN_HEADS],
                         loss_part[0, :1]])
    gr_in, gr_conv, gath_a, gath_b = _reduce_grads((p_in, p_conv), pack_a, pack_b)

    r_in, r_out, r_conv = _adamw_sharded([(wf, gr_in, m_wf, v_wf), (w_out[0], gr_out, m_w_out[0], v_w_out[0]),
                                          (conv_w[0], gr_conv, m_conv_w[0], v_conv_w[0])])
    flat = lambda a: a.reshape(4 * POOL_GROUP, POOL_GROUP)
    row = lambda a: a.reshape(1, -1)
    vecs = {"norm_w": (norm_w, m_norm_w, v_norm_w), "final_norm_w": (final_norm_w, m_final_norm_w, v_final_norm_w),
            "pool_scale": (pool_scale, m_pool_scale, v_pool_scale), "dn_norm_w": (dn_norm_w, m_dn_norm_w, v_dn_norm_w),
            "a_log": (a_log, m_a_log, v_a_log), "dt_bias": (dt_bias, m_dt_bias, v_dt_bias)}
    res, loss = _adamw_replicated(gath_a, gath_b, (flat(pool_w), flat(m_pool_w), flat(v_pool_w)),
                                  [tuple(row(a) for a in vecs[nm]) for nm in _ROW_ORDER])
    r_pool = res[0]
    r_vec = dict(zip(_ROW_ORDER, res[1:]))

    def group(k):
        vec = lambda nm: r_vec[nm][k].reshape(vecs[nm][0].shape)
        return (vec("norm_w"), from_flat(r_in[k]), r_pool[k].reshape(pool_w.shape), vec("pool_scale"), r_conv[k][None],
                vec("a_log"), vec("dt_bias"), vec("dn_norm_w"), r_out[k][None], vec("final_norm_w"))

    return (loss[0, 0], grad_x2[None], *group(0), *group(1), *group(2), *group(3))
```

```python
import functools

import jax
import jax.numpy as jnp
from jax import lax
from jax.experimental import pallas as pl
from jax.experimental.pallas import tpu as pltpu

F32 = jnp.float32
BF16 = jnp.bfloat16
HI = lax.Precision.HIGHEST
MESH = pl.DeviceIdType.MESH

D_MODEL = 1024
D_POOL = 512
D_DN = 512
POOL_WINDOWS = (2, 4, 8, 16)
POOL_GROUP = 128
DN_HEADS = 4
DN_HEAD_DIM = 128
CONV_WIDTH = 4
CHUNK = 64
NORM_EPS = 1e-6
D_IN = 3080
D_MAIN = 3072
FLAT_ROWS = D_MODEL // 128
D_IN_PAD = D_MAIN + 128
N_DEV = 8
W_IN_SHARD = D_IN // N_DEV
HALO = 16
DN_CHUNKS_PER_STEP = 8

ADAM_LR = 0.001
ADAM_B1 = 0.9
ADAM_B2 = 0.999
ADAM_EPS = 1e-08
ADAM_WD = 0.01
ADAM_STEP = 10

VMEM_LIMIT = 56 * 1024 * 1024


def _cp(sem=None, vmem=VMEM_LIMIT):
    kw = {"vmem_limit_bytes": vmem}
    if sem is not None:
        kw["dimension_semantics"] = sem
    return pltpu.CompilerParams(**kw)


def _dot_bf(a, b):
    return jnp.dot(a.astype(BF16), b.astype(BF16), preferred_element_type=F32)


def _dot_nt_bf(a, b):
    return lax.dot_general(a.astype(BF16), b.astype(BF16), (((1,), (1,)), ((), ())), preferred_element_type=F32)


def _dot_tn_bf(a, b):
    return lax.dot_general(a.astype(BF16), b.astype(BF16), (((0,), (0,)), ((), ())), preferred_element_type=F32)


def _mm_raw(a, b, ca, cb, prec):
    off = a.ndim - 2
    dn = (((ca + off,), (cb + off,)), ((0,), (0,)) if off else ((), ()))
    if prec == "hi":
        return lax.dot_general(a, b, dn, precision=HI, preferred_element_type=F32)
    ah, bh = a.astype(BF16), b.astype(BF16)
    out = lax.dot_general(ah, bh, dn, preferred_element_type=F32)
    if prec == "x3":
        al = (a - ah.astype(F32)).astype(BF16)
        bl = (b - bh.astype(F32)).astype(BF16)
        out = out + lax.dot_general(ah, bl, dn, preferred_element_type=F32)
        out = out + lax.dot_general(al, bh, dn, preferred_element_type=F32)
    return out


@functools.partial(jax.custom_vjp, nondiff_argnums=(2, 3, 4, 5))
def _mm(a, b, ca, cb, prec, bprec):
    return _mm_raw(a, b, ca, cb, prec)


def _mm_fwd(a, b, ca, cb, prec, bprec):
    return _mm_raw(a, b, ca, cb, prec), (a, b)


def _mm_bwd(ca, cb, prec, bprec, res, dc):
    a, b = res
    da = _mm_raw(dc, b, 1, 1 - cb, bprec) if ca == 1 else _mm_raw(b, dc, 1 - cb, 1, bprec)
    db = _mm_raw(a, dc, 1 - ca, 0, bprec) if cb == 0 else _mm_raw(dc, a, 0, 1 - ca, bprec)
    return da, db


_mm.defvjp(_mm_fwd, _mm_bwd)


@functools.partial(jax.custom_vjp, nondiff_argnums=(1, 2))
def _tri_inv(a, prec, bprec):
    n = a.shape[-1]
    ii = lax.broadcasted_iota(jnp.int32, (n, n), 0)
    jj = lax.broadcasted_iota(jnp.int32, (n, n), 1)
    p = (ii == jj).astype(F32) - a
    b = _mm_raw(a, a, 1, 0, prec)
    for _ in range(4):
        pb = _mm_raw(jnp.concatenate([p, b], axis=-2), b, 1, 0, prec)
        p = p + pb[..., :n, :]
        b = pb[..., n:, :]
    return p + _mm_raw(p, b, 1, 0, prec)


def _tri_inv_fwd(a, prec, bprec):
    t = _tri_inv(a, prec, bprec)
    return t, t


def _tri_inv_bwd(prec, bprec, t, dt):
    return (-_mm_raw(_mm_raw(t, dt, 0, 0, bprec), t, 1, 1, bprec),)


_tri_inv.defvjp(_tri_inv_fwd, _tri_inv_bwd)

@functools.partial(jax.custom_vjp, nondiff_argnums=(3, 4, 5))
def _mm_known(a, b, out, ca, cb, bprec):
    return out


def _mm_known_fwd(a, b, out, ca, cb, bprec):
    return out, (a, b)


def _mm_known_bwd(ca, cb, bprec, res, dc):
    return _mm_bwd(ca, cb, None, bprec, res, dc) + (jnp.zeros_like(dc),)


_mm_known.defvjp(_mm_known_fwd, _mm_known_bwd)


@jax.custom_vjp
def _use_known(x, known):
    return known


_use_known.defvjp(lambda x, known: (known, None), lambda _, g: (g, jnp.zeros_like(g)))


@functools.partial(jax.custom_vjp, nondiff_argnums=(2,))
def _tri_inv_known(a, t, bprec):
    return t


def _tri_inv_known_fwd(a, t, bprec):
    return t, t


def _tri_inv_known_bwd(bprec, t, dt):
    return _tri_inv_bwd(None, bprec, t, dt) + (jnp.zeros_like(dt),)


_tri_inv_known.defvjp(_tri_inv_known_fwd, _tri_inv_known_bwd)

_DN_PREC = {"akq": ("bf16", "bf16"), "inv": ("bf16", "bf16"), "uw": ("bf16", "bf16"), "ws": ("bf16", "bf16"),
            "ov": ("bf16", "bf16"), "st": ("bf16", "bf16")}


def _silu(x):
    return x * jax.nn.sigmoid(x)


def _softplus(x):
    pos = x > 0.0
    return jnp.where(pos, x, 0.0) + jnp.log1p(jnp.exp(jnp.where(pos, -x, x)))


def _mesh_pos():
    return lax.axis_index("x"), lax.axis_index("y"), lax.axis_index("c")


def _dev_index(x, y, c):
    return 4 * x + 2 * y + c


def _all_gather_blocks(outs, send_sems, recv_sems):
    x, y, c = _mesh_pos()
    me = (x, y, c)
    sibling = (x, y, 1 - c)
    chips = [(1 - x, y), (x, 1 - y), (1 - x, 1 - y)]

    def copy(a, k, block, to):
        rows = outs[a].at[_dev_index(*block)]
        return pltpu.make_async_remote_copy(src_ref=rows, dst_ref=rows, send_sem=send_sems.at[a, k],
                                            recv_sem=recv_sems.at[a, k], device_id=to, device_id_type=MESH)

    n = len(outs)
    first = []
    for a in range(n):
        first.append(copy(a, 0, me, sibling))
        for j, chip in enumerate(chips):
            first.append(copy(a, 1 + j, me, (*chip, c)))
    for cp in first:
        cp.start()
    passed = []
    for j, chip in enumerate(chips):
        for a in range(n):
            copy(a, 1 + j, (*chip, c), me).wait_recv()
            fwd = copy(a, 4 + j, (*chip, c), sibling)
            fwd.start()
            passed.append(fwd)
    for a in range(n):
        copy(a, 0, sibling, me).wait_recv()
        for j, chip in enumerate(chips):
            copy(a, 4 + j, (*chip, 1 - c), me).wait_recv()
    for cp in first + passed:
        cp.wait_send()


def _peer_relations():
    x, y, c = _mesh_pos()
    flips = [(fx, fy, fc) for fx in (0, 1) for fy in (0, 1) for fc in (0, 1)][1:]
    peers = [(1 - x if fx else x, 1 - y if fy else y, 1 - c if fc else c) for fx, fy, fc in flips]
    return (x, y, c), peers


def _direct_gather_start(out_ref, send_sems, recv_sems):
    me, peers = _peer_relations()
    rows = out_ref.at[_dev_index(*me)]
    for k, peer in enumerate(peers):
        pltpu.make_async_remote_copy(src_ref=rows, dst_ref=rows, send_sem=send_sems.at[k], recv_sem=recv_sems.at[k],
                                     device_id=peer, device_id_type=MESH).start()


def _direct_gather_wait(out_ref, send_sems, recv_sems):
    me, peers = _peer_relations()
    for k, peer in enumerate(peers):
        rows = out_ref.at[_dev_index(*peer)]
        cp = pltpu.make_async_remote_copy(src_ref=rows, dst_ref=rows, send_sem=send_sems.at[k],
                                          recv_sem=recv_sems.at[k], device_id=peer, device_id_type=MESH)
        cp.wait_recv()
        cp.wait_send()


def _direct_scatter_start(send_ref, recv_ref, send_sems, recv_sems):
    me, peers = _peer_relations()
    for k, peer in enumerate(peers):
        pltpu.make_async_remote_copy(src_ref=send_ref.at[_dev_index(*peer)], dst_ref=recv_ref.at[k],
                                     send_sem=send_sems.at[k], recv_sem=recv_sems.at[k],
                                     device_id=peer, device_id_type=MESH).start()


def _direct_scatter_wait(send_ref, recv_ref, send_sems, recv_sems):
    me, peers = _peer_relations()
    for k, peer in enumerate(peers):
        cp = pltpu.make_async_remote_copy(src_ref=send_ref.at[_dev_index(*peer)], dst_ref=recv_ref.at[k],
                                          send_sem=send_sems.at[k], recv_sem=recv_sems.at[k],
                                          device_id=peer, device_id_type=MESH)
        cp.wait_recv()
        cp.wait_send()


def _gather_weights(w_in_flat, conv_blk):
    def body(win_ref, conv_ref, gin_ref, gconv_ref, send_sems, recv_sems):
        x, y, c = _mesh_pos()
        me = _dev_index(x, y, c)
        for j in range(FLAT_ROWS):
            gin_ref[me, :, 128 * j:128 * (j + 1)] = win_ref[pl.ds(j, W_IN_SHARD, stride=FLAT_ROWS), :].astype(BF16)
        gconv_ref[me] = conv_ref[...]
        _all_gather_blocks((gin_ref, gconv_ref), send_sems, recv_sems)

    vm = pl.BlockSpec(memory_space=pltpu.VMEM)
    return pl.pallas_call(
        body, name="gather_weights",
        out_shape=(jax.ShapeDtypeStruct((N_DEV, W_IN_SHARD, D_MODEL), BF16),
                   jax.ShapeDtypeStruct((N_DEV,) + conv_blk.shape, F32)),
        in_specs=[vm, vm], out_specs=(vm, vm),
        scratch_shapes=[pltpu.SemaphoreType.DMA((2, 7)), pltpu.SemaphoreType.DMA((2, 7))],
        compiler_params=_cp(),
    )(w_in_flat, conv_blk)


def _reduce_grads(big, pack_a, pack_b):
    nb = len(big)

    def body(*refs):
        srcs, (pa_ref, pb_ref), outs, (ga_ref, gb_ref) = (
            refs[:nb], refs[nb:nb + 2], refs[nb + 2:2 * nb + 2], refs[2 * nb + 2:2 * nb + 4])
        scr = refs[2 * nb + 4:]
        r1s, r2s, sbs, sts = (scr[k * nb:(k + 1) * nb] for k in range(4))
        s1_send, s1_recv, s2_send, s2_recv, ag_send, ag_recv, st_sem = scr[4 * nb:]
        x, y, c = _mesh_pos()
        me = (x, y, c)
        sibling = (x, y, 1 - c)
        rel = [(x, y), (1 - x, y), (x, 1 - y), (1 - x, 1 - y)]

        ga_ref[_dev_index(*me)] = pa_ref[...]
        gb_ref[_dev_index(*me)] = pb_ref[...]

        def p1(a, r, to):
            return pltpu.make_async_remote_copy(
                src_ref=srcs[a].at[_dev_index(*rel[r], 1 - c)], dst_ref=r1s[a].at[r],
                send_sem=s1_send.at[a, r], recv_sem=s1_recv.at[a, r], device_id=to, device_id_type=MESH)

        def p2(a, r, to):
            return pltpu.make_async_remote_copy(
                src_ref=sbs[a].at[r - 1], dst_ref=r2s[a].at[r - 1],
                send_sem=s2_send.at[a, r - 1], recv_sem=s2_recv.at[a, r - 1], device_id=to, device_id_type=MESH)

        def stage(a, r):
            return pltpu.make_async_copy(srcs[a].at[_dev_index(*rel[r], c)], sts[a].at[r % 2], st_sem.at[a, r % 2])

        sends1 = [p1(a, r, sibling) for a in range(nb) for r in range(4)]
        for cp in sends1:
            cp.start()
        sends2 = []
        for a in range(nb):
            stage(a, 1).start()
            for r in (1, 2, 3, 0):
                nxt = {1: 2, 2: 3, 3: 0, 0: None}[r]
                if nxt is not None:
                    stage(a, nxt).start()
                stage(a, r).wait()
                p1(a, r, me).wait_recv()
                chip_sum = r1s[a][r] + sts[a][r % 2]
                if r == 0:
                    r1s[a][0] = chip_sum
                else:
                    sbs[a][r - 1] = chip_sum.astype(BF16)
                    cp = p2(a, r, (*rel[r], c))
                    cp.start()
                    sends2.append(cp)
        _all_gather_blocks((ga_ref, gb_ref), ag_send, ag_recv)
        for a in range(nb):
            for r in (1, 2, 3):
                p2(a, r, me).wait_recv()
            outs[a][...] = ((r1s[a][0] + r2s[a][0].astype(F32)) + r2s[a][1].astype(F32)) + r2s[a][2].astype(F32)
        for cp in sends1 + sends2:
            cp.wait_send()

    vm = pl.BlockSpec(memory_space=pltpu.VMEM)
    hbm = pl.BlockSpec(memory_space=pl.ANY)
    blk = [p.shape[1:] for p in big]
    scratch = ([pltpu.VMEM((4,) + b, F32) for b in blk] + [pltpu.VMEM((3,) + b, BF16) for b in blk]
               + [pltpu.VMEM((3,) + b, BF16) for b in blk] + [pltpu.VMEM((2,) + b, F32) for b in blk]
               + [pltpu.SemaphoreType.DMA((nb, 4)), pltpu.SemaphoreType.DMA((nb, 4)),
                  pltpu.SemaphoreType.DMA((nb, 3)), pltpu.SemaphoreType.DMA((nb, 3)),
                  pltpu.SemaphoreType.DMA((2, 7)), pltpu.SemaphoreType.DMA((2, 7)),
                  pltpu.SemaphoreType.DMA((nb, 2))])
    return pl.pallas_call(
        body, name="reduce_grads",
        out_shape=tuple(jax.ShapeDtypeStruct(b, F32) for b in blk)
        + (jax.ShapeDtypeStruct((N_DEV,) + pack_a.shape, F32), jax.ShapeDtypeStruct((N_DEV,) + pack_b.shape, F32)),
        in_specs=[hbm] * nb + [vm, vm], out_specs=(vm,) * (nb + 2),
        scratch_shapes=scratch,
        compiler_params=_cp(),
    )(*big, pack_a, pack_b)


def _rms_hat(xf):
    r = lax.rsqrt(jnp.mean(xf * xf, axis=-1, keepdims=True) + NORM_EPS)
    return xf * r, r


def _in_proj(x2, norm_w, g_in, tm):
    s = x2.shape[0]

    def body(x_ref, nw_ref, g_hbm, pm_ref, pb_ref, wt_hbm, g_vmem, wt_vmem, sem):
        @pl.when(pl.program_id(0) == 0)
        def _():
            cp = pltpu.make_async_copy(g_hbm, g_vmem, sem)
            cp.start()
            wt_vmem[D_MAIN:, :] = jnp.zeros((D_IN_PAD - D_MAIN, D_MODEL), BF16)
            cp.wait()
            for d in range(N_DEV):
                wt_vmem[W_IN_SHARD * d:W_IN_SHARD * (d + 1), :] = g_vmem[d]
            out = pltpu.make_async_copy(wt_vmem, wt_hbm, sem)
            out.start()
            out.wait()
        xhat, _ = _rms_hat(x_ref[...])
        n = (xhat * nw_ref[...]).astype(BF16)
        pm_ref[...] = _dot_nt_bf(n, wt_vmem[:D_MAIN, :])
        pb_ref[...] = _dot_nt_bf(n, wt_vmem[D_MAIN:, :])

    return pl.pallas_call(
        body, name="in_proj", grid=(s // tm,),
        out_shape=(jax.ShapeDtypeStruct((s, D_MAIN), F32), jax.ShapeDtypeStruct((s, 128), F32),
                   jax.ShapeDtypeStruct((D_IN_PAD, D_MODEL), BF16)),
        in_specs=[pl.BlockSpec((tm, D_MODEL), lambda i: (i, 0)),
                  pl.BlockSpec((1, D_MODEL), lambda i: (0, 0)),
                  pl.BlockSpec(memory_space=pl.ANY)],
        out_specs=(pl.BlockSpec((tm, D_MAIN), lambda i: (i, 0)), pl.BlockSpec((tm, 128), lambda i: (i, 0)),
                   pl.BlockSpec(memory_space=pl.ANY)),
        scratch_shapes=[pltpu.VMEM((N_DEV, W_IN_SHARD, D_MODEL), BF16), pltpu.VMEM((D_IN_PAD, D_MODEL), BF16),
                        pltpu.SemaphoreType.DMA],
        compiler_params=_cp(("arbitrary",)),
    )(x2, norm_w, g_in)


def _shift_down(cur, prev_tail, s):
    ext = jnp.concatenate([prev_tail, cur], axis=0)
    return pltpu.roll(ext, s, 0)[HALO:, :]


def _shift_up(cur, next_head, s):
    ext = jnp.concatenate([cur, next_head], axis=0)
    n = ext.shape[0]
    return pltpu.roll(ext, n - s, 0)[:cur.shape[0], :]


def _pool_counts(i, tp, w):
    t = i * tp + lax.broadcasted_iota(jnp.int32, (tp, 1), 0)
    return jnp.minimum(t + 1, w).astype(F32)


def _pool_mix(u, u_prev_tail, i, tp):
    ext = jnp.concatenate([u_prev_tail, u], axis=0)
    w2 = ext + pltpu.roll(ext, 1, 0)
    w4 = w2 + pltpu.roll(w2, 2, 0)
    w8 = w4 + pltpu.roll(w4, 4, 0)
    w16 = w8 + pltpu.roll(w8, 8, 0)
    mixes = []
    for gi, (w, win) in enumerate(zip(POOL_WINDOWS, (w2, w4, w8, w16))):
        cols = slice(gi * POOL_GROUP, (gi + 1) * POOL_GROUP)
        mixes.append(win[HALO:, cols] / _pool_counts(i, tp, w) - u[:, cols])
    return mixes


def _prev_halo_spec(tp, width, col):
    per = tp // HALO
    return pl.BlockSpec((HALO, width), lambda i: (jnp.maximum(i * per - 1, 0), col))


def _next_halo_spec(tp, width, col, n):
    per = tp // HALO
    return pl.BlockSpec((HALO, width), lambda i: (jnp.minimum((i + 1) * per, n * per - 1), col))


def _prev_tail(ref, i):
    return jnp.where(i > 0, ref[...], 0.0)


def _next_head(ref, i, n):
    return jnp.where(i < n - 1, ref[...], 0.0)


def _pool_fwd(proj_main, pool_w, pool_scale, tp):
    s = proj_main.shape[0]

    def body(u_ref, up_ref, z_ref, pw_ref, ps_ref, y_ref):
        i = pl.program_id(0)
        u = u_ref[...]
        mixes = _pool_mix(u, _prev_tail(up_ref, i), i, tp)
        gate = ps_ref[...] * _silu(z_ref[...])
        for gi in range(4):
            cols = slice(gi * POOL_GROUP, (gi + 1) * POOL_GROUP)
            y_ref[:, cols] = _dot_bf(mixes[gi], pw_ref[gi]) * gate[:, cols]

    return pl.pallas_call(
        body, name="pool_fwd", grid=(s // tp,),
        out_shape=jax.ShapeDtypeStruct((s, D_POOL), F32),
        in_specs=[pl.BlockSpec((tp, D_POOL), lambda i: (i, 0)),
                  _prev_halo_spec(tp, D_POOL, 0),
                  pl.BlockSpec((tp, D_POOL), lambda i: (i, 1)),
                  pl.BlockSpec((4, POOL_GROUP, POOL_GROUP), lambda i: (0, 0, 0)),
                  pl.BlockSpec((1, D_POOL), lambda i: (0, 0))],
        out_specs=pl.BlockSpec((tp, D_POOL), lambda i: (i, 0)),
        compiler_params=_cp(("parallel",)),
    )(proj_main, proj_main, proj_main, pool_w, pool_scale)


def _conv_taps(cur, prev_tail):
    ext = jnp.concatenate([prev_tail, cur], axis=0)
    return [cur] + [pltpu.roll(ext, sft, 0)[HALO:, :] for sft in range(1, CONV_WIDTH)]


def _conv_of_taps(taps, w4):
    y = taps[0] * w4[CONV_WIDTH - 1:CONV_WIDTH, :]
    for sft in range(1, CONV_WIDTH):
        y = y + taps[sft] * w4[CONV_WIDTH - 1 - sft:CONV_WIDTH - sft, :]
    return y


def _conv_fwd(cur, prev_tail, w4):
    ext = jnp.concatenate([prev_tail, cur], axis=0)
    y = ext * w4[CONV_WIDTH - 1:CONV_WIDTH, :]
    for sft in range(1, CONV_WIDTH):
        y = y + pltpu.roll(ext, sft, 0) * w4[CONV_WIDTH - 1 - sft:CONV_WIDTH - sft, :]
    return y[HALO:, :]


def _l2n_heads(t):
    parts = []
    for h in range(DN_HEADS):
        th = t[:, h * DN_HEAD_DIM:(h + 1) * DN_HEAD_DIM]
        parts.append(th * lax.rsqrt(jnp.sum(th * th, axis=-1, keepdims=True) + NORM_EPS))
    return jnp.concatenate(parts, axis=1)


def _post_conv(yq, yk, yv):
    return _l2n_heads(_silu(yq)), _l2n_heads(_silu(yk)), _silu(yv)


def _gates(ba, alog_lane, dtb_lane):
    lane = lax.broadcasted_iota(jnp.int32, ba.shape, 1)
    beta = jax.nn.sigmoid(ba)
    g = -jnp.exp(alog_lane) * _softplus(ba + dtb_lane)
    return jnp.where(lane < DN_HEADS, beta, jnp.where(lane < 2 * DN_HEADS, g, 0.0))


def _front(x2, norm_w, g_in, pool_w, pool_scale, conv_full, alog_lane, dtb_lane, w_out_blk, tm):
    s = x2.shape[0]

    def body(x_ref, nw_ref, pw_ref, ps_ref, cw_ref, al_ref, db_ref, wo_ref, g_hbm,
             pm_ref, pb_ref, yp_ref, qn_ref, kn_ref, vv_ref, gb_ref, gwo_hbm, wt_hbm,
             g_vmem, wt_vmem, tail_u, tail_qkv, gwo_ref, sem, wo_send, wo_recv):
        i = pl.program_id(0)

        @pl.when(i == 0)
        def _():
            gwo_ref[_dev_index(*_mesh_pos())] = wo_ref[...].astype(BF16)
            _direct_gather_start(gwo_ref, wo_send, wo_recv)
            cp = pltpu.make_async_copy(g_hbm, g_vmem, sem)
            cp.start()
            wt_vmem[D_MAIN:, :] = jnp.zeros((D_IN_PAD - D_MAIN, D_MODEL), BF16)
            tail_u[...] = jnp.zeros_like(tail_u)
            tail_qkv[...] = jnp.zeros_like(tail_qkv)
            cp.wait()
            for d in range(N_DEV):
                wt_vmem[W_IN_SHARD * d:W_IN_SHARD * (d + 1), :] = g_vmem[d]
            out = pltpu.make_async_copy(wt_vmem, wt_hbm, sem)
            out.start()
            out.wait()
        xhat, _ = _rms_hat(x_ref[...])
        n = (xhat * nw_ref[...]).astype(BF16)
        pm_ref[...] = _dot_nt_bf(n, wt_vmem[:D_MAIN, :])
        pb = _dot_nt_bf(n, wt_vmem[D_MAIN:, :])
        pb_ref[...] = pb
        u = pm_ref[:, :D_POOL]
        mixes = _pool_mix(u, tail_u[...], i, tm)
        tail_u[...] = u[tm - HALO:, :]
        gate = ps_ref[...] * _silu(pm_ref[:, D_POOL:2 * D_POOL])
        for gi in range(4):
            cols = slice(gi * POOL_GROUP, (gi + 1) * POOL_GROUP)
            yp_ref[:, cols] = _dot_bf(mixes[gi], pw_ref[gi]) * gate[:, cols]
        ys = []
        for c in range(3):
            cols = slice(c * D_DN, (c + 1) * D_DN)
            cur = pm_ref[:, 2 * D_POOL + c * D_DN:2 * D_POOL + (c + 1) * D_DN]
            ys.append(_conv_fwd(cur, tail_qkv[:, cols], cw_ref[:, cols]))
            tail_qkv[:, cols] = cur[tm - HALO:, :]
        qn, kn, vv = _post_conv(*ys)
        qn_ref[...] = qn
        kn_ref[...] = kn
        vv_ref[...] = vv
        gb_ref[...] = _gates(pb, al_ref[...], db_ref[...])

        @pl.when(i == s // tm - 1)
        def _():
            _direct_gather_wait(gwo_ref, wo_send, wo_recv)
            out = pltpu.make_async_copy(gwo_ref, gwo_hbm, sem)
            out.start()
            out.wait()

    tile = pl.BlockSpec((tm, D_DN), lambda i: (i, 0))
    lanes = pl.BlockSpec((tm, 128), lambda i: (i, 0))
    row = pl.BlockSpec((1, 128), lambda i: (0, 0))
    return pl.pallas_call(
        body, name="front", grid=(s // tm,),
        out_shape=(jax.ShapeDtypeStruct((s, D_MAIN), F32), jax.ShapeDtypeStruct((s, 128), F32),
                   jax.ShapeDtypeStruct((s, D_POOL), F32), jax.ShapeDtypeStruct((s, D_DN), F32),
                   jax.ShapeDtypeStruct((s, D_DN), F32), jax.ShapeDtypeStruct((s, D_DN), F32),
                   jax.ShapeDtypeStruct((s, 128), F32), jax.ShapeDtypeStruct((N_DEV,) + w_out_blk.shape, BF16),
                   jax.ShapeDtypeStruct((D_IN_PAD, D_MODEL), BF16)),
        in_specs=[pl.BlockSpec((tm, D_MODEL), lambda i: (i, 0)),
                  pl.BlockSpec((1, D_MODEL), lambda i: (0, 0)),
                  pl.BlockSpec((4, POOL_GROUP, POOL_GROUP), lambda i: (0, 0, 0)),
                  pl.BlockSpec((1, D_POOL), lambda i: (0, 0)),
                  pl.BlockSpec((CONV_WIDTH, 3 * D_DN), lambda i: (0, 0)), row, row,
                  pl.BlockSpec(memory_space=pltpu.VMEM), pl.BlockSpec(memory_space=pl.ANY)],
        out_specs=(pl.BlockSpec((tm, D_MAIN), lambda i: (i, 0)), lanes, tile, tile, tile, tile, lanes,
                   pl.BlockSpec(memory_space=pl.ANY), pl.BlockSpec(memory_space=pl.ANY)),
        scratch_shapes=[pltpu.VMEM((N_DEV, W_IN_SHARD, D_MODEL), BF16), pltpu.VMEM((D_IN_PAD, D_MODEL), BF16),
                        pltpu.VMEM((HALO, D_POOL), F32), pltpu.VMEM((HALO, 3 * D_DN), F32),
                        pltpu.VMEM((N_DEV,) + w_out_blk.shape, BF16),
                        pltpu.SemaphoreType.DMA, pltpu.SemaphoreType.DMA((7,)), pltpu.SemaphoreType.DMA((7,))],
        compiler_params=_cp(("arbitrary",)),
    )(x2, norm_w, pool_w, pool_scale, conv_full, alog_lane, dtb_lane, w_out_blk, g_in)


def _qkv_specs(tp, which):
    def spec(col, n=None):
        if which == 0:
            return pl.BlockSpec((tp, D_DN), lambda i: (i, col))
        if which < 0:
            return _prev_halo_spec(tp, D_DN, col)
        return _next_halo_spec(tp, D_DN, col, n)
    return spec


def _dn_pre(proj_main, proj_ba, conv_full, alog_lane, dtb_lane, tp):
    s = proj_main.shape[0]

    def body(q_ref, k_ref, v_ref, qp_ref, kp_ref, vp_ref, cw_ref, ba_ref, al_ref, db_ref,
             qn_ref, kn_ref, vv_ref, gb_ref):
        i = pl.program_id(0)
        ys = []
        for j, (cur, prev) in enumerate(((q_ref, qp_ref), (k_ref, kp_ref), (v_ref, vp_ref))):
            ys.append(_conv_fwd(cur[...], _prev_tail(prev, i), cw_ref[:, j * D_DN:(j + 1) * D_DN]))
        qn, kn, vv = _post_conv(*ys)
        qn_ref[...] = qn
        kn_ref[...] = kn
        vv_ref[...] = vv
        gb_ref[...] = _gates(ba_ref[...], al_ref[...], db_ref[...])

    cur, prev = _qkv_specs(tp, 0), _qkv_specs(tp, -1)
    row = pl.BlockSpec((1, 128), lambda i: (0, 0))
    tile = pl.BlockSpec((tp, D_DN), lambda i: (i, 0))
    return pl.pallas_call(
        body, name="dn_pre", grid=(s // tp,),
        out_shape=(jax.ShapeDtypeStruct((s, D_DN), F32),) * 3 + (jax.ShapeDtypeStruct((s, 128), F32),),
        in_specs=[cur(2), cur(3), cur(4), prev(2), prev(3), prev(4),
                  pl.BlockSpec((CONV_WIDTH, 3 * D_DN), lambda i: (0, 0)),
                  pl.BlockSpec((tp, 128), lambda i: (i, 0)), row, row],
        out_specs=(tile, tile, tile, pl.BlockSpec((tp, 128), lambda i: (i, 0))),
        compiler_params=_cp(("parallel",)),
    )(proj_main, proj_main, proj_main, proj_main, proj_main, proj_main, conv_full, proj_ba, alog_lane, dtb_lane)


def _dn_block(q, k, v, gcol, bcol, state, dz, nw, known=None):
    nb, n, d = q.shape
    ii = lax.broadcasted_iota(jnp.int32, (n, n), 0)
    jj = lax.broadcasted_iota(jnp.int32, (n, n), 1)
    lower = ii >= jj
    eye = (ii == jj).astype(F32)
    g_row = jnp.sum(eye * gcol, axis=1, keepdims=True)
    gc_col = jnp.sum(jnp.where(lower, g_row, 0.0), axis=2, keepdims=True)
    gc_row = jnp.sum(eye * gc_col, axis=1, keepdims=True)
    decay = jnp.where(lower, jnp.exp(jnp.where(lower, gc_col - gc_row, 0.0)), 0.0)
    kb = k * bcol
    vb = v * bcol
    qs = q * (DN_HEAD_DIM ** -0.5)
    egc = jnp.exp(gc_col)
    kq = jnp.concatenate([kb, qs], axis=1)
    vk = jnp.concatenate([vb, kb * egc], axis=2)
    if known is None:
        akq = _mm(kq, k, 1, 1, *_DN_PREC["akq"])
    else:
        akq = _mm_known(kq, k, known[0][:, :, :n].astype(F32), 1, 1, _DN_PREC["akq"][1])
    a = jnp.where(ii > jj, akq[:, :n] * decay, 0.0)
    qk = akq[:, n:] * decay
    if known is None:
        t = _tri_inv(a, *_DN_PREC["inv"])
        uw = _mm(t, vk, 1, 0, *_DN_PREC["uw"])
    else:
        t = _tri_inv_known(a, known[0][:, :n, n:].astype(F32), _DN_PREC["inv"][1])
        uw = _mm_known(t, vk, known[1], 1, 0, _DN_PREC["uw"][1])
    pre = jnp.concatenate([akq, jnp.concatenate([t, jnp.zeros_like(t)], axis=1)], axis=2)
    wq = jnp.concatenate([uw[:, :, d:], qs * egc], axis=1)
    g_last = gc_col[:, n - 1:n, :]
    k_dec = k * jnp.exp(g_last - gc_col)
    e_last = jnp.exp(g_last)
    os_, starts = [], []
    for c in range(nb // DN_HEADS):
        sl = slice(c * DN_HEADS, (c + 1) * DN_HEADS)
        if known is not None and c > 0:
            state = _use_known(state, known[2][sl])
        starts.append(state)
        ws = _mm(wq[sl], state, 1, 0, *_DN_PREC["ws"])
        v_new = uw[sl, :, :d] - ws[:, :n]
        os_.append(ws[:, n:] + _mm(qk[sl], v_new, 1, 0, *_DN_PREC["ov"]))
        state = state * e_last[sl] + _mm(k_dec[sl], v_new, 0, 0, *_DN_PREC["st"])
    o = jnp.concatenate(os_, axis=0)
    y = o * lax.rsqrt(jnp.mean(o * o, axis=-1, keepdims=True) + NORM_EPS) * nw * _silu(dz)
    return y, state, (pre, uw, jnp.concatenate(starts, axis=0))


def _dn_block_args(gc, q_ref, k_ref, v_ref, gb_ref, dz_ref):
    qs, ks, vs, gs, bs, zs = [], [], [], [], [], []
    for cc in range(gc):
        r = slice(cc * CHUNK, (cc + 1) * CHUNK)
        gbv = gb_ref[r, :]
        for h in range(DN_HEADS):
            cols = slice(h * DN_HEAD_DIM, (h + 1) * DN_HEAD_DIM)
            qs.append(q_ref[r, cols])
            ks.append(k_ref[r, cols])
            vs.append(v_ref[r, cols])
            zs.append(dz_ref[r, cols])
            gs.append(gbv[:, DN_HEADS + h:DN_HEADS + h + 1])
            bs.append(gbv[:, h:h + 1])
    return tuple(jnp.stack(t, axis=0) for t in (qs, ks, vs, gs, bs, zs))


def _dn_scan_fwd(qn, kn, vv, gb, proj_main, dn_norm_w, gc):
    s = qn.shape[0]
    nchunk = s // CHUNK
    rows = gc * CHUNK

    def body(q_ref, k_ref, v_ref, gb_ref, dz_ref, nw_ref, y_ref, ss_ref, pre_ref, uw_ref, state):
        @pl.when(pl.program_id(0) == 0)
        def _():
            state[...] = jnp.zeros_like(state)
        q, k, v, gcol, bcol, dz = _dn_block_args(gc, q_ref, k_ref, v_ref, gb_ref, dz_ref)
        y, new, (pre, uw, starts) = _dn_block(q, k, v, gcol, bcol, state[...], dz, nw_ref[...])
        state[...] = new
        ss_ref[...] = starts
        pre_ref[...] = pre
        uw_ref[...] = uw
        for cc in range(gc):
            for h in range(DN_HEADS):
                y_ref[cc * CHUNK:(cc + 1) * CHUNK, h * DN_HEAD_DIM:(h + 1) * DN_HEAD_DIM] = y[cc * DN_HEADS + h]

    tile = pl.BlockSpec((rows, D_DN), lambda i: (i, 0))
    return pl.pallas_call(
        body, name="dn_scan_fwd", grid=(nchunk // gc,),
        out_shape=(jax.ShapeDtypeStruct((s, D_DN), F32),
                   jax.ShapeDtypeStruct((nchunk * DN_HEADS, DN_HEAD_DIM, DN_HEAD_DIM), F32),
                   jax.ShapeDtypeStruct((nchunk * DN_HEADS, 2 * CHUNK, 2 * CHUNK), F32),
                   jax.ShapeDtypeStruct((nchunk * DN_HEADS, CHUNK, 2 * DN_HEAD_DIM), F32)),
        in_specs=[tile, tile, tile, pl.BlockSpec((rows, 128), lambda i: (i, 0)),
                  pl.BlockSpec((rows, D_DN), lambda i: (i, 5)), pl.BlockSpec((1, 128), lambda i: (0, 0))],
        out_specs=(tile, pl.BlockSpec((gc * DN_HEADS, DN_HEAD_DIM, DN_HEAD_DIM), lambda i: (i, 0, 0)),
                   pl.BlockSpec((gc * DN_HEADS, 2 * CHUNK, 2 * CHUNK), lambda i: (i, 0, 0)),
                   pl.BlockSpec((gc * DN_HEADS, CHUNK, 2 * DN_HEAD_DIM), lambda i: (i, 0, 0))),
        scratch_shapes=[pltpu.VMEM((DN_HEADS, DN_HEAD_DIM, DN_HEAD_DIM), F32)],
        compiler_params=_cp(("arbitrary",)),
    )(qn, kn, vv, gb, proj_main, dn_norm_w)


def _out_proj_loss(y_pool, y_dn, x2, tgt, w_out_full, fnw, tm):
    s = x2.shape[0]

    def body(yp_ref, yd_ref, x_ref, t_ref, wo_ref, fw_ref,
             dh_ref, dyp_ref, dyd_ref, gwo_ref, gfw_ref, loss_ref):
        @pl.when(pl.program_id(0) == 0)
        def _():
            gwo_ref[...] = jnp.zeros_like(gwo_ref)
            gfw_ref[...] = jnp.zeros_like(gfw_ref)
            loss_ref[...] = jnp.zeros_like(loss_ref)
        y = jnp.concatenate([yp_ref[...], yd_ref[...]], axis=1).astype(BF16)
        wo = wo_ref[...]
        h = x_ref[...] + jnp.dot(y, wo, preferred_element_type=F32)
        hn, r = _rms_hat(h)
        fw = fw_ref[...]
        err = hn * fw - t_ref[...]
        loss_ref[...] += 0.5 * jnp.sum(jnp.sum(err * err, axis=-1, keepdims=True) / D_MODEL, axis=0, keepdims=True)
        dout = err / D_MODEL
        gfw_ref[...] += jnp.sum(dout * hn, axis=0, keepdims=True)
        dhn = dout * fw
        dh = r * (dhn - hn * jnp.mean(dhn * hn, axis=-1, keepdims=True))
        dh_ref[...] = dh
        dhb = dh.astype(BF16)
        dy = _dot_nt_bf(dhb, wo)
        dyp_ref[...] = dy[:, :D_POOL]
        dyd_ref[...] = dy[:, D_POOL:]
        gwo_ref[...] += _dot_tn_bf(y, dhb)

    half = pl.BlockSpec((tm, D_POOL), lambda i: (i, 0))
    full = pl.BlockSpec((tm, D_MODEL), lambda i: (i, 0))
    return pl.pallas_call(
        body, name="out_proj_loss", grid=(s // tm,),
        out_shape=(jax.ShapeDtypeStruct((s, D_MODEL), F32), jax.ShapeDtypeStruct((s, D_POOL), F32),
                   jax.ShapeDtypeStruct((s, D_DN), F32), jax.ShapeDtypeStruct((D_MODEL, D_MODEL), F32),
                   jax.ShapeDtypeStruct((1, D_MODEL), F32), jax.ShapeDtypeStruct((1, 128), F32)),
        in_specs=[half, half, full, full, pl.BlockSpec((D_MODEL, D_MODEL), lambda i: (0, 0)),
                  pl.BlockSpec((1, D_MODEL), lambda i: (0, 0))],
        out_specs=(full, half, half, pl.BlockSpec((D_MODEL, D_MODEL), lambda i: (0, 0)),
                   pl.BlockSpec((1, D_MODEL), lambda i: (0, 0)), pl.BlockSpec((1, 128), lambda i: (0, 0))),
        compiler_params=_cp(("arbitrary",)),
    )(y_pool, y_dn, x2, tgt, w_out_full, fnw)


def _dn_scan_bwd(qn, kn, vv, gb, proj_main, dn_norm_w, states, pre, uw, dy_dn, p_out, gc):
    s = qn.shape[0]
    nchunk = s // CHUNK
    nstep = nchunk // gc
    rows = gc * CHUNK

    def body(q_ref, k_ref, v_ref, gb_ref, dz_ref, nw_ref, ss_ref, pre_ref, uw_ref, dy_ref, po_ref,
             dq_ref, dk_ref, dv_ref, dgb_ref, ddz_ref, dnw_ref, gro_ref, dstate, po_send, po_recv, rs_send, rs_recv):
        @pl.when(pl.program_id(0) == 0)
        def _():
            dstate[...] = jnp.zeros_like(dstate)
            dnw_ref[...] = jnp.zeros_like(dnw_ref)
            po_send[...] = po_ref[...].astype(BF16)
            _direct_scatter_start(po_send, po_recv, rs_send, rs_recv)

        @pl.when(pl.program_id(0) == nstep - 1)
        def _():
            _direct_scatter_wait(po_send, po_recv, rs_send, rs_recv)
            total = po_ref[_dev_index(*_mesh_pos())]
            for k in range(N_DEV - 1):
                total = total + po_recv[k].astype(F32)
            gro_ref[...] = total
        lane = lax.broadcasted_iota(jnp.int32, (CHUNK, 128), 1)
        q, k, v, gcol, bcol, dz = _dn_block_args(gc, q_ref, k_ref, v_ref, gb_ref, dz_ref)
        dy = jnp.stack([dy_ref[cc * CHUNK:(cc + 1) * CHUNK, h * DN_HEAD_DIM:(h + 1) * DN_HEAD_DIM]
                        for cc in range(gc) for h in range(DN_HEADS)], axis=0)
        known = (pre_ref[...], uw_ref[...], ss_ref[...])
        _, vjp = jax.vjp(lambda *a: _dn_block(*a, known=known)[:2], q, k, v, gcol, bcol, ss_ref[:DN_HEADS], dz,
                         nw_ref[...])
        dq, dk, dv, dg, db, dst, ddz, dnw = vjp((dy, dstate[...]))
        dstate[...] = dst
        dnw_ref[...] += dnw
        for cc in range(gc):
            r = slice(cc * CHUNK, (cc + 1) * CHUNK)
            dgb = jnp.zeros((CHUNK, 128), F32)
            for h in range(DN_HEADS):
                b = cc * DN_HEADS + h
                cols = slice(h * DN_HEAD_DIM, (h + 1) * DN_HEAD_DIM)
                for ref, val in zip((dq_ref, dk_ref, dv_ref, ddz_ref), (dq, dk, dv, ddz)):
                    ref[r, cols] = val[b]
                dgb = dgb + jnp.where(lane == h, db[b], 0.0) + jnp.where(lane == DN_HEADS + h, dg[b], 0.0)
            dgb_ref[r, :] = dgb

    rev = lambda i: (nstep - 1 - i, 0)
    tile = pl.BlockSpec((rows, D_DN), rev)
    lanes = pl.BlockSpec((rows, 128), rev)
    return pl.pallas_call(
        body, name="dn_scan_bwd", grid=(nstep,),
        out_shape=(jax.ShapeDtypeStruct((s, D_DN), F32),) * 3
        + (jax.ShapeDtypeStruct((s, 128), F32), jax.ShapeDtypeStruct((s, D_DN), F32),
           jax.ShapeDtypeStruct((1, 128), F32), jax.ShapeDtypeStruct(p_out.shape[1:], F32)),
        in_specs=[tile, tile, tile, lanes, pl.BlockSpec((rows, D_DN), lambda i: (nstep - 1 - i, 5)),
                  pl.BlockSpec((1, 128), lambda i: (0, 0)),
                  pl.BlockSpec((gc * DN_HEADS, DN_HEAD_DIM, DN_HEAD_DIM), lambda i: (nstep - 1 - i, 0, 0)),
                  pl.BlockSpec((gc * DN_HEADS, 2 * CHUNK, 2 * CHUNK), lambda i: (nstep - 1 - i, 0, 0)),
                  pl.BlockSpec((gc * DN_HEADS, CHUNK, 2 * DN_HEAD_DIM), lambda i: (nstep - 1 - i, 0, 0)), tile,
                  pl.BlockSpec(memory_space=pltpu.VMEM)],
        out_specs=(tile, tile, tile, lanes, tile, pl.BlockSpec((1, 128), lambda i: (0, 0)),
                   pl.BlockSpec(memory_space=pltpu.VMEM)),
        scratch_shapes=[pltpu.VMEM((DN_HEADS, DN_HEAD_DIM, DN_HEAD_DIM), F32),
                        pltpu.VMEM(p_out.shape, BF16), pltpu.VMEM((N_DEV - 1,) + p_out.shape[1:], BF16),
                        pltpu.SemaphoreType.DMA((7,)), pltpu.SemaphoreType.DMA((7,))],
        compiler_params=_cp(("arbitrary",)),
    )(qn, kn, vv, gb, proj_main, dn_norm_w, states, pre, uw, dy_dn, p_out)


def _dn_pre_bwd1(proj_main, proj_ba, conv_full, alog_lane, dtb_lane, dqn, dkn, dvv, dgb, tp):
    s = proj_main.shape[0]

    def body(q_ref, k_ref, v_ref, qp_ref, kp_ref, vp_ref, cw_ref, ba_ref, al_ref, db_ref,
             dqn_ref, dkn_ref, dvv_ref, dgb_ref,
             dcq_ref, dck_ref, dcv_ref, dba_ref, dcw_ref, dal_ref, ddb_ref):
        i = pl.program_id(0)

        @pl.when(i == 0)
        def _():
            dcw_ref[...] = jnp.zeros_like(dcw_ref)
            dal_ref[...] = jnp.zeros_like(dal_ref)
            ddb_ref[...] = jnp.zeros_like(ddb_ref)
        curs = (q_ref[...], k_ref[...], v_ref[...])
        tails = (_prev_tail(qp_ref, i), _prev_tail(kp_ref, i), _prev_tail(vp_ref, i))
        ys = [_conv_fwd(curs[j], tails[j], cw_ref[:, j * D_DN:(j + 1) * D_DN]) for j in range(3)]
        _, vjp = jax.vjp(_post_conv, *ys)
        dys = vjp((dqn_ref[...], dkn_ref[...], dvv_ref[...]))
        for j, (dy, out) in enumerate(zip(dys, (dcq_ref, dck_ref, dcv_ref))):
            out[...] = dy
            for sft in range(CONV_WIDTH):
                xs = curs[j] if sft == 0 else _shift_down(curs[j], tails[j], sft)
                row = CONV_WIDTH - 1 - sft
                dcw_ref[row:row + 1, j * D_DN:(j + 1) * D_DN] += jnp.sum(dy * xs, axis=0, keepdims=True)
        _, gvjp = jax.vjp(_gates, ba_ref[...], al_ref[...], db_ref[...])
        dba, dal, ddb = gvjp(dgb_ref[...])
        dba_ref[...] = dba
        dal_ref[...] += dal
        ddb_ref[...] += ddb

    cur, prev = _qkv_specs(tp, 0), _qkv_specs(tp, -1)
    row = pl.BlockSpec((1, 128), lambda i: (0, 0))
    tile = pl.BlockSpec((tp, D_DN), lambda i: (i, 0))
    lanes = pl.BlockSpec((tp, 128), lambda i: (i, 0))
    cw = pl.BlockSpec((CONV_WIDTH, 3 * D_DN), lambda i: (0, 0))
    return pl.pallas_call(
        body, name="dn_pre_bwd1", grid=(s // tp,),
        out_shape=(jax.ShapeDtypeStruct((s, D_DN), F32),) * 3
        + (jax.ShapeDtypeStruct((s, 128), F32), jax.ShapeDtypeStruct((CONV_WIDTH, 3 * D_DN), F32),
           jax.ShapeDtypeStruct((1, 128), F32), jax.ShapeDtypeStruct((1, 128), F32)),
        in_specs=[cur(2), cur(3), cur(4), prev(2), prev(3), prev(4), cw, lanes, row, row, tile, tile, tile, lanes],
        out_specs=(tile, tile, tile, lanes, cw, row, row),
        compiler_params=_cp(("arbitrary",)),
    )(proj_main, proj_main, proj_main, proj_main, proj_main, proj_main, conv_full, proj_ba, alog_lane, dtb_lane,
      dqn, dkn, dvv, dgb)


def _pool_bwd1(proj_main, pool_w, pool_scale, dyp, tp):
    s = proj_main.shape[0]

    def body(u_ref, up_ref, z_ref, pw_ref, ps_ref, dy_ref, dz_ref, dwin_ref, dpw_ref, dps_ref):
        i = pl.program_id(0)

        @pl.when(i == 0)
        def _():
            dpw_ref[...] = jnp.zeros_like(dpw_ref)
            dps_ref[...] = jnp.zeros_like(dps_ref)
        u = u_ref[...]
        z = z_ref[...]
        dy = dy_ref[...]
        ps = ps_ref[...]
        mixes = _pool_mix(u, _prev_tail(up_ref, i), i, tp)
        sg = jax.nn.sigmoid(z)
        sz = z * sg
        dsz = sg * (1.0 + z * (1.0 - sg))
        for gi, w in enumerate(POOL_WINDOWS):
            cols = slice(gi * POOL_GROUP, (gi + 1) * POOL_GROUP)
            mixw = _dot_bf(mixes[gi], pw_ref[gi])
            dmixw = dy[:, cols] * ps[:, cols] * sz[:, cols]
            dps_ref[:, cols] += jnp.sum(dy[:, cols] * mixw * sz[:, cols], axis=0, keepdims=True)
            dz_ref[:, cols] = dy[:, cols] * mixw * ps[:, cols] * dsz[:, cols]
            dpw_ref[gi] += _dot_tn_bf(mixes[gi], dmixw)
            dmix = _dot_nt_bf(dmixw, pw_ref[gi])
            dwin_ref[:, cols] = dmix / _pool_counts(i, tp, w)

    tile = pl.BlockSpec((tp, D_POOL), lambda i: (i, 0))
    pw = pl.BlockSpec((4, POOL_GROUP, POOL_GROUP), lambda i: (0, 0, 0))
    ps = pl.BlockSpec((1, D_POOL), lambda i: (0, 0))
    return pl.pallas_call(
        body, name="pool_bwd1", grid=(s // tp,),
        out_shape=(jax.ShapeDtypeStruct((s, D_POOL), F32), jax.ShapeDtypeStruct((s, D_POOL), F32),
                   jax.ShapeDtypeStruct((4, POOL_GROUP, POOL_GROUP), F32), jax.ShapeDtypeStruct((1, D_POOL), F32)),
        in_specs=[tile, _prev_halo_spec(tp, D_POOL, 0),
                  pl.BlockSpec((tp, D_POOL), lambda i: (i, 1)), pw, ps, tile],
        out_specs=(tile, tile, pw, ps),
        compiler_params=_cp(("arbitrary",)),
    )(proj_main, proj_main, proj_main, pool_w, pool_scale, dyp)


def _back(proj_main, proj_ba, dyp, dqn, dkn, dvv, dgb, ddz, x2, dh, norm_w, pool_w, pool_scale, conv_full,
          alog_lane, dtb_lane, wt_full, tm):
    s = x2.shape[0]
    nstep = s // tm
    per = tm // HALO

    def body(u_ref, z_ref, q_ref, k_ref, v_ref, up_ref, qp_ref, kp_ref, vp_ref, ba_ref,
             dyp_ref, dqn_ref, dkn_ref, dvv_ref, dgb_ref, ddz_ref, x_ref, dh_ref,
             nw_ref, pw_ref, ps_ref, cw_ref, al_ref, db_ref, wt_hbm,
             gx_ref, p_hbm, gnw_ref, dpw_ref, dps_ref, dcw_ref, dal_ref, ddb_ref,
             wt_vmem, acc, blk, head_dc, head_dw, sem, osem):
        j = pl.program_id(0)
        i = nstep - 1 - j

        @pl.when(j == 0)
        def _():
            cp = pltpu.make_async_copy(wt_hbm, wt_vmem, sem)
            cp.start()
            acc[...] = jnp.zeros_like(acc)
            for ref in (gnw_ref, dpw_ref, dps_ref, dcw_ref, dal_ref, ddb_ref, head_dc, head_dw):
                ref[...] = jnp.zeros_like(ref)
            cp.wait()

        u = u_ref[...]
        z = z_ref[...]
        dy = dyp_ref[...]
        ps = ps_ref[...]
        mixes = _pool_mix(u, _prev_tail(up_ref, i), i, tm)
        sg = jax.nn.sigmoid(z)
        sz = z * sg
        dsz = sg * (1.0 + z * (1.0 - sg))
        dzs, dwins = [], []
        for gi, w in enumerate(POOL_WINDOWS):
            cols = slice(gi * POOL_GROUP, (gi + 1) * POOL_GROUP)
            mixw = _dot_bf(mixes[gi], pw_ref[gi])
            dmixw = dy[:, cols] * ps[:, cols] * sz[:, cols]
            dps_ref[:, cols] += jnp.sum(dy[:, cols] * mixw * sz[:, cols], axis=0, keepdims=True)
            dzs.append(dy[:, cols] * mixw * ps[:, cols] * dsz[:, cols])
            dpw_ref[gi] += _dot_tn_bf(mixes[gi], dmixw)
            dwins.append(_dot_nt_bf(dmixw, pw_ref[gi]) / _pool_counts(i, tm, w))
        dzp = jnp.concatenate(dzs, axis=1)
        dw = jnp.concatenate(dwins, axis=1)
        ext = jnp.concatenate([dw, head_dw[...]], axis=0)
        m = ext.shape[0]
        a2 = ext + pltpu.roll(ext, m - 1, 0)
        a4 = a2 + pltpu.roll(a2, m - 2, 0)
        a8 = a4 + pltpu.roll(a4, m - 4, 0)
        a16 = a8 + pltpu.roll(a8, m - 8, 0)
        dup = jnp.concatenate(
            [acc_w[:tm, gi * POOL_GROUP:(gi + 1) * POOL_GROUP]
             - dw[:, gi * POOL_GROUP:(gi + 1) * POOL_GROUP] * _pool_counts(i, tm, w)
             for gi, (w, acc_w) in enumerate(zip(POOL_WINDOWS, (a2, a4, a8, a16)))], axis=1)
        head_dw[...] = dw[:HALO, :]

        curs = (q_ref[...], k_ref[...], v_ref[...])
        tails = (_prev_tail(qp_ref, i), _prev_tail(kp_ref, i), _prev_tail(vp_ref, i))
        taps = [_conv_taps(curs[c], tails[c]) for c in range(3)]
        ys = [_conv_of_taps(taps[c], cw_ref[:, c * D_DN:(c + 1) * D_DN]) for c in range(3)]
        _, vjp = jax.vjp(_post_conv, *ys)
        dys = vjp((dqn_ref[...], dkn_ref[...], dvv_ref[...]))
        dxs = []
        for c, dyc in enumerate(dys):
            cols = slice(c * D_DN, (c + 1) * D_DN)
            w4 = cw_ref[:, cols]
            for sft in range(CONV_WIDTH):
                row = CONV_WIDTH - 1 - sft
                dcw_ref[row:row + 1, cols] += jnp.sum(dyc * taps[c][sft], axis=0, keepdims=True)
            head = head_dc[:, cols]
            dx = dyc * w4[CONV_WIDTH - 1:CONV_WIDTH, :]
            for sft in range(1, CONV_WIDTH):
                dx = dx + _shift_up(dyc, head, sft) * w4[CONV_WIDTH - 1 - sft:CONV_WIDTH - sft, :]
            dxs.append(dx)
            head_dc[:, cols] = dyc[:HALO, :]
        _, gvjp = jax.vjp(_gates, ba_ref[...], al_ref[...], db_ref[...])
        dba, dal, ddb = gvjp(dgb_ref[...])
        dal_ref[...] += dal
        ddb_ref[...] += ddb

        dbab = dba.astype(BF16)
        xhat, r = _rms_hat(x_ref[...])
        nw = nw_ref[...]
        n = (xhat * nw).astype(BF16)
        acc[D_MAIN:, :] += _dot_tn_bf(dbab, n)
        dn = jnp.dot(dbab, wt_vmem[D_MAIN:, :], preferred_element_type=F32)
        for cb, d in enumerate((dup, dzp, dxs[0], dxs[1], dxs[2], ddz_ref[...])):
            rows = slice(cb * D_POOL, (cb + 1) * D_POOL)
            dpart = d.astype(BF16)
            acc[rows, :] += _dot_tn_bf(dpart, n)
            dn = dn + jnp.dot(dpart, wt_vmem[rows, :], preferred_element_type=F32)
        gnw_ref[...] += jnp.sum(dn * xhat, axis=0, keepdims=True)
        dxh = dn * nw
        gx_ref[...] = dh_ref[...] + r * (dxh - xhat * jnp.mean(dxh * xhat, axis=-1, keepdims=True))

        @pl.when(j == nstep - 1)
        def _():
            def out(d):
                return pltpu.make_async_copy(blk.at[d % 2], p_hbm.at[d], osem.at[d % 2])
            for d in range(N_DEV):
                if d >= 2:
                    out(d - 2).wait()
                blk[d % 2] = acc[W_IN_SHARD * d:W_IN_SHARD * (d + 1), :]
                out(d).start()
            out(N_DEV - 2).wait()
            out(N_DEV - 1).wait()

    def col(c):
        return pl.BlockSpec((tm, D_POOL), lambda j: (nstep - 1 - j, c))

    def halo(c):
        return pl.BlockSpec((HALO, D_POOL), lambda j: (jnp.maximum((nstep - 1 - j) * per - 1, 0), c))

    rev = lambda j: (nstep - 1 - j, 0)
    part = pl.BlockSpec((tm, D_POOL), rev)
    lanes = pl.BlockSpec((tm, 128), rev)
    full = pl.BlockSpec((tm, D_MODEL), rev)
    row = pl.BlockSpec((1, D_MODEL), lambda j: (0, 0))
    lrow = pl.BlockSpec((1, 128), lambda j: (0, 0))
    pw = pl.BlockSpec((4, POOL_GROUP, POOL_GROUP), lambda j: (0, 0, 0))
    psp = pl.BlockSpec((1, D_POOL), lambda j: (0, 0))
    cw = pl.BlockSpec((CONV_WIDTH, 3 * D_DN), lambda j: (0, 0))
    return pl.pallas_call(
        body, name="back", grid=(nstep,),
        out_shape=(jax.ShapeDtypeStruct((s, D_MODEL), F32),
                   jax.ShapeDtypeStruct((N_DEV, W_IN_SHARD, D_MODEL), F32), jax.ShapeDtypeStruct((1, D_MODEL), F32),
                   jax.ShapeDtypeStruct((4, POOL_GROUP, POOL_GROUP), F32), jax.ShapeDtypeStruct((1, D_POOL), F32),
                   jax.ShapeDtypeStruct((CONV_WIDTH, 3 * D_DN), F32),
                   jax.ShapeDtypeStruct((1, 128), F32), jax.ShapeDtypeStruct((1, 128), F32)),
        in_specs=[col(0), col(1), col(2), col(3), col(4), halo(0), halo(2), halo(3), halo(4), lanes,
                  part, part, part, part, lanes, part, full, full,
                  row, pw, psp, cw, lrow, lrow, pl.BlockSpec(memory_space=pl.ANY)],
        out_specs=(full, pl.BlockSpec(memory_space=pl.ANY), row, pw, psp, cw, lrow, lrow),
        scratch_shapes=[pltpu.VMEM((D_IN_PAD, D_MODEL), BF16), pltpu.VMEM((D_IN_PAD, D_MODEL), F32),
                        pltpu.VMEM((2, W_IN_SHARD, D_MODEL), F32),
                        pltpu.VMEM((HALO, 3 * D_DN), F32), pltpu.VMEM((HALO, D_POOL), F32),
                        pltpu.SemaphoreType.DMA, pltpu.SemaphoreType.DMA((2,))],
        compiler_params=_cp(("arbitrary",)),
    )(proj_main, proj_main, proj_main, proj_main, proj_main, proj_main, proj_main, proj_main, proj_main, proj_ba,
      dyp, dqn, dkn, dvv, dgb, ddz, x2, dh, norm_w, pool_w, pool_scale, conv_full, alog_lane, dtb_lane, wt_full)


def _in_proj_bwd(dwin, dzp, dcq, dck, dcv, ddz, dba, x2, dh, norm_w, wt_full, conv_full, tm):
    s = x2.shape[0]
    nstep = s // tm

    def body(dwin_ref, dzp_ref, dcq_ref, dck_ref, dcv_ref, ddz_ref, dba_ref, x_ref, dh_ref, nw_ref, cw_ref, wt_hbm,
             gx_ref, p_hbm, gnw_ref, wt_vmem, acc, blk, head_dc, head_dw, sem, osem):
        j = pl.program_id(0)
        i = nstep - 1 - j

        @pl.when(j == 0)
        def _():
            cp = pltpu.make_async_copy(wt_hbm, wt_vmem, sem)
            cp.start()
            acc[...] = jnp.zeros_like(acc)
            gnw_ref[...] = jnp.zeros_like(gnw_ref)
            head_dc[...] = jnp.zeros_like(head_dc)
            head_dw[...] = jnp.zeros_like(head_dw)
            cp.wait()
        dw = dwin_ref[...]
        ext = jnp.concatenate([dw, head_dw[...]], axis=0)
        m = ext.shape[0]
        a2 = ext + pltpu.roll(ext, m - 1, 0)
        a4 = a2 + pltpu.roll(a2, m - 2, 0)
        a8 = a4 + pltpu.roll(a4, m - 4, 0)
        a16 = a8 + pltpu.roll(a8, m - 8, 0)
        dup = jnp.concatenate(
            [acc_w[:tm, gi * POOL_GROUP:(gi + 1) * POOL_GROUP]
             - dw[:, gi * POOL_GROUP:(gi + 1) * POOL_GROUP] * _pool_counts(i, tm, w)
             for gi, (w, acc_w) in enumerate(zip(POOL_WINDOWS, (a2, a4, a8, a16)))], axis=1)
        head_dw[...] = dw[:HALO, :]
        dxs = []
        for c, ref in enumerate((dcq_ref, dck_ref, dcv_ref)):
            cols = slice(c * D_DN, (c + 1) * D_DN)
            w4 = cw_ref[:, cols]
            dy = ref[...]
            head = head_dc[:, cols]
            dx = dy * w4[CONV_WIDTH - 1:CONV_WIDTH, :]
            for sft in range(1, CONV_WIDTH):
                dx = dx + _shift_up(dy, head, sft) * w4[CONV_WIDTH - 1 - sft:CONV_WIDTH - sft, :]
            dxs.append(dx)
            head_dc[:, cols] = dy[:HALO, :]
        dbab = dba_ref[...].astype(BF16)
        xhat, r = _rms_hat(x_ref[...])
        nw = nw_ref[...]
        n = (xhat * nw).astype(BF16)
        acc[D_MAIN:, :] += _dot_tn_bf(dbab, n)
        dn = jnp.dot(dbab, wt_vmem[D_MAIN:, :], preferred_element_type=F32)
        for cb, d in enumerate((dup, dzp_ref[...], dxs[0], dxs[1], dxs[2], ddz_ref[...])):
            rows = slice(cb * D_POOL, (cb + 1) * D_POOL)
            dpart = d.astype(BF16)
            acc[rows, :] += _dot_tn_bf(dpart, n)
            dn = dn + jnp.dot(dpart, wt_vmem[rows, :], preferred_element_type=F32)
        gnw_ref[...] += jnp.sum(dn * xhat, axis=0, keepdims=True)
        dxh = dn * nw
        gx_ref[...] = dh_ref[...] + r * (dxh - xhat * jnp.mean(dxh * xhat, axis=-1, keepdims=True))

        @pl.when(j == nstep - 1)
        def _():
            def out(d):
                return pltpu.make_async_copy(blk.at[d % 2], p_hbm.at[d], osem.at[d % 2])
            for d in range(N_DEV):
                if d >= 2:
                    out(d - 2).wait()
                blk[d % 2] = acc[W_IN_SHARD * d:W_IN_SHARD * (d + 1), :]
                out(d).start()
            out(N_DEV - 2).wait()
            out(N_DEV - 1).wait()

    rev = lambda j: (nstep - 1 - j, 0)
    part = pl.BlockSpec((tm, D_POOL), rev)
    full = pl.BlockSpec((tm, D_MODEL), rev)
    row = pl.BlockSpec((1, D_MODEL), lambda j: (0, 0))
    return pl.pallas_call(
        body, name="in_proj_bwd", grid=(nstep,),
        out_shape=(jax.ShapeDtypeStruct((s, D_MODEL), F32),
                   jax.ShapeDtypeStruct((N_DEV, W_IN_SHARD, D_MODEL), F32), jax.ShapeDtypeStruct((1, D_MODEL), F32)),
        in_specs=[part] * 6 + [pl.BlockSpec((tm, 128), rev), full, full, row,
                               pl.BlockSpec((CONV_WIDTH, 3 * D_DN), lambda j: (0, 0)),
                               pl.BlockSpec(memory_space=pl.ANY)],
        out_specs=(full, pl.BlockSpec(memory_space=pl.ANY), row),
        scratch_shapes=[pltpu.VMEM((D_IN_PAD, D_MODEL), BF16), pltpu.VMEM((D_IN_PAD, D_MODEL), F32),
                        pltpu.VMEM((2, W_IN_SHARD, D_MODEL), F32),
                        pltpu.VMEM((HALO, 3 * D_DN), F32), pltpu.VMEM((HALO, D_POOL), F32),
                        pltpu.SemaphoreType.DMA, pltpu.SemaphoreType.DMA((2,))],
        compiler_params=_cp(("arbitrary",)),
    )(dwin, dzp, dcq, dck, dcv, ddz, dba, x2, dh, norm_w, conv_full, wt_full)


def _adamw_math(w, g, m, v):
    m = ADAM_B1 * m + (1.0 - ADAM_B1) * g
    v = ADAM_B2 * v + (1.0 - ADAM_B2) * (g * g)
    m_hat = m / (1.0 - ADAM_B1 ** ADAM_STEP)
    v_hat = v / (1.0 - ADAM_B2 ** ADAM_STEP)
    delta = -ADAM_LR * (m_hat / (jnp.sqrt(v_hat) + ADAM_EPS) + ADAM_WD * w)
    return delta, m, v


def _adamw_sharded(params):
    k = len(params)

    def body(*refs):
        ins, outs = refs[:4 * k], refs[4 * k:]
        for p in range(k):
            w_ref, g_ref, m_ref, v_ref = ins[4 * p:4 * p + 4]
            go_ref = outs[4 * p]
            if g_ref.shape == w_ref.shape:
                go_ref[...] = g_ref[...]
            else:
                for j in range(FLAT_ROWS):
                    go_ref[pl.ds(j, W_IN_SHARD, stride=FLAT_ROWS), :] = g_ref[:, 128 * j:128 * (j + 1)]
            d, nm, nv = _adamw_math(w_ref[...], go_ref[...], m_ref[...], v_ref[...])
            outs[4 * p + 1][...] = d
            outs[4 * p + 2][...] = nm
            outs[4 * p + 3][...] = nv

    flat = [a for p in params for a in p]
    out_shape = tuple(jax.ShapeDtypeStruct(p[0].shape, F32) for p in params for _ in range(4))
    res = pl.pallas_call(body, name="adamw_sharded", out_shape=out_shape, compiler_params=_cp())(*flat)
    return [tuple(res[4 * p:4 * p + 4]) for p in range(k)]


def _adamw_replicated(gath_a, gath_b, pool, rows):
    nrow = len(rows)

    def body(*refs):
        ga_ref, gb_ref = refs[:2]
        ins = refs[2:2 + 3 * (nrow + 1)]
        outs = refs[2 + 3 * (nrow + 1):]

        def total(ref):
            g = ref[0]
            for d in range(1, N_DEV):
                g = g + ref[d]
            return g

        def update(g, wmv, o):
            w, m, v = (r[...] for r in wmv)
            dl, nm, nv = _adamw_math(w, g, m, v)
            o[0][...] = g
            o[1][...] = dl
            o[2][...] = nm
            o[3][...] = nv

        update(total(ga_ref), ins[:3], outs[:4])
        gb = total(gb_ref)
        for r in range(nrow):
            n = ins[3 * (r + 1)].shape[1]
            update(gb[r:r + 1, :n], ins[3 * (r + 1):3 * (r + 2)], outs[4 * (r + 1):4 * (r + 2)])
        outs[4 * (nrow + 1)][...] = gb[nrow:nrow + 1, 0:1]

    flat = list(pool) + [a for wmv in rows for a in wmv]
    out_shape = ((jax.ShapeDtypeStruct(pool[0].shape, F32),) * 4
                 + tuple(jax.ShapeDtypeStruct(wmv[0].shape, F32) for wmv in rows for _ in range(4))
                 + (jax.ShapeDtypeStruct((1, 1), F32),))
    res = pl.pallas_call(body, name="adamw_replicated", out_shape=out_shape, compiler_params=_cp())(
        gath_a, gath_b, *flat)
    return [res[4 * k:4 * k + 4] for k in range(nrow + 1)], res[-1]


_ROW_ORDER = ("norm_w", "final_norm_w", "pool_scale", "dn_norm_w", "a_log", "dt_bias")


def _pack_rows(vectors):
    out = [jnp.pad(v.reshape(-1), (0, D_MODEL - v.size)) for v in vectors]
    out += [jnp.zeros((D_MODEL,), F32)] * (8 - len(out))
    return jnp.stack(out, axis=0)


def _lane_row(vec4, start):
    return jnp.pad(vec4.reshape(-1), (start, 128 - start - vec4.size)).reshape(1, 128)


def kernel(x, norm_w, w_in, pool_w, pool_scale, conv_w, a_log, dt_bias, dn_norm_w, w_out, final_norm_w, loss_target, m_norm_w, m_w_in, m_pool_w, m_pool_scale, m_conv_w, m_a_log, m_dt_bias, m_dn_norm_w, m_w_out, m_final_norm_w, v_norm_w, v_w_in, v_pool_w, v_pool_scale, v_conv_w, v_a_log, v_dt_bias, v_dn_norm_w, v_w_out, v_final_norm_w):
    s = x.shape[1]
    tm = min(512, s)
    tmb = min(256, s)
    tp = min(512, s)
    x2 = x[0]
    tgt = loss_target[0]
    def to_flat(a):
        return a[0].reshape(FLAT_ROWS, 128, W_IN_SHARD).transpose(2, 0, 1).reshape(W_IN_SHARD * FLAT_ROWS, 128)

    def from_flat(f):
        return f.reshape(W_IN_SHARD, FLAT_ROWS, 128).transpose(1, 2, 0).reshape(1, D_MODEL, W_IN_SHARD)

    wf, m_wf, v_wf = to_flat(w_in), to_flat(m_w_in), to_flat(v_w_in)

    g_in, g_conv = _gather_weights(wf, conv_w[0])
    conv_full = g_conv.transpose(1, 0, 2).reshape(CONV_WIDTH, 3 * D_DN)
    alog_lane = _lane_row(a_log, DN_HEADS)
    dtb_lane = _lane_row(dt_bias, DN_HEADS)
    fnw = final_norm_w.reshape(1, D_MODEL)

    proj_main, proj_ba, y_pool, qn, kn, vv, gb, g_out, wt_full = _front(
        x2, norm_w, g_in, pool_w[0], pool_scale, conv_full, alog_lane, dtb_lane, w_out[0], tm)
    w_out_full = g_out.reshape(D_MODEL, D_MODEL)
    y_dn, states, dn_pre, dn_uw = _dn_scan_fwd(qn, kn, vv, gb, proj_main, dn_norm_w, DN_CHUNKS_PER_STEP)

    dh, dyp, dyd, g_wout, g_fnw, loss_part = _out_proj_loss(y_pool, y_dn, x2, tgt, w_out_full, fnw, tm)
    p_out = g_wout.reshape(N_DEV, D_MODEL // N_DEV, D_MODEL)
    dqn, dkn, dvv, dgb, ddz, g_dnw, gr_out = _dn_scan_bwd(qn, kn, vv, gb, proj_main, dn_norm_w, states, dn_pre, dn_uw,
                                                          dyd, p_out, DN_CHUNKS_PER_STEP)
    grad_x2, p_in, g_nw, g_pw, g_ps, g_conv_full, g_al, g_db = _back(
        proj_main, proj_ba, dyp, dqn, dkn, dvv, dgb, ddz, x2, dh, norm_w, pool_w[0], pool_scale, conv_full,
        alog_lane, dtb_lane, wt_full, tmb)

    p_conv = g_conv_full.reshape(CONV_WIDTH, N_DEV, 3 * D_DN // N_DEV).transpose(1, 0, 2)
    pack_a = g_pw.reshape(4 * POOL_GROUP, POOL_GROUP)
    pack_b = _pack_rows([g_nw, g_fnw, g_ps, g_dnw, g_al[0, DN_HEADS:2 * DN_HEADS], g_db[0, DN_HEADS:2 * DN_HEADS],
                         loss_part[0, :1]])
    gr_in, gr_conv, gath_a, gath_b = _reduce_grads((p_in, p_conv), pack_a, pack_b)

    r_in, r_out, r_conv = _adamw_sharded([(wf, gr_in, m_wf, v_wf), (w_out[0], gr_out, m_w_out[0], v_w_out[0]),
                                          (conv_w[0], gr_conv, m_conv_w[0], v_conv_w[0])])
    flat = lambda a: a.reshape(4 * POOL_GROUP, POOL_GROUP)
    row = lambda a: a.reshape(1, -1)
    vecs = {"norm_w": (norm_w, m_norm_w, v_norm_w), "final_norm_w": (final_norm_w, m_final_norm_w, v_final_norm_w),
            "pool_scale": (pool_scale, m_pool_scale, v_pool_scale), "dn_norm_w": (dn_norm_w, m_dn_norm_w, v_dn_norm_w),
            "a_log": (a_log, m_a_log, v_a_log), "dt_bias": (dt_bias, m_dt_bias, v_dt_bias)}
    res, loss = _adamw_replicated(gath_a, gath_b, (flat(pool_w), flat(m_pool_w), flat(v_pool_w)),
                                  [tuple(row(a) for a in vecs[nm]) for nm in _ROW_ORDER])
    r_pool = res[0]
    r_vec = dict(zip(_ROW_ORDER, res[1:]))

    def group(k):
        vec = lambda nm: r_vec[nm][k].reshape(vecs[nm][0].shape)
        return (vec("norm_w"), from_flat(r_in[k]), r_pool[k].reshape(pool_w.shape), vec("pool_scale"), r_conv[k][None],
                vec("a_log"), vec("dt_bias"), vec("dn_norm_w"), r_out[k][None], vec("final_norm_w"))

    return (loss[0, 0], grad_x2[None], *group(0), *group(1), *group(2), *group(3))
```

```python
import functools

import jax
import jax.numpy as jnp
from jax import lax
from jax.experimental import pallas as pl
from jax.experimental.pallas import tpu as pltpu

F32 = jnp.float32
BF16 = jnp.bfloat16
HI = lax.Precision.HIGHEST
MESH = pl.DeviceIdType.MESH

D_MODEL = 1024
D_POOL = 512
D_DN = 512
POOL_WINDOWS = (2, 4, 8, 16)
POOL_GROUP = 128
DN_HEADS = 4
DN_HEAD_DIM = 128
CONV_WIDTH = 4
CHUNK = 64
NORM_EPS = 1e-6
D_IN = 3080
D_MAIN = 3072
FLAT_ROWS = D_MODEL // 128
D_IN_PAD = D_MAIN + 128
N_DEV = 8
W_IN_SHARD = D_IN // N_DEV
HALO = 16
DN_CHUNKS_PER_STEP = 8

ADAM_LR = 0.001
ADAM_B1 = 0.9
ADAM_B2 = 0.999
ADAM_EPS = 1e-08
ADAM_WD = 0.01
ADAM_STEP = 10

VMEM_LIMIT = 56 * 1024 * 1024

def _cp(sem=None, vmem=VMEM_LIMIT):
    kw = {"vmem_limit_bytes": vmem}
    if sem is not None:
        kw["dimension_semantics"] = sem
    return pltpu.CompilerParams(**kw)


def _dot_bf(a, b):
    return jnp.dot(a.astype(BF16), b.astype(BF16), preferred_element_type=F32)


def _dot_nt_bf(a, b):
    return lax.dot_general(a.astype(BF16), b.astype(BF16), (((1,), (1,)), ((), ())), preferred_element_type=F32)


def _dot_tn_bf(a, b):
    return lax.dot_general(a.astype(BF16), b.astype(BF16), (((0,), (0,)), ((), ())), preferred_element_type=F32)


def _mm_raw(a, b, ca, cb, prec):
    off = a.ndim - 2
    dn = (((ca + off,), (cb + off,)), ((0,), (0,)) if off else ((), ()))
    if prec == "hi":
        return lax.dot_general(a, b, dn, precision=HI, preferred_element_type=F32)
    ah, bh = a.astype(BF16), b.astype(BF16)
    out = lax.dot_general(ah, bh, dn, preferred_element_type=F32)
    if prec == "x3":
        al = (a - ah.astype(F32)).astype(BF16)
        bl = (b - bh.astype(F32)).astype(BF16)
        out = out + lax.dot_general(ah, bl, dn, preferred_element_type=F32)
        out = out + lax.dot_general(al, bh, dn, preferred_element_type=F32)
    return out


@functools.partial(jax.custom_vjp, nondiff_argnums=(2, 3, 4, 5))
def _mm(a, b, ca, cb, prec, bprec):
    return _mm_raw(a, b, ca, cb, prec)


def _mm_fwd(a, b, ca, cb, prec, bprec):
    return _mm_raw(a, b, ca, cb, prec), (a, b)


def _mm_bwd(ca, cb, prec, bprec, res, dc):
    a, b = res
    da = _mm_raw(dc, b, 1, 1 - cb, bprec) if ca == 1 else _mm_raw(b, dc, 1 - cb, 1, bprec)
    db = _mm_raw(a, dc, 1 - ca, 0, bprec) if cb == 0 else _mm_raw(dc, a, 0, 1 - ca, bprec)
    return da, db


_mm.defvjp(_mm_fwd, _mm_bwd)


@functools.partial(jax.custom_vjp, nondiff_argnums=(1, 2))
def _tri_inv(a, prec, bprec):
    n = a.shape[-1]
    ii = lax.broadcasted_iota(jnp.int32, (n, n), 0)
    jj = lax.broadcasted_iota(jnp.int32, (n, n), 1)
    p = (ii == jj).astype(F32) - a
    b = _mm_raw(a, a, 1, 0, prec)
    for _ in range(4):
        pb = _mm_raw(jnp.concatenate([p, b], axis=-2), b, 1, 0, prec)
        p = p + pb[..., :n, :]
        b = pb[..., n:, :]
    return p + _mm_raw(p, b, 1, 0, prec)


def _tri_inv_fwd(a, prec, bprec):
    t = _tri_inv(a, prec, bprec)
    return t, t


def _tri_inv_bwd(prec, bprec, t, dt):
    return (-_mm_raw(_mm_raw(t, dt, 0, 0, bprec), t, 1, 1, bprec),)


_tri_inv.defvjp(_tri_inv_fwd, _tri_inv_bwd)

@functools.partial(jax.custom_vjp, nondiff_argnums=(3, 4, 5))
def _mm_known(a, b, out, ca, cb, bprec):
    return out


def _mm_known_fwd(a, b, out, ca, cb, bprec):
    return out, (a, b)


def _mm_known_bwd(ca, cb, bprec, res, dc):
    return _mm_bwd(ca, cb, None, bprec, res, dc) + (jnp.zeros_like(dc),)


_mm_known.defvjp(_mm_known_fwd, _mm_known_bwd)


@jax.custom_vjp
def _use_known(x, known):
    return known


_use_known.defvjp(lambda x, known: (known, None), lambda _, g: (g, jnp.zeros_like(g)))


@functools.partial(jax.custom_vjp, nondiff_argnums=(2,))
def _tri_inv_known(a, t, bprec):
    return t


def _tri_inv_known_fwd(a, t, bprec):
    return t, t


def _tri_inv_known_bwd(bprec, t, dt):
    return _tri_inv_bwd(None, bprec, t, dt) + (jnp.zeros_like(dt),)


_tri_inv_known.defvjp(_tri_inv_known_fwd, _tri_inv_known_bwd)

_DN_PREC = {"akq": ("bf16", "bf16"), "inv": ("bf16", "bf16"), "uw": ("bf16", "bf16"), "ws": ("bf16", "bf16"),
            "ov": ("bf16", "bf16"), "st": ("bf16", "bf16")}


def _silu(x):
    return x * jax.nn.sigmoid(x)


def _softplus(x):
    pos = x > 0.0
    return jnp.where(pos, x, 0.0) + jnp.log1p(jnp.exp(jnp.where(pos, -x, x)))


def _mesh_pos():
    return lax.axis_index("x"), lax.axis_index("y"), lax.axis_index("c")


def _dev_index(x, y, c):
    return 4 * x + 2 * y + c


def _relay_order():
    x, y, c = _mesh_pos()
    n1 = (x + (1 - c) * (1 - 2 * x), y + c * (1 - 2 * y))
    n2 = (x + c * (1 - 2 * x), y + (1 - c) * (1 - 2 * y))
    return (x, y, c), (x, y, 1 - c), n1, n2, (1 - x, 1 - y)


def _all_gather_blocks(outs, send_sems, recv_sems):
    me, sibling, n1, n2, diag = _relay_order()
    c = me[2]

    def copy(a, k, block, to):
        rows = outs[a].at[_dev_index(*block)]
        return pltpu.make_async_remote_copy(src_ref=rows, dst_ref=rows, send_sem=send_sems.at[a, k],
                                            recv_sem=recv_sems.at[a, k], device_id=to, device_id_type=MESH)

    n = len(outs)
    started = []

    def start(cp):
        cp.start()
        started.append(cp)

    for a in range(n):
        start(copy(a, 1, me, (*n1, c)))
        start(copy(a, 2, me, (*n2, c)))
        start(copy(a, 0, me, sibling))
    for a in range(n):
        copy(a, 1, (*n1, c), me).wait_recv()
        start(copy(a, 3, (*n1, c), (*n2, c)))
        start(copy(a, 4, (*n1, c), sibling))
    for a in range(n):
        copy(a, 2, (*n2, c), me).wait_recv()
        start(copy(a, 5, (*n2, c), sibling))
        copy(a, 3, (*diag, c), me).wait_recv()
        start(copy(a, 6, (*diag, c), sibling))
    for a in range(n):
        copy(a, 0, sibling, me).wait_recv()
        copy(a, 4, (*n2, 1 - c), me).wait_recv()
        copy(a, 5, (*n1, 1 - c), me).wait_recv()
        copy(a, 6, (*diag, 1 - c), me).wait_recv()
    for cp in started:
        cp.wait_send()


def _peer_relations():
    x, y, c = _mesh_pos()
    flips = [(fx, fy, fc) for fx in (0, 1) for fy in (0, 1) for fc in (0, 1)][1:]
    peers = [(1 - x if fx else x, 1 - y if fy else y, 1 - c if fc else c) for fx, fy, fc in flips]
    return (x, y, c), peers


def _direct_gather_start(out_ref, send_sems, recv_sems):
    me, peers = _peer_relations()
    rows = out_ref.at[_dev_index(*me)]
    for k, peer in enumerate(peers):
        pltpu.make_async_remote_copy(src_ref=rows, dst_ref=rows, send_sem=send_sems.at[k], recv_sem=recv_sems.at[k],
                                     device_id=peer, device_id_type=MESH).start()


def _direct_gather_wait(out_ref, send_sems, recv_sems):
    me, peers = _peer_relations()
    for k, peer in enumerate(peers):
        rows = out_ref.at[_dev_index(*peer)]
        cp = pltpu.make_async_remote_copy(src_ref=rows, dst_ref=rows, send_sem=send_sems.at[k],
                                          recv_sem=recv_sems.at[k], device_id=peer, device_id_type=MESH)
        cp.wait_recv()
        cp.wait_send()


def _direct_scatter_start(send_ref, recv_ref, send_sems, recv_sems):
    me, peers = _peer_relations()
    for k, peer in enumerate(peers):
        pltpu.make_async_remote_copy(src_ref=send_ref.at[_dev_index(*peer)], dst_ref=recv_ref.at[k],
                                     send_sem=send_sems.at[k], recv_sem=recv_sems.at[k],
                                     device_id=peer, device_id_type=MESH).start()


def _direct_scatter_wait(send_ref, recv_ref, send_sems, recv_sems):
    me, peers = _peer_relations()
    for k, peer in enumerate(peers):
        cp = pltpu.make_async_remote_copy(src_ref=send_ref.at[_dev_index(*peer)], dst_ref=recv_ref.at[k],
                                          send_sem=send_sems.at[k], recv_sem=recv_sems.at[k],
                                          device_id=peer, device_id_type=MESH)
        cp.wait_recv()
        cp.wait_send()


def _gather_weights(w_in_flat, conv_blk):
    def body(win_ref, conv_ref, gin_ref, gconv_ref, send_sems, recv_sems):
        x, y, c = _mesh_pos()
        me = _dev_index(x, y, c)
        for j in range(FLAT_ROWS):
            gin_ref[me, :, 128 * j:128 * (j + 1)] = win_ref[pl.ds(j, W_IN_SHARD, stride=FLAT_ROWS), :].astype(BF16)
        gconv_ref[me] = conv_ref[...]
        _all_gather_blocks((gin_ref, gconv_ref), send_sems, recv_sems)

    vm = pl.BlockSpec(memory_space=pltpu.VMEM)
    return pl.pallas_call(
        body, name="gather_weights",
        out_shape=(jax.ShapeDtypeStruct((N_DEV, W_IN_SHARD, D_MODEL), BF16),
                   jax.ShapeDtypeStruct((N_DEV,) + conv_blk.shape, F32)),
        in_specs=[vm, vm], out_specs=(vm, vm),
        scratch_shapes=[pltpu.SemaphoreType.DMA((2, 7)), pltpu.SemaphoreType.DMA((2, 7))],
        compiler_params=_cp(),
    )(w_in_flat, conv_blk)


def _reduce_grads(big, pack_a, pack_b):
    nb = len(big)

    def body(*refs):
        srcs, (pa_ref, pb_ref), outs, (ga_ref, gb_ref) = (
            refs[:nb], refs[nb:nb + 2], refs[nb + 2:2 * nb + 2], refs[2 * nb + 2:2 * nb + 4])
        scr = refs[2 * nb + 4:]
        r1s, r2s, sbs, sts = (scr[k * nb:(k + 1) * nb] for k in range(4))
        s1_send, s1_recv, s2_send, s2_recv, ag_send, ag_recv, st_sem = scr[4 * nb:]
        x, y, c = _mesh_pos()
        me = (x, y, c)
        sibling = (x, y, 1 - c)
        rel = [(x, y), (1 - x, y), (x, 1 - y), (1 - x, 1 - y)]

        ga_ref[_dev_index(*me)] = pa_ref[...]
        gb_ref[_dev_index(*me)] = pb_ref[...]

        def p1(a, r, to):
            return pltpu.make_async_remote_copy(
                src_ref=srcs[a].at[_dev_index(*rel[r], 1 - c)], dst_ref=r1s[a].at[r],
                send_sem=s1_send.at[a, r], recv_sem=s1_recv.at[a, r], device_id=to, device_id_type=MESH)

        _, _, n1, n2, diag = _relay_order()
        order = ((n1, 1 + c), (diag, 3), (n2, 2 - c), ((x, y), 0))

        def p1_landed(a, slot):
            return pltpu.make_async_remote_copy(
                src_ref=r1s[a].at[slot], dst_ref=r1s[a].at[slot], send_sem=s1_send.at[a, slot],
                recv_sem=s1_recv.at[a, slot], device_id=me, device_id_type=MESH)

        def p2(a, k, to):
            return pltpu.make_async_remote_copy(
                src_ref=sbs[a].at[k], dst_ref=r2s[a].at[k],
                send_sem=s2_send.at[a, k], recv_sem=s2_recv.at[a, k], device_id=to, device_id_type=MESH)

        def stage(a, i):
            return pltpu.make_async_copy(srcs[a].at[_dev_index(*order[i][0], c)], sts[a].at[i % 2], st_sem.at[a, i % 2])

        sends = [p1(a, r, sibling) for a in range(nb) for r in range(4)]
        for cp in sends:
            cp.start()
        for a in range(nb):
            stage(a, 0).start()
            for i, (_, slot) in enumerate(order):
                if i + 1 < len(order):
                    stage(a, i + 1).start()
                stage(a, i).wait()
                p1_landed(a, slot).wait_recv()
                chip_sum = r1s[a][slot] + sts[a][i % 2]
                if i < 2:
                    sbs[a][i] = chip_sum.astype(BF16)
                    sends.append(p2(a, i, (*n1, c)))
                    sends[-1].start()
                else:
                    r1s[a][slot] = chip_sum
        for a in range(nb):
            p2(a, 0, me).wait_recv()
            p2(a, 1, me).wait_recv()
            sbs[a][2] = (r1s[a][2 - c] + r2s[a][1].astype(F32)).astype(BF16)
            sends.append(p2(a, 2, (*n2, c)))
            sends[-1].start()
        _all_gather_blocks((ga_ref, gb_ref), ag_send, ag_recv)
        for a in range(nb):
            p2(a, 2, me).wait_recv()
            outs[a][...] = (r1s[a][0] + r2s[a][0].astype(F32)) + r2s[a][2].astype(F32)
        for cp in sends:
            cp.wait_send()

    vm = pl.BlockSpec(memory_space=pltpu.VMEM)
    hbm = pl.BlockSpec(memory_space=pl.ANY)
    blk = [p.shape[1:] for p in big]
    scratch = ([pltpu.VMEM((4,) + b, F32) for b in blk] + [pltpu.VMEM((3,) + b, BF16) for b in blk]
               + [pltpu.VMEM((3,) + b, BF16) for b in blk] + [pltpu.VMEM((2,) + b, F32) for b in blk]
               + [pltpu.SemaphoreType.DMA((nb, 4)), pltpu.SemaphoreType.DMA((nb, 4)),
                  pltpu.SemaphoreType.DMA((nb, 3)), pltpu.SemaphoreType.DMA((nb, 3)),
                  pltpu.SemaphoreType.DMA((2, 7)), pltpu.SemaphoreType.DMA((2, 7)),
                  pltpu.SemaphoreType.DMA((nb, 2))])
    return pl.pallas_call(
        body, name="reduce_grads",
        out_shape=tuple(jax.ShapeDtypeStruct(b, F32) for b in blk)
        + (jax.ShapeDtypeStruct((N_DEV,) + pack_a.shape, F32), jax.ShapeDtypeStruct((N_DEV,) + pack_b.shape, F32)),
        in_specs=[hbm] * nb + [vm, vm], out_specs=(vm,) * (nb + 2),
        scratch_shapes=scratch,
        compiler_params=_cp(),
    )(*big, pack_a, pack_b)


def _rms_hat(xf):
    r = lax.rsqrt(jnp.mean(xf * xf, axis=-1, keepdims=True) + NORM_EPS)
    return xf * r, r


def _in_proj(x2, norm_w, g_in, tm):
    s = x2.shape[0]

    def body(x_ref, nw_ref, g_hbm, pm_ref, pb_ref, wt_hbm, g_vmem, wt_vmem, sem):
        @pl.when(pl.program_id(0) == 0)
        def _():
            cp = pltpu.make_async_copy(g_hbm, g_vmem, sem)
            cp.start()
            wt_vmem[D_MAIN:, :] = jnp.zeros((D_IN_PAD - D_MAIN, D_MODEL), BF16)
            cp.wait()
            for d in range(N_DEV):
                wt_vmem[W_IN_SHARD * d:W_IN_SHARD * (d + 1), :] = g_vmem[d]
            out = pltpu.make_async_copy(wt_vmem, wt_hbm, sem)
            out.start()
            out.wait()
        xhat, _ = _rms_hat(x_ref[...])
        n = (xhat * nw_ref[...]).astype(BF16)
        pm_ref[...] = _dot_nt_bf(n, wt_vmem[:D_MAIN, :])
        pb_ref[...] = _dot_nt_bf(n, wt_vmem[D_MAIN:, :])

    return pl.pallas_call(
        body, name="in_proj", grid=(s // tm,),
        out_shape=(jax.ShapeDtypeStruct((s, D_MAIN), F32), jax.ShapeDtypeStruct((s, 128), F32),
                   jax.ShapeDtypeStruct((D_IN_PAD, D_MODEL), BF16)),
        in_specs=[pl.BlockSpec((tm, D_MODEL), lambda i: (i, 0)),
                  pl.BlockSpec((1, D_MODEL), lambda i: (0, 0)),
                  pl.BlockSpec(memory_space=pl.ANY)],
        out_specs=(pl.BlockSpec((tm, D_MAIN), lambda i: (i, 0)), pl.BlockSpec((tm, 128), lambda i: (i, 0)),
                   pl.BlockSpec(memory_space=pl.ANY)),
        scratch_shapes=[pltpu.VMEM((N_DEV, W_IN_SHARD, D_MODEL), BF16), pltpu.VMEM((D_IN_PAD, D_MODEL), BF16),
                        pltpu.SemaphoreType.DMA],
        compiler_params=_cp(("arbitrary",)),
    )(x2, norm_w, g_in)


def _shift_down(cur, prev_tail, s):
    ext = jnp.concatenate([prev_tail, cur], axis=0)
    return pltpu.roll(ext, s, 0)[HALO:, :]


def _shift_up(cur, next_head, s):
    ext = jnp.concatenate([cur, next_head], axis=0)
    n = ext.shape[0]
    return pltpu.roll(ext, n - s, 0)[:cur.shape[0], :]


def _pool_counts(i, tp, w):
    t = i * tp + lax.broadcasted_iota(jnp.int32, (tp, 1), 0)
    return jnp.minimum(t + 1, w).astype(F32)


def _pool_mix(u, u_prev_tail, i, tp):
    ext = jnp.concatenate([u_prev_tail, u], axis=0)
    w2 = ext + pltpu.roll(ext, 1, 0)
    w4 = w2 + pltpu.roll(w2, 2, 0)
    w8 = w4 + pltpu.roll(w4, 4, 0)
    w16 = w8 + pltpu.roll(w8, 8, 0)
    mixes = []
    for gi, (w, win) in enumerate(zip(POOL_WINDOWS, (w2, w4, w8, w16))):
        cols = slice(gi * POOL_GROUP, (gi + 1) * POOL_GROUP)
        mixes.append(win[HALO:, cols] / _pool_counts(i, tp, w) - u[:, cols])
    return mixes


def _prev_halo_spec(tp, width, col):
    per = tp // HALO
    return pl.BlockSpec((HALO, width), lambda i: (jnp.maximum(i * per - 1, 0), col))


def _next_halo_spec(tp, width, col, n):
    per = tp // HALO
    return pl.BlockSpec((HALO, width), lambda i: (jnp.minimum((i + 1) * per, n * per - 1), col))


def _prev_tail(ref, i):
    return jnp.where(i > 0, ref[...], 0.0)


def _next_head(ref, i, n):
    return jnp.where(i < n - 1, ref[...], 0.0)


def _pool_fwd(proj_main, pool_w, pool_scale, tp):
    s = proj_main.shape[0]

    def body(u_ref, up_ref, z_ref, pw_ref, ps_ref, y_ref):
        i = pl.program_id(0)
        u = u_ref[...]
        mixes = _pool_mix(u, _prev_tail(up_ref, i), i, tp)
        gate = ps_ref[...] * _silu(z_ref[...])
        for gi in range(4):
            cols = slice(gi * POOL_GROUP, (gi + 1) * POOL_GROUP)
            y_ref[:, cols] = _dot_bf(mixes[gi], pw_ref[gi]) * gate[:, cols]

    return pl.pallas_call(
        body, name="pool_fwd", grid=(s // tp,),
        out_shape=jax.ShapeDtypeStruct((s, D_POOL), F32),
        in_specs=[pl.BlockSpec((tp, D_POOL), lambda i: (i, 0)),
                  _prev_halo_spec(tp, D_POOL, 0),
                  pl.BlockSpec((tp, D_POOL), lambda i: (i, 1)),
                  pl.BlockSpec((4, POOL_GROUP, POOL_GROUP), lambda i: (0, 0, 0)),
                  pl.BlockSpec((1, D_POOL), lambda i: (0, 0))],
        out_specs=pl.BlockSpec((tp, D_POOL), lambda i: (i, 0)),
        compiler_params=_cp(("parallel",)),
    )(proj_main, proj_main, proj_main, pool_w, pool_scale)


def _conv_taps(cur, prev_tail):
    ext = jnp.concatenate([prev_tail, cur], axis=0)
    return [cur] + [pltpu.roll(ext, sft, 0)[HALO:, :] for sft in range(1, CONV_WIDTH)]


def _conv_of_taps(taps, w4):
    y = taps[0] * w4[CONV_WIDTH - 1:CONV_WIDTH, :]
    for sft in range(1, CONV_WIDTH):
        y = y + taps[sft] * w4[CONV_WIDTH - 1 - sft:CONV_WIDTH - sft, :]
    return y


def _conv_fwd(cur, prev_tail, w4):
    ext = jnp.concatenate([prev_tail, cur], axis=0)
    y = ext * w4[CONV_WIDTH - 1:CONV_WIDTH, :]
    for sft in range(1, CONV_WIDTH):
        y = y + pltpu.roll(ext, sft, 0) * w4[CONV_WIDTH - 1 - sft:CONV_WIDTH - sft, :]
    return y[HALO:, :]


def _l2n_heads(t):
    parts = []
    for h in range(DN_HEADS):
        th = t[:, h * DN_HEAD_DIM:(h + 1) * DN_HEAD_DIM]
        parts.append(th * lax.rsqrt(jnp.sum(th * th, axis=-1, keepdims=True) + NORM_EPS))
    return jnp.concatenate(parts, axis=1)


def _post_conv(yq, yk, yv):
    return _l2n_heads(_silu(yq)), _l2n_heads(_silu(yk)), _silu(yv)


def _gates(ba, alog_lane, dtb_lane):
    lane = lax.broadcasted_iota(jnp.int32, ba.shape, 1)
    beta = jax.nn.sigmoid(ba)
    g = -jnp.exp(alog_lane) * _softplus(ba + dtb_lane)
    return jnp.where(lane < DN_HEADS, beta, jnp.where(lane < 2 * DN_HEADS, g, 0.0))


def _front(x2, norm_w, g_in, pool_w, pool_scale, conv_full, alog_lane, dtb_lane, w_out_blk, tm):
    s = x2.shape[0]

    def body(x_ref, nw_ref, pw_ref, ps_ref, cw_ref, al_ref, db_ref, wo_ref, g_hbm,
             pm_ref, pb_ref, yp_ref, qn_ref, kn_ref, vv_ref, gb_ref, gwo_hbm, wt_hbm,
             g_vmem, wt_vmem, tail_u, tail_qkv, gwo_ref, sem, wo_send, wo_recv):
        i = pl.program_id(0)

        @pl.when(i == 0)
        def _():
            gwo_ref[_dev_index(*_mesh_pos())] = wo_ref[...].astype(BF16)
            _direct_gather_start(gwo_ref, wo_send, wo_recv)
            cp = pltpu.make_async_copy(g_hbm, g_vmem, sem)
            cp.start()
            wt_vmem[D_MAIN:, :] = jnp.zeros((D_IN_PAD - D_MAIN, D_MODEL), BF16)
            tail_u[...] = jnp.zeros_like(tail_u)
            tail_qkv[...] = jnp.zeros_like(tail_qkv)
            cp.wait()
            for d in range(N_DEV):
                wt_vmem[W_IN_SHARD * d:W_IN_SHARD * (d + 1), :] = g_vmem[d]
            out = pltpu.make_async_copy(wt_vmem, wt_hbm, sem)
            out.start()
            out.wait()
        xhat, _ = _rms_hat(x_ref[...])
        n = (xhat * nw_ref[...]).astype(BF16)
        pm_ref[...] = _dot_nt_bf(n, wt_vmem[:D_MAIN, :])
        pb = _dot_nt_bf(n, wt_vmem[D_MAIN:, :])
        pb_ref[...] = pb
        u = pm_ref[:, :D_POOL]
        mixes = _pool_mix(u, tail_u[...], i, tm)
        tail_u[...] = u[tm - HALO:, :]
        gate = ps_ref[...] * _silu(pm_ref[:, D_POOL:2 * D_POOL])
        for gi in range(4):
            cols = slice(gi * POOL_GROUP, (gi + 1) * POOL_GROUP)
            yp_ref[:, cols] = _dot_bf(mixes[gi], pw_ref[gi]) * gate[:, cols]
        ys = []
        for c in range(3):
            cols = slice(c * D_DN, (c + 1) * D_DN)
            cur = pm_ref[:, 2 * D_POOL + c * D_DN:2 * D_POOL + (c + 1) * D_DN]
            ys.append(_conv_fwd(cur, tail_qkv[:, cols], cw_ref[:, cols]))
            tail_qkv[:, cols] = cur[tm - HALO:, :]
        qn, kn, vv = _post_conv(*ys)
        qn_ref[...] = qn
        kn_ref[...] = kn
        vv_ref[...] = vv
        gb_ref[...] = _gates(pb, al_ref[...], db_ref[...])

        @pl.when(i == s // tm - 1)
        def _():
            _direct_gather_wait(gwo_ref, wo_send, wo_recv)
            out = pltpu.make_async_copy(gwo_ref, gwo_hbm, sem)
            out.start()
            out.wait()

    tile = pl.BlockSpec((tm, D_DN), lambda i: (i, 0))
    lanes = pl.BlockSpec((tm, 128), lambda i: (i, 0))
    row = pl.BlockSpec((1, 128), lambda i: (0, 0))
    return pl.pallas_call(
        body, name="front", grid=(s // tm,),
        out_shape=(jax.ShapeDtypeStruct((s, D_MAIN), F32), jax.ShapeDtypeStruct((s, 128), F32),
                   jax.ShapeDtypeStruct((s, D_POOL), F32), jax.ShapeDtypeStruct((s, D_DN), F32),
                   jax.ShapeDtypeStruct((s, D_DN), F32), jax.ShapeDtypeStruct((s, D_DN), F32),
                   jax.ShapeDtypeStruct((s, 128), F32), jax.ShapeDtypeStruct((N_DEV,) + w_out_blk.shape, BF16),
                   jax.ShapeDtypeStruct((D_IN_PAD, D_MODEL), BF16)),
        in_specs=[pl.BlockSpec((tm, D_MODEL), lambda i: (i, 0)),
                  pl.BlockSpec((1, D_MODEL), lambda i: (0, 0)),
                  pl.BlockSpec((4, POOL_GROUP, POOL_GROUP), lambda i: (0, 0, 0)),
                  pl.BlockSpec((1, D_POOL), lambda i: (0, 0)),
                  pl.BlockSpec((CONV_WIDTH, 3 * D_DN), lambda i: (0, 0)), row, row,
                  pl.BlockSpec(memory_space=pltpu.VMEM), pl.BlockSpec(memory_space=pl.ANY)],
        out_specs=(pl.BlockSpec((tm, D_MAIN), lambda i: (i, 0)), lanes, tile, tile, tile, tile, lanes,
                   pl.BlockSpec(memory_space=pl.ANY), pl.BlockSpec(memory_space=pl.ANY)),
        scratch_shapes=[pltpu.VMEM((N_DEV, W_IN_SHARD, D_MODEL), BF16), pltpu.VMEM((D_IN_PAD, D_MODEL), BF16),
                        pltpu.VMEM((HALO, D_POOL), F32), pltpu.VMEM((HALO, 3 * D_DN), F32),
                        pltpu.VMEM((N_DEV,) + w_out_blk.shape, BF16),
                        pltpu.SemaphoreType.DMA, pltpu.SemaphoreType.DMA((7,)), pltpu.SemaphoreType.DMA((7,))],
        compiler_params=_cp(("arbitrary",)),
    )(x2, norm_w, pool_w, pool_scale, conv_full, alog_lane, dtb_lane, w_out_blk, g_in)


def _qkv_specs(tp, which):
    def spec(col, n=None):
        if which == 0:
            return pl.BlockSpec((tp, D_DN), lambda i: (i, col))
        if which < 0:
            return _prev_halo_spec(tp, D_DN, col)
        return _next_halo_spec(tp, D_DN, col, n)
    return spec


def _dn_pre(proj_main, proj_ba, conv_full, alog_lane, dtb_lane, tp):
    s = proj_main.shape[0]

    def body(q_ref, k_ref, v_ref, qp_ref, kp_ref, vp_ref, cw_ref, ba_ref, al_ref, db_ref,
             qn_ref, kn_ref, vv_ref, gb_ref):
        i = pl.program_id(0)
        ys = []
        for j, (cur, prev) in enumerate(((q_ref, qp_ref), (k_ref, kp_ref), (v_ref, vp_ref))):
            ys.append(_conv_fwd(cur[...], _prev_tail(prev, i), cw_ref[:, j * D_DN:(j + 1) * D_DN]))
        qn, kn, vv = _post_conv(*ys)
        qn_ref[...] = qn
        kn_ref[...] = kn
        vv_ref[...] = vv
        gb_ref[...] = _gates(ba_ref[...], al_ref[...], db_ref[...])

    cur, prev = _qkv_specs(tp, 0), _qkv_specs(tp, -1)
    row = pl.BlockSpec((1, 128), lambda i: (0, 0))
    tile = pl.BlockSpec((tp, D_DN), lambda i: (i, 0))
    return pl.pallas_call(
        body, name="dn_pre", grid=(s // tp,),
        out_shape=(jax.ShapeDtypeStruct((s, D_DN), F32),) * 3 + (jax.ShapeDtypeStruct((s, 128), F32),),
        in_specs=[cur(2), cur(3), cur(4), prev(2), prev(3), prev(4),
                  pl.BlockSpec((CONV_WIDTH, 3 * D_DN), lambda i: (0, 0)),
                  pl.BlockSpec((tp, 128), lambda i: (i, 0)), row, row],
        out_specs=(tile, tile, tile, pl.BlockSpec((tp, 128), lambda i: (i, 0))),
        compiler_params=_cp(("parallel",)),
    )(proj_main, proj_main, proj_main, proj_main, proj_main, proj_main, conv_full, proj_ba, alog_lane, dtb_lane)


def _dn_block(q, k, v, gcol, bcol, state, dz, nw, known=None):
    nb, n, d = q.shape
    ii = lax.broadcasted_iota(jnp.int32, (n, n), 0)
    jj = lax.broadcasted_iota(jnp.int32, (n, n), 1)
    lower = ii >= jj
    eye = (ii == jj).astype(F32)
    g_row = jnp.sum(eye * gcol, axis=1, keepdims=True)
    gc_col = jnp.sum(jnp.where(lower, g_row, 0.0), axis=2, keepdims=True)
    gc_row = jnp.sum(eye * gc_col, axis=1, keepdims=True)
    decay = jnp.where(lower, jnp.exp(jnp.where(lower, gc_col - gc_row, 0.0)), 0.0)
    kb = k * bcol
    vb = v * bcol
    qs = q * (DN_HEAD_DIM ** -0.5)
    egc = jnp.exp(gc_col)
    kq = jnp.concatenate([kb, qs], axis=1)
    vk = jnp.concatenate([vb, kb * egc], axis=2)
    if known is None:
        akq = _mm(kq, k, 1, 1, *_DN_PREC["akq"])
    else:
        akq = _mm_known(kq, k, known[0][:, :, :n].astype(F32), 1, 1, _DN_PREC["akq"][1])
    a = jnp.where(ii > jj, akq[:, :n] * decay, 0.0)
    qk = akq[:, n:] * decay
    if known is None:
        t = _tri_inv(a, *_DN_PREC["inv"])
        uw = _mm(t, vk, 1, 0, *_DN_PREC["uw"])
    else:
        t = _tri_inv_known(a, known[0][:, :n, n:].astype(F32), _DN_PREC["inv"][1])
        uw = _mm_known(t, vk, known[1], 1, 0, _DN_PREC["uw"][1])
    pre = jnp.concatenate([akq, jnp.concatenate([t, jnp.zeros_like(t)], axis=1)], axis=2)
    wq = jnp.concatenate([uw[:, :, d:], qs * egc], axis=1)
    g_last = gc_col[:, n - 1:n, :]
    k_dec = k * jnp.exp(g_last - gc_col)
    e_last = jnp.exp(g_last)
    os_, starts = [], []
    for c in range(nb // DN_HEADS):
        sl = slice(c * DN_HEADS, (c + 1) * DN_HEADS)
        if known is not None and c > 0:
            state = _use_known(state, known[2][sl])
        starts.append(state)
        ws = _mm(wq[sl], state, 1, 0, *_DN_PREC["ws"])
        v_new = uw[sl, :, :d] - ws[:, :n]
        os_.append(ws[:, n:] + _mm(qk[sl], v_new, 1, 0, *_DN_PREC["ov"]))
        state = state * e_last[sl] + _mm(k_dec[sl], v_new, 0, 0, *_DN_PREC["st"])
    o = jnp.concatenate(os_, axis=0)
    y = o * lax.rsqrt(jnp.mean(o * o, axis=-1, keepdims=True) + NORM_EPS) * nw * _silu(dz)
    return y, state, (pre, uw, jnp.concatenate(starts, axis=0))


def _dn_block_args(gc, q_ref, k_ref, v_ref, gb_ref, dz_ref):
    qs, ks, vs, gs, bs, zs = [], [], [], [], [], []
    for cc in range(gc):
        r = slice(cc * CHUNK, (cc + 1) * CHUNK)
        gbv = gb_ref[r, :]
        for h in range(DN_HEADS):
            cols = slice(h * DN_HEAD_DIM, (h + 1) * DN_HEAD_DIM)
            qs.append(q_ref[r, cols])
            ks.append(k_ref[r, cols])
            vs.append(v_ref[r, cols])
            zs.append(dz_ref[r, cols])
            gs.append(gbv[:, DN_HEADS + h:DN_HEADS + h + 1])
            bs.append(gbv[:, h:h + 1])
    return tuple(jnp.stack(t, axis=0) for t in (qs, ks, vs, gs, bs, zs))


def _dn_scan_fwd(qn, kn, vv, gb, proj_main, dn_norm_w, gc):
    s = qn.shape[0]
    nchunk = s // CHUNK
    rows = gc * CHUNK

    def body(q_ref, k_ref, v_ref, gb_ref, dz_ref, nw_ref, y_ref, ss_ref, pre_ref, uw_ref, state):
        @pl.when(pl.program_id(0) == 0)
        def _():
            state[...] = jnp.zeros_like(state)
        q, k, v, gcol, bcol, dz = _dn_block_args(gc, q_ref, k_ref, v_ref, gb_ref, dz_ref)
        y, new, (pre, uw, starts) = _dn_block(q, k, v, gcol, bcol, state[...], dz, nw_ref[...])
        state[...] = new
        ss_ref[...] = starts
        pre_ref[...] = pre
        uw_ref[...] = uw
        for cc in range(gc):
            for h in range(DN_HEADS):
                y_ref[cc * CHUNK:(cc + 1) * CHUNK, h * DN_HEAD_DIM:(h + 1) * DN_HEAD_DIM] = y[cc * DN_HEADS + h]

    tile = pl.BlockSpec((rows, D_DN), lambda i: (i, 0))
    return pl.pallas_call(
        body, name="dn_scan_fwd", grid=(nchunk // gc,),
        out_shape=(jax.ShapeDtypeStruct((s, D_DN), F32),
                   jax.ShapeDtypeStruct((nchunk * DN_HEADS, DN_HEAD_DIM, DN_HEAD_DIM), F32),
                   jax.ShapeDtypeStruct((nchunk * DN_HEADS, 2 * CHUNK, 2 * CHUNK), F32),
                   jax.ShapeDtypeStruct((nchunk * DN_HEADS, CHUNK, 2 * DN_HEAD_DIM), F32)),
        in_specs=[tile, tile, tile, pl.BlockSpec((rows, 128), lambda i: (i, 0)),
                  pl.BlockSpec((rows, D_DN), lambda i: (i, 5)), pl.BlockSpec((1, 128), lambda i: (0, 0))],
        out_specs=(tile, pl.BlockSpec((gc * DN_HEADS, DN_HEAD_DIM, DN_HEAD_DIM), lambda i: (i, 0, 0)),
                   pl.BlockSpec((gc * DN_HEADS, 2 * CHUNK, 2 * CHUNK), lambda i: (i, 0, 0)),
                   pl.BlockSpec((gc * DN_HEADS, CHUNK, 2 * DN_HEAD_DIM), lambda i: (i, 0, 0))),
        scratch_shapes=[pltpu.VMEM((DN_HEADS, DN_HEAD_DIM, DN_HEAD_DIM), F32)],
        compiler_params=_cp(("arbitrary",)),
    )(qn, kn, vv, gb, proj_main, dn_norm_w)


def _out_proj_loss(y_pool, y_dn, x2, tgt, w_out_full, fnw, tm):
    s = x2.shape[0]

    def body(yp_ref, yd_ref, x_ref, t_ref, wo_ref, fw_ref,
             dh_ref, dyp_ref, dyd_ref, gwo_ref, gfw_ref, loss_ref):
        @pl.when(pl.program_id(0) == 0)
        def _():
            gwo_ref[...] = jnp.zeros_like(gwo_ref)
            gfw_ref[...] = jnp.zeros_like(gfw_ref)
            loss_ref[...] = jnp.zeros_like(loss_ref)
        y = jnp.concatenate([yp_ref[...], yd_ref[...]], axis=1).astype(BF16)
        wo = wo_ref[...]
        h = x_ref[...] + jnp.dot(y, wo, preferred_element_type=F32)
        hn, r = _rms_hat(h)
        fw = fw_ref[...]
        err = hn * fw - t_ref[...]
        loss_ref[...] += 0.5 * jnp.sum(jnp.sum(err * err, axis=-1, keepdims=True) / D_MODEL, axis=0, keepdims=True)
        dout = err / D_MODEL
        gfw_ref[...] += jnp.sum(dout * hn, axis=0, keepdims=True)
        dhn = dout * fw
        dh = r * (dhn - hn * jnp.mean(dhn * hn, axis=-1, keepdims=True))
        dh_ref[...] = dh
        dhb = dh.astype(BF16)
        dy = _dot_nt_bf(dhb, wo)
        dyp_ref[...] = dy[:, :D_POOL]
        dyd_ref[...] = dy[:, D_POOL:]
        gwo_ref[...] += _dot_tn_bf(y, dhb)

    half = pl.BlockSpec((tm, D_POOL), lambda i: (i, 0))
    full = pl.BlockSpec((tm, D_MODEL), lambda i: (i, 0))
    return pl.pallas_call(
        body, name="out_proj_loss", grid=(s // tm,),
        out_shape=(jax.ShapeDtypeStruct((s, D_MODEL), F32), jax.ShapeDtypeStruct((s, D_POOL), F32),
                   jax.ShapeDtypeStruct((s, D_DN), F32), jax.ShapeDtypeStruct((D_MODEL, D_MODEL), F32),
                   jax.ShapeDtypeStruct((1, D_MODEL), F32), jax.ShapeDtypeStruct((1, 128), F32)),
        in_specs=[half, half, full, full, pl.BlockSpec((D_MODEL, D_MODEL), lambda i: (0, 0)),
                  pl.BlockSpec((1, D_MODEL), lambda i: (0, 0))],
        out_specs=(full, half, half, pl.BlockSpec((D_MODEL, D_MODEL), lambda i: (0, 0)),
                   pl.BlockSpec((1, D_MODEL), lambda i: (0, 0)), pl.BlockSpec((1, 128), lambda i: (0, 0))),
        compiler_params=_cp(("arbitrary",)),
    )(y_pool, y_dn, x2, tgt, w_out_full, fnw)


def _dn_scan_bwd(qn, kn, vv, gb, proj_main, dn_norm_w, states, pre, uw, dy_dn, p_out, gc):
    s = qn.shape[0]
    nchunk = s // CHUNK
    nstep = nchunk // gc
    rows = gc * CHUNK

    def body(q_ref, k_ref, v_ref, gb_ref, dz_ref, nw_ref, ss_ref, pre_ref, uw_ref, dy_ref, po_ref,
             dq_ref, dk_ref, dv_ref, dgb_ref, ddz_ref, dnw_ref, gro_ref, dstate, po_send, po_recv, rs_send, rs_recv):
        @pl.when(pl.program_id(0) == 0)
        def _():
            dstate[...] = jnp.zeros_like(dstate)
            dnw_ref[...] = jnp.zeros_like(dnw_ref)
            po_send[...] = po_ref[...].astype(BF16)
            _direct_scatter_start(po_send, po_recv, rs_send, rs_recv)

        @pl.when(pl.program_id(0) == nstep - 1)
        def _():
            _direct_scatter_wait(po_send, po_recv, rs_send, rs_recv)
            total = po_ref[_dev_index(*_mesh_pos())]
            for k in range(N_DEV - 1):
                total = total + po_recv[k].astype(F32)
            gro_ref[...] = total
        lane = lax.broadcasted_iota(jnp.int32, (CHUNK, 128), 1)
        q, k, v, gcol, bcol, dz = _dn_block_args(gc, q_ref, k_ref, v_ref, gb_ref, dz_ref)
        dy = jnp.stack([dy_ref[cc * CHUNK:(cc + 1) * CHUNK, h * DN_HEAD_DIM:(h + 1) * DN_HEAD_DIM]
                        for cc in range(gc) for h in range(DN_HEADS)], axis=0)
        known = (pre_ref[...], uw_ref[...], ss_ref[...])
        _, vjp = jax.vjp(lambda *a: _dn_block(*a, known=known)[:2], q, k, v, gcol, bcol, ss_ref[:DN_HEADS], dz,
                         nw_ref[...])
        dq, dk, dv, dg, db, dst, ddz, dnw = vjp((dy, dstate[...]))
        dstate[...] = dst
        dnw_ref[...] += dnw
        for cc in range(gc):
            r = slice(cc * CHUNK, (cc + 1) * CHUNK)
            dgb = jnp.zeros((CHUNK, 128), F32)
            for h in range(DN_HEADS):
                b = cc * DN_HEADS + h
                cols = slice(h * DN_HEAD_DIM, (h + 1) * DN_HEAD_DIM)
                for ref, val in zip((dq_ref, dk_ref, dv_ref, ddz_ref), (dq, dk, dv, ddz)):
                    ref[r, cols] = val[b]
                dgb = dgb + jnp.where(lane == h, db[b], 0.0) + jnp.where(lane == DN_HEADS + h, dg[b], 0.0)
            dgb_ref[r, :] = dgb

    rev = lambda i: (nstep - 1 - i, 0)
    tile = pl.BlockSpec((rows, D_DN), rev)
    lanes = pl.BlockSpec((rows, 128), rev)
    return pl.pallas_call(
        body, name="dn_scan_bwd", grid=(nstep,),
        out_shape=(jax.ShapeDtypeStruct((s, D_DN), F32),) * 3
        + (jax.ShapeDtypeStruct((s, 128), F32), jax.ShapeDtypeStruct((s, D_DN), F32),
           jax.ShapeDtypeStruct((1, 128), F32), jax.ShapeDtypeStruct(p_out.shape[1:], F32)),
        in_specs=[tile, tile, tile, lanes, pl.BlockSpec((rows, D_DN), lambda i: (nstep - 1 - i, 5)),
                  pl.BlockSpec((1, 128), lambda i: (0, 0)),
                  pl.BlockSpec((gc * DN_HEADS, DN_HEAD_DIM, DN_HEAD_DIM), lambda i: (nstep - 1 - i, 0, 0)),
                  pl.BlockSpec((gc * DN_HEADS, 2 * CHUNK, 2 * CHUNK), lambda i: (nstep - 1 - i, 0, 0)),
                  pl.BlockSpec((gc * DN_HEADS, CHUNK, 2 * DN_HEAD_DIM), lambda i: (nstep - 1 - i, 0, 0)), tile,
                  pl.BlockSpec(memory_space=pltpu.VMEM)],
        out_specs=(tile, tile, tile, lanes, tile, pl.BlockSpec((1, 128), lambda i: (0, 0)),
                   pl.BlockSpec(memory_space=pltpu.VMEM)),
        scratch_shapes=[pltpu.VMEM((DN_HEADS, DN_HEAD_DIM, DN_HEAD_DIM), F32),
                        pltpu.VMEM(p_out.shape, BF16), pltpu.VMEM((N_DEV - 1,) + p_out.shape[1:], BF16),
                        pltpu.SemaphoreType.DMA((7,)), pltpu.SemaphoreType.DMA((7,))],
        compiler_params=_cp(("arbitrary",)),
    )(qn, kn, vv, gb, proj_main, dn_norm_w, states, pre, uw, dy_dn, p_out)


def _dn_pre_bwd1(proj_main, proj_ba, conv_full, alog_lane, dtb_lane, dqn, dkn, dvv, dgb, tp):
    s = proj_main.shape[0]

    def body(q_ref, k_ref, v_ref, qp_ref, kp_ref, vp_ref, cw_ref, ba_ref, al_ref, db_ref,
             dqn_ref, dkn_ref, dvv_ref, dgb_ref,
             dcq_ref, dck_ref, dcv_ref, dba_ref, dcw_ref, dal_ref, ddb_ref):
        i = pl.program_id(0)

        @pl.when(i == 0)
        def _():
            dcw_ref[...] = jnp.zeros_like(dcw_ref)
            dal_ref[...] = jnp.zeros_like(dal_ref)
            ddb_ref[...] = jnp.zeros_like(ddb_ref)
        curs = (q_ref[...], k_ref[...], v_ref[...])
        tails = (_prev_tail(qp_ref, i), _prev_tail(kp_ref, i), _prev_tail(vp_ref, i))
        ys = [_conv_fwd(curs[j], tails[j], cw_ref[:, j * D_DN:(j + 1) * D_DN]) for j in range(3)]
        _, vjp = jax.vjp(_post_conv, *ys)
        dys = vjp((dqn_ref[...], dkn_ref[...], dvv_ref[...]))
        for j, (dy, out) in enumerate(zip(dys, (dcq_ref, dck_ref, dcv_ref))):
            out[...] = dy
            for sft in range(CONV_WIDTH):
                xs = curs[j] if sft == 0 else _shift_down(curs[j], tails[j], sft)
                row = CONV_WIDTH - 1 - sft
                dcw_ref[row:row + 1, j * D_DN:(j + 1) * D_DN] += jnp.sum(dy * xs, axis=0, keepdims=True)
        _, gvjp = jax.vjp(_gates, ba_ref[...], al_ref[...], db_ref[...])
        dba, dal, ddb = gvjp(dgb_ref[...])
        dba_ref[...] = dba
        dal_ref[...] += dal
        ddb_ref[...] += ddb

    cur, prev = _qkv_specs(tp, 0), _qkv_specs(tp, -1)
    row = pl.BlockSpec((1, 128), lambda i: (0, 0))
    tile = pl.BlockSpec((tp, D_DN), lambda i: (i, 0))
    lanes = pl.BlockSpec((tp, 128), lambda i: (i, 0))
    cw = pl.BlockSpec((CONV_WIDTH, 3 * D_DN), lambda i: (0, 0))
    return pl.pallas_call(
        body, name="dn_pre_bwd1", grid=(s // tp,),
        out_shape=(jax.ShapeDtypeStruct((s, D_DN), F32),) * 3
        + (jax.ShapeDtypeStruct((s, 128), F32), jax.ShapeDtypeStruct((CONV_WIDTH, 3 * D_DN), F32),
           jax.ShapeDtypeStruct((1, 128), F32), jax.ShapeDtypeStruct((1, 128), F32)),
        in_specs=[cur(2), cur(3), cur(4), prev(2), prev(3), prev(4), cw, lanes, row, row, tile, tile, tile, lanes],
        out_specs=(tile, tile, tile, lanes, cw, row, row),
        compiler_params=_cp(("arbitrary",)),
    )(proj_main, proj_main, proj_main, proj_main, proj_main, proj_main, conv_full, proj_ba, alog_lane, dtb_lane,
      dqn, dkn, dvv, dgb)


def _pool_bwd1(proj_main, pool_w, pool_scale, dyp, tp):
    s = proj_main.shape[0]

    def body(u_ref, up_ref, z_ref, pw_ref, ps_ref, dy_ref, dz_ref, dwin_ref, dpw_ref, dps_ref):
        i = pl.program_id(0)

        @pl.when(i == 0)
        def _():
            dpw_ref[...] = jnp.zeros_like(dpw_ref)
            dps_ref[...] = jnp.zeros_like(dps_ref)
        u = u_ref[...]
        z = z_ref[...]
        dy = dy_ref[...]
        ps = ps_ref[...]
        mixes = _pool_mix(u, _prev_tail(up_ref, i), i, tp)
        sg = jax.nn.sigmoid(z)
        sz = z * sg
        dsz = sg * (1.0 + z * (1.0 - sg))
        for gi, w in enumerate(POOL_WINDOWS):
            cols = slice(gi * POOL_GROUP, (gi + 1) * POOL_GROUP)
            mixw = _dot_bf(mixes[gi], pw_ref[gi])
            dmixw = dy[:, cols] * ps[:, cols] * sz[:, cols]
            dps_ref[:, cols] += jnp.sum(dy[:, cols] * mixw * sz[:, cols], axis=0, keepdims=True)
            dz_ref[:, cols] = dy[:, cols] * mixw * ps[:, cols] * dsz[:, cols]
            dpw_ref[gi] += _dot_tn_bf(mixes[gi], dmixw)
            dmix = _dot_nt_bf(dmixw, pw_ref[gi])
            dwin_ref[:, cols] = dmix / _pool_counts(i, tp, w)

    tile = pl.BlockSpec((tp, D_POOL), lambda i: (i, 0))
    pw = pl.BlockSpec((4, POOL_GROUP, POOL_GROUP), lambda i: (0, 0, 0))
    ps = pl.BlockSpec((1, D_POOL), lambda i: (0, 0))
    return pl.pallas_call(
        body, name="pool_bwd1", grid=(s // tp,),
        out_shape=(jax.ShapeDtypeStruct((s, D_POOL), F32), jax.ShapeDtypeStruct((s, D_POOL), F32),
                   jax.ShapeDtypeStruct((4, POOL_GROUP, POOL_GROUP), F32), jax.ShapeDtypeStruct((1, D_POOL), F32)),
        in_specs=[tile, _prev_halo_spec(tp, D_POOL, 0),
                  pl.BlockSpec((tp, D_POOL), lambda i: (i, 1)), pw, ps, tile],
        out_specs=(tile, tile, pw, ps),
        compiler_params=_cp(("arbitrary",)),
    )(proj_main, proj_main, proj_main, pool_w, pool_scale, dyp)


def _back(proj_main, proj_ba, dyp, dqn, dkn, dvv, dgb, ddz, x2, dh, norm_w, pool_w, pool_scale, conv_full,
          alog_lane, dtb_lane, wt_full, tm):
    s = x2.shape[0]
    nstep = s // tm
    per = tm // HALO

    def body(u_ref, z_ref, q_ref, k_ref, v_ref, up_ref, qp_ref, kp_ref, vp_ref, ba_ref,
             dyp_ref, dqn_ref, dkn_ref, dvv_ref, dgb_ref, ddz_ref, x_ref, dh_ref,
             nw_ref, pw_ref, ps_ref, cw_ref, al_ref, db_ref, wt_hbm,
             gx_ref, p_hbm, gnw_ref, dpw_ref, dps_ref, dcw_ref, dal_ref, ddb_ref,
             wt_vmem, acc, blk, head_dc, head_dw, sem, osem):
        j = pl.program_id(0)
        i = nstep - 1 - j

        @pl.when(j == 0)
        def _():
            cp = pltpu.make_async_copy(wt_hbm, wt_vmem, sem)
            cp.start()
            acc[...] = jnp.zeros_like(acc)
            for ref in (gnw_ref, dpw_ref, dps_ref, dcw_ref, dal_ref, ddb_ref, head_dc, head_dw):
                ref[...] = jnp.zeros_like(ref)
            cp.wait()

        u = u_ref[...]
        z = z_ref[...]
        dy = dyp_ref[...]
        ps = ps_ref[...]
        mixes = _pool_mix(u, _prev_tail(up_ref, i), i, tm)
        sg = jax.nn.sigmoid(z)
        sz = z * sg
        dsz = sg * (1.0 + z * (1.0 - sg))
        dzs, dwins = [], []
        for gi, w in enumerate(POOL_WINDOWS):
            cols = slice(gi * POOL_GROUP, (gi + 1) * POOL_GROUP)
            mixw = _dot_bf(mixes[gi], pw_ref[gi])
            dmixw = dy[:, cols] * ps[:, cols] * sz[:, cols]
            dps_ref[:, cols] += jnp.sum(dy[:, cols] * mixw * sz[:, cols], axis=0, keepdims=True)
            dzs.append(dy[:, cols] * mixw * ps[:, cols] * dsz[:, cols])
            dpw_ref[gi] += _dot_tn_bf(mixes[gi], dmixw)
            dwins.append(_dot_nt_bf(dmixw, pw_ref[gi]) / _pool_counts(i, tm, w))
        dzp = jnp.concatenate(dzs, axis=1)
        dw = jnp.concatenate(dwins, axis=1)
        ext = jnp.concatenate([dw, head_dw[...]], axis=0)
        m = ext.shape[0]
        a2 = ext + pltpu.roll(ext, m - 1, 0)
        a4 = a2 + pltpu.roll(a2, m - 2, 0)
        a8 = a4 + pltpu.roll(a4, m - 4, 0)
        a16 = a8 + pltpu.roll(a8, m - 8, 0)
        dup = jnp.concatenate(
            [acc_w[:tm, gi * POOL_GROUP:(gi + 1) * POOL_GROUP]
             - dw[:, gi * POOL_GROUP:(gi + 1) * POOL_GROUP] * _pool_counts(i, tm, w)
             for gi, (w, acc_w) in enumerate(zip(POOL_WINDOWS, (a2, a4, a8, a16)))], axis=1)
        head_dw[...] = dw[:HALO, :]

        curs = (q_ref[...], k_ref[...], v_ref[...])
        tails = (_prev_tail(qp_ref, i), _prev_tail(kp_ref, i), _prev_tail(vp_ref, i))
        taps = [_conv_taps(curs[c], tails[c]) for c in range(3)]
        ys = [_conv_of_taps(taps[c], cw_ref[:, c * D_DN:(c + 1) * D_DN]) for c in range(3)]
        _, vjp = jax.vjp(_post_conv, *ys)
        dys = vjp((dqn_ref[...], dkn_ref[...], dvv_ref[...]))
        dxs = []
        for c, dyc in enumerate(dys):
            cols = slice(c * D_DN, (c + 1) * D_DN)
            w4 = cw_ref[:, cols]
            for sft in range(CONV_WIDTH):
                row = CONV_WIDTH - 1 - sft
                dcw_ref[row:row + 1, cols] += jnp.sum(dyc * taps[c][sft], axis=0, keepdims=True)
            head = head_dc[:, cols]
            dx = dyc * w4[CONV_WIDTH - 1:CONV_WIDTH, :]
            for sft in range(1, CONV_WIDTH):
                dx = dx + _shift_up(dyc, head, sft) * w4[CONV_WIDTH - 1 - sft:CONV_WIDTH - sft, :]
            dxs.append(dx)
            head_dc[:, cols] = dyc[:HALO, :]
        _, gvjp = jax.vjp(_gates, ba_ref[...], al_ref[...], db_ref[...])
        dba, dal, ddb = gvjp(dgb_ref[...])
        dal_ref[...] += dal
        ddb_ref[...] += ddb

        dbab = dba.astype(BF16)
        xhat, r = _rms_hat(x_ref[...])
        nw = nw_ref[...]
        n = (xhat * nw).astype(BF16)
        acc[D_MAIN:, :] += _dot_tn_bf(dbab, n)
        dn = jnp.dot(dbab, wt_vmem[D_MAIN:, :], preferred_element_type=F32)
        for cb, d in enumerate((dup, dzp, dxs[0], dxs[1], dxs[2], ddz_ref[...])):
            rows = slice(cb * D_POOL, (cb + 1) * D_POOL)
            dpart = d.astype(BF16)
            acc[rows, :] += _dot_tn_bf(dpart, n)
            dn = dn + jnp.dot(dpart, wt_vmem[rows, :], preferred_element_type=F32)
        gnw_ref[...] += jnp.sum(dn * xhat, axis=0, keepdims=True)
        dxh = dn * nw
        gx_ref[...] = dh_ref[...] + r * (dxh - xhat * jnp.mean(dxh * xhat, axis=-1, keepdims=True))

        @pl.when(j == nstep - 1)
        def _():
            def out(d):
                return pltpu.make_async_copy(blk.at[d % 2], p_hbm.at[d], osem.at[d % 2])
            for d in range(N_DEV):
                if d >= 2:
                    out(d - 2).wait()
                blk[d % 2] = acc[W_IN_SHARD * d:W_IN_SHARD * (d + 1), :]
                out(d).start()
            out(N_DEV - 2).wait()
            out(N_DEV - 1).wait()

    def col(c):
        return pl.BlockSpec((tm, D_POOL), lambda j: (nstep - 1 - j, c))

    def halo(c):
        return pl.BlockSpec((HALO, D_POOL), lambda j: (jnp.maximum((nstep - 1 - j) * per - 1, 0), c))

    rev = lambda j: (nstep - 1 - j, 0)
    part = pl.BlockSpec((tm, D_POOL), rev)
    lanes = pl.BlockSpec((tm, 128), rev)
    full = pl.BlockSpec((tm, D_MODEL), rev)
    row = pl.BlockSpec((1, D_MODEL), lambda j: (0, 0))
    lrow = pl.BlockSpec((1, 128), lambda j: (0, 0))
    pw = pl.BlockSpec((4, POOL_GROUP, POOL_GROUP), lambda j: (0, 0, 0))
    psp = pl.BlockSpec((1, D_POOL), lambda j: (0, 0))
    cw = pl.BlockSpec((CONV_WIDTH, 3 * D_DN), lambda j: (0, 0))
    return pl.pallas_call(
        body, name="back", grid=(nstep,),
        out_shape=(jax.ShapeDtypeStruct((s, D_MODEL), F32),
                   jax.ShapeDtypeStruct((N_DEV, W_IN_SHARD, D_MODEL), F32), jax.ShapeDtypeStruct((1, D_MODEL), F32),
                   jax.ShapeDtypeStruct((4, POOL_GROUP, POOL_GROUP), F32), jax.ShapeDtypeStruct((1, D_POOL), F32),
                   jax.ShapeDtypeStruct((CONV_WIDTH, 3 * D_DN), F32),
                   jax.ShapeDtypeStruct((1, 128), F32), jax.ShapeDtypeStruct((1, 128), F32)),
        in_specs=[col(0), col(1), col(2), col(3), col(4), halo(0), halo(2), halo(3), halo(4), lanes,
                  part, part, part, part, lanes, part, full, full,
                  row, pw, psp, cw, lrow, lrow, pl.BlockSpec(memory_space=pl.ANY)],
        out_specs=(full, pl.BlockSpec(memory_space=pl.ANY), row, pw, psp, cw, lrow, lrow),
        scratch_shapes=[pltpu.VMEM((D_IN_PAD, D_MODEL), BF16), pltpu.VMEM((D_IN_PAD, D_MODEL), F32),
                        pltpu.VMEM((2, W_IN_SHARD, D_MODEL), F32),
                        pltpu.VMEM((HALO, 3 * D_DN), F32), pltpu.VMEM((HALO, D_POOL), F32),
                        pltpu.SemaphoreType.DMA, pltpu.SemaphoreType.DMA((2,))],
        compiler_params=_cp(("arbitrary",)),
    )(proj_main, proj_main, proj_main, proj_main, proj_main, proj_main, proj_main, proj_main, proj_main, proj_ba,
      dyp, dqn, dkn, dvv, dgb, ddz, x2, dh, norm_w, pool_w, pool_scale, conv_full, alog_lane, dtb_lane, wt_full)


def _in_proj_bwd(dwin, dzp, dcq, dck, dcv, ddz, dba, x2, dh, norm_w, wt_full, conv_full, tm):
    s = x2.shape[0]
    nstep = s // tm

    def body(dwin_ref, dzp_ref, dcq_ref, dck_ref, dcv_ref, ddz_ref, dba_ref, x_ref, dh_ref, nw_ref, cw_ref, wt_hbm,
             gx_ref, p_hbm, gnw_ref, wt_vmem, acc, blk, head_dc, head_dw, sem, osem):
        j = pl.program_id(0)
        i = nstep - 1 - j

        @pl.when(j == 0)
        def _():
            cp = pltpu.make_async_copy(wt_hbm, wt_vmem, sem)
            cp.start()
            acc[...] = jnp.zeros_like(acc)
            gnw_ref[...] = jnp.zeros_like(gnw_ref)
            head_dc[...] = jnp.zeros_like(head_dc)
            head_dw[...] = jnp.zeros_like(head_dw)
            cp.wait()
        dw = dwin_ref[...]
        ext = jnp.concatenate([dw, head_dw[...]], axis=0)
        m = ext.shape[0]
        a2 = ext + pltpu.roll(ext, m - 1, 0)
        a4 = a2 + pltpu.roll(a2, m - 2, 0)
        a8 = a4 + pltpu.roll(a4, m - 4, 0)
        a16 = a8 + pltpu.roll(a8, m - 8, 0)
        dup = jnp.concatenate(
            [acc_w[:tm, gi * POOL_GROUP:(gi + 1) * POOL_GROUP]
             - dw[:, gi * POOL_GROUP:(gi + 1) * POOL_GROUP] * _pool_counts(i, tm, w)
             for gi, (w, acc_w) in enumerate(zip(POOL_WINDOWS, (a2, a4, a8, a16)))], axis=1)
        head_dw[...] = dw[:HALO, :]
        dxs = []
        for c, ref in enumerate((dcq_ref, dck_ref, dcv_ref)):
            cols = slice(c * D_DN, (c + 1) * D_DN)
            w4 = cw_ref[:, cols]
            dy = ref[...]
            head = head_dc[:, cols]
            dx = dy * w4[CONV_WIDTH - 1:CONV_WIDTH, :]
            for sft in range(1, CONV_WIDTH):
                dx = dx + _shift_up(dy, head, sft) * w4[CONV_WIDTH - 1 - sft:CONV_WIDTH - sft, :]
            dxs.append(dx)
            head_dc[:, cols] = dy[:HALO, :]
        dbab = dba_ref[...].astype(BF16)
        xhat, r = _rms_hat(x_ref[...])
        nw = nw_ref[...]
        n = (xhat * nw).astype(BF16)
        acc[D_MAIN:, :] += _dot_tn_bf(dbab, n)
        dn = jnp.dot(dbab, wt_vmem[D_MAIN:, :], preferred_element_type=F32)
        for cb, d in enumerate((dup, dzp_ref[...], dxs[0], dxs[1], dxs[2], ddz_ref[...])):
            rows = slice(cb * D_POOL, (cb + 1) * D_POOL)
            dpart = d.astype(BF16)
            acc[rows, :] += _dot_tn_bf(dpart, n)
            dn = dn + jnp.dot(dpart, wt_vmem[rows, :], preferred_element_type=F32)
        gnw_ref[...] += jnp.sum(dn * xhat, axis=0, keepdims=True)
        dxh = dn * nw
        gx_ref[...] = dh_ref[...] + r * (dxh - xhat * jnp.mean(dxh * xhat, axis=-1, keepdims=True))

        @pl.when(j == nstep - 1)
        def _():
            def out(d):
                return pltpu.make_async_copy(blk.at[d % 2], p_hbm.at[d], osem.at[d % 2])
            for d in range(N_DEV):
                if d >= 2:
                    out(d - 2).wait()
                blk[d % 2] = acc[W_IN_SHARD * d:W_IN_SHARD * (d + 1), :]
                out(d).start()
            out(N_DEV - 2).wait()
            out(N_DEV - 1).wait()

    rev = lambda j: (nstep - 1 - j, 0)
    part = pl.BlockSpec((tm, D_POOL), rev)
    full = pl.BlockSpec((tm, D_MODEL), rev)
    row = pl.BlockSpec((1, D_MODEL), lambda j: (0, 0))
    return pl.pallas_call(
        body, name="in_proj_bwd", grid=(nstep,),
        out_shape=(jax.ShapeDtypeStruct((s, D_MODEL), F32),
                   jax.ShapeDtypeStruct((N_DEV, W_IN_SHARD, D_MODEL), F32), jax.ShapeDtypeStruct((1, D_MODEL), F32)),
        in_specs=[part] * 6 + [pl.BlockSpec((tm, 128), rev), full, full, row,
                               pl.BlockSpec((CONV_WIDTH, 3 * D_DN), lambda j: (0, 0)),
                               pl.BlockSpec(memory_space=pl.ANY)],
        out_specs=(full, pl.BlockSpec(memory_space=pl.ANY), row),
        scratch_shapes=[pltpu.VMEM((D_IN_PAD, D_MODEL), BF16), pltpu.VMEM((D_IN_PAD, D_MODEL), F32),
                        pltpu.VMEM((2, W_IN_SHARD, D_MODEL), F32),
                        pltpu.VMEM((HALO, 3 * D_DN), F32), pltpu.VMEM((HALO, D_POOL), F32),
                        pltpu.SemaphoreType.DMA, pltpu.SemaphoreType.DMA((2,))],
        compiler_params=_cp(("arbitrary",)),
    )(dwin, dzp, dcq, dck, dcv, ddz, dba, x2, dh, norm_w, conv_full, wt_full)


def _adamw_math(w, g, m, v):
    m = ADAM_B1 * m + (1.0 - ADAM_B1) * g
    v = ADAM_B2 * v + (1.0 - ADAM_B2) * (g * g)
    m_hat = m / (1.0 - ADAM_B1 ** ADAM_STEP)
    v_hat = v / (1.0 - ADAM_B2 ** ADAM_STEP)
    delta = -ADAM_LR * (m_hat / (jnp.sqrt(v_hat) + ADAM_EPS) + ADAM_WD * w)
    return delta, m, v


def _adamw_sharded(params):
    k = len(params)

    def body(*refs):
        ins, outs = refs[:4 * k], refs[4 * k:]
        for p in range(k):
            w_ref, g_ref, m_ref, v_ref = ins[4 * p:4 * p + 4]
            go_ref = outs[4 * p]
            if g_ref.shape == w_ref.shape:
                go_ref[...] = g_ref[...]
            else:
                for j in range(FLAT_ROWS):
                    go_ref[pl.ds(j, W_IN_SHARD, stride=FLAT_ROWS), :] = g_ref[:, 128 * j:128 * (j + 1)]
            d, nm, nv = _adamw_math(w_ref[...], go_ref[...], m_ref[...], v_ref[...])
            outs[4 * p + 1][...] = d
            outs[4 * p + 2][...] = nm
            outs[4 * p + 3][...] = nv

    flat = [a for p in params for a in p]
    out_shape = tuple(jax.ShapeDtypeStruct(p[0].shape, F32) for p in params for _ in range(4))
    res = pl.pallas_call(body, name="adamw_sharded", out_shape=out_shape, compiler_params=_cp())(*flat)
    return [tuple(res[4 * p:4 * p + 4]) for p in range(k)]


def _adamw_replicated(gath_a, gath_b, pool, rows):
    nrow = len(rows)

    def body(*refs):
        ga_ref, gb_ref = refs[:2]
        ins = refs[2:2 + 3 * (nrow + 1)]
        outs = refs[2 + 3 * (nrow + 1):]

        def total(ref):
            g = ref[0]
            for d in range(1, N_DEV):
                g = g + ref[d]
            return g

        def update(g, wmv, o):
            w, m, v = (r[...] for r in wmv)
            dl, nm, nv = _adamw_math(w, g, m, v)
            o[0][...] = g
            o[1][...] = dl
            o[2][...] = nm
            o[3][...] = nv

        update(total(ga_ref), ins[:3], outs[:4])
        gb = total(gb_ref)
        for r in range(nrow):
            n = ins[3 * (r + 1)].shape[1]
            update(gb[r:r + 1, :n], ins[3 * (r + 1):3 * (r + 2)], outs[4 * (r + 1):4 * (r + 2)])
        outs[4 * (nrow + 1)][...] = gb[nrow:nrow + 1, 0:1]

    flat = list(pool) + [a for wmv in rows for a in wmv]
    out_shape = ((jax.ShapeDtypeStruct(pool[0].shape, F32),) * 4
                 + tuple(jax.ShapeDtypeStruct(wmv[0].shape, F32) for wmv in rows for _ in range(4))
                 + (jax.ShapeDtypeStruct((1, 1), F32),))
    res = pl.pallas_call(body, name="adamw_replicated", out_shape=out_shape, compiler_params=_cp())(
        gath_a, gath_b, *flat)
    return [res[4 * k:4 * k + 4] for k in range(nrow + 1)], res[-1]


_ROW_ORDER = ("norm_w", "final_norm_w", "pool_scale", "dn_norm_w", "a_log", "dt_bias")


def _pack_rows(vectors):
    out = [jnp.pad(v.reshape(-1), (0, D_MODEL - v.size)) for v in vectors]
    out += [jnp.zeros((D_MODEL,), F32)] * (8 - len(out))
    return jnp.stack(out, axis=0)


def _lane_row(vec4, start):
    return jnp.pad(vec4.reshape(-1), (start, 128 - start - vec4.size)).reshape(1, 128)


def kernel(x, norm_w, w_in, pool_w, pool_scale, conv_w, a_log, dt_bias, dn_norm_w, w_out, final_norm_w, loss_target, m_norm_w, m_w_in, m_pool_w, m_pool_scale, m_conv_w, m_a_log, m_dt_bias, m_dn_norm_w, m_w_out, m_final_norm_w, v_norm_w, v_w_in, v_pool_w, v_pool_scale, v_conv_w, v_a_log, v_dt_bias, v_dn_norm_w, v_w_out, v_final_norm_w):
    s = x.shape[1]
    tm = min(512, s)
    tmb = min(256, s)
    tp = min(512, s)
    x2 = x[0]
    tgt = loss_target[0]
    def to_flat(a):
        return a[0].reshape(FLAT_ROWS, 128, W_IN_SHARD).transpose(2, 0, 1).reshape(W_IN_SHARD * FLAT_ROWS, 128)

    def from_flat(f):
        return f.reshape(W_IN_SHARD, FLAT_ROWS, 128).transpose(1, 2, 0).reshape(1, D_MODEL, W_IN_SHARD)

    wf, m_wf, v_wf = to_flat(w_in), to_flat(m_w_in), to_flat(v_w_in)

    g_in, g_conv = _gather_weights(wf, conv_w[0])
    conv_full = g_conv.transpose(1, 0, 2).reshape(CONV_WIDTH, 3 * D_DN)
    alog_lane = _lane_row(a_log, DN_HEADS)
    dtb_lane = _lane_row(dt_bias, DN_HEADS)
    fnw = final_norm_w.reshape(1, D_MODEL)

    proj_main, proj_ba, y_pool, qn, kn, vv, gb, g_out, wt_full = _front(
        x2, norm_w, g_in, pool_w[0], pool_scale, conv_full, alog_lane, dtb_lane, w_out[0], tm)
    w_out_full = g_out.reshape(D_MODEL, D_MODEL)
    y_dn, states, dn_pre, dn_uw = _dn_scan_fwd(qn, kn, vv, gb, proj_main, dn_norm_w, DN_CHUNKS_PER_STEP)

    dh, dyp, dyd, g_wout, g_fnw, loss_part = _out_proj_loss(y_pool, y_dn, x2, tgt, w_out_full, fnw, tm)
    p_out = g_wout.reshape(N_DEV, D_MODEL // N_DEV, D_MODEL)
    dqn, dkn, dvv, dgb, ddz, g_dnw, gr_out = _dn_scan_bwd(qn, kn, vv, gb, proj_main, dn_norm_w, states, dn_pre, dn_uw,
                                                          dyd, p_out, DN_CHUNKS_PER_STEP)
    grad_x2, p_in, g_nw, g_pw, g_ps, g_conv_full, g_al, g_db = _back(
        proj_main, proj_ba, dyp, dqn, dkn, dvv, dgb, ddz, x2, dh, norm_w, pool_w[0], pool_scale, conv_full,
        alog_lane, dtb_lane, wt_full, tmb)

    p_conv = g_conv_full.reshape(CONV_WIDTH, N_DEV, 3 * D_DN // N_DEV).transpose(1, 0, 2)
    pack_a = g_pw.reshape(4 * POOL_GROUP, POOL_GROUP)
    pack_b = _pack_rows([g_nw, g_fnw, g_ps, g_dnw, g_al[0, DN_HEADS:2 * DN_HEADS], g_db[0, DN_HEADS:2 * DN_HEADS],
                         loss_part[0, :1]])
    gr_in, gr_conv, gath_a, gath_b = _reduce_grads((p_in, p_conv), pack_a, pack_b)

    r_in, r_out, r_conv = _adamw_sharded([(wf, gr_in, m_wf, v_wf), (w_out[0], gr_out, m_w_out[0], v_w_out[0]),
                                          (conv_w[0], gr_conv, m_conv_w[0], v_conv_w[0])])
    flat = lambda a: a.reshape(4 * POOL_GROUP, POOL_GROUP)
    row = lambda a: a.reshape(1, -1)
    vecs = {"norm_w": (norm_w, m_norm_w, v_norm_w), "final_norm_w": (final_norm_w, m_final_norm_w, v_final_norm_w),
            "pool_scale": (pool_scale, m_pool_scale, v_pool_scale), "dn_norm_w": (dn_norm_w, m_dn_norm_w, v_dn_norm_w),
            "a_log": (a_log, m_a_log, v_a_log), "dt_bias": (dt_bias, m_dt_bias, v_dt_bias)}
    res, loss = _adamw_replicated(gath_a, gath_b, (flat(pool_w), flat(m_pool_w), flat(v_pool_w)),
                                  [tuple(row(a) for a in vecs[nm]) for nm in _ROW_ORDER])
    r_pool = res[0]
    r_vec = dict(zip(_ROW_ORDER, res[1:]))

    def group(k):
        vec = lambda nm: r_vec[nm][k].reshape(vecs[nm][0].shape)
        return (vec("norm_w"), from_flat(r_in[k]), r_pool[k].reshape(pool_w.shape), vec("pool_scale"), r_conv[k][None],
                vec("a_log"), vec("dt_bias"), vec("dn_norm_w"), r_out[k][None], vec("final_norm_w"))

    return (loss[0, 0], grad_x2[None], *group(0), *group(1), *group(2), *group(3))
```

```python
import functools

import jax
import jax.numpy as jnp
from jax import lax
from jax.experimental import pallas as pl
from jax.experimental.pallas import tpu as pltpu

F32 = jnp.float32
BF16 = jnp.bfloat16
HI = lax.Precision.HIGHEST
MESH = pl.DeviceIdType.MESH

D_MODEL = 1024
D_POOL = 512
D_DN = 512
POOL_WINDOWS = (2, 4, 8, 16)
POOL_GROUP = 128
DN_HEADS = 4
DN_HEAD_DIM = 128
CONV_WIDTH = 4
CHUNK = 64
NORM_EPS = 1e-6
D_IN = 3080
D_MAIN = 3072
FLAT_ROWS = D_MODEL // 128
D_IN_PAD = D_MAIN + 128
N_DEV = 8
W_IN_SHARD = D_IN // N_DEV
HALO = 16
DN_CHUNKS_PER_STEP = 8

ADAM_LR = 0.001
ADAM_B1 = 0.9
ADAM_B2 = 0.999
ADAM_EPS = 1e-08
ADAM_WD = 0.01
ADAM_STEP = 10

VMEM_LIMIT = 56 * 1024 * 1024

def _cp(sem=None, vmem=VMEM_LIMIT):
    kw = {"vmem_limit_bytes": vmem}
    if sem is not None:
        kw["dimension_semantics"] = sem
    return pltpu.CompilerParams(**kw)


def _dot_bf(a, b):
    return jnp.dot(a.astype(BF16), b.astype(BF16), preferred_element_type=F32)


def _dot_nt_bf(a, b):
    return lax.dot_general(a.astype(BF16), b.astype(BF16), (((1,), (1,)), ((), ())), preferred_element_type=F32)


def _dot_tn_bf(a, b):
    return lax.dot_general(a.astype(BF16), b.astype(BF16), (((0,), (0,)), ((), ())), preferred_element_type=F32)


def _mm_raw(a, b, ca, cb, prec):
    off = a.ndim - 2
    dn = (((ca + off,), (cb + off,)), ((0,), (0,)) if off else ((), ()))
    if prec == "hi":
        return lax.dot_general(a, b, dn, precision=HI, preferred_element_type=F32)
    ah, bh = a.astype(BF16), b.astype(BF16)
    out = lax.dot_general(ah, bh, dn, preferred_element_type=F32)
    if prec == "x3":
        al = (a - ah.astype(F32)).astype(BF16)
        bl = (b - bh.astype(F32)).astype(BF16)
        out = out + lax.dot_general(ah, bl, dn, preferred_element_type=F32)
        out = out + lax.dot_general(al, bh, dn, preferred_element_type=F32)
    return out


@functools.partial(jax.custom_vjp, nondiff_argnums=(2, 3, 4, 5))
def _mm(a, b, ca, cb, prec, bprec):
    return _mm_raw(a, b, ca, cb, prec)


def _mm_fwd(a, b, ca, cb, prec, bprec):
    return _mm_raw(a, b, ca, cb, prec), (a, b)


def _mm_bwd(ca, cb, prec, bprec, res, dc):
    a, b = res
    da = _mm_raw(dc, b, 1, 1 - cb, bprec) if ca == 1 else _mm_raw(b, dc, 1 - cb, 1, bprec)
    db = _mm_raw(a, dc, 1 - ca, 0, bprec) if cb == 0 else _mm_raw(dc, a, 0, 1 - ca, bprec)
    return da, db


_mm.defvjp(_mm_fwd, _mm_bwd)


@functools.partial(jax.custom_vjp, nondiff_argnums=(1, 2))
def _tri_inv(a, prec, bprec):
    n = a.shape[-1]
    ii = lax.broadcasted_iota(jnp.int32, (n, n), 0)
    jj = lax.broadcasted_iota(jnp.int32, (n, n), 1)
    p = (ii == jj).astype(F32) - a
    b = _mm_raw(a, a, 1, 0, prec)
    for _ in range(4):
        pb = _mm_raw(jnp.concatenate([p, b], axis=-2), b, 1, 0, prec)
        p = p + pb[..., :n, :]
        b = pb[..., n:, :]
    return p + _mm_raw(p, b, 1, 0, prec)


def _tri_inv_fwd(a, prec, bprec):
    t = _tri_inv(a, prec, bprec)
    return t, t


def _tri_inv_bwd(prec, bprec, t, dt):
    return (-_mm_raw(_mm_raw(t, dt, 0, 0, bprec), t, 1, 1, bprec),)


_tri_inv.defvjp(_tri_inv_fwd, _tri_inv_bwd)

@functools.partial(jax.custom_vjp, nondiff_argnums=(3, 4, 5))
def _mm_known(a, b, out, ca, cb, bprec):
    return out


def _mm_known_fwd(a, b, out, ca, cb, bprec):
    return out, (a, b)


def _mm_known_bwd(ca, cb, bprec, res, dc):
    return _mm_bwd(ca, cb, None, bprec, res, dc) + (jnp.zeros_like(dc),)


_mm_known.defvjp(_mm_known_fwd, _mm_known_bwd)


@jax.custom_vjp
def _use_known(x, known):
    return known


_use_known.defvjp(lambda x, known: (known, None), lambda _, g: (g, jnp.zeros_like(g)))


@functools.partial(jax.custom_vjp, nondiff_argnums=(2,))
def _tri_inv_known(a, t, bprec):
    return t


def _tri_inv_known_fwd(a, t, bprec):
    return t, t


def _tri_inv_known_bwd(bprec, t, dt):
    return _tri_inv_bwd(None, bprec, t, dt) + (jnp.zeros_like(dt),)


_tri_inv_known.defvjp(_tri_inv_known_fwd, _tri_inv_known_bwd)

_DN_PREC = {"akq": ("bf16", "bf16"), "inv": ("bf16", "bf16"), "uw": ("bf16", "bf16"), "ws": ("bf16", "bf16"),
            "ov": ("bf16", "bf16"), "st": ("bf16", "bf16")}


def _silu(x):
    return x * jax.nn.sigmoid(x)


def _softplus(x):
    pos = x > 0.0
    return jnp.where(pos, x, 0.0) + jnp.log1p(jnp.exp(jnp.where(pos, -x, x)))


def _mesh_pos():
    return lax.axis_index("x"), lax.axis_index("y"), lax.axis_index("c")


def _dev_index(x, y, c):
    return 4 * x + 2 * y + c


def _relay_order():
    x, y, c = _mesh_pos()
    n1 = (x + (1 - c) * (1 - 2 * x), y + c * (1 - 2 * y))
    n2 = (x + c * (1 - 2 * x), y + (1 - c) * (1 - 2 * y))
    return (x, y, c), (x, y, 1 - c), n1, n2, (1 - x, 1 - y)


def _all_gather_blocks(outs, send_sems, recv_sems):
    me, sibling, n1, n2, diag = _relay_order()
    c = me[2]

    def copy(a, k, block, to):
        rows = outs[a].at[_dev_index(*block)]
        return pltpu.make_async_remote_copy(src_ref=rows, dst_ref=rows, send_sem=send_sems.at[a, k],
                                            recv_sem=recv_sems.at[a, k], device_id=to, device_id_type=MESH)

    n = len(outs)
    started = []

    def start(cp):
        cp.start()
        started.append(cp)

    for a in range(n):
        start(copy(a, 1, me, (*n1, c)))
        start(copy(a, 2, me, (*n2, c)))
        start(copy(a, 0, me, sibling))
    for a in range(n):
        copy(a, 1, (*n1, c), me).wait_recv()
        start(copy(a, 3, (*n1, c), (*n2, c)))
        start(copy(a, 4, (*n1, c), sibling))
    for a in range(n):
        copy(a, 2, (*n2, c), me).wait_recv()
        start(copy(a, 5, (*n2, c), sibling))
        copy(a, 3, (*diag, c), me).wait_recv()
        start(copy(a, 6, (*diag, c), sibling))
    for a in range(n):
        copy(a, 0, sibling, me).wait_recv()
        copy(a, 4, (*n2, 1 - c), me).wait_recv()
        copy(a, 5, (*n1, 1 - c), me).wait_recv()
        copy(a, 6, (*diag, 1 - c), me).wait_recv()
    for cp in started:
        cp.wait_send()


def _peer_relations():
    x, y, c = _mesh_pos()
    flips = [(fx, fy, fc) for fx in (0, 1) for fy in (0, 1) for fc in (0, 1)][1:]
    peers = [(1 - x if fx else x, 1 - y if fy else y, 1 - c if fc else c) for fx, fy, fc in flips]
    return (x, y, c), peers


def _direct_gather_start(out_ref, send_sems, recv_sems):
    me, peers = _peer_relations()
    rows = out_ref.at[_dev_index(*me)]
    for k, peer in enumerate(peers):
        pltpu.make_async_remote_copy(src_ref=rows, dst_ref=rows, send_sem=send_sems.at[k], recv_sem=recv_sems.at[k],
                                     device_id=peer, device_id_type=MESH).start()


def _direct_gather_wait(out_ref, send_sems, recv_sems):
    me, peers = _peer_relations()
    for k, peer in enumerate(peers):
        rows = out_ref.at[_dev_index(*peer)]
        cp = pltpu.make_async_remote_copy(src_ref=rows, dst_ref=rows, send_sem=send_sems.at[k],
                                          recv_sem=recv_sems.at[k], device_id=peer, device_id_type=MESH)
        cp.wait_recv()
        cp.wait_send()


def _direct_scatter_start(send_ref, recv_ref, send_sems, recv_sems):
    me, peers = _peer_relations()
    for k, peer in enumerate(peers):
        pltpu.make_async_remote_copy(src_ref=send_ref.at[_dev_index(*peer)], dst_ref=recv_ref.at[k],
                                     send_sem=send_sems.at[k], recv_sem=recv_sems.at[k],
                                     device_id=peer, device_id_type=MESH).start()


def _direct_scatter_wait(send_ref, recv_ref, send_sems, recv_sems):
    me, peers = _peer_relations()
    for k, peer in enumerate(peers):
        cp = pltpu.make_async_remote_copy(src_ref=send_ref.at[_dev_index(*peer)], dst_ref=recv_ref.at[k],
                                          send_sem=send_sems.at[k], recv_sem=recv_sems.at[k],
                                          device_id=peer, device_id_type=MESH)
        cp.wait_recv()
        cp.wait_send()


def _gather_weights(w_in_flat, conv_blk):
    def body(win_ref, conv_ref, gin_ref, gconv_ref, send_sems, recv_sems):
        x, y, c = _mesh_pos()
        me = _dev_index(x, y, c)
        for j in range(FLAT_ROWS):
            gin_ref[me, :, 128 * j:128 * (j + 1)] = win_ref[pl.ds(j, W_IN_SHARD, stride=FLAT_ROWS), :].astype(BF16)
        gconv_ref[me] = conv_ref[...]
        _all_gather_blocks((gin_ref, gconv_ref), send_sems, recv_sems)

    vm = pl.BlockSpec(memory_space=pltpu.VMEM)
    return pl.pallas_call(
        body, name="gather_weights",
        out_shape=(jax.ShapeDtypeStruct((N_DEV, W_IN_SHARD, D_MODEL), BF16),
                   jax.ShapeDtypeStruct((N_DEV,) + conv_blk.shape, F32)),
        in_specs=[vm, vm], out_specs=(vm, vm),
        scratch_shapes=[pltpu.SemaphoreType.DMA((2, 7)), pltpu.SemaphoreType.DMA((2, 7))],
        compiler_params=_cp(),
    )(w_in_flat, conv_blk)


def _reduce_grads(big, pack_a, pack_b):
    nb = len(big)

    def body(*refs):
        srcs, (pa_ref, pb_ref), outs, (ga_ref, gb_ref) = (
            refs[:nb], refs[nb:nb + 2], refs[nb + 2:2 * nb + 2], refs[2 * nb + 2:2 * nb + 4])
        scr = refs[2 * nb + 4:]
        r1s, r2s, sbs, sts = (scr[k * nb:(k + 1) * nb] for k in range(4))
        s1_send, s1_recv, s2_send, s2_recv, ag_send, ag_recv, st_sem = scr[4 * nb:]
        x, y, c = _mesh_pos()
        me = (x, y, c)
        sibling = (x, y, 1 - c)
        rel = [(x, y), (1 - x, y), (x, 1 - y), (1 - x, 1 - y)]

        ga_ref[_dev_index(*me)] = pa_ref[...]
        gb_ref[_dev_index(*me)] = pb_ref[...]

        def p1(a, r, to):
            return pltpu.make_async_remote_copy(
                src_ref=srcs[a].at[_dev_index(*rel[r], 1 - c)], dst_ref=r1s[a].at[r],
                send_sem=s1_send.at[a, r], recv_sem=s1_recv.at[a, r], device_id=to, device_id_type=MESH)

        _, _, n1, n2, diag = _relay_order()
        order = ((n1, 1 + c), (diag, 3), (n2, 2 - c), ((x, y), 0))

        def p1_landed(a, slot):
            return pltpu.make_async_remote_copy(
                src_ref=r1s[a].at[slot], dst_ref=r1s[a].at[slot], send_sem=s1_send.at[a, slot],
                recv_sem=s1_recv.at[a, slot], device_id=me, device_id_type=MESH)

        def p2(a, k, to):
            return pltpu.make_async_remote_copy(
                src_ref=sbs[a].at[k], dst_ref=r2s[a].at[k],
                send_sem=s2_send.at[a, k], recv_sem=s2_recv.at[a, k], device_id=to, device_id_type=MESH)

        def stage(a, i):
            return pltpu.make_async_copy(srcs[a].at[_dev_index(*order[i][0], c)], sts[a].at[i % 2], st_sem.at[a, i % 2])

        sends = [p1(a, r, sibling) for a in range(nb) for r in range(4)]
        for cp in sends:
            cp.start()
        _all_gather_blocks((ga_ref, gb_ref), ag_send, ag_recv)
        for a in range(nb):
            stage(a, 0).start()
            for i, (_, slot) in enumerate(order):
                if i + 1 < len(order):
                    stage(a, i + 1).start()
                stage(a, i).wait()
                p1_landed(a, slot).wait_recv()
                chip_sum = r1s[a][slot] + sts[a][i % 2]
                if i < 2:
                    sbs[a][i] = chip_sum.astype(BF16)
                    sends.append(p2(a, i, (*n1, c)))
                    sends[-1].start()
                else:
                    r1s[a][slot] = chip_sum
        for a in range(nb):
            p2(a, 0, me).wait_recv()
            p2(a, 1, me).wait_recv()
            sbs[a][2] = (r1s[a][2 - c] + r2s[a][1].astype(F32)).astype(BF16)
            sends.append(p2(a, 2, (*n2, c)))
            sends[-1].start()
        for a in range(nb):
            p2(a, 2, me).wait_recv()
            outs[a][...] = (r1s[a][0] + r2s[a][0].astype(F32)) + r2s[a][2].astype(F32)
        for cp in sends:
            cp.wait_send()

    vm = pl.BlockSpec(memory_space=pltpu.VMEM)
    hbm = pl.BlockSpec(memory_space=pl.ANY)
    blk = [p.shape[1:] for p in big]
    scratch = ([pltpu.VMEM((4,) + b, F32) for b in blk] + [pltpu.VMEM((3,) + b, BF16) for b in blk]
               + [pltpu.VMEM((3,) + b, BF16) for b in blk] + [pltpu.VMEM((2,) + b, F32) for b in blk]
               + [pltpu.SemaphoreType.DMA((nb, 4)), pltpu.SemaphoreType.DMA((nb, 4)),
                  pltpu.SemaphoreType.DMA((nb, 3)), pltpu.SemaphoreType.DMA((nb, 3)),
                  pltpu.SemaphoreType.DMA((2, 7)), pltpu.SemaphoreType.DMA((2, 7)),
                  pltpu.SemaphoreType.DMA((nb, 2))])
    return pl.pallas_call(
        body, name="reduce_grads",
        out_shape=tuple(jax.ShapeDtypeStruct(b, F32) for b in blk)
        + (jax.ShapeDtypeStruct((N_DEV,) + pack_a.shape, F32), jax.ShapeDtypeStruct((N_DEV,) + pack_b.shape, F32)),
        in_specs=[hbm] * nb + [vm, vm], out_specs=(vm,) * (nb + 2),
        scratch_shapes=scratch,
        compiler_params=_cp(),
    )(*big, pack_a, pack_b)


def _rms_hat(xf):
    r = lax.rsqrt(jnp.mean(xf * xf, axis=-1, keepdims=True) + NORM_EPS)
    return xf * r, r


def _in_proj(x2, norm_w, g_in, tm):
    s = x2.shape[0]

    def body(x_ref, nw_ref, g_hbm, pm_ref, pb_ref, wt_hbm, g_vmem, wt_vmem, sem):
        @pl.when(pl.program_id(0) == 0)
        def _():
            cp = pltpu.make_async_copy(g_hbm, g_vmem, sem)
            cp.start()
            wt_vmem[D_MAIN:, :] = jnp.zeros((D_IN_PAD - D_MAIN, D_MODEL), BF16)
            cp.wait()
            for d in range(N_DEV):
                wt_vmem[W_IN_SHARD * d:W_IN_SHARD * (d + 1), :] = g_vmem[d]
            out = pltpu.make_async_copy(wt_vmem, wt_hbm, sem)
            out.start()
            out.wait()
        xhat, _ = _rms_hat(x_ref[...])
        n = (xhat * nw_ref[...]).astype(BF16)
        pm_ref[...] = _dot_nt_bf(n, wt_vmem[:D_MAIN, :])
        pb_ref[...] = _dot_nt_bf(n, wt_vmem[D_MAIN:, :])

    return pl.pallas_call(
        body, name="in_proj", grid=(s // tm,),
        out_shape=(jax.ShapeDtypeStruct((s, D_MAIN), F32), jax.ShapeDtypeStruct((s, 128), F32),
                   jax.ShapeDtypeStruct((D_IN_PAD, D_MODEL), BF16)),
        in_specs=[pl.BlockSpec((tm, D_MODEL), lambda i: (i, 0)),
                  pl.BlockSpec((1, D_MODEL), lambda i: (0, 0)),
                  pl.BlockSpec(memory_space=pl.ANY)],
        out_specs=(pl.BlockSpec((tm, D_MAIN), lambda i: (i, 0)), pl.BlockSpec((tm, 128), lambda i: (i, 0)),
                   pl.BlockSpec(memory_space=pl.ANY)),
        scratch_shapes=[pltpu.VMEM((N_DEV, W_IN_SHARD, D_MODEL), BF16), pltpu.VMEM((D_IN_PAD, D_MODEL), BF16),
                        pltpu.SemaphoreType.DMA],
        compiler_params=_cp(("arbitrary",)),
    )(x2, norm_w, g_in)


def _shift_down(cur, prev_tail, s):
    ext = jnp.concatenate([prev_tail, cur], axis=0)
    return pltpu.roll(ext, s, 0)[HALO:, :]


def _shift_up(cur, next_head, s):
    ext = jnp.concatenate([cur, next_head], axis=0)
    n = ext.shape[0]
    return pltpu.roll(ext, n - s, 0)[:cur.shape[0], :]


def _pool_counts(i, tp, w):
    t = i * tp + lax.broadcasted_iota(jnp.int32, (tp, 1), 0)
    return jnp.minimum(t + 1, w).astype(F32)


def _pool_mix(u, u_prev_tail, i, tp):
    ext = jnp.concatenate([u_prev_tail, u], axis=0)
    w2 = ext + pltpu.roll(ext, 1, 0)
    w4 = w2 + pltpu.roll(w2, 2, 0)
    w8 = w4 + pltpu.roll(w4, 4, 0)
    w16 = w8 + pltpu.roll(w8, 8, 0)
    mixes = []
    for gi, (w, win) in enumerate(zip(POOL_WINDOWS, (w2, w4, w8, w16))):
        cols = slice(gi * POOL_GROUP, (gi + 1) * POOL_GROUP)
        mixes.append(win[HALO:, cols] / _pool_counts(i, tp, w) - u[:, cols])
    return mixes


def _prev_halo_spec(tp, width, col):
    per = tp // HALO
    return pl.BlockSpec((HALO, width), lambda i: (jnp.maximum(i * per - 1, 0), col))


def _next_halo_spec(tp, width, col, n):
    per = tp // HALO
    return pl.BlockSpec((HALO, width), lambda i: (jnp.minimum((i + 1) * per, n * per - 1), col))


def _prev_tail(ref, i):
    return jnp.where(i > 0, ref[...], 0.0)


def _next_head(ref, i, n):
    return jnp.where(i < n - 1, ref[...], 0.0)


def _pool_fwd(proj_main, pool_w, pool_scale, tp):
    s = proj_main.shape[0]

    def body(u_ref, up_ref, z_ref, pw_ref, ps_ref, y_ref):
        i = pl.program_id(0)
        u = u_ref[...]
        mixes = _pool_mix(u, _prev_tail(up_ref, i), i, tp)
        gate = ps_ref[...] * _silu(z_ref[...])
        for gi in range(4):
            cols = slice(gi * POOL_GROUP, (gi + 1) * POOL_GROUP)
            y_ref[:, cols] = _dot_bf(mixes[gi], pw_ref[gi]) * gate[:, cols]

    return pl.pallas_call(
        body, name="pool_fwd", grid=(s // tp,),
        out_shape=jax.ShapeDtypeStruct((s, D_POOL), F32),
        in_specs=[pl.BlockSpec((tp, D_POOL), lambda i: (i, 0)),
                  _prev_halo_spec(tp, D_POOL, 0),
                  pl.BlockSpec((tp, D_POOL), lambda i: (i, 1)),
                  pl.BlockSpec((4, POOL_GROUP, POOL_GROUP), lambda i: (0, 0, 0)),
                  pl.BlockSpec((1, D_POOL), lambda i: (0, 0))],
        out_specs=pl.BlockSpec((tp, D_POOL), lambda i: (i, 0)),
        compiler_params=_cp(("parallel",)),
    )(proj_main, proj_main, proj_main, pool_w, pool_scale)


def _conv_taps(cur, prev_tail):
    ext = jnp.concatenate([prev_tail, cur], axis=0)
    return [cur] + [pltpu.roll(ext, sft, 0)[HALO:, :] for sft in range(1, CONV_WIDTH)]


def _conv_of_taps(taps, w4):
    y = taps[0] * w4[CONV_WIDTH - 1:CONV_WIDTH, :]
    for sft in range(1, CONV_WIDTH):
        y = y + taps[sft] * w4[CONV_WIDTH - 1 - sft:CONV_WIDTH - sft, :]
    return y


def _conv_fwd(cur, prev_tail, w4):
    ext = jnp.concatenate([prev_tail, cur], axis=0)
    y = ext * w4[CONV_WIDTH - 1:CONV_WIDTH, :]
    for sft in range(1, CONV_WIDTH):
        y = y + pltpu.roll(ext, sft, 0) * w4[CONV_WIDTH - 1 - sft:CONV_WIDTH - sft, :]
    return y[HALO:, :]


def _l2n_heads(t):
    parts = []
    for h in range(DN_HEADS):
        th = t[:, h * DN_HEAD_DIM:(h + 1) * DN_HEAD_DIM]
        parts.append(th * lax.rsqrt(jnp.sum(th * th, axis=-1, keepdims=True) + NORM_EPS))
    return jnp.concatenate(parts, axis=1)


def _post_conv(yq, yk, yv):
    return _l2n_heads(_silu(yq)), _l2n_heads(_silu(yk)), _silu(yv)


def _gates(ba, alog_lane, dtb_lane):
    lane = lax.broadcasted_iota(jnp.int32, ba.shape, 1)
    beta = jax.nn.sigmoid(ba)
    g = -jnp.exp(alog_lane) * _softplus(ba + dtb_lane)
    return jnp.where(lane < DN_HEADS, beta, jnp.where(lane < 2 * DN_HEADS, g, 0.0))


def _front(x2, norm_w, g_in, pool_w, pool_scale, conv_full, alog_lane, dtb_lane, w_out_blk, tm):
    s = x2.shape[0]

    def body(x_ref, nw_ref, pw_ref, ps_ref, cw_ref, al_ref, db_ref, wo_ref, g_hbm,
             pm_ref, pb_ref, yp_ref, qn_ref, kn_ref, vv_ref, gb_ref, gwo_hbm, wt_hbm,
             g_vmem, wt_vmem, tail_u, tail_qkv, gwo_ref, sem, wo_send, wo_recv):
        i = pl.program_id(0)

        @pl.when(i == 0)
        def _():
            gwo_ref[_dev_index(*_mesh_pos())] = wo_ref[...].astype(BF16)
            _direct_gather_start(gwo_ref, wo_send, wo_recv)
            cp = pltpu.make_async_copy(g_hbm, g_vmem, sem)
            cp.start()
            wt_vmem[D_MAIN:, :] = jnp.zeros((D_IN_PAD - D_MAIN, D_MODEL), BF16)
            tail_u[...] = jnp.zeros_like(tail_u)
            tail_qkv[...] = jnp.zeros_like(tail_qkv)
            cp.wait()
            for d in range(N_DEV):
                wt_vmem[W_IN_SHARD * d:W_IN_SHARD * (d + 1), :] = g_vmem[d]
            out = pltpu.make_async_copy(wt_vmem, wt_hbm, sem)
            out.start()
            out.wait()
        xhat, _ = _rms_hat(x_ref[...])
        n = (xhat * nw_ref[...]).astype(BF16)
        pm_ref[...] = _dot_nt_bf(n, wt_vmem[:D_MAIN, :])
        pb = _dot_nt_bf(n, wt_vmem[D_MAIN:, :])
        pb_ref[...] = pb
        u = pm_ref[:, :D_POOL]
        mixes = _pool_mix(u, tail_u[...], i, tm)
        tail_u[...] = u[tm - HALO:, :]
        gate = ps_ref[...] * _silu(pm_ref[:, D_POOL:2 * D_POOL])
        for gi in range(4):
            cols = slice(gi * POOL_GROUP, (gi + 1) * POOL_GROUP)
            yp_ref[:, cols] = _dot_bf(mixes[gi], pw_ref[gi]) * gate[:, cols]
        ys = []
        for c in range(3):
            cols = slice(c * D_DN, (c + 1) * D_DN)
            cur = pm_ref[:, 2 * D_POOL + c * D_DN:2 * D_POOL + (c + 1) * D_DN]
            ys.append(_conv_fwd(cur, tail_qkv[:, cols], cw_ref[:, cols]))
            tail_qkv[:, cols] = cur[tm - HALO:, :]
        qn, kn, vv = _post_conv(*ys)
        qn_ref[...] = qn
        kn_ref[...] = kn
        vv_ref[...] = vv
        gb_ref[...] = _gates(pb, al_ref[...], db_ref[...])

        @pl.when(i == s // tm - 1)
        def _():
            _direct_gather_wait(gwo_ref, wo_send, wo_recv)
            out = pltpu.make_async_copy(gwo_ref, gwo_hbm, sem)
            out.start()
            out.wait()

    tile = pl.BlockSpec((tm, D_DN), lambda i: (i, 0))
    lanes = pl.BlockSpec((tm, 128), lambda i: (i, 0))
    row = pl.BlockSpec((1, 128), lambda i: (0, 0))
    return pl.pallas_call(
        body, name="front", grid=(s // tm,),
        out_shape=(jax.ShapeDtypeStruct((s, D_MAIN), F32), jax.ShapeDtypeStruct((s, 128), F32),
                   jax.ShapeDtypeStruct((s, D_POOL), F32), jax.ShapeDtypeStruct((s, D_DN), F32),
                   jax.ShapeDtypeStruct((s, D_DN), F32), jax.ShapeDtypeStruct((s, D_DN), F32),
                   jax.ShapeDtypeStruct((s, 128), F32), jax.ShapeDtypeStruct((N_DEV,) + w_out_blk.shape, BF16),
                   jax.ShapeDtypeStruct((D_IN_PAD, D_MODEL), BF16)),
        in_specs=[pl.BlockSpec((tm, D_MODEL), lambda i: (i, 0)),
                  pl.BlockSpec((1, D_MODEL), lambda i: (0, 0)),
                  pl.BlockSpec((4, POOL_GROUP, POOL_GROUP), lambda i: (0, 0, 0)),
                  pl.BlockSpec((1, D_POOL), lambda i: (0, 0)),
                  pl.BlockSpec((CONV_WIDTH, 3 * D_DN), lambda i: (0, 0)), row, row,
                  pl.BlockSpec(memory_space=pltpu.VMEM), pl.BlockSpec(memory_space=pl.ANY)],
        out_specs=(pl.BlockSpec((tm, D_MAIN), lambda i: (i, 0)), lanes, tile, tile, tile, tile, lanes,
                   pl.BlockSpec(memory_space=pl.ANY), pl.BlockSpec(memory_space=pl.ANY)),
        scratch_shapes=[pltpu.VMEM((N_DEV, W_IN_SHARD, D_MODEL), BF16), pltpu.VMEM((D_IN_PAD, D_MODEL), BF16),
                        pltpu.VMEM((HALO, D_POOL), F32), pltpu.VMEM((HALO, 3 * D_DN), F32),
                        pltpu.VMEM((N_DEV,) + w_out_blk.shape, BF16),
                        pltpu.SemaphoreType.DMA, pltpu.SemaphoreType.DMA((7,)), pltpu.SemaphoreType.DMA((7,))],
        compiler_params=_cp(("arbitrary",)),
    )(x2, norm_w, pool_w, pool_scale, conv_full, alog_lane, dtb_lane, w_out_blk, g_in)


def _qkv_specs(tp, which):
    def spec(col, n=None):
        if which == 0:
            return pl.BlockSpec((tp, D_DN), lambda i: (i, col))
        if which < 0:
            return _prev_halo_spec(tp, D_DN, col)
        return _next_halo_spec(tp, D_DN, col, n)
    return spec


def _dn_pre(proj_main, proj_ba, conv_full, alog_lane, dtb_lane, tp):
    s = proj_main.shape[0]

    def body(q_ref, k_ref, v_ref, qp_ref, kp_ref, vp_ref, cw_ref, ba_ref, al_ref, db_ref,
             qn_ref, kn_ref, vv_ref, gb_ref):
        i = pl.program_id(0)
        ys = []
        for j, (cur, prev) in enumerate(((q_ref, qp_ref), (k_ref, kp_ref), (v_ref, vp_ref))):
            ys.append(_conv_fwd(cur[...], _prev_tail(prev, i), cw_ref[:, j * D_DN:(j + 1) * D_DN]))
        qn, kn, vv = _post_conv(*ys)
        qn_ref[...] = qn
        kn_ref[...] = kn
        vv_ref[...] = vv
        gb_ref[...] = _gates(ba_ref[...], al_ref[...], db_ref[...])

    cur, prev = _qkv_specs(tp, 0), _qkv_specs(tp, -1)
    row = pl.BlockSpec((1, 128), lambda i: (0, 0))
    tile = pl.BlockSpec((tp, D_DN), lambda i: (i, 0))
    return pl.pallas_call(
        body, name="dn_pre", grid=(s // tp,),
        out_shape=(jax.ShapeDtypeStruct((s, D_DN), F32),) * 3 + (jax.ShapeDtypeStruct((s, 128), F32),),
        in_specs=[cur(2), cur(3), cur(4), prev(2), prev(3), prev(4),
                  pl.BlockSpec((CONV_WIDTH, 3 * D_DN), lambda i: (0, 0)),
                  pl.BlockSpec((tp, 128), lambda i: (i, 0)), row, row],
        out_specs=(tile, tile, tile, pl.BlockSpec((tp, 128), lambda i: (i, 0))),
        compiler_params=_cp(("parallel",)),
    )(proj_main, proj_main, proj_main, proj_main, proj_main, proj_main, conv_full, proj_ba, alog_lane, dtb_lane)


def _dn_block(q, k, v, gcol, bcol, state, dz, nw, known=None):
    nb, n, d = q.shape
    ii = lax.broadcasted_iota(jnp.int32, (n, n), 0)
    jj = lax.broadcasted_iota(jnp.int32, (n, n), 1)
    lower = ii >= jj
    eye = (ii == jj).astype(F32)
    g_row = jnp.sum(eye * gcol, axis=1, keepdims=True)
    gc_col = jnp.sum(jnp.where(lower, g_row, 0.0), axis=2, keepdims=True)
    gc_row = jnp.sum(eye * gc_col, axis=1, keepdims=True)
    decay = jnp.where(lower, jnp.exp(jnp.where(lower, gc_col - gc_row, 0.0)), 0.0)
    kb = k * bcol
    vb = v * bcol
    qs = q * (DN_HEAD_DIM ** -0.5)
    egc = jnp.exp(gc_col)
    kq = jnp.concatenate([kb, qs], axis=1)
    vk = jnp.concatenate([vb, kb * egc], axis=2)
    if known is None:
        akq = _mm(kq, k, 1, 1, *_DN_PREC["akq"])
    else:
        akq = _mm_known(kq, k, known[0][:, :, :n].astype(F32), 1, 1, _DN_PREC["akq"][1])
    a = jnp.where(ii > jj, akq[:, :n] * decay, 0.0)
    qk = akq[:, n:] * decay
    if known is None:
        t = _tri_inv(a, *_DN_PREC["inv"])
        uw = _mm(t, vk, 1, 0, *_DN_PREC["uw"])
    else:
        t = _tri_inv_known(a, known[0][:, :n, n:].astype(F32), _DN_PREC["inv"][1])
        uw = _mm_known(t, vk, known[1], 1, 0, _DN_PREC["uw"][1])
    pre = jnp.concatenate([akq, jnp.concatenate([t, jnp.zeros_like(t)], axis=1)], axis=2)
    wq = jnp.concatenate([uw[:, :, d:], qs * egc], axis=1)
    g_last = gc_col[:, n - 1:n, :]
    k_dec = k * jnp.exp(g_last - gc_col)
    e_last = jnp.exp(g_last)
    os_, starts = [], []
    for c in range(nb // DN_HEADS):
        sl = slice(c * DN_HEADS, (c + 1) * DN_HEADS)
        if known is not None and c > 0:
            state = _use_known(state, known[2][sl])
        starts.append(state)
        ws = _mm(wq[sl], state, 1, 0, *_DN_PREC["ws"])
        v_new = uw[sl, :, :d] - ws[:, :n]
        os_.append(ws[:, n:] + _mm(qk[sl], v_new, 1, 0, *_DN_PREC["ov"]))
        state = state * e_last[sl] + _mm(k_dec[sl], v_new, 0, 0, *_DN_PREC["st"])
    o = jnp.concatenate(os_, axis=0)
    y = o * lax.rsqrt(jnp.mean(o * o, axis=-1, keepdims=True) + NORM_EPS) * nw * _silu(dz)
    return y, state, (pre, uw, jnp.concatenate(starts, axis=0))


def _dn_block_args(gc, q_ref, k_ref, v_ref, gb_ref, dz_ref):
    qs, ks, vs, gs, bs, zs = [], [], [], [], [], []
    for cc in range(gc):
        r = slice(cc * CHUNK, (cc + 1) * CHUNK)
        gbv = gb_ref[r, :]
        for h in range(DN_HEADS):
            cols = slice(h * DN_HEAD_DIM, (h + 1) * DN_HEAD_DIM)
            qs.append(q_ref[r, cols])
            ks.append(k_ref[r, cols])
            vs.append(v_ref[r, cols])
            zs.append(dz_ref[r, cols])
            gs.append(gbv[:, DN_HEADS + h:DN_HEADS + h + 1])
            bs.append(gbv[:, h:h + 1])
    return tuple(jnp.stack(t, axis=0) for t in (qs, ks, vs, gs, bs, zs))


def _dn_scan_fwd(qn, kn, vv, gb, proj_main, dn_norm_w, gc):
    s = qn.shape[0]
    nchunk = s // CHUNK
    rows = gc * CHUNK

    def body(q_ref, k_ref, v_ref, gb_ref, dz_ref, nw_ref, y_ref, ss_ref, pre_ref, uw_ref, state):
        @pl.when(pl.program_id(0) == 0)
        def _():
            state[...] = jnp.zeros_like(state)
        q, k, v, gcol, bcol, dz = _dn_block_args(gc, q_ref, k_ref, v_ref, gb_ref, dz_ref)
        y, new, (pre, uw, starts) = _dn_block(q, k, v, gcol, bcol, state[...], dz, nw_ref[...])
        state[...] = new
        ss_ref[...] = starts
        pre_ref[...] = pre
        uw_ref[...] = uw
        for cc in range(gc):
            for h in range(DN_HEADS):
                y_ref[cc * CHUNK:(cc + 1) * CHUNK, h * DN_HEAD_DIM:(h + 1) * DN_HEAD_DIM] = y[cc * DN_HEADS + h]

    tile = pl.BlockSpec((rows, D_DN), lambda i: (i, 0))
    return pl.pallas_call(
        body, name="dn_scan_fwd", grid=(nchunk // gc,),
        out_shape=(jax.ShapeDtypeStruct((s, D_DN), F32),
                   jax.ShapeDtypeStruct((nchunk * DN_HEADS, DN_HEAD_DIM, DN_HEAD_DIM), F32),
                   jax.ShapeDtypeStruct((nchunk * DN_HEADS, 2 * CHUNK, 2 * CHUNK), F32),
                   jax.ShapeDtypeStruct((nchunk * DN_HEADS, CHUNK, 2 * DN_HEAD_DIM), F32)),
        in_specs=[tile, tile, tile, pl.BlockSpec((rows, 128), lambda i: (i, 0)),
                  pl.BlockSpec((rows, D_DN), lambda i: (i, 5)), pl.BlockSpec((1, 128), lambda i: (0, 0))],
        out_specs=(tile, pl.BlockSpec((gc * DN_HEADS, DN_HEAD_DIM, DN_HEAD_DIM), lambda i: (i, 0, 0)),
                   pl.BlockSpec((gc * DN_HEADS, 2 * CHUNK, 2 * CHUNK), lambda i: (i, 0, 0)),
                   pl.BlockSpec((gc * DN_HEADS, CHUNK, 2 * DN_HEAD_DIM), lambda i: (i, 0, 0))),
        scratch_shapes=[pltpu.VMEM((DN_HEADS, DN_HEAD_DIM, DN_HEAD_DIM), F32)],
        compiler_params=_cp(("arbitrary",)),
    )(qn, kn, vv, gb, proj_main, dn_norm_w)


def _out_proj_loss(y_pool, y_dn, x2, tgt, w_out_full, fnw, tm):
    s = x2.shape[0]

    def body(yp_ref, yd_ref, x_ref, t_ref, wo_ref, fw_ref,
             dh_ref, dyp_ref, dyd_ref, gwo_ref, gfw_ref, loss_ref):
        @pl.when(pl.program_id(0) == 0)
        def _():
            gwo_ref[...] = jnp.zeros_like(gwo_ref)
            gfw_ref[...] = jnp.zeros_like(gfw_ref)
            loss_ref[...] = jnp.zeros_like(loss_ref)
        y = jnp.concatenate([yp_ref[...], yd_ref[...]], axis=1).astype(BF16)
        wo = wo_ref[...]
        h = x_ref[...] + jnp.dot(y, wo, preferred_element_type=F32)
        hn, r = _rms_hat(h)
        fw = fw_ref[...]
        err = hn * fw - t_ref[...]
        loss_ref[...] += 0.5 * jnp.sum(jnp.sum(err * err, axis=-1, keepdims=True) / D_MODEL, axis=0, keepdims=True)
        dout = err / D_MODEL
        gfw_ref[...] += jnp.sum(dout * hn, axis=0, keepdims=True)
        dhn = dout * fw
        dh = r * (dhn - hn * jnp.mean(dhn * hn, axis=-1, keepdims=True))
        dh_ref[...] = dh
        dhb = dh.astype(BF16)
        dy = _dot_nt_bf(dhb, wo)
        dyp_ref[...] = dy[:, :D_POOL]
        dyd_ref[...] = dy[:, D_POOL:]
        gwo_ref[...] += _dot_tn_bf(y, dhb)

    half = pl.BlockSpec((tm, D_POOL), lambda i: (i, 0))
    full = pl.BlockSpec((tm, D_MODEL), lambda i: (i, 0))
    return pl.pallas_call(
        body, name="out_proj_loss", grid=(s // tm,),
        out_shape=(jax.ShapeDtypeStruct((s, D_MODEL), F32), jax.ShapeDtypeStruct((s, D_POOL), F32),
                   jax.ShapeDtypeStruct((s, D_DN), F32), jax.ShapeDtypeStruct((D_MODEL, D_MODEL), F32),
                   jax.ShapeDtypeStruct((1, D_MODEL), F32), jax.ShapeDtypeStruct((1, 128), F32)),
        in_specs=[half, half, full, full, pl.BlockSpec((D_MODEL, D_MODEL), lambda i: (0, 0)),
                  pl.BlockSpec((1, D_MODEL), lambda i: (0, 0))],
        out_specs=(full, half, half, pl.BlockSpec((D_MODEL, D_MODEL), lambda i: (0, 0)),
                   pl.BlockSpec((1, D_MODEL), lambda i: (0, 0)), pl.BlockSpec((1, 128), lambda i: (0, 0))),
        compiler_params=_cp(("arbitrary",)),
    )(y_pool, y_dn, x2, tgt, w_out_full, fnw)


def _dn_scan_bwd(qn, kn, vv, gb, proj_main, dn_norm_w, states, pre, uw, dy_dn, p_out, gc):
    s = qn.shape[0]
    nchunk = s // CHUNK
    nstep = nchunk // gc
    rows = gc * CHUNK

    def body(q_ref, k_ref, v_ref, gb_ref, dz_ref, nw_ref, ss_ref, pre_ref, uw_ref, dy_ref, po_ref,
             dq_ref, dk_ref, dv_ref, dgb_ref, ddz_ref, dnw_ref, gro_ref, dstate, po_send, po_recv, rs_send, rs_recv):
        @pl.when(pl.program_id(0) == 0)
        def _():
            dstate[...] = jnp.zeros_like(dstate)
            dnw_ref[...] = jnp.zeros_like(dnw_ref)
            po_send[...] = po_ref[...].astype(BF16)
            _direct_scatter_start(po_send, po_recv, rs_send, rs_recv)

        @pl.when(pl.program_id(0) == nstep - 1)
        def _():
            _direct_scatter_wait(po_send, po_recv, rs_send, rs_recv)
            total = po_ref[_dev_index(*_mesh_pos())]
            for k in range(N_DEV - 1):
                total = total + po_recv[k].astype(F32)
            gro_ref[...] = total
        lane = lax.broadcasted_iota(jnp.int32, (CHUNK, 128), 1)
        q, k, v, gcol, bcol, dz = _dn_block_args(gc, q_ref, k_ref, v_ref, gb_ref, dz_ref)
        dy = jnp.stack([dy_ref[cc * CHUNK:(cc + 1) * CHUNK, h * DN_HEAD_DIM:(h + 1) * DN_HEAD_DIM]
                        for cc in range(gc) for h in range(DN_HEADS)], axis=0)
        known = (pre_ref[...], uw_ref[...], ss_ref[...])
        _, vjp = jax.vjp(lambda *a: _dn_block(*a, known=known)[:2], q, k, v, gcol, bcol, ss_ref[:DN_HEADS], dz,
                         nw_ref[...])
        dq, dk, dv, dg, db, dst, ddz, dnw = vjp((dy, dstate[...]))
        dstate[...] = dst
        dnw_ref[...] += dnw
        for cc in range(gc):
            r = slice(cc * CHUNK, (cc + 1) * CHUNK)
            dgb = jnp.zeros((CHUNK, 128), F32)
            for h in range(DN_HEADS):
                b = cc * DN_HEADS + h
                cols = slice(h * DN_HEAD_DIM, (h + 1) * DN_HEAD_DIM)
                for ref, val in zip((dq_ref, dk_ref, dv_ref, ddz_ref), (dq, dk, dv, ddz)):
                    ref[r, cols] = val[b]
                dgb = dgb + jnp.where(lane == h, db[b], 0.0) + jnp.where(lane == DN_HEADS + h, dg[b], 0.0)
            dgb_ref[r, :] = dgb

    rev = lambda i: (nstep - 1 - i, 0)
    tile = pl.BlockSpec((rows, D_DN), rev)
    lanes = pl.BlockSpec((rows, 128), rev)
    return pl.pallas_call(
        body, name="dn_scan_bwd", grid=(nstep,),
        out_shape=(jax.ShapeDtypeStruct((s, D_DN), F32),) * 3
        + (jax.ShapeDtypeStruct((s, 128), F32), jax.ShapeDtypeStruct((s, D_DN), F32),
           jax.ShapeDtypeStruct((1, 128), F32), jax.ShapeDtypeStruct(p_out.shape[1:], F32)),
        in_specs=[tile, tile, tile, lanes, pl.BlockSpec((rows, D_DN), lambda i: (nstep - 1 - i, 5)),
                  pl.BlockSpec((1, 128), lambda i: (0, 0)),
                  pl.BlockSpec((gc * DN_HEADS, DN_HEAD_DIM, DN_HEAD_DIM), lambda i: (nstep - 1 - i, 0, 0)),
                  pl.BlockSpec((gc * DN_HEADS, 2 * CHUNK, 2 * CHUNK), lambda i: (nstep - 1 - i, 0, 0)),
                  pl.BlockSpec((gc * DN_HEADS, CHUNK, 2 * DN_HEAD_DIM), lambda i: (nstep - 1 - i, 0, 0)), tile,
                  pl.BlockSpec(memory_space=pltpu.VMEM)],
        out_specs=(tile, tile, tile, lanes, tile, pl.BlockSpec((1, 128), lambda i: (0, 0)),
                   pl.BlockSpec(memory_space=pltpu.VMEM)),
        scratch_shapes=[pltpu.VMEM((DN_HEADS, DN_HEAD_DIM, DN_HEAD_DIM), F32),
                        pltpu.VMEM(p_out.shape, BF16), pltpu.VMEM((N_DEV - 1,) + p_out.shape[1:], BF16),
                        pltpu.SemaphoreType.DMA((7,)), pltpu.SemaphoreType.DMA((7,))],
        compiler_params=_cp(("arbitrary",)),
    )(qn, kn, vv, gb, proj_main, dn_norm_w, states, pre, uw, dy_dn, p_out)


def _dn_pre_bwd1(proj_main, proj_ba, conv_full, alog_lane, dtb_lane, dqn, dkn, dvv, dgb, tp):
    s = proj_main.shape[0]

    def body(q_ref, k_ref, v_ref, qp_ref, kp_ref, vp_ref, cw_ref, ba_ref, al_ref, db_ref,
             dqn_ref, dkn_ref, dvv_ref, dgb_ref,
             dcq_ref, dck_ref, dcv_ref, dba_ref, dcw_ref, dal_ref, ddb_ref):
        i = pl.program_id(0)

        @pl.when(i == 0)
        def _():
            dcw_ref[...] = jnp.zeros_like(dcw_ref)
            dal_ref[...] = jnp.zeros_like(dal_ref)
            ddb_ref[...] = jnp.zeros_like(ddb_ref)
        curs = (q_ref[...], k_ref[...], v_ref[...])
        tails = (_prev_tail(qp_ref, i), _prev_tail(kp_ref, i), _prev_tail(vp_ref, i))
        ys = [_conv_fwd(curs[j], tails[j], cw_ref[:, j * D_DN:(j + 1) * D_DN]) for j in range(3)]
        _, vjp = jax.vjp(_post_conv, *ys)
        dys = vjp((dqn_ref[...], dkn_ref[...], dvv_ref[...]))
        for j, (dy, out) in enumerate(zip(dys, (dcq_ref, dck_ref, dcv_ref))):
            out[...] = dy
            for sft in range(CONV_WIDTH):
                xs = curs[j] if sft == 0 else _shift_down(curs[j], tails[j], sft)
                row = CONV_WIDTH - 1 - sft
                dcw_ref[row:row + 1, j * D_DN:(j + 1) * D_DN] += jnp.sum(dy * xs, axis=0, keepdims=True)
        _, gvjp = jax.vjp(_gates, ba_ref[...], al_ref[...], db_ref[...])
        dba, dal, ddb = gvjp(dgb_ref[...])
        dba_ref[...] = dba
        dal_ref[...] += dal
        ddb_ref[...] += ddb

    cur, prev = _qkv_specs(tp, 0), _qkv_specs(tp, -1)
    row = pl.BlockSpec((1, 128), lambda i: (0, 0))
    tile = pl.BlockSpec((tp, D_DN), lambda i: (i, 0))
    lanes = pl.BlockSpec((tp, 128), lambda i: (i, 0))
    cw = pl.BlockSpec((CONV_WIDTH, 3 * D_DN), lambda i: (0, 0))
    return pl.pallas_call(
        body, name="dn_pre_bwd1", grid=(s // tp,),
        out_shape=(jax.ShapeDtypeStruct((s, D_DN), F32),) * 3
        + (jax.ShapeDtypeStruct((s, 128), F32), jax.ShapeDtypeStruct((CONV_WIDTH, 3 * D_DN), F32),
           jax.ShapeDtypeStruct((1, 128), F32), jax.ShapeDtypeStruct((1, 128), F32)),
        in_specs=[cur(2), cur(3), cur(4), prev(2), prev(3), prev(4), cw, lanes, row, row, tile, tile, tile, lanes],
        out_specs=(tile, tile, tile, lanes, cw, row, row),
        compiler_params=_cp(("arbitrary",)),
    )(proj_main, proj_main, proj_main, proj_main, proj_main, proj_main, conv_full, proj_ba, alog_lane, dtb_lane,
      dqn, dkn, dvv, dgb)


def _pool_bwd1(proj_main, pool_w, pool_scale, dyp, tp):
    s = proj_main.shape[0]

    def body(u_ref, up_ref, z_ref, pw_ref, ps_ref, dy_ref, dz_ref, dwin_ref, dpw_ref, dps_ref):
        i = pl.program_id(0)

        @pl.when(i == 0)
        def _():
            dpw_ref[...] = jnp.zeros_like(dpw_ref)
            dps_ref[...] = jnp.zeros_like(dps_ref)
        u = u_ref[...]
        z = z_ref[...]
        dy = dy_ref[...]
        ps = ps_ref[...]
        mixes = _pool_mix(u, _prev_tail(up_ref, i), i, tp)
        sg = jax.nn.sigmoid(z)
        sz = z * sg
        dsz = sg * (1.0 + z * (1.0 - sg))
        for gi, w in enumerate(POOL_WINDOWS):
            cols = slice(gi * POOL_GROUP, (gi + 1) * POOL_GROUP)
            mixw = _dot_bf(mixes[gi], pw_ref[gi])
            dmixw = dy[:, cols] * ps[:, cols] * sz[:, cols]
            dps_ref[:, cols] += jnp.sum(dy[:, cols] * mixw * sz[:, cols], axis=0, keepdims=True)
            dz_ref[:, cols] = dy[:, cols] * mixw * ps[:, cols] * dsz[:, cols]
            dpw_ref[gi] += _dot_tn_bf(mixes[gi], dmixw)
            dmix = _dot_nt_bf(dmixw, pw_ref[gi])
            dwin_ref[:, cols] = dmix / _pool_counts(i, tp, w)

    tile = pl.BlockSpec((tp, D_POOL), lambda i: (i, 0))
    pw = pl.BlockSpec((4, POOL_GROUP, POOL_GROUP), lambda i: (0, 0, 0))
    ps = pl.BlockSpec((1, D_POOL), lambda i: (0, 0))
    return pl.pallas_call(
        body, name="pool_bwd1", grid=(s // tp,),
        out_shape=(jax.ShapeDtypeStruct((s, D_POOL), F32), jax.ShapeDtypeStruct((s, D_POOL), F32),
                   jax.ShapeDtypeStruct((4, POOL_GROUP, POOL_GROUP), F32), jax.ShapeDtypeStruct((1, D_POOL), F32)),
        in_specs=[tile, _prev_halo_spec(tp, D_POOL, 0),
                  pl.BlockSpec((tp, D_POOL), lambda i: (i, 1)), pw, ps, tile],
        out_specs=(tile, tile, pw, ps),
        compiler_params=_cp(("arbitrary",)),
    )(proj_main, proj_main, proj_main, pool_w, pool_scale, dyp)


def _back(proj_main, proj_ba, dyp, dqn, dkn, dvv, dgb, ddz, x2, dh, norm_w, pool_w, pool_scale, conv_full,
          alog_lane, dtb_lane, wt_full, tm):
    s = x2.shape[0]
    nstep = s // tm
    per = tm // HALO

    def body(u_ref, z_ref, q_ref, k_ref, v_ref, up_ref, qp_ref, kp_ref, vp_ref, ba_ref,
             dyp_ref, dqn_ref, dkn_ref, dvv_ref, dgb_ref, ddz_ref, x_ref, dh_ref,
             nw_ref, pw_ref, ps_ref, cw_ref, al_ref, db_ref, wt_hbm,
             gx_ref, p_hbm, gnw_ref, dpw_ref, dps_ref, dcw_ref, dal_ref, ddb_ref,
             wt_vmem, acc, blk, head_dc, head_dw, sem, osem):
        j = pl.program_id(0)
        i = nstep - 1 - j

        @pl.when(j == 0)
        def _():
            cp = pltpu.make_async_copy(wt_hbm, wt_vmem, sem)
            cp.start()
            acc[...] = jnp.zeros_like(acc)
            for ref in (gnw_ref, dpw_ref, dps_ref, dcw_ref, dal_ref, ddb_ref, head_dc, head_dw):
                ref[...] = jnp.zeros_like(ref)
            cp.wait()

        u = u_ref[...]
        z = z_ref[...]
        dy = dyp_ref[...]
        ps = ps_ref[...]
        mixes = _pool_mix(u, _prev_tail(up_ref, i), i, tm)
        sg = jax.nn.sigmoid(z)
        sz = z * sg
        dsz = sg * (1.0 + z * (1.0 - sg))
        dzs, dwins = [], []
        for gi, w in enumerate(POOL_WINDOWS):
            cols = slice(gi * POOL_GROUP, (gi + 1) * POOL_GROUP)
            mixw = _dot_bf(mixes[gi], pw_ref[gi])
            dmixw = dy[:, cols] * ps[:, cols] * sz[:, cols]
            dps_ref[:, cols] += jnp.sum(dy[:, cols] * mixw * sz[:, cols], axis=0, keepdims=True)
            dzs.append(dy[:, cols] * mixw * ps[:, cols] * dsz[:, cols])
            dpw_ref[gi] += _dot_tn_bf(mixes[gi], dmixw)
            dwins.append(_dot_nt_bf(dmixw, pw_ref[gi]) / _pool_counts(i, tm, w))
        dzp = jnp.concatenate(dzs, axis=1)
        dw = jnp.concatenate(dwins, axis=1)
        ext = jnp.concatenate([dw, head_dw[...]], axis=0)
        m = ext.shape[0]
        a2 = ext + pltpu.roll(ext, m - 1, 0)
        a4 = a2 + pltpu.roll(a2, m - 2, 0)
        a8 = a4 + pltpu.roll(a4, m - 4, 0)
        a16 = a8 + pltpu.roll(a8, m - 8, 0)
        dup = jnp.concatenate(
            [acc_w[:tm, gi * POOL_GROUP:(gi + 1) * POOL_GROUP]
             - dw[:, gi * POOL_GROUP:(gi + 1) * POOL_GROUP] * _pool_counts(i, tm, w)
             for gi, (w, acc_w) in enumerate(zip(POOL_WINDOWS, (a2, a4, a8, a16)))], axis=1)
        head_dw[...] = dw[:HALO, :]

        curs = (q_ref[...], k_ref[...], v_ref[...])
        tails = (_prev_tail(qp_ref, i), _prev_tail(kp_ref, i), _prev_tail(vp_ref, i))
        taps = [_conv_taps(curs[c], tails[c]) for c in range(3)]
        ys = [_conv_of_taps(taps[c], cw_ref[:, c * D_DN:(c + 1) * D_DN]) for c in range(3)]
        _, vjp = jax.vjp(_post_conv, *ys)
        dys = vjp((dqn_ref[...], dkn_ref[...], dvv_ref[...]))
        dxs = []
        for c, dyc in enumerate(dys):
            cols = slice(c * D_DN, (c + 1) * D_DN)
            w4 = cw_ref[:, cols]
            for sft in range(CONV_WIDTH):
                row = CONV_WIDTH - 1 - sft
                dcw_ref[row:row + 1, cols] += jnp.sum(dyc * taps[c][sft], axis=0, keepdims=True)
            head = head_dc[:, cols]
            dx = dyc * w4[CONV_WIDTH - 1:CONV_WIDTH, :]
            for sft in range(1, CONV_WIDTH):
                dx = dx + _shift_up(dyc, head, sft) * w4[CONV_WIDTH - 1 - sft:CONV_WIDTH - sft, :]
            dxs.append(dx)
            head_dc[:, cols] = dyc[:HALO, :]
        _, gvjp = jax.vjp(_gates, ba_ref[...], al_ref[...], db_ref[...])
        dba, dal, ddb = gvjp(dgb_ref[...])
        dal_ref[...] += dal
        ddb_ref[...] += ddb

        dbab = dba.astype(BF16)
        xhat, r = _rms_hat(x_ref[...])
        nw = nw_ref[...]
        n = (xhat * nw).astype(BF16)
        acc[D_MAIN:, :] += _dot_tn_bf(dbab, n)
        dn = jnp.dot(dbab, wt_vmem[D_MAIN:, :], preferred_element_type=F32)
        for cb, d in enumerate((dup, dzp, dxs[0], dxs[1], dxs[2], ddz_ref[...])):
            rows = slice(cb * D_POOL, (cb + 1) * D_POOL)
            dpart = d.astype(BF16)
            acc[rows, :] += _dot_tn_bf(dpart, n)
            dn = dn + jnp.dot(dpart, wt_vmem[rows, :], preferred_element_type=F32)
        gnw_ref[...] += jnp.sum(dn * xhat, axis=0, keepdims=True)
        dxh = dn * nw
        gx_ref[...] = dh_ref[...] + r * (dxh - xhat * jnp.mean(dxh * xhat, axis=-1, keepdims=True))

        @pl.when(j == nstep - 1)
        def _():
            def out(d):
                return pltpu.make_async_copy(blk.at[d % 2], p_hbm.at[d], osem.at[d % 2])
            for d in range(N_DEV):
                if d >= 2:
                    out(d - 2).wait()
                blk[d % 2] = acc[W_IN_SHARD * d:W_IN_SHARD * (d + 1), :]
                out(d).start()
            out(N_DEV - 2).wait()
            out(N_DEV - 1).wait()

    def col(c):
        return pl.BlockSpec((tm, D_POOL), lambda j: (nstep - 1 - j, c))

    def halo(c):
        return pl.BlockSpec((HALO, D_POOL), lambda j: (jnp.maximum((nstep - 1 - j) * per - 1, 0), c))

    rev = lambda j: (nstep - 1 - j, 0)
    part = pl.BlockSpec((tm, D_POOL), rev)
    lanes = pl.BlockSpec((tm, 128), rev)
    full = pl.BlockSpec((tm, D_MODEL), rev)
    row = pl.BlockSpec((1, D_MODEL), lambda j: (0, 0))
    lrow = pl.BlockSpec((1, 128), lambda j: (0, 0))
    pw = pl.BlockSpec((4, POOL_GROUP, POOL_GROUP), lambda j: (0, 0, 0))
    psp = pl.BlockSpec((1, D_POOL), lambda j: (0, 0))
    cw = pl.BlockSpec((CONV_WIDTH, 3 * D_DN), lambda j: (0, 0))
    return pl.pallas_call(
        body, name="back", grid=(nstep,),
        out_shape=(jax.ShapeDtypeStruct((s, D_MODEL), F32),
                   jax.ShapeDtypeStruct((N_DEV, W_IN_SHARD, D_MODEL), F32), jax.ShapeDtypeStruct((1, D_MODEL), F32),
                   jax.ShapeDtypeStruct((4, POOL_GROUP, POOL_GROUP), F32), jax.ShapeDtypeStruct((1, D_POOL), F32),
                   jax.ShapeDtypeStruct((CONV_WIDTH, 3 * D_DN), F32),
                   jax.ShapeDtypeStruct((1, 128), F32), jax.ShapeDtypeStruct((1, 128), F32)),
        in_specs=[col(0), col(1), col(2), col(3), col(4), halo(0), halo(2), halo(3), halo(4), lanes,
                  part, part, part, part, lanes, part, full, full,
                  row, pw, psp, cw, lrow, lrow, pl.BlockSpec(memory_space=pl.ANY)],
        out_specs=(full, pl.BlockSpec(memory_space=pl.ANY), row, pw, psp, cw, lrow, lrow),
        scratch_shapes=[pltpu.VMEM((D_IN_PAD, D_MODEL), BF16), pltpu.VMEM((D_IN_PAD, D_MODEL), F32),
                        pltpu.VMEM((2, W_IN_SHARD, D_MODEL), F32),
                        pltpu.VMEM((HALO, 3 * D_DN), F32), pltpu.VMEM((HALO, D_POOL), F32),
                        pltpu.SemaphoreType.DMA, pltpu.SemaphoreType.DMA((2,))],
        compiler_params=_cp(("arbitrary",)),
    )(proj_main, proj_main, proj_main, proj_main, proj_main, proj_main, proj_main, proj_main, proj_main, proj_ba,
      dyp, dqn, dkn, dvv, dgb, ddz, x2, dh, norm_w, pool_w, pool_scale, conv_full, alog_lane, dtb_lane, wt_full)


def _in_proj_bwd(dwin, dzp, dcq, dck, dcv, ddz, dba, x2, dh, norm_w, wt_full, conv_full, tm):
    s = x2.shape[0]
    nstep = s // tm

    def body(dwin_ref, dzp_ref, dcq_ref, dck_ref, dcv_ref, ddz_ref, dba_ref, x_ref, dh_ref, nw_ref, cw_ref, wt_hbm,
             gx_ref, p_hbm, gnw_ref, wt_vmem, acc, blk, head_dc, head_dw, sem, osem):
        j = pl.program_id(0)
        i = nstep - 1 - j

        @pl.when(j == 0)
        def _():
            cp = pltpu.make_async_copy(wt_hbm, wt_vmem, sem)
            cp.start()
            acc[...] = jnp.zeros_like(acc)
            gnw_ref[...] = jnp.zeros_like(gnw_ref)
            head_dc[...] = jnp.zeros_like(head_dc)
            head_dw[...] = jnp.zeros_like(head_dw)
            cp.wait()
        dw = dwin_ref[...]
        ext = jnp.concatenate([dw, head_dw[...]], axis=0)
        m = ext.shape[0]
        a2 = ext + pltpu.roll(ext, m - 1, 0)
        a4 = a2 + pltpu.roll(a2, m - 2, 0)
        a8 = a4 + pltpu.roll(a4, m - 4, 0)
        a16 = a8 + pltpu.roll(a8, m - 8, 0)
        dup = jnp.concatenate(
            [acc_w[:tm, gi * POOL_GROUP:(gi + 1) * POOL_GROUP]
             - dw[:, gi * POOL_GROUP:(gi + 1) * POOL_GROUP] * _pool_counts(i, tm, w)
             for gi, (w, acc_w) in enumerate(zip(POOL_WINDOWS, (a2, a4, a8, a16)))], axis=1)
        head_dw[...] = dw[:HALO, :]
        dxs = []
        for c, ref in enumerate((dcq_ref, dck_ref, dcv_ref)):
            cols = slice(c * D_DN, (c + 1) * D_DN)
            w4 = cw_ref[:, cols]
            dy = ref[...]
            head = head_dc[:, cols]
            dx = dy * w4[CONV_WIDTH - 1:CONV_WIDTH, :]
            for sft in range(1, CONV_WIDTH):
                dx = dx + _shift_up(dy, head, sft) * w4[CONV_WIDTH - 1 - sft:CONV_WIDTH - sft, :]
            dxs.append(dx)
            head_dc[:, cols] = dy[:HALO, :]
        dbab = dba_ref[...].astype(BF16)
        xhat, r = _rms_hat(x_ref[...])
        nw = nw_ref[...]
        n = (xhat * nw).astype(BF16)
        acc[D_MAIN:, :] += _dot_tn_bf(dbab, n)
        dn = jnp.dot(dbab, wt_vmem[D_MAIN:, :], preferred_element_type=F32)
        for cb, d in enumerate((dup, dzp_ref[...], dxs[0], dxs[1], dxs[2], ddz_ref[...])):
            rows = slice(cb * D_POOL, (cb + 1) * D_POOL)
            dpart = d.astype(BF16)
            acc[rows, :] += _dot_tn_bf(dpart, n)
            dn = dn + jnp.dot(dpart, wt_vmem[rows, :], preferred_element_type=F32)
        gnw_ref[...] += jnp.sum(dn * xhat, axis=0, keepdims=True)
        dxh = dn * nw
        gx_ref[...] = dh_ref[...] + r * (dxh - xhat * jnp.mean(dxh * xhat, axis=-1, keepdims=True))

        @pl.when(j == nstep - 1)
        def _():
            def out(d):
                return pltpu.make_async_copy(blk.at[d % 2], p_hbm.at[d], osem.at[d % 2])
            for d in range(N_DEV):
                if d >= 2:
                    out(d - 2).wait()
                blk[d % 2] = acc[W_IN_SHARD * d:W_IN_SHARD * (d + 1), :]
                out(d).start()
            out(N_DEV - 2).wait()
            out(N_DEV - 1).wait()

    rev = lambda j: (nstep - 1 - j, 0)
    part = pl.BlockSpec((tm, D_POOL), rev)
    full = pl.BlockSpec((tm, D_MODEL), rev)
    row = pl.BlockSpec((1, D_MODEL), lambda j: (0, 0))
    return pl.pallas_call(
        body, name="in_proj_bwd", grid=(nstep,),
        out_shape=(jax.ShapeDtypeStruct((s, D_MODEL), F32),
                   jax.ShapeDtypeStruct((N_DEV, W_IN_SHARD, D_MODEL), F32), jax.ShapeDtypeStruct((1, D_MODEL), F32)),
        in_specs=[part] * 6 + [pl.BlockSpec((tm, 128), rev), full, full, row,
                               pl.BlockSpec((CONV_WIDTH, 3 * D_DN), lambda j: (0, 0)),
                               pl.BlockSpec(memory_space=pl.ANY)],
        out_specs=(full, pl.BlockSpec(memory_space=pl.ANY), row),
        scratch_shapes=[pltpu.VMEM((D_IN_PAD, D_MODEL), BF16), pltpu.VMEM((D_IN_PAD, D_MODEL), F32),
                        pltpu.VMEM((2, W_IN_SHARD, D_MODEL), F32),
                        pltpu.VMEM((HALO, 3 * D_DN), F32), pltpu.VMEM((HALO, D_POOL), F32),
                        pltpu.SemaphoreType.DMA, pltpu.SemaphoreType.DMA((2,))],
        compiler_params=_cp(("arbitrary",)),
    )(dwin, dzp, dcq, dck, dcv, ddz, dba, x2, dh, norm_w, conv_full, wt_full)


def _adamw_math(w, g, m, v):
    m = ADAM_B1 * m + (1.0 - ADAM_B1) * g
    v = ADAM_B2 * v + (1.0 - ADAM_B2) * (g * g)
    m_hat = m / (1.0 - ADAM_B1 ** ADAM_STEP)
    v_hat = v / (1.0 - ADAM_B2 ** ADAM_STEP)
    delta = -ADAM_LR * (m_hat / (jnp.sqrt(v_hat) + ADAM_EPS) + ADAM_WD * w)
    return delta, m, v


def _adamw_sharded(params):
    k = len(params)

    def body(*refs):
        ins, outs = refs[:4 * k], refs[4 * k:]
        for p in range(k):
            w_ref, g_ref, m_ref, v_ref = ins[4 * p:4 * p + 4]
            go_ref = outs[4 * p]
            if g_ref.shape == w_ref.shape:
                go_ref[...] = g_ref[...]
            else:
                for j in range(FLAT_ROWS):
                    go_ref[pl.ds(j, W_IN_SHARD, stride=FLAT_ROWS), :] = g_ref[:, 128 * j:128 * (j + 1)]
            d, nm, nv = _adamw_math(w_ref[...], go_ref[...], m_ref[...], v_ref[...])
            outs[4 * p + 1][...] = d
            outs[4 * p + 2][...] = nm
            outs[4 * p + 3][...] = nv

    flat = [a for p in params for a in p]
    out_shape = tuple(jax.ShapeDtypeStruct(p[0].shape, F32) for p in params for _ in range(4))
    res = pl.pallas_call(body, name="adamw_sharded", out_shape=out_shape, compiler_params=_cp())(*flat)
    return [tuple(res[4 * p:4 * p + 4]) for p in range(k)]


def _adamw_replicated(gath_a, gath_b, pool, rows):
    nrow = len(rows)

    def body(*refs):
        ga_ref, gb_ref = refs[:2]
        ins = refs[2:2 + 3 * (nrow + 1)]
        outs = refs[2 + 3 * (nrow + 1):]

        def total(ref):
            g = ref[0]
            for d in range(1, N_DEV):
                g = g + ref[d]
            return g

        def update(g, wmv, o):
            w, m, v = (r[...] for r in wmv)
            dl, nm, nv = _adamw_math(w, g, m, v)
            o[0][...] = g
            o[1][...] = dl
            o[2][...] = nm
            o[3][...] = nv

        update(total(ga_ref), ins[:3], outs[:4])
        gb = total(gb_ref)
        for r in range(nrow):
            n = ins[3 * (r + 1)].shape[1]
            update(gb[r:r + 1, :n], ins[3 * (r + 1):3 * (r + 2)], outs[4 * (r + 1):4 * (r + 2)])
        outs[4 * (nrow + 1)][...] = gb[nrow:nrow + 1, 0:1]

    flat = list(pool) + [a for wmv in rows for a in wmv]
    out_shape = ((jax.ShapeDtypeStruct(pool[0].shape, F32),) * 4
                 + tuple(jax.ShapeDtypeStruct(wmv[0].shape, F32) for wmv in rows for _ in range(4))
                 + (jax.ShapeDtypeStruct((1, 1), F32),))
    res = pl.pallas_call(body, name="adamw_replicated", out_shape=out_shape, compiler_params=_cp())(
        gath_a, gath_b, *flat)
    return [res[4 * k:4 * k + 4] for k in range(nrow + 1)], res[-1]


_ROW_ORDER = ("norm_w", "final_norm_w", "pool_scale", "dn_norm_w", "a_log", "dt_bias")


def _pack_rows(vectors):
    out = [jnp.pad(v.reshape(-1), (0, D_MODEL - v.size)) for v in vectors]
    out += [jnp.zeros((D_MODEL,), F32)] * (8 - len(out))
    return jnp.stack(out, axis=0)


def _lane_row(vec4, start):
    return jnp.pad(vec4.reshape(-1), (start, 128 - start - vec4.size)).reshape(1, 128)


def kernel(x, norm_w, w_in, pool_w, pool_scale, conv_w, a_log, dt_bias, dn_norm_w, w_out, final_norm_w, loss_target, m_norm_w, m_w_in, m_pool_w, m_pool_scale, m_conv_w, m_a_log, m_dt_bias, m_dn_norm_w, m_w_out, m_final_norm_w, v_norm_w, v_w_in, v_pool_w, v_pool_scale, v_conv_w, v_a_log, v_dt_bias, v_dn_norm_w, v_w_out, v_final_norm_w):
    s = x.shape[1]
    tm = min(512, s)
    tmb = min(256, s)
    tp = min(512, s)
    x2 = x[0]
    tgt = loss_target[0]
    def to_flat(a):
        return a[0].reshape(FLAT_ROWS, 128, W_IN_SHARD).transpose(2, 0, 1).reshape(W_IN_SHARD * FLAT_ROWS, 128)

    def from_flat(f):
        return f.reshape(W_IN_SHARD, FLAT_ROWS, 128).transpose(1, 2, 0).reshape(1, D_MODEL, W_IN_SHARD)

    wf, m_wf, v_wf = to_flat(w_in), to_flat(m_w_in), to_flat(v_w_in)

    g_in, g_conv = _gather_weights(wf, conv_w[0])
    conv_full = g_conv.transpose(1, 0, 2).reshape(CONV_WIDTH, 3 * D_DN)
    alog_lane = _lane_row(a_log, DN_HEADS)
    dtb_lane = _lane_row(dt_bias, DN_HEADS)
    fnw = final_norm_w.reshape(1, D_MODEL)

    proj_main, proj_ba, y_pool, qn, kn, vv, gb, g_out, wt_full = _front(
        x2, norm_w, g_in, pool_w[0], pool_scale, conv_full, alog_lane, dtb_lane, w_out[0], tm)
    w_out_full = g_out.reshape(D_MODEL, D_MODEL)
    y_dn, states, dn_pre, dn_uw = _dn_scan_fwd(qn, kn, vv, gb, proj_main, dn_norm_w, DN_CHUNKS_PER_STEP)

    dh, dyp, dyd, g_wout, g_fnw, loss_part = _out_proj_loss(y_pool, y_dn, x2, tgt, w_out_full, fnw, tm)
    p_out = g_wout.reshape(N_DEV, D_MODEL // N_DEV, D_MODEL)
    dqn, dkn, dvv, dgb, ddz, g_dnw, gr_out = _dn_scan_bwd(qn, kn, vv, gb, proj_main, dn_norm_w, states, dn_pre, dn_uw,
                                                          dyd, p_out, DN_CHUNKS_PER_STEP)
    grad_x2, p_in, g_nw, g_pw, g_ps, g_conv_full, g_al, g_db = _back(
        proj_main, proj_ba, dyp, dqn, dkn, dvv, dgb, ddz, x2, dh, norm_w, pool_w[0], pool_scale, conv_full,
        alog_lane, dtb_lane, wt_full, tmb)

    p_conv = g_conv_full.reshape(CONV_WIDTH, N_DEV, 3 * D_DN // N_DEV).transpose(1, 0, 2)
    pack_a = g_pw.reshape(4 * POOL_GROUP, POOL_GROUP)
    pack_b = _pack_rows([g_nw, g_fnw, g_ps, g_dnw, g_al[0, DN_HEADS:2 * DN_HEADS], g_db[0, DN_HEADS:2 * DN_HEADS],
                         loss_part[0, :1]])
    gr_in, gr_conv, gath_a, gath_b = _reduce_grads((p_in, p_conv), pack_a, pack_b)

    r_in, r_out, r_conv = _adamw_sharded([(wf, gr_in, m_wf, v_wf), (w_out[0], gr_out, m_w_out[0], v_w_out[0]),
                                          (conv_w[0], gr_conv, m_conv_w[0], v_conv_w[0])])
    flat = lambda a: a.reshape(4 * POOL_GROUP, POOL_GROUP)
    row = lambda a: a.reshape(1, -1)
    vecs = {"norm_w": (norm_w, m_norm_w, v_norm_w), "final_norm_w": (final_norm_w, m_final_norm_w, v_final_norm_w),
            "pool_scale": (pool_scale, m_pool_scale, v_pool_scale), "dn_norm_w": (dn_norm_w, m_dn_norm_w, v_dn_norm_w),
            "a_log": (a_log, m_a_log, v_a_log), "dt_bias": (dt_bias, m_dt_bias, v_dt_bias)}
    res, loss = _adamw_replicated(gath_a, gath_b, (flat(pool_w), flat(m_pool_w), flat(v_pool_w)),
                                  [tuple(row(a) for a in vecs[nm]) for nm in _ROW_ORDER])
    r_pool = res[0]
    r_vec = dict(zip(_ROW_ORDER, res[1:]))

    def group(k):
        vec = lambda nm: r_vec[nm][k].reshape(vecs[nm][0].shape)
        return (vec("norm_w"), from_flat(r_in[k]), r_pool[k].reshape(pool_w.shape), vec("pool_scale"), r_conv[k][None],
                vec("a_log"), vec("dt_bias"), vec("dn_norm_w"), r_out[k][None], vec("final_norm_w"))

    return (loss[0, 0], grad_x2[None], *group(0), *group(1), *group(2), *group(3))
```

```python
import functools

import jax
import jax.numpy as jnp
from jax import lax
from jax.experimental import pallas as pl
from jax.experimental.pallas import tpu as pltpu

F32 = jnp.float32
BF16 = jnp.bfloat16
HI = lax.Precision.HIGHEST
MESH = pl.DeviceIdType.MESH

D_MODEL = 1024
D_POOL = 512
D_DN = 512
POOL_WINDOWS = (2, 4, 8, 16)
POOL_GROUP = 128
DN_HEADS = 4
DN_HEAD_DIM = 128
CONV_WIDTH = 4
CHUNK = 64
NORM_EPS = 1e-6
D_IN = 3080
D_MAIN = 3072
FLAT_ROWS = D_MODEL // 128
D_IN_PAD = D_MAIN + 128
N_DEV = 8
W_IN_SHARD = D_IN // N_DEV
HALO = 16
DN_CHUNKS_PER_STEP = 8

ADAM_LR = 0.001
ADAM_B1 = 0.9
ADAM_B2 = 0.999
ADAM_EPS = 1e-08
ADAM_WD = 0.01
ADAM_STEP = 10

VMEM_LIMIT = 56 * 1024 * 1024

def _cp(sem=None, vmem=VMEM_LIMIT):
    kw = {"vmem_limit_bytes": vmem}
    if sem is not None:
        kw["dimension_semantics"] = sem
    return pltpu.CompilerParams(**kw)


def _dot_bf(a, b):
    return jnp.dot(a.astype(BF16), b.astype(BF16), preferred_element_type=F32)


def _dot_nt_bf(a, b):
    return lax.dot_general(a.astype(BF16), b.astype(BF16), (((1,), (1,)), ((), ())), preferred_element_type=F32)


def _dot_tn_bf(a, b):
    return lax.dot_general(a.astype(BF16), b.astype(BF16), (((0,), (0,)), ((), ())), preferred_element_type=F32)


def _mm_raw(a, b, ca, cb, prec):
    off = a.ndim - 2
    dn = (((ca + off,), (cb + off,)), ((0,), (0,)) if off else ((), ()))
    if prec == "hi":
        return lax.dot_general(a, b, dn, precision=HI, preferred_element_type=F32)
    ah, bh = a.astype(BF16), b.astype(BF16)
    out = lax.dot_general(ah, bh, dn, preferred_element_type=F32)
    if prec == "x3":
        al = (a - ah.astype(F32)).astype(BF16)
        bl = (b - bh.astype(F32)).astype(BF16)
        out = out + lax.dot_general(ah, bl, dn, preferred_element_type=F32)
        out = out + lax.dot_general(al, bh, dn, preferred_element_type=F32)
    return out


@functools.partial(jax.custom_vjp, nondiff_argnums=(2, 3, 4, 5))
def _mm(a, b, ca, cb, prec, bprec):
    return _mm_raw(a, b, ca, cb, prec)


def _mm_fwd(a, b, ca, cb, prec, bprec):
    return _mm_raw(a, b, ca, cb, prec), (a, b)


def _mm_bwd(ca, cb, prec, bprec, res, dc):
    a, b = res
    da = _mm_raw(dc, b, 1, 1 - cb, bprec) if ca == 1 else _mm_raw(b, dc, 1 - cb, 1, bprec)
    db = _mm_raw(a, dc, 1 - ca, 0, bprec) if cb == 0 else _mm_raw(dc, a, 0, 1 - ca, bprec)
    return da, db


_mm.defvjp(_mm_fwd, _mm_bwd)


@functools.partial(jax.custom_vjp, nondiff_argnums=(1, 2))
def _tri_inv(a, prec, bprec):
    n = a.shape[-1]
    ii = lax.broadcasted_iota(jnp.int32, (n, n), 0)
    jj = lax.broadcasted_iota(jnp.int32, (n, n), 1)
    p = (ii == jj).astype(F32) - a
    b = _mm_raw(a, a, 1, 0, prec)
    for _ in range(4):
        pb = _mm_raw(jnp.concatenate([p, b], axis=-2), b, 1, 0, prec)
        p = p + pb[..., :n, :]
        b = pb[..., n:, :]
    return p + _mm_raw(p, b, 1, 0, prec)


def _tri_inv_fwd(a, prec, bprec):
    t = _tri_inv(a, prec, bprec)
    return t, t


def _tri_inv_bwd(prec, bprec, t, dt):
    return (-_mm_raw(_mm_raw(t, dt, 0, 0, bprec), t, 1, 1, bprec),)


_tri_inv.defvjp(_tri_inv_fwd, _tri_inv_bwd)

@functools.partial(jax.custom_vjp, nondiff_argnums=(3, 4, 5))
def _mm_known(a, b, out, ca, cb, bprec):
    return out


def _mm_known_fwd(a, b, out, ca, cb, bprec):
    return out, (a, b)


def _mm_known_bwd(ca, cb, bprec, res, dc):
    return _mm_bwd(ca, cb, None, bprec, res, dc) + (jnp.zeros_like(dc),)


_mm_known.defvjp(_mm_known_fwd, _mm_known_bwd)


@jax.custom_vjp
def _use_known(x, known):
    return known


_use_known.defvjp(lambda x, known: (known, None), lambda _, g: (g, jnp.zeros_like(g)))


@functools.partial(jax.custom_vjp, nondiff_argnums=(2,))
def _tri_inv_known(a, t, bprec):
    return t


def _tri_inv_known_fwd(a, t, bprec):
    return t, t


def _tri_inv_known_bwd(bprec, t, dt):
    return _tri_inv_bwd(None, bprec, t, dt) + (jnp.zeros_like(dt),)


_tri_inv_known.defvjp(_tri_inv_known_fwd, _tri_inv_known_bwd)

_DN_PREC = {"akq": ("bf16", "bf16"), "inv": ("bf16", "bf16"), "uw": ("bf16", "bf16"), "ws": ("bf16", "bf16"),
            "ov": ("bf16", "bf16"), "st": ("bf16", "bf16")}


def _silu(x):
    return x * jax.nn.sigmoid(x)


def _softplus(x):
    pos = x > 0.0
    return jnp.where(pos, x, 0.0) + jnp.log1p(jnp.exp(jnp.where(pos, -x, x)))


def _mesh_pos():
    return lax.axis_index("x"), lax.axis_index("y"), lax.axis_index("c")


def _dev_index(x, y, c):
    return 4 * x + 2 * y + c


def _relay_order():
    x, y, c = _mesh_pos()
    n1 = (x + (1 - c) * (1 - 2 * x), y + c * (1 - 2 * y))
    n2 = (x + c * (1 - 2 * x), y + (1 - c) * (1 - 2 * y))
    return (x, y, c), (x, y, 1 - c), n1, n2, (1 - x, 1 - y)


def _all_gather_blocks(outs, send_sems, recv_sems):
    me, sibling, n1, n2, diag = _relay_order()
    c = me[2]

    def copy(a, k, block, to):
        rows = outs[a].at[_dev_index(*block)]
        return pltpu.make_async_remote_copy(src_ref=rows, dst_ref=rows, send_sem=send_sems.at[a, k],
                                            recv_sem=recv_sems.at[a, k], device_id=to, device_id_type=MESH)

    n = len(outs)
    started = []

    def start(cp):
        cp.start()
        started.append(cp)

    for a in range(n):
        start(copy(a, 1, me, (*n1, c)))
        start(copy(a, 2, me, (*n2, c)))
        start(copy(a, 0, me, sibling))
    for a in range(n):
        copy(a, 1, (*n1, c), me).wait_recv()
        start(copy(a, 3, (*n1, c), (*n2, c)))
        start(copy(a, 4, (*n1, c), sibling))
    for a in range(n):
        copy(a, 2, (*n2, c), me).wait_recv()
        start(copy(a, 5, (*n2, c), sibling))
        copy(a, 3, (*diag, c), me).wait_recv()
        start(copy(a, 6, (*diag, c), sibling))
    for a in range(n):
        copy(a, 0, sibling, me).wait_recv()
        copy(a, 4, (*n2, 1 - c), me).wait_recv()
        copy(a, 5, (*n1, 1 - c), me).wait_recv()
        copy(a, 6, (*diag, 1 - c), me).wait_recv()
    for cp in started:
        cp.wait_send()


def _peer_relations():
    x, y, c = _mesh_pos()
    flips = [(fx, fy, fc) for fx in (0, 1) for fy in (0, 1) for fc in (0, 1)][1:]
    peers = [(1 - x if fx else x, 1 - y if fy else y, 1 - c if fc else c) for fx, fy, fc in flips]
    return (x, y, c), peers


def _direct_gather_start(out_ref, send_sems, recv_sems):
    me, peers = _peer_relations()
    rows = out_ref.at[_dev_index(*me)]
    for k, peer in enumerate(peers):
        pltpu.make_async_remote_copy(src_ref=rows, dst_ref=rows, send_sem=send_sems.at[k], recv_sem=recv_sems.at[k],
                                     device_id=peer, device_id_type=MESH).start()


def _direct_gather_wait(out_ref, send_sems, recv_sems):
    me, peers = _peer_relations()
    for k, peer in enumerate(peers):
        rows = out_ref.at[_dev_index(*peer)]
        cp = pltpu.make_async_remote_copy(src_ref=rows, dst_ref=rows, send_sem=send_sems.at[k],
                                          recv_sem=recv_sems.at[k], device_id=peer, device_id_type=MESH)
        cp.wait_recv()
        cp.wait_send()


def _direct_scatter_start(send_ref, recv_ref, send_sems, recv_sems):
    me, peers = _peer_relations()
    for k, peer in enumerate(peers):
        pltpu.make_async_remote_copy(src_ref=send_ref.at[_dev_index(*peer)], dst_ref=recv_ref.at[k],
                                     send_sem=send_sems.at[k], recv_sem=recv_sems.at[k],
                                     device_id=peer, device_id_type=MESH).start()


def _direct_scatter_wait(send_ref, recv_ref, send_sems, recv_sems):
    me, peers = _peer_relations()
    for k, peer in enumerate(peers):
        cp = pltpu.make_async_remote_copy(src_ref=send_ref.at[_dev_index(*peer)], dst_ref=recv_ref.at[k],
                                          send_sem=send_sems.at[k], recv_sem=recv_sems.at[k],
                                          device_id=peer, device_id_type=MESH)
        cp.wait_recv()
        cp.wait_send()


def _gather_weights(w_in_flat, conv_blk):
    def body(win_ref, conv_ref, gin_ref, gconv_ref, send_sems, recv_sems):
        x, y, c = _mesh_pos()
        me = _dev_index(x, y, c)
        for j in range(FLAT_ROWS):
            gin_ref[me, :, 128 * j:128 * (j + 1)] = win_ref[pl.ds(j, W_IN_SHARD, stride=FLAT_ROWS), :].astype(BF16)
        gconv_ref[me] = conv_ref[...]
        _all_gather_blocks((gin_ref, gconv_ref), send_sems, recv_sems)

    vm = pl.BlockSpec(memory_space=pltpu.VMEM)
    return pl.pallas_call(
        body, name="gather_weights",
        out_shape=(jax.ShapeDtypeStruct((N_DEV, W_IN_SHARD, D_MODEL), BF16),
                   jax.ShapeDtypeStruct((N_DEV,) + conv_blk.shape, F32)),
        in_specs=[vm, vm], out_specs=(vm, vm),
        scratch_shapes=[pltpu.SemaphoreType.DMA((2, 7)), pltpu.SemaphoreType.DMA((2, 7))],
        compiler_params=_cp(),
    )(w_in_flat, conv_blk)


def _reduce_grads(big, pack_a, pack_b):
    nb = len(big)

    def body(*refs):
        srcs, (pa_ref, pb_ref), outs, (ga_ref, gb_ref) = (
            refs[:nb], refs[nb:nb + 2], refs[nb + 2:2 * nb + 2], refs[2 * nb + 2:2 * nb + 4])
        scr = refs[2 * nb + 4:]
        r1s, r2s, sbs, sts = (scr[k * nb:(k + 1) * nb] for k in range(4))
        s1_send, s1_recv, s2_send, s2_recv, ag_send, ag_recv, st_sem = scr[4 * nb:]
        x, y, c = _mesh_pos()
        me = (x, y, c)
        sibling = (x, y, 1 - c)
        rel = [(x, y), (1 - x, y), (x, 1 - y), (1 - x, 1 - y)]

        ga_ref[_dev_index(*me)] = pa_ref[...]
        gb_ref[_dev_index(*me)] = pb_ref[...]

        def p1(a, r, to):
            return pltpu.make_async_remote_copy(
                src_ref=srcs[a].at[_dev_index(*rel[r], 1 - c)], dst_ref=r1s[a].at[r],
                send_sem=s1_send.at[a, r], recv_sem=s1_recv.at[a, r], device_id=to, device_id_type=MESH)

        _, _, n1, n2, diag = _relay_order()
        order = ((n1, 1 + c), (diag, 3), (n2, 2 - c), ((x, y), 0))

        def p1_landed(a, slot):
            return pltpu.make_async_remote_copy(
                src_ref=r1s[a].at[slot], dst_ref=r1s[a].at[slot], send_sem=s1_send.at[a, slot],
                recv_sem=s1_recv.at[a, slot], device_id=me, device_id_type=MESH)

        def p2(a, k, to):
            return pltpu.make_async_remote_copy(
                src_ref=sbs[a].at[k], dst_ref=r2s[a].at[k],
                send_sem=s2_send.at[a, k], recv_sem=s2_recv.at[a, k], device_id=to, device_id_type=MESH)

        def stage(a, i):
            return pltpu.make_async_copy(srcs[a].at[_dev_index(*order[i][0], c)], sts[a].at[i % 2], st_sem.at[a, i % 2])

        sends = [p1(a, r, sibling) for a in range(nb) for r in range(4)]
        for cp in sends:
            cp.start()
        _all_gather_blocks((ga_ref, gb_ref), ag_send, ag_recv)
        for a in range(nb):
            stage(a, 0).start()
            for i, (_, slot) in enumerate(order):
                if i + 1 < len(order):
                    stage(a, i + 1).start()
                stage(a, i).wait()
                p1_landed(a, slot).wait_recv()
                chip_sum = r1s[a][slot] + sts[a][i % 2]
                if i < 2:
                    sbs[a][i] = chip_sum.astype(BF16)
                    sends.append(p2(a, i, (*n1, c)))
                    sends[-1].start()
                else:
                    r1s[a][slot] = chip_sum
        for a in range(nb):
            p2(a, 0, me).wait_recv()
            p2(a, 1, me).wait_recv()
            sbs[a][2] = (r1s[a][2 - c] + r2s[a][1].astype(F32)).astype(BF16)
            sends.append(p2(a, 2, (*n2, c)))
            sends[-1].start()
        for a in range(nb):
            p2(a, 2, me).wait_recv()
            outs[a][...] = (r1s[a][0] + r2s[a][0].astype(F32)) + r2s[a][2].astype(F32)
        for cp in sends:
            cp.wait_send()

    vm = pl.BlockSpec(memory_space=pltpu.VMEM)
    hbm = pl.BlockSpec(memory_space=pl.ANY)
    blk = [p.shape[1:] for p in big]
    scratch = ([pltpu.VMEM((4,) + b, F32) for b in blk] + [pltpu.VMEM((3,) + b, BF16) for b in blk]
               + [pltpu.VMEM((3,) + b, BF16) for b in blk] + [pltpu.VMEM((2,) + b, F32) for b in blk]
               + [pltpu.SemaphoreType.DMA((nb, 4)), pltpu.SemaphoreType.DMA((nb, 4)),
                  pltpu.SemaphoreType.DMA((nb, 3)), pltpu.SemaphoreType.DMA((nb, 3)),
                  pltpu.SemaphoreType.DMA((2, 7)), pltpu.SemaphoreType.DMA((2, 7)),
                  pltpu.SemaphoreType.DMA((nb, 2))])
    return pl.pallas_call(
        body, name="reduce_grads",
        out_shape=tuple(jax.ShapeDtypeStruct(b, F32) for b in blk)
        + (jax.ShapeDtypeStruct((N_DEV,) + pack_a.shape, F32), jax.ShapeDtypeStruct((N_DEV,) + pack_b.shape, F32)),
        in_specs=[hbm] * nb + [vm, vm], out_specs=(vm,) * (nb + 2),
        scratch_shapes=scratch,
        compiler_params=_cp(),
    )(*big, pack_a, pack_b)


def _rms_hat(xf):
    r = lax.rsqrt(jnp.mean(xf * xf, axis=-1, keepdims=True) + NORM_EPS)
    return xf * r, r


def _shift_up(cur, next_head, s):
    ext = jnp.concatenate([cur, next_head], axis=0)
    n = ext.shape[0]
    return pltpu.roll(ext, n - s, 0)[:cur.shape[0], :]


def _pool_counts(i, tp, w):
    t = i * tp + lax.broadcasted_iota(jnp.int32, (tp, 1), 0)
    return jnp.minimum(t + 1, w).astype(F32)


def _pool_mix(u, u_prev_tail, i, tp):
    win = jnp.concatenate([u_prev_tail, u], axis=0)
    mixes = []
    for gi, w in enumerate(POOL_WINDOWS):
        win = win + pltpu.roll(win, w // 2, 0)
        cols = slice(gi * POOL_GROUP, (gi + 1) * POOL_GROUP)
        mixes.append(win[HALO:, :POOL_GROUP] / _pool_counts(i, tp, w) - u[:, cols])
        if gi + 1 < len(POOL_WINDOWS):
            win = win[:, POOL_GROUP:]
    return mixes


def _prev_tail(ref, i):
    return jnp.where(i > 0, ref[...], 0.0)


def _conv_taps(cur, prev_tail):
    ext = jnp.concatenate([prev_tail, cur], axis=0)
    return [cur] + [pltpu.roll(ext, sft, 0)[HALO:, :] for sft in range(1, CONV_WIDTH)]


def _conv_of_taps(taps, w4):
    y = taps[0] * w4[CONV_WIDTH - 1:CONV_WIDTH, :]
    for sft in range(1, CONV_WIDTH):
        y = y + taps[sft] * w4[CONV_WIDTH - 1 - sft:CONV_WIDTH - sft, :]
    return y


def _conv_fwd(cur, prev_tail, w4):
    ext = jnp.concatenate([prev_tail, cur], axis=0)
    y = ext * w4[CONV_WIDTH - 1:CONV_WIDTH, :]
    for sft in range(1, CONV_WIDTH):
        y = y + pltpu.roll(ext, sft, 0) * w4[CONV_WIDTH - 1 - sft:CONV_WIDTH - sft, :]
    return y[HALO:, :]


def _l2n_heads(t):
    parts = []
    for h in range(DN_HEADS):
        th = t[:, h * DN_HEAD_DIM:(h + 1) * DN_HEAD_DIM]
        parts.append(th * lax.rsqrt(jnp.sum(th * th, axis=-1, keepdims=True) + NORM_EPS))
    return jnp.concatenate(parts, axis=1)


def _post_conv(yq, yk, yv):
    return _l2n_heads(_silu(yq)), _l2n_heads(_silu(yk)), _silu(yv)


def _gates(ba, alog_lane, dtb_lane):
    lane = lax.broadcasted_iota(jnp.int32, ba.shape, 1)
    beta = jax.nn.sigmoid(ba)
    g = -jnp.exp(alog_lane) * _softplus(ba + dtb_lane)
    return jnp.where(lane < DN_HEADS, beta, jnp.where(lane < 2 * DN_HEADS, g, 0.0))


def _front(x2, norm_w, g_in, pool_w, pool_scale, conv_full, alog_lane, dtb_lane, w_out_blk, tm):
    s = x2.shape[0]

    def body(x_ref, nw_ref, pw_ref, ps_ref, cw_ref, al_ref, db_ref, wo_ref, g_hbm,
             pm_ref, pb_ref, yp_ref, qn_ref, kn_ref, vv_ref, gb_ref, gwo_hbm, wt_hbm,
             g_vmem, wt_vmem, tail_u, tail_qkv, gwo_ref, sem, wo_send, wo_recv):
        i = pl.program_id(0)

        @pl.when(i == 0)
        def _():
            gwo_ref[_dev_index(*_mesh_pos())] = wo_ref[...].astype(BF16)
            _direct_gather_start(gwo_ref, wo_send, wo_recv)
            cp = pltpu.make_async_copy(g_hbm, g_vmem, sem)
            cp.start()
            wt_vmem[D_MAIN:, :] = jnp.zeros((D_IN_PAD - D_MAIN, D_MODEL), BF16)
            tail_u[...] = jnp.zeros_like(tail_u)
            tail_qkv[...] = jnp.zeros_like(tail_qkv)
            cp.wait()
            for d in range(N_DEV):
                wt_vmem[W_IN_SHARD * d:W_IN_SHARD * (d + 1), :] = g_vmem[d]
            out = pltpu.make_async_copy(wt_vmem, wt_hbm, sem)
            out.start()
            out.wait()
        xhat, _ = _rms_hat(x_ref[...])
        n = (xhat * nw_ref[...]).astype(BF16)
        pm_ref[...] = _dot_nt_bf(n, wt_vmem[:D_MAIN, :])
        pb = _dot_nt_bf(n, wt_vmem[D_MAIN:, :])
        pb_ref[...] = pb
        u = pm_ref[:, :D_POOL]
        mixes = _pool_mix(u, tail_u[...], i, tm)
        tail_u[...] = u[tm - HALO:, :]
        gate = ps_ref[...] * _silu(pm_ref[:, D_POOL:2 * D_POOL])
        for gi in range(4):
            cols = slice(gi * POOL_GROUP, (gi + 1) * POOL_GROUP)
            yp_ref[:, cols] = _dot_bf(mixes[gi], pw_ref[gi]) * gate[:, cols]
        ys = []
        for c in range(3):
            cols = slice(c * D_DN, (c + 1) * D_DN)
            cur = pm_ref[:, 2 * D_POOL + c * D_DN:2 * D_POOL + (c + 1) * D_DN]
            ys.append(_conv_fwd(cur, tail_qkv[:, cols], cw_ref[:, cols]))
            tail_qkv[:, cols] = cur[tm - HALO:, :]
        qn, kn, vv = _post_conv(*ys)
        qn_ref[...] = qn
        kn_ref[...] = kn
        vv_ref[...] = vv
        gb_ref[...] = _gates(pb, al_ref[...], db_ref[...])

        @pl.when(i == s // tm - 1)
        def _():
            _direct_gather_wait(gwo_ref, wo_send, wo_recv)
            out = pltpu.make_async_copy(gwo_ref, gwo_hbm, sem)
            out.start()
            out.wait()

    tile = pl.BlockSpec((tm, D_DN), lambda i: (i, 0))
    lanes = pl.BlockSpec((tm, 128), lambda i: (i, 0))
    row = pl.BlockSpec((1, 128), lambda i: (0, 0))
    return pl.pallas_call(
        body, name="front", grid=(s // tm,),
        out_shape=(jax.ShapeDtypeStruct((s, D_MAIN), F32), jax.ShapeDtypeStruct((s, 128), F32),
                   jax.ShapeDtypeStruct((s, D_POOL), F32), jax.ShapeDtypeStruct((s, D_DN), F32),
                   jax.ShapeDtypeStruct((s, D_DN), F32), jax.ShapeDtypeStruct((s, D_DN), F32),
                   jax.ShapeDtypeStruct((s, 128), F32), jax.ShapeDtypeStruct((N_DEV,) + w_out_blk.shape, BF16),
                   jax.ShapeDtypeStruct((D_IN_PAD, D_MODEL), BF16)),
        in_specs=[pl.BlockSpec((tm, D_MODEL), lambda i: (i, 0)),
                  pl.BlockSpec((1, D_MODEL), lambda i: (0, 0)),
                  pl.BlockSpec((4, POOL_GROUP, POOL_GROUP), lambda i: (0, 0, 0)),
                  pl.BlockSpec((1, D_POOL), lambda i: (0, 0)),
                  pl.BlockSpec((CONV_WIDTH, 3 * D_DN), lambda i: (0, 0)), row, row,
                  pl.BlockSpec(memory_space=pltpu.VMEM), pl.BlockSpec(memory_space=pl.ANY)],
        out_specs=(pl.BlockSpec((tm, D_MAIN), lambda i: (i, 0)), lanes, tile, tile, tile, tile, lanes,
                   pl.BlockSpec(memory_space=pl.ANY), pl.BlockSpec(memory_space=pl.ANY)),
        scratch_shapes=[pltpu.VMEM((N_DEV, W_IN_SHARD, D_MODEL), BF16), pltpu.VMEM((D_IN_PAD, D_MODEL), BF16),
                        pltpu.VMEM((HALO, D_POOL), F32), pltpu.VMEM((HALO, 3 * D_DN), F32),
                        pltpu.VMEM((N_DEV,) + w_out_blk.shape, BF16),
                        pltpu.SemaphoreType.DMA, pltpu.SemaphoreType.DMA((7,)), pltpu.SemaphoreType.DMA((7,))],
        compiler_params=_cp(("arbitrary",)),
    )(x2, norm_w, pool_w, pool_scale, conv_full, alog_lane, dtb_lane, w_out_blk, g_in)


def _dn_block(q, k, v, gcol, bcol, state, dz, nw, known=None):
    nb, n, d = q.shape
    ii = lax.broadcasted_iota(jnp.int32, (n, n), 0)
    jj = lax.broadcasted_iota(jnp.int32, (n, n), 1)
    lower = ii >= jj
    eye = (ii == jj).astype(F32)
    g_row = jnp.sum(eye * gcol, axis=1, keepdims=True)
    gc_col = jnp.sum(jnp.where(lower, g_row, 0.0), axis=2, keepdims=True)
    gc_row = jnp.sum(eye * gc_col, axis=1, keepdims=True)
    decay = jnp.where(lower, jnp.exp(jnp.where(lower, gc_col - gc_row, 0.0)), 0.0)
    kb = k * bcol
    vb = v * bcol
    qs = q * (DN_HEAD_DIM ** -0.5)
    egc = jnp.exp(gc_col)
    kq = jnp.concatenate([kb, qs], axis=1)
    vk = jnp.concatenate([vb, kb * egc], axis=2)
    if known is None:
        akq = _mm(kq, k, 1, 1, *_DN_PREC["akq"])
    else:
        akq = _mm_known(kq, k, known[0][:, :, :n].astype(F32), 1, 1, _DN_PREC["akq"][1])
    a = jnp.where(ii > jj, akq[:, :n] * decay, 0.0)
    qk = akq[:, n:] * decay
    if known is None:
        t = _tri_inv(a, *_DN_PREC["inv"])
        uw = _mm(t, vk, 1, 0, *_DN_PREC["uw"])
    else:
        t = _tri_inv_known(a, known[0][:, :n, n:].astype(F32), _DN_PREC["inv"][1])
        uw = _mm_known(t, vk, known[1], 1, 0, _DN_PREC["uw"][1])
    pre = jnp.concatenate([akq, jnp.concatenate([t, jnp.zeros_like(t)], axis=1)], axis=2)
    wq = jnp.concatenate([uw[:, :, d:], qs * egc], axis=1)
    g_last = gc_col[:, n - 1:n, :]
    k_dec = k * jnp.exp(g_last - gc_col)
    e_last = jnp.exp(g_last)
    os_, starts = [], []
    for c in range(nb // DN_HEADS):
        sl = slice(c * DN_HEADS, (c + 1) * DN_HEADS)
        if known is not None and c > 0:
            state = _use_known(state, known[2][sl])
        starts.append(state)
        ws = _mm(wq[sl], state, 1, 0, *_DN_PREC["ws"])
        v_new = uw[sl, :, :d] - ws[:, :n]
        os_.append(ws[:, n:] + _mm(qk[sl], v_new, 1, 0, *_DN_PREC["ov"]))
        state = state * e_last[sl] + _mm(k_dec[sl], v_new, 0, 0, *_DN_PREC["st"])
    o = jnp.concatenate(os_, axis=0)
    y = o * lax.rsqrt(jnp.mean(o * o, axis=-1, keepdims=True) + NORM_EPS) * nw * _silu(dz)
    return y, state, (pre, uw, jnp.concatenate(starts, axis=0))


def _dn_block_args(gc, q_ref, k_ref, v_ref, gb_ref, dz_ref):
    qs, ks, vs, gs, bs, zs = [], [], [], [], [], []
    for cc in range(gc):
        r = slice(cc * CHUNK, (cc + 1) * CHUNK)
        gbv = gb_ref[r, :]
        for h in range(DN_HEADS):
            cols = slice(h * DN_HEAD_DIM, (h + 1) * DN_HEAD_DIM)
            qs.append(q_ref[r, cols])
            ks.append(k_ref[r, cols])
            vs.append(v_ref[r, cols])
            zs.append(dz_ref[r, cols])
            gs.append(gbv[:, DN_HEADS + h:DN_HEADS + h + 1])
            bs.append(gbv[:, h:h + 1])
    return tuple(jnp.stack(t, axis=0) for t in (qs, ks, vs, gs, bs, zs))


def _dn_scan_fwd(qn, kn, vv, gb, proj_main, dn_norm_w, gc):
    s = qn.shape[0]
    nchunk = s // CHUNK
    rows = gc * CHUNK

    def body(q_ref, k_ref, v_ref, gb_ref, dz_ref, nw_ref, y_ref, ss_ref, pre_ref, uw_ref, state):
        @pl.when(pl.program_id(0) == 0)
        def _():
            state[...] = jnp.zeros_like(state)
        q, k, v, gcol, bcol, dz = _dn_block_args(gc, q_ref, k_ref, v_ref, gb_ref, dz_ref)
        y, new, (pre, uw, starts) = _dn_block(q, k, v, gcol, bcol, state[...], dz, nw_ref[...])
        state[...] = new
        ss_ref[...] = starts
        pre_ref[...] = pre
        uw_ref[...] = uw
        for cc in range(gc):
            for h in range(DN_HEADS):
                y_ref[cc * CHUNK:(cc + 1) * CHUNK, h * DN_HEAD_DIM:(h + 1) * DN_HEAD_DIM] = y[cc * DN_HEADS + h]

    tile = pl.BlockSpec((rows, D_DN), lambda i: (i, 0))
    return pl.pallas_call(
        body, name="dn_scan_fwd", grid=(nchunk // gc,),
        out_shape=(jax.ShapeDtypeStruct((s, D_DN), F32),
                   jax.ShapeDtypeStruct((nchunk * DN_HEADS, DN_HEAD_DIM, DN_HEAD_DIM), F32),
                   jax.ShapeDtypeStruct((nchunk * DN_HEADS, 2 * CHUNK, 2 * CHUNK), F32),
                   jax.ShapeDtypeStruct((nchunk * DN_HEADS, CHUNK, 2 * DN_HEAD_DIM), F32)),
        in_specs=[tile, tile, tile, pl.BlockSpec((rows, 128), lambda i: (i, 0)),
                  pl.BlockSpec((rows, D_DN), lambda i: (i, 5)), pl.BlockSpec((1, 128), lambda i: (0, 0))],
        out_specs=(tile, pl.BlockSpec((gc * DN_HEADS, DN_HEAD_DIM, DN_HEAD_DIM), lambda i: (i, 0, 0)),
                   pl.BlockSpec((gc * DN_HEADS, 2 * CHUNK, 2 * CHUNK), lambda i: (i, 0, 0)),
                   pl.BlockSpec((gc * DN_HEADS, CHUNK, 2 * DN_HEAD_DIM), lambda i: (i, 0, 0))),
        scratch_shapes=[pltpu.VMEM((DN_HEADS, DN_HEAD_DIM, DN_HEAD_DIM), F32)],
        compiler_params=_cp(("arbitrary",)),
    )(qn, kn, vv, gb, proj_main, dn_norm_w)


def _out_proj_loss(y_pool, y_dn, x2, tgt, w_out_full, fnw, tm):
    s = x2.shape[0]

    def body(yp_ref, yd_ref, x_ref, t_ref, wo_ref, fw_ref,
             dh_ref, dyp_ref, dyd_ref, gwo_ref, gfw_ref, loss_ref):
        @pl.when(pl.program_id(0) == 0)
        def _():
            gwo_ref[...] = jnp.zeros_like(gwo_ref)
            gfw_ref[...] = jnp.zeros_like(gfw_ref)
            loss_ref[...] = jnp.zeros_like(loss_ref)
        y = jnp.concatenate([yp_ref[...], yd_ref[...]], axis=1).astype(BF16)
        wo = wo_ref[...]
        h = x_ref[...] + jnp.dot(y, wo, preferred_element_type=F32)
        hn, r = _rms_hat(h)
        fw = fw_ref[...]
        err = hn * fw - t_ref[...]
        loss_ref[...] += 0.5 * jnp.sum(jnp.sum(err * err, axis=-1, keepdims=True) / D_MODEL, axis=0, keepdims=True)
        dout = err / D_MODEL
        gfw_ref[...] += jnp.sum(dout * hn, axis=0, keepdims=True)
        dhn = dout * fw
        dh = r * (dhn - hn * jnp.mean(dhn * hn, axis=-1, keepdims=True))
        dh_ref[...] = dh
        dhb = dh.astype(BF16)
        dy = _dot_nt_bf(dhb, wo)
        dyp_ref[...] = dy[:, :D_POOL]
        dyd_ref[...] = dy[:, D_POOL:]
        gwo_ref[...] += _dot_tn_bf(y, dhb)

    half = pl.BlockSpec((tm, D_POOL), lambda i: (i, 0))
    full = pl.BlockSpec((tm, D_MODEL), lambda i: (i, 0))
    return pl.pallas_call(
        body, name="out_proj_loss", grid=(s // tm,),
        out_shape=(jax.ShapeDtypeStruct((s, D_MODEL), F32), jax.ShapeDtypeStruct((s, D_POOL), F32),
                   jax.ShapeDtypeStruct((s, D_DN), F32), jax.ShapeDtypeStruct((D_MODEL, D_MODEL), F32),
                   jax.ShapeDtypeStruct((1, D_MODEL), F32), jax.ShapeDtypeStruct((1, 128), F32)),
        in_specs=[half, half, full, full, pl.BlockSpec((D_MODEL, D_MODEL), lambda i: (0, 0)),
                  pl.BlockSpec((1, D_MODEL), lambda i: (0, 0))],
        out_specs=(full, half, half, pl.BlockSpec((D_MODEL, D_MODEL), lambda i: (0, 0)),
                   pl.BlockSpec((1, D_MODEL), lambda i: (0, 0)), pl.BlockSpec((1, 128), lambda i: (0, 0))),
        compiler_params=_cp(("arbitrary",)),
    )(y_pool, y_dn, x2, tgt, w_out_full, fnw)


def _dn_scan_bwd(qn, kn, vv, gb, proj_main, dn_norm_w, states, pre, uw, dy_dn, p_out, gc):
    s = qn.shape[0]
    nchunk = s // CHUNK
    nstep = nchunk // gc
    rows = gc * CHUNK

    def body(q_ref, k_ref, v_ref, gb_ref, dz_ref, nw_ref, ss_ref, pre_ref, uw_ref, dy_ref, po_ref,
             dq_ref, dk_ref, dv_ref, dgb_ref, ddz_ref, dnw_ref, gro_ref, dstate, po_send, po_recv, rs_send, rs_recv):
        @pl.when(pl.program_id(0) == 0)
        def _():
            dstate[...] = jnp.zeros_like(dstate)
            dnw_ref[...] = jnp.zeros_like(dnw_ref)
            po_send[...] = po_ref[...].astype(BF16)
            _direct_scatter_start(po_send, po_recv, rs_send, rs_recv)

        @pl.when(pl.program_id(0) == nstep - 1)
        def _():
            _direct_scatter_wait(po_send, po_recv, rs_send, rs_recv)
            total = po_ref[_dev_index(*_mesh_pos())]
            for k in range(N_DEV - 1):
                total = total + po_recv[k].astype(F32)
            gro_ref[...] = total
        lane = lax.broadcasted_iota(jnp.int32, (CHUNK, 128), 1)
        q, k, v, gcol, bcol, dz = _dn_block_args(gc, q_ref, k_ref, v_ref, gb_ref, dz_ref)
        dy = jnp.stack([dy_ref[cc * CHUNK:(cc + 1) * CHUNK, h * DN_HEAD_DIM:(h + 1) * DN_HEAD_DIM]
                        for cc in range(gc) for h in range(DN_HEADS)], axis=0)
        known = (pre_ref[...], uw_ref[...], ss_ref[...])
        _, vjp = jax.vjp(lambda *a: _dn_block(*a, known=known)[:2], q, k, v, gcol, bcol, ss_ref[:DN_HEADS], dz,
                         nw_ref[...])
        dq, dk, dv, dg, db, dst, ddz, dnw = vjp((dy, dstate[...]))
        dstate[...] = dst
        dnw_ref[...] += dnw
        for cc in range(gc):
            r = slice(cc * CHUNK, (cc + 1) * CHUNK)
            dgb = jnp.zeros((CHUNK, 128), F32)
            for h in range(DN_HEADS):
                b = cc * DN_HEADS + h
                cols = slice(h * DN_HEAD_DIM, (h + 1) * DN_HEAD_DIM)
                for ref, val in zip((dq_ref, dk_ref, dv_ref, ddz_ref), (dq, dk, dv, ddz)):
                    ref[r, cols] = val[b]
                dgb = dgb + jnp.where(lane == h, db[b], 0.0) + jnp.where(lane == DN_HEADS + h, dg[b], 0.0)
            dgb_ref[r, :] = dgb

    rev = lambda i: (nstep - 1 - i, 0)
    tile = pl.BlockSpec((rows, D_DN), rev)
    lanes = pl.BlockSpec((rows, 128), rev)
    return pl.pallas_call(
        body, name="dn_scan_bwd", grid=(nstep,),
        out_shape=(jax.ShapeDtypeStruct((s, D_DN), F32),) * 3
        + (jax.ShapeDtypeStruct((s, 128), F32), jax.ShapeDtypeStruct((s, D_DN), F32),
           jax.ShapeDtypeStruct((1, 128), F32), jax.ShapeDtypeStruct(p_out.shape[1:], F32)),
        in_specs=[tile, tile, tile, lanes, pl.BlockSpec((rows, D_DN), lambda i: (nstep - 1 - i, 5)),
                  pl.BlockSpec((1, 128), lambda i: (0, 0)),
                  pl.BlockSpec((gc * DN_HEADS, DN_HEAD_DIM, DN_HEAD_DIM), lambda i: (nstep - 1 - i, 0, 0)),
                  pl.BlockSpec((gc * DN_HEADS, 2 * CHUNK, 2 * CHUNK), lambda i: (nstep - 1 - i, 0, 0)),
                  pl.BlockSpec((gc * DN_HEADS, CHUNK, 2 * DN_HEAD_DIM), lambda i: (nstep - 1 - i, 0, 0)), tile,
                  pl.BlockSpec(memory_space=pltpu.VMEM)],
        out_specs=(tile, tile, tile, lanes, tile, pl.BlockSpec((1, 128), lambda i: (0, 0)),
                   pl.BlockSpec(memory_space=pltpu.VMEM)),
        scratch_shapes=[pltpu.VMEM((DN_HEADS, DN_HEAD_DIM, DN_HEAD_DIM), F32),
                        pltpu.VMEM(p_out.shape, BF16), pltpu.VMEM((N_DEV - 1,) + p_out.shape[1:], BF16),
                        pltpu.SemaphoreType.DMA((7,)), pltpu.SemaphoreType.DMA((7,))],
        compiler_params=_cp(("arbitrary",)),
    )(qn, kn, vv, gb, proj_main, dn_norm_w, states, pre, uw, dy_dn, p_out)


def _back(proj_main, proj_ba, dyp, dqn, dkn, dvv, dgb, ddz, x2, dh, norm_w, pool_w, pool_scale, conv_full,
          alog_lane, dtb_lane, wt_full, tm):
    s = x2.shape[0]
    nstep = s // tm
    per = tm // HALO

    def body(u_ref, z_ref, q_ref, k_ref, v_ref, up_ref, qp_ref, kp_ref, vp_ref, ba_ref,
             dyp_ref, dqn_ref, dkn_ref, dvv_ref, dgb_ref, ddz_ref, x_ref, dh_ref,
             nw_ref, pw_ref, ps_ref, cw_ref, al_ref, db_ref, wt_hbm,
             gx_ref, p_hbm, gnw_ref, dpw_ref, dps_ref, dcw_ref, dal_ref, ddb_ref,
             wt_vmem, acc, blk, head_dc, head_dw, sem, osem):
        j = pl.program_id(0)
        i = nstep - 1 - j

        @pl.when(j == 0)
        def _():
            cp = pltpu.make_async_copy(wt_hbm, wt_vmem, sem)
            cp.start()
            acc[...] = jnp.zeros_like(acc)
            for ref in (gnw_ref, dpw_ref, dps_ref, dcw_ref, dal_ref, ddb_ref, head_dc, head_dw):
                ref[...] = jnp.zeros_like(ref)
            cp.wait()

        u = u_ref[...]
        z = z_ref[...]
        dy = dyp_ref[...]
        ps = ps_ref[...]
        mixes = _pool_mix(u, _prev_tail(up_ref, i), i, tm)
        sg = jax.nn.sigmoid(z)
        sz = z * sg
        dsz = sg * (1.0 + z * (1.0 - sg))
        dzs, dwins = [], []
        for gi, w in enumerate(POOL_WINDOWS):
            cols = slice(gi * POOL_GROUP, (gi + 1) * POOL_GROUP)
            mixw = _dot_bf(mixes[gi], pw_ref[gi])
            dmixw = dy[:, cols] * ps[:, cols] * sz[:, cols]
            dps_ref[:, cols] += jnp.sum(dy[:, cols] * mixw * sz[:, cols], axis=0, keepdims=True)
            dzs.append(dy[:, cols] * mixw * ps[:, cols] * dsz[:, cols])
            dpw_ref[gi] += _dot_tn_bf(mixes[gi], dmixw)
            dwins.append(_dot_nt_bf(dmixw, pw_ref[gi]) / _pool_counts(i, tm, w))
        dzp = jnp.concatenate(dzs, axis=1)
        dw = jnp.concatenate(dwins, axis=1)
        win = jnp.concatenate([dw, head_dw[...]], axis=0)
        m = win.shape[0]
        dups = []
        for gi, w in enumerate(POOL_WINDOWS):
            win = win + pltpu.roll(win, m - w // 2, 0)
            cols = slice(gi * POOL_GROUP, (gi + 1) * POOL_GROUP)
            dups.append(win[:tm, :POOL_GROUP] - dw[:, cols] * _pool_counts(i, tm, w))
            if gi + 1 < len(POOL_WINDOWS):
                win = win[:, POOL_GROUP:]
        dup = jnp.concatenate(dups, axis=1)
        head_dw[...] = dw[:HALO, :]

        curs = (q_ref[...], k_ref[...], v_ref[...])
        tails = (_prev_tail(qp_ref, i), _prev_tail(kp_ref, i), _prev_tail(vp_ref, i))
        taps = [_conv_taps(curs[c], tails[c]) for c in range(3)]
        ys = [_conv_of_taps(taps[c], cw_ref[:, c * D_DN:(c + 1) * D_DN]) for c in range(3)]
        _, vjp = jax.vjp(_post_conv, *ys)
        dys = vjp((dqn_ref[...], dkn_ref[...], dvv_ref[...]))
        dxs = []
        for c, dyc in enumerate(dys):
            cols = slice(c * D_DN, (c + 1) * D_DN)
            w4 = cw_ref[:, cols]
            for sft in range(CONV_WIDTH):
                row = CONV_WIDTH - 1 - sft
                dcw_ref[row:row + 1, cols] += jnp.sum(dyc * taps[c][sft], axis=0, keepdims=True)
            head = head_dc[:, cols]
            dx = dyc * w4[CONV_WIDTH - 1:CONV_WIDTH, :]
            for sft in range(1, CONV_WIDTH):
                dx = dx + _shift_up(dyc, head, sft) * w4[CONV_WIDTH - 1 - sft:CONV_WIDTH - sft, :]
            dxs.append(dx)
            head_dc[:, cols] = dyc[:HALO, :]
        _, gvjp = jax.vjp(_gates, ba_ref[...], al_ref[...], db_ref[...])
        dba, dal, ddb = gvjp(dgb_ref[...])
        dal_ref[...] += dal
        ddb_ref[...] += ddb

        dbab = dba.astype(BF16)
        xhat, r = _rms_hat(x_ref[...])
        nw = nw_ref[...]
        n = (xhat * nw).astype(BF16)
        acc[D_MAIN:, :] += _dot_tn_bf(dbab, n)
        dn = jnp.dot(dbab, wt_vmem[D_MAIN:, :], preferred_element_type=F32)
        for cb, d in enumerate((dup, dzp, dxs[0], dxs[1], dxs[2], ddz_ref[...])):
            rows = slice(cb * D_POOL, (cb + 1) * D_POOL)
            dpart = d.astype(BF16)
            acc[rows, :] += _dot_tn_bf(dpart, n)
            dn = dn + jnp.dot(dpart, wt_vmem[rows, :], preferred_element_type=F32)
        gnw_ref[...] += jnp.sum(dn * xhat, axis=0, keepdims=True)
        dxh = dn * nw
        gx_ref[...] = dh_ref[...] + r * (dxh - xhat * jnp.mean(dxh * xhat, axis=-1, keepdims=True))

        @pl.when(j == nstep - 1)
        def _():
            def out(d):
                return pltpu.make_async_copy(blk.at[d % 2], p_hbm.at[d], osem.at[d % 2])
            for d in range(N_DEV):
                if d >= 2:
                    out(d - 2).wait()
                blk[d % 2] = acc[W_IN_SHARD * d:W_IN_SHARD * (d + 1), :]
                out(d).start()
            out(N_DEV - 2).wait()
            out(N_DEV - 1).wait()

    def col(c):
        return pl.BlockSpec((tm, D_POOL), lambda j: (nstep - 1 - j, c))

    def halo(c):
        return pl.BlockSpec((HALO, D_POOL), lambda j: (jnp.maximum((nstep - 1 - j) * per - 1, 0), c))

    rev = lambda j: (nstep - 1 - j, 0)
    part = pl.BlockSpec((tm, D_POOL), rev)
    lanes = pl.BlockSpec((tm, 128), rev)
    full = pl.BlockSpec((tm, D_MODEL), rev)
    row = pl.BlockSpec((1, D_MODEL), lambda j: (0, 0))
    lrow = pl.BlockSpec((1, 128), lambda j: (0, 0))
    pw = pl.BlockSpec((4, POOL_GROUP, POOL_GROUP), lambda j: (0, 0, 0))
    psp = pl.BlockSpec((1, D_POOL), lambda j: (0, 0))
    cw = pl.BlockSpec((CONV_WIDTH, 3 * D_DN), lambda j: (0, 0))
    return pl.pallas_call(
        body, name="back", grid=(nstep,),
        out_shape=(jax.ShapeDtypeStruct((s, D_MODEL), F32),
                   jax.ShapeDtypeStruct((N_DEV, W_IN_SHARD, D_MODEL), F32), jax.ShapeDtypeStruct((1, D_MODEL), F32),
                   jax.ShapeDtypeStruct((4, POOL_GROUP, POOL_GROUP), F32), jax.ShapeDtypeStruct((1, D_POOL), F32),
                   jax.ShapeDtypeStruct((CONV_WIDTH, 3 * D_DN), F32),
                   jax.ShapeDtypeStruct((1, 128), F32), jax.ShapeDtypeStruct((1, 128), F32)),
        in_specs=[col(0), col(1), col(2), col(3), col(4), halo(0), halo(2), halo(3), halo(4), lanes,
                  part, part, part, part, lanes, part, full, full,
                  row, pw, psp, cw, lrow, lrow, pl.BlockSpec(memory_space=pl.ANY)],
        out_specs=(full, pl.BlockSpec(memory_space=pl.ANY), row, pw, psp, cw, lrow, lrow),
        scratch_shapes=[pltpu.VMEM((D_IN_PAD, D_MODEL), BF16), pltpu.VMEM((D_IN_PAD, D_MODEL), F32),
                        pltpu.VMEM((2, W_IN_SHARD, D_MODEL), F32),
                        pltpu.VMEM((HALO, 3 * D_DN), F32), pltpu.VMEM((HALO, D_POOL), F32),
                        pltpu.SemaphoreType.DMA, pltpu.SemaphoreType.DMA((2,))],
        compiler_params=_cp(("arbitrary",)),
    )(proj_main, proj_main, proj_main, proj_main, proj_main, proj_main, proj_main, proj_main, proj_main, proj_ba,
      dyp, dqn, dkn, dvv, dgb, ddz, x2, dh, norm_w, pool_w, pool_scale, conv_full, alog_lane, dtb_lane, wt_full)


def _adamw_math(w, g, m, v):
    m = ADAM_B1 * m + (1.0 - ADAM_B1) * g
    v = ADAM_B2 * v + (1.0 - ADAM_B2) * (g * g)
    m_hat = m / (1.0 - ADAM_B1 ** ADAM_STEP)
    v_hat = v / (1.0 - ADAM_B2 ** ADAM_STEP)
    delta = -ADAM_LR * (m_hat / (jnp.sqrt(v_hat) + ADAM_EPS) + ADAM_WD * w)
    return delta, m, v


def _adamw_sharded(params):
    k = len(params)

    def body(*refs):
        ins, outs = refs[:4 * k], refs[4 * k:]
        for p in range(k):
            w_ref, g_ref, m_ref, v_ref = ins[4 * p:4 * p + 4]
            go_ref = outs[4 * p]
            if g_ref.shape == w_ref.shape:
                go_ref[...] = g_ref[...]
            else:
                for j in range(FLAT_ROWS):
                    go_ref[pl.ds(j, W_IN_SHARD, stride=FLAT_ROWS), :] = g_ref[:, 128 * j:128 * (j + 1)]
            d, nm, nv = _adamw_math(w_ref[...], go_ref[...], m_ref[...], v_ref[...])
            outs[4 * p + 1][...] = d
            outs[4 * p + 2][...] = nm
            outs[4 * p + 3][...] = nv

    flat = [a for p in params for a in p]
    out_shape = tuple(jax.ShapeDtypeStruct(p[0].shape, F32) for p in params for _ in range(4))
    res = pl.pallas_call(body, name="adamw_sharded", out_shape=out_shape, compiler_params=_cp())(*flat)
    return [tuple(res[4 * p:4 * p + 4]) for p in range(k)]


def _adamw_replicated(gath_a, gath_b, pool, rows):
    nrow = len(rows)

    def body(*refs):
        ga_ref, gb_ref = refs[:2]
        ins = refs[2:2 + 3 * (nrow + 1)]
        outs = refs[2 + 3 * (nrow + 1):]

        def total(ref):
            g = ref[0]
            for d in range(1, N_DEV):
                g = g + ref[d]
            return g

        def update(g, wmv, o):
            w, m, v = (r[...] for r in wmv)
            dl, nm, nv = _adamw_math(w, g, m, v)
            o[0][...] = g
            o[1][...] = dl
            o[2][...] = nm
            o[3][...] = nv

        update(total(ga_ref), ins[:3], outs[:4])
        gb = total(gb_ref)
        for r in range(nrow):
            n = ins[3 * (r + 1)].shape[1]
            update(gb[r:r + 1, :n], ins[3 * (r + 1):3 * (r + 2)], outs[4 * (r + 1):4 * (r + 2)])
        outs[4 * (nrow + 1)][...] = gb[nrow:nrow + 1, 0:1]

    flat = list(pool) + [a for wmv in rows for a in wmv]
    out_shape = ((jax.ShapeDtypeStruct(pool[0].shape, F32),) * 4
                 + tuple(jax.ShapeDtypeStruct(wmv[0].shape, F32) for wmv in rows for _ in range(4))
                 + (jax.ShapeDtypeStruct((1, 1), F32),))
    res = pl.pallas_call(body, name="adamw_replicated", out_shape=out_shape, compiler_params=_cp())(
        gath_a, gath_b, *flat)
    return [res[4 * k:4 * k + 4] for k in range(nrow + 1)], res[-1]


_ROW_ORDER = ("norm_w", "final_norm_w", "pool_scale", "dn_norm_w", "a_log", "dt_bias")


def _pack_rows(vectors):
    out = [jnp.pad(v.reshape(-1), (0, D_MODEL - v.size)) for v in vectors]
    out += [jnp.zeros((D_MODEL,), F32)] * (8 - len(out))
    return jnp.stack(out, axis=0)


def _lane_row(vec4, start):
    return jnp.pad(vec4.reshape(-1), (start, 128 - start - vec4.size)).reshape(1, 128)


def kernel(x, norm_w, w_in, pool_w, pool_scale, conv_w, a_log, dt_bias, dn_norm_w, w_out, final_norm_w, loss_target, m_norm_w, m_w_in, m_pool_w, m_pool_scale, m_conv_w, m_a_log, m_dt_bias, m_dn_norm_w, m_w_out, m_final_norm_w, v_norm_w, v_w_in, v_pool_w, v_pool_scale, v_conv_w, v_a_log, v_dt_bias, v_dn_norm_w, v_w_out, v_final_norm_w):
    s = x.shape[1]
    tm = min(512, s)
    tmb = min(256, s)
    x2 = x[0]
    tgt = loss_target[0]
    def to_flat(a):
        return a[0].reshape(FLAT_ROWS, 128, W_IN_SHARD).transpose(2, 0, 1).reshape(W_IN_SHARD * FLAT_ROWS, 128)

    def from_flat(f):
        return f.reshape(W_IN_SHARD, FLAT_ROWS, 128).transpose(1, 2, 0).reshape(1, D_MODEL, W_IN_SHARD)

    wf, m_wf, v_wf = to_flat(w_in), to_flat(m_w_in), to_flat(v_w_in)

    g_in, g_conv = _gather_weights(wf, conv_w[0])
    conv_full = g_conv.transpose(1, 0, 2).reshape(CONV_WIDTH, 3 * D_DN)
    alog_lane = _lane_row(a_log, DN_HEADS)
    dtb_lane = _lane_row(dt_bias, DN_HEADS)
    fnw = final_norm_w.reshape(1, D_MODEL)

    proj_main, proj_ba, y_pool, qn, kn, vv, gb, g_out, wt_full = _front(
        x2, norm_w, g_in, pool_w[0], pool_scale, conv_full, alog_lane, dtb_lane, w_out[0], tm)
    w_out_full = g_out.reshape(D_MODEL, D_MODEL)
    y_dn, states, dn_pre, dn_uw = _dn_scan_fwd(qn, kn, vv, gb, proj_main, dn_norm_w, DN_CHUNKS_PER_STEP)

    dh, dyp, dyd, g_wout, g_fnw, loss_part = _out_proj_loss(y_pool, y_dn, x2, tgt, w_out_full, fnw, tm)
    p_out = g_wout.reshape(N_DEV, D_MODEL // N_DEV, D_MODEL)
    dqn, dkn, dvv, dgb, ddz, g_dnw, gr_out = _dn_scan_bwd(qn, kn, vv, gb, proj_main, dn_norm_w, states, dn_pre, dn_uw,
                                                          dyd, p_out, DN_CHUNKS_PER_STEP)
    grad_x2, p_in, g_nw, g_pw, g_ps, g_conv_full, g_al, g_db = _back(
        proj_main, proj_ba, dyp, dqn, dkn, dvv, dgb, ddz, x2, dh, norm_w, pool_w[0], pool_scale, conv_full,
        alog_lane, dtb_lane, wt_full, tmb)

    p_conv = g_conv_full.reshape(CONV_WIDTH, N_DEV, 3 * D_DN // N_DEV).transpose(1, 0, 2)
    pack_a = g_pw.reshape(4 * POOL_GROUP, POOL_GROUP)
    pack_b = _pack_rows([g_nw, g_fnw, g_ps, g_dnw, g_al[0, DN_HEADS:2 * DN_HEADS], g_db[0, DN_HEADS:2 * DN_HEADS],
                         loss_part[0, :1]])
    gr_in, gr_conv, gath_a, gath_b = _reduce_grads((p_in, p_conv), pack_a, pack_b)

    r_in, r_out, r_conv = _adamw_sharded([(wf, gr_in, m_wf, v_wf), (w_out[0], gr_out, m_w_out[0], v_w_out[0]),
                                          (conv_w[0], gr_conv, m_conv_w[0], v_conv_w[0])])
    flat = lambda a: a.reshape(4 * POOL_GROUP, POOL_GROUP)
    row = lambda a: a.reshape(1, -1)
    vecs = {"norm_w": (norm_w, m_norm_w, v_norm_w), "final_norm_w": (final_norm_w, m_final_norm_w, v_final_norm_w),
            "pool_scale": (pool_scale, m_pool_scale, v_pool_scale), "dn_norm_w": (dn_norm_w, m_dn_norm_w, v_dn_norm_w),
            "a_log": (a_log, m_a_log, v_a_log), "dt_bias": (dt_bias, m_dt_bias, v_dt_bias)}
    res, loss = _adamw_replicated(gath_a, gath_b, (flat(pool_w), flat(m_pool_w), flat(v_pool_w)),
                                  [tuple(row(a) for a in vecs[nm]) for nm in _ROW_ORDER])
    r_pool = res[0]
    r_vec = dict(zip(_ROW_ORDER, res[1:]))

    def group(k):
        vec = lambda nm: r_vec[nm][k].reshape(vecs[nm][0].shape)
        return (vec("norm_w"), from_flat(r_in[k]), r_pool[k].reshape(pool_w.shape), vec("pool_scale"), r_conv[k][None],
                vec("a_log"), vec("dt_bias"), vec("dn_norm_w"), r_out[k][None], vec("final_norm_w"))

    return (loss[0, 0], grad_x2[None], *group(0), *group(1), *group(2), *group(3))
```

```python
import functools

import jax
import jax.numpy as jnp
from jax import lax
from jax.experimental import pallas as pl
from jax.experimental.pallas import tpu as pltpu

F32 = jnp.float32
BF16 = jnp.bfloat16
HI = lax.Precision.HIGHEST
MESH = pl.DeviceIdType.MESH

D_MODEL = 1024
D_POOL = 512
D_DN = 512
POOL_WINDOWS = (2, 4, 8, 16)
POOL_GROUP = 128
DN_HEADS = 4
DN_HEAD_DIM = 128
CONV_WIDTH = 4
CHUNK = 64
NORM_EPS = 1e-6
D_IN = 3080
D_MAIN = 3072
FLAT_ROWS = D_MODEL // 128
D_IN_PAD = D_MAIN + 128
N_DEV = 8
W_IN_SHARD = D_IN // N_DEV
HALO = 16
DN_CHUNKS_PER_STEP = 8

ADAM_LR = 0.001
ADAM_B1 = 0.9
ADAM_B2 = 0.999
ADAM_EPS = 1e-08
ADAM_WD = 0.01
ADAM_STEP = 10

VMEM_LIMIT = 56 * 1024 * 1024
def _cp(sem=None, vmem=VMEM_LIMIT):
    kw = {"vmem_limit_bytes": vmem}
    if sem is not None:
        kw["dimension_semantics"] = sem
    return pltpu.CompilerParams(**kw)


def _dot_bf(a, b):
    return jnp.dot(a.astype(BF16), b.astype(BF16), preferred_element_type=F32)


def _dot_nt_bf(a, b):
    return lax.dot_general(a.astype(BF16), b.astype(BF16), (((1,), (1,)), ((), ())), preferred_element_type=F32)


def _dot_tn_bf(a, b):
    return lax.dot_general(a.astype(BF16), b.astype(BF16), (((0,), (0,)), ((), ())), preferred_element_type=F32)


def _mm_raw(a, b, ca, cb, prec):
    off = a.ndim - 2
    dn = (((ca + off,), (cb + off,)), ((0,), (0,)) if off else ((), ()))
    if prec == "hi":
        return lax.dot_general(a, b, dn, precision=HI, preferred_element_type=F32)
    ah, bh = a.astype(BF16), b.astype(BF16)
    out = lax.dot_general(ah, bh, dn, preferred_element_type=F32)
    if prec == "x3":
        al = (a - ah.astype(F32)).astype(BF16)
        bl = (b - bh.astype(F32)).astype(BF16)
        out = out + lax.dot_general(ah, bl, dn, preferred_element_type=F32)
        out = out + lax.dot_general(al, bh, dn, preferred_element_type=F32)
    return out


@functools.partial(jax.custom_vjp, nondiff_argnums=(2, 3, 4, 5))
def _mm(a, b, ca, cb, prec, bprec):
    return _mm_raw(a, b, ca, cb, prec)


def _mm_fwd(a, b, ca, cb, prec, bprec):
    return _mm_raw(a, b, ca, cb, prec), (a, b)


def _mm_bwd(ca, cb, prec, bprec, res, dc):
    a, b = res
    da = _mm_raw(dc, b, 1, 1 - cb, bprec) if ca == 1 else _mm_raw(b, dc, 1 - cb, 1, bprec)
    db = _mm_raw(a, dc, 1 - ca, 0, bprec) if cb == 0 else _mm_raw(dc, a, 0, 1 - ca, bprec)
    return da, db


_mm.defvjp(_mm_fwd, _mm_bwd)


@functools.partial(jax.custom_vjp, nondiff_argnums=(1, 2))
def _tri_inv(a, prec, bprec):
    n = a.shape[-1]
    ii = lax.broadcasted_iota(jnp.int32, (n, n), 0)
    jj = lax.broadcasted_iota(jnp.int32, (n, n), 1)
    p = (ii == jj).astype(F32) - a
    b = _mm_raw(a, a, 1, 0, prec)
    for _ in range(4):
        pb = _mm_raw(jnp.concatenate([p, b], axis=-2), b, 1, 0, prec)
        p = p + pb[..., :n, :]
        b = pb[..., n:, :]
    return p + _mm_raw(p, b, 1, 0, prec)


def _tri_inv_fwd(a, prec, bprec):
    t = _tri_inv(a, prec, bprec)
    return t, t


def _tri_inv_bwd(prec, bprec, t, dt):
    return (-_mm_raw(_mm_raw(t, dt, 0, 0, bprec), t, 1, 1, bprec),)


_tri_inv.defvjp(_tri_inv_fwd, _tri_inv_bwd)

@functools.partial(jax.custom_vjp, nondiff_argnums=(3, 4, 5))
def _mm_known(a, b, out, ca, cb, bprec):
    return out


def _mm_known_fwd(a, b, out, ca, cb, bprec):
    return out, (a, b)


def _mm_known_bwd(ca, cb, bprec, res, dc):
    return _mm_bwd(ca, cb, None, bprec, res, dc) + (jnp.zeros_like(dc),)


_mm_known.defvjp(_mm_known_fwd, _mm_known_bwd)


@jax.custom_vjp
def _use_known(x, known):
    return known


_use_known.defvjp(lambda x, known: (known, None), lambda _, g: (g, jnp.zeros_like(g)))


@functools.partial(jax.custom_vjp, nondiff_argnums=(2,))
def _tri_inv_known(a, t, bprec):
    return t


def _tri_inv_known_fwd(a, t, bprec):
    return t, t


def _tri_inv_known_bwd(bprec, t, dt):
    return _tri_inv_bwd(None, bprec, t, dt) + (jnp.zeros_like(dt),)


_tri_inv_known.defvjp(_tri_inv_known_fwd, _tri_inv_known_bwd)

_DN_PREC = {"akq": ("bf16", "bf16"), "inv": ("bf16", "bf16"), "uw": ("bf16", "bf16"), "ws": ("bf16", "bf16"),
            "ov": ("bf16", "bf16"), "st": ("bf16", "bf16")}


def _silu(x):
    return x * jax.nn.sigmoid(x)


def _softplus(x):
    pos = x > 0.0
    return jnp.where(pos, x, 0.0) + jnp.log1p(jnp.exp(jnp.where(pos, -x, x)))


def _mesh_pos():
    return lax.axis_index("x"), lax.axis_index("y"), lax.axis_index("c")


def _dev_index(x, y, c):
    return 4 * x + 2 * y + c


def _relay_order():
    x, y, c = _mesh_pos()
    n1 = (x + (1 - c) * (1 - 2 * x), y + c * (1 - 2 * y))
    n2 = (x + c * (1 - 2 * x), y + (1 - c) * (1 - 2 * y))
    return (x, y, c), (x, y, 1 - c), n1, n2, (1 - x, 1 - y)


def _all_gather_blocks(outs, send_sems, recv_sems):
    me, sibling, n1, n2, diag = _relay_order()
    c = me[2]

    def copy(a, k, block, to):
        rows = outs[a].at[_dev_index(*block)]
        return pltpu.make_async_remote_copy(src_ref=rows, dst_ref=rows, send_sem=send_sems.at[a, k],
                                            recv_sem=recv_sems.at[a, k], device_id=to, device_id_type=MESH)

    n = len(outs)
    started = []

    def start(cp):
        cp.start()
        started.append(cp)

    for a in range(n):
        start(copy(a, 1, me, (*n1, c)))
        start(copy(a, 2, me, (*n2, c)))
        start(copy(a, 0, me, sibling))
    for a in range(n):
        copy(a, 1, (*n1, c), me).wait_recv()
        start(copy(a, 3, (*n1, c), (*n2, c)))
        start(copy(a, 4, (*n1, c), sibling))
    for a in range(n):
        copy(a, 2, (*n2, c), me).wait_recv()
        start(copy(a, 5, (*n2, c), sibling))
        copy(a, 3, (*diag, c), me).wait_recv()
        start(copy(a, 6, (*diag, c), sibling))
    for a in range(n):
        copy(a, 0, sibling, me).wait_recv()
        copy(a, 4, (*n2, 1 - c), me).wait_recv()
        copy(a, 5, (*n1, 1 - c), me).wait_recv()
        copy(a, 6, (*diag, 1 - c), me).wait_recv()
    for cp in started:
        cp.wait_send()


def _peer_relations():
    x, y, c = _mesh_pos()
    flips = [(fx, fy, fc) for fx in (0, 1) for fy in (0, 1) for fc in (0, 1)][1:]
    peers = [(1 - x if fx else x, 1 - y if fy else y, 1 - c if fc else c) for fx, fy, fc in flips]
    return (x, y, c), peers


def _direct_gather_start(out_ref, send_sems, recv_sems):
    me, peers = _peer_relations()
    rows = out_ref.at[_dev_index(*me)]
    for k, peer in enumerate(peers):
        pltpu.make_async_remote_copy(src_ref=rows, dst_ref=rows, send_sem=send_sems.at[k], recv_sem=recv_sems.at[k],
                                     device_id=peer, device_id_type=MESH).start()


def _direct_gather_wait(out_ref, send_sems, recv_sems):
    me, peers = _peer_relations()
    for k, peer in enumerate(peers):
        rows = out_ref.at[_dev_index(*peer)]
        cp = pltpu.make_async_remote_copy(src_ref=rows, dst_ref=rows, send_sem=send_sems.at[k],
                                          recv_sem=recv_sems.at[k], device_id=peer, device_id_type=MESH)
        cp.wait_recv()
        cp.wait_send()


def _direct_scatter_start(send_ref, recv_ref, send_sems, recv_sems):
    me, peers = _peer_relations()
    for k, peer in enumerate(peers):
        pltpu.make_async_remote_copy(src_ref=send_ref.at[_dev_index(*peer)], dst_ref=recv_ref.at[k],
                                     send_sem=send_sems.at[k], recv_sem=recv_sems.at[k],
                                     device_id=peer, device_id_type=MESH).start()


def _direct_scatter_wait(send_ref, recv_ref, send_sems, recv_sems):
    me, peers = _peer_relations()
    for k, peer in enumerate(peers):
        cp = pltpu.make_async_remote_copy(src_ref=send_ref.at[_dev_index(*peer)], dst_ref=recv_ref.at[k],
                                          send_sem=send_sems.at[k], recv_sem=recv_sems.at[k],
                                          device_id=peer, device_id_type=MESH)
        cp.wait_recv()
        cp.wait_send()


def _gather_weights(w_in_flat, conv_blk):
    def body(win_ref, conv_ref, gin_ref, gconv_ref, send_sems, recv_sems):
        x, y, c = _mesh_pos()
        me = _dev_index(x, y, c)
        for j in range(FLAT_ROWS):
            gin_ref[me, :, 128 * j:128 * (j + 1)] = win_ref[pl.ds(j, W_IN_SHARD, stride=FLAT_ROWS), :].astype(BF16)
        gconv_ref[me] = conv_ref[...]
        _all_gather_blocks((gin_ref, gconv_ref), send_sems, recv_sems)

    vm = pl.BlockSpec(memory_space=pltpu.VMEM)
    return pl.pallas_call(
        body, name="gather_weights",
        out_shape=(jax.ShapeDtypeStruct((N_DEV, W_IN_SHARD, D_MODEL), BF16),
                   jax.ShapeDtypeStruct((N_DEV,) + conv_blk.shape, F32)),
        in_specs=[vm, vm], out_specs=(vm, vm),
        scratch_shapes=[pltpu.SemaphoreType.DMA((2, 7)), pltpu.SemaphoreType.DMA((2, 7))],
        compiler_params=_cp(),
    )(w_in_flat, conv_blk)


def _reduce_grads(big, pack_a, pack_b):
    nb = len(big)

    def body(*refs):
        srcs, (pa_ref, pb_ref), outs, (ga_ref, gb_ref) = (
            refs[:nb], refs[nb:nb + 2], refs[nb + 2:2 * nb + 2], refs[2 * nb + 2:2 * nb + 4])
        scr = refs[2 * nb + 4:]
        r1s, r2s, sbs, sts = (scr[k * nb:(k + 1) * nb] for k in range(4))
        s1_send, s1_recv, s2_send, s2_recv, ag_send, ag_recv, st_sem = scr[4 * nb:]
        x, y, c = _mesh_pos()
        me = (x, y, c)
        sibling = (x, y, 1 - c)
        rel = [(x, y), (1 - x, y), (x, 1 - y), (1 - x, 1 - y)]

        ga_ref[_dev_index(*me)] = pa_ref[...]
        gb_ref[_dev_index(*me)] = pb_ref[...]

        def p1(a, r, to):
            return pltpu.make_async_remote_copy(
                src_ref=srcs[a].at[_dev_index(*rel[r], 1 - c)], dst_ref=r1s[a].at[r],
                send_sem=s1_send.at[a, r], recv_sem=s1_recv.at[a, r], device_id=to, device_id_type=MESH)

        _, _, n1, n2, diag = _relay_order()
        order = ((n1, 1 + c), (diag, 3), (n2, 2 - c), ((x, y), 0))

        def p1_landed(a, slot):
            return pltpu.make_async_remote_copy(
                src_ref=r1s[a].at[slot], dst_ref=r1s[a].at[slot], send_sem=s1_send.at[a, slot],
                recv_sem=s1_recv.at[a, slot], device_id=me, device_id_type=MESH)

        def p2(a, k, to):
            return pltpu.make_async_remote_copy(
                src_ref=sbs[a].at[k], dst_ref=r2s[a].at[k],
                send_sem=s2_send.at[a, k], recv_sem=s2_recv.at[a, k], device_id=to, device_id_type=MESH)

        def stage(a, i):
            return pltpu.make_async_copy(srcs[a].at[_dev_index(*order[i][0], c)], sts[a].at[i % 2], st_sem.at[a, i % 2])

        sends = [p1(a, r, sibling) for a in range(nb) for r in range(4)]
        for cp in sends:
            cp.start()
        _all_gather_blocks((ga_ref, gb_ref), ag_send, ag_recv)
        for a in range(nb):
            stage(a, 0).start()
            for i, (_, slot) in enumerate(order):
                if i + 1 < len(order):
                    stage(a, i + 1).start()
                stage(a, i).wait()
                p1_landed(a, slot).wait_recv()
                chip_sum = r1s[a][slot] + sts[a][i % 2]
                if i < 2:
                    sbs[a][i] = chip_sum.astype(BF16)
                    sends.append(p2(a, i, (*n1, c)))
                    sends[-1].start()
                else:
                    r1s[a][slot] = chip_sum
        for a in range(nb):
            p2(a, 0, me).wait_recv()
            p2(a, 1, me).wait_recv()
            sbs[a][2] = (r1s[a][2 - c] + r2s[a][1].astype(F32)).astype(BF16)
            sends.append(p2(a, 2, (*n2, c)))
            sends[-1].start()
        for a in range(nb):
            p2(a, 2, me).wait_recv()
            outs[a][...] = (r1s[a][0] + r2s[a][0].astype(F32)) + r2s[a][2].astype(F32)
        for cp in sends:
            cp.wait_send()

    vm = pl.BlockSpec(memory_space=pltpu.VMEM)
    hbm = pl.BlockSpec(memory_space=pl.ANY)
    blk = [p.shape[1:] for p in big]
    scratch = ([pltpu.VMEM((4,) + b, F32) for b in blk] + [pltpu.VMEM((3,) + b, BF16) for b in blk]
               + [pltpu.VMEM((3,) + b, BF16) for b in blk] + [pltpu.VMEM((2,) + b, F32) for b in blk]
               + [pltpu.SemaphoreType.DMA((nb, 4)), pltpu.SemaphoreType.DMA((nb, 4)),
                  pltpu.SemaphoreType.DMA((nb, 3)), pltpu.SemaphoreType.DMA((nb, 3)),
                  pltpu.SemaphoreType.DMA((2, 7)), pltpu.SemaphoreType.DMA((2, 7)),
                  pltpu.SemaphoreType.DMA((nb, 2))])
    return pl.pallas_call(
        body, name="reduce_grads",
        out_shape=tuple(jax.ShapeDtypeStruct(b, F32) for b in blk)
        + (jax.ShapeDtypeStruct((N_DEV,) + pack_a.shape, F32), jax.ShapeDtypeStruct((N_DEV,) + pack_b.shape, F32)),
        in_specs=[hbm] * nb + [vm, vm], out_specs=(vm,) * (nb + 2),
        scratch_shapes=scratch,
        compiler_params=_cp(),
    )(*big, pack_a, pack_b)


def _rms_hat(xf):
    r = lax.rsqrt(jnp.mean(xf * xf, axis=-1, keepdims=True) + NORM_EPS)
    return xf * r, r


def _shift_up(cur, next_head, s):
    ext = jnp.concatenate([cur, next_head], axis=0)
    n = ext.shape[0]
    return pltpu.roll(ext, n - s, 0)[:cur.shape[0], :]


def _pool_counts(i, tp, w):
    t = i * tp + lax.broadcasted_iota(jnp.int32, (tp, 1), 0)
    return jnp.minimum(t + 1, w).astype(F32)


def _pool_mix(u, u_prev_tail, i, tp):
    win = jnp.concatenate([u_prev_tail, u], axis=0)
    mixes = []
    for gi, w in enumerate(POOL_WINDOWS):
        win = win + pltpu.roll(win, w // 2, 0)
        cols = slice(gi * POOL_GROUP, (gi + 1) * POOL_GROUP)
        mixes.append(win[HALO:, :POOL_GROUP] / _pool_counts(i, tp, w) - u[:, cols])
        if gi + 1 < len(POOL_WINDOWS):
            win = win[:, POOL_GROUP:]
    return mixes


def _prev_tail(ref, i):
    return jnp.where(i > 0, ref[...], 0.0)


def _conv_taps(cur, prev_tail):
    ext = jnp.concatenate([prev_tail, cur], axis=0)
    return [cur] + [pltpu.roll(ext, sft, 0)[HALO:, :] for sft in range(1, CONV_WIDTH)]


def _conv_of_taps(taps, w4):
    y = taps[0] * w4[CONV_WIDTH - 1:CONV_WIDTH, :]
    for sft in range(1, CONV_WIDTH):
        y = y + taps[sft] * w4[CONV_WIDTH - 1 - sft:CONV_WIDTH - sft, :]
    return y


def _conv_fwd(cur, prev_tail, w4):
    ext = jnp.concatenate([prev_tail, cur], axis=0)
    y = ext * w4[CONV_WIDTH - 1:CONV_WIDTH, :]
    for sft in range(1, CONV_WIDTH):
        y = y + pltpu.roll(ext, sft, 0) * w4[CONV_WIDTH - 1 - sft:CONV_WIDTH - sft, :]
    return y[HALO:, :]


def _l2n_heads(t):
    parts = []
    for h in range(DN_HEADS):
        th = t[:, h * DN_HEAD_DIM:(h + 1) * DN_HEAD_DIM]
        parts.append(th * lax.rsqrt(jnp.sum(th * th, axis=-1, keepdims=True) + NORM_EPS))
    return jnp.concatenate(parts, axis=1)


def _post_conv(yq, yk, yv):
    return _l2n_heads(_silu(yq)), _l2n_heads(_silu(yk)), _silu(yv)


def _gates(ba, alog_lane, dtb_lane):
    lane = lax.broadcasted_iota(jnp.int32, ba.shape, 1)
    beta = jax.nn.sigmoid(ba)
    g = -jnp.exp(alog_lane) * _softplus(ba + dtb_lane)
    return jnp.where(lane < DN_HEADS, beta, jnp.where(lane < 2 * DN_HEADS, g, 0.0))


def _front(x2, norm_w, g_in, pool_w, pool_scale, conv_full, alog_lane, dtb_lane, w_out_blk, tm):
    s = x2.shape[0]

    def body(x_ref, nw_ref, pw_ref, ps_ref, cw_ref, al_ref, db_ref, wo_ref, g_hbm,
             pm_ref, pb_ref, yp_ref, qn_ref, kn_ref, vv_ref, gb_ref, gwo_hbm, wt_hbm,
             g_vmem, wt_vmem, tail_u, tail_qkv, gwo_ref, sem, wo_send, wo_recv):
        i = pl.program_id(0)

        @pl.when(i == 0)
        def _():
            gwo_ref[_dev_index(*_mesh_pos())] = wo_ref[...].astype(BF16)
            _direct_gather_start(gwo_ref, wo_send, wo_recv)
            cp = pltpu.make_async_copy(g_hbm, g_vmem, sem)
            cp.start()
            wt_vmem[D_MAIN:, :] = jnp.zeros((D_IN_PAD - D_MAIN, D_MODEL), BF16)
            tail_u[...] = jnp.zeros_like(tail_u)
            tail_qkv[...] = jnp.zeros_like(tail_qkv)
            cp.wait()
            for d in range(N_DEV):
                wt_vmem[W_IN_SHARD * d:W_IN_SHARD * (d + 1), :] = g_vmem[d]
            out = pltpu.make_async_copy(wt_vmem, wt_hbm, sem)
            out.start()
            out.wait()
        xhat, _ = _rms_hat(x_ref[...])
        n = (xhat * nw_ref[...]).astype(BF16)
        pm_ref[...] = _dot_nt_bf(n, wt_vmem[:D_MAIN, :])
        pb = _dot_nt_bf(n, wt_vmem[D_MAIN:, :])
        pb_ref[...] = pb
        u = pm_ref[:, :D_POOL]
        mixes = _pool_mix(u, tail_u[...], i, tm)
        tail_u[...] = u[tm - HALO:, :]
        gate = ps_ref[...] * _silu(pm_ref[:, D_POOL:2 * D_POOL])
        for gi in range(4):
            cols = slice(gi * POOL_GROUP, (gi + 1) * POOL_GROUP)
            yp_ref[:, cols] = _dot_bf(mixes[gi], pw_ref[gi]) * gate[:, cols]
        ys = []
        for c in range(3):
            cols = slice(c * D_DN, (c + 1) * D_DN)
            cur = pm_ref[:, 2 * D_POOL + c * D_DN:2 * D_POOL + (c + 1) * D_DN]
            ys.append(_conv_fwd(cur, tail_qkv[:, cols], cw_ref[:, cols]))
            tail_qkv[:, cols] = cur[tm - HALO:, :]
        qn, kn, vv = _post_conv(*ys)
        qn_ref[...] = qn
        kn_ref[...] = kn
        vv_ref[...] = vv
        gb_ref[...] = _gates(pb, al_ref[...], db_ref[...])

        @pl.when(i == s // tm - 1)
        def _():
            _direct_gather_wait(gwo_ref, wo_send, wo_recv)
            out = pltpu.make_async_copy(gwo_ref, gwo_hbm, sem)
            out.start()
            out.wait()

    tile = pl.BlockSpec((tm, D_DN), lambda i: (i, 0))
    lanes = pl.BlockSpec((tm, 128), lambda i: (i, 0))
    row = pl.BlockSpec((1, 128), lambda i: (0, 0))
    return pl.pallas_call(
        body, name="front", grid=(s // tm,),
        out_shape=(jax.ShapeDtypeStruct((s, D_MAIN), F32), jax.ShapeDtypeStruct((s, 128), F32),
                   jax.ShapeDtypeStruct((s, D_POOL), F32), jax.ShapeDtypeStruct((s, D_DN), F32),
                   jax.ShapeDtypeStruct((s, D_DN), F32), jax.ShapeDtypeStruct((s, D_DN), F32),
                   jax.ShapeDtypeStruct((s, 128), F32), jax.ShapeDtypeStruct((N_DEV,) + w_out_blk.shape, BF16),
                   jax.ShapeDtypeStruct((D_IN_PAD, D_MODEL), BF16)),
        in_specs=[pl.BlockSpec((tm, D_MODEL), lambda i: (i, 0)),
                  pl.BlockSpec((1, D_MODEL), lambda i: (0, 0)),
                  pl.BlockSpec((4, POOL_GROUP, POOL_GROUP), lambda i: (0, 0, 0)),
                  pl.BlockSpec((1, D_POOL), lambda i: (0, 0)),
                  pl.BlockSpec((CONV_WIDTH, 3 * D_DN), lambda i: (0, 0)), row, row,
                  pl.BlockSpec(memory_space=pltpu.VMEM), pl.BlockSpec(memory_space=pl.ANY)],
        out_specs=(pl.BlockSpec((tm, D_MAIN), lambda i: (i, 0)), lanes, tile, tile, tile, tile, lanes,
                   pl.BlockSpec(memory_space=pl.ANY), pl.BlockSpec(memory_space=pl.ANY)),
        scratch_shapes=[pltpu.VMEM((N_DEV, W_IN_SHARD, D_MODEL), BF16), pltpu.VMEM((D_IN_PAD, D_MODEL), BF16),
                        pltpu.VMEM((HALO, D_POOL), F32), pltpu.VMEM((HALO, 3 * D_DN), F32),
                        pltpu.VMEM((N_DEV,) + w_out_blk.shape, BF16),
                        pltpu.SemaphoreType.DMA, pltpu.SemaphoreType.DMA((7,)), pltpu.SemaphoreType.DMA((7,))],
        compiler_params=_cp(("arbitrary",)),
    )(x2, norm_w, pool_w, pool_scale, conv_full, alog_lane, dtb_lane, w_out_blk, g_in)


def _dn_block(q, k, v, gcol, bcol, state, known=None):
    nb, n, d = q.shape
    ii = lax.broadcasted_iota(jnp.int32, (n, n), 0)
    jj = lax.broadcasted_iota(jnp.int32, (n, n), 1)
    lower = ii >= jj
    eye = (ii == jj).astype(F32)
    g_row = jnp.sum(eye * gcol, axis=1, keepdims=True)
    gc_col = jnp.sum(jnp.where(lower, g_row, 0.0), axis=2, keepdims=True)
    gc_row = jnp.sum(eye * gc_col, axis=1, keepdims=True)
    decay = jnp.where(lower, jnp.exp(jnp.where(lower, gc_col - gc_row, 0.0)), 0.0)
    kb = k * bcol
    vb = v * bcol
    qs = q * (DN_HEAD_DIM ** -0.5)
    egc = jnp.exp(gc_col)
    kq = jnp.concatenate([kb, qs], axis=1)
    vk = jnp.concatenate([vb, kb * egc], axis=2)
    if known is None:
        akq = _mm(kq, k, 1, 1, *_DN_PREC["akq"])
    else:
        akq = _mm_known(kq, k, known[0][:, :, :n].astype(F32), 1, 1, _DN_PREC["akq"][1])
    a = jnp.where(ii > jj, akq[:, :n] * decay, 0.0)
    qk = akq[:, n:] * decay
    if known is None:
        t = _tri_inv(a, *_DN_PREC["inv"])
        uw = _mm(t, vk, 1, 0, *_DN_PREC["uw"])
    else:
        t = _tri_inv_known(a, known[0][:, :n, n:].astype(F32), _DN_PREC["inv"][1])
        uw = _mm_known(t, vk, known[1], 1, 0, _DN_PREC["uw"][1])
    pre = jnp.concatenate([akq, jnp.concatenate([t, jnp.zeros_like(t)], axis=1)], axis=2)
    wq = jnp.concatenate([uw[:, :, d:], qs * egc], axis=1)
    g_last = gc_col[:, n - 1:n, :]
    k_dec = k * jnp.exp(g_last - gc_col)
    e_last = jnp.exp(g_last)
    os_, starts = [], []
    for c in range(nb // DN_HEADS):
        sl = slice(c * DN_HEADS, (c + 1) * DN_HEADS)
        if known is not None and c > 0:
            state = _use_known(state, known[2][sl])
        starts.append(state)
        ws = _mm(wq[sl], state, 1, 0, *_DN_PREC["ws"])
        v_new = uw[sl, :, :d] - ws[:, :n]
        os_.append(ws[:, n:] + _mm(qk[sl], v_new, 1, 0, *_DN_PREC["ov"]))
        state = state * e_last[sl] + _mm(k_dec[sl], v_new, 0, 0, *_DN_PREC["st"])
    return jnp.concatenate(os_, axis=0), state, (pre, uw, jnp.concatenate(starts, axis=0))


def _gated_norm(o, dz, nw):
    parts = []
    for h in range(DN_HEADS):
        oh = o[:, h * DN_HEAD_DIM:(h + 1) * DN_HEAD_DIM]
        parts.append(oh * lax.rsqrt(jnp.mean(oh * oh, axis=-1, keepdims=True) + NORM_EPS) * nw)
    return jnp.concatenate(parts, axis=1) * _silu(dz)


def _dn_block_args(gc, q_ref, k_ref, v_ref, gb_ref):
    qs, ks, vs, gs, bs = [], [], [], [], []
    for cc in range(gc):
        r = slice(cc * CHUNK, (cc + 1) * CHUNK)
        gbv = gb_ref[r, :]
        for h in range(DN_HEADS):
            cols = slice(h * DN_HEAD_DIM, (h + 1) * DN_HEAD_DIM)
            qs.append(q_ref[r, cols])
            ks.append(k_ref[r, cols])
            vs.append(v_ref[r, cols])
            gs.append(gbv[:, DN_HEADS + h:DN_HEADS + h + 1])
            bs.append(gbv[:, h:h + 1])
    return tuple(jnp.stack(t, axis=0) for t in (qs, ks, vs, gs, bs))


def _dn_scan_fwd(qn, kn, vv, gb, gc):
    s = qn.shape[0]
    nchunk = s // CHUNK
    rows = gc * CHUNK

    def body(q_ref, k_ref, v_ref, gb_ref, y_ref, ss_ref, pre_ref, uw_ref, state):
        @pl.when(pl.program_id(0) == 0)
        def _():
            state[...] = jnp.zeros_like(state)
        q, k, v, gcol, bcol = _dn_block_args(gc, q_ref, k_ref, v_ref, gb_ref)
        y, new, (pre, uw, starts) = _dn_block(q, k, v, gcol, bcol, state[...])
        state[...] = new
        ss_ref[...] = starts
        pre_ref[...] = pre
        uw_ref[...] = uw
        for cc in range(gc):
            for h in range(DN_HEADS):
                y_ref[cc * CHUNK:(cc + 1) * CHUNK, h * DN_HEAD_DIM:(h + 1) * DN_HEAD_DIM] = y[cc * DN_HEADS + h]

    tile = pl.BlockSpec((rows, D_DN), lambda i: (i, 0))
    return pl.pallas_call(
        body, name="dn_scan_fwd", grid=(nchunk // gc,),
        out_shape=(jax.ShapeDtypeStruct((s, D_DN), F32),
                   jax.ShapeDtypeStruct((nchunk * DN_HEADS, DN_HEAD_DIM, DN_HEAD_DIM), F32),
                   jax.ShapeDtypeStruct((nchunk * DN_HEADS, 2 * CHUNK, 2 * CHUNK), F32),
                   jax.ShapeDtypeStruct((nchunk * DN_HEADS, CHUNK, 2 * DN_HEAD_DIM), F32)),
        in_specs=[tile, tile, tile, pl.BlockSpec((rows, 128), lambda i: (i, 0))],
        out_specs=(tile, pl.BlockSpec((gc * DN_HEADS, DN_HEAD_DIM, DN_HEAD_DIM), lambda i: (i, 0, 0)),
                   pl.BlockSpec((gc * DN_HEADS, 2 * CHUNK, 2 * CHUNK), lambda i: (i, 0, 0)),
                   pl.BlockSpec((gc * DN_HEADS, CHUNK, 2 * DN_HEAD_DIM), lambda i: (i, 0, 0))),
        scratch_shapes=[pltpu.VMEM((DN_HEADS, DN_HEAD_DIM, DN_HEAD_DIM), F32)],
        compiler_params=_cp(("arbitrary",)),
    )(qn, kn, vv, gb)


def _out_proj_loss(y_pool, o_dn, proj_main, dn_norm_w, x2, tgt, w_out_full, fnw, tm):
    s = x2.shape[0]

    def body(yp_ref, o_ref, dz_ref, nw_ref, x_ref, t_ref, wo_ref, fw_ref,
             dh_ref, dyp_ref, do_ref, ddz_ref, gwo_ref, gfw_ref, loss_ref, dnw_ref):
        @pl.when(pl.program_id(0) == 0)
        def _():
            gwo_ref[...] = jnp.zeros_like(gwo_ref)
            gfw_ref[...] = jnp.zeros_like(gfw_ref)
            loss_ref[...] = jnp.zeros_like(loss_ref)
            dnw_ref[...] = jnp.zeros_like(dnw_ref)
        y_dn, gate_vjp = jax.vjp(_gated_norm, o_ref[...], dz_ref[...], nw_ref[...])
        y = jnp.concatenate([yp_ref[...], y_dn], axis=1).astype(BF16)
        wo = wo_ref[...]
        h = x_ref[...] + jnp.dot(y, wo, preferred_element_type=F32)
        hn, r = _rms_hat(h)
        fw = fw_ref[...]
        err = hn * fw - t_ref[...]
        loss_ref[...] += 0.5 * jnp.sum(jnp.sum(err * err, axis=-1, keepdims=True) / D_MODEL, axis=0, keepdims=True)
        dout = err / D_MODEL
        gfw_ref[...] += jnp.sum(dout * hn, axis=0, keepdims=True)
        dhn = dout * fw
        dh = r * (dhn - hn * jnp.mean(dhn * hn, axis=-1, keepdims=True))
        dh_ref[...] = dh
        dhb = dh.astype(BF16)
        dy = _dot_nt_bf(dhb, wo)
        dyp_ref[...] = dy[:, :D_POOL]
        do, ddz, dnw = gate_vjp(dy[:, D_POOL:])
        do_ref[...] = do
        ddz_ref[...] = ddz
        dnw_ref[...] += dnw
        gwo_ref[...] += _dot_tn_bf(y, dhb)

    half = pl.BlockSpec((tm, D_POOL), lambda i: (i, 0))
    full = pl.BlockSpec((tm, D_MODEL), lambda i: (i, 0))
    lrow = pl.BlockSpec((1, 128), lambda i: (0, 0))
    return pl.pallas_call(
        body, name="out_proj_loss", grid=(s // tm,),
        out_shape=(jax.ShapeDtypeStruct((s, D_MODEL), F32), jax.ShapeDtypeStruct((s, D_POOL), F32),
                   jax.ShapeDtypeStruct((s, D_DN), F32), jax.ShapeDtypeStruct((s, D_DN), F32),
                   jax.ShapeDtypeStruct((D_MODEL, D_MODEL), F32),
                   jax.ShapeDtypeStruct((1, D_MODEL), F32), jax.ShapeDtypeStruct((1, 128), F32),
                   jax.ShapeDtypeStruct((1, 128), F32)),
        in_specs=[half, half, pl.BlockSpec((tm, D_DN), lambda i: (i, 5)), lrow, full, full,
                  pl.BlockSpec((D_MODEL, D_MODEL), lambda i: (0, 0)), pl.BlockSpec((1, D_MODEL), lambda i: (0, 0))],
        out_specs=(full, half, half, half, pl.BlockSpec((D_MODEL, D_MODEL), lambda i: (0, 0)),
                   pl.BlockSpec((1, D_MODEL), lambda i: (0, 0)), lrow, lrow),
        compiler_params=_cp(("arbitrary",)),
    )(y_pool, o_dn, proj_main, dn_norm_w, x2, tgt, w_out_full, fnw)


def _dn_scan_bwd(qn, kn, vv, gb, states, pre, uw, do_dn, p_out, gc):
    s = qn.shape[0]
    nchunk = s // CHUNK
    nstep = nchunk // gc
    rows = gc * CHUNK

    def body(q_ref, k_ref, v_ref, gb_ref, ss_ref, pre_ref, uw_ref, dy_ref, po_ref,
             dq_ref, dk_ref, dv_ref, dgb_ref, gro_ref, dstate, po_send, po_recv, rs_send, rs_recv):
        @pl.when(pl.program_id(0) == 0)
        def _():
            dstate[...] = jnp.zeros_like(dstate)
            po_send[...] = po_ref[...].astype(BF16)
            _direct_scatter_start(po_send, po_recv, rs_send, rs_recv)

        @pl.when(pl.program_id(0) == nstep - 1)
        def _():
            _direct_scatter_wait(po_send, po_recv, rs_send, rs_recv)
            total = po_ref[_dev_index(*_mesh_pos())]
            for k in range(N_DEV - 1):
                total = total + po_recv[k].astype(F32)
            gro_ref[...] = total
        lane = lax.broadcasted_iota(jnp.int32, (CHUNK, 128), 1)
        q, k, v, gcol, bcol = _dn_block_args(gc, q_ref, k_ref, v_ref, gb_ref)
        dy = jnp.stack([dy_ref[cc * CHUNK:(cc + 1) * CHUNK, h * DN_HEAD_DIM:(h + 1) * DN_HEAD_DIM]
                        for cc in range(gc) for h in range(DN_HEADS)], axis=0)
        known = (pre_ref[...], uw_ref[...], ss_ref[...])
        _, vjp = jax.vjp(lambda *a: _dn_block(*a, known=known)[:2], q, k, v, gcol, bcol, ss_ref[:DN_HEADS])
        dq, dk, dv, dg, db, dst = vjp((dy, dstate[...]))
        dstate[...] = dst
        for cc in range(gc):
            r = slice(cc * CHUNK, (cc + 1) * CHUNK)
            dgb = jnp.zeros((CHUNK, 128), F32)
            for h in range(DN_HEADS):
                b = cc * DN_HEADS + h
                cols = slice(h * DN_HEAD_DIM, (h + 1) * DN_HEAD_DIM)
                for ref, val in zip((dq_ref, dk_ref, dv_ref), (dq, dk, dv)):
                    ref[r, cols] = val[b]
                dgb = dgb + jnp.where(lane == h, db[b], 0.0) + jnp.where(lane == DN_HEADS + h, dg[b], 0.0)
            dgb_ref[r, :] = dgb

    rev = lambda i: (nstep - 1 - i, 0)
    tile = pl.BlockSpec((rows, D_DN), rev)
    lanes = pl.BlockSpec((rows, 128), rev)
    return pl.pallas_call(
        body, name="dn_scan_bwd", grid=(nstep,),
        out_shape=(jax.ShapeDtypeStruct((s, D_DN), F32),) * 3
        + (jax.ShapeDtypeStruct((s, 128), F32), jax.ShapeDtypeStruct(p_out.shape[1:], F32)),
        in_specs=[tile, tile, tile, lanes,
                  pl.BlockSpec((gc * DN_HEADS, DN_HEAD_DIM, DN_HEAD_DIM), lambda i: (nstep - 1 - i, 0, 0)),
                  pl.BlockSpec((gc * DN_HEADS, 2 * CHUNK, 2 * CHUNK), lambda i: (nstep - 1 - i, 0, 0)),
                  pl.BlockSpec((gc * DN_HEADS, CHUNK, 2 * DN_HEAD_DIM), lambda i: (nstep - 1 - i, 0, 0)), tile,
                  pl.BlockSpec(memory_space=pltpu.VMEM)],
        out_specs=(tile, tile, tile, lanes, pl.BlockSpec(memory_space=pltpu.VMEM)),
        scratch_shapes=[pltpu.VMEM((DN_HEADS, DN_HEAD_DIM, DN_HEAD_DIM), F32),
                        pltpu.VMEM(p_out.shape, BF16), pltpu.VMEM((N_DEV - 1,) + p_out.shape[1:], BF16),
                        pltpu.SemaphoreType.DMA((7,)), pltpu.SemaphoreType.DMA((7,))],
        compiler_params=_cp(("arbitrary",)),
    )(qn, kn, vv, gb, states, pre, uw, do_dn, p_out)


def _back(proj_main, proj_ba, dyp, dqn, dkn, dvv, dgb, ddz, x2, dh, norm_w, pool_w, pool_scale, conv_full,
          alog_lane, dtb_lane, wt_full, tm):
    s = x2.shape[0]
    nstep = s // tm
    per = tm // HALO

    def body(u_ref, z_ref, q_ref, k_ref, v_ref, up_ref, qp_ref, kp_ref, vp_ref, ba_ref,
             dyp_ref, dqn_ref, dkn_ref, dvv_ref, dgb_ref, ddz_ref, x_ref, dh_ref,
             nw_ref, pw_ref, ps_ref, cw_ref, al_ref, db_ref, wt_hbm,
             gx_ref, p_hbm, gnw_ref, dpw_ref, dps_ref, dcw_ref, dal_ref, ddb_ref,
             wt_vmem, acc, blk, head_dc, head_dw, sem, osem):
        j = pl.program_id(0)
        i = nstep - 1 - j

        @pl.when(j == 0)
        def _():
            cp = pltpu.make_async_copy(wt_hbm, wt_vmem, sem)
            cp.start()
            acc[...] = jnp.zeros_like(acc)
            for ref in (gnw_ref, dpw_ref, dps_ref, dcw_ref, dal_ref, ddb_ref, head_dc, head_dw):
                ref[...] = jnp.zeros_like(ref)
            cp.wait()

        u = u_ref[...]
        z = z_ref[...]
        dy = dyp_ref[...]
        ps = ps_ref[...]
        mixes = _pool_mix(u, _prev_tail(up_ref, i), i, tm)
        sg = jax.nn.sigmoid(z)
        sz = z * sg
        dsz = sg * (1.0 + z * (1.0 - sg))
        dzs, dwins = [], []
        for gi, w in enumerate(POOL_WINDOWS):
            cols = slice(gi * POOL_GROUP, (gi + 1) * POOL_GROUP)
            mixw = _dot_bf(mixes[gi], pw_ref[gi])
            dmixw = dy[:, cols] * ps[:, cols] * sz[:, cols]
            dps_ref[:, cols] += jnp.sum(dy[:, cols] * mixw * sz[:, cols], axis=0, keepdims=True)
            dzs.append(dy[:, cols] * mixw * ps[:, cols] * dsz[:, cols])
            dpw_ref[gi] += _dot_tn_bf(mixes[gi], dmixw)
            dwins.append(_dot_nt_bf(dmixw, pw_ref[gi]) / _pool_counts(i, tm, w))
        dzp = jnp.concatenate(dzs, axis=1)
        dw = jnp.concatenate(dwins, axis=1)
        win = jnp.concatenate([dw, head_dw[...]], axis=0)
        m = win.shape[0]
        dups = []
        for gi, w in enumerate(POOL_WINDOWS):
            win = win + pltpu.roll(win, m - w // 2, 0)
            cols = slice(gi * POOL_GROUP, (gi + 1) * POOL_GROUP)
            dups.append(win[:tm, :POOL_GROUP] - dw[:, cols] * _pool_counts(i, tm, w))
            if gi + 1 < len(POOL_WINDOWS):
                win = win[:, POOL_GROUP:]
        dup = jnp.concatenate(dups, axis=1)
        head_dw[...] = dw[:HALO, :]

        curs = (q_ref[...], k_ref[...], v_ref[...])
        tails = (_prev_tail(qp_ref, i), _prev_tail(kp_ref, i), _prev_tail(vp_ref, i))
        taps = [_conv_taps(curs[c], tails[c]) for c in range(3)]
        ys = [_conv_of_taps(taps[c], cw_ref[:, c * D_DN:(c + 1) * D_DN]) for c in range(3)]
        _, vjp = jax.vjp(_post_conv, *ys)
        dys = vjp((dqn_ref[...], dkn_ref[...], dvv_ref[...]))
        dxs = []
        for c, dyc in enumerate(dys):
            cols = slice(c * D_DN, (c + 1) * D_DN)
            w4 = cw_ref[:, cols]
            for sft in range(CONV_WIDTH):
                row = CONV_WIDTH - 1 - sft
                dcw_ref[row:row + 1, cols] += jnp.sum(dyc * taps[c][sft], axis=0, keepdims=True)
            head = head_dc[:, cols]
            dx = dyc * w4[CONV_WIDTH - 1:CONV_WIDTH, :]
            for sft in range(1, CONV_WIDTH):
                dx = dx + _shift_up(dyc, head, sft) * w4[CONV_WIDTH - 1 - sft:CONV_WIDTH - sft, :]
            dxs.append(dx)
            head_dc[:, cols] = dyc[:HALO, :]
        _, gvjp = jax.vjp(_gates, ba_ref[...], al_ref[...], db_ref[...])
        dba, dal, ddb = gvjp(dgb_ref[...])
        dal_ref[...] += dal
        ddb_ref[...] += ddb

        dbab = dba.astype(BF16)
        xhat, r = _rms_hat(x_ref[...])
        nw = nw_ref[...]
        n = (xhat * nw).astype(BF16)
        acc[D_MAIN:, :] += _dot_tn_bf(dbab, n)
        dn = jnp.dot(dbab, wt_vmem[D_MAIN:, :], preferred_element_type=F32)
        for cb, d in enumerate((dup, dzp, dxs[0], dxs[1], dxs[2], ddz_ref[...])):
            rows = slice(cb * D_POOL, (cb + 1) * D_POOL)
            dpart = d.astype(BF16)
            acc[rows, :] += _dot_tn_bf(dpart, n)
            dn = dn + jnp.dot(dpart, wt_vmem[rows, :], preferred_element_type=F32)
        gnw_ref[...] += jnp.sum(dn * xhat, axis=0, keepdims=True)
        dxh = dn * nw
        gx_ref[...] = dh_ref[...] + r * (dxh - xhat * jnp.mean(dxh * xhat, axis=-1, keepdims=True))

        @pl.when(j == nstep - 1)
        def _():
            def out(d):
                return pltpu.make_async_copy(blk.at[d % 2], p_hbm.at[d], osem.at[d % 2])
            for d in range(N_DEV):
                if d >= 2:
                    out(d - 2).wait()
                blk[d % 2] = acc[W_IN_SHARD * d:W_IN_SHARD * (d + 1), :]
                out(d).start()
            out(N_DEV - 2).wait()
            out(N_DEV - 1).wait()

    def col(c):
        return pl.BlockSpec((tm, D_POOL), lambda j: (nstep - 1 - j, c))

    def halo(c):
        return pl.BlockSpec((HALO, D_POOL), lambda j: (jnp.maximum((nstep - 1 - j) * per - 1, 0), c))

    rev = lambda j: (nstep - 1 - j, 0)
    part = pl.BlockSpec((tm, D_POOL), rev)
    lanes = pl.BlockSpec((tm, 128), rev)
    full = pl.BlockSpec((tm, D_MODEL), rev)
    row = pl.BlockSpec((1, D_MODEL), lambda j: (0, 0))
    lrow = pl.BlockSpec((1, 128), lambda j: (0, 0))
    pw = pl.BlockSpec((4, POOL_GROUP, POOL_GROUP), lambda j: (0, 0, 0))
    psp = pl.BlockSpec((1, D_POOL), lambda j: (0, 0))
    cw = pl.BlockSpec((CONV_WIDTH, 3 * D_DN), lambda j: (0, 0))
    return pl.pallas_call(
        body, name="back", grid=(nstep,),
        out_shape=(jax.ShapeDtypeStruct((s, D_MODEL), F32),
                   jax.ShapeDtypeStruct((N_DEV, W_IN_SHARD, D_MODEL), F32), jax.ShapeDtypeStruct((1, D_MODEL), F32),
                   jax.ShapeDtypeStruct((4, POOL_GROUP, POOL_GROUP), F32), jax.ShapeDtypeStruct((1, D_POOL), F32),
                   jax.ShapeDtypeStruct((CONV_WIDTH, 3 * D_DN), F32),
                   jax.ShapeDtypeStruct((1, 128), F32), jax.ShapeDtypeStruct((1, 128), F32)),
        in_specs=[col(0), col(1), col(2), col(3), col(4), halo(0), halo(2), halo(3), halo(4), lanes,
                  part, part, part, part, lanes, part, full, full,
                  row, pw, psp, cw, lrow, lrow, pl.BlockSpec(memory_space=pl.ANY)],
        out_specs=(full, pl.BlockSpec(memory_space=pl.ANY), row, pw, psp, cw, lrow, lrow),
        scratch_shapes=[pltpu.VMEM((D_IN_PAD, D_MODEL), BF16), pltpu.VMEM((D_IN_PAD, D_MODEL), F32),
                        pltpu.VMEM((2, W_IN_SHARD, D_MODEL), F32),
                        pltpu.VMEM((HALO, 3 * D_DN), F32), pltpu.VMEM((HALO, D_POOL), F32),
                        pltpu.SemaphoreType.DMA, pltpu.SemaphoreType.DMA((2,))],
        compiler_params=_cp(("arbitrary",)),
    )(proj_main, proj_main, proj_main, proj_main, proj_main, proj_main, proj_main, proj_main, proj_main, proj_ba,
      dyp, dqn, dkn, dvv, dgb, ddz, x2, dh, norm_w, pool_w, pool_scale, conv_full, alog_lane, dtb_lane, wt_full)


def _adamw_math(w, g, m, v):
    m = ADAM_B1 * m + (1.0 - ADAM_B1) * g
    v = ADAM_B2 * v + (1.0 - ADAM_B2) * (g * g)
    m_hat = m / (1.0 - ADAM_B1 ** ADAM_STEP)
    v_hat = v / (1.0 - ADAM_B2 ** ADAM_STEP)
    delta = -ADAM_LR * (m_hat / (jnp.sqrt(v_hat) + ADAM_EPS) + ADAM_WD * w)
    return delta, m, v


def _adamw_sharded(params):
    k = len(params)

    def body(*refs):
        ins, outs = refs[:4 * k], refs[4 * k:]
        for p in range(k):
            w_ref, g_ref, m_ref, v_ref = ins[4 * p:4 * p + 4]
            go_ref = outs[4 * p]
            if g_ref.shape == w_ref.shape:
                go_ref[...] = g_ref[...]
            else:
                for j in range(FLAT_ROWS):
                    go_ref[pl.ds(j, W_IN_SHARD, stride=FLAT_ROWS), :] = g_ref[:, 128 * j:128 * (j + 1)]
            d, nm, nv = _adamw_math(w_ref[...], go_ref[...], m_ref[...], v_ref[...])
            outs[4 * p + 1][...] = d
            outs[4 * p + 2][...] = nm
            outs[4 * p + 3][...] = nv

    flat = [a for p in params for a in p]
    out_shape = tuple(jax.ShapeDtypeStruct(p[0].shape, F32) for p in params for _ in range(4))
    res = pl.pallas_call(body, name="adamw_sharded", out_shape=out_shape, compiler_params=_cp())(*flat)
    return [tuple(res[4 * p:4 * p + 4]) for p in range(k)]


def _adamw_replicated(gath_a, gath_b, pool, rows):
    nrow = len(rows)

    def body(*refs):
        ga_ref, gb_ref = refs[:2]
        ins = refs[2:2 + 3 * (nrow + 1)]
        outs = refs[2 + 3 * (nrow + 1):]

        def total(ref):
            g = ref[0]
            for d in range(1, N_DEV):
                g = g + ref[d]
            return g

        def update(g, wmv, o):
            w, m, v = (r[...] for r in wmv)
            dl, nm, nv = _adamw_math(w, g, m, v)
            o[0][...] = g
            o[1][...] = dl
            o[2][...] = nm
            o[3][...] = nv

        update(total(ga_ref), ins[:3], outs[:4])
        gb = total(gb_ref)
        for r in range(nrow):
            n = ins[3 * (r + 1)].shape[1]
            update(gb[r:r + 1, :n], ins[3 * (r + 1):3 * (r + 2)], outs[4 * (r + 1):4 * (r + 2)])
        outs[4 * (nrow + 1)][...] = gb[nrow:nrow + 1, 0:1]

    flat = list(pool) + [a for wmv in rows for a in wmv]
    out_shape = ((jax.ShapeDtypeStruct(pool[0].shape, F32),) * 4
                 + tuple(jax.ShapeDtypeStruct(wmv[0].shape, F32) for wmv in rows for _ in range(4))
                 + (jax.ShapeDtypeStruct((1, 1), F32),))
    res = pl.pallas_call(body, name="adamw_replicated", out_shape=out_shape, compiler_params=_cp())(
        gath_a, gath_b, *flat)
    return [res[4 * k:4 * k + 4] for k in range(nrow + 1)], res[-1]


_ROW_ORDER = ("norm_w", "final_norm_w", "pool_scale", "dn_norm_w", "a_log", "dt_bias")


def _pack_rows(vectors):
    out = [jnp.pad(v.reshape(-1), (0, D_MODEL - v.size)) for v in vectors]
    out += [jnp.zeros((D_MODEL,), F32)] * (8 - len(out))
    return jnp.stack(out, axis=0)


def _lane_row(vec4, start):
    return jnp.pad(vec4.reshape(-1), (start, 128 - start - vec4.size)).reshape(1, 128)


def kernel(x, norm_w, w_in, pool_w, pool_scale, conv_w, a_log, dt_bias, dn_norm_w, w_out, final_norm_w, loss_target, m_norm_w, m_w_in, m_pool_w, m_pool_scale, m_conv_w, m_a_log, m_dt_bias, m_dn_norm_w, m_w_out, m_final_norm_w, v_norm_w, v_w_in, v_pool_w, v_pool_scale, v_conv_w, v_a_log, v_dt_bias, v_dn_norm_w, v_w_out, v_final_norm_w):
    s = x.shape[1]
    tm = min(512, s)
    tmb = min(256, s)
    x2 = x[0]
    tgt = loss_target[0]
    def to_flat(a):
        return a[0].reshape(FLAT_ROWS, 128, W_IN_SHARD).transpose(2, 0, 1).reshape(W_IN_SHARD * FLAT_ROWS, 128)

    def from_flat(f):
        return f.reshape(W_IN_SHARD, FLAT_ROWS, 128).transpose(1, 2, 0).reshape(1, D_MODEL, W_IN_SHARD)

    wf, m_wf, v_wf = to_flat(w_in), to_flat(m_w_in), to_flat(v_w_in)

    g_in, g_conv = _gather_weights(wf, conv_w[0])
    conv_full = g_conv.transpose(1, 0, 2).reshape(CONV_WIDTH, 3 * D_DN)
    alog_lane = _lane_row(a_log, DN_HEADS)
    dtb_lane = _lane_row(dt_bias, DN_HEADS)
    fnw = final_norm_w.reshape(1, D_MODEL)

    proj_main, proj_ba, y_pool, qn, kn, vv, gb, g_out, wt_full = _front(
        x2, norm_w, g_in, pool_w[0], pool_scale, conv_full, alog_lane, dtb_lane, w_out[0], tm)
    w_out_full = g_out.reshape(D_MODEL, D_MODEL)
    o_dn, states, dn_pre, dn_uw = _dn_scan_fwd(qn, kn, vv, gb, DN_CHUNKS_PER_STEP)

    dh, dyp, do_dn, ddz, g_wout, g_fnw, loss_part, g_dnw = _out_proj_loss(
        y_pool, o_dn, proj_main, dn_norm_w, x2, tgt, w_out_full, fnw, tm)
    p_out = g_wout.reshape(N_DEV, D_MODEL // N_DEV, D_MODEL)
    dqn, dkn, dvv, dgb, gr_out = _dn_scan_bwd(qn, kn, vv, gb, states, dn_pre, dn_uw, do_dn, p_out, DN_CHUNKS_PER_STEP)
    grad_x2, p_in, g_nw, g_pw, g_ps, g_conv_full, g_al, g_db = _back(
        proj_main, proj_ba, dyp, dqn, dkn, dvv, dgb, ddz, x2, dh, norm_w, pool_w[0], pool_scale, conv_full,
        alog_lane, dtb_lane, wt_full, tmb)

    p_conv = g_conv_full.reshape(CONV_WIDTH, N_DEV, 3 * D_DN // N_DEV).transpose(1, 0, 2)
    pack_a = g_pw.reshape(4 * POOL_GROUP, POOL_GROUP)
    pack_b = _pack_rows([g_nw, g_fnw, g_ps, g_dnw, g_al[0, DN_HEADS:2 * DN_HEADS], g_db[0, DN_HEADS:2 * DN_HEADS],
                         loss_part[0, :1]])
    gr_in, gr_conv, gath_a, gath_b = _reduce_grads((p_in, p_conv), pack_a, pack_b)

    r_in, r_out, r_conv = _adamw_sharded([(wf, gr_in, m_wf, v_wf), (w_out[0], gr_out, m_w_out[0], v_w_out[0]),
                                          (conv_w[0], gr_conv, m_conv_w[0], v_conv_w[0])])
    flat = lambda a: a.reshape(4 * POOL_GROUP, POOL_GROUP)
    row = lambda a: a.reshape(1, -1)
    vecs = {"norm_w": (norm_w, m_norm_w, v_norm_w), "final_norm_w": (final_norm_w, m_final_norm_w, v_final_norm_w),
            "pool_scale": (pool_scale, m_pool_scale, v_pool_scale), "dn_norm_w": (dn_norm_w, m_dn_norm_w, v_dn_norm_w),
            "a_log": (a_log, m_a_log, v_a_log), "dt_bias": (dt_bias, m_dt_bias, v_dt_bias)}
    res, loss = _adamw_replicated(gath_a, gath_b, (flat(pool_w), flat(m_pool_w), flat(v_pool_w)),
                                  [tuple(row(a) for a in vecs[nm]) for nm in _ROW_ORDER])
    r_pool = res[0]
    r_vec = dict(zip(_ROW_ORDER, res[1:]))

    def group(k):
        vec = lambda nm: r_vec[nm][k].reshape(vecs[nm][0].shape)
        return (vec("norm_w"), from_flat(r_in[k]), r_pool[k].reshape(pool_w.shape), vec("pool_scale"), r_conv[k][None],
                vec("a_log"), vec("dt_bias"), vec("dn_norm_w"), r_out[k][None], vec("final_norm_w"))

    return (loss[0, 0], grad_x2[None], *group(0), *group(1), *group(2), *group(3))
```

```python
import functools

import jax
import jax.numpy as jnp
from jax import lax
from jax.experimental import pallas as pl
from jax.experimental.pallas import tpu as pltpu

F32 = jnp.float32
BF16 = jnp.bfloat16
HI = lax.Precision.HIGHEST
MESH = pl.DeviceIdType.MESH

D_MODEL = 1024
D_POOL = 512
D_DN = 512
POOL_WINDOWS = (2, 4, 8, 16)
POOL_GROUP = 128
DN_HEADS = 4
DN_HEAD_DIM = 128
CONV_WIDTH = 4
CHUNK = 64
NORM_EPS = 1e-6
D_IN = 3080
D_MAIN = 3072
FLAT_ROWS = D_MODEL // 128
D_IN_PAD = D_MAIN + 128
N_DEV = 8
W_IN_SHARD = D_IN // N_DEV
HALO = 16
DN_CHUNKS_PER_STEP = 8

ADAM_LR = 0.001
ADAM_B1 = 0.9
ADAM_B2 = 0.999
ADAM_EPS = 1e-08
ADAM_WD = 0.01
ADAM_STEP = 10

VMEM_LIMIT = 56 * 1024 * 1024
def _cp(sem=None, vmem=VMEM_LIMIT):
    kw = {"vmem_limit_bytes": vmem}
    if sem is not None:
        kw["dimension_semantics"] = sem
    return pltpu.CompilerParams(**kw)


def _dot_bf(a, b):
    return jnp.dot(a.astype(BF16), b.astype(BF16), preferred_element_type=F32)


def _dot_nt_bf(a, b):
    return lax.dot_general(a.astype(BF16), b.astype(BF16), (((1,), (1,)), ((), ())), preferred_element_type=F32)


def _dot_tn_bf(a, b):
    return lax.dot_general(a.astype(BF16), b.astype(BF16), (((0,), (0,)), ((), ())), preferred_element_type=F32)


def _mm_raw(a, b, ca, cb, prec):
    off = a.ndim - 2
    dn = (((ca + off,), (cb + off,)), ((0,), (0,)) if off else ((), ()))
    if prec == "hi":
        return lax.dot_general(a, b, dn, precision=HI, preferred_element_type=F32)
    ah, bh = a.astype(BF16), b.astype(BF16)
    out = lax.dot_general(ah, bh, dn, preferred_element_type=F32)
    if prec == "x3":
        al = (a - ah.astype(F32)).astype(BF16)
        bl = (b - bh.astype(F32)).astype(BF16)
        out = out + lax.dot_general(ah, bl, dn, preferred_element_type=F32)
        out = out + lax.dot_general(al, bh, dn, preferred_element_type=F32)
    return out


@functools.partial(jax.custom_vjp, nondiff_argnums=(2, 3, 4, 5))
def _mm(a, b, ca, cb, prec, bprec):
    return _mm_raw(a, b, ca, cb, prec)


def _mm_fwd(a, b, ca, cb, prec, bprec):
    return _mm_raw(a, b, ca, cb, prec), (a, b)


def _mm_bwd(ca, cb, prec, bprec, res, dc):
    a, b = res
    da = _mm_raw(dc, b, 1, 1 - cb, bprec) if ca == 1 else _mm_raw(b, dc, 1 - cb, 1, bprec)
    db = _mm_raw(a, dc, 1 - ca, 0, bprec) if cb == 0 else _mm_raw(dc, a, 0, 1 - ca, bprec)
    return da, db


_mm.defvjp(_mm_fwd, _mm_bwd)


@functools.partial(jax.custom_vjp, nondiff_argnums=(1, 2))
def _tri_inv(a, prec, bprec):
    n = a.shape[-1]
    ii = lax.broadcasted_iota(jnp.int32, (n, n), 0)
    jj = lax.broadcasted_iota(jnp.int32, (n, n), 1)
    p = (ii == jj).astype(F32) - a
    b = _mm_raw(a, a, 1, 0, prec)
    for _ in range(4):
        pb = _mm_raw(jnp.concatenate([p, b], axis=-2), b, 1, 0, prec)
        p = p + pb[..., :n, :]
        b = pb[..., n:, :]
    return p + _mm_raw(p, b, 1, 0, prec)


def _tri_inv_fwd(a, prec, bprec):
    t = _tri_inv(a, prec, bprec)
    return t, t


def _tri_inv_bwd(prec, bprec, t, dt):
    return (-_mm_raw(_mm_raw(t, dt, 0, 0, bprec), t, 1, 1, bprec),)


_tri_inv.defvjp(_tri_inv_fwd, _tri_inv_bwd)

@functools.partial(jax.custom_vjp, nondiff_argnums=(3, 4, 5))
def _mm_known(a, b, out, ca, cb, bprec):
    return out


def _mm_known_fwd(a, b, out, ca, cb, bprec):
    return out, (a, b)


def _mm_known_bwd(ca, cb, bprec, res, dc):
    return _mm_bwd(ca, cb, None, bprec, res, dc) + (jnp.zeros_like(dc),)


_mm_known.defvjp(_mm_known_fwd, _mm_known_bwd)


@jax.custom_vjp
def _use_known(x, known):
    return known


_use_known.defvjp(lambda x, known: (known, None), lambda _, g: (g, jnp.zeros_like(g)))


@functools.partial(jax.custom_vjp, nondiff_argnums=(2,))
def _tri_inv_known(a, t, bprec):
    return t


def _tri_inv_known_fwd(a, t, bprec):
    return t, t


def _tri_inv_known_bwd(bprec, t, dt):
    return _tri_inv_bwd(None, bprec, t, dt) + (jnp.zeros_like(dt),)


_tri_inv_known.defvjp(_tri_inv_known_fwd, _tri_inv_known_bwd)

_DN_PREC = {"akq": ("bf16", "bf16"), "inv": ("bf16", "bf16"), "uw": ("bf16", "bf16"), "ws": ("bf16", "bf16"),
            "ov": ("bf16", "bf16"), "st": ("bf16", "bf16")}


def _silu(x):
    return x * jax.nn.sigmoid(x)


def _softplus(x):
    pos = x > 0.0
    return jnp.where(pos, x, 0.0) + jnp.log1p(jnp.exp(jnp.where(pos, -x, x)))


def _mesh_pos():
    return lax.axis_index("x"), lax.axis_index("y"), lax.axis_index("c")


def _dev_index(x, y, c):
    return 4 * x + 2 * y + c


def _relay_order():
    x, y, c = _mesh_pos()
    n1 = (x + (1 - c) * (1 - 2 * x), y + c * (1 - 2 * y))
    n2 = (x + c * (1 - 2 * x), y + (1 - c) * (1 - 2 * y))
    return (x, y, c), (x, y, 1 - c), n1, n2, (1 - x, 1 - y)


def _all_gather_blocks(outs, send_sems, recv_sems, meanwhile=None):
    me, sibling, n1, n2, diag = _relay_order()
    c = me[2]

    def copy(a, k, block, to):
        rows = outs[a].at[_dev_index(*block)]
        return pltpu.make_async_remote_copy(src_ref=rows, dst_ref=rows, send_sem=send_sems.at[a, k],
                                            recv_sem=recv_sems.at[a, k], device_id=to, device_id_type=MESH)

    n = len(outs)
    started = []

    def start(cp):
        cp.start()
        started.append(cp)

    for a in range(n):
        start(copy(a, 1, me, (*n1, c)))
        start(copy(a, 2, me, (*n2, c)))
        start(copy(a, 0, me, sibling))
    if meanwhile is not None:
        meanwhile()
    for a in range(n):
        copy(a, 1, (*n1, c), me).wait_recv()
        start(copy(a, 3, (*n1, c), (*n2, c)))
        start(copy(a, 4, (*n1, c), sibling))
    for a in range(n):
        copy(a, 2, (*n2, c), me).wait_recv()
        start(copy(a, 5, (*n2, c), sibling))
        copy(a, 3, (*diag, c), me).wait_recv()
        start(copy(a, 6, (*diag, c), sibling))
    for a in range(n):
        copy(a, 0, sibling, me).wait_recv()
        copy(a, 4, (*n2, 1 - c), me).wait_recv()
        copy(a, 5, (*n1, 1 - c), me).wait_recv()
        copy(a, 6, (*diag, 1 - c), me).wait_recv()
    for cp in started:
        cp.wait_send()


def _peer_relations():
    x, y, c = _mesh_pos()
    flips = [(fx, fy, fc) for fx in (0, 1) for fy in (0, 1) for fc in (0, 1)][1:]
    peers = [(1 - x if fx else x, 1 - y if fy else y, 1 - c if fc else c) for fx, fy, fc in flips]
    return (x, y, c), peers


def _direct_gather_start(out_ref, send_sems, recv_sems):
    me, peers = _peer_relations()
    rows = out_ref.at[_dev_index(*me)]
    for k, peer in enumerate(peers):
        pltpu.make_async_remote_copy(src_ref=rows, dst_ref=rows, send_sem=send_sems.at[k], recv_sem=recv_sems.at[k],
                                     device_id=peer, device_id_type=MESH).start()


def _direct_gather_wait(out_ref, send_sems, recv_sems):
    me, peers = _peer_relations()
    for k, peer in enumerate(peers):
        rows = out_ref.at[_dev_index(*peer)]
        cp = pltpu.make_async_remote_copy(src_ref=rows, dst_ref=rows, send_sem=send_sems.at[k],
                                          recv_sem=recv_sems.at[k], device_id=peer, device_id_type=MESH)
        cp.wait_recv()
        cp.wait_send()


def _direct_scatter_start(send_ref, recv_ref, send_sems, recv_sems):
    me, peers = _peer_relations()
    for k, peer in enumerate(peers):
        pltpu.make_async_remote_copy(src_ref=send_ref.at[_dev_index(*peer)], dst_ref=recv_ref.at[k],
                                     send_sem=send_sems.at[k], recv_sem=recv_sems.at[k],
                                     device_id=peer, device_id_type=MESH).start()


def _direct_scatter_wait(send_ref, recv_ref, send_sems, recv_sems):
    me, peers = _peer_relations()
    for k, peer in enumerate(peers):
        cp = pltpu.make_async_remote_copy(src_ref=send_ref.at[_dev_index(*peer)], dst_ref=recv_ref.at[k],
                                          send_sem=send_sems.at[k], recv_sem=recv_sems.at[k],
                                          device_id=peer, device_id_type=MESH)
        cp.wait_recv()
        cp.wait_send()


def _gather_weights(w_in_flat, conv_blk, x2, norm_w, tn):
    nt = x2.shape[0] // tn

    def body(win_ref, conv_ref, x_hbm, nw_ref, gin_ref, gconv_ref, n_hbm, xbuf, nbuf, send_sems, recv_sems, xsem, nsem):
        x, y, c = _mesh_pos()
        me = _dev_index(x, y, c)
        for j in range(FLAT_ROWS):
            gin_ref[me, :, 128 * j:128 * (j + 1)] = win_ref[pl.ds(j, W_IN_SHARD, stride=FLAT_ROWS), :].astype(BF16)
        gconv_ref[me] = conv_ref[...]

        def norm_x():
            def load(t):
                return pltpu.make_async_copy(x_hbm.at[pl.ds(t * tn, tn), :], xbuf.at[t % 2], xsem.at[t % 2])

            def store(t):
                return pltpu.make_async_copy(nbuf.at[t % 2], n_hbm.at[pl.ds(t * tn, tn), :], nsem.at[t % 2])

            load(0).start()
            for t in range(nt):
                if t + 1 < nt:
                    load(t + 1).start()
                load(t).wait()
                if t >= 2:
                    store(t - 2).wait()
                xhat, _ = _rms_hat(xbuf[t % 2])
                nbuf[t % 2] = (xhat * nw_ref[...]).astype(BF16)
                store(t).start()
            for t in range(max(nt - 2, 0), nt):
                store(t).wait()

        _all_gather_blocks((gin_ref, gconv_ref), send_sems, recv_sems, meanwhile=norm_x)

    vm = pl.BlockSpec(memory_space=pltpu.VMEM)
    hbm = pl.BlockSpec(memory_space=pl.ANY)
    return pl.pallas_call(
        body, name="gather_weights",
        out_shape=(jax.ShapeDtypeStruct((N_DEV, W_IN_SHARD, D_MODEL), BF16),
                   jax.ShapeDtypeStruct((N_DEV,) + conv_blk.shape, F32),
                   jax.ShapeDtypeStruct(x2.shape, BF16)),
        in_specs=[vm, vm, hbm, vm], out_specs=(vm, vm, hbm),
        scratch_shapes=[pltpu.VMEM((2, tn, D_MODEL), F32), pltpu.VMEM((2, tn, D_MODEL), BF16),
                        pltpu.SemaphoreType.DMA((2, 7)), pltpu.SemaphoreType.DMA((2, 7)),
                        pltpu.SemaphoreType.DMA((2,)), pltpu.SemaphoreType.DMA((2,))],
        compiler_params=_cp(),
    )(w_in_flat, conv_blk, x2, norm_w)


def _reduce_grads(big, pack_a, pack_b):
    nb = len(big)

    def body(*refs):
        srcs, (pa_ref, pb_ref), outs, (ga_ref, gb_ref) = (
            refs[:nb], refs[nb:nb + 2], refs[nb + 2:2 * nb + 2], refs[2 * nb + 2:2 * nb + 4])
        scr = refs[2 * nb + 4:]
        r1s, r2s, sbs, sts = (scr[k * nb:(k + 1) * nb] for k in range(4))
        s1_send, s1_recv, s2_send, s2_recv, ag_send, ag_recv, st_sem = scr[4 * nb:]
        x, y, c = _mesh_pos()
        me = (x, y, c)
        sibling = (x, y, 1 - c)
        rel = [(x, y), (1 - x, y), (x, 1 - y), (1 - x, 1 - y)]

        ga_ref[_dev_index(*me)] = pa_ref[...]
        gb_ref[_dev_index(*me)] = pb_ref[...]

        def p1(a, r, to):
            return pltpu.make_async_remote_copy(
                src_ref=srcs[a].at[_dev_index(*rel[r], 1 - c)], dst_ref=r1s[a].at[r],
                send_sem=s1_send.at[a, r], recv_sem=s1_recv.at[a, r], device_id=to, device_id_type=MESH)

        _, _, n1, n2, diag = _relay_order()
        order = ((n1, 1 + c), (diag, 3), (n2, 2 - c), ((x, y), 0))

        def p1_landed(a, slot):
            return pltpu.make_async_remote_copy(
                src_ref=r1s[a].at[slot], dst_ref=r1s[a].at[slot], send_sem=s1_send.at[a, slot],
                recv_sem=s1_recv.at[a, slot], device_id=me, device_id_type=MESH)

        def p2(a, k, to):
            return pltpu.make_async_remote_copy(
                src_ref=sbs[a].at[k], dst_ref=r2s[a].at[k],
                send_sem=s2_send.at[a, k], recv_sem=s2_recv.at[a, k], device_id=to, device_id_type=MESH)

        def stage(a, i):
            return pltpu.make_async_copy(srcs[a].at[_dev_index(*order[i][0], c)], sts[a].at[i % 2], st_sem.at[a, i % 2])

        sends = [p1(a, r, sibling) for a in range(nb) for r in range(4)]
        for cp in sends:
            cp.start()
        _all_gather_blocks((ga_ref, gb_ref), ag_send, ag_recv)
        for a in range(nb):
            stage(a, 0).start()
            for i, (_, slot) in enumerate(order):
                if i + 1 < len(order):
                    stage(a, i + 1).start()
                stage(a, i).wait()
                p1_landed(a, slot).wait_recv()
                chip_sum = r1s[a][slot] + sts[a][i % 2]
                if i < 2:
                    sbs[a][i] = chip_sum.astype(BF16)
                    sends.append(p2(a, i, (*n1, c)))
                    sends[-1].start()
                else:
                    r1s[a][slot] = chip_sum
        for a in range(nb):
            p2(a, 0, me).wait_recv()
            p2(a, 1, me).wait_recv()
            sbs[a][2] = (r1s[a][2 - c] + r2s[a][1].astype(F32)).astype(BF16)
            sends.append(p2(a, 2, (*n2, c)))
            sends[-1].start()
        for a in range(nb):
            p2(a, 2, me).wait_recv()
            outs[a][...] = (r1s[a][0] + r2s[a][0].astype(F32)) + r2s[a][2].astype(F32)
        for cp in sends:
            cp.wait_send()

    vm = pl.BlockSpec(memory_space=pltpu.VMEM)
    hbm = pl.BlockSpec(memory_space=pl.ANY)
    blk = [p.shape[1:] for p in big]
    scratch = ([pltpu.VMEM((4,) + b, F32) for b in blk] + [pltpu.VMEM((3,) + b, BF16) for b in blk]
               + [pltpu.VMEM((3,) + b, BF16) for b in blk] + [pltpu.VMEM((2,) + b, F32) for b in blk]
               + [pltpu.SemaphoreType.DMA((nb, 4)), pltpu.SemaphoreType.DMA((nb, 4)),
                  pltpu.SemaphoreType.DMA((nb, 3)), pltpu.SemaphoreType.DMA((nb, 3)),
                  pltpu.SemaphoreType.DMA((2, 7)), pltpu.SemaphoreType.DMA((2, 7)),
                  pltpu.SemaphoreType.DMA((nb, 2))])
    return pl.pallas_call(
        body, name="reduce_grads",
        out_shape=tuple(jax.ShapeDtypeStruct(b, F32) for b in blk)
        + (jax.ShapeDtypeStruct((N_DEV,) + pack_a.shape, F32), jax.ShapeDtypeStruct((N_DEV,) + pack_b.shape, F32)),
        in_specs=[hbm] * nb + [vm, vm], out_specs=(vm,) * (nb + 2),
        scratch_shapes=scratch,
        compiler_params=_cp(),
    )(*big, pack_a, pack_b)


def _rms_hat(xf):
    r = lax.rsqrt(jnp.mean(xf * xf, axis=-1, keepdims=True) + NORM_EPS)
    return xf * r, r


def _shift_up(cur, next_head, s):
    ext = jnp.concatenate([cur, next_head], axis=0)
    n = ext.shape[0]
    return pltpu.roll(ext, n - s, 0)[:cur.shape[0], :]


def _pool_counts(i, tp, w):
    t = i * tp + lax.broadcasted_iota(jnp.int32, (tp, 1), 0)
    return jnp.minimum(t + 1, w).astype(F32)


def _pool_mix(u, u_prev_tail, i, tp):
    win = jnp.concatenate([u_prev_tail, u], axis=0)
    mixes = []
    for gi, w in enumerate(POOL_WINDOWS):
        win = win + pltpu.roll(win, w // 2, 0)
        cols = slice(gi * POOL_GROUP, (gi + 1) * POOL_GROUP)
        mixes.append(win[HALO:, :POOL_GROUP] / _pool_counts(i, tp, w) - u[:, cols])
        if gi + 1 < len(POOL_WINDOWS):
            win = win[:, POOL_GROUP:]
    return mixes


def _prev_tail(ref, i):
    return jnp.where(i > 0, ref[...], 0.0)


def _conv_taps(cur, prev_tail):
    ext = jnp.concatenate([prev_tail, cur], axis=0)
    return [cur] + [pltpu.roll(ext, sft, 0)[HALO:, :] for sft in range(1, CONV_WIDTH)]


def _conv_of_taps(taps, w4):
    y = taps[0] * w4[CONV_WIDTH - 1:CONV_WIDTH, :]
    for sft in range(1, CONV_WIDTH):
        y = y + taps[sft] * w4[CONV_WIDTH - 1 - sft:CONV_WIDTH - sft, :]
    return y


def _conv_fwd(cur, prev_tail, w4):
    ext = jnp.concatenate([prev_tail, cur], axis=0)
    y = ext * w4[CONV_WIDTH - 1:CONV_WIDTH, :]
    for sft in range(1, CONV_WIDTH):
        y = y + pltpu.roll(ext, sft, 0) * w4[CONV_WIDTH - 1 - sft:CONV_WIDTH - sft, :]
    return y[HALO:, :]


def _l2n_heads(t):
    parts = []
    for h in range(DN_HEADS):
        th = t[:, h * DN_HEAD_DIM:(h + 1) * DN_HEAD_DIM]
        parts.append(th * lax.rsqrt(jnp.sum(th * th, axis=-1, keepdims=True) + NORM_EPS))
    return jnp.concatenate(parts, axis=1)


def _post_conv(yq, yk, yv):
    return _l2n_heads(_silu(yq)), _l2n_heads(_silu(yk)), _silu(yv)


def _gates(ba, alog_lane, dtb_lane):
    lane = lax.broadcasted_iota(jnp.int32, ba.shape, 1)
    beta = jax.nn.sigmoid(ba)
    g = -jnp.exp(alog_lane) * _softplus(ba + dtb_lane)
    return jnp.where(lane < DN_HEADS, beta, jnp.where(lane < 2 * DN_HEADS, g, 0.0))


def _front(n_all, g_in, pool_w, pool_scale, conv_full, alog_lane, dtb_lane, w_out_blk, tm):
    s = n_all.shape[0]

    def body(n_ref, pw_ref, ps_ref, cw_ref, al_ref, db_ref, wo_ref, g_hbm,
             pm_ref, pb_ref, yp_ref, qn_ref, kn_ref, vv_ref, gb_ref, gwo_hbm, wt_hbm,
             g_vmem, wt_vmem, tail_u, tail_qkv, gwo_ref, sem, wo_send, wo_recv):
        i = pl.program_id(0)

        @pl.when(i == 0)
        def _():
            gwo_ref[_dev_index(*_mesh_pos())] = wo_ref[...].astype(BF16)
            _direct_gather_start(gwo_ref, wo_send, wo_recv)
            cp = pltpu.make_async_copy(g_hbm, g_vmem, sem)
            cp.start()
            wt_vmem[D_MAIN:, :] = jnp.zeros((D_IN_PAD - D_MAIN, D_MODEL), BF16)
            tail_u[...] = jnp.zeros_like(tail_u)
            tail_qkv[...] = jnp.zeros_like(tail_qkv)
            cp.wait()
            for d in range(N_DEV):
                wt_vmem[W_IN_SHARD * d:W_IN_SHARD * (d + 1), :] = g_vmem[d]
            out = pltpu.make_async_copy(wt_vmem, wt_hbm, sem)
            out.start()
            out.wait()
        n = n_ref[...]
        pm_ref[...] = _dot_nt_bf(n, wt_vmem[:D_MAIN, :])
        pb = _dot_nt_bf(n, wt_vmem[D_MAIN:, :])
        pb_ref[...] = pb
        u = pm_ref[:, :D_POOL]
        mixes = _pool_mix(u, tail_u[...], i, tm)
        tail_u[...] = u[tm - HALO:, :]
        gate = ps_ref[...] * _silu(pm_ref[:, D_POOL:2 * D_POOL])
        for gi in range(4):
            cols = slice(gi * POOL_GROUP, (gi + 1) * POOL_GROUP)
            yp_ref[:, cols] = _dot_bf(mixes[gi], pw_ref[gi]) * gate[:, cols]
        ys = []
        for c in range(3):
            cols = slice(c * D_DN, (c + 1) * D_DN)
            cur = pm_ref[:, 2 * D_POOL + c * D_DN:2 * D_POOL + (c + 1) * D_DN]
            ys.append(_conv_fwd(cur, tail_qkv[:, cols], cw_ref[:, cols]))
            tail_qkv[:, cols] = cur[tm - HALO:, :]
        qn, kn, vv = _post_conv(*ys)
        qn_ref[...] = qn
        kn_ref[...] = kn
        vv_ref[...] = vv
        gb_ref[...] = _gates(pb, al_ref[...], db_ref[...])

        @pl.when(i == s // tm - 1)
        def _():
            _direct_gather_wait(gwo_ref, wo_send, wo_recv)
            out = pltpu.make_async_copy(gwo_ref, gwo_hbm, sem)
            out.start()
            out.wait()

    tile = pl.BlockSpec((tm, D_DN), lambda i: (i, 0))
    lanes = pl.BlockSpec((tm, 128), lambda i: (i, 0))
    row = pl.BlockSpec((1, 128), lambda i: (0, 0))
    return pl.pallas_call(
        body, name="front", grid=(s // tm,),
        out_shape=(jax.ShapeDtypeStruct((s, D_MAIN), F32), jax.ShapeDtypeStruct((s, 128), F32),
                   jax.ShapeDtypeStruct((s, D_POOL), F32), jax.ShapeDtypeStruct((s, D_DN), F32),
                   jax.ShapeDtypeStruct((s, D_DN), F32), jax.ShapeDtypeStruct((s, D_DN), F32),
                   jax.ShapeDtypeStruct((s, 128), F32), jax.ShapeDtypeStruct((N_DEV,) + w_out_blk.shape, BF16),
                   jax.ShapeDtypeStruct((D_IN_PAD, D_MODEL), BF16)),
        in_specs=[pl.BlockSpec((tm, D_MODEL), lambda i: (i, 0)),
                  pl.BlockSpec((4, POOL_GROUP, POOL_GROUP), lambda i: (0, 0, 0)),
                  pl.BlockSpec((1, D_POOL), lambda i: (0, 0)),
                  pl.BlockSpec((CONV_WIDTH, 3 * D_DN), lambda i: (0, 0)), row, row,
                  pl.BlockSpec(memory_space=pltpu.VMEM), pl.BlockSpec(memory_space=pl.ANY)],
        out_specs=(pl.BlockSpec((tm, D_MAIN), lambda i: (i, 0)), lanes, tile, tile, tile, tile, lanes,
                   pl.BlockSpec(memory_space=pl.ANY), pl.BlockSpec(memory_space=pl.ANY)),
        scratch_shapes=[pltpu.VMEM((N_DEV, W_IN_SHARD, D_MODEL), BF16), pltpu.VMEM((D_IN_PAD, D_MODEL), BF16),
                        pltpu.VMEM((HALO, D_POOL), F32), pltpu.VMEM((HALO, 3 * D_DN), F32),
                        pltpu.VMEM((N_DEV,) + w_out_blk.shape, BF16),
                        pltpu.SemaphoreType.DMA, pltpu.SemaphoreType.DMA((7,)), pltpu.SemaphoreType.DMA((7,))],
        compiler_params=_cp(("arbitrary",)),
    )(n_all, pool_w, pool_scale, conv_full, alog_lane, dtb_lane, w_out_blk, g_in)


def _dn_block(q, k, v, gcol, bcol, state, known=None):
    nb, n, d = q.shape
    ii = lax.broadcasted_iota(jnp.int32, (n, n), 0)
    jj = lax.broadcasted_iota(jnp.int32, (n, n), 1)
    lower = ii >= jj
    eye = (ii == jj).astype(F32)
    g_row = jnp.sum(eye * gcol, axis=1, keepdims=True)
    gc_col = jnp.sum(jnp.where(lower, g_row, 0.0), axis=2, keepdims=True)
    gc_row = jnp.sum(eye * gc_col, axis=1, keepdims=True)
    decay = jnp.where(lower, jnp.exp(jnp.where(lower, gc_col - gc_row, 0.0)), 0.0)
    kb = k * bcol
    vb = v * bcol
    qs = q * (DN_HEAD_DIM ** -0.5)
    egc = jnp.exp(gc_col)
    kq = jnp.concatenate([kb, qs], axis=1)
    vk = jnp.concatenate([vb, kb * egc], axis=2)
    if known is None:
        akq = _mm(kq, k, 1, 1, *_DN_PREC["akq"])
    else:
        akq = _mm_known(kq, k, known[0][:, :, :n].astype(F32), 1, 1, _DN_PREC["akq"][1])
    a = jnp.where(ii > jj, akq[:, :n] * decay, 0.0)
    qk = akq[:, n:] * decay
    if known is None:
        t = _tri_inv(a, *_DN_PREC["inv"])
        uw = _mm(t, vk, 1, 0, *_DN_PREC["uw"])
    else:
        t = _tri_inv_known(a, known[0][:, :n, n:].astype(F32), _DN_PREC["inv"][1])
        uw = _mm_known(t, vk, known[1], 1, 0, _DN_PREC["uw"][1])
    pre = jnp.concatenate([akq, jnp.concatenate([t, jnp.zeros_like(t)], axis=1)], axis=2)
    wq = jnp.concatenate([uw[:, :, d:], qs * egc], axis=1)
    g_last = gc_col[:, n - 1:n, :]
    k_dec = k * jnp.exp(g_last - gc_col)
    e_last = jnp.exp(g_last)
    os_, starts = [], []
    for c in range(nb // DN_HEADS):
        sl = slice(c * DN_HEADS, (c + 1) * DN_HEADS)
        if known is not None and c > 0:
            state = _use_known(state, known[2][sl])
        starts.append(state)
        ws = _mm(wq[sl], state, 1, 0, *_DN_PREC["ws"])
        v_new = uw[sl, :, :d] - ws[:, :n]
        os_.append(ws[:, n:] + _mm(qk[sl], v_new, 1, 0, *_DN_PREC["ov"]))
        state = state * e_last[sl] + _mm(k_dec[sl], v_new, 0, 0, *_DN_PREC["st"])
    return jnp.concatenate(os_, axis=0), state, (pre, uw, jnp.concatenate(starts, axis=0))


def _gated_norm(o, dz, nw):
    parts = []
    for h in range(DN_HEADS):
        oh = o[:, h * DN_HEAD_DIM:(h + 1) * DN_HEAD_DIM]
        parts.append(oh * lax.rsqrt(jnp.mean(oh * oh, axis=-1, keepdims=True) + NORM_EPS) * nw)
    return jnp.concatenate(parts, axis=1) * _silu(dz)


def _dn_block_args(gc, q_ref, k_ref, v_ref, gb_ref):
    qs, ks, vs, gs, bs = [], [], [], [], []
    for cc in range(gc):
        r = slice(cc * CHUNK, (cc + 1) * CHUNK)
        gbv = gb_ref[r, :]
        for h in range(DN_HEADS):
            cols = slice(h * DN_HEAD_DIM, (h + 1) * DN_HEAD_DIM)
            qs.append(q_ref[r, cols])
            ks.append(k_ref[r, cols])
            vs.append(v_ref[r, cols])
            gs.append(gbv[:, DN_HEADS + h:DN_HEADS + h + 1])
            bs.append(gbv[:, h:h + 1])
    return tuple(jnp.stack(t, axis=0) for t in (qs, ks, vs, gs, bs))


def _dn_scan_fwd(qn, kn, vv, gb, gc):
    s = qn.shape[0]
    nchunk = s // CHUNK
    rows = gc * CHUNK

    def body(q_ref, k_ref, v_ref, gb_ref, y_ref, ss_ref, pre_ref, uw_ref, state):
        @pl.when(pl.program_id(0) == 0)
        def _():
            state[...] = jnp.zeros_like(state)
        q, k, v, gcol, bcol = _dn_block_args(gc, q_ref, k_ref, v_ref, gb_ref)
        y, new, (pre, uw, starts) = _dn_block(q, k, v, gcol, bcol, state[...])
        state[...] = new
        ss_ref[...] = starts
        pre_ref[...] = pre
        uw_ref[...] = uw
        for cc in range(gc):
            for h in range(DN_HEADS):
                y_ref[cc * CHUNK:(cc + 1) * CHUNK, h * DN_HEAD_DIM:(h + 1) * DN_HEAD_DIM] = y[cc * DN_HEADS + h]

    tile = pl.BlockSpec((rows, D_DN), lambda i: (i, 0))
    return pl.pallas_call(
        body, name="dn_scan_fwd", grid=(nchunk // gc,),
        out_shape=(jax.ShapeDtypeStruct((s, D_DN), F32),
                   jax.ShapeDtypeStruct((nchunk * DN_HEADS, DN_HEAD_DIM, DN_HEAD_DIM), F32),
                   jax.ShapeDtypeStruct((nchunk * DN_HEADS, 2 * CHUNK, 2 * CHUNK), F32),
                   jax.ShapeDtypeStruct((nchunk * DN_HEADS, CHUNK, 2 * DN_HEAD_DIM), F32)),
        in_specs=[tile, tile, tile, pl.BlockSpec((rows, 128), lambda i: (i, 0))],
        out_specs=(tile, pl.BlockSpec((gc * DN_HEADS, DN_HEAD_DIM, DN_HEAD_DIM), lambda i: (i, 0, 0)),
                   pl.BlockSpec((gc * DN_HEADS, 2 * CHUNK, 2 * CHUNK), lambda i: (i, 0, 0)),
                   pl.BlockSpec((gc * DN_HEADS, CHUNK, 2 * DN_HEAD_DIM), lambda i: (i, 0, 0))),
        scratch_shapes=[pltpu.VMEM((DN_HEADS, DN_HEAD_DIM, DN_HEAD_DIM), F32)],
        compiler_params=_cp(("arbitrary",)),
    )(qn, kn, vv, gb)


def _out_proj_loss(y_pool, o_dn, proj_main, dn_norm_w, x2, tgt, w_out_full, fnw, tm):
    s = x2.shape[0]

    def body(yp_ref, o_ref, dz_ref, nw_ref, x_ref, t_ref, wo_ref, fw_ref,
             dh_ref, dyp_ref, do_ref, ddz_ref, gwo_ref, gfw_ref, loss_ref, dnw_ref):
        @pl.when(pl.program_id(0) == 0)
        def _():
            gwo_ref[...] = jnp.zeros_like(gwo_ref)
            gfw_ref[...] = jnp.zeros_like(gfw_ref)
            loss_ref[...] = jnp.zeros_like(loss_ref)
            dnw_ref[...] = jnp.zeros_like(dnw_ref)
        y_dn, gate_vjp = jax.vjp(_gated_norm, o_ref[...], dz_ref[...], nw_ref[...])
        y = jnp.concatenate([yp_ref[...], y_dn], axis=1).astype(BF16)
        wo = wo_ref[...]
        h = x_ref[...] + jnp.dot(y, wo, preferred_element_type=F32)
        hn, r = _rms_hat(h)
        fw = fw_ref[...]
        err = hn * fw - t_ref[...]
        loss_ref[...] += 0.5 * jnp.sum(jnp.sum(err * err, axis=-1, keepdims=True) / D_MODEL, axis=0, keepdims=True)
        dout = err / D_MODEL
        gfw_ref[...] += jnp.sum(dout * hn, axis=0, keepdims=True)
        dhn = dout * fw
        dh = r * (dhn - hn * jnp.mean(dhn * hn, axis=-1, keepdims=True))
        dh_ref[...] = dh
        dhb = dh.astype(BF16)
        dy = _dot_nt_bf(dhb, wo)
        dyp_ref[...] = dy[:, :D_POOL]
        do, ddz, dnw = gate_vjp(dy[:, D_POOL:])
        do_ref[...] = do
        ddz_ref[...] = ddz
        dnw_ref[...] += dnw
        gwo_ref[...] += _dot_tn_bf(y, dhb)

    half = pl.BlockSpec((tm, D_POOL), lambda i: (i, 0))
    full = pl.BlockSpec((tm, D_MODEL), lambda i: (i, 0))
    lrow = pl.BlockSpec((1, 128), lambda i: (0, 0))
    return pl.pallas_call(
        body, name="out_proj_loss", grid=(s // tm,),
        out_shape=(jax.ShapeDtypeStruct((s, D_MODEL), F32), jax.ShapeDtypeStruct((s, D_POOL), F32),
                   jax.ShapeDtypeStruct((s, D_DN), F32), jax.ShapeDtypeStruct((s, D_DN), F32),
                   jax.ShapeDtypeStruct((D_MODEL, D_MODEL), F32),
                   jax.ShapeDtypeStruct((1, D_MODEL), F32), jax.ShapeDtypeStruct((1, 128), F32),
                   jax.ShapeDtypeStruct((1, 128), F32)),
        in_specs=[half, half, pl.BlockSpec((tm, D_DN), lambda i: (i, 5)), lrow, full, full,
                  pl.BlockSpec((D_MODEL, D_MODEL), lambda i: (0, 0)), pl.BlockSpec((1, D_MODEL), lambda i: (0, 0))],
        out_specs=(full, half, half, half, pl.BlockSpec((D_MODEL, D_MODEL), lambda i: (0, 0)),
                   pl.BlockSpec((1, D_MODEL), lambda i: (0, 0)), lrow, lrow),
        compiler_params=_cp(("arbitrary",)),
    )(y_pool, o_dn, proj_main, dn_norm_w, x2, tgt, w_out_full, fnw)


def _dn_scan_bwd(qn, kn, vv, gb, states, pre, uw, do_dn, p_out, gc):
    s = qn.shape[0]
    nchunk = s // CHUNK
    nstep = nchunk // gc
    rows = gc * CHUNK

    def body(q_ref, k_ref, v_ref, gb_ref, ss_ref, pre_ref, uw_ref, dy_ref, po_ref,
             dq_ref, dk_ref, dv_ref, dgb_ref, gro_ref, dstate, po_send, po_recv, rs_send, rs_recv):
        @pl.when(pl.program_id(0) == 0)
        def _():
            dstate[...] = jnp.zeros_like(dstate)
            po_send[...] = po_ref[...].astype(BF16)
            _direct_scatter_start(po_send, po_recv, rs_send, rs_recv)

        @pl.when(pl.program_id(0) == nstep - 1)
        def _():
            _direct_scatter_wait(po_send, po_recv, rs_send, rs_recv)
            total = po_ref[_dev_index(*_mesh_pos())]
            for k in range(N_DEV - 1):
                total = total + po_recv[k].astype(F32)
            gro_ref[...] = total
        lane = lax.broadcasted_iota(jnp.int32, (CHUNK, 128), 1)
        q, k, v, gcol, bcol = _dn_block_args(gc, q_ref, k_ref, v_ref, gb_ref)
        dy = jnp.stack([dy_ref[cc * CHUNK:(cc + 1) * CHUNK, h * DN_HEAD_DIM:(h + 1) * DN_HEAD_DIM]
                        for cc in range(gc) for h in range(DN_HEADS)], axis=0)
        known = (pre_ref[...], uw_ref[...], ss_ref[...])
        _, vjp = jax.vjp(lambda *a: _dn_block(*a, known=known)[:2], q, k, v, gcol, bcol, ss_ref[:DN_HEADS])
        dq, dk, dv, dg, db, dst = vjp((dy, dstate[...]))
        dstate[...] = dst
        for cc in range(gc):
            r = slice(cc * CHUNK, (cc + 1) * CHUNK)
            dgb = jnp.zeros((CHUNK, 128), F32)
            for h in range(DN_HEADS):
                b = cc * DN_HEADS + h
                cols = slice(h * DN_HEAD_DIM, (h + 1) * DN_HEAD_DIM)
                for ref, val in zip((dq_ref, dk_ref, dv_ref), (dq, dk, dv)):
                    ref[r, cols] = val[b]
                dgb = dgb + jnp.where(lane == h, db[b], 0.0) + jnp.where(lane == DN_HEADS + h, dg[b], 0.0)
            dgb_ref[r, :] = dgb

    rev = lambda i: (nstep - 1 - i, 0)
    tile = pl.BlockSpec((rows, D_DN), rev)
    lanes = pl.BlockSpec((rows, 128), rev)
    return pl.pallas_call(
        body, name="dn_scan_bwd", grid=(nstep,),
        out_shape=(jax.ShapeDtypeStruct((s, D_DN), F32),) * 3
        + (jax.ShapeDtypeStruct((s, 128), F32), jax.ShapeDtypeStruct(p_out.shape[1:], F32)),
        in_specs=[tile, tile, tile, lanes,
                  pl.BlockSpec((gc * DN_HEADS, DN_HEAD_DIM, DN_HEAD_DIM), lambda i: (nstep - 1 - i, 0, 0)),
                  pl.BlockSpec((gc * DN_HEADS, 2 * CHUNK, 2 * CHUNK), lambda i: (nstep - 1 - i, 0, 0)),
                  pl.BlockSpec((gc * DN_HEADS, CHUNK, 2 * DN_HEAD_DIM), lambda i: (nstep - 1 - i, 0, 0)), tile,
                  pl.BlockSpec(memory_space=pltpu.VMEM)],
        out_specs=(tile, tile, tile, lanes, pl.BlockSpec(memory_space=pltpu.VMEM)),
        scratch_shapes=[pltpu.VMEM((DN_HEADS, DN_HEAD_DIM, DN_HEAD_DIM), F32),
                        pltpu.VMEM(p_out.shape, BF16), pltpu.VMEM((N_DEV - 1,) + p_out.shape[1:], BF16),
                        pltpu.SemaphoreType.DMA((7,)), pltpu.SemaphoreType.DMA((7,))],
        compiler_params=_cp(("arbitrary",)),
    )(qn, kn, vv, gb, states, pre, uw, do_dn, p_out)


def _back(proj_main, proj_ba, dyp, dqn, dkn, dvv, dgb, ddz, x2, dh, norm_w, pool_w, pool_scale, conv_full,
          alog_lane, dtb_lane, wt_full, tm):
    s = x2.shape[0]
    nstep = s // tm
    per = tm // HALO

    def body(u_ref, z_ref, q_ref, k_ref, v_ref, up_ref, qp_ref, kp_ref, vp_ref, ba_ref,
             dyp_ref, dqn_ref, dkn_ref, dvv_ref, dgb_ref, ddz_ref, x_ref, dh_ref,
             nw_ref, pw_ref, ps_ref, cw_ref, al_ref, db_ref, wt_hbm,
             gx_ref, p_hbm, gnw_ref, dpw_ref, dps_ref, dcw_ref, dal_ref, ddb_ref,
             wt_vmem, acc, blk, head_dc, head_dw, sem, osem):
        j = pl.program_id(0)
        i = nstep - 1 - j

        @pl.when(j == 0)
        def _():
            cp = pltpu.make_async_copy(wt_hbm, wt_vmem, sem)
            cp.start()
            acc[...] = jnp.zeros_like(acc)
            for ref in (gnw_ref, dpw_ref, dps_ref, dcw_ref, dal_ref, ddb_ref, head_dc, head_dw):
                ref[...] = jnp.zeros_like(ref)
            cp.wait()

        u = u_ref[...]
        z = z_ref[...]
        dy = dyp_ref[...]
        ps = ps_ref[...]
        mixes = _pool_mix(u, _prev_tail(up_ref, i), i, tm)
        sg = jax.nn.sigmoid(z)
        sz = z * sg
        dsz = sg * (1.0 + z * (1.0 - sg))
        dzs, dwins = [], []
        for gi, w in enumerate(POOL_WINDOWS):
            cols = slice(gi * POOL_GROUP, (gi + 1) * POOL_GROUP)
            mixw = _dot_bf(mixes[gi], pw_ref[gi])
            dmixw = dy[:, cols] * ps[:, cols] * sz[:, cols]
            dps_ref[:, cols] += jnp.sum(dy[:, cols] * mixw * sz[:, cols], axis=0, keepdims=True)
            dzs.append(dy[:, cols] * mixw * ps[:, cols] * dsz[:, cols])
            dpw_ref[gi] += _dot_tn_bf(mixes[gi], dmixw)
            dwins.append(_dot_nt_bf(dmixw, pw_ref[gi]) / _pool_counts(i, tm, w))
        dzp = jnp.concatenate(dzs, axis=1)
        dw = jnp.concatenate(dwins, axis=1)
        win = jnp.concatenate([dw, head_dw[...]], axis=0)
        m = win.shape[0]
        dups = []
        for gi, w in enumerate(POOL_WINDOWS):
            win = win + pltpu.roll(win, m - w // 2, 0)
            cols = slice(gi * POOL_GROUP, (gi + 1) * POOL_GROUP)
            dups.append(win[:tm, :POOL_GROUP] - dw[:, cols] * _pool_counts(i, tm, w))
            if gi + 1 < len(POOL_WINDOWS):
                win = win[:, POOL_GROUP:]
        dup = jnp.concatenate(dups, axis=1)
        head_dw[...] = dw[:HALO, :]

        curs = (q_ref[...], k_ref[...], v_ref[...])
        tails = (_prev_tail(qp_ref, i), _prev_tail(kp_ref, i), _prev_tail(vp_ref, i))
        taps = [_conv_taps(curs[c], tails[c]) for c in range(3)]
        ys = [_conv_of_taps(taps[c], cw_ref[:, c * D_DN:(c + 1) * D_DN]) for c in range(3)]
        _, vjp = jax.vjp(_post_conv, *ys)
        dys = vjp((dqn_ref[...], dkn_ref[...], dvv_ref[...]))
        dxs = []
        for c, dyc in enumerate(dys):
            cols = slice(c * D_DN, (c + 1) * D_DN)
            w4 = cw_ref[:, cols]
            for sft in range(CONV_WIDTH):
                row = CONV_WIDTH - 1 - sft
                dcw_ref[row:row + 1, cols] += jnp.sum(dyc * taps[c][sft], axis=0, keepdims=True)
            head = head_dc[:, cols]
            dx = dyc * w4[CONV_WIDTH - 1:CONV_WIDTH, :]
            for sft in range(1, CONV_WIDTH):
                dx = dx + _shift_up(dyc, head, sft) * w4[CONV_WIDTH - 1 - sft:CONV_WIDTH - sft, :]
            dxs.append(dx)
            head_dc[:, cols] = dyc[:HALO, :]
        _, gvjp = jax.vjp(_gates, ba_ref[...], al_ref[...], db_ref[...])
        dba, dal, ddb = gvjp(dgb_ref[...])
        dal_ref[...] += dal
        ddb_ref[...] += ddb

        dbab = dba.astype(BF16)
        xhat, r = _rms_hat(x_ref[...])
        nw = nw_ref[...]
        n = (xhat * nw).astype(BF16)
        acc[D_MAIN:, :] += _dot_tn_bf(dbab, n)
        dn = jnp.dot(dbab, wt_vmem[D_MAIN:, :], preferred_element_type=F32)
        for cb, d in enumerate((dup, dzp, dxs[0], dxs[1], dxs[2], ddz_ref[...])):
            rows = slice(cb * D_POOL, (cb + 1) * D_POOL)
            dpart = d.astype(BF16)
            acc[rows, :] += _dot_tn_bf(dpart, n)
            dn = dn + jnp.dot(dpart, wt_vmem[rows, :], preferred_element_type=F32)
        gnw_ref[...] += jnp.sum(dn * xhat, axis=0, keepdims=True)
        dxh = dn * nw
        gx_ref[...] = dh_ref[...] + r * (dxh - xhat * jnp.mean(dxh * xhat, axis=-1, keepdims=True))

        @pl.when(j == nstep - 1)
        def _():
            def out(d):
                return pltpu.make_async_copy(blk.at[d % 2], p_hbm.at[d], osem.at[d % 2])
            for d in range(N_DEV):
                if d >= 2:
                    out(d - 2).wait()
                blk[d % 2] = acc[W_IN_SHARD * d:W_IN_SHARD * (d + 1), :]
                out(d).start()
            out(N_DEV - 2).wait()
            out(N_DEV - 1).wait()

    def col(c):
        return pl.BlockSpec((tm, D_POOL), lambda j: (nstep - 1 - j, c))

    def halo(c):
        return pl.BlockSpec((HALO, D_POOL), lambda j: (jnp.maximum((nstep - 1 - j) * per - 1, 0), c))

    rev = lambda j: (nstep - 1 - j, 0)
    part = pl.BlockSpec((tm, D_POOL), rev)
    lanes = pl.BlockSpec((tm, 128), rev)
    full = pl.BlockSpec((tm, D_MODEL), rev)
    row = pl.BlockSpec((1, D_MODEL), lambda j: (0, 0))
    lrow = pl.BlockSpec((1, 128), lambda j: (0, 0))
    pw = pl.BlockSpec((4, POOL_GROUP, POOL_GROUP), lambda j: (0, 0, 0))
    psp = pl.BlockSpec((1, D_POOL), lambda j: (0, 0))
    cw = pl.BlockSpec((CONV_WIDTH, 3 * D_DN), lambda j: (0, 0))
    return pl.pallas_call(
        body, name="back", grid=(nstep,),
        out_shape=(jax.ShapeDtypeStruct((s, D_MODEL), F32),
                   jax.ShapeDtypeStruct((N_DEV, W_IN_SHARD, D_MODEL), F32), jax.ShapeDtypeStruct((1, D_MODEL), F32),
                   jax.ShapeDtypeStruct((4, POOL_GROUP, POOL_GROUP), F32), jax.ShapeDtypeStruct((1, D_POOL), F32),
                   jax.ShapeDtypeStruct((CONV_WIDTH, 3 * D_DN), F32),
                   jax.ShapeDtypeStruct((1, 128), F32), jax.ShapeDtypeStruct((1, 128), F32)),
        in_specs=[col(0), col(1), col(2), col(3), col(4), halo(0), halo(2), halo(3), halo(4), lanes,
                  part, part, part, part, lanes, part, full, full,
                  row, pw, psp, cw, lrow, lrow, pl.BlockSpec(memory_space=pl.ANY)],
        out_specs=(full, pl.BlockSpec(memory_space=pl.ANY), row, pw, psp, cw, lrow, lrow),
        scratch_shapes=[pltpu.VMEM((D_IN_PAD, D_MODEL), BF16), pltpu.VMEM((D_IN_PAD, D_MODEL), F32),
                        pltpu.VMEM((2, W_IN_SHARD, D_MODEL), F32),
                        pltpu.VMEM((HALO, 3 * D_DN), F32), pltpu.VMEM((HALO, D_POOL), F32),
                        pltpu.SemaphoreType.DMA, pltpu.SemaphoreType.DMA((2,))],
        compiler_params=_cp(("arbitrary",)),
    )(proj_main, proj_main, proj_main, proj_main, proj_main, proj_main, proj_main, proj_main, proj_main, proj_ba,
      dyp, dqn, dkn, dvv, dgb, ddz, x2, dh, norm_w, pool_w, pool_scale, conv_full, alog_lane, dtb_lane, wt_full)


def _adamw_math(w, g, m, v):
    m = ADAM_B1 * m + (1.0 - ADAM_B1) * g
    v = ADAM_B2 * v + (1.0 - ADAM_B2) * (g * g)
    m_hat = m / (1.0 - ADAM_B1 ** ADAM_STEP)
    v_hat = v / (1.0 - ADAM_B2 ** ADAM_STEP)
    delta = -ADAM_LR * (m_hat / (jnp.sqrt(v_hat) + ADAM_EPS) + ADAM_WD * w)
    return delta, m, v


def _adamw_sharded(params):
    k = len(params)

    def body(*refs):
        ins, outs = refs[:4 * k], refs[4 * k:]
        for p in range(k):
            w_ref, g_ref, m_ref, v_ref = ins[4 * p:4 * p + 4]
            go_ref = outs[4 * p]
            if g_ref.shape == w_ref.shape:
                go_ref[...] = g_ref[...]
            else:
                for j in range(FLAT_ROWS):
                    go_ref[pl.ds(j, W_IN_SHARD, stride=FLAT_ROWS), :] = g_ref[:, 128 * j:128 * (j + 1)]
            d, nm, nv = _adamw_math(w_ref[...], go_ref[...], m_ref[...], v_ref[...])
            outs[4 * p + 1][...] = d
            outs[4 * p + 2][...] = nm
            outs[4 * p + 3][...] = nv

    flat = [a for p in params for a in p]
    out_shape = tuple(jax.ShapeDtypeStruct(p[0].shape, F32) for p in params for _ in range(4))
    res = pl.pallas_call(body, name="adamw_sharded", out_shape=out_shape, compiler_params=_cp())(*flat)
    return [tuple(res[4 * p:4 * p + 4]) for p in range(k)]


def _adamw_replicated(gath_a, gath_b, pool, rows):
    nrow = len(rows)

    def body(*refs):
        ga_ref, gb_ref = refs[:2]
        ins = refs[2:2 + 3 * (nrow + 1)]
        outs = refs[2 + 3 * (nrow + 1):]

        def total(ref):
            g = ref[0]
            for d in range(1, N_DEV):
                g = g + ref[d]
            return g

        def update(g, wmv, o):
            w, m, v = (r[...] for r in wmv)
            dl, nm, nv = _adamw_math(w, g, m, v)
            o[0][...] = g
            o[1][...] = dl
            o[2][...] = nm
            o[3][...] = nv

        update(total(ga_ref), ins[:3], outs[:4])
        gb = total(gb_ref)
        for r in range(nrow):
            n = ins[3 * (r + 1)].shape[1]
            update(gb[r:r + 1, :n], ins[3 * (r + 1):3 * (r + 2)], outs[4 * (r + 1):4 * (r + 2)])
        outs[4 * (nrow + 1)][...] = gb[nrow:nrow + 1, 0:1]

    flat = list(pool) + [a for wmv in rows for a in wmv]
    out_shape = ((jax.ShapeDtypeStruct(pool[0].shape, F32),) * 4
                 + tuple(jax.ShapeDtypeStruct(wmv[0].shape, F32) for wmv in rows for _ in range(4))
                 + (jax.ShapeDtypeStruct((1, 1), F32),))
    res = pl.pallas_call(body, name="adamw_replicated", out_shape=out_shape, compiler_params=_cp())(
        gath_a, gath_b, *flat)
    return [res[4 * k:4 * k + 4] for k in range(nrow + 1)], res[-1]


_ROW_ORDER = ("norm_w", "final_norm_w", "pool_scale", "dn_norm_w", "a_log", "dt_bias")


def _pack_rows(vectors):
    out = [jnp.pad(v.reshape(-1), (0, D_MODEL - v.size)) for v in vectors]
    out += [jnp.zeros((D_MODEL,), F32)] * (8 - len(out))
    return jnp.stack(out, axis=0)


def _lane_row(vec4, start):
    return jnp.pad(vec4.reshape(-1), (start, 128 - start - vec4.size)).reshape(1, 128)


def kernel(x, norm_w, w_in, pool_w, pool_scale, conv_w, a_log, dt_bias, dn_norm_w, w_out, final_norm_w, loss_target, m_norm_w, m_w_in, m_pool_w, m_pool_scale, m_conv_w, m_a_log, m_dt_bias, m_dn_norm_w, m_w_out, m_final_norm_w, v_norm_w, v_w_in, v_pool_w, v_pool_scale, v_conv_w, v_a_log, v_dt_bias, v_dn_norm_w, v_w_out, v_final_norm_w):
    s = x.shape[1]
    tm = min(512, s)
    tmb = min(256, s)
    x2 = x[0]
    tgt = loss_target[0]
    def to_flat(a):
        return a[0].reshape(FLAT_ROWS, 128, W_IN_SHARD).transpose(2, 0, 1).reshape(W_IN_SHARD * FLAT_ROWS, 128)

    def from_flat(f):
        return f.reshape(W_IN_SHARD, FLAT_ROWS, 128).transpose(1, 2, 0).reshape(1, D_MODEL, W_IN_SHARD)

    wf, m_wf, v_wf = to_flat(w_in), to_flat(m_w_in), to_flat(v_w_in)

    g_in, g_conv, n_all = _gather_weights(wf, conv_w[0], x2, norm_w, tm)
    conv_full = g_conv.transpose(1, 0, 2).reshape(CONV_WIDTH, 3 * D_DN)
    alog_lane = _lane_row(a_log, DN_HEADS)
    dtb_lane = _lane_row(dt_bias, DN_HEADS)
    fnw = final_norm_w.reshape(1, D_MODEL)

    proj_main, proj_ba, y_pool, qn, kn, vv, gb, g_out, wt_full = _front(
        n_all, g_in, pool_w[0], pool_scale, conv_full, alog_lane, dtb_lane, w_out[0], tm)
    w_out_full = g_out.reshape(D_MODEL, D_MODEL)
    o_dn, states, dn_pre, dn_uw = _dn_scan_fwd(qn, kn, vv, gb, DN_CHUNKS_PER_STEP)

    dh, dyp, do_dn, ddz, g_wout, g_fnw, loss_part, g_dnw = _out_proj_loss(
        y_pool, o_dn, proj_main, dn_norm_w, x2, tgt, w_out_full, fnw, tm)
    p_out = g_wout.reshape(N_DEV, D_MODEL // N_DEV, D_MODEL)
    dqn, dkn, dvv, dgb, gr_out = _dn_scan_bwd(qn, kn, vv, gb, states, dn_pre, dn_uw, do_dn, p_out, DN_CHUNKS_PER_STEP)
    grad_x2, p_in, g_nw, g_pw, g_ps, g_conv_full, g_al, g_db = _back(
        proj_main, proj_ba, dyp, dqn, dkn, dvv, dgb, ddz, x2, dh, norm_w, pool_w[0], pool_scale, conv_full,
        alog_lane, dtb_lane, wt_full, tmb)

    p_conv = g_conv_full.reshape(CONV_WIDTH, N_DEV, 3 * D_DN // N_DEV).transpose(1, 0, 2)
    pack_a = g_pw.reshape(4 * POOL_GROUP, POOL_GROUP)
    pack_b = _pack_rows([g_nw, g_fnw, g_ps, g_dnw, g_al[0, DN_HEADS:2 * DN_HEADS], g_db[0, DN_HEADS:2 * DN_HEADS],
                         loss_part[0, :1]])
    gr_in, gr_conv, gath_a, gath_b = _reduce_grads((p_in, p_conv), pack_a, pack_b)

    r_in, r_out, r_conv = _adamw_sharded([(wf, gr_in, m_wf, v_wf), (w_out[0], gr_out, m_w_out[0], v_w_out[0]),
                                          (conv_w[0], gr_conv, m_conv_w[0], v_conv_w[0])])
    flat = lambda a: a.reshape(4 * POOL_GROUP, POOL_GROUP)
    row = lambda a: a.reshape(1, -1)
    vecs = {"norm_w": (norm_w, m_norm_w, v_norm_w), "final_norm_w": (final_norm_w, m_final_norm_w, v_final_norm_w),
            "pool_scale": (pool_scale, m_pool_scale, v_pool_scale), "dn_norm_w": (dn_norm_w, m_dn_norm_w, v_dn_norm_w),
            "a_log": (a_log, m_a_log, v_a_log), "dt_bias": (dt_bias, m_dt_bias, v_dt_bias)}
    res, loss = _adamw_replicated(gath_a, gath_b, (flat(pool_w), flat(m_pool_w), flat(v_pool_w)),
                                  [tuple(row(a) for a in vecs[nm]) for nm in _ROW_ORDER])
    r_pool = res[0]
    r_vec = dict(zip(_ROW_ORDER, res[1:]))

    def group(k):
        vec = lambda nm: r_vec[nm][k].reshape(vecs[nm][0].shape)
        return (vec("norm_w"), from_flat(r_in[k]), r_pool[k].reshape(pool_w.shape), vec("pool_scale"), r_conv[k][None],
                vec("a_log"), vec("dt_bias"), vec("dn_norm_w"), r_out[k][None], vec("final_norm_w"))

    return (loss[0, 0], grad_x2[None], *group(0), *group(1), *group(2), *group(3))
```

```python
import functools

import jax
import jax.numpy as jnp
from jax import lax
from jax.experimental import pallas as pl
from jax.experimental.pallas import tpu as pltpu

F32 = jnp.float32
BF16 = jnp.bfloat16
HI = lax.Precision.HIGHEST
MESH = pl.DeviceIdType.MESH

D_MODEL = 1024
D_POOL = 512
D_DN = 512
POOL_WINDOWS = (2, 4, 8, 16)
POOL_GROUP = 128
DN_HEADS = 4
DN_HEAD_DIM = 128
CONV_WIDTH = 4
CHUNK = 64
NORM_EPS = 1e-6
D_IN = 3080
D_MAIN = 3072
FLAT_ROWS = D_MODEL // 128
D_IN_PAD = D_MAIN + 128
N_DEV = 8
W_IN_SHARD = D_IN // N_DEV
HALO = 16
DN_CHUNKS_PER_STEP = 8

ADAM_LR = 0.001
ADAM_B1 = 0.9
ADAM_B2 = 0.999
ADAM_EPS = 1e-08
ADAM_WD = 0.01
ADAM_STEP = 10

VMEM_LIMIT = 56 * 1024 * 1024
def _cp(sem=None, vmem=VMEM_LIMIT):
    kw = {"vmem_limit_bytes": vmem}
    if sem is not None:
        kw["dimension_semantics"] = sem
    return pltpu.CompilerParams(**kw)


def _dot_bf(a, b):
    return jnp.dot(a.astype(BF16), b.astype(BF16), preferred_element_type=F32)


def _dot_nt_bf(a, b):
    return lax.dot_general(a.astype(BF16), b.astype(BF16), (((1,), (1,)), ((), ())), preferred_element_type=F32)


def _dot_tn_bf(a, b):
    return lax.dot_general(a.astype(BF16), b.astype(BF16), (((0,), (0,)), ((), ())), preferred_element_type=F32)


def _mm_raw(a, b, ca, cb, prec):
    off = a.ndim - 2
    dn = (((ca + off,), (cb + off,)), ((0,), (0,)) if off else ((), ()))
    if prec == "hi":
        return lax.dot_general(a, b, dn, precision=HI, preferred_element_type=F32)
    ah, bh = a.astype(BF16), b.astype(BF16)
    out = lax.dot_general(ah, bh, dn, preferred_element_type=F32)
    if prec == "x3":
        al = (a - ah.astype(F32)).astype(BF16)
        bl = (b - bh.astype(F32)).astype(BF16)
        out = out + lax.dot_general(ah, bl, dn, preferred_element_type=F32)
        out = out + lax.dot_general(al, bh, dn, preferred_element_type=F32)
    return out


@functools.partial(jax.custom_vjp, nondiff_argnums=(2, 3, 4, 5))
def _mm(a, b, ca, cb, prec, bprec):
    return _mm_raw(a, b, ca, cb, prec)


def _mm_fwd(a, b, ca, cb, prec, bprec):
    return _mm_raw(a, b, ca, cb, prec), (a, b)


def _mm_bwd(ca, cb, prec, bprec, res, dc):
    a, b = res
    da = _mm_raw(dc, b, 1, 1 - cb, bprec) if ca == 1 else _mm_raw(b, dc, 1 - cb, 1, bprec)
    db = _mm_raw(a, dc, 1 - ca, 0, bprec) if cb == 0 else _mm_raw(dc, a, 0, 1 - ca, bprec)
    return da, db


_mm.defvjp(_mm_fwd, _mm_bwd)


@functools.partial(jax.custom_vjp, nondiff_argnums=(1, 2))
def _tri_inv(a, prec, bprec):
    n = a.shape[-1]
    ii = lax.broadcasted_iota(jnp.int32, (n, n), 0)
    jj = lax.broadcasted_iota(jnp.int32, (n, n), 1)
    p = (ii == jj).astype(F32) - a
    b = _mm_raw(a, a, 1, 0, prec)
    for _ in range(4):
        pb = _mm_raw(jnp.concatenate([p, b], axis=-2), b, 1, 0, prec)
        p = p + pb[..., :n, :]
        b = pb[..., n:, :]
    return p + _mm_raw(p, b, 1, 0, prec)


def _tri_inv_fwd(a, prec, bprec):
    t = _tri_inv(a, prec, bprec)
    return t, t


def _tri_inv_bwd(prec, bprec, t, dt):
    return (-_mm_raw(_mm_raw(t, dt, 0, 0, bprec), t, 1, 1, bprec),)


_tri_inv.defvjp(_tri_inv_fwd, _tri_inv_bwd)

@functools.partial(jax.custom_vjp, nondiff_argnums=(3, 4, 5))
def _mm_known(a, b, out, ca, cb, bprec):
    return out


def _mm_known_fwd(a, b, out, ca, cb, bprec):
    return out, (a, b)


def _mm_known_bwd(ca, cb, bprec, res, dc):
    return _mm_bwd(ca, cb, None, bprec, res, dc) + (jnp.zeros_like(dc),)


_mm_known.defvjp(_mm_known_fwd, _mm_known_bwd)


@jax.custom_vjp
def _use_known(x, known):
    return known


_use_known.defvjp(lambda x, known: (known, None), lambda _, g: (g, jnp.zeros_like(g)))


@functools.partial(jax.custom_vjp, nondiff_argnums=(2,))
def _tri_inv_known(a, t, bprec):
    return t


def _tri_inv_known_fwd(a, t, bprec):
    return t, t


def _tri_inv_known_bwd(bprec, t, dt):
    return _tri_inv_bwd(None, bprec, t, dt) + (jnp.zeros_like(dt),)


_tri_inv_known.defvjp(_tri_inv_known_fwd, _tri_inv_known_bwd)

_DN_PREC = {"akq": ("bf16", "bf16"), "inv": ("bf16", "bf16"), "uw": ("bf16", "bf16"), "ws": ("bf16", "bf16"),
            "ov": ("bf16", "bf16"), "st": ("bf16", "bf16")}


def _silu(x):
    return x * jax.nn.sigmoid(x)


def _softplus(x):
    pos = x > 0.0
    return jnp.where(pos, x, 0.0) + jnp.log1p(jnp.exp(jnp.where(pos, -x, x)))


def _mesh_pos():
    return lax.axis_index("x"), lax.axis_index("y"), lax.axis_index("c")


def _dev_index(x, y, c):
    return 4 * x + 2 * y + c


def _relay_order():
    x, y, c = _mesh_pos()
    n1 = (x + (1 - c) * (1 - 2 * x), y + c * (1 - 2 * y))
    n2 = (x + c * (1 - 2 * x), y + (1 - c) * (1 - 2 * y))
    return (x, y, c), (x, y, 1 - c), n1, n2, (1 - x, 1 - y)


def _all_gather_blocks(outs, send_sems, recv_sems, meanwhile=None):
    me, sibling, n1, n2, diag = _relay_order()
    c = me[2]

    def copy(a, k, block, to):
        rows = outs[a].at[_dev_index(*block)]
        return pltpu.make_async_remote_copy(src_ref=rows, dst_ref=rows, send_sem=send_sems.at[a, k],
                                            recv_sem=recv_sems.at[a, k], device_id=to, device_id_type=MESH)

    n = len(outs)
    started = []

    def start(cp):
        cp.start()
        started.append(cp)

    for a in range(n):
        start(copy(a, 1, me, (*n1, c)))
        start(copy(a, 2, me, (*n2, c)))
        start(copy(a, 0, me, sibling))
    if meanwhile is not None:
        meanwhile()
    for a in range(n):
        copy(a, 1, (*n1, c), me).wait_recv()
        start(copy(a, 3, (*n1, c), (*n2, c)))
        start(copy(a, 4, (*n1, c), sibling))
    for a in range(n):
        copy(a, 2, (*n2, c), me).wait_recv()
        start(copy(a, 5, (*n2, c), sibling))
        copy(a, 3, (*diag, c), me).wait_recv()
        start(copy(a, 6, (*diag, c), sibling))
    for a in range(n):
        copy(a, 0, sibling, me).wait_recv()
        copy(a, 4, (*n2, 1 - c), me).wait_recv()
        copy(a, 5, (*n1, 1 - c), me).wait_recv()
        copy(a, 6, (*diag, 1 - c), me).wait_recv()
    for cp in started:
        cp.wait_send()


def _peer_relations():
    x, y, c = _mesh_pos()
    flips = [(fx, fy, fc) for fx in (0, 1) for fy in (0, 1) for fc in (0, 1)][1:]
    peers = [(1 - x if fx else x, 1 - y if fy else y, 1 - c if fc else c) for fx, fy, fc in flips]
    return (x, y, c), peers


def _direct_gather_start(out_ref, send_sems, recv_sems):
    me, peers = _peer_relations()
    rows = out_ref.at[_dev_index(*me)]
    for k, peer in enumerate(peers):
        pltpu.make_async_remote_copy(src_ref=rows, dst_ref=rows, send_sem=send_sems.at[k], recv_sem=recv_sems.at[k],
                                     device_id=peer, device_id_type=MESH).start()


def _direct_gather_wait(out_ref, send_sems, recv_sems):
    me, peers = _peer_relations()
    for k, peer in enumerate(peers):
        rows = out_ref.at[_dev_index(*peer)]
        cp = pltpu.make_async_remote_copy(src_ref=rows, dst_ref=rows, send_sem=send_sems.at[k],
                                          recv_sem=recv_sems.at[k], device_id=peer, device_id_type=MESH)
        cp.wait_recv()
        cp.wait_send()


def _direct_scatter_start(send_ref, recv_ref, send_sems, recv_sems):
    me, peers = _peer_relations()
    for k, peer in enumerate(peers):
        pltpu.make_async_remote_copy(src_ref=send_ref.at[_dev_index(*peer)], dst_ref=recv_ref.at[k],
                                     send_sem=send_sems.at[k], recv_sem=recv_sems.at[k],
                                     device_id=peer, device_id_type=MESH).start()


def _direct_scatter_wait(send_ref, recv_ref, send_sems, recv_sems):
    me, peers = _peer_relations()
    for k, peer in enumerate(peers):
        cp = pltpu.make_async_remote_copy(src_ref=send_ref.at[_dev_index(*peer)], dst_ref=recv_ref.at[k],
                                          send_sem=send_sems.at[k], recv_sem=recv_sems.at[k],
                                          device_id=peer, device_id_type=MESH)
        cp.wait_recv()
        cp.wait_send()


def _gather_weights(w_in_flat, conv_blk, x2, norm_w, tn):
    nt = x2.shape[0] // tn

    def body(win_ref, conv_ref, x_hbm, nw_ref, gin_ref, gconv_ref, n_hbm, xbuf, nbuf, send_sems, recv_sems, xsem, nsem):
        x, y, c = _mesh_pos()
        me = _dev_index(x, y, c)
        for j in range(FLAT_ROWS):
            gin_ref[me, :, 128 * j:128 * (j + 1)] = win_ref[pl.ds(j, W_IN_SHARD, stride=FLAT_ROWS), :].astype(BF16)
        gconv_ref[me] = conv_ref[...]

        def norm_x():
            def load(t):
                return pltpu.make_async_copy(x_hbm.at[pl.ds(t * tn, tn), :], xbuf.at[t % 2], xsem.at[t % 2])

            def store(t):
                return pltpu.make_async_copy(nbuf.at[t % 2], n_hbm.at[pl.ds(t * tn, tn), :], nsem.at[t % 2])

            load(0).start()
            for t in range(nt):
                if t + 1 < nt:
                    load(t + 1).start()
                load(t).wait()
                if t >= 2:
                    store(t - 2).wait()
                xhat, _ = _rms_hat(xbuf[t % 2])
                nbuf[t % 2] = (xhat * nw_ref[...]).astype(BF16)
                store(t).start()
            for t in range(max(nt - 2, 0), nt):
                store(t).wait()

        _all_gather_blocks((gin_ref, gconv_ref), send_sems, recv_sems, meanwhile=norm_x)

    vm = pl.BlockSpec(memory_space=pltpu.VMEM)
    hbm = pl.BlockSpec(memory_space=pl.ANY)
    return pl.pallas_call(
        body, name="gather_weights",
        out_shape=(jax.ShapeDtypeStruct((N_DEV, W_IN_SHARD, D_MODEL), BF16),
                   jax.ShapeDtypeStruct((N_DEV,) + conv_blk.shape, F32),
                   jax.ShapeDtypeStruct(x2.shape, BF16)),
        in_specs=[vm, vm, hbm, vm], out_specs=(vm, vm, hbm),
        scratch_shapes=[pltpu.VMEM((2, tn, D_MODEL), F32), pltpu.VMEM((2, tn, D_MODEL), BF16),
                        pltpu.SemaphoreType.DMA((2, 7)), pltpu.SemaphoreType.DMA((2, 7)),
                        pltpu.SemaphoreType.DMA((2,)), pltpu.SemaphoreType.DMA((2,))],
        compiler_params=_cp(),
    )(w_in_flat, conv_blk, x2, norm_w)


def _reduce_grads(big, pack_a, pack_b, sharded, known):
    nb = len(big)
    held = [a for wmv in sharded for a in wmv] + list(known)
    nh = len(held)

    def body(*refs):
        srcs, (pa_ref, pb_ref), held_hbm = refs[:nb], refs[nb:nb + 2], refs[nb + 2:nb + 2 + nh]
        o = nb + 2 + nh
        outs, known_outs, (ga_ref, gb_ref) = (
            refs[o:o + 4 * nb], refs[o + 4 * nb:o + 4 * nb + 3], refs[o + 4 * nb + 3:o + 4 * nb + 5])
        scr = refs[o + 4 * nb + 5:]
        r1s, r2s, sbs, sts = (scr[k * nb:(k + 1) * nb] for k in range(4))
        held_vmem = scr[4 * nb:4 * nb + nh]
        s1_send, s1_recv, s2_send, s2_recv, ag_send, ag_recv, st_sem, held_sem = scr[4 * nb + nh:]
        x, y, c = _mesh_pos()
        me = (x, y, c)
        sibling = (x, y, 1 - c)
        rel = [(x, y), (1 - x, y), (x, 1 - y), (1 - x, 1 - y)]

        ga_ref[_dev_index(*me)] = pa_ref[...]
        gb_ref[_dev_index(*me)] = pb_ref[...]

        def p1(a, r, to):
            return pltpu.make_async_remote_copy(
                src_ref=srcs[a].at[_dev_index(*rel[r], 1 - c)], dst_ref=r1s[a].at[r],
                send_sem=s1_send.at[a, r], recv_sem=s1_recv.at[a, r], device_id=to, device_id_type=MESH)

        _, _, n1, n2, diag = _relay_order()
        order = ((n1, 1 + c), (diag, 3), (n2, 2 - c), ((x, y), 0))

        def p1_landed(a, slot):
            return pltpu.make_async_remote_copy(
                src_ref=r1s[a].at[slot], dst_ref=r1s[a].at[slot], send_sem=s1_send.at[a, slot],
                recv_sem=s1_recv.at[a, slot], device_id=me, device_id_type=MESH)

        def p2(a, k, to):
            return pltpu.make_async_remote_copy(
                src_ref=sbs[a].at[k], dst_ref=r2s[a].at[k],
                send_sem=s2_send.at[a, k], recv_sem=s2_recv.at[a, k], device_id=to, device_id_type=MESH)

        def stage(a, i):
            return pltpu.make_async_copy(srcs[a].at[_dev_index(*order[i][0], c)], sts[a].at[i % 2], st_sem.at[a, i % 2])

        sends = [p1(a, r, sibling) for a in range(nb) for r in range(4)]
        for cp in sends:
            cp.start()
        loads = [pltpu.make_async_copy(held_hbm[k], held_vmem[k], held_sem.at[k]) for k in range(nh)]
        for cp in loads:
            cp.start()
        _all_gather_blocks((ga_ref, gb_ref), ag_send, ag_recv)
        for a in range(nb):
            stage(a, 0).start()
            for i, (_, slot) in enumerate(order):
                if i + 1 < len(order):
                    stage(a, i + 1).start()
                stage(a, i).wait()
                p1_landed(a, slot).wait_recv()
                chip_sum = r1s[a][slot] + sts[a][i % 2]
                if i < 2:
                    sbs[a][i] = chip_sum.astype(BF16)
                    sends.append(p2(a, i, (*n1, c)))
                    sends[-1].start()
                else:
                    r1s[a][slot] = chip_sum
        for cp in loads:
            cp.wait()
        kw, kg, km, kv = (r[...] for r in held_vmem[3 * nb:])
        for ref, val in zip(known_outs, _adamw_math(kw, kg, km, kv)):
            ref[...] = val
        for a in range(nb):
            p2(a, 0, me).wait_recv()
            p2(a, 1, me).wait_recv()
            sbs[a][2] = (r1s[a][2 - c] + r2s[a][1].astype(F32)).astype(BF16)
            sends.append(p2(a, 2, (*n2, c)))
            sends[-1].start()
        for a in range(nb):
            p2(a, 2, me).wait_recv()
            g_ref = outs[4 * a]
            total = (r1s[a][0] + r2s[a][0].astype(F32)) + r2s[a][2].astype(F32)
            if total.shape == g_ref.shape:
                g_ref[...] = total
            else:
                r1s[a][0] = total
                for j in range(FLAT_ROWS):
                    g_ref[pl.ds(j, W_IN_SHARD, stride=FLAT_ROWS), :] = r1s[a][0, :, 128 * j:128 * (j + 1)]
            w, m, v = (r[...] for r in held_vmem[3 * a:3 * a + 3])
            for ref, val in zip(outs[4 * a + 1:4 * a + 4], _adamw_math(w, g_ref[...], m, v)):
                ref[...] = val
        for cp in sends:
            cp.wait_send()

    vm = pl.BlockSpec(memory_space=pltpu.VMEM)
    hbm = pl.BlockSpec(memory_space=pl.ANY)
    blk = [p.shape[1:] for p in big]
    scratch = ([pltpu.VMEM((4,) + b, F32) for b in blk] + [pltpu.VMEM((3,) + b, BF16) for b in blk]
               + [pltpu.VMEM((3,) + b, BF16) for b in blk] + [pltpu.VMEM((2,) + b, F32) for b in blk]
               + [pltpu.VMEM(h.shape, F32) for h in held]
               + [pltpu.SemaphoreType.DMA((nb, 4)), pltpu.SemaphoreType.DMA((nb, 4)),
                  pltpu.SemaphoreType.DMA((nb, 3)), pltpu.SemaphoreType.DMA((nb, 3)),
                  pltpu.SemaphoreType.DMA((2, 7)), pltpu.SemaphoreType.DMA((2, 7)),
                  pltpu.SemaphoreType.DMA((nb, 2)), pltpu.SemaphoreType.DMA((nh,))])
    res = pl.pallas_call(
        body, name="reduce_grads",
        out_shape=tuple(jax.ShapeDtypeStruct(wmv[0].shape, F32) for wmv in sharded for _ in range(4))
        + (jax.ShapeDtypeStruct(known[0].shape, F32),) * 3
        + (jax.ShapeDtypeStruct((N_DEV,) + pack_a.shape, F32), jax.ShapeDtypeStruct((N_DEV,) + pack_b.shape, F32)),
        in_specs=[hbm] * nb + [vm, vm] + [hbm] * nh, out_specs=(vm,) * (4 * nb + 5),
        scratch_shapes=scratch,
        compiler_params=_cp(),
    )(*big, pack_a, pack_b, *held)
    return [tuple(res[4 * a:4 * a + 4]) for a in range(nb)], tuple(res[4 * nb:4 * nb + 3]), res[-2], res[-1]


def _rms_hat(xf):
    r = lax.rsqrt(jnp.mean(xf * xf, axis=-1, keepdims=True) + NORM_EPS)
    return xf * r, r


def _shift_up(cur, next_head, s):
    ext = jnp.concatenate([cur, next_head], axis=0)
    n = ext.shape[0]
    return pltpu.roll(ext, n - s, 0)[:cur.shape[0], :]


def _pool_counts(i, tp, w):
    t = i * tp + lax.broadcasted_iota(jnp.int32, (tp, 1), 0)
    return jnp.minimum(t + 1, w).astype(F32)


def _pool_mix(u, u_prev_tail, i, tp):
    win = jnp.concatenate([u_prev_tail, u], axis=0)
    mixes = []
    for gi, w in enumerate(POOL_WINDOWS):
        win = win + pltpu.roll(win, w // 2, 0)
        cols = slice(gi * POOL_GROUP, (gi + 1) * POOL_GROUP)
        mixes.append(win[HALO:, :POOL_GROUP] / _pool_counts(i, tp, w) - u[:, cols])
        if gi + 1 < len(POOL_WINDOWS):
            win = win[:, POOL_GROUP:]
    return mixes


def _prev_tail(ref, i):
    return jnp.where(i > 0, ref[...], 0.0)


def _conv_taps(cur, prev_tail):
    ext = jnp.concatenate([prev_tail, cur], axis=0)
    return [cur] + [pltpu.roll(ext, sft, 0)[HALO:, :] for sft in range(1, CONV_WIDTH)]


def _conv_of_taps(taps, w4):
    y = taps[0] * w4[CONV_WIDTH - 1:CONV_WIDTH, :]
    for sft in range(1, CONV_WIDTH):
        y = y + taps[sft] * w4[CONV_WIDTH - 1 - sft:CONV_WIDTH - sft, :]
    return y


def _conv_fwd(cur, prev_tail, w4):
    ext = jnp.concatenate([prev_tail, cur], axis=0)
    y = ext * w4[CONV_WIDTH - 1:CONV_WIDTH, :]
    for sft in range(1, CONV_WIDTH):
        y = y + pltpu.roll(ext, sft, 0) * w4[CONV_WIDTH - 1 - sft:CONV_WIDTH - sft, :]
    return y[HALO:, :]


def _l2n_heads(t):
    parts = []
    for h in range(DN_HEADS):
        th = t[:, h * DN_HEAD_DIM:(h + 1) * DN_HEAD_DIM]
        parts.append(th * lax.rsqrt(jnp.sum(th * th, axis=-1, keepdims=True) + NORM_EPS))
    return jnp.concatenate(parts, axis=1)


def _post_conv(yq, yk, yv):
    return _l2n_heads(_silu(yq)), _l2n_heads(_silu(yk)), _silu(yv)


def _gates(ba, alog_lane, dtb_lane):
    lane = lax.broadcasted_iota(jnp.int32, ba.shape, 1)
    beta = jax.nn.sigmoid(ba)
    g = -jnp.exp(alog_lane) * _softplus(ba + dtb_lane)
    return jnp.where(lane < DN_HEADS, beta, jnp.where(lane < 2 * DN_HEADS, g, 0.0))


def _front(n_all, g_in, pool_w, pool_scale, conv_full, alog_lane, dtb_lane, w_out_blk, tm):
    s = n_all.shape[0]

    def body(n_ref, pw_ref, ps_ref, cw_ref, al_ref, db_ref, wo_ref, g_hbm,
             pm_ref, pb_ref, yp_ref, qn_ref, kn_ref, vv_ref, gb_ref, gwo_hbm, wt_hbm,
             g_vmem, wt_vmem, tail_u, tail_qkv, gwo_ref, sem, wo_send, wo_recv):
        i = pl.program_id(0)

        @pl.when(i == 0)
        def _():
            gwo_ref[_dev_index(*_mesh_pos())] = wo_ref[...].astype(BF16)
            _direct_gather_start(gwo_ref, wo_send, wo_recv)
            cp = pltpu.make_async_copy(g_hbm, g_vmem, sem)
            cp.start()
            wt_vmem[D_MAIN:, :] = jnp.zeros((D_IN_PAD - D_MAIN, D_MODEL), BF16)
            tail_u[...] = jnp.zeros_like(tail_u)
            tail_qkv[...] = jnp.zeros_like(tail_qkv)
            cp.wait()
            for d in range(N_DEV):
                wt_vmem[W_IN_SHARD * d:W_IN_SHARD * (d + 1), :] = g_vmem[d]
            out = pltpu.make_async_copy(wt_vmem, wt_hbm, sem)
            out.start()
            out.wait()
        n = n_ref[...]
        pm_ref[...] = _dot_nt_bf(n, wt_vmem[:D_MAIN, :])
        pb = _dot_nt_bf(n, wt_vmem[D_MAIN:, :])
        pb_ref[...] = pb
        u = pm_ref[:, :D_POOL]
        mixes = _pool_mix(u, tail_u[...], i, tm)
        tail_u[...] = u[tm - HALO:, :]
        gate = ps_ref[...] * _silu(pm_ref[:, D_POOL:2 * D_POOL])
        for gi in range(4):
            cols = slice(gi * POOL_GROUP, (gi + 1) * POOL_GROUP)
            yp_ref[:, cols] = _dot_bf(mixes[gi], pw_ref[gi]) * gate[:, cols]
        ys = []
        for c in range(3):
            cols = slice(c * D_DN, (c + 1) * D_DN)
            cur = pm_ref[:, 2 * D_POOL + c * D_DN:2 * D_POOL + (c + 1) * D_DN]
            ys.append(_conv_fwd(cur, tail_qkv[:, cols], cw_ref[:, cols]))
            tail_qkv[:, cols] = cur[tm - HALO:, :]
        qn, kn, vv = _post_conv(*ys)
        qn_ref[...] = qn
        kn_ref[...] = kn
        vv_ref[...] = vv
        gb_ref[...] = _gates(pb, al_ref[...], db_ref[...])

        @pl.when(i == s // tm - 1)
        def _():
            _direct_gather_wait(gwo_ref, wo_send, wo_recv)
            out = pltpu.make_async_copy(gwo_ref, gwo_hbm, sem)
            out.start()
            out.wait()

    tile = pl.BlockSpec((tm, D_DN), lambda i: (i, 0))
    lanes = pl.BlockSpec((tm, 128), lambda i: (i, 0))
    row = pl.BlockSpec((1, 128), lambda i: (0, 0))
    return pl.pallas_call(
        body, name="front", grid=(s // tm,),
        out_shape=(jax.ShapeDtypeStruct((s, D_MAIN), F32), jax.ShapeDtypeStruct((s, 128), F32),
                   jax.ShapeDtypeStruct((s, D_POOL), F32), jax.ShapeDtypeStruct((s, D_DN), F32),
                   jax.ShapeDtypeStruct((s, D_DN), F32), jax.ShapeDtypeStruct((s, D_DN), F32),
                   jax.ShapeDtypeStruct((s, 128), F32), jax.ShapeDtypeStruct((N_DEV,) + w_out_blk.shape, BF16),
                   jax.ShapeDtypeStruct((D_IN_PAD, D_MODEL), BF16)),
        in_specs=[pl.BlockSpec((tm, D_MODEL), lambda i: (i, 0)),
                  pl.BlockSpec((4, POOL_GROUP, POOL_GROUP), lambda i: (0, 0, 0)),
                  pl.BlockSpec((1, D_POOL), lambda i: (0, 0)),
                  pl.BlockSpec((CONV_WIDTH, 3 * D_DN), lambda i: (0, 0)), row, row,
                  pl.BlockSpec(memory_space=pltpu.VMEM), pl.BlockSpec(memory_space=pl.ANY)],
        out_specs=(pl.BlockSpec((tm, D_MAIN), lambda i: (i, 0)), lanes, tile, tile, tile, tile, lanes,
                   pl.BlockSpec(memory_space=pl.ANY), pl.BlockSpec(memory_space=pl.ANY)),
        scratch_shapes=[pltpu.VMEM((N_DEV, W_IN_SHARD, D_MODEL), BF16), pltpu.VMEM((D_IN_PAD, D_MODEL), BF16),
                        pltpu.VMEM((HALO, D_POOL), F32), pltpu.VMEM((HALO, 3 * D_DN), F32),
                        pltpu.VMEM((N_DEV,) + w_out_blk.shape, BF16),
                        pltpu.SemaphoreType.DMA, pltpu.SemaphoreType.DMA((7,)), pltpu.SemaphoreType.DMA((7,))],
        compiler_params=_cp(("arbitrary",)),
    )(n_all, pool_w, pool_scale, conv_full, alog_lane, dtb_lane, w_out_blk, g_in)


def _dn_block(q, k, v, gcol, bcol, state, known=None):
    nb, n, d = q.shape
    ii = lax.broadcasted_iota(jnp.int32, (n, n), 0)
    jj = lax.broadcasted_iota(jnp.int32, (n, n), 1)
    lower = ii >= jj
    eye = (ii == jj).astype(F32)
    g_row = jnp.sum(eye * gcol, axis=1, keepdims=True)
    gc_col = jnp.sum(jnp.where(lower, g_row, 0.0), axis=2, keepdims=True)
    gc_row = jnp.sum(eye * gc_col, axis=1, keepdims=True)
    decay = jnp.where(lower, jnp.exp(jnp.where(lower, gc_col - gc_row, 0.0)), 0.0)
    kb = k * bcol
    vb = v * bcol
    qs = q * (DN_HEAD_DIM ** -0.5)
    egc = jnp.exp(gc_col)
    kq = jnp.concatenate([kb, qs], axis=1)
    vk = jnp.concatenate([vb, kb * egc], axis=2)
    if known is None:
        akq = _mm(kq, k, 1, 1, *_DN_PREC["akq"])
    else:
        akq = _mm_known(kq, k, known[0][:, :, :n].astype(F32), 1, 1, _DN_PREC["akq"][1])
    a = jnp.where(ii > jj, akq[:, :n] * decay, 0.0)
    qk = akq[:, n:] * decay
    if known is None:
        t = _tri_inv(a, *_DN_PREC["inv"])
        uw = _mm(t, vk, 1, 0, *_DN_PREC["uw"])
    else:
        t = _tri_inv_known(a, known[0][:, :n, n:].astype(F32), _DN_PREC["inv"][1])
        uw = _mm_known(t, vk, known[1], 1, 0, _DN_PREC["uw"][1])
    pre = jnp.concatenate([akq, jnp.concatenate([t, jnp.zeros_like(t)], axis=1)], axis=2)
    wq = jnp.concatenate([uw[:, :, d:], qs * egc], axis=1)
    g_last = gc_col[:, n - 1:n, :]
    k_dec = k * jnp.exp(g_last - gc_col)
    e_last = jnp.exp(g_last)
    os_, starts = [], []
    for c in range(nb // DN_HEADS):
        sl = slice(c * DN_HEADS, (c + 1) * DN_HEADS)
        if known is not None and c > 0:
            state = _use_known(state, known[2][sl])
        starts.append(state)
        ws = _mm(wq[sl], state, 1, 0, *_DN_PREC["ws"])
        v_new = uw[sl, :, :d] - ws[:, :n]
        os_.append(ws[:, n:] + _mm(qk[sl], v_new, 1, 0, *_DN_PREC["ov"]))
        state = state * e_last[sl] + _mm(k_dec[sl], v_new, 0, 0, *_DN_PREC["st"])
    return jnp.concatenate(os_, axis=0), state, (pre, uw, jnp.concatenate(starts, axis=0))


def _gated_norm(o, dz, nw):
    parts = []
    for h in range(DN_HEADS):
        oh = o[:, h * DN_HEAD_DIM:(h + 1) * DN_HEAD_DIM]
        parts.append(oh * lax.rsqrt(jnp.mean(oh * oh, axis=-1, keepdims=True) + NORM_EPS) * nw)
    return jnp.concatenate(parts, axis=1) * _silu(dz)


def _dn_block_args(gc, q_ref, k_ref, v_ref, gb_ref):
    qs, ks, vs, gs, bs = [], [], [], [], []
    for cc in range(gc):
        r = slice(cc * CHUNK, (cc + 1) * CHUNK)
        gbv = gb_ref[r, :]
        for h in range(DN_HEADS):
            cols = slice(h * DN_HEAD_DIM, (h + 1) * DN_HEAD_DIM)
            qs.append(q_ref[r, cols])
            ks.append(k_ref[r, cols])
            vs.append(v_ref[r, cols])
            gs.append(gbv[:, DN_HEADS + h:DN_HEADS + h + 1])
            bs.append(gbv[:, h:h + 1])
    return tuple(jnp.stack(t, axis=0) for t in (qs, ks, vs, gs, bs))


def _dn_scan_fwd(qn, kn, vv, gb, gc):
    s = qn.shape[0]
    nchunk = s // CHUNK
    rows = gc * CHUNK

    def body(q_ref, k_ref, v_ref, gb_ref, y_ref, ss_ref, pre_ref, uw_ref, state):
        @pl.when(pl.program_id(0) == 0)
        def _():
            state[...] = jnp.zeros_like(state)
        q, k, v, gcol, bcol = _dn_block_args(gc, q_ref, k_ref, v_ref, gb_ref)
        y, new, (pre, uw, starts) = _dn_block(q, k, v, gcol, bcol, state[...])
        state[...] = new
        ss_ref[...] = starts
        pre_ref[...] = pre
        uw_ref[...] = uw
        for cc in range(gc):
            for h in range(DN_HEADS):
                y_ref[cc * CHUNK:(cc + 1) * CHUNK, h * DN_HEAD_DIM:(h + 1) * DN_HEAD_DIM] = y[cc * DN_HEADS + h]

    tile = pl.BlockSpec((rows, D_DN), lambda i: (i, 0))
    return pl.pallas_call(
        body, name="dn_scan_fwd", grid=(nchunk // gc,),
        out_shape=(jax.ShapeDtypeStruct((s, D_DN), F32),
                   jax.ShapeDtypeStruct((nchunk * DN_HEADS, DN_HEAD_DIM, DN_HEAD_DIM), F32),
                   jax.ShapeDtypeStruct((nchunk * DN_HEADS, 2 * CHUNK, 2 * CHUNK), F32),
                   jax.ShapeDtypeStruct((nchunk * DN_HEADS, CHUNK, 2 * DN_HEAD_DIM), F32)),
        in_specs=[tile, tile, tile, pl.BlockSpec((rows, 128), lambda i: (i, 0))],
        out_specs=(tile, pl.BlockSpec((gc * DN_HEADS, DN_HEAD_DIM, DN_HEAD_DIM), lambda i: (i, 0, 0)),
                   pl.BlockSpec((gc * DN_HEADS, 2 * CHUNK, 2 * CHUNK), lambda i: (i, 0, 0)),
                   pl.BlockSpec((gc * DN_HEADS, CHUNK, 2 * DN_HEAD_DIM), lambda i: (i, 0, 0))),
        scratch_shapes=[pltpu.VMEM((DN_HEADS, DN_HEAD_DIM, DN_HEAD_DIM), F32)],
        compiler_params=_cp(("arbitrary",)),
    )(qn, kn, vv, gb)


def _out_proj_loss(y_pool, o_dn, proj_main, dn_norm_w, x2, tgt, w_out_full, fnw, tm):
    s = x2.shape[0]

    def body(yp_ref, o_ref, dz_ref, nw_ref, x_ref, t_ref, wo_ref, fw_ref,
             dh_ref, dyp_ref, do_ref, ddz_ref, gwo_ref, gfw_ref, loss_ref, dnw_ref):
        @pl.when(pl.program_id(0) == 0)
        def _():
            gwo_ref[...] = jnp.zeros_like(gwo_ref)
            gfw_ref[...] = jnp.zeros_like(gfw_ref)
            loss_ref[...] = jnp.zeros_like(loss_ref)
            dnw_ref[...] = jnp.zeros_like(dnw_ref)
        y_dn, gate_vjp = jax.vjp(_gated_norm, o_ref[...], dz_ref[...], nw_ref[...])
        y = jnp.concatenate([yp_ref[...], y_dn], axis=1).astype(BF16)
        wo = wo_ref[...]
        h = x_ref[...] + jnp.dot(y, wo, preferred_element_type=F32)
        hn, r = _rms_hat(h)
        fw = fw_ref[...]
        err = hn * fw - t_ref[...]
        loss_ref[...] += 0.5 * jnp.sum(jnp.sum(err * err, axis=-1, keepdims=True) / D_MODEL, axis=0, keepdims=True)
        dout = err / D_MODEL
        gfw_ref[...] += jnp.sum(dout * hn, axis=0, keepdims=True)
        dhn = dout * fw
        dh = r * (dhn - hn * jnp.mean(dhn * hn, axis=-1, keepdims=True))
        dh_ref[...] = dh
        dhb = dh.astype(BF16)
        dy = _dot_nt_bf(dhb, wo)
        dyp_ref[...] = dy[:, :D_POOL]
        do, ddz, dnw = gate_vjp(dy[:, D_POOL:])
        do_ref[...] = do
        ddz_ref[...] = ddz
        dnw_ref[...] += dnw
        gwo_ref[...] += _dot_tn_bf(y, dhb)

    half = pl.BlockSpec((tm, D_POOL), lambda i: (i, 0))
    full = pl.BlockSpec((tm, D_MODEL), lambda i: (i, 0))
    lrow = pl.BlockSpec((1, 128), lambda i: (0, 0))
    return pl.pallas_call(
        body, name="out_proj_loss", grid=(s // tm,),
        out_shape=(jax.ShapeDtypeStruct((s, D_MODEL), F32), jax.ShapeDtypeStruct((s, D_POOL), F32),
                   jax.ShapeDtypeStruct((s, D_DN), F32), jax.ShapeDtypeStruct((s, D_DN), F32),
                   jax.ShapeDtypeStruct((D_MODEL, D_MODEL), F32),
                   jax.ShapeDtypeStruct((1, D_MODEL), F32), jax.ShapeDtypeStruct((1, 128), F32),
                   jax.ShapeDtypeStruct((1, 128), F32)),
        in_specs=[half, half, pl.BlockSpec((tm, D_DN), lambda i: (i, 5)), lrow, full, full,
                  pl.BlockSpec((D_MODEL, D_MODEL), lambda i: (0, 0)), pl.BlockSpec((1, D_MODEL), lambda i: (0, 0))],
        out_specs=(full, half, half, half, pl.BlockSpec((D_MODEL, D_MODEL), lambda i: (0, 0)),
                   pl.BlockSpec((1, D_MODEL), lambda i: (0, 0)), lrow, lrow),
        compiler_params=_cp(("arbitrary",)),
    )(y_pool, o_dn, proj_main, dn_norm_w, x2, tgt, w_out_full, fnw)


def _dn_scan_bwd(qn, kn, vv, gb, states, pre, uw, do_dn, p_out, gc):
    s = qn.shape[0]
    nchunk = s // CHUNK
    nstep = nchunk // gc
    rows = gc * CHUNK

    def body(q_ref, k_ref, v_ref, gb_ref, ss_ref, pre_ref, uw_ref, dy_ref, po_ref,
             dq_ref, dk_ref, dv_ref, dgb_ref, gro_ref, dstate, po_send, po_recv, rs_send, rs_recv):
        @pl.when(pl.program_id(0) == 0)
        def _():
            dstate[...] = jnp.zeros_like(dstate)
            po_send[...] = po_ref[...].astype(BF16)
            _direct_scatter_start(po_send, po_recv, rs_send, rs_recv)

        @pl.when(pl.program_id(0) == nstep - 1)
        def _():
            _direct_scatter_wait(po_send, po_recv, rs_send, rs_recv)
            total = po_ref[_dev_index(*_mesh_pos())]
            for k in range(N_DEV - 1):
                total = total + po_recv[k].astype(F32)
            gro_ref[...] = total
        lane = lax.broadcasted_iota(jnp.int32, (CHUNK, 128), 1)
        q, k, v, gcol, bcol = _dn_block_args(gc, q_ref, k_ref, v_ref, gb_ref)
        dy = jnp.stack([dy_ref[cc * CHUNK:(cc + 1) * CHUNK, h * DN_HEAD_DIM:(h + 1) * DN_HEAD_DIM]
                        for cc in range(gc) for h in range(DN_HEADS)], axis=0)
        known = (pre_ref[...], uw_ref[...], ss_ref[...])
        _, vjp = jax.vjp(lambda *a: _dn_block(*a, known=known)[:2], q, k, v, gcol, bcol, ss_ref[:DN_HEADS])
        dq, dk, dv, dg, db, dst = vjp((dy, dstate[...]))
        dstate[...] = dst
        for cc in range(gc):
            r = slice(cc * CHUNK, (cc + 1) * CHUNK)
            dgb = jnp.zeros((CHUNK, 128), F32)
            for h in range(DN_HEADS):
                b = cc * DN_HEADS + h
                cols = slice(h * DN_HEAD_DIM, (h + 1) * DN_HEAD_DIM)
                for ref, val in zip((dq_ref, dk_ref, dv_ref), (dq, dk, dv)):
                    ref[r, cols] = val[b]
                dgb = dgb + jnp.where(lane == h, db[b], 0.0) + jnp.where(lane == DN_HEADS + h, dg[b], 0.0)
            dgb_ref[r, :] = dgb

    rev = lambda i: (nstep - 1 - i, 0)
    tile = pl.BlockSpec((rows, D_DN), rev)
    lanes = pl.BlockSpec((rows, 128), rev)
    return pl.pallas_call(
        body, name="dn_scan_bwd", grid=(nstep,),
        out_shape=(jax.ShapeDtypeStruct((s, D_DN), F32),) * 3
        + (jax.ShapeDtypeStruct((s, 128), F32), jax.ShapeDtypeStruct(p_out.shape[1:], F32)),
        in_specs=[tile, tile, tile, lanes,
                  pl.BlockSpec((gc * DN_HEADS, DN_HEAD_DIM, DN_HEAD_DIM), lambda i: (nstep - 1 - i, 0, 0)),
                  pl.BlockSpec((gc * DN_HEADS, 2 * CHUNK, 2 * CHUNK), lambda i: (nstep - 1 - i, 0, 0)),
                  pl.BlockSpec((gc * DN_HEADS, CHUNK, 2 * DN_HEAD_DIM), lambda i: (nstep - 1 - i, 0, 0)), tile,
                  pl.BlockSpec(memory_space=pltpu.VMEM)],
        out_specs=(tile, tile, tile, lanes, pl.BlockSpec(memory_space=pltpu.VMEM)),
        scratch_shapes=[pltpu.VMEM((DN_HEADS, DN_HEAD_DIM, DN_HEAD_DIM), F32),
                        pltpu.VMEM(p_out.shape, BF16), pltpu.VMEM((N_DEV - 1,) + p_out.shape[1:], BF16),
                        pltpu.SemaphoreType.DMA((7,)), pltpu.SemaphoreType.DMA((7,))],
        compiler_params=_cp(("arbitrary",)),
    )(qn, kn, vv, gb, states, pre, uw, do_dn, p_out)


def _back(proj_main, proj_ba, dyp, dqn, dkn, dvv, dgb, ddz, x2, dh, norm_w, pool_w, pool_scale, conv_full,
          alog_lane, dtb_lane, wt_full, tm):
    s = x2.shape[0]
    nstep = s // tm
    per = tm // HALO

    def body(u_ref, z_ref, q_ref, k_ref, v_ref, up_ref, qp_ref, kp_ref, vp_ref, ba_ref,
             dyp_ref, dqn_ref, dkn_ref, dvv_ref, dgb_ref, ddz_ref, x_ref, dh_ref,
             nw_ref, pw_ref, ps_ref, cw_ref, al_ref, db_ref, wt_hbm,
             gx_ref, p_hbm, gnw_ref, dpw_ref, dps_ref, dcw_ref, dal_ref, ddb_ref,
             wt_vmem, acc, blk, head_dc, head_dw, sem, osem):
        j = pl.program_id(0)
        i = nstep - 1 - j

        @pl.when(j == 0)
        def _():
            cp = pltpu.make_async_copy(wt_hbm, wt_vmem, sem)
            cp.start()
            acc[...] = jnp.zeros_like(acc)
            for ref in (gnw_ref, dpw_ref, dps_ref, dcw_ref, dal_ref, ddb_ref, head_dc, head_dw):
                ref[...] = jnp.zeros_like(ref)
            cp.wait()

        u = u_ref[...]
        z = z_ref[...]
        dy = dyp_ref[...]
        ps = ps_ref[...]
        mixes = _pool_mix(u, _prev_tail(up_ref, i), i, tm)
        sg = jax.nn.sigmoid(z)
        sz = z * sg
        dsz = sg * (1.0 + z * (1.0 - sg))
        dzs, dwins = [], []
        for gi, w in enumerate(POOL_WINDOWS):
            cols = slice(gi * POOL_GROUP, (gi + 1) * POOL_GROUP)
            mixw = _dot_bf(mixes[gi], pw_ref[gi])
            dmixw = dy[:, cols] * ps[:, cols] * sz[:, cols]
            dps_ref[:, cols] += jnp.sum(dy[:, cols] * mixw * sz[:, cols], axis=0, keepdims=True)
            dzs.append(dy[:, cols] * mixw * ps[:, cols] * dsz[:, cols])
            dpw_ref[gi] += _dot_tn_bf(mixes[gi], dmixw)
            dwins.append(_dot_nt_bf(dmixw, pw_ref[gi]) / _pool_counts(i, tm, w))
        dzp = jnp.concatenate(dzs, axis=1)
        dw = jnp.concatenate(dwins, axis=1)
        win = jnp.concatenate([dw, head_dw[...]], axis=0)
        m = win.shape[0]
        dups = []
        for gi, w in enumerate(POOL_WINDOWS):
            win = win + pltpu.roll(win, m - w // 2, 0)
            cols = slice(gi * POOL_GROUP, (gi + 1) * POOL_GROUP)
            dups.append(win[:tm, :POOL_GROUP] - dw[:, cols] * _pool_counts(i, tm, w))
            if gi + 1 < len(POOL_WINDOWS):
                win = win[:, POOL_GROUP:]
        dup = jnp.concatenate(dups, axis=1)
        head_dw[...] = dw[:HALO, :]

        curs = (q_ref[...], k_ref[...], v_ref[...])
        tails = (_prev_tail(qp_ref, i), _prev_tail(kp_ref, i), _prev_tail(vp_ref, i))
        taps = [_conv_taps(curs[c], tails[c]) for c in range(3)]
        ys = [_conv_of_taps(taps[c], cw_ref[:, c * D_DN:(c + 1) * D_DN]) for c in range(3)]
        _, vjp = jax.vjp(_post_conv, *ys)
        dys = vjp((dqn_ref[...], dkn_ref[...], dvv_ref[...]))
        dxs = []
        for c, dyc in enumerate(dys):
            cols = slice(c * D_DN, (c + 1) * D_DN)
            w4 = cw_ref[:, cols]
            for sft in range(CONV_WIDTH):
                row = CONV_WIDTH - 1 - sft
                dcw_ref[row:row + 1, cols] += jnp.sum(dyc * taps[c][sft], axis=0, keepdims=True)
            head = head_dc[:, cols]
            dx = dyc * w4[CONV_WIDTH - 1:CONV_WIDTH, :]
            for sft in range(1, CONV_WIDTH):
                dx = dx + _shift_up(dyc, head, sft) * w4[CONV_WIDTH - 1 - sft:CONV_WIDTH - sft, :]
            dxs.append(dx)
            head_dc[:, cols] = dyc[:HALO, :]
        _, gvjp = jax.vjp(_gates, ba_ref[...], al_ref[...], db_ref[...])
        dba, dal, ddb = gvjp(dgb_ref[...])
        dal_ref[...] += dal
        ddb_ref[...] += ddb

        dbab = dba.astype(BF16)
        xhat, r = _rms_hat(x_ref[...])
        nw = nw_ref[...]
        n = (xhat * nw).astype(BF16)
        acc[D_MAIN:, :] += _dot_tn_bf(dbab, n)
        dn = jnp.dot(dbab, wt_vmem[D_MAIN:, :], preferred_element_type=F32)
        for cb, d in enumerate((dup, dzp, dxs[0], dxs[1], dxs[2], ddz_ref[...])):
            rows = slice(cb * D_POOL, (cb + 1) * D_POOL)
            dpart = d.astype(BF16)
            acc[rows, :] += _dot_tn_bf(dpart, n)
            dn = dn + jnp.dot(dpart, wt_vmem[rows, :], preferred_element_type=F32)
        gnw_ref[...] += jnp.sum(dn * xhat, axis=0, keepdims=True)
        dxh = dn * nw
        gx_ref[...] = dh_ref[...] + r * (dxh - xhat * jnp.mean(dxh * xhat, axis=-1, keepdims=True))

        @pl.when(j == nstep - 1)
        def _():
            def out(d):
                return pltpu.make_async_copy(blk.at[d % 2], p_hbm.at[d], osem.at[d % 2])
            for d in range(N_DEV):
                if d >= 2:
                    out(d - 2).wait()
                blk[d % 2] = acc[W_IN_SHARD * d:W_IN_SHARD * (d + 1), :]
                out(d).start()
            out(N_DEV - 2).wait()
            out(N_DEV - 1).wait()

    def col(c):
        return pl.BlockSpec((tm, D_POOL), lambda j: (nstep - 1 - j, c))

    def halo(c):
        return pl.BlockSpec((HALO, D_POOL), lambda j: (jnp.maximum((nstep - 1 - j) * per - 1, 0), c))

    rev = lambda j: (nstep - 1 - j, 0)
    part = pl.BlockSpec((tm, D_POOL), rev)
    lanes = pl.BlockSpec((tm, 128), rev)
    full = pl.BlockSpec((tm, D_MODEL), rev)
    row = pl.BlockSpec((1, D_MODEL), lambda j: (0, 0))
    lrow = pl.BlockSpec((1, 128), lambda j: (0, 0))
    pw = pl.BlockSpec((4, POOL_GROUP, POOL_GROUP), lambda j: (0, 0, 0))
    psp = pl.BlockSpec((1, D_POOL), lambda j: (0, 0))
    cw = pl.BlockSpec((CONV_WIDTH, 3 * D_DN), lambda j: (0, 0))
    return pl.pallas_call(
        body, name="back", grid=(nstep,),
        out_shape=(jax.ShapeDtypeStruct((s, D_MODEL), F32),
                   jax.ShapeDtypeStruct((N_DEV, W_IN_SHARD, D_MODEL), F32), jax.ShapeDtypeStruct((1, D_MODEL), F32),
                   jax.ShapeDtypeStruct((4, POOL_GROUP, POOL_GROUP), F32), jax.ShapeDtypeStruct((1, D_POOL), F32),
                   jax.ShapeDtypeStruct((CONV_WIDTH, 3 * D_DN), F32),
                   jax.ShapeDtypeStruct((1, 128), F32), jax.ShapeDtypeStruct((1, 128), F32)),
        in_specs=[col(0), col(1), col(2), col(3), col(4), halo(0), halo(2), halo(3), halo(4), lanes,
                  part, part, part, part, lanes, part, full, full,
                  row, pw, psp, cw, lrow, lrow, pl.BlockSpec(memory_space=pl.ANY)],
        out_specs=(full, pl.BlockSpec(memory_space=pl.ANY), row, pw, psp, cw, lrow, lrow),
        scratch_shapes=[pltpu.VMEM((D_IN_PAD, D_MODEL), BF16), pltpu.VMEM((D_IN_PAD, D_MODEL), F32),
                        pltpu.VMEM((2, W_IN_SHARD, D_MODEL), F32),
                        pltpu.VMEM((HALO, 3 * D_DN), F32), pltpu.VMEM((HALO, D_POOL), F32),
                        pltpu.SemaphoreType.DMA, pltpu.SemaphoreType.DMA((2,))],
        compiler_params=_cp(("arbitrary",)),
    )(proj_main, proj_main, proj_main, proj_main, proj_main, proj_main, proj_main, proj_main, proj_main, proj_ba,
      dyp, dqn, dkn, dvv, dgb, ddz, x2, dh, norm_w, pool_w, pool_scale, conv_full, alog_lane, dtb_lane, wt_full)


def _adamw_math(w, g, m, v):
    m = ADAM_B1 * m + (1.0 - ADAM_B1) * g
    v = ADAM_B2 * v + (1.0 - ADAM_B2) * (g * g)
    m_hat = m / (1.0 - ADAM_B1 ** ADAM_STEP)
    v_hat = v / (1.0 - ADAM_B2 ** ADAM_STEP)
    delta = -ADAM_LR * (m_hat / (jnp.sqrt(v_hat) + ADAM_EPS) + ADAM_WD * w)
    return delta, m, v


def _adamw_replicated(gath_a, gath_b, pool, rows):
    nrow = len(rows)

    def body(*refs):
        ga_ref, gb_ref = refs[:2]
        ins = refs[2:2 + 3 * (nrow + 1)]
        outs = refs[2 + 3 * (nrow + 1):]

        def total(ref):
            g = ref[0]
            for d in range(1, N_DEV):
                g = g + ref[d]
            return g

        def update(g, wmv, o):
            w, m, v = (r[...] for r in wmv)
            dl, nm, nv = _adamw_math(w, g, m, v)
            o[0][...] = g
            o[1][...] = dl
            o[2][...] = nm
            o[3][...] = nv

        update(total(ga_ref), ins[:3], outs[:4])
        gb = total(gb_ref)
        for r in range(nrow):
            n = ins[3 * (r + 1)].shape[1]
            update(gb[r:r + 1, :n], ins[3 * (r + 1):3 * (r + 2)], outs[4 * (r + 1):4 * (r + 2)])
        outs[4 * (nrow + 1)][...] = gb[nrow:nrow + 1, 0:1]

    flat = list(pool) + [a for wmv in rows for a in wmv]
    out_shape = ((jax.ShapeDtypeStruct(pool[0].shape, F32),) * 4
                 + tuple(jax.ShapeDtypeStruct(wmv[0].shape, F32) for wmv in rows for _ in range(4))
                 + (jax.ShapeDtypeStruct((1, 1), F32),))
    res = pl.pallas_call(body, name="adamw_replicated", out_shape=out_shape, compiler_params=_cp())(
        gath_a, gath_b, *flat)
    return [res[4 * k:4 * k + 4] for k in range(nrow + 1)], res[-1]


_ROW_ORDER = ("norm_w", "final_norm_w", "pool_scale", "dn_norm_w", "a_log", "dt_bias")


def _pack_rows(vectors):
    out = [jnp.pad(v.reshape(-1), (0, D_MODEL - v.size)) for v in vectors]
    out += [jnp.zeros((D_MODEL,), F32)] * (8 - len(out))
    return jnp.stack(out, axis=0)


def _lane_row(vec4, start):
    return jnp.pad(vec4.reshape(-1), (start, 128 - start - vec4.size)).reshape(1, 128)


def kernel(x, norm_w, w_in, pool_w, pool_scale, conv_w, a_log, dt_bias, dn_norm_w, w_out, final_norm_w, loss_target, m_norm_w, m_w_in, m_pool_w, m_pool_scale, m_conv_w, m_a_log, m_dt_bias, m_dn_norm_w, m_w_out, m_final_norm_w, v_norm_w, v_w_in, v_pool_w, v_pool_scale, v_conv_w, v_a_log, v_dt_bias, v_dn_norm_w, v_w_out, v_final_norm_w):
    s = x.shape[1]
    tm = min(512, s)
    tmb = min(256, s)
    x2 = x[0]
    tgt = loss_target[0]
    def to_flat(a):
        return a[0].reshape(FLAT_ROWS, 128, W_IN_SHARD).transpose(2, 0, 1).reshape(W_IN_SHARD * FLAT_ROWS, 128)

    def from_flat(f):
        return f.reshape(W_IN_SHARD, FLAT_ROWS, 128).transpose(1, 2, 0).reshape(1, D_MODEL, W_IN_SHARD)

    wf, m_wf, v_wf = to_flat(w_in), to_flat(m_w_in), to_flat(v_w_in)

    g_in, g_conv, n_all = _gather_weights(wf, conv_w[0], x2, norm_w, tm)
    conv_full = g_conv.transpose(1, 0, 2).reshape(CONV_WIDTH, 3 * D_DN)
    alog_lane = _lane_row(a_log, DN_HEADS)
    dtb_lane = _lane_row(dt_bias, DN_HEADS)
    fnw = final_norm_w.reshape(1, D_MODEL)

    proj_main, proj_ba, y_pool, qn, kn, vv, gb, g_out, wt_full = _front(
        n_all, g_in, pool_w[0], pool_scale, conv_full, alog_lane, dtb_lane, w_out[0], tm)
    w_out_full = g_out.reshape(D_MODEL, D_MODEL)
    o_dn, states, dn_pre, dn_uw = _dn_scan_fwd(qn, kn, vv, gb, DN_CHUNKS_PER_STEP)

    dh, dyp, do_dn, ddz, g_wout, g_fnw, loss_part, g_dnw = _out_proj_loss(
        y_pool, o_dn, proj_main, dn_norm_w, x2, tgt, w_out_full, fnw, tm)
    p_out = g_wout.reshape(N_DEV, D_MODEL // N_DEV, D_MODEL)
    dqn, dkn, dvv, dgb, gr_out = _dn_scan_bwd(qn, kn, vv, gb, states, dn_pre, dn_uw, do_dn, p_out, DN_CHUNKS_PER_STEP)
    grad_x2, p_in, g_nw, g_pw, g_ps, g_conv_full, g_al, g_db = _back(
        proj_main, proj_ba, dyp, dqn, dkn, dvv, dgb, ddz, x2, dh, norm_w, pool_w[0], pool_scale, conv_full,
        alog_lane, dtb_lane, wt_full, tmb)

    p_conv = g_conv_full.reshape(CONV_WIDTH, N_DEV, 3 * D_DN // N_DEV).transpose(1, 0, 2)
    pack_a = g_pw.reshape(4 * POOL_GROUP, POOL_GROUP)
    pack_b = _pack_rows([g_nw, g_fnw, g_ps, g_dnw, g_al[0, DN_HEADS:2 * DN_HEADS], g_db[0, DN_HEADS:2 * DN_HEADS],
                         loss_part[0, :1]])
    (r_in, r_conv), r_out, gath_a, gath_b = _reduce_grads(
        (p_in, p_conv), pack_a, pack_b, [(wf, m_wf, v_wf), (conv_w[0], m_conv_w[0], v_conv_w[0])],
        (w_out[0], gr_out, m_w_out[0], v_w_out[0]))
    r_out = (gr_out,) + r_out

    flat = lambda a: a.reshape(4 * POOL_GROUP, POOL_GROUP)
    row = lambda a: a.reshape(1, -1)
    vecs = {"norm_w": (norm_w, m_norm_w, v_norm_w), "final_norm_w": (final_norm_w, m_final_norm_w, v_final_norm_w),
            "pool_scale": (pool_scale, m_pool_scale, v_pool_scale), "dn_norm_w": (dn_norm_w, m_dn_norm_w, v_dn_norm_w),
            "a_log": (a_log, m_a_log, v_a_log), "dt_bias": (dt_bias, m_dt_bias, v_dt_bias)}
    res, loss = _adamw_replicated(gath_a, gath_b, (flat(pool_w), flat(m_pool_w), flat(v_pool_w)),
                                  [tuple(row(a) for a in vecs[nm]) for nm in _ROW_ORDER])
    r_pool = res[0]
    r_vec = dict(zip(_ROW_ORDER, res[1:]))

    def group(k):
        vec = lambda nm: r_vec[nm][k].reshape(vecs[nm][0].shape)
        return (vec("norm_w"), from_flat(r_in[k]), r_pool[k].reshape(pool_w.shape), vec("pool_scale"), r_conv[k][None],
                vec("a_log"), vec("dt_bias"), vec("dn_norm_w"), r_out[k][None], vec("final_norm_w"))

    return (loss[0, 0], grad_x2[None], *group(0), *group(1), *group(2), *group(3))
```

```python
import functools

import jax
import jax.numpy as jnp
from jax import lax
from jax.experimental import pallas as pl
from jax.experimental.pallas import tpu as pltpu

F32 = jnp.float32
BF16 = jnp.bfloat16
HI = lax.Precision.HIGHEST
MESH = pl.DeviceIdType.MESH

D_MODEL = 1024
D_POOL = 512
D_DN = 512
POOL_WINDOWS = (2, 4, 8, 16)
POOL_GROUP = 128
DN_HEADS = 4
DN_HEAD_DIM = 128
CONV_WIDTH = 4
CHUNK = 64
NORM_EPS = 1e-6
D_IN = 3080
D_MAIN = 3072
FLAT_ROWS = D_MODEL // 128
D_IN_PAD = D_MAIN + 128
N_DEV = 8
W_IN_SHARD = D_IN // N_DEV
HALO = 16
DN_CHUNKS_PER_STEP = 8

ADAM_LR = 0.001
ADAM_B1 = 0.9
ADAM_B2 = 0.999
ADAM_EPS = 1e-08
ADAM_WD = 0.01
ADAM_STEP = 10

VMEM_LIMIT = 56 * 1024 * 1024
def _cp(sem=None, vmem=VMEM_LIMIT):
    kw = {"vmem_limit_bytes": vmem}
    if sem is not None:
        kw["dimension_semantics"] = sem
    return pltpu.CompilerParams(**kw)


def _dot_bf(a, b):
    return jnp.dot(a.astype(BF16), b.astype(BF16), preferred_element_type=F32)


def _dot_nt_bf(a, b):
    return lax.dot_general(a.astype(BF16), b.astype(BF16), (((1,), (1,)), ((), ())), preferred_element_type=F32)


def _dot_tn_bf(a, b):
    return lax.dot_general(a.astype(BF16), b.astype(BF16), (((0,), (0,)), ((), ())), preferred_element_type=F32)


def _mm_raw(a, b, ca, cb, prec):
    off = a.ndim - 2
    dn = (((ca + off,), (cb + off,)), ((0,), (0,)) if off else ((), ()))
    if prec == "hi":
        return lax.dot_general(a, b, dn, precision=HI, preferred_element_type=F32)
    ah, bh = a.astype(BF16), b.astype(BF16)
    out = lax.dot_general(ah, bh, dn, preferred_element_type=F32)
    if prec == "x3":
        al = (a - ah.astype(F32)).astype(BF16)
        bl = (b - bh.astype(F32)).astype(BF16)
        out = out + lax.dot_general(ah, bl, dn, preferred_element_type=F32)
        out = out + lax.dot_general(al, bh, dn, preferred_element_type=F32)
    return out


@functools.partial(jax.custom_vjp, nondiff_argnums=(2, 3, 4, 5))
def _mm(a, b, ca, cb, prec, bprec):
    return _mm_raw(a, b, ca, cb, prec)


def _mm_fwd(a, b, ca, cb, prec, bprec):
    return _mm_raw(a, b, ca, cb, prec), (a, b)


def _mm_bwd(ca, cb, prec, bprec, res, dc):
    a, b = res
    da = _mm_raw(dc, b, 1, 1 - cb, bprec) if ca == 1 else _mm_raw(b, dc, 1 - cb, 1, bprec)
    db = _mm_raw(a, dc, 1 - ca, 0, bprec) if cb == 0 else _mm_raw(dc, a, 0, 1 - ca, bprec)
    return da, db


_mm.defvjp(_mm_fwd, _mm_bwd)


@functools.partial(jax.custom_vjp, nondiff_argnums=(1, 2))
def _tri_inv(a, prec, bprec):
    n = a.shape[-1]
    ii = lax.broadcasted_iota(jnp.int32, (n, n), 0)
    jj = lax.broadcasted_iota(jnp.int32, (n, n), 1)
    p = (ii == jj).astype(F32) - a
    b = _mm_raw(a, a, 1, 0, prec)
    for _ in range(4):
        pb = _mm_raw(jnp.concatenate([p, b], axis=-2), b, 1, 0, prec)
        p = p + pb[..., :n, :]
        b = pb[..., n:, :]
    return p + _mm_raw(p, b, 1, 0, prec)


def _tri_inv_fwd(a, prec, bprec):
    t = _tri_inv(a, prec, bprec)
    return t, t


def _tri_inv_bwd(prec, bprec, t, dt):
    return (-_mm_raw(_mm_raw(t, dt, 0, 0, bprec), t, 1, 1, bprec),)


_tri_inv.defvjp(_tri_inv_fwd, _tri_inv_bwd)

@functools.partial(jax.custom_vjp, nondiff_argnums=(3, 4, 5))
def _mm_known(a, b, out, ca, cb, bprec):
    return out


def _mm_known_fwd(a, b, out, ca, cb, bprec):
    return out, (a, b)


def _mm_known_bwd(ca, cb, bprec, res, dc):
    return _mm_bwd(ca, cb, None, bprec, res, dc) + (jnp.zeros_like(dc),)


_mm_known.defvjp(_mm_known_fwd, _mm_known_bwd)


@jax.custom_vjp
def _use_known(x, known):
    return known


_use_known.defvjp(lambda x, known: (known, None), lambda _, g: (g, jnp.zeros_like(g)))


@functools.partial(jax.custom_vjp, nondiff_argnums=(2,))
def _tri_inv_known(a, t, bprec):
    return t


def _tri_inv_known_fwd(a, t, bprec):
    return t, t


def _tri_inv_known_bwd(bprec, t, dt):
    return _tri_inv_bwd(None, bprec, t, dt) + (jnp.zeros_like(dt),)


_tri_inv_known.defvjp(_tri_inv_known_fwd, _tri_inv_known_bwd)

_DN_PREC = {"akq": ("bf16", "bf16"), "inv": ("bf16", "bf16"), "uw": ("bf16", "bf16"), "ws": ("bf16", "bf16"),
            "ov": ("bf16", "bf16"), "st": ("bf16", "bf16")}


def _silu(x):
    return x * jax.nn.sigmoid(x)


def _softplus(x):
    pos = x > 0.0
    return jnp.where(pos, x, 0.0) + jnp.log1p(jnp.exp(jnp.where(pos, -x, x)))


def _mesh_pos():
    return lax.axis_index("x"), lax.axis_index("y"), lax.axis_index("c")


def _dev_index(x, y, c):
    return 4 * x + 2 * y + c


def _relay_order():
    x, y, c = _mesh_pos()
    n1 = (x + (1 - c) * (1 - 2 * x), y + c * (1 - 2 * y))
    n2 = (x + c * (1 - 2 * x), y + (1 - c) * (1 - 2 * y))
    return (x, y, c), (x, y, 1 - c), n1, n2, (1 - x, 1 - y)


def _all_gather_blocks(outs, send_sems, recv_sems, meanwhile=None):
    me, sibling, n1, n2, diag = _relay_order()
    c = me[2]

    def copy(a, k, block, to):
        rows = outs[a].at[_dev_index(*block)]
        return pltpu.make_async_remote_copy(src_ref=rows, dst_ref=rows, send_sem=send_sems.at[a, k],
                                            recv_sem=recv_sems.at[a, k], device_id=to, device_id_type=MESH)

    n = len(outs)
    started = []

    def start(cp):
        cp.start()
        started.append(cp)

    for a in range(n):
        start(copy(a, 1, me, (*n1, c)))
        start(copy(a, 2, me, (*n2, c)))
        start(copy(a, 0, me, sibling))
    if meanwhile is not None:
        meanwhile()
    for a in range(n):
        copy(a, 1, (*n1, c), me).wait_recv()
        start(copy(a, 3, (*n1, c), (*n2, c)))
        start(copy(a, 4, (*n1, c), sibling))
    for a in range(n):
        copy(a, 2, (*n2, c), me).wait_recv()
        start(copy(a, 5, (*n2, c), sibling))
        copy(a, 3, (*diag, c), me).wait_recv()
        start(copy(a, 6, (*diag, c), sibling))
    for a in range(n):
        copy(a, 0, sibling, me).wait_recv()
        copy(a, 4, (*n2, 1 - c), me).wait_recv()
        copy(a, 5, (*n1, 1 - c), me).wait_recv()
        copy(a, 6, (*diag, 1 - c), me).wait_recv()
    for cp in started:
        cp.wait_send()


def _peer_relations():
    x, y, c = _mesh_pos()
    flips = [(fx, fy, fc) for fx in (0, 1) for fy in (0, 1) for fc in (0, 1)][1:]
    peers = [(1 - x if fx else x, 1 - y if fy else y, 1 - c if fc else c) for fx, fy, fc in flips]
    return (x, y, c), peers


def _direct_gather_start(out_ref, send_sems, recv_sems):
    me, peers = _peer_relations()
    rows = out_ref.at[_dev_index(*me)]
    for k, peer in enumerate(peers):
        pltpu.make_async_remote_copy(src_ref=rows, dst_ref=rows, send_sem=send_sems.at[k], recv_sem=recv_sems.at[k],
                                     device_id=peer, device_id_type=MESH).start()


def _direct_gather_wait(out_ref, send_sems, recv_sems):
    me, peers = _peer_relations()
    for k, peer in enumerate(peers):
        rows = out_ref.at[_dev_index(*peer)]
        cp = pltpu.make_async_remote_copy(src_ref=rows, dst_ref=rows, send_sem=send_sems.at[k],
                                          recv_sem=recv_sems.at[k], device_id=peer, device_id_type=MESH)
        cp.wait_recv()
        cp.wait_send()


def _direct_scatter_start(send_ref, recv_ref, send_sems, recv_sems):
    me, peers = _peer_relations()
    for k, peer in enumerate(peers):
        pltpu.make_async_remote_copy(src_ref=send_ref.at[_dev_index(*peer)], dst_ref=recv_ref.at[k],
                                     send_sem=send_sems.at[k], recv_sem=recv_sems.at[k],
                                     device_id=peer, device_id_type=MESH).start()


def _direct_scatter_wait(send_ref, recv_ref, send_sems, recv_sems):
    me, peers = _peer_relations()
    for k, peer in enumerate(peers):
        cp = pltpu.make_async_remote_copy(src_ref=send_ref.at[_dev_index(*peer)], dst_ref=recv_ref.at[k],
                                          send_sem=send_sems.at[k], recv_sem=recv_sems.at[k],
                                          device_id=peer, device_id_type=MESH)
        cp.wait_recv()
        cp.wait_send()


def _gather_weights(w_in_flat, conv_blk, x2, norm_w, tn):
    nt = x2.shape[0] // tn

    def body(win_ref, conv_ref, x_hbm, nw_ref, gin_ref, gconv_ref, n_hbm, xbuf, nbuf, send_sems, recv_sems, xsem, nsem):
        x, y, c = _mesh_pos()
        me = _dev_index(x, y, c)
        for j in range(FLAT_ROWS):
            gin_ref[me, :, 128 * j:128 * (j + 1)] = win_ref[pl.ds(j, W_IN_SHARD, stride=FLAT_ROWS), :].astype(BF16)
        gconv_ref[me] = conv_ref[...]

        def norm_x():
            def load(t):
                return pltpu.make_async_copy(x_hbm.at[pl.ds(t * tn, tn), :], xbuf.at[t % 2], xsem.at[t % 2])

            def store(t):
                return pltpu.make_async_copy(nbuf.at[t % 2], n_hbm.at[pl.ds(t * tn, tn), :], nsem.at[t % 2])

            load(0).start()
            for t in range(nt):
                if t + 1 < nt:
                    load(t + 1).start()
                load(t).wait()
                if t >= 2:
                    store(t - 2).wait()
                xhat, _ = _rms_hat(xbuf[t % 2])
                nbuf[t % 2] = (xhat * nw_ref[...]).astype(BF16)
                store(t).start()
            for t in range(max(nt - 2, 0), nt):
                store(t).wait()

        _all_gather_blocks((gin_ref, gconv_ref), send_sems, recv_sems, meanwhile=norm_x)

    vm = pl.BlockSpec(memory_space=pltpu.VMEM)
    hbm = pl.BlockSpec(memory_space=pl.ANY)
    return pl.pallas_call(
        body, name="gather_weights",
        out_shape=(jax.ShapeDtypeStruct((N_DEV, W_IN_SHARD, D_MODEL), BF16),
                   jax.ShapeDtypeStruct((N_DEV,) + conv_blk.shape, F32),
                   jax.ShapeDtypeStruct(x2.shape, BF16)),
        in_specs=[vm, vm, hbm, vm], out_specs=(vm, vm, hbm),
        scratch_shapes=[pltpu.VMEM((2, tn, D_MODEL), F32), pltpu.VMEM((2, tn, D_MODEL), BF16),
                        pltpu.SemaphoreType.DMA((2, 7)), pltpu.SemaphoreType.DMA((2, 7)),
                        pltpu.SemaphoreType.DMA((2,)), pltpu.SemaphoreType.DMA((2,))],
        compiler_params=_cp(),
    )(w_in_flat, conv_blk, x2, norm_w)


def _reduce_grads(big, pack_a, pack_b):
    nb = len(big)

    def body(*refs):
        srcs, (pa_ref, pb_ref), outs, (ga_ref, gb_ref) = (
            refs[:nb], refs[nb:nb + 2], refs[nb + 2:2 * nb + 2], refs[2 * nb + 2:2 * nb + 4])
        scr = refs[2 * nb + 4:]
        r1s, r2s, sbs, sts = (scr[k * nb:(k + 1) * nb] for k in range(4))
        s1_send, s1_recv, s2_send, s2_recv, ag_send, ag_recv, st_sem = scr[4 * nb:]
        x, y, c = _mesh_pos()
        me = (x, y, c)
        sibling = (x, y, 1 - c)

        ga_ref[_dev_index(*me)] = pa_ref[...]
        gb_ref[_dev_index(*me)] = pb_ref[...]

        _, _, n1, n2, diag = _relay_order()
        order = ((n1, 1 + c), (diag, 3), (n2, 2 - c), ((x, y), 0))

        def p1(a, i, to):
            chip, slot = order[i]
            return pltpu.make_async_remote_copy(
                src_ref=srcs[a].at[_dev_index(*chip, 1 - c)], dst_ref=r1s[a].at[slot],
                send_sem=s1_send.at[a, slot], recv_sem=s1_recv.at[a, slot], device_id=to, device_id_type=MESH)

        def p1_landed(a, slot):
            return pltpu.make_async_remote_copy(
                src_ref=r1s[a].at[slot], dst_ref=r1s[a].at[slot], send_sem=s1_send.at[a, slot],
                recv_sem=s1_recv.at[a, slot], device_id=me, device_id_type=MESH)

        def p2(a, k, to):
            return pltpu.make_async_remote_copy(
                src_ref=sbs[a].at[k], dst_ref=r2s[a].at[k],
                send_sem=s2_send.at[a, k], recv_sem=s2_recv.at[a, k], device_id=to, device_id_type=MESH)

        def stage(a, i):
            return pltpu.make_async_copy(srcs[a].at[_dev_index(*order[i][0], c)], sts[a].at[i % 2], st_sem.at[a, i % 2])

        sends = [p1(a, i, sibling) for a in range(nb) for i in (2, 1, 0, 3)]
        for cp in sends:
            cp.start()

        def step_1():
            for a in range(nb):
                stage(a, 0).start()
                for i, (_, slot) in enumerate(order):
                    if i + 1 < len(order):
                        stage(a, i + 1).start()
                    stage(a, i).wait()
                    p1_landed(a, slot).wait_recv()
                    chip_sum = r1s[a][slot] + sts[a][i % 2]
                    if i < 2:
                        sbs[a][i] = chip_sum.astype(BF16)
                        sends.append(p2(a, i, (*n1, c)))
                        sends[-1].start()
                    else:
                        r1s[a][slot] = chip_sum

        _all_gather_blocks((ga_ref, gb_ref), ag_send, ag_recv, meanwhile=step_1)
        for a in range(nb):
            p2(a, 0, me).wait_recv()
            p2(a, 1, me).wait_recv()
            sbs[a][2] = (r1s[a][2 - c] + r2s[a][1].astype(F32)).astype(BF16)
            sends.append(p2(a, 2, (*n2, c)))
            sends[-1].start()
        for a in range(nb):
            p2(a, 2, me).wait_recv()
            outs[a][...] = (r1s[a][0] + r2s[a][0].astype(F32)) + r2s[a][2].astype(F32)
        for cp in sends:
            cp.wait_send()

    vm = pl.BlockSpec(memory_space=pltpu.VMEM)
    hbm = pl.BlockSpec(memory_space=pl.ANY)
    blk = [p.shape[1:] for p in big]
    scratch = ([pltpu.VMEM((4,) + b, F32) for b in blk] + [pltpu.VMEM((3,) + b, BF16) for b in blk]
               + [pltpu.VMEM((3,) + b, BF16) for b in blk] + [pltpu.VMEM((2,) + b, F32) for b in blk]
               + [pltpu.SemaphoreType.DMA((nb, 4)), pltpu.SemaphoreType.DMA((nb, 4)),
                  pltpu.SemaphoreType.DMA((nb, 3)), pltpu.SemaphoreType.DMA((nb, 3)),
                  pltpu.SemaphoreType.DMA((2, 7)), pltpu.SemaphoreType.DMA((2, 7)),
                  pltpu.SemaphoreType.DMA((nb, 2))])
    return pl.pallas_call(
        body, name="reduce_grads",
        out_shape=tuple(jax.ShapeDtypeStruct(b, F32) for b in blk)
        + (jax.ShapeDtypeStruct((N_DEV,) + pack_a.shape, F32), jax.ShapeDtypeStruct((N_DEV,) + pack_b.shape, F32)),
        in_specs=[hbm] * nb + [vm, vm], out_specs=(vm,) * (nb + 2),
        scratch_shapes=scratch,
        compiler_params=_cp(),
    )(*big, pack_a, pack_b)


def _rms_hat(xf):
    r = lax.rsqrt(jnp.mean(xf * xf, axis=-1, keepdims=True) + NORM_EPS)
    return xf * r, r


def _shift_up(cur, next_head, s):
    ext = jnp.concatenate([cur, next_head], axis=0)
    n = ext.shape[0]
    return pltpu.roll(ext, n - s, 0)[:cur.shape[0], :]


def _pool_counts(i, tp, w):
    t = i * tp + lax.broadcasted_iota(jnp.int32, (tp, 1), 0)
    return jnp.minimum(t + 1, w).astype(F32)


def _pool_mix(u, u_prev_tail, i, tp):
    win = jnp.concatenate([u_prev_tail, u], axis=0)
    mixes = []
    for gi, w in enumerate(POOL_WINDOWS):
        win = win + pltpu.roll(win, w // 2, 0)
        cols = slice(gi * POOL_GROUP, (gi + 1) * POOL_GROUP)
        mixes.append(win[HALO:, :POOL_GROUP] / _pool_counts(i, tp, w) - u[:, cols])
        if gi + 1 < len(POOL_WINDOWS):
            win = win[:, POOL_GROUP:]
    return mixes


def _prev_tail(ref, i):
    return jnp.where(i > 0, ref[...], 0.0)


def _conv_taps(cur, prev_tail):
    ext = jnp.concatenate([prev_tail, cur], axis=0)
    return [cur] + [pltpu.roll(ext, sft, 0)[HALO:, :] for sft in range(1, CONV_WIDTH)]


def _conv_of_taps(taps, w4):
    y = taps[0] * w4[CONV_WIDTH - 1:CONV_WIDTH, :]
    for sft in range(1, CONV_WIDTH):
        y = y + taps[sft] * w4[CONV_WIDTH - 1 - sft:CONV_WIDTH - sft, :]
    return y


def _conv_fwd(cur, prev_tail, w4):
    ext = jnp.concatenate([prev_tail, cur], axis=0)
    y = ext * w4[CONV_WIDTH - 1:CONV_WIDTH, :]
    for sft in range(1, CONV_WIDTH):
        y = y + pltpu.roll(ext, sft, 0) * w4[CONV_WIDTH - 1 - sft:CONV_WIDTH - sft, :]
    return y[HALO:, :]


def _l2n_heads(t):
    parts = []
    for h in range(DN_HEADS):
        th = t[:, h * DN_HEAD_DIM:(h + 1) * DN_HEAD_DIM]
        parts.append(th * lax.rsqrt(jnp.sum(th * th, axis=-1, keepdims=True) + NORM_EPS))
    return jnp.concatenate(parts, axis=1)


def _post_conv(yq, yk, yv):
    return _l2n_heads(_silu(yq)), _l2n_heads(_silu(yk)), _silu(yv)


def _gates(ba, alog_lane, dtb_lane):
    lane = lax.broadcasted_iota(jnp.int32, ba.shape, 1)
    beta = jax.nn.sigmoid(ba)
    g = -jnp.exp(alog_lane) * _softplus(ba + dtb_lane)
    return jnp.where(lane < DN_HEADS, beta, jnp.where(lane < 2 * DN_HEADS, g, 0.0))


def _front(n_all, g_in, pool_w, pool_scale, conv_full, alog_lane, dtb_lane, w_out_blk, tm):
    s = n_all.shape[0]

    def body(n_ref, pw_ref, ps_ref, cw_ref, al_ref, db_ref, wo_ref, g_hbm,
             pm_ref, pb_ref, yp_ref, qn_ref, kn_ref, vv_ref, gb_ref, gwo_hbm, wt_hbm,
             g_vmem, wt_vmem, tail_u, tail_qkv, gwo_ref, sem, wo_send, wo_recv):
        i = pl.program_id(0)

        @pl.when(i == 0)
        def _():
            gwo_ref[_dev_index(*_mesh_pos())] = wo_ref[...].astype(BF16)
            _direct_gather_start(gwo_ref, wo_send, wo_recv)
            cp = pltpu.make_async_copy(g_hbm, g_vmem, sem)
            cp.start()
            wt_vmem[D_MAIN:, :] = jnp.zeros((D_IN_PAD - D_MAIN, D_MODEL), BF16)
            tail_u[...] = jnp.zeros_like(tail_u)
            tail_qkv[...] = jnp.zeros_like(tail_qkv)
            cp.wait()
            for d in range(N_DEV):
                wt_vmem[W_IN_SHARD * d:W_IN_SHARD * (d + 1), :] = g_vmem[d]
            out = pltpu.make_async_copy(wt_vmem, wt_hbm, sem)
            out.start()
            out.wait()
        n = n_ref[...]
        pm_ref[...] = _dot_nt_bf(n, wt_vmem[:D_MAIN, :])
        pb = _dot_nt_bf(n, wt_vmem[D_MAIN:, :])
        pb_ref[...] = pb
        u = pm_ref[:, :D_POOL]
        mixes = _pool_mix(u, tail_u[...], i, tm)
        tail_u[...] = u[tm - HALO:, :]
        gate = ps_ref[...] * _silu(pm_ref[:, D_POOL:2 * D_POOL])
        for gi in range(4):
            cols = slice(gi * POOL_GROUP, (gi + 1) * POOL_GROUP)
            yp_ref[:, cols] = _dot_bf(mixes[gi], pw_ref[gi]) * gate[:, cols]
        ys = []
        for c in range(3):
            cols = slice(c * D_DN, (c + 1) * D_DN)
            cur = pm_ref[:, 2 * D_POOL + c * D_DN:2 * D_POOL + (c + 1) * D_DN]
            ys.append(_conv_fwd(cur, tail_qkv[:, cols], cw_ref[:, cols]))
            tail_qkv[:, cols] = cur[tm - HALO:, :]
        qn, kn, vv = _post_conv(*ys)
        qn_ref[...] = qn
        kn_ref[...] = kn
        vv_ref[...] = vv
        gb_ref[...] = _gates(pb, al_ref[...], db_ref[...])

        @pl.when(i == s // tm - 1)
        def _():
            _direct_gather_wait(gwo_ref, wo_send, wo_recv)
            out = pltpu.make_async_copy(gwo_ref, gwo_hbm, sem)
            out.start()
            out.wait()

    tile = pl.BlockSpec((tm, D_DN), lambda i: (i, 0))
    lanes = pl.BlockSpec((tm, 128), lambda i: (i, 0))
    row = pl.BlockSpec((1, 128), lambda i: (0, 0))
    return pl.pallas_call(
        body, name="front", grid=(s // tm,),
        out_shape=(jax.ShapeDtypeStruct((s, D_MAIN), F32), jax.ShapeDtypeStruct((s, 128), F32),
                   jax.ShapeDtypeStruct((s, D_POOL), F32), jax.ShapeDtypeStruct((s, D_DN), F32),
                   jax.ShapeDtypeStruct((s, D_DN), F32), jax.ShapeDtypeStruct((s, D_DN), F32),
                   jax.ShapeDtypeStruct((s, 128), F32), jax.ShapeDtypeStruct((N_DEV,) + w_out_blk.shape, BF16),
                   jax.ShapeDtypeStruct((D_IN_PAD, D_MODEL), BF16)),
        in_specs=[pl.BlockSpec((tm, D_MODEL), lambda i: (i, 0)),
                  pl.BlockSpec((4, POOL_GROUP, POOL_GROUP), lambda i: (0, 0, 0)),
                  pl.BlockSpec((1, D_POOL), lambda i: (0, 0)),
                  pl.BlockSpec((CONV_WIDTH, 3 * D_DN), lambda i: (0, 0)), row, row,
                  pl.BlockSpec(memory_space=pltpu.VMEM), pl.BlockSpec(memory_space=pl.ANY)],
        out_specs=(pl.BlockSpec((tm, D_MAIN), lambda i: (i, 0)), lanes, tile, tile, tile, tile, lanes,
                   pl.BlockSpec(memory_space=pl.ANY), pl.BlockSpec(memory_space=pl.ANY)),
        scratch_shapes=[pltpu.VMEM((N_DEV, W_IN_SHARD, D_MODEL), BF16), pltpu.VMEM((D_IN_PAD, D_MODEL), BF16),
                        pltpu.VMEM((HALO, D_POOL), F32), pltpu.VMEM((HALO, 3 * D_DN), F32),
                        pltpu.VMEM((N_DEV,) + w_out_blk.shape, BF16),
                        pltpu.SemaphoreType.DMA, pltpu.SemaphoreType.DMA((7,)), pltpu.SemaphoreType.DMA((7,))],
        compiler_params=_cp(("arbitrary",)),
    )(n_all, pool_w, pool_scale, conv_full, alog_lane, dtb_lane, w_out_blk, g_in)


def _dn_block(q, k, v, gcol, bcol, state, known=None):
    nb, n, d = q.shape
    ii = lax.broadcasted_iota(jnp.int32, (n, n), 0)
    jj = lax.broadcasted_iota(jnp.int32, (n, n), 1)
    lower = ii >= jj
    eye = (ii == jj).astype(F32)
    g_row = jnp.sum(eye * gcol, axis=1, keepdims=True)
    gc_col = jnp.sum(jnp.where(lower, g_row, 0.0), axis=2, keepdims=True)
    gc_row = jnp.sum(eye * gc_col, axis=1, keepdims=True)
    decay = jnp.where(lower, jnp.exp(jnp.where(lower, gc_col - gc_row, 0.0)), 0.0)
    kb = k * bcol
    vb = v * bcol
    qs = q * (DN_HEAD_DIM ** -0.5)
    egc = jnp.exp(gc_col)
    kq = jnp.concatenate([kb, qs], axis=1)
    vk = jnp.concatenate([vb, kb * egc], axis=2)
    if known is None:
        akq = _mm(kq, k, 1, 1, *_DN_PREC["akq"])
    else:
        akq = _mm_known(kq, k, known[0][:, :, :n].astype(F32), 1, 1, _DN_PREC["akq"][1])
    a = jnp.where(ii > jj, akq[:, :n] * decay, 0.0)
    qk = akq[:, n:] * decay
    if known is None:
        t = _tri_inv(a, *_DN_PREC["inv"])
        uw = _mm(t, vk, 1, 0, *_DN_PREC["uw"])
    else:
        t = _tri_inv_known(a, known[0][:, :n, n:].astype(F32), _DN_PREC["inv"][1])
        uw = _mm_known(t, vk, known[1], 1, 0, _DN_PREC["uw"][1])
    pre = jnp.concatenate([akq, jnp.concatenate([t, jnp.zeros_like(t)], axis=1)], axis=2)
    wq = jnp.concatenate([uw[:, :, d:], qs * egc], axis=1)
    g_last = gc_col[:, n - 1:n, :]
    k_dec = k * jnp.exp(g_last - gc_col)
    e_last = jnp.exp(g_last)
    os_, starts = [], []
    for c in range(nb // DN_HEADS):
        sl = slice(c * DN_HEADS, (c + 1) * DN_HEADS)
        if known is not None and c > 0:
            state = _use_known(state, known[2][sl])
        starts.append(state)
        ws = _mm(wq[sl], state, 1, 0, *_DN_PREC["ws"])
        v_new = uw[sl, :, :d] - ws[:, :n]
        os_.append(ws[:, n:] + _mm(qk[sl], v_new, 1, 0, *_DN_PREC["ov"]))
        state = state * e_last[sl] + _mm(k_dec[sl], v_new, 0, 0, *_DN_PREC["st"])
    return jnp.concatenate(os_, axis=0), state, (pre, uw, jnp.concatenate(starts, axis=0))


def _gated_norm(o, dz, nw):
    parts = []
    for h in range(DN_HEADS):
        oh = o[:, h * DN_HEAD_DIM:(h + 1) * DN_HEAD_DIM]
        parts.append(oh * lax.rsqrt(jnp.mean(oh * oh, axis=-1, keepdims=True) + NORM_EPS) * nw)
    return jnp.concatenate(parts, axis=1) * _silu(dz)


def _dn_block_args(gc, q_ref, k_ref, v_ref, gb_ref):
    qs, ks, vs, gs, bs = [], [], [], [], []
    for cc in range(gc):
        r = slice(cc * CHUNK, (cc + 1) * CHUNK)
        gbv = gb_ref[r, :]
        for h in range(DN_HEADS):
            cols = slice(h * DN_HEAD_DIM, (h + 1) * DN_HEAD_DIM)
            qs.append(q_ref[r, cols])
            ks.append(k_ref[r, cols])
            vs.append(v_ref[r, cols])
            gs.append(gbv[:, DN_HEADS + h:DN_HEADS + h + 1])
            bs.append(gbv[:, h:h + 1])
    return tuple(jnp.stack(t, axis=0) for t in (qs, ks, vs, gs, bs))


def _dn_scan_fwd(qn, kn, vv, gb, gc):
    s = qn.shape[0]
    nchunk = s // CHUNK
    rows = gc * CHUNK

    def body(q_ref, k_ref, v_ref, gb_ref, y_ref, ss_ref, pre_ref, uw_ref, state):
        @pl.when(pl.program_id(0) == 0)
        def _():
            state[...] = jnp.zeros_like(state)
        q, k, v, gcol, bcol = _dn_block_args(gc, q_ref, k_ref, v_ref, gb_ref)
        y, new, (pre, uw, starts) = _dn_block(q, k, v, gcol, bcol, state[...])
        state[...] = new
        ss_ref[...] = starts
        pre_ref[...] = pre
        uw_ref[...] = uw
        for cc in range(gc):
            for h in range(DN_HEADS):
                y_ref[cc * CHUNK:(cc + 1) * CHUNK, h * DN_HEAD_DIM:(h + 1) * DN_HEAD_DIM] = y[cc * DN_HEADS + h]

    tile = pl.BlockSpec((rows, D_DN), lambda i: (i, 0))
    return pl.pallas_call(
        body, name="dn_scan_fwd", grid=(nchunk // gc,),
        out_shape=(jax.ShapeDtypeStruct((s, D_DN), F32),
                   jax.ShapeDtypeStruct((nchunk * DN_HEADS, DN_HEAD_DIM, DN_HEAD_DIM), F32),
                   jax.ShapeDtypeStruct((nchunk * DN_HEADS, 2 * CHUNK, 2 * CHUNK), F32),
                   jax.ShapeDtypeStruct((nchunk * DN_HEADS, CHUNK, 2 * DN_HEAD_DIM), F32)),
        in_specs=[tile, tile, tile, pl.BlockSpec((rows, 128), lambda i: (i, 0))],
        out_specs=(tile, pl.BlockSpec((gc * DN_HEADS, DN_HEAD_DIM, DN_HEAD_DIM), lambda i: (i, 0, 0)),
                   pl.BlockSpec((gc * DN_HEADS, 2 * CHUNK, 2 * CHUNK), lambda i: (i, 0, 0)),
                   pl.BlockSpec((gc * DN_HEADS, CHUNK, 2 * DN_HEAD_DIM), lambda i: (i, 0, 0))),
        scratch_shapes=[pltpu.VMEM((DN_HEADS, DN_HEAD_DIM, DN_HEAD_DIM), F32)],
        compiler_params=_cp(("arbitrary",)),
    )(qn, kn, vv, gb)


def _out_proj_loss(y_pool, o_dn, proj_main, dn_norm_w, x2, tgt, w_out_full, fnw, tm):
    s = x2.shape[0]

    def body(yp_ref, o_ref, dz_ref, nw_ref, x_ref, t_ref, wo_ref, fw_ref,
             dh_ref, dyp_ref, do_ref, ddz_ref, gwo_ref, gfw_ref, loss_ref, dnw_ref):
        @pl.when(pl.program_id(0) == 0)
        def _():
            gwo_ref[...] = jnp.zeros_like(gwo_ref)
            gfw_ref[...] = jnp.zeros_like(gfw_ref)
            loss_ref[...] = jnp.zeros_like(loss_ref)
            dnw_ref[...] = jnp.zeros_like(dnw_ref)
        y_dn, gate_vjp = jax.vjp(_gated_norm, o_ref[...], dz_ref[...], nw_ref[...])
        y = jnp.concatenate([yp_ref[...], y_dn], axis=1).astype(BF16)
        wo = wo_ref[...]
        h = x_ref[...] + jnp.dot(y, wo, preferred_element_type=F32)
        hn, r = _rms_hat(h)
        fw = fw_ref[...]
        err = hn * fw - t_ref[...]
        loss_ref[...] += 0.5 * jnp.sum(jnp.sum(err * err, axis=-1, keepdims=True) / D_MODEL, axis=0, keepdims=True)
        dout = err / D_MODEL
        gfw_ref[...] += jnp.sum(dout * hn, axis=0, keepdims=True)
        dhn = dout * fw
        dh = r * (dhn - hn * jnp.mean(dhn * hn, axis=-1, keepdims=True))
        dh_ref[...] = dh
        dhb = dh.astype(BF16)
        dy = _dot_nt_bf(dhb, wo)
        dyp_ref[...] = dy[:, :D_POOL]
        do, ddz, dnw = gate_vjp(dy[:, D_POOL:])
        do_ref[...] = do
        ddz_ref[...] = ddz
        dnw_ref[...] += dnw
        gwo_ref[...] += _dot_tn_bf(y, dhb)

    half = pl.BlockSpec((tm, D_POOL), lambda i: (i, 0))
    full = pl.BlockSpec((tm, D_MODEL), lambda i: (i, 0))
    lrow = pl.BlockSpec((1, 128), lambda i: (0, 0))
    return pl.pallas_call(
        body, name="out_proj_loss", grid=(s // tm,),
        out_shape=(jax.ShapeDtypeStruct((s, D_MODEL), F32), jax.ShapeDtypeStruct((s, D_POOL), F32),
                   jax.ShapeDtypeStruct((s, D_DN), F32), jax.ShapeDtypeStruct((s, D_DN), F32),
                   jax.ShapeDtypeStruct((D_MODEL, D_MODEL), F32),
                   jax.ShapeDtypeStruct((1, D_MODEL), F32), jax.ShapeDtypeStruct((1, 128), F32),
                   jax.ShapeDtypeStruct((1, 128), F32)),
        in_specs=[half, half, pl.BlockSpec((tm, D_DN), lambda i: (i, 5)), lrow, full, full,
                  pl.BlockSpec((D_MODEL, D_MODEL), lambda i: (0, 0)), pl.BlockSpec((1, D_MODEL), lambda i: (0, 0))],
        out_specs=(full, half, half, half, pl.BlockSpec((D_MODEL, D_MODEL), lambda i: (0, 0)),
                   pl.BlockSpec((1, D_MODEL), lambda i: (0, 0)), lrow, lrow),
        compiler_params=_cp(("arbitrary",)),
    )(y_pool, o_dn, proj_main, dn_norm_w, x2, tgt, w_out_full, fnw)


def _dn_scan_bwd(qn, kn, vv, gb, states, pre, uw, do_dn, p_out, gc):
    s = qn.shape[0]
    nchunk = s // CHUNK
    nstep = nchunk // gc
    rows = gc * CHUNK

    def body(q_ref, k_ref, v_ref, gb_ref, ss_ref, pre_ref, uw_ref, dy_ref, po_ref,
             dq_ref, dk_ref, dv_ref, dgb_ref, gro_ref, dstate, po_send, po_recv, rs_send, rs_recv):
        @pl.when(pl.program_id(0) == 0)
        def _():
            dstate[...] = jnp.zeros_like(dstate)
            po_send[...] = po_ref[...].astype(BF16)
            _direct_scatter_start(po_send, po_recv, rs_send, rs_recv)

        @pl.when(pl.program_id(0) == nstep - 1)
        def _():
            _direct_scatter_wait(po_send, po_recv, rs_send, rs_recv)
            total = po_ref[_dev_index(*_mesh_pos())]
            for k in range(N_DEV - 1):
                total = total + po_recv[k].astype(F32)
            gro_ref[...] = total
        lane = lax.broadcasted_iota(jnp.int32, (CHUNK, 128), 1)
        q, k, v, gcol, bcol = _dn_block_args(gc, q_ref, k_ref, v_ref, gb_ref)
        dy = jnp.stack([dy_ref[cc * CHUNK:(cc + 1) * CHUNK, h * DN_HEAD_DIM:(h + 1) * DN_HEAD_DIM]
                        for cc in range(gc) for h in range(DN_HEADS)], axis=0)
        known = (pre_ref[...], uw_ref[...], ss_ref[...])
        _, vjp = jax.vjp(lambda *a: _dn_block(*a, known=known)[:2], q, k, v, gcol, bcol, ss_ref[:DN_HEADS])
        dq, dk, dv, dg, db, dst = vjp((dy, dstate[...]))
        dstate[...] = dst
        for cc in range(gc):
            r = slice(cc * CHUNK, (cc + 1) * CHUNK)
            dgb = jnp.zeros((CHUNK, 128), F32)
            for h in range(DN_HEADS):
                b = cc * DN_HEADS + h
                cols = slice(h * DN_HEAD_DIM, (h + 1) * DN_HEAD_DIM)
                for ref, val in zip((dq_ref, dk_ref, dv_ref), (dq, dk, dv)):
                    ref[r, cols] = val[b]
                dgb = dgb + jnp.where(lane == h, db[b], 0.0) + jnp.where(lane == DN_HEADS + h, dg[b], 0.0)
            dgb_ref[r, :] = dgb

    rev = lambda i: (nstep - 1 - i, 0)
    tile = pl.BlockSpec((rows, D_DN), rev)
    lanes = pl.BlockSpec((rows, 128), rev)
    return pl.pallas_call(
        body, name="dn_scan_bwd", grid=(nstep,),
        out_shape=(jax.ShapeDtypeStruct((s, D_DN), F32),) * 3
        + (jax.ShapeDtypeStruct((s, 128), F32), jax.ShapeDtypeStruct(p_out.shape[1:], F32)),
        in_specs=[tile, tile, tile, lanes,
                  pl.BlockSpec((gc * DN_HEADS, DN_HEAD_DIM, DN_HEAD_DIM), lambda i: (nstep - 1 - i, 0, 0)),
                  pl.BlockSpec((gc * DN_HEADS, 2 * CHUNK, 2 * CHUNK), lambda i: (nstep - 1 - i, 0, 0)),
                  pl.BlockSpec((gc * DN_HEADS, CHUNK, 2 * DN_HEAD_DIM), lambda i: (nstep - 1 - i, 0, 0)), tile,
                  pl.BlockSpec(memory_space=pltpu.VMEM)],
        out_specs=(tile, tile, tile, lanes, pl.BlockSpec(memory_space=pltpu.VMEM)),
        scratch_shapes=[pltpu.VMEM((DN_HEADS, DN_HEAD_DIM, DN_HEAD_DIM), F32),
                        pltpu.VMEM(p_out.shape, BF16), pltpu.VMEM((N_DEV - 1,) + p_out.shape[1:], BF16),
                        pltpu.SemaphoreType.DMA((7,)), pltpu.SemaphoreType.DMA((7,))],
        compiler_params=_cp(("arbitrary",)),
    )(qn, kn, vv, gb, states, pre, uw, do_dn, p_out)


def _back(proj_main, proj_ba, dyp, dqn, dkn, dvv, dgb, ddz, x2, dh, norm_w, pool_w, pool_scale, conv_full,
          alog_lane, dtb_lane, wt_full, tm):
    s = x2.shape[0]
    nstep = s // tm
    per = tm // HALO

    def body(u_ref, z_ref, q_ref, k_ref, v_ref, up_ref, qp_ref, kp_ref, vp_ref, ba_ref,
             dyp_ref, dqn_ref, dkn_ref, dvv_ref, dgb_ref, ddz_ref, x_ref, dh_ref,
             nw_ref, pw_ref, ps_ref, cw_ref, al_ref, db_ref, wt_hbm,
             gx_ref, p_hbm, gnw_ref, dpw_ref, dps_ref, dcw_ref, dal_ref, ddb_ref,
             wt_vmem, acc, blk, head_dc, head_dw, sem, osem):
        j = pl.program_id(0)
        i = nstep - 1 - j

        @pl.when(j == 0)
        def _():
            cp = pltpu.make_async_copy(wt_hbm, wt_vmem, sem)
            cp.start()
            acc[...] = jnp.zeros_like(acc)
            for ref in (gnw_ref, dpw_ref, dps_ref, dcw_ref, dal_ref, ddb_ref, head_dc, head_dw):
                ref[...] = jnp.zeros_like(ref)
            cp.wait()

        u = u_ref[...]
        z = z_ref[...]
        dy = dyp_ref[...]
        ps = ps_ref[...]
        mixes = _pool_mix(u, _prev_tail(up_ref, i), i, tm)
        sg = jax.nn.sigmoid(z)
        sz = z * sg
        dsz = sg * (1.0 + z * (1.0 - sg))
        dzs, dwins = [], []
        for gi, w in enumerate(POOL_WINDOWS):
            cols = slice(gi * POOL_GROUP, (gi + 1) * POOL_GROUP)
            mixw = _dot_bf(mixes[gi], pw_ref[gi])
            dmixw = dy[:, cols] * ps[:, cols] * sz[:, cols]
            dps_ref[:, cols] += jnp.sum(dy[:, cols] * mixw * sz[:, cols], axis=0, keepdims=True)
            dzs.append(dy[:, cols] * mixw * ps[:, cols] * dsz[:, cols])
            dpw_ref[gi] += _dot_tn_bf(mixes[gi], dmixw)
            dwins.append(_dot_nt_bf(dmixw, pw_ref[gi]) / _pool_counts(i, tm, w))
        dzp = jnp.concatenate(dzs, axis=1)
        dw = jnp.concatenate(dwins, axis=1)
        win = jnp.concatenate([dw, head_dw[...]], axis=0)
        m = win.shape[0]
        dups = []
        for gi, w in enumerate(POOL_WINDOWS):
            win = win + pltpu.roll(win, m - w // 2, 0)
            cols = slice(gi * POOL_GROUP, (gi + 1) * POOL_GROUP)
            dups.append(win[:tm, :POOL_GROUP] - dw[:, cols] * _pool_counts(i, tm, w))
            if gi + 1 < len(POOL_WINDOWS):
                win = win[:, POOL_GROUP:]
        dup = jnp.concatenate(dups, axis=1)
        head_dw[...] = dw[:HALO, :]

        curs = (q_ref[...], k_ref[...], v_ref[...])
        tails = (_prev_tail(qp_ref, i), _prev_tail(kp_ref, i), _prev_tail(vp_ref, i))
        taps = [_conv_taps(curs[c], tails[c]) for c in range(3)]
        ys = [_conv_of_taps(taps[c], cw_ref[:, c * D_DN:(c + 1) * D_DN]) for c in range(3)]
        _, vjp = jax.vjp(_post_conv, *ys)
        dys = vjp((dqn_ref[...], dkn_ref[...], dvv_ref[...]))
        dxs = []
        for c, dyc in enumerate(dys):
            cols = slice(c * D_DN, (c + 1) * D_DN)
            w4 = cw_ref[:, cols]
            for sft in range(CONV_WIDTH):
                row = CONV_WIDTH - 1 - sft
                dcw_ref[row:row + 1, cols] += jnp.sum(dyc * taps[c][sft], axis=0, keepdims=True)
            head = head_dc[:, cols]
            dx = dyc * w4[CONV_WIDTH - 1:CONV_WIDTH, :]
            for sft in range(1, CONV_WIDTH):
                dx = dx + _shift_up(dyc, head, sft) * w4[CONV_WIDTH - 1 - sft:CONV_WIDTH - sft, :]
            dxs.append(dx)
            head_dc[:, cols] = dyc[:HALO, :]
        _, gvjp = jax.vjp(_gates, ba_ref[...], al_ref[...], db_ref[...])
        dba, dal, ddb = gvjp(dgb_ref[...])
        dal_ref[...] += dal
        ddb_ref[...] += ddb

        dbab = dba.astype(BF16)
        xhat, r = _rms_hat(x_ref[...])
        nw = nw_ref[...]
        n = (xhat * nw).astype(BF16)
        acc[D_MAIN:, :] += _dot_tn_bf(dbab, n)
        dn = jnp.dot(dbab, wt_vmem[D_MAIN:, :], preferred_element_type=F32)
        for cb, d in enumerate((dup, dzp, dxs[0], dxs[1], dxs[2], ddz_ref[...])):
            rows = slice(cb * D_POOL, (cb + 1) * D_POOL)
            dpart = d.astype(BF16)
            acc[rows, :] += _dot_tn_bf(dpart, n)
            dn = dn + jnp.dot(dpart, wt_vmem[rows, :], preferred_element_type=F32)
        gnw_ref[...] += jnp.sum(dn * xhat, axis=0, keepdims=True)
        dxh = dn * nw
        gx_ref[...] = dh_ref[...] + r * (dxh - xhat * jnp.mean(dxh * xhat, axis=-1, keepdims=True))

        @pl.when(j == nstep - 1)
        def _():
            def out(d):
                return pltpu.make_async_copy(blk.at[d % 2], p_hbm.at[d], osem.at[d % 2])
            for d in range(N_DEV):
                if d >= 2:
                    out(d - 2).wait()
                blk[d % 2] = acc[W_IN_SHARD * d:W_IN_SHARD * (d + 1), :]
                out(d).start()
            out(N_DEV - 2).wait()
            out(N_DEV - 1).wait()

    def col(c):
        return pl.BlockSpec((tm, D_POOL), lambda j: (nstep - 1 - j, c))

    def halo(c):
        return pl.BlockSpec((HALO, D_POOL), lambda j: (jnp.maximum((nstep - 1 - j) * per - 1, 0), c))

    rev = lambda j: (nstep - 1 - j, 0)
    part = pl.BlockSpec((tm, D_POOL), rev)
    lanes = pl.BlockSpec((tm, 128), rev)
    full = pl.BlockSpec((tm, D_MODEL), rev)
    row = pl.BlockSpec((1, D_MODEL), lambda j: (0, 0))
    lrow = pl.BlockSpec((1, 128), lambda j: (0, 0))
    pw = pl.BlockSpec((4, POOL_GROUP, POOL_GROUP), lambda j: (0, 0, 0))
    psp = pl.BlockSpec((1, D_POOL), lambda j: (0, 0))
    cw = pl.BlockSpec((CONV_WIDTH, 3 * D_DN), lambda j: (0, 0))
    return pl.pallas_call(
        body, name="back", grid=(nstep,),
        out_shape=(jax.ShapeDtypeStruct((s, D_MODEL), F32),
                   jax.ShapeDtypeStruct((N_DEV, W_IN_SHARD, D_MODEL), F32), jax.ShapeDtypeStruct((1, D_MODEL), F32),
                   jax.ShapeDtypeStruct((4, POOL_GROUP, POOL_GROUP), F32), jax.ShapeDtypeStruct((1, D_POOL), F32),
                   jax.ShapeDtypeStruct((CONV_WIDTH, 3 * D_DN), F32),
                   jax.ShapeDtypeStruct((1, 128), F32), jax.ShapeDtypeStruct((1, 128), F32)),
        in_specs=[col(0), col(1), col(2), col(3), col(4), halo(0), halo(2), halo(3), halo(4), lanes,
                  part, part, part, part, lanes, part, full, full,
                  row, pw, psp, cw, lrow, lrow, pl.BlockSpec(memory_space=pl.ANY)],
        out_specs=(full, pl.BlockSpec(memory_space=pl.ANY), row, pw, psp, cw, lrow, lrow),
        scratch_shapes=[pltpu.VMEM((D_IN_PAD, D_MODEL), BF16), pltpu.VMEM((D_IN_PAD, D_MODEL), F32),
                        pltpu.VMEM((2, W_IN_SHARD, D_MODEL), F32),
                        pltpu.VMEM((HALO, 3 * D_DN), F32), pltpu.VMEM((HALO, D_POOL), F32),
                        pltpu.SemaphoreType.DMA, pltpu.SemaphoreType.DMA((2,))],
        compiler_params=_cp(("arbitrary",)),
    )(proj_main, proj_main, proj_main, proj_main, proj_main, proj_main, proj_main, proj_main, proj_main, proj_ba,
      dyp, dqn, dkn, dvv, dgb, ddz, x2, dh, norm_w, pool_w, pool_scale, conv_full, alog_lane, dtb_lane, wt_full)


def _adamw_math(w, g, m, v):
    m = ADAM_B1 * m + (1.0 - ADAM_B1) * g
    v = ADAM_B2 * v + (1.0 - ADAM_B2) * (g * g)
    m_hat = m / (1.0 - ADAM_B1 ** ADAM_STEP)
    v_hat = v / (1.0 - ADAM_B2 ** ADAM_STEP)
    delta = -ADAM_LR * (m_hat / (jnp.sqrt(v_hat) + ADAM_EPS) + ADAM_WD * w)
    return delta, m, v


def _adamw_sharded(params):
    k = len(params)

    def body(*refs):
        ins, outs = refs[:4 * k], refs[4 * k:]
        for p in range(k):
            w_ref, g_ref, m_ref, v_ref = ins[4 * p:4 * p + 4]
            go_ref = outs[4 * p]
            if g_ref.shape == w_ref.shape:
                go_ref[...] = g_ref[...]
            else:
                for j in range(FLAT_ROWS):
                    go_ref[pl.ds(j, W_IN_SHARD, stride=FLAT_ROWS), :] = g_ref[:, 128 * j:128 * (j + 1)]
            d, nm, nv = _adamw_math(w_ref[...], go_ref[...], m_ref[...], v_ref[...])
            outs[4 * p + 1][...] = d
            outs[4 * p + 2][...] = nm
            outs[4 * p + 3][...] = nv

    flat = [a for p in params for a in p]
    out_shape = tuple(jax.ShapeDtypeStruct(p[0].shape, F32) for p in params for _ in range(4))
    res = pl.pallas_call(body, name="adamw_sharded", out_shape=out_shape, compiler_params=_cp())(*flat)
    return [tuple(res[4 * p:4 * p + 4]) for p in range(k)]


def _adamw_replicated(gath_a, gath_b, pool, rows):
    nrow = len(rows)

    def body(*refs):
        ga_ref, gb_ref = refs[:2]
        ins = refs[2:2 + 3 * (nrow + 1)]
        outs = refs[2 + 3 * (nrow + 1):]

        def total(ref):
            g = ref[0]
            for d in range(1, N_DEV):
                g = g + ref[d]
            return g

        def update(g, wmv, o):
            w, m, v = (r[...] for r in wmv)
            dl, nm, nv = _adamw_math(w, g, m, v)
            o[0][...] = g
            o[1][...] = dl
            o[2][...] = nm
            o[3][...] = nv

        update(total(ga_ref), ins[:3], outs[:4])
        gb = total(gb_ref)
        for r in range(nrow):
            n = ins[3 * (r + 1)].shape[1]
            update(gb[r:r + 1, :n], ins[3 * (r + 1):3 * (r + 2)], outs[4 * (r + 1):4 * (r + 2)])
        outs[4 * (nrow + 1)][...] = gb[nrow:nrow + 1, 0:1]

    flat = list(pool) + [a for wmv in rows for a in wmv]
    out_shape = ((jax.ShapeDtypeStruct(pool[0].shape, F32),) * 4
                 + tuple(jax.ShapeDtypeStruct(wmv[0].shape, F32) for wmv in rows for _ in range(4))
                 + (jax.ShapeDtypeStruct((1, 1), F32),))
    res = pl.pallas_call(body, name="adamw_replicated", out_shape=out_shape, compiler_params=_cp())(
        gath_a, gath_b, *flat)
    return [res[4 * k:4 * k + 4] for k in range(nrow + 1)], res[-1]


_ROW_ORDER = ("norm_w", "final_norm_w", "pool_scale", "dn_norm_w", "a_log", "dt_bias")


def _pack_rows(vectors):
    out = [jnp.pad(v.reshape(-1), (0, D_MODEL - v.size)) for v in vectors]
    out += [jnp.zeros((D_MODEL,), F32)] * (8 - len(out))
    return jnp.stack(out, axis=0)


def _lane_row(vec4, start):
    return jnp.pad(vec4.reshape(-1), (start, 128 - start - vec4.size)).reshape(1, 128)


def kernel(x, norm_w, w_in, pool_w, pool_scale, conv_w, a_log, dt_bias, dn_norm_w, w_out, final_norm_w, loss_target, m_norm_w, m_w_in, m_pool_w, m_pool_scale, m_conv_w, m_a_log, m_dt_bias, m_dn_norm_w, m_w_out, m_final_norm_w, v_norm_w, v_w_in, v_pool_w, v_pool_scale, v_conv_w, v_a_log, v_dt_bias, v_dn_norm_w, v_w_out, v_final_norm_w):
    s = x.shape[1]
    tm = min(512, s)
    tmb = min(256, s)
    x2 = x[0]
    tgt = loss_target[0]
    def to_flat(a):
        return a[0].reshape(FLAT_ROWS, 128, W_IN_SHARD).transpose(2, 0, 1).reshape(W_IN_SHARD * FLAT_ROWS, 128)

    def from_flat(f):
        return f.reshape(W_IN_SHARD, FLAT_ROWS, 128).transpose(1, 2, 0).reshape(1, D_MODEL, W_IN_SHARD)

    wf, m_wf, v_wf = to_flat(w_in), to_flat(m_w_in), to_flat(v_w_in)

    g_in, g_conv, n_all = _gather_weights(wf, conv_w[0], x2, norm_w, tm)
    conv_full = g_conv.transpose(1, 0, 2).reshape(CONV_WIDTH, 3 * D_DN)
    alog_lane = _lane_row(a_log, DN_HEADS)
    dtb_lane = _lane_row(dt_bias, DN_HEADS)
    fnw = final_norm_w.reshape(1, D_MODEL)

    proj_main, proj_ba, y_pool, qn, kn, vv, gb, g_out, wt_full = _front(
        n_all, g_in, pool_w[0], pool_scale, conv_full, alog_lane, dtb_lane, w_out[0], tm)
    w_out_full = g_out.reshape(D_MODEL, D_MODEL)
    o_dn, states, dn_pre, dn_uw = _dn_scan_fwd(qn, kn, vv, gb, DN_CHUNKS_PER_STEP)

    dh, dyp, do_dn, ddz, g_wout, g_fnw, loss_part, g_dnw = _out_proj_loss(
        y_pool, o_dn, proj_main, dn_norm_w, x2, tgt, w_out_full, fnw, tm)
    p_out = g_wout.reshape(N_DEV, D_MODEL // N_DEV, D_MODEL)
    dqn, dkn, dvv, dgb, gr_out = _dn_scan_bwd(qn, kn, vv, gb, states, dn_pre, dn_uw, do_dn, p_out, DN_CHUNKS_PER_STEP)
    grad_x2, p_in, g_nw, g_pw, g_ps, g_conv_full, g_al, g_db = _back(
        proj_main, proj_ba, dyp, dqn, dkn, dvv, dgb, ddz, x2, dh, norm_w, pool_w[0], pool_scale, conv_full,
        alog_lane, dtb_lane, wt_full, tmb)

    p_conv = g_conv_full.reshape(CONV_WIDTH, N_DEV, 3 * D_DN // N_DEV).transpose(1, 0, 2)
    pack_a = g_pw.reshape(4 * POOL_GROUP, POOL_GROUP)
    pack_b = _pack_rows([g_nw, g_fnw, g_ps, g_dnw, g_al[0, DN_HEADS:2 * DN_HEADS], g_db[0, DN_HEADS:2 * DN_HEADS],
                         loss_part[0, :1]])
    gr_in, gr_conv, gath_a, gath_b = _reduce_grads((p_in, p_conv), pack_a, pack_b)

    r_in, r_out, r_conv = _adamw_sharded([(wf, gr_in, m_wf, v_wf), (w_out[0], gr_out, m_w_out[0], v_w_out[0]),
                                          (conv_w[0], gr_conv, m_conv_w[0], v_conv_w[0])])
    flat = lambda a: a.reshape(4 * POOL_GROUP, POOL_GROUP)
    row = lambda a: a.reshape(1, -1)
    vecs = {"norm_w": (norm_w, m_norm_w, v_norm_w), "final_norm_w": (final_norm_w, m_final_norm_w, v_final_norm_w),
            "pool_scale": (pool_scale, m_pool_scale, v_pool_scale), "dn_norm_w": (dn_norm_w, m_dn_norm_w, v_dn_norm_w),
            "a_log": (a_log, m_a_log, v_a_log), "dt_bias": (dt_bias, m_dt_bias, v_dt_bias)}
    res, loss = _adamw_replicated(gath_a, gath_b, (flat(pool_w), flat(m_pool_w), flat(v_pool_w)),
                                  [tuple(row(a) for a in vecs[nm]) for nm in _ROW_ORDER])
    r_pool = res[0]
    r_vec = dict(zip(_ROW_ORDER, res[1:]))

    def group(k):
        vec = lambda nm: r_vec[nm][k].reshape(vecs[nm][0].shape)
        return (vec("norm_w"), from_flat(r_in[k]), r_pool[k].reshape(pool_w.shape), vec("pool_scale"), r_conv[k][None],
                vec("a_log"), vec("dt_bias"), vec("dn_norm_w"), r_out[k][None], vec("final_norm_w"))

    return (loss[0, 0], grad_x2[None], *group(0), *group(1), *group(2), *group(3))
```

```python
import functools

import jax
import jax.numpy as jnp
from jax import lax
from jax.experimental import pallas as pl
from jax.experimental.pallas import tpu as pltpu

F32 = jnp.float32
BF16 = jnp.bfloat16
HI = lax.Precision.HIGHEST
MESH = pl.DeviceIdType.MESH

D_MODEL = 1024
D_POOL = 512
D_DN = 512
POOL_WINDOWS = (2, 4, 8, 16)
POOL_GROUP = 128
DN_HEADS = 4
DN_HEAD_DIM = 128
CONV_WIDTH = 4
CHUNK = 64
NORM_EPS = 1e-6
D_IN = 3080
D_MAIN = 3072
FLAT_ROWS = D_MODEL // 128
D_IN_PAD = D_MAIN + 128
N_DEV = 8
W_IN_SHARD = D_IN // N_DEV
HALO = 16
DN_CHUNKS_PER_STEP = 8
ADAMW_STEPS = 5

ADAM_LR = 0.001
ADAM_B1 = 0.9
ADAM_B2 = 0.999
ADAM_EPS = 1e-08
ADAM_WD = 0.01
ADAM_STEP = 10

VMEM_LIMIT = 56 * 1024 * 1024
def _cp(sem=None, vmem=VMEM_LIMIT):
    kw = {"vmem_limit_bytes": vmem}
    if sem is not None:
        kw["dimension_semantics"] = sem
    return pltpu.CompilerParams(**kw)


def _dot_bf(a, b):
    return jnp.dot(a.astype(BF16), b.astype(BF16), preferred_element_type=F32)


def _dot_nt_bf(a, b):
    return lax.dot_general(a.astype(BF16), b.astype(BF16), (((1,), (1,)), ((), ())), preferred_element_type=F32)


def _dot_tn_bf(a, b):
    return lax.dot_general(a.astype(BF16), b.astype(BF16), (((0,), (0,)), ((), ())), preferred_element_type=F32)


def _mm_raw(a, b, ca, cb, prec):
    off = a.ndim - 2
    dn = (((ca + off,), (cb + off,)), ((0,), (0,)) if off else ((), ()))
    if prec == "hi":
        return lax.dot_general(a, b, dn, precision=HI, preferred_element_type=F32)
    ah, bh = a.astype(BF16), b.astype(BF16)
    out = lax.dot_general(ah, bh, dn, preferred_element_type=F32)
    if prec == "x3":
        al = (a - ah.astype(F32)).astype(BF16)
        bl = (b - bh.astype(F32)).astype(BF16)
        out = out + lax.dot_general(ah, bl, dn, preferred_element_type=F32)
        out = out + lax.dot_general(al, bh, dn, preferred_element_type=F32)
    return out


@functools.partial(jax.custom_vjp, nondiff_argnums=(2, 3, 4, 5))
def _mm(a, b, ca, cb, prec, bprec):
    return _mm_raw(a, b, ca, cb, prec)


def _mm_fwd(a, b, ca, cb, prec, bprec):
    return _mm_raw(a, b, ca, cb, prec), (a, b)


def _mm_bwd(ca, cb, prec, bprec, res, dc):
    a, b = res
    da = _mm_raw(dc, b, 1, 1 - cb, bprec) if ca == 1 else _mm_raw(b, dc, 1 - cb, 1, bprec)
    db = _mm_raw(a, dc, 1 - ca, 0, bprec) if cb == 0 else _mm_raw(dc, a, 0, 1 - ca, bprec)
    return da, db


_mm.defvjp(_mm_fwd, _mm_bwd)


@functools.partial(jax.custom_vjp, nondiff_argnums=(1, 2))
def _tri_inv(a, prec, bprec):
    n = a.shape[-1]
    ii = lax.broadcasted_iota(jnp.int32, (n, n), 0)
    jj = lax.broadcasted_iota(jnp.int32, (n, n), 1)
    p = (ii == jj).astype(F32) - a
    b = _mm_raw(a, a, 1, 0, prec)
    for _ in range(4):
        pb = _mm_raw(jnp.concatenate([p, b], axis=-2), b, 1, 0, prec)
        p = p + pb[..., :n, :]
        b = pb[..., n:, :]
    return p + _mm_raw(p, b, 1, 0, prec)


def _tri_inv_fwd(a, prec, bprec):
    t = _tri_inv(a, prec, bprec)
    return t, t


def _tri_inv_bwd(prec, bprec, t, dt):
    return (-_mm_raw(_mm_raw(t, dt, 0, 0, bprec), t, 1, 1, bprec),)


_tri_inv.defvjp(_tri_inv_fwd, _tri_inv_bwd)

@functools.partial(jax.custom_vjp, nondiff_argnums=(3, 4, 5))
def _mm_known(a, b, out, ca, cb, bprec):
    return out


def _mm_known_fwd(a, b, out, ca, cb, bprec):
    return out, (a, b)


def _mm_known_bwd(ca, cb, bprec, res, dc):
    return _mm_bwd(ca, cb, None, bprec, res, dc) + (jnp.zeros_like(dc),)


_mm_known.defvjp(_mm_known_fwd, _mm_known_bwd)


@jax.custom_vjp
def _use_known(x, known):
    return known


_use_known.defvjp(lambda x, known: (known, None), lambda _, g: (g, jnp.zeros_like(g)))


@functools.partial(jax.custom_vjp, nondiff_argnums=(2,))
def _tri_inv_known(a, t, bprec):
    return t


def _tri_inv_known_fwd(a, t, bprec):
    return t, t


def _tri_inv_known_bwd(bprec, t, dt):
    return _tri_inv_bwd(None, bprec, t, dt) + (jnp.zeros_like(dt),)


_tri_inv_known.defvjp(_tri_inv_known_fwd, _tri_inv_known_bwd)

_DN_PREC = {"akq": ("bf16", "bf16"), "inv": ("bf16", "bf16"), "uw": ("bf16", "bf16"), "ws": ("bf16", "bf16"),
            "ov": ("bf16", "bf16"), "st": ("bf16", "bf16")}


def _silu(x):
    return x * jax.nn.sigmoid(x)


def _softplus(x):
    pos = x > 0.0
    return jnp.where(pos, x, 0.0) + jnp.log1p(jnp.exp(jnp.where(pos, -x, x)))


def _mesh_pos():
    return lax.axis_index("x"), lax.axis_index("y"), lax.axis_index("c")


def _dev_index(x, y, c):
    return 4 * x + 2 * y + c


def _relay_order():
    x, y, c = _mesh_pos()
    n1 = (x + (1 - c) * (1 - 2 * x), y + c * (1 - 2 * y))
    n2 = (x + c * (1 - 2 * x), y + (1 - c) * (1 - 2 * y))
    return (x, y, c), (x, y, 1 - c), n1, n2, (1 - x, 1 - y)


def _all_gather_blocks(outs, send_sems, recv_sems, meanwhile=None):
    me, sibling, n1, n2, diag = _relay_order()
    c = me[2]

    def copy(a, k, block, to):
        rows = outs[a].at[_dev_index(*block)]
        return pltpu.make_async_remote_copy(src_ref=rows, dst_ref=rows, send_sem=send_sems.at[a, k],
                                            recv_sem=recv_sems.at[a, k], device_id=to, device_id_type=MESH)

    n = len(outs)
    started = []

    def start(cp):
        cp.start()
        started.append(cp)

    for a in range(n):
        start(copy(a, 1, me, (*n1, c)))
        start(copy(a, 2, me, (*n2, c)))
        start(copy(a, 0, me, sibling))
    if meanwhile is not None:
        meanwhile()
    for a in range(n):
        copy(a, 1, (*n1, c), me).wait_recv()
        start(copy(a, 3, (*n1, c), (*n2, c)))
        start(copy(a, 4, (*n1, c), sibling))
    for a in range(n):
        copy(a, 2, (*n2, c), me).wait_recv()
        start(copy(a, 5, (*n2, c), sibling))
        copy(a, 3, (*diag, c), me).wait_recv()
        start(copy(a, 6, (*diag, c), sibling))
    for a in range(n):
        copy(a, 0, sibling, me).wait_recv()
        copy(a, 4, (*n2, 1 - c), me).wait_recv()
        copy(a, 5, (*n1, 1 - c), me).wait_recv()
        copy(a, 6, (*diag, 1 - c), me).wait_recv()
    for cp in started:
        cp.wait_send()


def _peer_relations():
    x, y, c = _mesh_pos()
    flips = [(fx, fy, fc) for fx in (0, 1) for fy in (0, 1) for fc in (0, 1)][1:]
    peers = [(1 - x if fx else x, 1 - y if fy else y, 1 - c if fc else c) for fx, fy, fc in flips]
    return (x, y, c), peers


def _direct_gather_start(out_ref, send_sems, recv_sems):
    me, peers = _peer_relations()
    rows = out_ref.at[_dev_index(*me)]
    for k, peer in enumerate(peers):
        pltpu.make_async_remote_copy(src_ref=rows, dst_ref=rows, send_sem=send_sems.at[k], recv_sem=recv_sems.at[k],
                                     device_id=peer, device_id_type=MESH).start()


def _direct_gather_wait(out_ref, send_sems, recv_sems):
    me, peers = _peer_relations()
    for k, peer in enumerate(peers):
        rows = out_ref.at[_dev_index(*peer)]
        cp = pltpu.make_async_remote_copy(src_ref=rows, dst_ref=rows, send_sem=send_sems.at[k],
                                          recv_sem=recv_sems.at[k], device_id=peer, device_id_type=MESH)
        cp.wait_recv()
        cp.wait_send()


def _direct_scatter_start(send_ref, recv_ref, send_sems, recv_sems):
    me, peers = _peer_relations()
    for k, peer in enumerate(peers):
        pltpu.make_async_remote_copy(src_ref=send_ref.at[_dev_index(*peer)], dst_ref=recv_ref.at[k],
                                     send_sem=send_sems.at[k], recv_sem=recv_sems.at[k],
                                     device_id=peer, device_id_type=MESH).start()


def _direct_scatter_wait(send_ref, recv_ref, send_sems, recv_sems):
    me, peers = _peer_relations()
    for k, peer in enumerate(peers):
        cp = pltpu.make_async_remote_copy(src_ref=send_ref.at[_dev_index(*peer)], dst_ref=recv_ref.at[k],
                                          send_sem=send_sems.at[k], recv_sem=recv_sems.at[k],
                                          device_id=peer, device_id_type=MESH)
        cp.wait_recv()
        cp.wait_send()


def _gather_weights(w_in_flat, conv_blk, x2, norm_w, tn):
    nt = x2.shape[0] // tn

    def body(win_ref, conv_ref, x_hbm, nw_ref, gin_ref, gconv_ref, n_hbm, xbuf, nbuf, send_sems, recv_sems, xsem, nsem):
        x, y, c = _mesh_pos()
        me = _dev_index(x, y, c)
        for j in range(FLAT_ROWS):
            gin_ref[me, :, 128 * j:128 * (j + 1)] = win_ref[pl.ds(j, W_IN_SHARD, stride=FLAT_ROWS), :].astype(BF16)
        gconv_ref[me] = conv_ref[...]

        def norm_x():
            def load(t):
                return pltpu.make_async_copy(x_hbm.at[pl.ds(t * tn, tn), :], xbuf.at[t % 2], xsem.at[t % 2])

            def store(t):
                return pltpu.make_async_copy(nbuf.at[t % 2], n_hbm.at[pl.ds(t * tn, tn), :], nsem.at[t % 2])

            load(0).start()
            for t in range(nt):
                if t + 1 < nt:
                    load(t + 1).start()
                load(t).wait()
                if t >= 2:
                    store(t - 2).wait()
                xhat, _ = _rms_hat(xbuf[t % 2])
                nbuf[t % 2] = (xhat * nw_ref[...]).astype(BF16)
                store(t).start()
            for t in range(max(nt - 2, 0), nt):
                store(t).wait()

        _all_gather_blocks((gin_ref, gconv_ref), send_sems, recv_sems, meanwhile=norm_x)

    vm = pl.BlockSpec(memory_space=pltpu.VMEM)
    hbm = pl.BlockSpec(memory_space=pl.ANY)
    return pl.pallas_call(
        body, name="gather_weights",
        out_shape=(jax.ShapeDtypeStruct((N_DEV, W_IN_SHARD, D_MODEL), BF16),
                   jax.ShapeDtypeStruct((N_DEV,) + conv_blk.shape, F32),
                   jax.ShapeDtypeStruct(x2.shape, BF16)),
        in_specs=[vm, vm, hbm, vm], out_specs=(vm, vm, hbm),
        scratch_shapes=[pltpu.VMEM((2, tn, D_MODEL), F32), pltpu.VMEM((2, tn, D_MODEL), BF16),
                        pltpu.SemaphoreType.DMA((2, 7)), pltpu.SemaphoreType.DMA((2, 7)),
                        pltpu.SemaphoreType.DMA((2,)), pltpu.SemaphoreType.DMA((2,))],
        compiler_params=_cp(),
    )(w_in_flat, conv_blk, x2, norm_w)


def _reduce_grads(big, like, pack_a, pack_b):
    nb = len(big)

    def body(*refs):
        srcs, (pa_ref, pb_ref), outs, (ga_ref, gb_ref) = (
            refs[:nb], refs[nb:nb + 2], refs[nb + 2:2 * nb + 2], refs[2 * nb + 2:2 * nb + 4])
        scr = refs[2 * nb + 4:]
        r1s, r2s, sbs, sts = (scr[k * nb:(k + 1) * nb] for k in range(4))
        s1_send, s1_recv, s2_send, s2_recv, ag_send, ag_recv, st_sem = scr[4 * nb:]
        x, y, c = _mesh_pos()
        me = (x, y, c)
        sibling = (x, y, 1 - c)

        ga_ref[_dev_index(*me)] = pa_ref[...]
        gb_ref[_dev_index(*me)] = pb_ref[...]

        _, _, n1, n2, diag = _relay_order()
        order = ((diag, 3), (n1, 1 + c), (n2, 2 - c), ((x, y), 0))

        def p1(a, i, to):
            chip, slot = order[i]
            return pltpu.make_async_remote_copy(
                src_ref=srcs[a].at[_dev_index(*chip, 1 - c)], dst_ref=r1s[a].at[slot],
                send_sem=s1_send.at[a, slot], recv_sem=s1_recv.at[a, slot], device_id=to, device_id_type=MESH)

        def p1_landed(a, slot):
            return pltpu.make_async_remote_copy(
                src_ref=r1s[a].at[slot], dst_ref=r1s[a].at[slot], send_sem=s1_send.at[a, slot],
                recv_sem=s1_recv.at[a, slot], device_id=me, device_id_type=MESH)

        def p2(a, k, to):
            return pltpu.make_async_remote_copy(
                src_ref=sbs[a].at[k], dst_ref=r2s[a].at[k],
                send_sem=s2_send.at[a, k], recv_sem=s2_recv.at[a, k], device_id=to, device_id_type=MESH)

        def stage(a, i):
            return pltpu.make_async_copy(srcs[a].at[_dev_index(*order[i][0], c)], sts[a].at[i % 2], st_sem.at[a, i % 2])

        sends = [p1(a, i, sibling) for a in range(nb) for i in (0, 2, 1, 3)]
        for cp in sends:
            cp.start()

        def step_1():
            for a in range(nb):
                stage(a, 0).start()
                for i, (_, slot) in enumerate(order):
                    if i + 1 < len(order):
                        stage(a, i + 1).start()
                    stage(a, i).wait()
                    p1_landed(a, slot).wait_recv()
                    chip_sum = r1s[a][slot] + sts[a][i % 2]
                    if i < 2:
                        sbs[a][i] = chip_sum.astype(BF16)
                        sends.append(p2(a, i, (*n1, c)))
                        sends[-1].start()
                    else:
                        r1s[a][slot] = chip_sum

        _all_gather_blocks((ga_ref, gb_ref), ag_send, ag_recv, meanwhile=step_1)
        for a in range(nb):
            p2(a, 0, me).wait_recv()
            sbs[a][2] = (r1s[a][2 - c] + r2s[a][0].astype(F32)).astype(BF16)
            sends.append(p2(a, 2, (*n2, c)))
            sends[-1].start()
        for a in range(nb):
            p2(a, 1, me).wait_recv()
            p2(a, 2, me).wait_recv()
            total = (r1s[a][0] + r2s[a][1].astype(F32)) + r2s[a][2].astype(F32)
            if outs[a].shape == total.shape:
                outs[a][...] = total
            else:
                r1s[a][0] = total
                for j in range(FLAT_ROWS):
                    outs[a][pl.ds(j, W_IN_SHARD, stride=FLAT_ROWS), :] = r1s[a][0, :, 128 * j:128 * (j + 1)]
        for cp in sends:
            cp.wait_send()

    vm = pl.BlockSpec(memory_space=pltpu.VMEM)
    hbm = pl.BlockSpec(memory_space=pl.ANY)
    blk = [p.shape[1:] for p in big]
    scratch = ([pltpu.VMEM((4,) + b, F32) for b in blk] + [pltpu.VMEM((3,) + b, BF16) for b in blk]
               + [pltpu.VMEM((3,) + b, BF16) for b in blk] + [pltpu.VMEM((2,) + b, F32) for b in blk]
               + [pltpu.SemaphoreType.DMA((nb, 4)), pltpu.SemaphoreType.DMA((nb, 4)),
                  pltpu.SemaphoreType.DMA((nb, 3)), pltpu.SemaphoreType.DMA((nb, 3)),
                  pltpu.SemaphoreType.DMA((2, 7)), pltpu.SemaphoreType.DMA((2, 7)),
                  pltpu.SemaphoreType.DMA((nb, 2))])
    return pl.pallas_call(
        body, name="reduce_grads",
        out_shape=tuple(jax.ShapeDtypeStruct(w.shape, F32) for w in like)
        + (jax.ShapeDtypeStruct((N_DEV,) + pack_a.shape, F32), jax.ShapeDtypeStruct((N_DEV,) + pack_b.shape, F32)),
        in_specs=[hbm] * nb + [vm, vm], out_specs=(vm,) * (nb + 2),
        scratch_shapes=scratch,
        compiler_params=_cp(),
    )(*big, pack_a, pack_b)


def _rms_hat(xf):
    r = lax.rsqrt(jnp.mean(xf * xf, axis=-1, keepdims=True) + NORM_EPS)
    return xf * r, r


def _shift_up(cur, next_head, s):
    ext = jnp.concatenate([cur, next_head], axis=0)
    n = ext.shape[0]
    return pltpu.roll(ext, n - s, 0)[:cur.shape[0], :]


def _pool_counts(i, tp, w):
    t = i * tp + lax.broadcasted_iota(jnp.int32, (tp, 1), 0)
    return jnp.minimum(t + 1, w).astype(F32)


def _pool_mix(u, u_prev_tail, i, tp):
    win = jnp.concatenate([u_prev_tail, u], axis=0)
    mixes = []
    for gi, w in enumerate(POOL_WINDOWS):
        win = win + pltpu.roll(win, w // 2, 0)
        cols = slice(gi * POOL_GROUP, (gi + 1) * POOL_GROUP)
        mixes.append(win[HALO:, :POOL_GROUP] / _pool_counts(i, tp, w) - u[:, cols])
        if gi + 1 < len(POOL_WINDOWS):
            win = win[:, POOL_GROUP:]
    return mixes


def _prev_tail(ref, i):
    return jnp.where(i > 0, ref[...], 0.0)


def _conv_taps(cur, prev_tail):
    ext = jnp.concatenate([prev_tail, cur], axis=0)
    return [cur] + [pltpu.roll(ext, sft, 0)[HALO:, :] for sft in range(1, CONV_WIDTH)]


def _conv_of_taps(taps, w4):
    y = taps[0] * w4[CONV_WIDTH - 1:CONV_WIDTH, :]
    for sft in range(1, CONV_WIDTH):
        y = y + taps[sft] * w4[CONV_WIDTH - 1 - sft:CONV_WIDTH - sft, :]
    return y


def _conv_fwd(cur, prev_tail, w4):
    ext = jnp.concatenate([prev_tail, cur], axis=0)
    y = ext * w4[CONV_WIDTH - 1:CONV_WIDTH, :]
    for sft in range(1, CONV_WIDTH):
        y = y + pltpu.roll(ext, sft, 0) * w4[CONV_WIDTH - 1 - sft:CONV_WIDTH - sft, :]
    return y[HALO:, :]


def _l2n_heads(t):
    parts = []
    for h in range(DN_HEADS):
        th = t[:, h * DN_HEAD_DIM:(h + 1) * DN_HEAD_DIM]
        parts.append(th * lax.rsqrt(jnp.sum(th * th, axis=-1, keepdims=True) + NORM_EPS))
    return jnp.concatenate(parts, axis=1)


def _post_conv(yq, yk, yv):
    return _l2n_heads(_silu(yq)), _l2n_heads(_silu(yk)), _silu(yv)


def _gates(ba, alog_lane, dtb_lane):
    lane = lax.broadcasted_iota(jnp.int32, ba.shape, 1)
    beta = jax.nn.sigmoid(ba)
    g = -jnp.exp(alog_lane) * _softplus(ba + dtb_lane)
    return jnp.where(lane < DN_HEADS, beta, jnp.where(lane < 2 * DN_HEADS, g, 0.0))


def _front(n_all, g_in, pool_w, pool_scale, conv_full, alog_lane, dtb_lane, w_out_blk, tm):
    s = n_all.shape[0]

    def body(n_ref, pw_ref, ps_ref, cw_ref, al_ref, db_ref, wo_ref, g_hbm,
             pm_ref, pb_ref, yp_ref, qn_ref, kn_ref, vv_ref, gb_ref, gwo_hbm, wt_hbm,
             g_vmem, wt_vmem, tail_u, tail_qkv, gwo_ref, sem, wo_send, wo_recv):
        i = pl.program_id(0)

        @pl.when(i == 0)
        def _():
            gwo_ref[_dev_index(*_mesh_pos())] = wo_ref[...].astype(BF16)
            _direct_gather_start(gwo_ref, wo_send, wo_recv)
            cp = pltpu.make_async_copy(g_hbm, g_vmem, sem)
            cp.start()
            wt_vmem[D_MAIN:, :] = jnp.zeros((D_IN_PAD - D_MAIN, D_MODEL), BF16)
            tail_u[...] = jnp.zeros_like(tail_u)
            tail_qkv[...] = jnp.zeros_like(tail_qkv)
            cp.wait()
            for d in range(N_DEV):
                wt_vmem[W_IN_SHARD * d:W_IN_SHARD * (d + 1), :] = g_vmem[d]
            out = pltpu.make_async_copy(wt_vmem, wt_hbm, sem)
            out.start()
            out.wait()
        n = n_ref[...]
        pm_ref[...] = _dot_nt_bf(n, wt_vmem[:D_MAIN, :])
        pb = _dot_nt_bf(n, wt_vmem[D_MAIN:, :])
        pb_ref[...] = pb
        u = pm_ref[:, :D_POOL]
        mixes = _pool_mix(u, tail_u[...], i, tm)
        tail_u[...] = u[tm - HALO:, :]
        gate = ps_ref[...] * _silu(pm_ref[:, D_POOL:2 * D_POOL])
        for gi in range(4):
            cols = slice(gi * POOL_GROUP, (gi + 1) * POOL_GROUP)
            yp_ref[:, cols] = _dot_bf(mixes[gi], pw_ref[gi]) * gate[:, cols]
        ys = []
        for c in range(3):
            cols = slice(c * D_DN, (c + 1) * D_DN)
            cur = pm_ref[:, 2 * D_POOL + c * D_DN:2 * D_POOL + (c + 1) * D_DN]
            ys.append(_conv_fwd(cur, tail_qkv[:, cols], cw_ref[:, cols]))
            tail_qkv[:, cols] = cur[tm - HALO:, :]
        qn, kn, vv = _post_conv(*ys)
        qn_ref[...] = qn
        kn_ref[...] = kn
        vv_ref[...] = vv
        gb_ref[...] = _gates(pb, al_ref[...], db_ref[...])

        @pl.when(i == s // tm - 1)
        def _():
            _direct_gather_wait(gwo_ref, wo_send, wo_recv)
            out = pltpu.make_async_copy(gwo_ref, gwo_hbm, sem)
            out.start()
            out.wait()

    tile = pl.BlockSpec((tm, D_DN), lambda i: (i, 0))
    lanes = pl.BlockSpec((tm, 128), lambda i: (i, 0))
    row = pl.BlockSpec((1, 128), lambda i: (0, 0))
    return pl.pallas_call(
        body, name="front", grid=(s // tm,),
        out_shape=(jax.ShapeDtypeStruct((s, D_MAIN), F32), jax.ShapeDtypeStruct((s, 128), F32),
                   jax.ShapeDtypeStruct((s, D_POOL), F32), jax.ShapeDtypeStruct((s, D_DN), F32),
                   jax.ShapeDtypeStruct((s, D_DN), F32), jax.ShapeDtypeStruct((s, D_DN), F32),
                   jax.ShapeDtypeStruct((s, 128), F32), jax.ShapeDtypeStruct((N_DEV,) + w_out_blk.shape, BF16),
                   jax.ShapeDtypeStruct((D_IN_PAD, D_MODEL), BF16)),
        in_specs=[pl.BlockSpec((tm, D_MODEL), lambda i: (i, 0)),
                  pl.BlockSpec((4, POOL_GROUP, POOL_GROUP), lambda i: (0, 0, 0)),
                  pl.BlockSpec((1, D_POOL), lambda i: (0, 0)),
                  pl.BlockSpec((CONV_WIDTH, 3 * D_DN), lambda i: (0, 0)), row, row,
                  pl.BlockSpec(memory_space=pltpu.VMEM), pl.BlockSpec(memory_space=pl.ANY)],
        out_specs=(pl.BlockSpec((tm, D_MAIN), lambda i: (i, 0)), lanes, tile, tile, tile, tile, lanes,
                   pl.BlockSpec(memory_space=pl.ANY), pl.BlockSpec(memory_space=pl.ANY)),
        scratch_shapes=[pltpu.VMEM((N_DEV, W_IN_SHARD, D_MODEL), BF16), pltpu.VMEM((D_IN_PAD, D_MODEL), BF16),
                        pltpu.VMEM((HALO, D_POOL), F32), pltpu.VMEM((HALO, 3 * D_DN), F32),
                        pltpu.VMEM((N_DEV,) + w_out_blk.shape, BF16),
                        pltpu.SemaphoreType.DMA, pltpu.SemaphoreType.DMA((7,)), pltpu.SemaphoreType.DMA((7,))],
        compiler_params=_cp(("arbitrary",)),
    )(n_all, pool_w, pool_scale, conv_full, alog_lane, dtb_lane, w_out_blk, g_in)


def _dn_block(q, k, v, gcol, bcol, state, known=None):
    nb, n, d = q.shape
    ii = lax.broadcasted_iota(jnp.int32, (n, n), 0)
    jj = lax.broadcasted_iota(jnp.int32, (n, n), 1)
    lower = ii >= jj
    eye = (ii == jj).astype(F32)
    g_row = jnp.sum(eye * gcol, axis=1, keepdims=True)
    gc_col = jnp.sum(jnp.where(lower, g_row, 0.0), axis=2, keepdims=True)
    gc_row = jnp.sum(eye * gc_col, axis=1, keepdims=True)
    decay = jnp.where(lower, jnp.exp(jnp.where(lower, gc_col - gc_row, 0.0)), 0.0)
    kb = k * bcol
    vb = v * bcol
    qs = q * (DN_HEAD_DIM ** -0.5)
    egc = jnp.exp(gc_col)
    kq = jnp.concatenate([kb, qs], axis=1)
    vk = jnp.concatenate([vb, kb * egc], axis=2)
    if known is None:
        akq = _mm(kq, k, 1, 1, *_DN_PREC["akq"])
    else:
        akq = _mm_known(kq, k, known[0][:, :, :n].astype(F32), 1, 1, _DN_PREC["akq"][1])
    a = jnp.where(ii > jj, akq[:, :n] * decay, 0.0)
    qk = akq[:, n:] * decay
    if known is None:
        t = _tri_inv(a, *_DN_PREC["inv"])
        uw = _mm(t, vk, 1, 0, *_DN_PREC["uw"])
    else:
        t = _tri_inv_known(a, known[0][:, :n, n:].astype(F32), _DN_PREC["inv"][1])
        uw = _mm_known(t, vk, known[1], 1, 0, _DN_PREC["uw"][1])
    pre = jnp.concatenate([akq, jnp.concatenate([t, jnp.zeros_like(t)], axis=1)], axis=2)
    wq = jnp.concatenate([uw[:, :, d:], qs * egc], axis=1)
    g_last = gc_col[:, n - 1:n, :]
    k_dec = k * jnp.exp(g_last - gc_col)
    e_last = jnp.exp(g_last)
    os_, starts = [], []
    for c in range(nb // DN_HEADS):
        sl = slice(c * DN_HEADS, (c + 1) * DN_HEADS)
        if known is not None and c > 0:
            state = _use_known(state, known[2][sl])
        starts.append(state)
        ws = _mm(wq[sl], state, 1, 0, *_DN_PREC["ws"])
        v_new = uw[sl, :, :d] - ws[:, :n]
        os_.append(ws[:, n:] + _mm(qk[sl], v_new, 1, 0, *_DN_PREC["ov"]))
        state = state * e_last[sl] + _mm(k_dec[sl], v_new, 0, 0, *_DN_PREC["st"])
    return jnp.concatenate(os_, axis=0), state, (pre, uw, jnp.concatenate(starts, axis=0))


def _gated_norm(o, dz, nw):
    parts = []
    for h in range(DN_HEADS):
        oh = o[:, h * DN_HEAD_DIM:(h + 1) * DN_HEAD_DIM]
        parts.append(oh * lax.rsqrt(jnp.mean(oh * oh, axis=-1, keepdims=True) + NORM_EPS) * nw)
    return jnp.concatenate(parts, axis=1) * _silu(dz)


def _dn_block_args(gc, q_ref, k_ref, v_ref, gb_ref):
    qs, ks, vs, gs, bs = [], [], [], [], []
    for cc in range(gc):
        r = slice(cc * CHUNK, (cc + 1) * CHUNK)
        gbv = gb_ref[r, :]
        for h in range(DN_HEADS):
            cols = slice(h * DN_HEAD_DIM, (h + 1) * DN_HEAD_DIM)
            qs.append(q_ref[r, cols])
            ks.append(k_ref[r, cols])
            vs.append(v_ref[r, cols])
            gs.append(gbv[:, DN_HEADS + h:DN_HEADS + h + 1])
            bs.append(gbv[:, h:h + 1])
    return tuple(jnp.stack(t, axis=0) for t in (qs, ks, vs, gs, bs))


def _dn_scan_fwd(qn, kn, vv, gb, gc):
    s = qn.shape[0]
    nchunk = s // CHUNK
    rows = gc * CHUNK

    def body(q_ref, k_ref, v_ref, gb_ref, y_ref, ss_ref, pre_ref, uw_ref, state):
        @pl.when(pl.program_id(0) == 0)
        def _():
            state[...] = jnp.zeros_like(state)
        q, k, v, gcol, bcol = _dn_block_args(gc, q_ref, k_ref, v_ref, gb_ref)
        y, new, (pre, uw, starts) = _dn_block(q, k, v, gcol, bcol, state[...])
        state[...] = new
        ss_ref[...] = starts
        pre_ref[...] = pre
        uw_ref[...] = uw
        for cc in range(gc):
            for h in range(DN_HEADS):
                y_ref[cc * CHUNK:(cc + 1) * CHUNK, h * DN_HEAD_DIM:(h + 1) * DN_HEAD_DIM] = y[cc * DN_HEADS + h]

    tile = pl.BlockSpec((rows, D_DN), lambda i: (i, 0))
    return pl.pallas_call(
        body, name="dn_scan_fwd", grid=(nchunk // gc,),
        out_shape=(jax.ShapeDtypeStruct((s, D_DN), F32),
                   jax.ShapeDtypeStruct((nchunk * DN_HEADS, DN_HEAD_DIM, DN_HEAD_DIM), F32),
                   jax.ShapeDtypeStruct((nchunk * DN_HEADS, 2 * CHUNK, 2 * CHUNK), F32),
                   jax.ShapeDtypeStruct((nchunk * DN_HEADS, CHUNK, 2 * DN_HEAD_DIM), F32)),
        in_specs=[tile, tile, tile, pl.BlockSpec((rows, 128), lambda i: (i, 0))],
        out_specs=(tile, pl.BlockSpec((gc * DN_HEADS, DN_HEAD_DIM, DN_HEAD_DIM), lambda i: (i, 0, 0)),
                   pl.BlockSpec((gc * DN_HEADS, 2 * CHUNK, 2 * CHUNK), lambda i: (i, 0, 0)),
                   pl.BlockSpec((gc * DN_HEADS, CHUNK, 2 * DN_HEAD_DIM), lambda i: (i, 0, 0))),
        scratch_shapes=[pltpu.VMEM((DN_HEADS, DN_HEAD_DIM, DN_HEAD_DIM), F32)],
        compiler_params=_cp(("arbitrary",)),
    )(qn, kn, vv, gb)


def _out_proj_loss(y_pool, o_dn, proj_main, dn_norm_w, x2, tgt, w_out_full, fnw, tm):
    s = x2.shape[0]

    def body(yp_ref, o_ref, dz_ref, nw_ref, x_ref, t_ref, wo_ref, fw_ref,
             dh_ref, dyp_ref, do_ref, ddz_ref, gwo_ref, gfw_ref, loss_ref, dnw_ref):
        @pl.when(pl.program_id(0) == 0)
        def _():
            gwo_ref[...] = jnp.zeros_like(gwo_ref)
            gfw_ref[...] = jnp.zeros_like(gfw_ref)
            loss_ref[...] = jnp.zeros_like(loss_ref)
            dnw_ref[...] = jnp.zeros_like(dnw_ref)
        y_dn, gate_vjp = jax.vjp(_gated_norm, o_ref[...], dz_ref[...], nw_ref[...])
        y = jnp.concatenate([yp_ref[...], y_dn], axis=1).astype(BF16)
        wo = wo_ref[...]
        h = x_ref[...] + jnp.dot(y, wo, preferred_element_type=F32)
        hn, r = _rms_hat(h)
        fw = fw_ref[...]
        err = hn * fw - t_ref[...]
        loss_ref[...] += 0.5 * jnp.sum(jnp.sum(err * err, axis=-1, keepdims=True) / D_MODEL, axis=0, keepdims=True)
        dout = err / D_MODEL
        gfw_ref[...] += jnp.sum(dout * hn, axis=0, keepdims=True)
        dhn = dout * fw
        dh = r * (dhn - hn * jnp.mean(dhn * hn, axis=-1, keepdims=True))
        dh_ref[...] = dh
        dhb = dh.astype(BF16)
        dy = _dot_nt_bf(dhb, wo)
        dyp_ref[...] = dy[:, :D_POOL]
        do, ddz, dnw = gate_vjp(dy[:, D_POOL:])
        do_ref[...] = do
        ddz_ref[...] = ddz
        dnw_ref[...] += dnw
        gwo_ref[...] += _dot_tn_bf(y, dhb)

    half = pl.BlockSpec((tm, D_POOL), lambda i: (i, 0))
    full = pl.BlockSpec((tm, D_MODEL), lambda i: (i, 0))
    lrow = pl.BlockSpec((1, 128), lambda i: (0, 0))
    return pl.pallas_call(
        body, name="out_proj_loss", grid=(s // tm,),
        out_shape=(jax.ShapeDtypeStruct((s, D_MODEL), F32), jax.ShapeDtypeStruct((s, D_POOL), F32),
                   jax.ShapeDtypeStruct((s, D_DN), F32), jax.ShapeDtypeStruct((s, D_DN), F32),
                   jax.ShapeDtypeStruct((D_MODEL, D_MODEL), F32),
                   jax.ShapeDtypeStruct((1, D_MODEL), F32), jax.ShapeDtypeStruct((1, 128), F32),
                   jax.ShapeDtypeStruct((1, 128), F32)),
        in_specs=[half, half, pl.BlockSpec((tm, D_DN), lambda i: (i, 5)), lrow, full, full,
                  pl.BlockSpec((D_MODEL, D_MODEL), lambda i: (0, 0)), pl.BlockSpec((1, D_MODEL), lambda i: (0, 0))],
        out_specs=(full, half, half, half, pl.BlockSpec((D_MODEL, D_MODEL), lambda i: (0, 0)),
                   pl.BlockSpec((1, D_MODEL), lambda i: (0, 0)), lrow, lrow),
        compiler_params=_cp(("arbitrary",)),
    )(y_pool, o_dn, proj_main, dn_norm_w, x2, tgt, w_out_full, fnw)


def _dn_scan_bwd(qn, kn, vv, gb, states, pre, uw, do_dn, p_out, gc):
    s = qn.shape[0]
    nchunk = s // CHUNK
    nstep = nchunk // gc
    rows = gc * CHUNK

    def body(q_ref, k_ref, v_ref, gb_ref, ss_ref, pre_ref, uw_ref, dy_ref, po_ref,
             dq_ref, dk_ref, dv_ref, dgb_ref, gro_ref, dstate, po_send, po_recv, rs_send, rs_recv):
        @pl.when(pl.program_id(0) == 0)
        def _():
            dstate[...] = jnp.zeros_like(dstate)
            po_send[...] = po_ref[...].astype(BF16)
            _direct_scatter_start(po_send, po_recv, rs_send, rs_recv)

        @pl.when(pl.program_id(0) == nstep - 1)
        def _():
            _direct_scatter_wait(po_send, po_recv, rs_send, rs_recv)
            total = po_ref[_dev_index(*_mesh_pos())]
            for k in range(N_DEV - 1):
                total = total + po_recv[k].astype(F32)
            gro_ref[...] = total
        lane = lax.broadcasted_iota(jnp.int32, (CHUNK, 128), 1)
        q, k, v, gcol, bcol = _dn_block_args(gc, q_ref, k_ref, v_ref, gb_ref)
        dy = jnp.stack([dy_ref[cc * CHUNK:(cc + 1) * CHUNK, h * DN_HEAD_DIM:(h + 1) * DN_HEAD_DIM]
                        for cc in range(gc) for h in range(DN_HEADS)], axis=0)
        known = (pre_ref[...], uw_ref[...], ss_ref[...])
        _, vjp = jax.vjp(lambda *a: _dn_block(*a, known=known)[:2], q, k, v, gcol, bcol, ss_ref[:DN_HEADS])
        dq, dk, dv, dg, db, dst = vjp((dy, dstate[...]))
        dstate[...] = dst
        for cc in range(gc):
            r = slice(cc * CHUNK, (cc + 1) * CHUNK)
            dgb = jnp.zeros((CHUNK, 128), F32)
            for h in range(DN_HEADS):
                b = cc * DN_HEADS + h
                cols = slice(h * DN_HEAD_DIM, (h + 1) * DN_HEAD_DIM)
                for ref, val in zip((dq_ref, dk_ref, dv_ref), (dq, dk, dv)):
                    ref[r, cols] = val[b]
                dgb = dgb + jnp.where(lane == h, db[b], 0.0) + jnp.where(lane == DN_HEADS + h, dg[b], 0.0)
            dgb_ref[r, :] = dgb

    rev = lambda i: (nstep - 1 - i, 0)
    tile = pl.BlockSpec((rows, D_DN), rev)
    lanes = pl.BlockSpec((rows, 128), rev)
    return pl.pallas_call(
        body, name="dn_scan_bwd", grid=(nstep,),
        out_shape=(jax.ShapeDtypeStruct((s, D_DN), F32),) * 3
        + (jax.ShapeDtypeStruct((s, 128), F32), jax.ShapeDtypeStruct(p_out.shape[1:], F32)),
        in_specs=[tile, tile, tile, lanes,
                  pl.BlockSpec((gc * DN_HEADS, DN_HEAD_DIM, DN_HEAD_DIM), lambda i: (nstep - 1 - i, 0, 0)),
                  pl.BlockSpec((gc * DN_HEADS, 2 * CHUNK, 2 * CHUNK), lambda i: (nstep - 1 - i, 0, 0)),
                  pl.BlockSpec((gc * DN_HEADS, CHUNK, 2 * DN_HEAD_DIM), lambda i: (nstep - 1 - i, 0, 0)), tile,
                  pl.BlockSpec(memory_space=pltpu.VMEM)],
        out_specs=(tile, tile, tile, lanes, pl.BlockSpec(memory_space=pltpu.VMEM)),
        scratch_shapes=[pltpu.VMEM((DN_HEADS, DN_HEAD_DIM, DN_HEAD_DIM), F32),
                        pltpu.VMEM(p_out.shape, BF16), pltpu.VMEM((N_DEV - 1,) + p_out.shape[1:], BF16),
                        pltpu.SemaphoreType.DMA((7,)), pltpu.SemaphoreType.DMA((7,))],
        compiler_params=_cp(("arbitrary",)),
    )(qn, kn, vv, gb, states, pre, uw, do_dn, p_out)


def _back(proj_main, proj_ba, dyp, dqn, dkn, dvv, dgb, ddz, x2, dh, norm_w, pool_w, pool_scale, conv_full,
          alog_lane, dtb_lane, wt_full, tm):
    s = x2.shape[0]
    nstep = s // tm
    per = tm // HALO

    def body(u_ref, z_ref, q_ref, k_ref, v_ref, up_ref, qp_ref, kp_ref, vp_ref, ba_ref,
             dyp_ref, dqn_ref, dkn_ref, dvv_ref, dgb_ref, ddz_ref, x_ref, dh_ref,
             nw_ref, pw_ref, ps_ref, cw_ref, al_ref, db_ref, wt_hbm,
             gx_ref, p_hbm, gnw_ref, dpw_ref, dps_ref, dcw_ref, dal_ref, ddb_ref,
             wt_vmem, acc, blk, head_dc, head_dw, sem, osem):
        j = pl.program_id(0)
        i = nstep - 1 - j

        @pl.when(j == 0)
        def _():
            cp = pltpu.make_async_copy(wt_hbm, wt_vmem, sem)
            cp.start()
            acc[...] = jnp.zeros_like(acc)
            for ref in (gnw_ref, dpw_ref, dps_ref, dcw_ref, dal_ref, ddb_ref, head_dc, head_dw):
                ref[...] = jnp.zeros_like(ref)
            cp.wait()

        u = u_ref[...]
        z = z_ref[...]
        dy = dyp_ref[...]
        ps = ps_ref[...]
        mixes = _pool_mix(u, _prev_tail(up_ref, i), i, tm)
        sg = jax.nn.sigmoid(z)
        sz = z * sg
        dsz = sg * (1.0 + z * (1.0 - sg))
        dzs, dwins = [], []
        for gi, w in enumerate(POOL_WINDOWS):
            cols = slice(gi * POOL_GROUP, (gi + 1) * POOL_GROUP)
            mixw = _dot_bf(mixes[gi], pw_ref[gi])
            dmixw = dy[:, cols] * ps[:, cols] * sz[:, cols]
            dps_ref[:, cols] += jnp.sum(dy[:, cols] * mixw * sz[:, cols], axis=0, keepdims=True)
            dzs.append(dy[:, cols] * mixw * ps[:, cols] * dsz[:, cols])
            dpw_ref[gi] += _dot_tn_bf(mixes[gi], dmixw)
            dwins.append(_dot_nt_bf(dmixw, pw_ref[gi]) / _pool_counts(i, tm, w))
        dzp = jnp.concatenate(dzs, axis=1)
        dw = jnp.concatenate(dwins, axis=1)
        win = jnp.concatenate([dw, head_dw[...]], axis=0)
        m = win.shape[0]
        dups = []
        for gi, w in enumerate(POOL_WINDOWS):
            win = win + pltpu.roll(win, m - w // 2, 0)
            cols = slice(gi * POOL_GROUP, (gi + 1) * POOL_GROUP)
            dups.append(win[:tm, :POOL_GROUP] - dw[:, cols] * _pool_counts(i, tm, w))
            if gi + 1 < len(POOL_WINDOWS):
                win = win[:, POOL_GROUP:]
        dup = jnp.concatenate(dups, axis=1)
        head_dw[...] = dw[:HALO, :]

        curs = (q_ref[...], k_ref[...], v_ref[...])
        tails = (_prev_tail(qp_ref, i), _prev_tail(kp_ref, i), _prev_tail(vp_ref, i))
        taps = [_conv_taps(curs[c], tails[c]) for c in range(3)]
        ys = [_conv_of_taps(taps[c], cw_ref[:, c * D_DN:(c + 1) * D_DN]) for c in range(3)]
        _, vjp = jax.vjp(_post_conv, *ys)
        dys = vjp((dqn_ref[...], dkn_ref[...], dvv_ref[...]))
        dxs = []
        for c, dyc in enumerate(dys):
            cols = slice(c * D_DN, (c + 1) * D_DN)
            w4 = cw_ref[:, cols]
            for sft in range(CONV_WIDTH):
                row = CONV_WIDTH - 1 - sft
                dcw_ref[row:row + 1, cols] += jnp.sum(dyc * taps[c][sft], axis=0, keepdims=True)
            head = head_dc[:, cols]
            dx = dyc * w4[CONV_WIDTH - 1:CONV_WIDTH, :]
            for sft in range(1, CONV_WIDTH):
                dx = dx + _shift_up(dyc, head, sft) * w4[CONV_WIDTH - 1 - sft:CONV_WIDTH - sft, :]
            dxs.append(dx)
            head_dc[:, cols] = dyc[:HALO, :]
        _, gvjp = jax.vjp(_gates, ba_ref[...], al_ref[...], db_ref[...])
        dba, dal, ddb = gvjp(dgb_ref[...])
        dal_ref[...] += dal
        ddb_ref[...] += ddb

        dbab = dba.astype(BF16)
        xhat, r = _rms_hat(x_ref[...])
        nw = nw_ref[...]
        n = (xhat * nw).astype(BF16)
        acc[D_MAIN:, :] += _dot_tn_bf(dbab, n)
        dn = jnp.dot(dbab, wt_vmem[D_MAIN:, :], preferred_element_type=F32)
        for cb, d in enumerate((dup, dzp, dxs[0], dxs[1], dxs[2], ddz_ref[...])):
            rows = slice(cb * D_POOL, (cb + 1) * D_POOL)
            dpart = d.astype(BF16)
            acc[rows, :] += _dot_tn_bf(dpart, n)
            dn = dn + jnp.dot(dpart, wt_vmem[rows, :], preferred_element_type=F32)
        gnw_ref[...] += jnp.sum(dn * xhat, axis=0, keepdims=True)
        dxh = dn * nw
        gx_ref[...] = dh_ref[...] + r * (dxh - xhat * jnp.mean(dxh * xhat, axis=-1, keepdims=True))

        @pl.when(j == nstep - 1)
        def _():
            def out(d):
                return pltpu.make_async_copy(blk.at[d % 2], p_hbm.at[d], osem.at[d % 2])
            for d in range(N_DEV):
                if d >= 2:
                    out(d - 2).wait()
                blk[d % 2] = acc[W_IN_SHARD * d:W_IN_SHARD * (d + 1), :]
                out(d).start()
            out(N_DEV - 2).wait()
            out(N_DEV - 1).wait()

    def col(c):
        return pl.BlockSpec((tm, D_POOL), lambda j: (nstep - 1 - j, c))

    def halo(c):
        return pl.BlockSpec((HALO, D_POOL), lambda j: (jnp.maximum((nstep - 1 - j) * per - 1, 0), c))

    rev = lambda j: (nstep - 1 - j, 0)
    part = pl.BlockSpec((tm, D_POOL), rev)
    lanes = pl.BlockSpec((tm, 128), rev)
    full = pl.BlockSpec((tm, D_MODEL), rev)
    row = pl.BlockSpec((1, D_MODEL), lambda j: (0, 0))
    lrow = pl.BlockSpec((1, 128), lambda j: (0, 0))
    pw = pl.BlockSpec((4, POOL_GROUP, POOL_GROUP), lambda j: (0, 0, 0))
    psp = pl.BlockSpec((1, D_POOL), lambda j: (0, 0))
    cw = pl.BlockSpec((CONV_WIDTH, 3 * D_DN), lambda j: (0, 0))
    return pl.pallas_call(
        body, name="back", grid=(nstep,),
        out_shape=(jax.ShapeDtypeStruct((s, D_MODEL), F32),
                   jax.ShapeDtypeStruct((N_DEV, W_IN_SHARD, D_MODEL), F32), jax.ShapeDtypeStruct((1, D_MODEL), F32),
                   jax.ShapeDtypeStruct((4, POOL_GROUP, POOL_GROUP), F32), jax.ShapeDtypeStruct((1, D_POOL), F32),
                   jax.ShapeDtypeStruct((CONV_WIDTH, 3 * D_DN), F32),
                   jax.ShapeDtypeStruct((1, 128), F32), jax.ShapeDtypeStruct((1, 128), F32)),
        in_specs=[col(0), col(1), col(2), col(3), col(4), halo(0), halo(2), halo(3), halo(4), lanes,
                  part, part, part, part, lanes, part, full, full,
                  row, pw, psp, cw, lrow, lrow, pl.BlockSpec(memory_space=pl.ANY)],
        out_specs=(full, pl.BlockSpec(memory_space=pl.ANY), row, pw, psp, cw, lrow, lrow),
        scratch_shapes=[pltpu.VMEM((D_IN_PAD, D_MODEL), BF16), pltpu.VMEM((D_IN_PAD, D_MODEL), F32),
                        pltpu.VMEM((2, W_IN_SHARD, D_MODEL), F32),
                        pltpu.VMEM((HALO, 3 * D_DN), F32), pltpu.VMEM((HALO, D_POOL), F32),
                        pltpu.SemaphoreType.DMA, pltpu.SemaphoreType.DMA((2,))],
        compiler_params=_cp(("arbitrary",)),
    )(proj_main, proj_main, proj_main, proj_main, proj_main, proj_main, proj_main, proj_main, proj_main, proj_ba,
      dyp, dqn, dkn, dvv, dgb, ddz, x2, dh, norm_w, pool_w, pool_scale, conv_full, alog_lane, dtb_lane, wt_full)


def _adamw_math(w, g, m, v):
    m = ADAM_B1 * m + (1.0 - ADAM_B1) * g
    v = ADAM_B2 * v + (1.0 - ADAM_B2) * (g * g)
    m_hat = m / (1.0 - ADAM_B1 ** ADAM_STEP)
    v_hat = v / (1.0 - ADAM_B2 ** ADAM_STEP)
    delta = -ADAM_LR * (m_hat / (jnp.sqrt(v_hat) + ADAM_EPS) + ADAM_WD * w)
    return delta, m, v


def _adamw(tiled, whole, gath_a, gath_b, pool, rows, nstep):
    kw, nrow = len(whole), len(rows)
    ins = list(tiled) + [a for p in whole for a in p] + [gath_a, gath_b] + list(pool) + [a for wmv in rows for a in wmv]

    def body(*refs):
        in_refs, outs = refs[:len(ins)], refs[len(ins):]

        def update(w, g, m, v, o):
            dl, nm, nv = _adamw_math(w, g, m, v)
            for ref, val in zip(o, (g, dl, nm, nv)):
                ref[...] = val

        update(*(r[...] for r in in_refs[:4]), outs[:4])

        @pl.when(pl.program_id(0) == 0)
        def _():
            for p in range(1, kw + 1):
                update(*(r[...] for r in in_refs[4 * p:4 * p + 4]), outs[4 * p:4 * p + 4])
            ga_ref, gb_ref = in_refs[4 * kw + 4:4 * kw + 6]
            rep_in, rep_out = in_refs[4 * kw + 6:], outs[4 * kw + 4:]

            def total(ref):
                g = ref[0]
                for d in range(1, N_DEV):
                    g = g + ref[d]
                return g

            gs = [total(ga_ref)]
            gb = total(gb_ref)
            gs += [gb[r:r + 1, :rep_in[3 * (r + 1)].shape[1]] for r in range(nrow)]
            for r, g in enumerate(gs):
                w, m, v = (ref[...] for ref in rep_in[3 * r:3 * r + 3])
                update(w, g, m, v, rep_out[4 * r:4 * r + 4])
            rep_out[4 * (nrow + 1)][...] = gb[nrow:nrow + 1, 0:1]

    tile = pl.BlockSpec((tiled[0].shape[0] // nstep, tiled[0].shape[1]), lambda i: (i, 0))

    def full(a):
        return pl.BlockSpec(a.shape, lambda i, nd=a.ndim: (0,) * nd)

    firsts = [p[0] for p in whole] + [pool[0]] + [wmv[0] for wmv in rows]
    out_shape = ([jax.ShapeDtypeStruct(tiled[0].shape, F32)] * 4
                 + [jax.ShapeDtypeStruct(w.shape, F32) for w in firsts for _ in range(4)]
                 + [jax.ShapeDtypeStruct((1, 1), F32)])
    res = pl.pallas_call(
        body, name="adamw", grid=(nstep,),
        in_specs=[tile] * 4 + [full(a) for a in ins[4:]],
        out_specs=tuple([tile] * 4 + [full(o) for o in out_shape[4:]]),
        out_shape=tuple(out_shape),
        compiler_params=_cp(("arbitrary",)),
    )(*ins)
    groups = [tuple(res[4 * k:4 * k + 4]) for k in range(kw + nrow + 2)]
    return groups[0], groups[1:kw + 1], groups[kw + 1:], res[-1]


_ROW_ORDER = ("norm_w", "final_norm_w", "pool_scale", "dn_norm_w", "a_log", "dt_bias")


def _pack_rows(vectors):
    out = [jnp.pad(v.reshape(-1), (0, D_MODEL - v.size)) for v in vectors]
    out += [jnp.zeros((D_MODEL,), F32)] * (8 - len(out))
    return jnp.stack(out, axis=0)


def _lane_row(vec4, start):
    return jnp.pad(vec4.reshape(-1), (start, 128 - start - vec4.size)).reshape(1, 128)


def kernel(x, norm_w, w_in, pool_w, pool_scale, conv_w, a_log, dt_bias, dn_norm_w, w_out, final_norm_w, loss_target, m_norm_w, m_w_in, m_pool_w, m_pool_scale, m_conv_w, m_a_log, m_dt_bias, m_dn_norm_w, m_w_out, m_final_norm_w, v_norm_w, v_w_in, v_pool_w, v_pool_scale, v_conv_w, v_a_log, v_dt_bias, v_dn_norm_w, v_w_out, v_final_norm_w):
    s = x.shape[1]
    tm = min(512, s)
    tmb = min(256, s)
    x2 = x[0]
    tgt = loss_target[0]
    def to_flat(a):
        return a[0].reshape(FLAT_ROWS, 128, W_IN_SHARD).transpose(2, 0, 1).reshape(W_IN_SHARD * FLAT_ROWS, 128)

    def from_flat(f):
        return f.reshape(W_IN_SHARD, FLAT_ROWS, 128).transpose(1, 2, 0).reshape(1, D_MODEL, W_IN_SHARD)

    wf, m_wf, v_wf = to_flat(w_in), to_flat(m_w_in), to_flat(v_w_in)

    g_in, g_conv, n_all = _gather_weights(wf, conv_w[0], x2, norm_w, tm)
    conv_full = g_conv.transpose(1, 0, 2).reshape(CONV_WIDTH, 3 * D_DN)
    alog_lane = _lane_row(a_log, DN_HEADS)
    dtb_lane = _lane_row(dt_bias, DN_HEADS)
    fnw = final_norm_w.reshape(1, D_MODEL)

    proj_main, proj_ba, y_pool, qn, kn, vv, gb, g_out, wt_full = _front(
        n_all, g_in, pool_w[0], pool_scale, conv_full, alog_lane, dtb_lane, w_out[0], tm)
    w_out_full = g_out.reshape(D_MODEL, D_MODEL)
    o_dn, states, dn_pre, dn_uw = _dn_scan_fwd(qn, kn, vv, gb, DN_CHUNKS_PER_STEP)

    dh, dyp, do_dn, ddz, g_wout, g_fnw, loss_part, g_dnw = _out_proj_loss(
        y_pool, o_dn, proj_main, dn_norm_w, x2, tgt, w_out_full, fnw, tm)
    p_out = g_wout.reshape(N_DEV, D_MODEL // N_DEV, D_MODEL)
    dqn, dkn, dvv, dgb, gr_out = _dn_scan_bwd(qn, kn, vv, gb, states, dn_pre, dn_uw, do_dn, p_out, DN_CHUNKS_PER_STEP)
    grad_x2, p_in, g_nw, g_pw, g_ps, g_conv_full, g_al, g_db = _back(
        proj_main, proj_ba, dyp, dqn, dkn, dvv, dgb, ddz, x2, dh, norm_w, pool_w[0], pool_scale, conv_full,
        alog_lane, dtb_lane, wt_full, tmb)

    p_conv = g_conv_full.reshape(CONV_WIDTH, N_DEV, 3 * D_DN // N_DEV).transpose(1, 0, 2)
    pack_a = g_pw.reshape(4 * POOL_GROUP, POOL_GROUP)
    pack_b = _pack_rows([g_nw, g_fnw, g_ps, g_dnw, g_al[0, DN_HEADS:2 * DN_HEADS], g_db[0, DN_HEADS:2 * DN_HEADS],
                         loss_part[0, :1]])
    gr_in, gr_conv, gath_a, gath_b = _reduce_grads((p_in, p_conv), (wf, conv_w[0]), pack_a, pack_b)

    flat = lambda a: a.reshape(4 * POOL_GROUP, POOL_GROUP)
    row = lambda a: a.reshape(1, -1)
    vecs = {"norm_w": (norm_w, m_norm_w, v_norm_w), "final_norm_w": (final_norm_w, m_final_norm_w, v_final_norm_w),
            "pool_scale": (pool_scale, m_pool_scale, v_pool_scale), "dn_norm_w": (dn_norm_w, m_dn_norm_w, v_dn_norm_w),
            "a_log": (a_log, m_a_log, v_a_log), "dt_bias": (dt_bias, m_dt_bias, v_dt_bias)}
    r_in, (r_out, r_conv), res, loss = _adamw(
        (wf, gr_in, m_wf, v_wf),
        [(w_out[0], gr_out, m_w_out[0], v_w_out[0]), (conv_w[0], gr_conv, m_conv_w[0], v_conv_w[0])],
        gath_a, gath_b, (flat(pool_w), flat(m_pool_w), flat(v_pool_w)),
        [tuple(row(a) for a in vecs[nm]) for nm in _ROW_ORDER], ADAMW_STEPS)
    r_pool = res[0]
    r_vec = dict(zip(_ROW_ORDER, res[1:]))

    def group(k):
        vec = lambda nm: r_vec[nm][k].reshape(vecs[nm][0].shape)
        return (vec("norm_w"), from_flat(r_in[k]), r_pool[k].reshape(pool_w.shape), vec("pool_scale"), r_conv[k][None],
                vec("a_log"), vec("dt_bias"), vec("dn_norm_w"), r_out[k][None], vec("final_norm_w"))

    return (loss[0, 0], grad_x2[None], *group(0), *group(1), *group(2), *group(3))
```

```python
import functools

import jax
import jax.numpy as jnp
from jax import lax
from jax.experimental import pallas as pl
from jax.experimental.pallas import tpu as pltpu

F32 = jnp.float32
BF16 = jnp.bfloat16
HI = lax.Precision.HIGHEST
MESH = pl.DeviceIdType.MESH

D_MODEL = 1024
D_POOL = 512
D_DN = 512
POOL_WINDOWS = (2, 4, 8, 16)
POOL_GROUP = 128
DN_HEADS = 4
DN_HEAD_DIM = 128
CONV_WIDTH = 4
CHUNK = 64
NORM_EPS = 1e-6
D_IN = 3080
D_MAIN = 3072
FLAT_ROWS = D_MODEL // 128
D_IN_PAD = D_MAIN + 128
N_DEV = 8
W_IN_SHARD = D_IN // N_DEV
HALO = 16
DN_CHUNKS_PER_STEP = 8
ADAMW_STEPS = 5

ADAM_LR = 0.001
ADAM_B1 = 0.9
ADAM_B2 = 0.999
ADAM_EPS = 1e-08
ADAM_WD = 0.01
ADAM_STEP = 10

VMEM_LIMIT = 56 * 1024 * 1024
def _cp(sem=None, vmem=VMEM_LIMIT):
    kw = {"vmem_limit_bytes": vmem}
    if sem is not None:
        kw["dimension_semantics"] = sem
    return pltpu.CompilerParams(**kw)


def _dot_bf(a, b):
    return jnp.dot(a.astype(BF16), b.astype(BF16), preferred_element_type=F32)


def _dot_nt_bf(a, b):
    return lax.dot_general(a.astype(BF16), b.astype(BF16), (((1,), (1,)), ((), ())), preferred_element_type=F32)


def _dot_tn_bf(a, b):
    return lax.dot_general(a.astype(BF16), b.astype(BF16), (((0,), (0,)), ((), ())), preferred_element_type=F32)


def _mm_raw(a, b, ca, cb, prec):
    off = a.ndim - 2
    dn = (((ca + off,), (cb + off,)), ((0,), (0,)) if off else ((), ()))
    if prec == "hi":
        return lax.dot_general(a, b, dn, precision=HI, preferred_element_type=F32)
    ah, bh = a.astype(BF16), b.astype(BF16)
    out = lax.dot_general(ah, bh, dn, preferred_element_type=F32)
    if prec == "x3":
        al = (a - ah.astype(F32)).astype(BF16)
        bl = (b - bh.astype(F32)).astype(BF16)
        out = out + lax.dot_general(ah, bl, dn, preferred_element_type=F32)
        out = out + lax.dot_general(al, bh, dn, preferred_element_type=F32)
    return out


@functools.partial(jax.custom_vjp, nondiff_argnums=(2, 3, 4, 5))
def _mm(a, b, ca, cb, prec, bprec):
    return _mm_raw(a, b, ca, cb, prec)


def _mm_fwd(a, b, ca, cb, prec, bprec):
    return _mm_raw(a, b, ca, cb, prec), (a, b)


def _mm_bwd(ca, cb, prec, bprec, res, dc):
    a, b = res
    da = _mm_raw(dc, b, 1, 1 - cb, bprec) if ca == 1 else _mm_raw(b, dc, 1 - cb, 1, bprec)
    db = _mm_raw(a, dc, 1 - ca, 0, bprec) if cb == 0 else _mm_raw(dc, a, 0, 1 - ca, bprec)
    return da, db


_mm.defvjp(_mm_fwd, _mm_bwd)


@functools.partial(jax.custom_vjp, nondiff_argnums=(1, 2))
def _tri_inv(a, prec, bprec):
    n = a.shape[-1]
    ii = lax.broadcasted_iota(jnp.int32, (n, n), 0)
    jj = lax.broadcasted_iota(jnp.int32, (n, n), 1)
    p = (ii == jj).astype(F32) - a
    b = _mm_raw(a, a, 1, 0, prec)
    for _ in range(4):
        pb = _mm_raw(jnp.concatenate([p, b], axis=-2), b, 1, 0, prec)
        p = p + pb[..., :n, :]
        b = pb[..., n:, :]
    return p + _mm_raw(p, b, 1, 0, prec)


def _tri_inv_fwd(a, prec, bprec):
    t = _tri_inv(a, prec, bprec)
    return t, t


def _tri_inv_bwd(prec, bprec, t, dt):
    return (-_mm_raw(_mm_raw(t, dt, 0, 0, bprec), t, 1, 1, bprec),)


_tri_inv.defvjp(_tri_inv_fwd, _tri_inv_bwd)

@functools.partial(jax.custom_vjp, nondiff_argnums=(3, 4, 5))
def _mm_known(a, b, out, ca, cb, bprec):
    return out


def _mm_known_fwd(a, b, out, ca, cb, bprec):
    return out, (a, b)


def _mm_known_bwd(ca, cb, bprec, res, dc):
    return _mm_bwd(ca, cb, None, bprec, res, dc) + (jnp.zeros_like(dc),)


_mm_known.defvjp(_mm_known_fwd, _mm_known_bwd)


@jax.custom_vjp
def _use_known(x, known):
    return known


_use_known.defvjp(lambda x, known: (known, None), lambda _, g: (g, jnp.zeros_like(g)))


@functools.partial(jax.custom_vjp, nondiff_argnums=(2,))
def _tri_inv_known(a, t, bprec):
    return t


def _tri_inv_known_fwd(a, t, bprec):
    return t, t


def _tri_inv_known_bwd(bprec, t, dt):
    return _tri_inv_bwd(None, bprec, t, dt) + (jnp.zeros_like(dt),)


_tri_inv_known.defvjp(_tri_inv_known_fwd, _tri_inv_known_bwd)

_DN_PREC = {"akq": ("bf16", "bf16"), "inv": ("bf16", "bf16"), "uw": ("bf16", "bf16"), "ws": ("bf16", "bf16"),
            "ov": ("bf16", "bf16"), "st": ("bf16", "bf16")}


def _silu(x):
    return x * jax.nn.sigmoid(x)


def _softplus(x):
    pos = x > 0.0
    return jnp.where(pos, x, 0.0) + jnp.log1p(jnp.exp(jnp.where(pos, -x, x)))


def _mesh_pos():
    return lax.axis_index("x"), lax.axis_index("y"), lax.axis_index("c")


def _dev_index(x, y, c):
    return 4 * x + 2 * y + c


def _relay_order():
    x, y, c = _mesh_pos()
    n1 = (x + (1 - c) * (1 - 2 * x), y + c * (1 - 2 * y))
    n2 = (x + c * (1 - 2 * x), y + (1 - c) * (1 - 2 * y))
    return (x, y, c), (x, y, 1 - c), n1, n2, (1 - x, 1 - y)


def _all_gather_blocks(outs, send_sems, recv_sems, meanwhile=None, own=None):
    me, sibling, n1, n2, diag = _relay_order()
    c = me[2]

    def copy(a, k, block, to, src=None):
        rows = outs[a].at[_dev_index(*block)]
        return pltpu.make_async_remote_copy(src_ref=rows if src is None else src, dst_ref=rows,
                                            send_sem=send_sems.at[a, k], recv_sem=recv_sems.at[a, k],
                                            device_id=to, device_id_type=MESH)

    n = len(outs)
    started = []

    def start(cp):
        cp.start()
        started.append(cp)

    for a in range(n):
        mine = None if own is None else own[a]
        start(copy(a, 1, me, (*n1, c), mine))
        start(copy(a, 2, me, (*n2, c), mine))
        start(copy(a, 0, me, sibling, mine))
    if meanwhile is not None:
        meanwhile()
    for a in range(n):
        copy(a, 1, (*n1, c), me).wait_recv()
        start(copy(a, 3, (*n1, c), (*n2, c)))
        start(copy(a, 4, (*n1, c), sibling))
    for a in range(n):
        copy(a, 2, (*n2, c), me).wait_recv()
        start(copy(a, 5, (*n2, c), sibling))
        copy(a, 3, (*diag, c), me).wait_recv()
        start(copy(a, 6, (*diag, c), sibling))
    for a in range(n):
        copy(a, 0, sibling, me).wait_recv()
        copy(a, 4, (*n2, 1 - c), me).wait_recv()
        copy(a, 5, (*n1, 1 - c), me).wait_recv()
        copy(a, 6, (*diag, 1 - c), me).wait_recv()
    for cp in started:
        cp.wait_send()


def _peer_relations():
    x, y, c = _mesh_pos()
    flips = [(fx, fy, fc) for fx in (0, 1) for fy in (0, 1) for fc in (0, 1)][1:]
    peers = [(1 - x if fx else x, 1 - y if fy else y, 1 - c if fc else c) for fx, fy, fc in flips]
    return (x, y, c), peers


def _direct_gather_start(out_ref, send_sems, recv_sems):
    me, peers = _peer_relations()
    rows = out_ref.at[_dev_index(*me)]
    for k, peer in enumerate(peers):
        pltpu.make_async_remote_copy(src_ref=rows, dst_ref=rows, send_sem=send_sems.at[k], recv_sem=recv_sems.at[k],
                                     device_id=peer, device_id_type=MESH).start()


def _direct_gather_wait(out_ref, send_sems, recv_sems):
    me, peers = _peer_relations()
    for k, peer in enumerate(peers):
        rows = out_ref.at[_dev_index(*peer)]
        cp = pltpu.make_async_remote_copy(src_ref=rows, dst_ref=rows, send_sem=send_sems.at[k],
                                          recv_sem=recv_sems.at[k], device_id=peer, device_id_type=MESH)
        cp.wait_recv()
        cp.wait_send()


def _direct_scatter_start(send_ref, recv_ref, send_sems, recv_sems):
    me, peers = _peer_relations()
    for k, peer in enumerate(peers):
        pltpu.make_async_remote_copy(src_ref=send_ref.at[_dev_index(*peer)], dst_ref=recv_ref.at[k],
                                     send_sem=send_sems.at[k], recv_sem=recv_sems.at[k],
                                     device_id=peer, device_id_type=MESH).start()


def _direct_scatter_wait(send_ref, recv_ref, send_sems, recv_sems):
    me, peers = _peer_relations()
    for k, peer in enumerate(peers):
        cp = pltpu.make_async_remote_copy(src_ref=send_ref.at[_dev_index(*peer)], dst_ref=recv_ref.at[k],
                                          send_sem=send_sems.at[k], recv_sem=recv_sems.at[k],
                                          device_id=peer, device_id_type=MESH)
        cp.wait_recv()
        cp.wait_send()


def _gather_weights(w_in_flat, conv_blk, x2, norm_w, tn):
    nt = x2.shape[0] // tn

    def body(win_ref, conv_ref, x_hbm, nw_ref, gin_hbm, gconv_ref, n_hbm, own_ref, xbuf, nbuf, send_sems, recv_sems,
             xsem, nsem, own_sem):
        x, y, c = _mesh_pos()
        me = _dev_index(x, y, c)
        for j in range(FLAT_ROWS):
            own_ref[:, 128 * j:128 * (j + 1)] = win_ref[pl.ds(j, W_IN_SHARD, stride=FLAT_ROWS), :].astype(BF16)
        keep_own = pltpu.make_async_copy(own_ref, gin_hbm.at[me], own_sem.at[0])
        keep_own.start()
        gconv_ref[me] = conv_ref[...]

        def norm_x():
            def load(t):
                return pltpu.make_async_copy(x_hbm.at[pl.ds(t * tn, tn), :], xbuf.at[t % 2], xsem.at[t % 2])

            def store(t):
                return pltpu.make_async_copy(nbuf.at[t % 2], n_hbm.at[pl.ds(t * tn, tn), :], nsem.at[t % 2])

            load(0).start()
            for t in range(nt):
                if t + 1 < nt:
                    load(t + 1).start()
                load(t).wait()
                if t >= 2:
                    store(t - 2).wait()
                xhat, _ = _rms_hat(xbuf[t % 2])
                nbuf[t % 2] = (xhat * nw_ref[...]).astype(BF16)
                store(t).start()
            for t in range(max(nt - 2, 0), nt):
                store(t).wait()

        _all_gather_blocks((gin_hbm, gconv_ref), send_sems, recv_sems, meanwhile=norm_x, own=(own_ref, None))
        keep_own.wait()

    vm = pl.BlockSpec(memory_space=pltpu.VMEM)
    hbm = pl.BlockSpec(memory_space=pl.ANY)
    return pl.pallas_call(
        body, name="gather_weights",
        out_shape=(jax.ShapeDtypeStruct((N_DEV, W_IN_SHARD, D_MODEL), BF16),
                   jax.ShapeDtypeStruct((N_DEV,) + conv_blk.shape, F32),
                   jax.ShapeDtypeStruct(x2.shape, BF16)),
        in_specs=[vm, vm, hbm, vm], out_specs=(hbm, vm, hbm),
        scratch_shapes=[pltpu.VMEM((W_IN_SHARD, D_MODEL), BF16),
                        pltpu.VMEM((2, tn, D_MODEL), F32), pltpu.VMEM((2, tn, D_MODEL), BF16),
                        pltpu.SemaphoreType.DMA((2, 7)), pltpu.SemaphoreType.DMA((2, 7)),
                        pltpu.SemaphoreType.DMA((2,)), pltpu.SemaphoreType.DMA((2,)), pltpu.SemaphoreType.DMA((1,))],
        compiler_params=_cp(),
    )(w_in_flat, conv_blk, x2, norm_w)


def _reduce_grads(big, like, pack_a, pack_b):
    nb = len(big)

    def body(*refs):
        srcs, (pa_ref, pb_ref), outs, (ga_ref, gb_ref) = (
            refs[:nb], refs[nb:nb + 2], refs[nb + 2:2 * nb + 2], refs[2 * nb + 2:2 * nb + 4])
        scr = refs[2 * nb + 4:]
        r1s, r2s, sbs, sts = (scr[k * nb:(k + 1) * nb] for k in range(4))
        s1_send, s1_recv, s2_send, s2_recv, ag_send, ag_recv, st_sem = scr[4 * nb:]
        x, y, c = _mesh_pos()
        me = (x, y, c)
        sibling = (x, y, 1 - c)

        ga_ref[_dev_index(*me)] = pa_ref[...]
        gb_ref[_dev_index(*me)] = pb_ref[...]

        _, _, n1, n2, diag = _relay_order()
        order = ((diag, 3), (n1, 1 + c), (n2, 2 - c), ((x, y), 0))

        def p1(a, i, to):
            chip, slot = order[i]
            return pltpu.make_async_remote_copy(
                src_ref=srcs[a].at[_dev_index(*chip, 1 - c)], dst_ref=r1s[a].at[slot],
                send_sem=s1_send.at[a, slot], recv_sem=s1_recv.at[a, slot], device_id=to, device_id_type=MESH)

        def p1_landed(a, slot):
            return pltpu.make_async_remote_copy(
                src_ref=r1s[a].at[slot], dst_ref=r1s[a].at[slot], send_sem=s1_send.at[a, slot],
                recv_sem=s1_recv.at[a, slot], device_id=me, device_id_type=MESH)

        def p2(a, k, to):
            return pltpu.make_async_remote_copy(
                src_ref=sbs[a].at[k], dst_ref=r2s[a].at[k],
                send_sem=s2_send.at[a, k], recv_sem=s2_recv.at[a, k], device_id=to, device_id_type=MESH)

        def stage(a, i):
            return pltpu.make_async_copy(srcs[a].at[_dev_index(*order[i][0], c)], sts[a].at[i % 2], st_sem.at[a, i % 2])

        sends = [p1(a, i, sibling) for a in range(nb) for i in (0, 2, 1, 3)]
        for cp in sends:
            cp.start()

        def step_1():
            for a in range(nb):
                stage(a, 0).start()
                for i, (_, slot) in enumerate(order):
                    if i + 1 < len(order):
                        stage(a, i + 1).start()
                    stage(a, i).wait()
                    p1_landed(a, slot).wait_recv()
                    chip_sum = r1s[a][slot] + sts[a][i % 2]
                    if i < 2:
                        sbs[a][i] = chip_sum.astype(BF16)
                        sends.append(p2(a, i, (*n1, c)))
                        sends[-1].start()
                    else:
                        r1s[a][slot] = chip_sum

        _all_gather_blocks((ga_ref, gb_ref), ag_send, ag_recv, meanwhile=step_1)
        for a in range(nb):
            p2(a, 0, me).wait_recv()
            sbs[a][2] = (r1s[a][2 - c] + r2s[a][0].astype(F32)).astype(BF16)
            sends.append(p2(a, 2, (*n2, c)))
            sends[-1].start()
        for a in range(nb):
            p2(a, 1, me).wait_recv()
            p2(a, 2, me).wait_recv()
            total = (r1s[a][0] + r2s[a][1].astype(F32)) + r2s[a][2].astype(F32)
            if outs[a].shape == total.shape:
                outs[a][...] = total
            else:
                r1s[a][0] = total
                for j in range(FLAT_ROWS):
                    outs[a][pl.ds(j, W_IN_SHARD, stride=FLAT_ROWS), :] = r1s[a][0, :, 128 * j:128 * (j + 1)]
        for cp in sends:
            cp.wait_send()

    vm = pl.BlockSpec(memory_space=pltpu.VMEM)
    hbm = pl.BlockSpec(memory_space=pl.ANY)
    blk = [p.shape[1:] for p in big]
    scratch = ([pltpu.VMEM((4,) + b, F32) for b in blk] + [pltpu.VMEM((3,) + b, BF16) for b in blk]
               + [pltpu.VMEM((3,) + b, BF16) for b in blk] + [pltpu.VMEM((2,) + b, F32) for b in blk]
               + [pltpu.SemaphoreType.DMA((nb, 4)), pltpu.SemaphoreType.DMA((nb, 4)),
                  pltpu.SemaphoreType.DMA((nb, 3)), pltpu.SemaphoreType.DMA((nb, 3)),
                  pltpu.SemaphoreType.DMA((2, 7)), pltpu.SemaphoreType.DMA((2, 7)),
                  pltpu.SemaphoreType.DMA((nb, 2))])
    return pl.pallas_call(
        body, name="reduce_grads",
        out_shape=tuple(jax.ShapeDtypeStruct(w.shape, F32) for w in like)
        + (jax.ShapeDtypeStruct((N_DEV,) + pack_a.shape, F32), jax.ShapeDtypeStruct((N_DEV,) + pack_b.shape, F32)),
        in_specs=[hbm] * nb + [vm, vm], out_specs=(vm,) * (nb + 2),
        scratch_shapes=scratch,
        compiler_params=_cp(),
    )(*big, pack_a, pack_b)


def _rms_hat(xf):
    r = lax.rsqrt(jnp.mean(xf * xf, axis=-1, keepdims=True) + NORM_EPS)
    return xf * r, r


def _shift_up(cur, next_head, s):
    ext = jnp.concatenate([cur, next_head], axis=0)
    n = ext.shape[0]
    return pltpu.roll(ext, n - s, 0)[:cur.shape[0], :]


def _pool_counts(i, tp, w):
    t = i * tp + lax.broadcasted_iota(jnp.int32, (tp, 1), 0)
    return jnp.minimum(t + 1, w).astype(F32)


def _pool_mix(u, u_prev_tail, i, tp):
    win = jnp.concatenate([u_prev_tail, u], axis=0)
    mixes = []
    for gi, w in enumerate(POOL_WINDOWS):
        win = win + pltpu.roll(win, w // 2, 0)
        cols = slice(gi * POOL_GROUP, (gi + 1) * POOL_GROUP)
        mixes.append(win[HALO:, :POOL_GROUP] / _pool_counts(i, tp, w) - u[:, cols])
        if gi + 1 < len(POOL_WINDOWS):
            win = win[:, POOL_GROUP:]
    return mixes


def _prev_tail(ref, i):
    return jnp.where(i > 0, ref[...], 0.0)


def _conv_taps(cur, prev_tail):
    ext = jnp.concatenate([prev_tail, cur], axis=0)
    return [cur] + [pltpu.roll(ext, sft, 0)[HALO:, :] for sft in range(1, CONV_WIDTH)]


def _conv_of_taps(taps, w4):
    y = taps[0] * w4[CONV_WIDTH - 1:CONV_WIDTH, :]
    for sft in range(1, CONV_WIDTH):
        y = y + taps[sft] * w4[CONV_WIDTH - 1 - sft:CONV_WIDTH - sft, :]
    return y


def _conv_fwd(cur, prev_tail, w4):
    ext = jnp.concatenate([prev_tail, cur], axis=0)
    y = ext * w4[CONV_WIDTH - 1:CONV_WIDTH, :]
    for sft in range(1, CONV_WIDTH):
        y = y + pltpu.roll(ext, sft, 0) * w4[CONV_WIDTH - 1 - sft:CONV_WIDTH - sft, :]
    return y[HALO:, :]


def _l2n_heads(t):
    parts = []
    for h in range(DN_HEADS):
        th = t[:, h * DN_HEAD_DIM:(h + 1) * DN_HEAD_DIM]
        parts.append(th * lax.rsqrt(jnp.sum(th * th, axis=-1, keepdims=True) + NORM_EPS))
    return jnp.concatenate(parts, axis=1)


def _post_conv(yq, yk, yv):
    return _l2n_heads(_silu(yq)), _l2n_heads(_silu(yk)), _silu(yv)


def _gates(ba, alog_lane, dtb_lane):
    lane = lax.broadcasted_iota(jnp.int32, ba.shape, 1)
    beta = jax.nn.sigmoid(ba)
    g = -jnp.exp(alog_lane) * _softplus(ba + dtb_lane)
    return jnp.where(lane < DN_HEADS, beta, jnp.where(lane < 2 * DN_HEADS, g, 0.0))


def _front(n_all, g_in, pool_w, pool_scale, conv_full, alog_lane, dtb_lane, w_out_blk, tm):
    s = n_all.shape[0]

    def body(n_ref, pw_ref, ps_ref, cw_ref, al_ref, db_ref, wo_ref, g_hbm,
             pm_ref, pb_ref, yp_ref, qn_ref, kn_ref, vv_ref, gb_ref, gwo_hbm, wt_hbm,
             g_vmem, wt_vmem, tail_u, tail_qkv, gwo_ref, sem, g_sems, wo_send, wo_recv):
        i = pl.program_id(0)

        @pl.when(i == 0)
        def _():
            gwo_ref[_dev_index(*_mesh_pos())] = wo_ref[...].astype(BF16)
            _direct_gather_start(gwo_ref, wo_send, wo_recv)
            loads = [pltpu.make_async_copy(g_hbm.at[d], g_vmem.at[d], g_sems.at[d]) for d in range(N_DEV)]
            for cp in loads:
                cp.start()
            wt_vmem[D_MAIN:, :] = jnp.zeros((D_IN_PAD - D_MAIN, D_MODEL), BF16)
            tail_u[...] = jnp.zeros_like(tail_u)
            tail_qkv[...] = jnp.zeros_like(tail_qkv)
            for d, cp in enumerate(loads):
                cp.wait()
                wt_vmem[W_IN_SHARD * d:W_IN_SHARD * (d + 1), :] = g_vmem[d]
            pltpu.make_async_copy(wt_vmem, wt_hbm, sem).start()
        n = n_ref[...]
        pm_ref[...] = _dot_nt_bf(n, wt_vmem[:D_MAIN, :])
        pb = _dot_nt_bf(n, wt_vmem[D_MAIN:, :])
        pb_ref[...] = pb
        u = pm_ref[:, :D_POOL]
        mixes = _pool_mix(u, tail_u[...], i, tm)
        tail_u[...] = u[tm - HALO:, :]
        gate = ps_ref[...] * _silu(pm_ref[:, D_POOL:2 * D_POOL])
        for gi in range(4):
            cols = slice(gi * POOL_GROUP, (gi + 1) * POOL_GROUP)
            yp_ref[:, cols] = _dot_bf(mixes[gi], pw_ref[gi]) * gate[:, cols]
        ys = []
        for c in range(3):
            cols = slice(c * D_DN, (c + 1) * D_DN)
            cur = pm_ref[:, 2 * D_POOL + c * D_DN:2 * D_POOL + (c + 1) * D_DN]
            ys.append(_conv_fwd(cur, tail_qkv[:, cols], cw_ref[:, cols]))
            tail_qkv[:, cols] = cur[tm - HALO:, :]
        qn, kn, vv = _post_conv(*ys)
        qn_ref[...] = qn
        kn_ref[...] = kn
        vv_ref[...] = vv
        gb_ref[...] = _gates(pb, al_ref[...], db_ref[...])

        @pl.when(i == s // tm - 1)
        def _():
            pltpu.make_async_copy(wt_vmem, wt_hbm, sem).wait()
            _direct_gather_wait(gwo_ref, wo_send, wo_recv)
            out = pltpu.make_async_copy(gwo_ref, gwo_hbm, sem)
            out.start()
            out.wait()

    tile = pl.BlockSpec((tm, D_DN), lambda i: (i, 0))
    lanes = pl.BlockSpec((tm, 128), lambda i: (i, 0))
    row = pl.BlockSpec((1, 128), lambda i: (0, 0))
    return pl.pallas_call(
        body, name="front", grid=(s // tm,),
        out_shape=(jax.ShapeDtypeStruct((s, D_MAIN), F32), jax.ShapeDtypeStruct((s, 128), F32),
                   jax.ShapeDtypeStruct((s, D_POOL), F32), jax.ShapeDtypeStruct((s, D_DN), F32),
                   jax.ShapeDtypeStruct((s, D_DN), F32), jax.ShapeDtypeStruct((s, D_DN), F32),
                   jax.ShapeDtypeStruct((s, 128), F32), jax.ShapeDtypeStruct((N_DEV,) + w_out_blk.shape, BF16),
                   jax.ShapeDtypeStruct((D_IN_PAD, D_MODEL), BF16)),
        in_specs=[pl.BlockSpec((tm, D_MODEL), lambda i: (i, 0)),
                  pl.BlockSpec((4, POOL_GROUP, POOL_GROUP), lambda i: (0, 0, 0)),
                  pl.BlockSpec((1, D_POOL), lambda i: (0, 0)),
                  pl.BlockSpec((CONV_WIDTH, 3 * D_DN), lambda i: (0, 0)), row, row,
                  pl.BlockSpec(memory_space=pltpu.VMEM), pl.BlockSpec(memory_space=pl.ANY)],
        out_specs=(pl.BlockSpec((tm, D_MAIN), lambda i: (i, 0)), lanes, tile, tile, tile, tile, lanes,
                   pl.BlockSpec(memory_space=pl.ANY), pl.BlockSpec(memory_space=pl.ANY)),
        scratch_shapes=[pltpu.VMEM((N_DEV, W_IN_SHARD, D_MODEL), BF16), pltpu.VMEM((D_IN_PAD, D_MODEL), BF16),
                        pltpu.VMEM((HALO, D_POOL), F32), pltpu.VMEM((HALO, 3 * D_DN), F32),
                        pltpu.VMEM((N_DEV,) + w_out_blk.shape, BF16),
                        pltpu.SemaphoreType.DMA, pltpu.SemaphoreType.DMA((N_DEV,)),
                        pltpu.SemaphoreType.DMA((7,)), pltpu.SemaphoreType.DMA((7,))],
        compiler_params=_cp(("arbitrary",)),
    )(n_all, pool_w, pool_scale, conv_full, alog_lane, dtb_lane, w_out_blk, g_in)


def _dn_block(q, k, v, gcol, bcol, state, known=None):
    nb, n, d = q.shape
    ii = lax.broadcasted_iota(jnp.int32, (n, n), 0)
    jj = lax.broadcasted_iota(jnp.int32, (n, n), 1)
    lower = ii >= jj
    eye = (ii == jj).astype(F32)
    g_row = jnp.sum(eye * gcol, axis=1, keepdims=True)
    gc_col = jnp.sum(jnp.where(lower, g_row, 0.0), axis=2, keepdims=True)
    gc_row = jnp.sum(eye * gc_col, axis=1, keepdims=True)
    decay = jnp.where(lower, jnp.exp(jnp.where(lower, gc_col - gc_row, 0.0)), 0.0)
    kb = k * bcol
    vb = v * bcol
    qs = q * (DN_HEAD_DIM ** -0.5)
    egc = jnp.exp(gc_col)
    kq = jnp.concatenate([kb, qs], axis=1)
    vk = jnp.concatenate([vb, kb * egc], axis=2)
    if known is None:
        akq = _mm(kq, k, 1, 1, *_DN_PREC["akq"])
    else:
        akq = _mm_known(kq, k, known[0][:, :, :n].astype(F32), 1, 1, _DN_PREC["akq"][1])
    a = jnp.where(ii > jj, akq[:, :n] * decay, 0.0)
    qk = akq[:, n:] * decay
    if known is None:
        t = _tri_inv(a, *_DN_PREC["inv"])
        uw = _mm(t, vk, 1, 0, *_DN_PREC["uw"])
    else:
        t = _tri_inv_known(a, known[0][:, :n, n:].astype(F32), _DN_PREC["inv"][1])
        uw = _mm_known(t, vk, known[1], 1, 0, _DN_PREC["uw"][1])
    pre = jnp.concatenate([akq, jnp.concatenate([t, jnp.zeros_like(t)], axis=1)], axis=2)
    wq = jnp.concatenate([uw[:, :, d:], qs * egc], axis=1)
    g_last = gc_col[:, n - 1:n, :]
    k_dec = k * jnp.exp(g_last - gc_col)
    e_last = jnp.exp(g_last)
    os_, starts = [], []
    for c in range(nb // DN_HEADS):
        sl = slice(c * DN_HEADS, (c + 1) * DN_HEADS)
        if known is not None and c > 0:
            state = _use_known(state, known[2][sl])
        starts.append(state)
        ws = _mm(wq[sl], state, 1, 0, *_DN_PREC["ws"])
        v_new = uw[sl, :, :d] - ws[:, :n]
        os_.append(ws[:, n:] + _mm(qk[sl], v_new, 1, 0, *_DN_PREC["ov"]))
        state = state * e_last[sl] + _mm(k_dec[sl], v_new, 0, 0, *_DN_PREC["st"])
    return jnp.concatenate(os_, axis=0), state, (pre, uw, jnp.concatenate(starts, axis=0))


def _gated_norm(o, dz, nw):
    parts = []
    for h in range(DN_HEADS):
        oh = o[:, h * DN_HEAD_DIM:(h + 1) * DN_HEAD_DIM]
        parts.append(oh * lax.rsqrt(jnp.mean(oh * oh, axis=-1, keepdims=True) + NORM_EPS) * nw)
    return jnp.concatenate(parts, axis=1) * _silu(dz)


def _dn_block_args(gc, q_ref, k_ref, v_ref, gb_ref):
    qs, ks, vs, gs, bs = [], [], [], [], []
    for cc in range(gc):
        r = slice(cc * CHUNK, (cc + 1) * CHUNK)
        gbv = gb_ref[r, :]
        for h in range(DN_HEADS):
            cols = slice(h * DN_HEAD_DIM, (h + 1) * DN_HEAD_DIM)
            qs.append(q_ref[r, cols])
            ks.append(k_ref[r, cols])
            vs.append(v_ref[r, cols])
            gs.append(gbv[:, DN_HEADS + h:DN_HEADS + h + 1])
            bs.append(gbv[:, h:h + 1])
    return tuple(jnp.stack(t, axis=0) for t in (qs, ks, vs, gs, bs))


def _dn_scan_fwd(qn, kn, vv, gb, gc):
    s = qn.shape[0]
    nchunk = s // CHUNK
    rows = gc * CHUNK

    def body(q_ref, k_ref, v_ref, gb_ref, y_ref, ss_ref, pre_ref, uw_ref, state):
        @pl.when(pl.program_id(0) == 0)
        def _():
            state[...] = jnp.zeros_like(state)
        q, k, v, gcol, bcol = _dn_block_args(gc, q_ref, k_ref, v_ref, gb_ref)
        y, new, (pre, uw, starts) = _dn_block(q, k, v, gcol, bcol, state[...])
        state[...] = new
        ss_ref[...] = starts
        pre_ref[...] = pre
        uw_ref[...] = uw
        for cc in range(gc):
            for h in range(DN_HEADS):
                y_ref[cc * CHUNK:(cc + 1) * CHUNK, h * DN_HEAD_DIM:(h + 1) * DN_HEAD_DIM] = y[cc * DN_HEADS + h]

    tile = pl.BlockSpec((rows, D_DN), lambda i: (i, 0))
    return pl.pallas_call(
        body, name="dn_scan_fwd", grid=(nchunk // gc,),
        out_shape=(jax.ShapeDtypeStruct((s, D_DN), F32),
                   jax.ShapeDtypeStruct((nchunk * DN_HEADS, DN_HEAD_DIM, DN_HEAD_DIM), F32),
                   jax.ShapeDtypeStruct((nchunk * DN_HEADS, 2 * CHUNK, 2 * CHUNK), F32),
                   jax.ShapeDtypeStruct((nchunk * DN_HEADS, CHUNK, 2 * DN_HEAD_DIM), F32)),
        in_specs=[tile, tile, tile, pl.BlockSpec((rows, 128), lambda i: (i, 0))],
        out_specs=(tile, pl.BlockSpec((gc * DN_HEADS, DN_HEAD_DIM, DN_HEAD_DIM), lambda i: (i, 0, 0)),
                   pl.BlockSpec((gc * DN_HEADS, 2 * CHUNK, 2 * CHUNK), lambda i: (i, 0, 0)),
                   pl.BlockSpec((gc * DN_HEADS, CHUNK, 2 * DN_HEAD_DIM), lambda i: (i, 0, 0))),
        scratch_shapes=[pltpu.VMEM((DN_HEADS, DN_HEAD_DIM, DN_HEAD_DIM), F32)],
        compiler_params=_cp(("arbitrary",)),
    )(qn, kn, vv, gb)


def _out_proj_loss(y_pool, o_dn, proj_main, dn_norm_w, x2, tgt, w_out_full, fnw, tm):
    s = x2.shape[0]

    def body(yp_ref, o_ref, dz_ref, nw_ref, x_ref, t_ref, wo_ref, fw_ref,
             dh_ref, dyp_ref, do_ref, ddz_ref, gwo_ref, gfw_ref, loss_ref, dnw_ref):
        @pl.when(pl.program_id(0) == 0)
        def _():
            gwo_ref[...] = jnp.zeros_like(gwo_ref)
            gfw_ref[...] = jnp.zeros_like(gfw_ref)
            loss_ref[...] = jnp.zeros_like(loss_ref)
            dnw_ref[...] = jnp.zeros_like(dnw_ref)
        y_dn, gate_vjp = jax.vjp(_gated_norm, o_ref[...], dz_ref[...], nw_ref[...])
        y = jnp.concatenate([yp_ref[...], y_dn], axis=1).astype(BF16)
        wo = wo_ref[...]
        h = x_ref[...] + jnp.dot(y, wo, preferred_element_type=F32)
        hn, r = _rms_hat(h)
        fw = fw_ref[...]
        err = hn * fw - t_ref[...]
        loss_ref[...] += 0.5 * jnp.sum(jnp.sum(err * err, axis=-1, keepdims=True) / D_MODEL, axis=0, keepdims=True)
        dout = err / D_MODEL
        gfw_ref[...] += jnp.sum(dout * hn, axis=0, keepdims=True)
        dhn = dout * fw
        dh = r * (dhn - hn * jnp.mean(dhn * hn, axis=-1, keepdims=True))
        dh_ref[...] = dh
        dhb = dh.astype(BF16)
        dy = _dot_nt_bf(dhb, wo)
        dyp_ref[...] = dy[:, :D_POOL]
        do, ddz, dnw = gate_vjp(dy[:, D_POOL:])
        do_ref[...] = do
        ddz_ref[...] = ddz
        dnw_ref[...] += dnw
        gwo_ref[...] += _dot_tn_bf(y, dhb)

    half = pl.BlockSpec((tm, D_POOL), lambda i: (i, 0))
    full = pl.BlockSpec((tm, D_MODEL), lambda i: (i, 0))
    lrow = pl.BlockSpec((1, 128), lambda i: (0, 0))
    return pl.pallas_call(
        body, name="out_proj_loss", grid=(s // tm,),
        out_shape=(jax.ShapeDtypeStruct((s, D_MODEL), F32), jax.ShapeDtypeStruct((s, D_POOL), F32),
                   jax.ShapeDtypeStruct((s, D_DN), F32), jax.ShapeDtypeStruct((s, D_DN), F32),
                   jax.ShapeDtypeStruct((D_MODEL, D_MODEL), F32),
                   jax.ShapeDtypeStruct((1, D_MODEL), F32), jax.ShapeDtypeStruct((1, 128), F32),
                   jax.ShapeDtypeStruct((1, 128), F32)),
        in_specs=[half, half, pl.BlockSpec((tm, D_DN), lambda i: (i, 5)), lrow, full, full,
                  pl.BlockSpec((D_MODEL, D_MODEL), lambda i: (0, 0)), pl.BlockSpec((1, D_MODEL), lambda i: (0, 0))],
        out_specs=(full, half, half, half, pl.BlockSpec((D_MODEL, D_MODEL), lambda i: (0, 0)),
                   pl.BlockSpec((1, D_MODEL), lambda i: (0, 0)), lrow, lrow),
        compiler_params=_cp(("arbitrary",)),
    )(y_pool, o_dn, proj_main, dn_norm_w, x2, tgt, w_out_full, fnw)


def _dn_scan_bwd(qn, kn, vv, gb, states, pre, uw, do_dn, p_out, gc):
    s = qn.shape[0]
    nchunk = s // CHUNK
    nstep = nchunk // gc
    rows = gc * CHUNK

    def body(q_ref, k_ref, v_ref, gb_ref, ss_ref, pre_ref, uw_ref, dy_ref, po_ref,
             dq_ref, dk_ref, dv_ref, dgb_ref, gro_ref, dstate, po_send, po_recv, rs_send, rs_recv):
        @pl.when(pl.program_id(0) == 0)
        def _():
            dstate[...] = jnp.zeros_like(dstate)
            po_send[...] = po_ref[...].astype(BF16)
            _direct_scatter_start(po_send, po_recv, rs_send, rs_recv)

        @pl.when(pl.program_id(0) == nstep - 1)
        def _():
            _direct_scatter_wait(po_send, po_recv, rs_send, rs_recv)
            total = po_ref[_dev_index(*_mesh_pos())]
            for k in range(N_DEV - 1):
                total = total + po_recv[k].astype(F32)
            gro_ref[...] = total
        lane = lax.broadcasted_iota(jnp.int32, (CHUNK, 128), 1)
        q, k, v, gcol, bcol = _dn_block_args(gc, q_ref, k_ref, v_ref, gb_ref)
        dy = jnp.stack([dy_ref[cc * CHUNK:(cc + 1) * CHUNK, h * DN_HEAD_DIM:(h + 1) * DN_HEAD_DIM]
                        for cc in range(gc) for h in range(DN_HEADS)], axis=0)
        known = (pre_ref[...], uw_ref[...], ss_ref[...])
        _, vjp = jax.vjp(lambda *a: _dn_block(*a, known=known)[:2], q, k, v, gcol, bcol, ss_ref[:DN_HEADS])
        dq, dk, dv, dg, db, dst = vjp((dy, dstate[...]))
        dstate[...] = dst
        for cc in range(gc):
            r = slice(cc * CHUNK, (cc + 1) * CHUNK)
            dgb = jnp.zeros((CHUNK, 128), F32)
            for h in range(DN_HEADS):
                b = cc * DN_HEADS + h
                cols = slice(h * DN_HEAD_DIM, (h + 1) * DN_HEAD_DIM)
                for ref, val in zip((dq_ref, dk_ref, dv_ref), (dq, dk, dv)):
                    ref[r, cols] = val[b]
                dgb = dgb + jnp.where(lane == h, db[b], 0.0) + jnp.where(lane == DN_HEADS + h, dg[b], 0.0)
            dgb_ref[r, :] = dgb

    rev = lambda i: (nstep - 1 - i, 0)
    tile = pl.BlockSpec((rows, D_DN), rev)
    lanes = pl.BlockSpec((rows, 128), rev)
    return pl.pallas_call(
        body, name="dn_scan_bwd", grid=(nstep,),
        out_shape=(jax.ShapeDtypeStruct((s, D_DN), F32),) * 3
        + (jax.ShapeDtypeStruct((s, 128), F32), jax.ShapeDtypeStruct(p_out.shape[1:], F32)),
        in_specs=[tile, tile, tile, lanes,
                  pl.BlockSpec((gc * DN_HEADS, DN_HEAD_DIM, DN_HEAD_DIM), lambda i: (nstep - 1 - i, 0, 0)),
                  pl.BlockSpec((gc * DN_HEADS, 2 * CHUNK, 2 * CHUNK), lambda i: (nstep - 1 - i, 0, 0)),
                  pl.BlockSpec((gc * DN_HEADS, CHUNK, 2 * DN_HEAD_DIM), lambda i: (nstep - 1 - i, 0, 0)), tile,
                  pl.BlockSpec(memory_space=pltpu.VMEM)],
        out_specs=(tile, tile, tile, lanes, pl.BlockSpec(memory_space=pltpu.VMEM)),
        scratch_shapes=[pltpu.VMEM((DN_HEADS, DN_HEAD_DIM, DN_HEAD_DIM), F32),
                        pltpu.VMEM(p_out.shape, BF16), pltpu.VMEM((N_DEV - 1,) + p_out.shape[1:], BF16),
                        pltpu.SemaphoreType.DMA((7,)), pltpu.SemaphoreType.DMA((7,))],
        compiler_params=_cp(("arbitrary",)),
    )(qn, kn, vv, gb, states, pre, uw, do_dn, p_out)


def _back(proj_main, proj_ba, dyp, dqn, dkn, dvv, dgb, ddz, x2, dh, norm_w, pool_w, pool_scale, conv_full,
          alog_lane, dtb_lane, wt_full, tm):
    s = x2.shape[0]
    nstep = s // tm
    per = tm // HALO

    def body(u_ref, z_ref, q_ref, k_ref, v_ref, up_ref, qp_ref, kp_ref, vp_ref, ba_ref,
             dyp_ref, dqn_ref, dkn_ref, dvv_ref, dgb_ref, ddz_ref, x_ref, dh_ref,
             nw_ref, pw_ref, ps_ref, cw_ref, al_ref, db_ref, wt_hbm,
             gx_ref, p_hbm, gnw_ref, dpw_ref, dps_ref, dcw_ref, dal_ref, ddb_ref,
             wt_vmem, acc, blk, head_dc, head_dw, sem, osem):
        j = pl.program_id(0)
        i = nstep - 1 - j

        @pl.when(j == 0)
        def _():
            cp = pltpu.make_async_copy(wt_hbm, wt_vmem, sem)
            cp.start()
            acc[...] = jnp.zeros_like(acc)
            for ref in (gnw_ref, dpw_ref, dps_ref, dcw_ref, dal_ref, ddb_ref, head_dc, head_dw):
                ref[...] = jnp.zeros_like(ref)
            cp.wait()

        u = u_ref[...]
        z = z_ref[...]
        dy = dyp_ref[...]
        ps = ps_ref[...]
        mixes = _pool_mix(u, _prev_tail(up_ref, i), i, tm)
        sg = jax.nn.sigmoid(z)
        sz = z * sg
        dsz = sg * (1.0 + z * (1.0 - sg))
        dzs, dwins = [], []
        for gi, w in enumerate(POOL_WINDOWS):
            cols = slice(gi * POOL_GROUP, (gi + 1) * POOL_GROUP)
            mixw = _dot_bf(mixes[gi], pw_ref[gi])
            dmixw = dy[:, cols] * ps[:, cols] * sz[:, cols]
            dps_ref[:, cols] += jnp.sum(dy[:, cols] * mixw * sz[:, cols], axis=0, keepdims=True)
            dzs.append(dy[:, cols] * mixw * ps[:, cols] * dsz[:, cols])
            dpw_ref[gi] += _dot_tn_bf(mixes[gi], dmixw)
            dwins.append(_dot_nt_bf(dmixw, pw_ref[gi]) / _pool_counts(i, tm, w))
        dzp = jnp.concatenate(dzs, axis=1)
        dw = jnp.concatenate(dwins, axis=1)
        win = jnp.concatenate([dw, head_dw[...]], axis=0)
        m = win.shape[0]
        dups = []
        for gi, w in enumerate(POOL_WINDOWS):
            win = win + pltpu.roll(win, m - w // 2, 0)
            cols = slice(gi * POOL_GROUP, (gi + 1) * POOL_GROUP)
            dups.append(win[:tm, :POOL_GROUP] - dw[:, cols] * _pool_counts(i, tm, w))
            if gi + 1 < len(POOL_WINDOWS):
                win = win[:, POOL_GROUP:]
        dup = jnp.concatenate(dups, axis=1)
        head_dw[...] = dw[:HALO, :]

        curs = (q_ref[...], k_ref[...], v_ref[...])
        tails = (_prev_tail(qp_ref, i), _prev_tail(kp_ref, i), _prev_tail(vp_ref, i))
        taps = [_conv_taps(curs[c], tails[c]) for c in range(3)]
        ys = [_conv_of_taps(taps[c], cw_ref[:, c * D_DN:(c + 1) * D_DN]) for c in range(3)]
        _, vjp = jax.vjp(_post_conv, *ys)
        dys = vjp((dqn_ref[...], dkn_ref[...], dvv_ref[...]))
        dxs = []
        for c, dyc in enumerate(dys):
            cols = slice(c * D_DN, (c + 1) * D_DN)
            w4 = cw_ref[:, cols]
            for sft in range(CONV_WIDTH):
                row = CONV_WIDTH - 1 - sft
                dcw_ref[row:row + 1, cols] += jnp.sum(dyc * taps[c][sft], axis=0, keepdims=True)
            head = head_dc[:, cols]
            dx = dyc * w4[CONV_WIDTH - 1:CONV_WIDTH, :]
            for sft in range(1, CONV_WIDTH):
                dx = dx + _shift_up(dyc, head, sft) * w4[CONV_WIDTH - 1 - sft:CONV_WIDTH - sft, :]
            dxs.append(dx)
            head_dc[:, cols] = dyc[:HALO, :]
        _, gvjp = jax.vjp(_gates, ba_ref[...], al_ref[...], db_ref[...])
        dba, dal, ddb = gvjp(dgb_ref[...])
        dal_ref[...] += dal
        ddb_ref[...] += ddb

        dbab = dba.astype(BF16)
        xhat, r = _rms_hat(x_ref[...])
        nw = nw_ref[...]
        n = (xhat * nw).astype(BF16)
        acc[D_MAIN:, :] += _dot_tn_bf(dbab, n)
        dn = jnp.dot(dbab, wt_vmem[D_MAIN:, :], preferred_element_type=F32)
        for cb, d in enumerate((dup, dzp, dxs[0], dxs[1], dxs[2], ddz_ref[...])):
            rows = slice(cb * D_POOL, (cb + 1) * D_POOL)
            dpart = d.astype(BF16)
            acc[rows, :] += _dot_tn_bf(dpart, n)
            dn = dn + jnp.dot(dpart, wt_vmem[rows, :], preferred_element_type=F32)
        gnw_ref[...] += jnp.sum(dn * xhat, axis=0, keepdims=True)
        dxh = dn * nw
        gx_ref[...] = dh_ref[...] + r * (dxh - xhat * jnp.mean(dxh * xhat, axis=-1, keepdims=True))

        @pl.when(j == nstep - 1)
        def _():
            def out(d):
                return pltpu.make_async_copy(blk.at[d % 2], p_hbm.at[d], osem.at[d % 2])
            for d in range(N_DEV):
                if d >= 2:
                    out(d - 2).wait()
                blk[d % 2] = acc[W_IN_SHARD * d:W_IN_SHARD * (d + 1), :]
                out(d).start()
            out(N_DEV - 2).wait()
            out(N_DEV - 1).wait()

    def col(c):
        return pl.BlockSpec((tm, D_POOL), lambda j: (nstep - 1 - j, c))

    def halo(c):
        return pl.BlockSpec((HALO, D_POOL), lambda j: (jnp.maximum((nstep - 1 - j) * per - 1, 0), c))

    rev = lambda j: (nstep - 1 - j, 0)
    part = pl.BlockSpec((tm, D_POOL), rev)
    lanes = pl.BlockSpec((tm, 128), rev)
    full = pl.BlockSpec((tm, D_MODEL), rev)
    row = pl.BlockSpec((1, D_MODEL), lambda j: (0, 0))
    lrow = pl.BlockSpec((1, 128), lambda j: (0, 0))
    pw = pl.BlockSpec((4, POOL_GROUP, POOL_GROUP), lambda j: (0, 0, 0))
    psp = pl.BlockSpec((1, D_POOL), lambda j: (0, 0))
    cw = pl.BlockSpec((CONV_WIDTH, 3 * D_DN), lambda j: (0, 0))
    return pl.pallas_call(
        body, name="back", grid=(nstep,),
        out_shape=(jax.ShapeDtypeStruct((s, D_MODEL), F32),
                   jax.ShapeDtypeStruct((N_DEV, W_IN_SHARD, D_MODEL), F32), jax.ShapeDtypeStruct((1, D_MODEL), F32),
                   jax.ShapeDtypeStruct((4, POOL_GROUP, POOL_GROUP), F32), jax.ShapeDtypeStruct((1, D_POOL), F32),
                   jax.ShapeDtypeStruct((CONV_WIDTH, 3 * D_DN), F32),
                   jax.ShapeDtypeStruct((1, 128), F32), jax.ShapeDtypeStruct((1, 128), F32)),
        in_specs=[col(0), col(1), col(2), col(3), col(4), halo(0), halo(2), halo(3), halo(4), lanes,
                  part, part, part, part, lanes, part, full, full,
                  row, pw, psp, cw, lrow, lrow, pl.BlockSpec(memory_space=pl.ANY)],
        out_specs=(full, pl.BlockSpec(memory_space=pl.ANY), row, pw, psp, cw, lrow, lrow),
        scratch_shapes=[pltpu.VMEM((D_IN_PAD, D_MODEL), BF16), pltpu.VMEM((D_IN_PAD, D_MODEL), F32),
                        pltpu.VMEM((2, W_IN_SHARD, D_MODEL), F32),
                        pltpu.VMEM((HALO, 3 * D_DN), F32), pltpu.VMEM((HALO, D_POOL), F32),
                        pltpu.SemaphoreType.DMA, pltpu.SemaphoreType.DMA((2,))],
        compiler_params=_cp(("arbitrary",)),
    )(proj_main, proj_main, proj_main, proj_main, proj_main, proj_main, proj_main, proj_main, proj_main, proj_ba,
      dyp, dqn, dkn, dvv, dgb, ddz, x2, dh, norm_w, pool_w, pool_scale, conv_full, alog_lane, dtb_lane, wt_full)


def _adamw_math(w, g, m, v):
    m = ADAM_B1 * m + (1.0 - ADAM_B1) * g
    v = ADAM_B2 * v + (1.0 - ADAM_B2) * (g * g)
    m_hat = m / (1.0 - ADAM_B1 ** ADAM_STEP)
    v_hat = v / (1.0 - ADAM_B2 ** ADAM_STEP)
    delta = -ADAM_LR * (m_hat / (jnp.sqrt(v_hat) + ADAM_EPS) + ADAM_WD * w)
    return delta, m, v


def _adamw(tiled, whole, gath_a, gath_b, pool, rows, nstep):
    kw, nrow = len(whole), len(rows)
    ins = list(tiled) + [a for p in whole for a in p] + [gath_a, gath_b] + list(pool) + [a for wmv in rows for a in wmv]

    def body(*refs):
        in_refs, outs = refs[:len(ins)], refs[len(ins):]

        def update(w, g, m, v, o):
            dl, nm, nv = _adamw_math(w, g, m, v)
            for ref, val in zip(o, (g, dl, nm, nv)):
                ref[...] = val

        update(*(r[...] for r in in_refs[:4]), outs[:4])

        @pl.when(pl.program_id(0) == 0)
        def _():
            for p in range(1, kw + 1):
                update(*(r[...] for r in in_refs[4 * p:4 * p + 4]), outs[4 * p:4 * p + 4])
            ga_ref, gb_ref = in_refs[4 * kw + 4:4 * kw + 6]
            rep_in, rep_out = in_refs[4 * kw + 6:], outs[4 * kw + 4:]

            def total(ref):
                g = ref[0]
                for d in range(1, N_DEV):
                    g = g + ref[d]
                return g

            gs = [total(ga_ref)]
            gb = total(gb_ref)
            gs += [gb[r:r + 1, :rep_in[3 * (r + 1)].shape[1]] for r in range(nrow)]
            for r, g in enumerate(gs):
                w, m, v = (ref[...] for ref in rep_in[3 * r:3 * r + 3])
                update(w, g, m, v, rep_out[4 * r:4 * r + 4])
            rep_out[4 * (nrow + 1)][...] = gb[nrow:nrow + 1, 0:1]

    tile = pl.BlockSpec((tiled[0].shape[0] // nstep, tiled[0].shape[1]), lambda i: (i, 0))

    def full(a):
        return pl.BlockSpec(a.shape, lambda i, nd=a.ndim: (0,) * nd)

    firsts = [p[0] for p in whole] + [pool[0]] + [wmv[0] for wmv in rows]
    out_shape = ([jax.ShapeDtypeStruct(tiled[0].shape, F32)] * 4
                 + [jax.ShapeDtypeStruct(w.shape, F32) for w in firsts for _ in range(4)]
                 + [jax.ShapeDtypeStruct((1, 1), F32)])
    res = pl.pallas_call(
        body, name="adamw", grid=(nstep,),
        in_specs=[tile] * 4 + [full(a) for a in ins[4:]],
        out_specs=tuple([tile] * 4 + [full(o) for o in out_shape[4:]]),
        out_shape=tuple(out_shape),
        compiler_params=_cp(("arbitrary",)),
    )(*ins)
    groups = [tuple(res[4 * k:4 * k + 4]) for k in range(kw + nrow + 2)]
    return groups[0], groups[1:kw + 1], groups[kw + 1:], res[-1]


_ROW_ORDER = ("norm_w", "final_norm_w", "pool_scale", "dn_norm_w", "a_log", "dt_bias")


def _pack_rows(vectors):
    out = [jnp.pad(v.reshape(-1), (0, D_MODEL - v.size)) for v in vectors]
    out += [jnp.zeros((D_MODEL,), F32)] * (8 - len(out))
    return jnp.stack(out, axis=0)


def _lane_row(vec4, start):
    return jnp.pad(vec4.reshape(-1), (start, 128 - start - vec4.size)).reshape(1, 128)


def kernel(x, norm_w, w_in, pool_w, pool_scale, conv_w, a_log, dt_bias, dn_norm_w, w_out, final_norm_w, loss_target, m_norm_w, m_w_in, m_pool_w, m_pool_scale, m_conv_w, m_a_log, m_dt_bias, m_dn_norm_w, m_w_out, m_final_norm_w, v_norm_w, v_w_in, v_pool_w, v_pool_scale, v_conv_w, v_a_log, v_dt_bias, v_dn_norm_w, v_w_out, v_final_norm_w):
    s = x.shape[1]
    tm = min(512, s)
    tmb = min(256, s)
    x2 = x[0]
    tgt = loss_target[0]
    def to_flat(a):
        return a[0].reshape(FLAT_ROWS, 128, W_IN_SHARD).transpose(2, 0, 1).reshape(W_IN_SHARD * FLAT_ROWS, 128)

    def from_flat(f):
        return f.reshape(W_IN_SHARD, FLAT_ROWS, 128).transpose(1, 2, 0).reshape(1, D_MODEL, W_IN_SHARD)

    wf, m_wf, v_wf = to_flat(w_in), to_flat(m_w_in), to_flat(v_w_in)

    g_in, g_conv, n_all = _gather_weights(wf, conv_w[0], x2, norm_w, tm)
    conv_full = g_conv.transpose(1, 0, 2).reshape(CONV_WIDTH, 3 * D_DN)
    alog_lane = _lane_row(a_log, DN_HEADS)
    dtb_lane = _lane_row(dt_bias, DN_HEADS)
    fnw = final_norm_w.reshape(1, D_MODEL)

    proj_main, proj_ba, y_pool, qn, kn, vv, gb, g_out, wt_full = _front(
        n_all, g_in, pool_w[0], pool_scale, conv_full, alog_lane, dtb_lane, w_out[0], tm)
    w_out_full = g_out.reshape(D_MODEL, D_MODEL)
    o_dn, states, dn_pre, dn_uw = _dn_scan_fwd(qn, kn, vv, gb, DN_CHUNKS_PER_STEP)

    dh, dyp, do_dn, ddz, g_wout, g_fnw, loss_part, g_dnw = _out_proj_loss(
        y_pool, o_dn, proj_main, dn_norm_w, x2, tgt, w_out_full, fnw, tm)
    p_out = g_wout.reshape(N_DEV, D_MODEL // N_DEV, D_MODEL)
    dqn, dkn, dvv, dgb, gr_out = _dn_scan_bwd(qn, kn, vv, gb, states, dn_pre, dn_uw, do_dn, p_out, DN_CHUNKS_PER_STEP)
    grad_x2, p_in, g_nw, g_pw, g_ps, g_conv_full, g_al, g_db = _back(
        proj_main, proj_ba, dyp, dqn, dkn, dvv, dgb, ddz, x2, dh, norm_w, pool_w[0], pool_scale, conv_full,
        alog_lane, dtb_lane, wt_full, tmb)

    p_conv = g_conv_full.reshape(CONV_WIDTH, N_DEV, 3 * D_DN // N_DEV).transpose(1, 0, 2)
    pack_a = g_pw.reshape(4 * POOL_GROUP, POOL_GROUP)
    pack_b = _pack_rows([g_nw, g_fnw, g_ps, g_dnw, g_al[0, DN_HEADS:2 * DN_HEADS], g_db[0, DN_HEADS:2 * DN_HEADS],
                         loss_part[0, :1]])
    gr_in, gr_conv, gath_a, gath_b = _reduce_grads((p_in, p_conv), (wf, conv_w[0]), pack_a, pack_b)

    flat = lambda a: a.reshape(4 * POOL_GROUP, POOL_GROUP)
    row = lambda a: a.reshape(1, -1)
    vecs = {"norm_w": (norm_w, m_norm_w, v_norm_w), "final_norm_w": (final_norm_w, m_final_norm_w, v_final_norm_w),
            "pool_scale": (pool_scale, m_pool_scale, v_pool_scale), "dn_norm_w": (dn_norm_w, m_dn_norm_w, v_dn_norm_w),
            "a_log": (a_log, m_a_log, v_a_log), "dt_bias": (dt_bias, m_dt_bias, v_dt_bias)}
    r_in, (r_out, r_conv), res, loss = _adamw(
        (wf, gr_in, m_wf, v_wf),
        [(w_out[0], gr_out, m_w_out[0], v_w_out[0]), (conv_w[0], gr_conv, m_conv_w[0], v_conv_w[0])],
        gath_a, gath_b, (flat(pool_w), flat(m_pool_w), flat(v_pool_w)),
        [tuple(row(a) for a in vecs[nm]) for nm in _ROW_ORDER], ADAMW_STEPS)
    r_pool = res[0]
    r_vec = dict(zip(_ROW_ORDER, res[1:]))

    def group(k):
        vec = lambda nm: r_vec[nm][k].reshape(vecs[nm][0].shape)
        return (vec("norm_w"), from_flat(r_in[k]), r_pool[k].reshape(pool_w.shape), vec("pool_scale"), r_conv[k][None],
                vec("a_log"), vec("dt_bias"), vec("dn_norm_w"), r_out[k][None], vec("final_norm_w"))

    return (loss[0, 0], grad_x2[None], *group(0), *group(1), *group(2), *group(3))
```

```python
import functools

import jax
import jax.numpy as jnp
from jax import lax
from jax.experimental import pallas as pl
from jax.experimental.pallas import tpu as pltpu

F32 = jnp.float32
BF16 = jnp.bfloat16
HI = lax.Precision.HIGHEST
MESH = pl.DeviceIdType.MESH

D_MODEL = 1024
D_POOL = 512
D_DN = 512
POOL_WINDOWS = (2, 4, 8, 16)
POOL_GROUP = 128
DN_HEADS = 4
DN_HEAD_DIM = 128
CONV_WIDTH = 4
CHUNK = 64
NORM_EPS = 1e-6
D_IN = 3080
D_MAIN = 3072
FLAT_ROWS = D_MODEL // 128
D_IN_PAD = D_MAIN + 128
N_DEV = 8
W_IN_SHARD = D_IN // N_DEV
HALO = 16
DN_CHUNKS_PER_STEP = 8
ADAMW_STEPS = 5

ADAM_LR = 0.001
ADAM_B1 = 0.9
ADAM_B2 = 0.999
ADAM_EPS = 1e-08
ADAM_WD = 0.01
ADAM_STEP = 10

VMEM_LIMIT = 56 * 1024 * 1024
def _cp(sem=None, vmem=VMEM_LIMIT):
    kw = {"vmem_limit_bytes": vmem}
    if sem is not None:
        kw["dimension_semantics"] = sem
    return pltpu.CompilerParams(**kw)


def _dot_bf(a, b):
    return jnp.dot(a.astype(BF16), b.astype(BF16), preferred_element_type=F32)


def _dot_nt_bf(a, b):
    return lax.dot_general(a.astype(BF16), b.astype(BF16), (((1,), (1,)), ((), ())), preferred_element_type=F32)


def _dot_tn_bf(a, b):
    return lax.dot_general(a.astype(BF16), b.astype(BF16), (((0,), (0,)), ((), ())), preferred_element_type=F32)


def _mm_raw(a, b, ca, cb, prec):
    off = a.ndim - 2
    dn = (((ca + off,), (cb + off,)), ((0,), (0,)) if off else ((), ()))
    if prec == "hi":
        return lax.dot_general(a, b, dn, precision=HI, preferred_element_type=F32)
    ah, bh = a.astype(BF16), b.astype(BF16)
    out = lax.dot_general(ah, bh, dn, preferred_element_type=F32)
    if prec == "x3":
        al = (a - ah.astype(F32)).astype(BF16)
        bl = (b - bh.astype(F32)).astype(BF16)
        out = out + lax.dot_general(ah, bl, dn, preferred_element_type=F32)
        out = out + lax.dot_general(al, bh, dn, preferred_element_type=F32)
    return out


@functools.partial(jax.custom_vjp, nondiff_argnums=(2, 3, 4, 5))
def _mm(a, b, ca, cb, prec, bprec):
    return _mm_raw(a, b, ca, cb, prec)


def _mm_fwd(a, b, ca, cb, prec, bprec):
    return _mm_raw(a, b, ca, cb, prec), (a, b)


def _mm_bwd(ca, cb, prec, bprec, res, dc):
    a, b = res
    da = _mm_raw(dc, b, 1, 1 - cb, bprec) if ca == 1 else _mm_raw(b, dc, 1 - cb, 1, bprec)
    db = _mm_raw(a, dc, 1 - ca, 0, bprec) if cb == 0 else _mm_raw(dc, a, 0, 1 - ca, bprec)
    return da, db


_mm.defvjp(_mm_fwd, _mm_bwd)


@functools.partial(jax.custom_vjp, nondiff_argnums=(1, 2))
def _tri_inv(a, prec, bprec):
    n = a.shape[-1]
    ii = lax.broadcasted_iota(jnp.int32, (n, n), 0)
    jj = lax.broadcasted_iota(jnp.int32, (n, n), 1)
    p = (ii == jj).astype(F32) - a
    b = _mm_raw(a, a, 1, 0, prec)
    for _ in range(4):
        pb = _mm_raw(jnp.concatenate([p, b], axis=-2), b, 1, 0, prec)
        p = p + pb[..., :n, :]
        b = pb[..., n:, :]
    return p + _mm_raw(p, b, 1, 0, prec)


def _tri_inv_fwd(a, prec, bprec):
    t = _tri_inv(a, prec, bprec)
    return t, t


def _tri_inv_bwd(prec, bprec, t, dt):
    return (-_mm_raw(_mm_raw(t, dt, 0, 0, bprec), t, 1, 1, bprec),)


_tri_inv.defvjp(_tri_inv_fwd, _tri_inv_bwd)

@functools.partial(jax.custom_vjp, nondiff_argnums=(3, 4, 5))
def _mm_known(a, b, out, ca, cb, bprec):
    return out


def _mm_known_fwd(a, b, out, ca, cb, bprec):
    return out, (a, b)


def _mm_known_bwd(ca, cb, bprec, res, dc):
    return _mm_bwd(ca, cb, None, bprec, res, dc) + (jnp.zeros_like(dc),)


_mm_known.defvjp(_mm_known_fwd, _mm_known_bwd)


@jax.custom_vjp
def _use_known(x, known):
    return known


_use_known.defvjp(lambda x, known: (known, None), lambda _, g: (g, jnp.zeros_like(g)))


@functools.partial(jax.custom_vjp, nondiff_argnums=(2,))
def _tri_inv_known(a, t, bprec):
    return t


def _tri_inv_known_fwd(a, t, bprec):
    return t, t


def _tri_inv_known_bwd(bprec, t, dt):
    return _tri_inv_bwd(None, bprec, t, dt) + (jnp.zeros_like(dt),)


_tri_inv_known.defvjp(_tri_inv_known_fwd, _tri_inv_known_bwd)

_DN_PREC = {"akq": ("bf16", "bf16"), "inv": ("bf16", "bf16"), "uw": ("bf16", "bf16"), "ws": ("bf16", "bf16"),
            "ov": ("bf16", "bf16"), "st": ("bf16", "bf16")}


def _silu(x):
    return x * jax.nn.sigmoid(x)


def _softplus(x):
    pos = x > 0.0
    return jnp.where(pos, x, 0.0) + jnp.log1p(jnp.exp(jnp.where(pos, -x, x)))


def _mesh_pos():
    return lax.axis_index("x"), lax.axis_index("y"), lax.axis_index("c")


def _dev_index(x, y, c):
    return 4 * x + 2 * y + c


def _relay_order():
    x, y, c = _mesh_pos()
    n1 = (x + (1 - c) * (1 - 2 * x), y + c * (1 - 2 * y))
    n2 = (x + c * (1 - 2 * x), y + (1 - c) * (1 - 2 * y))
    return (x, y, c), (x, y, 1 - c), n1, n2, (1 - x, 1 - y)


def _all_gather_blocks(outs, send_sems, recv_sems, meanwhile=None, own=None):
    me, sibling, n1, n2, diag = _relay_order()
    c = me[2]

    def copy(a, k, block, to, src=None):
        rows = outs[a].at[_dev_index(*block)]
        return pltpu.make_async_remote_copy(src_ref=rows if src is None else src, dst_ref=rows,
                                            send_sem=send_sems.at[a, k], recv_sem=recv_sems.at[a, k],
                                            device_id=to, device_id_type=MESH)

    n = len(outs)
    started = []

    def start(cp):
        cp.start()
        started.append(cp)

    for a in range(n):
        mine = None if own is None else own[a]
        start(copy(a, 1, me, (*n1, c), mine))
        start(copy(a, 2, me, (*n2, c), mine))
        start(copy(a, 0, me, sibling, mine))
    if meanwhile is not None:
        meanwhile()
    for a in range(n):
        copy(a, 1, (*n1, c), me).wait_recv()
        start(copy(a, 3, (*n1, c), (*n2, c)))
        start(copy(a, 4, (*n1, c), sibling))
    for a in range(n):
        copy(a, 2, (*n2, c), me).wait_recv()
        start(copy(a, 5, (*n2, c), sibling))
        copy(a, 3, (*diag, c), me).wait_recv()
        start(copy(a, 6, (*diag, c), sibling))
    for a in range(n):
        copy(a, 0, sibling, me).wait_recv()
        copy(a, 4, (*n2, 1 - c), me).wait_recv()
        copy(a, 5, (*n1, 1 - c), me).wait_recv()
        copy(a, 6, (*diag, 1 - c), me).wait_recv()
    for cp in started:
        cp.wait_send()


def _peer_relations():
    x, y, c = _mesh_pos()
    flips = [(fx, fy, fc) for fx in (0, 1) for fy in (0, 1) for fc in (0, 1)][1:]
    peers = [(1 - x if fx else x, 1 - y if fy else y, 1 - c if fc else c) for fx, fy, fc in flips]
    return (x, y, c), peers


def _direct_gather_start(out_ref, send_sems, recv_sems):
    me, peers = _peer_relations()
    rows = out_ref.at[_dev_index(*me)]
    for k, peer in enumerate(peers):
        pltpu.make_async_remote_copy(src_ref=rows, dst_ref=rows, send_sem=send_sems.at[k], recv_sem=recv_sems.at[k],
                                     device_id=peer, device_id_type=MESH).start()


def _direct_gather_wait(out_ref, send_sems, recv_sems):
    me, peers = _peer_relations()
    for k, peer in enumerate(peers):
        rows = out_ref.at[_dev_index(*peer)]
        cp = pltpu.make_async_remote_copy(src_ref=rows, dst_ref=rows, send_sem=send_sems.at[k],
                                          recv_sem=recv_sems.at[k], device_id=peer, device_id_type=MESH)
        cp.wait_recv()
        cp.wait_send()


def _direct_scatter_start(send_ref, recv_ref, send_sems, recv_sems):
    me, peers = _peer_relations()
    for k, peer in enumerate(peers):
        pltpu.make_async_remote_copy(src_ref=send_ref.at[_dev_index(*peer)], dst_ref=recv_ref.at[k],
                                     send_sem=send_sems.at[k], recv_sem=recv_sems.at[k],
                                     device_id=peer, device_id_type=MESH).start()


def _direct_scatter_wait(send_ref, recv_ref, send_sems, recv_sems):
    me, peers = _peer_relations()
    for k, peer in enumerate(peers):
        cp = pltpu.make_async_remote_copy(src_ref=send_ref.at[_dev_index(*peer)], dst_ref=recv_ref.at[k],
                                          send_sem=send_sems.at[k], recv_sem=recv_sems.at[k],
                                          device_id=peer, device_id_type=MESH)
        cp.wait_recv()
        cp.wait_send()


def _gather_weights(w_in_flat, conv_blk, x2, norm_w, tn):
    nt = x2.shape[0] // tn

    def body(win_ref, conv_ref, x_hbm, nw_ref, gin_hbm, gconv_ref, n_hbm, own_ref, xbuf, nbuf, send_sems, recv_sems,
             xsem, nsem, own_sem):
        x, y, c = _mesh_pos()
        me = _dev_index(x, y, c)
        for j in range(FLAT_ROWS):
            own_ref[:, 128 * j:128 * (j + 1)] = win_ref[pl.ds(j, W_IN_SHARD, stride=FLAT_ROWS), :].astype(BF16)
        keep_own = pltpu.make_async_copy(own_ref, gin_hbm.at[me], own_sem.at[0])
        keep_own.start()
        gconv_ref[me] = conv_ref[...]

        def norm_x():
            def load(t):
                return pltpu.make_async_copy(x_hbm.at[pl.ds(t * tn, tn), :], xbuf.at[t % 2], xsem.at[t % 2])

            def store(t):
                return pltpu.make_async_copy(nbuf.at[t % 2], n_hbm.at[pl.ds(t * tn, tn), :], nsem.at[t % 2])

            load(0).start()
            for t in range(nt):
                if t + 1 < nt:
                    load(t + 1).start()
                load(t).wait()
                if t >= 2:
                    store(t - 2).wait()
                xhat, _ = _rms_hat(xbuf[t % 2])
                nbuf[t % 2] = (xhat * nw_ref[...]).astype(BF16)
                store(t).start()
            for t in range(max(nt - 2, 0), nt):
                store(t).wait()

        _all_gather_blocks((gin_hbm, gconv_ref), send_sems, recv_sems, meanwhile=norm_x, own=(own_ref, None))
        keep_own.wait()

    vm = pl.BlockSpec(memory_space=pltpu.VMEM)
    hbm = pl.BlockSpec(memory_space=pl.ANY)
    return pl.pallas_call(
        body, name="gather_weights",
        out_shape=(jax.ShapeDtypeStruct((N_DEV, W_IN_SHARD, D_MODEL), BF16),
                   jax.ShapeDtypeStruct((N_DEV,) + conv_blk.shape, F32),
                   jax.ShapeDtypeStruct(x2.shape, BF16)),
        in_specs=[vm, vm, hbm, vm], out_specs=(hbm, vm, hbm),
        scratch_shapes=[pltpu.VMEM((W_IN_SHARD, D_MODEL), BF16),
                        pltpu.VMEM((2, tn, D_MODEL), F32), pltpu.VMEM((2, tn, D_MODEL), BF16),
                        pltpu.SemaphoreType.DMA((2, 7)), pltpu.SemaphoreType.DMA((2, 7)),
                        pltpu.SemaphoreType.DMA((2,)), pltpu.SemaphoreType.DMA((2,)), pltpu.SemaphoreType.DMA((1,))],
        compiler_params=_cp(),
    )(w_in_flat, conv_blk, x2, norm_w)


def _reduce_grads(big, like, pack_a, pack_b):
    nb = len(big)

    def body(*refs):
        srcs, (pa_ref, pb_ref), outs, (ga_ref, gb_ref) = (
            refs[:nb], refs[nb:nb + 2], refs[nb + 2:2 * nb + 2], refs[2 * nb + 2:2 * nb + 4])
        scr = refs[2 * nb + 4:]
        r1s, r2s, sbs, sts, sums = (scr[k * nb:(k + 1) * nb] for k in range(5))
        s1_send, s1_recv, s2_send, s2_recv, ag_send, ag_recv, st_sem = scr[5 * nb:]
        x, y, c = _mesh_pos()
        me = (x, y, c)
        sibling = (x, y, 1 - c)

        ga_ref[_dev_index(*me)] = pa_ref[...]
        gb_ref[_dev_index(*me)] = pb_ref[...]

        _, _, n1, n2, diag = _relay_order()
        order = ((diag, 3), (n1, 1 + c), (n2, 2 - c), ((x, y), 0))

        def p1(a, i, to):
            chip, slot = order[i]
            return pltpu.make_async_remote_copy(
                src_ref=srcs[a].at[_dev_index(*chip, 1 - c)], dst_ref=r1s[a].at[slot],
                send_sem=s1_send.at[a, slot], recv_sem=s1_recv.at[a, slot], device_id=to, device_id_type=MESH)

        def p1_landed(a, slot):
            return pltpu.make_async_remote_copy(
                src_ref=r1s[a].at[slot], dst_ref=r1s[a].at[slot], send_sem=s1_send.at[a, slot],
                recv_sem=s1_recv.at[a, slot], device_id=me, device_id_type=MESH)

        def p2(a, k, to):
            return pltpu.make_async_remote_copy(
                src_ref=sbs[a].at[k], dst_ref=r2s[a].at[k],
                send_sem=s2_send.at[a, k], recv_sem=s2_recv.at[a, k], device_id=to, device_id_type=MESH)

        def stage(a, i):
            return pltpu.make_async_copy(srcs[a].at[_dev_index(*order[i][0], c)], sts[a].at[i % 2], st_sem.at[a, i % 2])

        sends = [p1(a, i, sibling) for a in range(nb) for i in (0, 2, 1, 3)]
        for cp in sends:
            cp.start()

        def step_1():
            for a in range(nb):
                stage(a, 0).start()
                for i, (_, slot) in enumerate(order):
                    if i + 1 < len(order):
                        stage(a, i + 1).start()
                    stage(a, i).wait()
                    p1_landed(a, slot).wait_recv()
                    chip_sum = r1s[a][slot].astype(F32) + sts[a][i % 2].astype(F32)
                    if i < 2:
                        sbs[a][i] = chip_sum.astype(BF16)
                        sends.append(p2(a, i, (*n1, c)))
                        sends[-1].start()
                    else:
                        sums[a][i - 2] = chip_sum

        _all_gather_blocks((ga_ref, gb_ref), ag_send, ag_recv, meanwhile=step_1)
        for a in range(nb):
            p2(a, 0, me).wait_recv()
            sbs[a][2] = (sums[a][0] + r2s[a][0].astype(F32)).astype(BF16)
            sends.append(p2(a, 2, (*n2, c)))
            sends[-1].start()
        for a in range(nb):
            p2(a, 1, me).wait_recv()
            p2(a, 2, me).wait_recv()
            total = (sums[a][1] + r2s[a][1].astype(F32)) + r2s[a][2].astype(F32)
            if outs[a].shape == total.shape:
                outs[a][...] = total
            else:
                sums[a][1] = total
                for j in range(FLAT_ROWS):
                    outs[a][pl.ds(j, W_IN_SHARD, stride=FLAT_ROWS), :] = sums[a][1, :, 128 * j:128 * (j + 1)]
        for cp in sends:
            cp.wait_send()

    vm = pl.BlockSpec(memory_space=pltpu.VMEM)
    hbm = pl.BlockSpec(memory_space=pl.ANY)
    blk = [p.shape[1:] for p in big]
    scratch = ([pltpu.VMEM((4,) + p.shape[1:], p.dtype) for p in big] + [pltpu.VMEM((3,) + b, BF16) for b in blk]
               + [pltpu.VMEM((3,) + b, BF16) for b in blk] + [pltpu.VMEM((2,) + p.shape[1:], p.dtype) for p in big]
               + [pltpu.VMEM((2,) + b, F32) for b in blk]
               + [pltpu.SemaphoreType.DMA((nb, 4)), pltpu.SemaphoreType.DMA((nb, 4)),
                  pltpu.SemaphoreType.DMA((nb, 3)), pltpu.SemaphoreType.DMA((nb, 3)),
                  pltpu.SemaphoreType.DMA((2, 7)), pltpu.SemaphoreType.DMA((2, 7)),
                  pltpu.SemaphoreType.DMA((nb, 2))])
    return pl.pallas_call(
        body, name="reduce_grads",
        out_shape=tuple(jax.ShapeDtypeStruct(w.shape, F32) for w in like)
        + (jax.ShapeDtypeStruct((N_DEV,) + pack_a.shape, F32), jax.ShapeDtypeStruct((N_DEV,) + pack_b.shape, F32)),
        in_specs=[hbm] * nb + [vm, vm], out_specs=(vm,) * (nb + 2),
        scratch_shapes=scratch,
        compiler_params=_cp(),
    )(*big, pack_a, pack_b)


def _rms_hat(xf):
    r = lax.rsqrt(jnp.mean(xf * xf, axis=-1, keepdims=True) + NORM_EPS)
    return xf * r, r


def _shift_up(cur, next_head, s):
    ext = jnp.concatenate([cur, next_head], axis=0)
    n = ext.shape[0]
    return pltpu.roll(ext, n - s, 0)[:cur.shape[0], :]


def _pool_counts(i, tp, w):
    t = i * tp + lax.broadcasted_iota(jnp.int32, (tp, 1), 0)
    return jnp.minimum(t + 1, w).astype(F32)


def _pool_mix(u, u_prev_tail, i, tp):
    win = jnp.concatenate([u_prev_tail, u], axis=0)
    mixes = []
    for gi, w in enumerate(POOL_WINDOWS):
        win = win + pltpu.roll(win, w // 2, 0)
        cols = slice(gi * POOL_GROUP, (gi + 1) * POOL_GROUP)
        mixes.append(win[HALO:, :POOL_GROUP] / _pool_counts(i, tp, w) - u[:, cols])
        if gi + 1 < len(POOL_WINDOWS):
            win = win[:, POOL_GROUP:]
    return mixes


def _prev_tail(ref, i):
    return jnp.where(i > 0, ref[...], 0.0)


def _conv_taps(cur, prev_tail):
    ext = jnp.concatenate([prev_tail, cur], axis=0)
    return [cur] + [pltpu.roll(ext, sft, 0)[HALO:, :] for sft in range(1, CONV_WIDTH)]


def _conv_of_taps(taps, w4):
    y = taps[0] * w4[CONV_WIDTH - 1:CONV_WIDTH, :]
    for sft in range(1, CONV_WIDTH):
        y = y + taps[sft] * w4[CONV_WIDTH - 1 - sft:CONV_WIDTH - sft, :]
    return y


def _conv_fwd(cur, prev_tail, w4):
    ext = jnp.concatenate([prev_tail, cur], axis=0)
    y = ext * w4[CONV_WIDTH - 1:CONV_WIDTH, :]
    for sft in range(1, CONV_WIDTH):
        y = y + pltpu.roll(ext, sft, 0) * w4[CONV_WIDTH - 1 - sft:CONV_WIDTH - sft, :]
    return y[HALO:, :]


def _l2n_heads(t):
    parts = []
    for h in range(DN_HEADS):
        th = t[:, h * DN_HEAD_DIM:(h + 1) * DN_HEAD_DIM]
        parts.append(th * lax.rsqrt(jnp.sum(th * th, axis=-1, keepdims=True) + NORM_EPS))
    return jnp.concatenate(parts, axis=1)


def _post_conv(yq, yk, yv):
    return _l2n_heads(_silu(yq)), _l2n_heads(_silu(yk)), _silu(yv)


def _gates(ba, alog_lane, dtb_lane):
    lane = lax.broadcasted_iota(jnp.int32, ba.shape, 1)
    beta = jax.nn.sigmoid(ba)
    g = -jnp.exp(alog_lane) * _softplus(ba + dtb_lane)
    return jnp.where(lane < DN_HEADS, beta, jnp.where(lane < 2 * DN_HEADS, g, 0.0))


def _front(n_all, g_in, pool_w, pool_scale, conv_full, alog_lane, dtb_lane, w_out_blk, tm):
    s = n_all.shape[0]

    def body(n_ref, pw_ref, ps_ref, cw_ref, al_ref, db_ref, wo_ref, g_hbm,
             pm_ref, pb_ref, yp_ref, qn_ref, kn_ref, vv_ref, gb_ref, gwo_hbm, wt_hbm,
             g_vmem, wt_vmem, tail_u, tail_qkv, gwo_ref, sem, g_sems, wo_send, wo_recv):
        i = pl.program_id(0)

        @pl.when(i == 0)
        def _():
            gwo_ref[_dev_index(*_mesh_pos())] = wo_ref[...].astype(BF16)
            _direct_gather_start(gwo_ref, wo_send, wo_recv)
            loads = [pltpu.make_async_copy(g_hbm.at[d], g_vmem.at[d], g_sems.at[d]) for d in range(N_DEV)]
            for cp in loads:
                cp.start()
            wt_vmem[D_MAIN:, :] = jnp.zeros((D_IN_PAD - D_MAIN, D_MODEL), BF16)
            tail_u[...] = jnp.zeros_like(tail_u)
            tail_qkv[...] = jnp.zeros_like(tail_qkv)
            for d, cp in enumerate(loads):
                cp.wait()
                wt_vmem[W_IN_SHARD * d:W_IN_SHARD * (d + 1), :] = g_vmem[d]
            pltpu.make_async_copy(wt_vmem, wt_hbm, sem).start()
        n = n_ref[...]
        pm_ref[...] = _dot_nt_bf(n, wt_vmem[:D_MAIN, :])
        pb = _dot_nt_bf(n, wt_vmem[D_MAIN:, :])
        pb_ref[...] = pb
        u = pm_ref[:, :D_POOL]
        mixes = _pool_mix(u, tail_u[...], i, tm)
        tail_u[...] = u[tm - HALO:, :]
        gate = ps_ref[...] * _silu(pm_ref[:, D_POOL:2 * D_POOL])
        for gi in range(4):
            cols = slice(gi * POOL_GROUP, (gi + 1) * POOL_GROUP)
            yp_ref[:, cols] = _dot_bf(mixes[gi], pw_ref[gi]) * gate[:, cols]
        ys = []
        for c in range(3):
            cols = slice(c * D_DN, (c + 1) * D_DN)
            cur = pm_ref[:, 2 * D_POOL + c * D_DN:2 * D_POOL + (c + 1) * D_DN]
            ys.append(_conv_fwd(cur, tail_qkv[:, cols], cw_ref[:, cols]))
            tail_qkv[:, cols] = cur[tm - HALO:, :]
        qn, kn, vv = _post_conv(*ys)
        qn_ref[...] = qn
        kn_ref[...] = kn
        vv_ref[...] = vv
        gb_ref[...] = _gates(pb, al_ref[...], db_ref[...])

        @pl.when(i == s // tm - 1)
        def _():
            pltpu.make_async_copy(wt_vmem, wt_hbm, sem).wait()
            _direct_gather_wait(gwo_ref, wo_send, wo_recv)
            out = pltpu.make_async_copy(gwo_ref, gwo_hbm, sem)
            out.start()
            out.wait()

    tile = pl.BlockSpec((tm, D_DN), lambda i: (i, 0))
    lanes = pl.BlockSpec((tm, 128), lambda i: (i, 0))
    row = pl.BlockSpec((1, 128), lambda i: (0, 0))
    return pl.pallas_call(
        body, name="front", grid=(s // tm,),
        out_shape=(jax.ShapeDtypeStruct((s, D_MAIN), F32), jax.ShapeDtypeStruct((s, 128), F32),
                   jax.ShapeDtypeStruct((s, D_POOL), F32), jax.ShapeDtypeStruct((s, D_DN), F32),
                   jax.ShapeDtypeStruct((s, D_DN), F32), jax.ShapeDtypeStruct((s, D_DN), F32),
                   jax.ShapeDtypeStruct((s, 128), F32), jax.ShapeDtypeStruct((N_DEV,) + w_out_blk.shape, BF16),
                   jax.ShapeDtypeStruct((D_IN_PAD, D_MODEL), BF16)),
        in_specs=[pl.BlockSpec((tm, D_MODEL), lambda i: (i, 0)),
                  pl.BlockSpec((4, POOL_GROUP, POOL_GROUP), lambda i: (0, 0, 0)),
                  pl.BlockSpec((1, D_POOL), lambda i: (0, 0)),
                  pl.BlockSpec((CONV_WIDTH, 3 * D_DN), lambda i: (0, 0)), row, row,
                  pl.BlockSpec(memory_space=pltpu.VMEM), pl.BlockSpec(memory_space=pl.ANY)],
        out_specs=(pl.BlockSpec((tm, D_MAIN), lambda i: (i, 0)), lanes, tile, tile, tile, tile, lanes,
                   pl.BlockSpec(memory_space=pl.ANY), pl.BlockSpec(memory_space=pl.ANY)),
        scratch_shapes=[pltpu.VMEM((N_DEV, W_IN_SHARD, D_MODEL), BF16), pltpu.VMEM((D_IN_PAD, D_MODEL), BF16),
                        pltpu.VMEM((HALO, D_POOL), F32), pltpu.VMEM((HALO, 3 * D_DN), F32),
                        pltpu.VMEM((N_DEV,) + w_out_blk.shape, BF16),
                        pltpu.SemaphoreType.DMA, pltpu.SemaphoreType.DMA((N_DEV,)),
                        pltpu.SemaphoreType.DMA((7,)), pltpu.SemaphoreType.DMA((7,))],
        compiler_params=_cp(("arbitrary",)),
    )(n_all, pool_w, pool_scale, conv_full, alog_lane, dtb_lane, w_out_blk, g_in)


def _dn_block(q, k, v, gcol, bcol, state, known=None):
    nb, n, d = q.shape
    ii = lax.broadcasted_iota(jnp.int32, (n, n), 0)
    jj = lax.broadcasted_iota(jnp.int32, (n, n), 1)
    lower = ii >= jj
    eye = (ii == jj).astype(F32)
    g_row = jnp.sum(eye * gcol, axis=1, keepdims=True)
    gc_col = jnp.sum(jnp.where(lower, g_row, 0.0), axis=2, keepdims=True)
    gc_row = jnp.sum(eye * gc_col, axis=1, keepdims=True)
    decay = jnp.where(lower, jnp.exp(jnp.where(lower, gc_col - gc_row, 0.0)), 0.0)
    kb = k * bcol
    vb = v * bcol
    qs = q * (DN_HEAD_DIM ** -0.5)
    egc = jnp.exp(gc_col)
    kq = jnp.concatenate([kb, qs], axis=1)
    vk = jnp.concatenate([vb, kb * egc], axis=2)
    if known is None:
        akq = _mm(kq, k, 1, 1, *_DN_PREC["akq"])
    else:
        akq = _mm_known(kq, k, known[0][:, :, :n].astype(F32), 1, 1, _DN_PREC["akq"][1])
    a = jnp.where(ii > jj, akq[:, :n] * decay, 0.0)
    qk = akq[:, n:] * decay
    if known is None:
        t = _tri_inv(a, *_DN_PREC["inv"])
        uw = _mm(t, vk, 1, 0, *_DN_PREC["uw"])
    else:
        t = _tri_inv_known(a, known[0][:, :n, n:].astype(F32), _DN_PREC["inv"][1])
        uw = _mm_known(t, vk, known[1], 1, 0, _DN_PREC["uw"][1])
    pre = jnp.concatenate([akq, jnp.concatenate([t, jnp.zeros_like(t)], axis=1)], axis=2)
    wq = jnp.concatenate([uw[:, :, d:], qs * egc], axis=1)
    g_last = gc_col[:, n - 1:n, :]
    k_dec = k * jnp.exp(g_last - gc_col)
    e_last = jnp.exp(g_last)
    os_, starts = [], []
    for c in range(nb // DN_HEADS):
        sl = slice(c * DN_HEADS, (c + 1) * DN_HEADS)
        if known is not None and c > 0:
            state = _use_known(state, known[2][sl])
        starts.append(state)
        ws = _mm(wq[sl], state, 1, 0, *_DN_PREC["ws"])
        v_new = uw[sl, :, :d] - ws[:, :n]
        os_.append(ws[:, n:] + _mm(qk[sl], v_new, 1, 0, *_DN_PREC["ov"]))
        state = state * e_last[sl] + _mm(k_dec[sl], v_new, 0, 0, *_DN_PREC["st"])
    return jnp.concatenate(os_, axis=0), state, (pre, uw, jnp.concatenate(starts, axis=0))


def _gated_norm(o, dz, nw):
    parts = []
    for h in range(DN_HEADS):
        oh = o[:, h * DN_HEAD_DIM:(h + 1) * DN_HEAD_DIM]
        parts.append(oh * lax.rsqrt(jnp.mean(oh * oh, axis=-1, keepdims=True) + NORM_EPS) * nw)
    return jnp.concatenate(parts, axis=1) * _silu(dz)


def _dn_block_args(gc, q_ref, k_ref, v_ref, gb_ref):
    qs, ks, vs, gs, bs = [], [], [], [], []
    for cc in range(gc):
        r = slice(cc * CHUNK, (cc + 1) * CHUNK)
        gbv = gb_ref[r, :]
        for h in range(DN_HEADS):
            cols = slice(h * DN_HEAD_DIM, (h + 1) * DN_HEAD_DIM)
            qs.append(q_ref[r, cols])
            ks.append(k_ref[r, cols])
            vs.append(v_ref[r, cols])
            gs.append(gbv[:, DN_HEADS + h:DN_HEADS + h + 1])
            bs.append(gbv[:, h:h + 1])
    return tuple(jnp.stack(t, axis=0) for t in (qs, ks, vs, gs, bs))


def _dn_scan_fwd(qn, kn, vv, gb, gc):
    s = qn.shape[0]
    nchunk = s // CHUNK
    rows = gc * CHUNK

    def body(q_ref, k_ref, v_ref, gb_ref, y_ref, ss_ref, pre_ref, uw_ref, state):
        @pl.when(pl.program_id(0) == 0)
        def _():
            state[...] = jnp.zeros_like(state)
        q, k, v, gcol, bcol = _dn_block_args(gc, q_ref, k_ref, v_ref, gb_ref)
        y, new, (pre, uw, starts) = _dn_block(q, k, v, gcol, bcol, state[...])
        state[...] = new
        ss_ref[...] = starts
        pre_ref[...] = pre
        uw_ref[...] = uw
        for cc in range(gc):
            for h in range(DN_HEADS):
                y_ref[cc * CHUNK:(cc + 1) * CHUNK, h * DN_HEAD_DIM:(h + 1) * DN_HEAD_DIM] = y[cc * DN_HEADS + h]

    tile = pl.BlockSpec((rows, D_DN), lambda i: (i, 0))
    return pl.pallas_call(
        body, name="dn_scan_fwd", grid=(nchunk // gc,),
        out_shape=(jax.ShapeDtypeStruct((s, D_DN), F32),
                   jax.ShapeDtypeStruct((nchunk * DN_HEADS, DN_HEAD_DIM, DN_HEAD_DIM), F32),
                   jax.ShapeDtypeStruct((nchunk * DN_HEADS, 2 * CHUNK, 2 * CHUNK), F32),
                   jax.ShapeDtypeStruct((nchunk * DN_HEADS, CHUNK, 2 * DN_HEAD_DIM), F32)),
        in_specs=[tile, tile, tile, pl.BlockSpec((rows, 128), lambda i: (i, 0))],
        out_specs=(tile, pl.BlockSpec((gc * DN_HEADS, DN_HEAD_DIM, DN_HEAD_DIM), lambda i: (i, 0, 0)),
                   pl.BlockSpec((gc * DN_HEADS, 2 * CHUNK, 2 * CHUNK), lambda i: (i, 0, 0)),
                   pl.BlockSpec((gc * DN_HEADS, CHUNK, 2 * DN_HEAD_DIM), lambda i: (i, 0, 0))),
        scratch_shapes=[pltpu.VMEM((DN_HEADS, DN_HEAD_DIM, DN_HEAD_DIM), F32)],
        compiler_params=_cp(("arbitrary",)),
    )(qn, kn, vv, gb)


def _out_proj_loss(y_pool, o_dn, proj_main, dn_norm_w, x2, tgt, w_out_full, fnw, tm):
    s = x2.shape[0]

    def body(yp_ref, o_ref, dz_ref, nw_ref, x_ref, t_ref, wo_ref, fw_ref,
             dh_ref, dyp_ref, do_ref, ddz_ref, gwo_ref, gfw_ref, loss_ref, dnw_ref):
        @pl.when(pl.program_id(0) == 0)
        def _():
            gwo_ref[...] = jnp.zeros_like(gwo_ref)
            gfw_ref[...] = jnp.zeros_like(gfw_ref)
            loss_ref[...] = jnp.zeros_like(loss_ref)
            dnw_ref[...] = jnp.zeros_like(dnw_ref)
        y_dn, gate_vjp = jax.vjp(_gated_norm, o_ref[...], dz_ref[...], nw_ref[...])
        y = jnp.concatenate([yp_ref[...], y_dn], axis=1).astype(BF16)
        wo = wo_ref[...]
        h = x_ref[...] + jnp.dot(y, wo, preferred_element_type=F32)
        hn, r = _rms_hat(h)
        fw = fw_ref[...]
        err = hn * fw - t_ref[...]
        loss_ref[...] += 0.5 * jnp.sum(jnp.sum(err * err, axis=-1, keepdims=True) / D_MODEL, axis=0, keepdims=True)
        dout = err / D_MODEL
        gfw_ref[...] += jnp.sum(dout * hn, axis=0, keepdims=True)
        dhn = dout * fw
        dh = r * (dhn - hn * jnp.mean(dhn * hn, axis=-1, keepdims=True))
        dh_ref[...] = dh
        dhb = dh.astype(BF16)
        dy = _dot_nt_bf(dhb, wo)
        dyp_ref[...] = dy[:, :D_POOL]
        do, ddz, dnw = gate_vjp(dy[:, D_POOL:])
        do_ref[...] = do
        ddz_ref[...] = ddz
        dnw_ref[...] += dnw
        gwo_ref[...] += _dot_tn_bf(y, dhb)

    half = pl.BlockSpec((tm, D_POOL), lambda i: (i, 0))
    full = pl.BlockSpec((tm, D_MODEL), lambda i: (i, 0))
    lrow = pl.BlockSpec((1, 128), lambda i: (0, 0))
    return pl.pallas_call(
        body, name="out_proj_loss", grid=(s // tm,),
        out_shape=(jax.ShapeDtypeStruct((s, D_MODEL), F32), jax.ShapeDtypeStruct((s, D_POOL), F32),
                   jax.ShapeDtypeStruct((s, D_DN), F32), jax.ShapeDtypeStruct((s, D_DN), F32),
                   jax.ShapeDtypeStruct((D_MODEL, D_MODEL), F32),
                   jax.ShapeDtypeStruct((1, D_MODEL), F32), jax.ShapeDtypeStruct((1, 128), F32),
                   jax.ShapeDtypeStruct((1, 128), F32)),
        in_specs=[half, half, pl.BlockSpec((tm, D_DN), lambda i: (i, 5)), lrow, full, full,
                  pl.BlockSpec((D_MODEL, D_MODEL), lambda i: (0, 0)), pl.BlockSpec((1, D_MODEL), lambda i: (0, 0))],
        out_specs=(full, half, half, half, pl.BlockSpec((D_MODEL, D_MODEL), lambda i: (0, 0)),
                   pl.BlockSpec((1, D_MODEL), lambda i: (0, 0)), lrow, lrow),
        compiler_params=_cp(("arbitrary",)),
    )(y_pool, o_dn, proj_main, dn_norm_w, x2, tgt, w_out_full, fnw)


def _dn_scan_bwd(qn, kn, vv, gb, states, pre, uw, do_dn, p_out, gc):
    s = qn.shape[0]
    nchunk = s // CHUNK
    nstep = nchunk // gc
    rows = gc * CHUNK

    def body(q_ref, k_ref, v_ref, gb_ref, ss_ref, pre_ref, uw_ref, dy_ref, po_ref,
             dq_ref, dk_ref, dv_ref, dgb_ref, gro_ref, dstate, po_send, po_recv, rs_send, rs_recv):
        @pl.when(pl.program_id(0) == 0)
        def _():
            dstate[...] = jnp.zeros_like(dstate)
            po_send[...] = po_ref[...].astype(BF16)
            _direct_scatter_start(po_send, po_recv, rs_send, rs_recv)

        @pl.when(pl.program_id(0) == nstep - 1)
        def _():
            _direct_scatter_wait(po_send, po_recv, rs_send, rs_recv)
            total = po_ref[_dev_index(*_mesh_pos())]
            for k in range(N_DEV - 1):
                total = total + po_recv[k].astype(F32)
            gro_ref[...] = total
        lane = lax.broadcasted_iota(jnp.int32, (CHUNK, 128), 1)
        q, k, v, gcol, bcol = _dn_block_args(gc, q_ref, k_ref, v_ref, gb_ref)
        dy = jnp.stack([dy_ref[cc * CHUNK:(cc + 1) * CHUNK, h * DN_HEAD_DIM:(h + 1) * DN_HEAD_DIM]
                        for cc in range(gc) for h in range(DN_HEADS)], axis=0)
        known = (pre_ref[...], uw_ref[...], ss_ref[...])
        _, vjp = jax.vjp(lambda *a: _dn_block(*a, known=known)[:2], q, k, v, gcol, bcol, ss_ref[:DN_HEADS])
        dq, dk, dv, dg, db, dst = vjp((dy, dstate[...]))
        dstate[...] = dst
        for cc in range(gc):
            r = slice(cc * CHUNK, (cc + 1) * CHUNK)
            dgb = jnp.zeros((CHUNK, 128), F32)
            for h in range(DN_HEADS):
                b = cc * DN_HEADS + h
                cols = slice(h * DN_HEAD_DIM, (h + 1) * DN_HEAD_DIM)
                for ref, val in zip((dq_ref, dk_ref, dv_ref), (dq, dk, dv)):
                    ref[r, cols] = val[b]
                dgb = dgb + jnp.where(lane == h, db[b], 0.0) + jnp.where(lane == DN_HEADS + h, dg[b], 0.0)
            dgb_ref[r, :] = dgb

    rev = lambda i: (nstep - 1 - i, 0)
    tile = pl.BlockSpec((rows, D_DN), rev)
    lanes = pl.BlockSpec((rows, 128), rev)
    return pl.pallas_call(
        body, name="dn_scan_bwd", grid=(nstep,),
        out_shape=(jax.ShapeDtypeStruct((s, D_DN), F32),) * 3
        + (jax.ShapeDtypeStruct((s, 128), F32), jax.ShapeDtypeStruct(p_out.shape[1:], F32)),
        in_specs=[tile, tile, tile, lanes,
                  pl.BlockSpec((gc * DN_HEADS, DN_HEAD_DIM, DN_HEAD_DIM), lambda i: (nstep - 1 - i, 0, 0)),
                  pl.BlockSpec((gc * DN_HEADS, 2 * CHUNK, 2 * CHUNK), lambda i: (nstep - 1 - i, 0, 0)),
                  pl.BlockSpec((gc * DN_HEADS, CHUNK, 2 * DN_HEAD_DIM), lambda i: (nstep - 1 - i, 0, 0)), tile,
                  pl.BlockSpec(memory_space=pltpu.VMEM)],
        out_specs=(tile, tile, tile, lanes, pl.BlockSpec(memory_space=pltpu.VMEM)),
        scratch_shapes=[pltpu.VMEM((DN_HEADS, DN_HEAD_DIM, DN_HEAD_DIM), F32),
                        pltpu.VMEM(p_out.shape, BF16), pltpu.VMEM((N_DEV - 1,) + p_out.shape[1:], BF16),
                        pltpu.SemaphoreType.DMA((7,)), pltpu.SemaphoreType.DMA((7,))],
        compiler_params=_cp(("arbitrary",)),
    )(qn, kn, vv, gb, states, pre, uw, do_dn, p_out)


def _back(proj_main, proj_ba, dyp, dqn, dkn, dvv, dgb, ddz, x2, dh, norm_w, pool_w, pool_scale, conv_full,
          alog_lane, dtb_lane, wt_full, tm):
    s = x2.shape[0]
    nstep = s // tm
    per = tm // HALO

    def body(u_ref, z_ref, q_ref, k_ref, v_ref, up_ref, qp_ref, kp_ref, vp_ref, ba_ref,
             dyp_ref, dqn_ref, dkn_ref, dvv_ref, dgb_ref, ddz_ref, x_ref, dh_ref,
             nw_ref, pw_ref, ps_ref, cw_ref, al_ref, db_ref, wt_hbm,
             gx_ref, p_hbm, gnw_ref, dpw_ref, dps_ref, dcw_ref, dal_ref, ddb_ref,
             wt_vmem, acc, blk, head_dc, head_dw, sem, osem):
        j = pl.program_id(0)
        i = nstep - 1 - j

        @pl.when(j == 0)
        def _():
            cp = pltpu.make_async_copy(wt_hbm, wt_vmem, sem)
            cp.start()
            acc[...] = jnp.zeros_like(acc)
            for ref in (gnw_ref, dpw_ref, dps_ref, dcw_ref, dal_ref, ddb_ref, head_dc, head_dw):
                ref[...] = jnp.zeros_like(ref)
            cp.wait()

        u = u_ref[...]
        z = z_ref[...]
        dy = dyp_ref[...]
        ps = ps_ref[...]
        mixes = _pool_mix(u, _prev_tail(up_ref, i), i, tm)
        sg = jax.nn.sigmoid(z)
        sz = z * sg
        dsz = sg * (1.0 + z * (1.0 - sg))
        dzs, dwins = [], []
        for gi, w in enumerate(POOL_WINDOWS):
            cols = slice(gi * POOL_GROUP, (gi + 1) * POOL_GROUP)
            mixw = _dot_bf(mixes[gi], pw_ref[gi])
            dmixw = dy[:, cols] * ps[:, cols] * sz[:, cols]
            dps_ref[:, cols] += jnp.sum(dy[:, cols] * mixw * sz[:, cols], axis=0, keepdims=True)
            dzs.append(dy[:, cols] * mixw * ps[:, cols] * dsz[:, cols])
            dpw_ref[gi] += _dot_tn_bf(mixes[gi], dmixw)
            dwins.append(_dot_nt_bf(dmixw, pw_ref[gi]) / _pool_counts(i, tm, w))
        dzp = jnp.concatenate(dzs, axis=1)
        dw = jnp.concatenate(dwins, axis=1)
        win = jnp.concatenate([dw, head_dw[...]], axis=0)
        m = win.shape[0]
        dups = []
        for gi, w in enumerate(POOL_WINDOWS):
            win = win + pltpu.roll(win, m - w // 2, 0)
            cols = slice(gi * POOL_GROUP, (gi + 1) * POOL_GROUP)
            dups.append(win[:tm, :POOL_GROUP] - dw[:, cols] * _pool_counts(i, tm, w))
            if gi + 1 < len(POOL_WINDOWS):
                win = win[:, POOL_GROUP:]
        dup = jnp.concatenate(dups, axis=1)
        head_dw[...] = dw[:HALO, :]

        curs = (q_ref[...], k_ref[...], v_ref[...])
        tails = (_prev_tail(qp_ref, i), _prev_tail(kp_ref, i), _prev_tail(vp_ref, i))
        taps = [_conv_taps(curs[c], tails[c]) for c in range(3)]
        ys = [_conv_of_taps(taps[c], cw_ref[:, c * D_DN:(c + 1) * D_DN]) for c in range(3)]
        _, vjp = jax.vjp(_post_conv, *ys)
        dys = vjp((dqn_ref[...], dkn_ref[...], dvv_ref[...]))
        dxs = []
        for c, dyc in enumerate(dys):
            cols = slice(c * D_DN, (c + 1) * D_DN)
            w4 = cw_ref[:, cols]
            for sft in range(CONV_WIDTH):
                row = CONV_WIDTH - 1 - sft
                dcw_ref[row:row + 1, cols] += jnp.sum(dyc * taps[c][sft], axis=0, keepdims=True)
            head = head_dc[:, cols]
            dx = dyc * w4[CONV_WIDTH - 1:CONV_WIDTH, :]
            for sft in range(1, CONV_WIDTH):
                dx = dx + _shift_up(dyc, head, sft) * w4[CONV_WIDTH - 1 - sft:CONV_WIDTH - sft, :]
            dxs.append(dx)
            head_dc[:, cols] = dyc[:HALO, :]
        _, gvjp = jax.vjp(_gates, ba_ref[...], al_ref[...], db_ref[...])
        dba, dal, ddb = gvjp(dgb_ref[...])
        dal_ref[...] += dal
        ddb_ref[...] += ddb

        dbab = dba.astype(BF16)
        xhat, r = _rms_hat(x_ref[...])
        nw = nw_ref[...]
        n = (xhat * nw).astype(BF16)
        acc[D_MAIN:, :] += _dot_tn_bf(dbab, n)
        dn = jnp.dot(dbab, wt_vmem[D_MAIN:, :], preferred_element_type=F32)
        for cb, d in enumerate((dup, dzp, dxs[0], dxs[1], dxs[2], ddz_ref[...])):
            rows = slice(cb * D_POOL, (cb + 1) * D_POOL)
            dpart = d.astype(BF16)
            acc[rows, :] += _dot_tn_bf(dpart, n)
            dn = dn + jnp.dot(dpart, wt_vmem[rows, :], preferred_element_type=F32)
        gnw_ref[...] += jnp.sum(dn * xhat, axis=0, keepdims=True)
        dxh = dn * nw
        gx_ref[...] = dh_ref[...] + r * (dxh - xhat * jnp.mean(dxh * xhat, axis=-1, keepdims=True))

        @pl.when(j == nstep - 1)
        def _():
            def out(d):
                return pltpu.make_async_copy(blk.at[d % 2], p_hbm.at[d], osem.at[d % 2])
            for d in range(N_DEV):
                if d >= 2:
                    out(d - 2).wait()
                blk[d % 2] = acc[W_IN_SHARD * d:W_IN_SHARD * (d + 1), :].astype(BF16)
                out(d).start()
            out(N_DEV - 2).wait()
            out(N_DEV - 1).wait()

    def col(c):
        return pl.BlockSpec((tm, D_POOL), lambda j: (nstep - 1 - j, c))

    def halo(c):
        return pl.BlockSpec((HALO, D_POOL), lambda j: (jnp.maximum((nstep - 1 - j) * per - 1, 0), c))

    rev = lambda j: (nstep - 1 - j, 0)
    part = pl.BlockSpec((tm, D_POOL), rev)
    lanes = pl.BlockSpec((tm, 128), rev)
    full = pl.BlockSpec((tm, D_MODEL), rev)
    row = pl.BlockSpec((1, D_MODEL), lambda j: (0, 0))
    lrow = pl.BlockSpec((1, 128), lambda j: (0, 0))
    pw = pl.BlockSpec((4, POOL_GROUP, POOL_GROUP), lambda j: (0, 0, 0))
    psp = pl.BlockSpec((1, D_POOL), lambda j: (0, 0))
    cw = pl.BlockSpec((CONV_WIDTH, 3 * D_DN), lambda j: (0, 0))
    return pl.pallas_call(
        body, name="back", grid=(nstep,),
        out_shape=(jax.ShapeDtypeStruct((s, D_MODEL), F32),
                   jax.ShapeDtypeStruct((N_DEV, W_IN_SHARD, D_MODEL), BF16), jax.ShapeDtypeStruct((1, D_MODEL), F32),
                   jax.ShapeDtypeStruct((4, POOL_GROUP, POOL_GROUP), F32), jax.ShapeDtypeStruct((1, D_POOL), F32),
                   jax.ShapeDtypeStruct((CONV_WIDTH, 3 * D_DN), F32),
                   jax.ShapeDtypeStruct((1, 128), F32), jax.ShapeDtypeStruct((1, 128), F32)),
        in_specs=[col(0), col(1), col(2), col(3), col(4), halo(0), halo(2), halo(3), halo(4), lanes,
                  part, part, part, part, lanes, part, full, full,
                  row, pw, psp, cw, lrow, lrow, pl.BlockSpec(memory_space=pl.ANY)],
        out_specs=(full, pl.BlockSpec(memory_space=pl.ANY), row, pw, psp, cw, lrow, lrow),
        scratch_shapes=[pltpu.VMEM((D_IN_PAD, D_MODEL), BF16), pltpu.VMEM((D_IN_PAD, D_MODEL), F32),
                        pltpu.VMEM((2, W_IN_SHARD, D_MODEL), BF16),
                        pltpu.VMEM((HALO, 3 * D_DN), F32), pltpu.VMEM((HALO, D_POOL), F32),
                        pltpu.SemaphoreType.DMA, pltpu.SemaphoreType.DMA((2,))],
        compiler_params=_cp(("arbitrary",)),
    )(proj_main, proj_main, proj_main, proj_main, proj_main, proj_main, proj_main, proj_main, proj_main, proj_ba,
      dyp, dqn, dkn, dvv, dgb, ddz, x2, dh, norm_w, pool_w, pool_scale, conv_full, alog_lane, dtb_lane, wt_full)


def _adamw_math(w, g, m, v):
    m = ADAM_B1 * m + (1.0 - ADAM_B1) * g
    v = ADAM_B2 * v + (1.0 - ADAM_B2) * (g * g)
    m_hat = m / (1.0 - ADAM_B1 ** ADAM_STEP)
    v_hat = v / (1.0 - ADAM_B2 ** ADAM_STEP)
    delta = -ADAM_LR * (m_hat / (jnp.sqrt(v_hat) + ADAM_EPS) + ADAM_WD * w)
    return delta, m, v


def _adamw(tiled, whole, gath_a, gath_b, pool, rows, nstep):
    kw, nrow = len(whole), len(rows)
    ins = list(tiled) + [a for p in whole for a in p] + [gath_a, gath_b] + list(pool) + [a for wmv in rows for a in wmv]

    def body(*refs):
        in_refs, outs = refs[:len(ins)], refs[len(ins):]

        def update(w, g, m, v, o):
            dl, nm, nv = _adamw_math(w, g, m, v)
            for ref, val in zip(o, (g, dl, nm, nv)):
                ref[...] = val

        update(*(r[...] for r in in_refs[:4]), outs[:4])

        @pl.when(pl.program_id(0) == 0)
        def _():
            for p in range(1, kw + 1):
                update(*(r[...] for r in in_refs[4 * p:4 * p + 4]), outs[4 * p:4 * p + 4])
            ga_ref, gb_ref = in_refs[4 * kw + 4:4 * kw + 6]
            rep_in, rep_out = in_refs[4 * kw + 6:], outs[4 * kw + 4:]

            def total(ref):
                g = ref[0]
                for d in range(1, N_DEV):
                    g = g + ref[d]
                return g

            gs = [total(ga_ref)]
            gb = total(gb_ref)
            gs += [gb[r:r + 1, :rep_in[3 * (r + 1)].shape[1]] for r in range(nrow)]
            for r, g in enumerate(gs):
                w, m, v = (ref[...] for ref in rep_in[3 * r:3 * r + 3])
                update(w, g, m, v, rep_out[4 * r:4 * r + 4])
            rep_out[4 * (nrow + 1)][...] = gb[nrow:nrow + 1, 0:1]

    tile = pl.BlockSpec((tiled[0].shape[0] // nstep, tiled[0].shape[1]), lambda i: (i, 0))

    def full(a):
        return pl.BlockSpec(a.shape, lambda i, nd=a.ndim: (0,) * nd)

    firsts = [p[0] for p in whole] + [pool[0]] + [wmv[0] for wmv in rows]
    out_shape = ([jax.ShapeDtypeStruct(tiled[0].shape, F32)] * 4
                 + [jax.ShapeDtypeStruct(w.shape, F32) for w in firsts for _ in range(4)]
                 + [jax.ShapeDtypeStruct((1, 1), F32)])
    res = pl.pallas_call(
        body, name="adamw", grid=(nstep,),
        in_specs=[tile] * 4 + [full(a) for a in ins[4:]],
        out_specs=tuple([tile] * 4 + [full(o) for o in out_shape[4:]]),
        out_shape=tuple(out_shape),
        compiler_params=_cp(("arbitrary",)),
    )(*ins)
    groups = [tuple(res[4 * k:4 * k + 4]) for k in range(kw + nrow + 2)]
    return groups[0], groups[1:kw + 1], groups[kw + 1:], res[-1]


_ROW_ORDER = ("norm_w", "final_norm_w", "pool_scale", "dn_norm_w", "a_log", "dt_bias")


def _pack_rows(vectors):
    out = [jnp.pad(v.reshape(-1), (0, D_MODEL - v.size)) for v in vectors]
    out += [jnp.zeros((D_MODEL,), F32)] * (8 - len(out))
    return jnp.stack(out, axis=0)


def _lane_row(vec4, start):
    return jnp.pad(vec4.reshape(-1), (start, 128 - start - vec4.size)).reshape(1, 128)


def kernel(x, norm_w, w_in, pool_w, pool_scale, conv_w, a_log, dt_bias, dn_norm_w, w_out, final_norm_w, loss_target, m_norm_w, m_w_in, m_pool_w, m_pool_scale, m_conv_w, m_a_log, m_dt_bias, m_dn_norm_w, m_w_out, m_final_norm_w, v_norm_w, v_w_in, v_pool_w, v_pool_scale, v_conv_w, v_a_log, v_dt_bias, v_dn_norm_w, v_w_out, v_final_norm_w):
    s = x.shape[1]
    tm = min(512, s)
    tmb = min(256, s)
    x2 = x[0]
    tgt = loss_target[0]
    def to_flat(a):
        return a[0].reshape(FLAT_ROWS, 128, W_IN_SHARD).transpose(2, 0, 1).reshape(W_IN_SHARD * FLAT_ROWS, 128)

    def from_flat(f):
        return f.reshape(W_IN_SHARD, FLAT_ROWS, 128).transpose(1, 2, 0).reshape(1, D_MODEL, W_IN_SHARD)

    wf, m_wf, v_wf = to_flat(w_in), to_flat(m_w_in), to_flat(v_w_in)

    g_in, g_conv, n_all = _gather_weights(wf, conv_w[0], x2, norm_w, tm)
    conv_full = g_conv.transpose(1, 0, 2).reshape(CONV_WIDTH, 3 * D_DN)
    alog_lane = _lane_row(a_log, DN_HEADS)
    dtb_lane = _lane_row(dt_bias, DN_HEADS)
    fnw = final_norm_w.reshape(1, D_MODEL)

    proj_main, proj_ba, y_pool, qn, kn, vv, gb, g_out, wt_full = _front(
        n_all, g_in, pool_w[0], pool_scale, conv_full, alog_lane, dtb_lane, w_out[0], tm)
    w_out_full = g_out.reshape(D_MODEL, D_MODEL)
    o_dn, states, dn_pre, dn_uw = _dn_scan_fwd(qn, kn, vv, gb, DN_CHUNKS_PER_STEP)

    dh, dyp, do_dn, ddz, g_wout, g_fnw, loss_part, g_dnw = _out_proj_loss(
        y_pool, o_dn, proj_main, dn_norm_w, x2, tgt, w_out_full, fnw, tm)
    p_out = g_wout.reshape(N_DEV, D_MODEL // N_DEV, D_MODEL)
    dqn, dkn, dvv, dgb, gr_out = _dn_scan_bwd(qn, kn, vv, gb, states, dn_pre, dn_uw, do_dn, p_out, DN_CHUNKS_PER_STEP)
    grad_x2, p_in, g_nw, g_pw, g_ps, g_conv_full, g_al, g_db = _back(
        proj_main, proj_ba, dyp, dqn, dkn, dvv, dgb, ddz, x2, dh, norm_w, pool_w[0], pool_scale, conv_full,
        alog_lane, dtb_lane, wt_full, tmb)

    p_conv = g_conv_full.reshape(CONV_WIDTH, N_DEV, 3 * D_DN // N_DEV).transpose(1, 0, 2)
    pack_a = g_pw.reshape(4 * POOL_GROUP, POOL_GROUP)
    pack_b = _pack_rows([g_nw, g_fnw, g_ps, g_dnw, g_al[0, DN_HEADS:2 * DN_HEADS], g_db[0, DN_HEADS:2 * DN_HEADS],
                         loss_part[0, :1]])
    gr_in, gr_conv, gath_a, gath_b = _reduce_grads((p_in, p_conv), (wf, conv_w[0]), pack_a, pack_b)

    flat = lambda a: a.reshape(4 * POOL_GROUP, POOL_GROUP)
    row = lambda a: a.reshape(1, -1)
    vecs = {"norm_w": (norm_w, m_norm_w, v_norm_w), "final_norm_w": (final_norm_w, m_final_norm_w, v_final_norm_w),
            "pool_scale": (pool_scale, m_pool_scale, v_pool_scale), "dn_norm_w": (dn_norm_w, m_dn_norm_w, v_dn_norm_w),
            "a_log": (a_log, m_a_log, v_a_log), "dt_bias": (dt_bias, m_dt_bias, v_dt_bias)}
    r_in, (r_out, r_conv), res, loss = _adamw(
        (wf, gr_in, m_wf, v_wf),
        [(w_out[0], gr_out, m_w_out[0], v_w_out[0]), (conv_w[0], gr_conv, m_conv_w[0], v_conv_w[0])],
        gath_a, gath_b, (flat(pool_w), flat(m_pool_w), flat(v_pool_w)),
        [tuple(row(a) for a in vecs[nm]) for nm in _ROW_ORDER], ADAMW_STEPS)
    r_pool = res[0]
    r_vec = dict(zip(_ROW_ORDER, res[1:]))

    def group(k):
        vec = lambda nm: r_vec[nm][k].reshape(vecs[nm][0].shape)
        return (vec("norm_w"), from_flat(r_in[k]), r_pool[k].reshape(pool_w.shape), vec("pool_scale"), r_conv[k][None],
                vec("a_log"), vec("dt_bias"), vec("dn_norm_w"), r_out[k][None], vec("final_norm_w"))

    return (loss[0, 0], grad_x2[None], *group(0), *group(1), *group(2), *group(3))
```

```python
import functools

import jax
import jax.numpy as jnp
from jax import lax
from jax.experimental import pallas as pl
from jax.experimental.pallas import tpu as pltpu

F32 = jnp.float32
BF16 = jnp.bfloat16
HI = lax.Precision.HIGHEST
MESH = pl.DeviceIdType.MESH

D_MODEL = 1024
D_POOL = 512
D_DN = 512
POOL_WINDOWS = (2, 4, 8, 16)
POOL_GROUP = 128
DN_HEADS = 4
DN_HEAD_DIM = 128
CONV_WIDTH = 4
CHUNK = 64
NORM_EPS = 1e-6
D_IN = 3080
D_MAIN = 3072
FLAT_ROWS = D_MODEL // 128
D_IN_PAD = D_MAIN + 128
N_DEV = 8
W_IN_SHARD = D_IN // N_DEV
HALO = 16
DN_CHUNKS_PER_STEP = 8
ADAMW_STEPS = 5

ADAM_LR = 0.001
ADAM_B1 = 0.9
ADAM_B2 = 0.999
ADAM_EPS = 1e-08
ADAM_WD = 0.01
ADAM_STEP = 10

VMEM_LIMIT = 56 * 1024 * 1024
def _cp(sem=None, vmem=VMEM_LIMIT):
    kw = {"vmem_limit_bytes": vmem}
    if sem is not None:
        kw["dimension_semantics"] = sem
    return pltpu.CompilerParams(**kw)


def _dot_bf(a, b):
    return jnp.dot(a.astype(BF16), b.astype(BF16), preferred_element_type=F32)


def _dot_nt_bf(a, b):
    return lax.dot_general(a.astype(BF16), b.astype(BF16), (((1,), (1,)), ((), ())), preferred_element_type=F32)


def _dot_tn_bf(a, b):
    return lax.dot_general(a.astype(BF16), b.astype(BF16), (((0,), (0,)), ((), ())), preferred_element_type=F32)


def _mm_raw(a, b, ca, cb, prec):
    off = a.ndim - 2
    dn = (((ca + off,), (cb + off,)), ((0,), (0,)) if off else ((), ()))
    if prec == "hi":
        return lax.dot_general(a, b, dn, precision=HI, preferred_element_type=F32)
    ah, bh = a.astype(BF16), b.astype(BF16)
    out = lax.dot_general(ah, bh, dn, preferred_element_type=F32)
    if prec == "x3":
        al = (a - ah.astype(F32)).astype(BF16)
        bl = (b - bh.astype(F32)).astype(BF16)
        out = out + lax.dot_general(ah, bl, dn, preferred_element_type=F32)
        out = out + lax.dot_general(al, bh, dn, preferred_element_type=F32)
    return out


@functools.partial(jax.custom_vjp, nondiff_argnums=(2, 3, 4, 5))
def _mm(a, b, ca, cb, prec, bprec):
    return _mm_raw(a, b, ca, cb, prec)


def _mm_fwd(a, b, ca, cb, prec, bprec):
    return _mm_raw(a, b, ca, cb, prec), (a, b)


def _mm_bwd(ca, cb, prec, bprec, res, dc):
    a, b = res
    da = _mm_raw(dc, b, 1, 1 - cb, bprec) if ca == 1 else _mm_raw(b, dc, 1 - cb, 1, bprec)
    db = _mm_raw(a, dc, 1 - ca, 0, bprec) if cb == 0 else _mm_raw(dc, a, 0, 1 - ca, bprec)
    return da, db


_mm.defvjp(_mm_fwd, _mm_bwd)


@functools.partial(jax.custom_vjp, nondiff_argnums=(1, 2))
def _tri_inv(a, prec, bprec):
    n = a.shape[-1]
    ii = lax.broadcasted_iota(jnp.int32, (n, n), 0)
    jj = lax.broadcasted_iota(jnp.int32, (n, n), 1)
    p = (ii == jj).astype(F32) - a
    b = _mm_raw(a, a, 1, 0, prec)
    for _ in range(4):
        pb = _mm_raw(jnp.concatenate([p, b], axis=-2), b, 1, 0, prec)
        p = p + pb[..., :n, :]
        b = pb[..., n:, :]
    return p + _mm_raw(p, b, 1, 0, prec)


def _tri_inv_fwd(a, prec, bprec):
    t = _tri_inv(a, prec, bprec)
    return t, t


def _tri_inv_bwd(prec, bprec, t, dt):
    return (-_mm_raw(_mm_raw(t, dt, 0, 0, bprec), t, 1, 1, bprec),)


_tri_inv.defvjp(_tri_inv_fwd, _tri_inv_bwd)

@functools.partial(jax.custom_vjp, nondiff_argnums=(3, 4, 5))
def _mm_known(a, b, out, ca, cb, bprec):
    return out


def _mm_known_fwd(a, b, out, ca, cb, bprec):
    return out, (a, b)


def _mm_known_bwd(ca, cb, bprec, res, dc):
    return _mm_bwd(ca, cb, None, bprec, res, dc) + (jnp.zeros_like(dc),)


_mm_known.defvjp(_mm_known_fwd, _mm_known_bwd)


@jax.custom_vjp
def _use_known(x, known):
    return known


_use_known.defvjp(lambda x, known: (known, None), lambda _, g: (g, jnp.zeros_like(g)))


@functools.partial(jax.custom_vjp, nondiff_argnums=(2,))
def _tri_inv_known(a, t, bprec):
    return t


def _tri_inv_known_fwd(a, t, bprec):
    return t, t


def _tri_inv_known_bwd(bprec, t, dt):
    return _tri_inv_bwd(None, bprec, t, dt) + (jnp.zeros_like(dt),)


_tri_inv_known.defvjp(_tri_inv_known_fwd, _tri_inv_known_bwd)

_DN_PREC = {"akq": ("bf16", "bf16"), "inv": ("bf16", "bf16"), "uw": ("bf16", "bf16"), "ws": ("bf16", "bf16"),
            "ov": ("bf16", "bf16"), "st": ("bf16", "bf16")}


def _silu(x):
    return x * jax.nn.sigmoid(x)


def _softplus(x):
    pos = x > 0.0
    return jnp.where(pos, x, 0.0) + jnp.log1p(jnp.exp(jnp.where(pos, -x, x)))


def _mesh_pos():
    return lax.axis_index("x"), lax.axis_index("y"), lax.axis_index("c")


def _dev_index(x, y, c):
    return 4 * x + 2 * y + c


def _relay_order():
    x, y, c = _mesh_pos()
    n1 = (x + (1 - c) * (1 - 2 * x), y + c * (1 - 2 * y))
    n2 = (x + c * (1 - 2 * x), y + (1 - c) * (1 - 2 * y))
    return (x, y, c), (x, y, 1 - c), n1, n2, (1 - x, 1 - y)


def _all_gather_blocks(outs, send_sems, recv_sems, meanwhile=None, own=None):
    me, sibling, n1, n2, diag = _relay_order()
    c = me[2]

    def copy(a, k, block, to, src=None):
        rows = outs[a].at[_dev_index(*block)]
        return pltpu.make_async_remote_copy(src_ref=rows if src is None else src, dst_ref=rows,
                                            send_sem=send_sems.at[a, k], recv_sem=recv_sems.at[a, k],
                                            device_id=to, device_id_type=MESH)

    n = len(outs)
    started = []

    def start(cp):
        cp.start()
        started.append(cp)

    for a in range(n):
        mine = None if own is None else own[a]
        start(copy(a, 1, me, (*n1, c), mine))
        start(copy(a, 2, me, (*n2, c), mine))
        start(copy(a, 0, me, sibling, mine))
    if meanwhile is not None:
        meanwhile()
    for a in range(n):
        copy(a, 1, (*n1, c), me).wait_recv()
        start(copy(a, 3, (*n1, c), (*n2, c)))
        start(copy(a, 4, (*n1, c), sibling))
    for a in range(n):
        copy(a, 2, (*n2, c), me).wait_recv()
        start(copy(a, 5, (*n2, c), sibling))
        copy(a, 3, (*diag, c), me).wait_recv()
        start(copy(a, 6, (*diag, c), sibling))
    for a in range(n):
        copy(a, 0, sibling, me).wait_recv()
        copy(a, 4, (*n2, 1 - c), me).wait_recv()
        copy(a, 5, (*n1, 1 - c), me).wait_recv()
        copy(a, 6, (*diag, 1 - c), me).wait_recv()
    for cp in started:
        cp.wait_send()


def _peer_relations():
    x, y, c = _mesh_pos()
    flips = [(fx, fy, fc) for fx in (0, 1) for fy in (0, 1) for fc in (0, 1)][1:]
    peers = [(1 - x if fx else x, 1 - y if fy else y, 1 - c if fc else c) for fx, fy, fc in flips]
    return (x, y, c), peers


def _direct_gather_start(out_ref, send_sems, recv_sems):
    me, peers = _peer_relations()
    rows = out_ref.at[_dev_index(*me)]
    for k, peer in enumerate(peers):
        pltpu.make_async_remote_copy(src_ref=rows, dst_ref=rows, send_sem=send_sems.at[k], recv_sem=recv_sems.at[k],
                                     device_id=peer, device_id_type=MESH).start()


def _direct_gather_wait(out_ref, send_sems, recv_sems):
    me, peers = _peer_relations()
    for k, peer in enumerate(peers):
        rows = out_ref.at[_dev_index(*peer)]
        cp = pltpu.make_async_remote_copy(src_ref=rows, dst_ref=rows, send_sem=send_sems.at[k],
                                          recv_sem=recv_sems.at[k], device_id=peer, device_id_type=MESH)
        cp.wait_recv()
        cp.wait_send()


def _direct_scatter_start(send_ref, recv_ref, send_sems, recv_sems):
    me, peers = _peer_relations()
    for k, peer in enumerate(peers):
        pltpu.make_async_remote_copy(src_ref=send_ref.at[_dev_index(*peer)], dst_ref=recv_ref.at[k],
                                     send_sem=send_sems.at[k], recv_sem=recv_sems.at[k],
                                     device_id=peer, device_id_type=MESH).start()


def _direct_scatter_wait(send_ref, recv_ref, send_sems, recv_sems):
    me, peers = _peer_relations()
    for k, peer in enumerate(peers):
        cp = pltpu.make_async_remote_copy(src_ref=send_ref.at[_dev_index(*peer)], dst_ref=recv_ref.at[k],
                                          send_sem=send_sems.at[k], recv_sem=recv_sems.at[k],
                                          device_id=peer, device_id_type=MESH)
        cp.wait_recv()
        cp.wait_send()


def _gather_weights(w_in_flat, conv_blk, x2, norm_w, tn):
    nt = x2.shape[0] // tn

    def body(win_ref, conv_ref, x_hbm, nw_ref, gin_hbm, gconv_ref, n_hbm, own_ref, xbuf, nbuf, send_sems, recv_sems,
             xsem, nsem, own_sem):
        x, y, c = _mesh_pos()
        me = _dev_index(x, y, c)
        for j in range(FLAT_ROWS):
            own_ref[:, 128 * j:128 * (j + 1)] = win_ref[pl.ds(j, W_IN_SHARD, stride=FLAT_ROWS), :].astype(BF16)
        keep_own = pltpu.make_async_copy(own_ref, gin_hbm.at[me], own_sem.at[0])
        keep_own.start()
        gconv_ref[me] = conv_ref[...]

        def norm_x():
            def load(t):
                return pltpu.make_async_copy(x_hbm.at[pl.ds(t * tn, tn), :], xbuf.at[t % 2], xsem.at[t % 2])

            def store(t):
                return pltpu.make_async_copy(nbuf.at[t % 2], n_hbm.at[pl.ds(t * tn, tn), :], nsem.at[t % 2])

            load(0).start()
            for t in range(nt):
                if t + 1 < nt:
                    load(t + 1).start()
                load(t).wait()
                if t >= 2:
                    store(t - 2).wait()
                xhat, _ = _rms_hat(xbuf[t % 2])
                nbuf[t % 2] = (xhat * nw_ref[...]).astype(BF16)
                store(t).start()
            for t in range(max(nt - 2, 0), nt):
                store(t).wait()

        _all_gather_blocks((gin_hbm, gconv_ref), send_sems, recv_sems, meanwhile=norm_x, own=(own_ref, None))
        keep_own.wait()

    vm = pl.BlockSpec(memory_space=pltpu.VMEM)
    hbm = pl.BlockSpec(memory_space=pl.ANY)
    return pl.pallas_call(
        body, name="gather_weights",
        out_shape=(jax.ShapeDtypeStruct((N_DEV, W_IN_SHARD, D_MODEL), BF16),
                   jax.ShapeDtypeStruct((N_DEV,) + conv_blk.shape, F32),
                   jax.ShapeDtypeStruct(x2.shape, BF16)),
        in_specs=[vm, vm, hbm, vm], out_specs=(hbm, vm, hbm),
        scratch_shapes=[pltpu.VMEM((W_IN_SHARD, D_MODEL), BF16),
                        pltpu.VMEM((2, tn, D_MODEL), F32), pltpu.VMEM((2, tn, D_MODEL), BF16),
                        pltpu.SemaphoreType.DMA((2, 7)), pltpu.SemaphoreType.DMA((2, 7)),
                        pltpu.SemaphoreType.DMA((2,)), pltpu.SemaphoreType.DMA((2,)), pltpu.SemaphoreType.DMA((1,))],
        compiler_params=_cp(),
    )(w_in_flat, conv_blk, x2, norm_w)


def _reduce_grads(big, like, pack_a, rows):
    nb = len(big)

    def body(*refs):
        nr = len(rows)
        srcs, pa_ref, row_refs = refs[:nb], refs[nb], refs[nb + 1:nb + 1 + nr]
        outs, (ga_ref, gb_ref) = refs[nb + 1 + nr:2 * nb + 1 + nr], refs[2 * nb + 1 + nr:2 * nb + 3 + nr]
        scr = refs[2 * nb + 3 + nr:]
        r1s, r2s, sbs, sts = (scr[k * nb:(k + 1) * nb] for k in range(4))
        s1_send, s1_recv, s2_send, s2_recv, ag_send, ag_recv, st_sem = scr[4 * nb:]
        x, y, c = _mesh_pos()
        me = (x, y, c)
        sibling = (x, y, 1 - c)

        ga_ref[_dev_index(*me)] = pa_ref[...]
        gb_ref[_dev_index(*me)] = jnp.zeros(gb_ref.shape[1:], F32)
        for r, ref in enumerate(row_refs):
            gb_ref[_dev_index(*me), r:r + 1, :ref.shape[1]] = ref[...]

        _, _, n1, n2, diag = _relay_order()
        order = ((diag, 3), (n1, 1 + c), (n2, 2 - c), ((x, y), 0))

        def p1(a, i, to):
            chip, slot = order[i]
            return pltpu.make_async_remote_copy(
                src_ref=srcs[a].at[_dev_index(*chip, 1 - c)], dst_ref=r1s[a].at[slot],
                send_sem=s1_send.at[a, slot], recv_sem=s1_recv.at[a, slot], device_id=to, device_id_type=MESH)

        def p1_landed(a, slot):
            return pltpu.make_async_remote_copy(
                src_ref=r1s[a].at[slot], dst_ref=r1s[a].at[slot], send_sem=s1_send.at[a, slot],
                recv_sem=s1_recv.at[a, slot], device_id=me, device_id_type=MESH)

        def p2(a, k, to):
            return pltpu.make_async_remote_copy(
                src_ref=sbs[a].at[k], dst_ref=r2s[a].at[k],
                send_sem=s2_send.at[a, k], recv_sem=s2_recv.at[a, k], device_id=to, device_id_type=MESH)

        def stage(a, i):
            return pltpu.make_async_copy(srcs[a].at[_dev_index(*order[i][0], c)], sts[a].at[i % 2], st_sem.at[a, i % 2])

        sends = [p1(a, i, sibling) for a in range(nb) for i in (0, 2, 1, 3)]
        for cp in sends:
            cp.start()

        def step_1():
            for a in range(nb):
                stage(a, 0).start()
                for i, (_, slot) in enumerate(order):
                    if i + 1 < len(order):
                        stage(a, i + 1).start()
                    stage(a, i).wait()
                    p1_landed(a, slot).wait_recv()
                    chip_sum = r1s[a][slot] + sts[a][i % 2]
                    if i < 2:
                        sbs[a][i] = chip_sum.astype(BF16)
                        sends.append(p2(a, i, (*n1, c)))
                        sends[-1].start()
                    else:
                        r1s[a][slot] = chip_sum

        _all_gather_blocks((ga_ref, gb_ref), ag_send, ag_recv, meanwhile=step_1)
        for a in range(nb):
            p2(a, 0, me).wait_recv()
            sbs[a][2] = (r1s[a][2 - c] + r2s[a][0].astype(F32)).astype(BF16)
            sends.append(p2(a, 2, (*n2, c)))
            sends[-1].start()
        for a in range(nb):
            p2(a, 1, me).wait_recv()
            p2(a, 2, me).wait_recv()
            total = (r1s[a][0] + r2s[a][1].astype(F32)) + r2s[a][2].astype(F32)
            if outs[a].shape == total.shape:
                outs[a][...] = total
            else:
                r1s[a][0] = total
                for j in range(FLAT_ROWS):
                    outs[a][pl.ds(j, W_IN_SHARD, stride=FLAT_ROWS), :] = r1s[a][0, :, 128 * j:128 * (j + 1)]
        for cp in sends:
            cp.wait_send()

    vm = pl.BlockSpec(memory_space=pltpu.VMEM)
    hbm = pl.BlockSpec(memory_space=pl.ANY)
    blk = [p.shape[1:] for p in big]
    scratch = ([pltpu.VMEM((4,) + b, F32) for b in blk] + [pltpu.VMEM((3,) + b, BF16) for b in blk]
               + [pltpu.VMEM((3,) + b, BF16) for b in blk] + [pltpu.VMEM((2,) + b, F32) for b in blk]
               + [pltpu.SemaphoreType.DMA((nb, 4)), pltpu.SemaphoreType.DMA((nb, 4)),
                  pltpu.SemaphoreType.DMA((nb, 3)), pltpu.SemaphoreType.DMA((nb, 3)),
                  pltpu.SemaphoreType.DMA((2, 7)), pltpu.SemaphoreType.DMA((2, 7)),
                  pltpu.SemaphoreType.DMA((nb, 2))])
    return pl.pallas_call(
        body, name="reduce_grads",
        out_shape=tuple(jax.ShapeDtypeStruct(w.shape, F32) for w in like)
        + (jax.ShapeDtypeStruct((N_DEV,) + pack_a.shape, F32), jax.ShapeDtypeStruct((N_DEV, 8, D_MODEL), F32)),
        in_specs=[hbm] * nb + [vm] * (1 + len(rows)), out_specs=(vm,) * (nb + 2),
        scratch_shapes=scratch,
        compiler_params=_cp(),
    )(*big, pack_a, *rows)


def _rms_hat(xf):
    r = lax.rsqrt(jnp.mean(xf * xf, axis=-1, keepdims=True) + NORM_EPS)
    return xf * r, r


def _shift_up(cur, next_head, s):
    ext = jnp.concatenate([cur, next_head], axis=0)
    n = ext.shape[0]
    return pltpu.roll(ext, n - s, 0)[:cur.shape[0], :]


def _pool_counts(i, tp, w):
    t = i * tp + lax.broadcasted_iota(jnp.int32, (tp, 1), 0)
    return jnp.minimum(t + 1, w).astype(F32)


def _pool_mix(u, u_prev_tail, i, tp):
    win = jnp.concatenate([u_prev_tail, u], axis=0)
    mixes = []
    for gi, w in enumerate(POOL_WINDOWS):
        win = win + pltpu.roll(win, w // 2, 0)
        cols = slice(gi * POOL_GROUP, (gi + 1) * POOL_GROUP)
        mixes.append(win[HALO:, :POOL_GROUP] / _pool_counts(i, tp, w) - u[:, cols])
        if gi + 1 < len(POOL_WINDOWS):
            win = win[:, POOL_GROUP:]
    return mixes


def _prev_tail(ref, i):
    return jnp.where(i > 0, ref[...], 0.0)


def _conv_taps(cur, prev_tail):
    ext = jnp.concatenate([prev_tail, cur], axis=0)
    return [cur] + [pltpu.roll(ext, sft, 0)[HALO:, :] for sft in range(1, CONV_WIDTH)]


def _conv_of_taps(taps, w4):
    y = taps[0] * w4[CONV_WIDTH - 1:CONV_WIDTH, :]
    for sft in range(1, CONV_WIDTH):
        y = y + taps[sft] * w4[CONV_WIDTH - 1 - sft:CONV_WIDTH - sft, :]
    return y


def _conv_fwd(cur, prev_tail, w4):
    ext = jnp.concatenate([prev_tail, cur], axis=0)
    y = ext * w4[CONV_WIDTH - 1:CONV_WIDTH, :]
    for sft in range(1, CONV_WIDTH):
        y = y + pltpu.roll(ext, sft, 0) * w4[CONV_WIDTH - 1 - sft:CONV_WIDTH - sft, :]
    return y[HALO:, :]


def _l2n_heads(t):
    parts = []
    for h in range(DN_HEADS):
        th = t[:, h * DN_HEAD_DIM:(h + 1) * DN_HEAD_DIM]
        parts.append(th * lax.rsqrt(jnp.sum(th * th, axis=-1, keepdims=True) + NORM_EPS))
    return jnp.concatenate(parts, axis=1)


def _post_conv(yq, yk, yv):
    return _l2n_heads(_silu(yq)), _l2n_heads(_silu(yk)), _silu(yv)


def _gates(ba, alog_lane, dtb_lane):
    lane = lax.broadcasted_iota(jnp.int32, ba.shape, 1)
    beta = jax.nn.sigmoid(ba)
    g = -jnp.exp(alog_lane) * _softplus(ba + dtb_lane)
    return jnp.where(lane < DN_HEADS, beta, jnp.where(lane < 2 * DN_HEADS, g, 0.0))


def _front(n_all, g_in, pool_w, pool_scale, conv_full, alog_lane, dtb_lane, w_out_blk, tm):
    s = n_all.shape[0]

    def body(n_ref, pw_ref, ps_ref, cw_ref, al_ref, db_ref, wo_ref, g_hbm,
             pm_ref, pb_ref, yp_ref, qn_ref, kn_ref, vv_ref, gb_ref, gwo_hbm, wt_hbm,
             g_vmem, wt_vmem, tail_u, tail_qkv, gwo_ref, sem, g_sems, wo_send, wo_recv):
        i = pl.program_id(0)

        @pl.when(i == 0)
        def _():
            gwo_ref[_dev_index(*_mesh_pos())] = wo_ref[...].astype(BF16)
            _direct_gather_start(gwo_ref, wo_send, wo_recv)
            loads = [pltpu.make_async_copy(g_hbm.at[d], g_vmem.at[d], g_sems.at[d]) for d in range(N_DEV)]
            for cp in loads:
                cp.start()
            wt_vmem[D_MAIN:, :] = jnp.zeros((D_IN_PAD - D_MAIN, D_MODEL), BF16)
            tail_u[...] = jnp.zeros_like(tail_u)
            tail_qkv[...] = jnp.zeros_like(tail_qkv)
            for d, cp in enumerate(loads):
                cp.wait()
                wt_vmem[W_IN_SHARD * d:W_IN_SHARD * (d + 1), :] = g_vmem[d]
            pltpu.make_async_copy(wt_vmem, wt_hbm, sem).start()
        n = n_ref[...]
        pm_ref[...] = _dot_nt_bf(n, wt_vmem[:D_MAIN, :])
        pb = _dot_nt_bf(n, wt_vmem[D_MAIN:, :])
        pb_ref[...] = pb
        u = pm_ref[:, :D_POOL]
        mixes = _pool_mix(u, tail_u[...], i, tm)
        tail_u[...] = u[tm - HALO:, :]
        gate = ps_ref[...] * _silu(pm_ref[:, D_POOL:2 * D_POOL])
        for gi in range(4):
            cols = slice(gi * POOL_GROUP, (gi + 1) * POOL_GROUP)
            yp_ref[:, cols] = _dot_bf(mixes[gi], pw_ref[gi]) * gate[:, cols]
        ys = []
        for c in range(3):
            cols = slice(c * D_DN, (c + 1) * D_DN)
            cur = pm_ref[:, 2 * D_POOL + c * D_DN:2 * D_POOL + (c + 1) * D_DN]
            ys.append(_conv_fwd(cur, tail_qkv[:, cols], cw_ref[:, cols]))
            tail_qkv[:, cols] = cur[tm - HALO:, :]
        qn, kn, vv = _post_conv(*ys)
        qn_ref[...] = qn
        kn_ref[...] = kn
        vv_ref[...] = vv
        gb_ref[...] = _gates(pb, al_ref[...], db_ref[...])

        @pl.when(i == s // tm - 1)
        def _():
            pltpu.make_async_copy(wt_vmem, wt_hbm, sem).wait()
            _direct_gather_wait(gwo_ref, wo_send, wo_recv)
            out = pltpu.make_async_copy(gwo_ref, gwo_hbm, sem)
            out.start()
            out.wait()

    tile = pl.BlockSpec((tm, D_DN), lambda i: (i, 0))
    lanes = pl.BlockSpec((tm, 128), lambda i: (i, 0))
    row = pl.BlockSpec((1, 128), lambda i: (0, 0))
    return pl.pallas_call(
        body, name="front", grid=(s // tm,),
        out_shape=(jax.ShapeDtypeStruct((s, D_MAIN), F32), jax.ShapeDtypeStruct((s, 128), F32),
                   jax.ShapeDtypeStruct((s, D_POOL), F32), jax.ShapeDtypeStruct((s, D_DN), F32),
                   jax.ShapeDtypeStruct((s, D_DN), F32), jax.ShapeDtypeStruct((s, D_DN), F32),
                   jax.ShapeDtypeStruct((s, 128), F32), jax.ShapeDtypeStruct((N_DEV,) + w_out_blk.shape, BF16),
                   jax.ShapeDtypeStruct((D_IN_PAD, D_MODEL), BF16)),
        in_specs=[pl.BlockSpec((tm, D_MODEL), lambda i: (i, 0)),
                  pl.BlockSpec((4, POOL_GROUP, POOL_GROUP), lambda i: (0, 0, 0)),
                  pl.BlockSpec((1, D_POOL), lambda i: (0, 0)),
                  pl.BlockSpec((CONV_WIDTH, 3 * D_DN), lambda i: (0, 0)), row, row,
                  pl.BlockSpec(memory_space=pltpu.VMEM), pl.BlockSpec(memory_space=pl.ANY)],
        out_specs=(pl.BlockSpec((tm, D_MAIN), lambda i: (i, 0)), lanes, tile, tile, tile, tile, lanes,
                   pl.BlockSpec(memory_space=pl.ANY), pl.BlockSpec(memory_space=pl.ANY)),
        scratch_shapes=[pltpu.VMEM((N_DEV, W_IN_SHARD, D_MODEL), BF16), pltpu.VMEM((D_IN_PAD, D_MODEL), BF16),
                        pltpu.VMEM((HALO, D_POOL), F32), pltpu.VMEM((HALO, 3 * D_DN), F32),
                        pltpu.VMEM((N_DEV,) + w_out_blk.shape, BF16),
                        pltpu.SemaphoreType.DMA, pltpu.SemaphoreType.DMA((N_DEV,)),
                        pltpu.SemaphoreType.DMA((7,)), pltpu.SemaphoreType.DMA((7,))],
        compiler_params=_cp(("arbitrary",)),
    )(n_all, pool_w, pool_scale, conv_full, alog_lane, dtb_lane, w_out_blk, g_in)


def _dn_block(q, k, v, gcol, bcol, state, known=None):
    nb, n, d = q.shape
    ii = lax.broadcasted_iota(jnp.int32, (n, n), 0)
    jj = lax.broadcasted_iota(jnp.int32, (n, n), 1)
    lower = ii >= jj
    eye = (ii == jj).astype(F32)
    g_row = jnp.sum(eye * gcol, axis=1, keepdims=True)
    gc_col = jnp.sum(jnp.where(lower, g_row, 0.0), axis=2, keepdims=True)
    gc_row = jnp.sum(eye * gc_col, axis=1, keepdims=True)
    decay = jnp.where(lower, jnp.exp(jnp.where(lower, gc_col - gc_row, 0.0)), 0.0)
    kb = k * bcol
    vb = v * bcol
    qs = q * (DN_HEAD_DIM ** -0.5)
    egc = jnp.exp(gc_col)
    kq = jnp.concatenate([kb, qs], axis=1)
    vk = jnp.concatenate([vb, kb * egc], axis=2)
    if known is None:
        akq = _mm(kq, k, 1, 1, *_DN_PREC["akq"])
    else:
        akq = _mm_known(kq, k, known[0][:, :, :n].astype(F32), 1, 1, _DN_PREC["akq"][1])
    a = jnp.where(ii > jj, akq[:, :n] * decay, 0.0)
    qk = akq[:, n:] * decay
    if known is None:
        t = _tri_inv(a, *_DN_PREC["inv"])
        uw = _mm(t, vk, 1, 0, *_DN_PREC["uw"])
    else:
        t = _tri_inv_known(a, known[0][:, :n, n:].astype(F32), _DN_PREC["inv"][1])
        uw = _mm_known(t, vk, known[1], 1, 0, _DN_PREC["uw"][1])
    pre = jnp.concatenate([akq, jnp.concatenate([t, jnp.zeros_like(t)], axis=1)], axis=2)
    wq = jnp.concatenate([uw[:, :, d:], qs * egc], axis=1)
    g_last = gc_col[:, n - 1:n, :]
    k_dec = k * jnp.exp(g_last - gc_col)
    e_last = jnp.exp(g_last)
    os_, starts = [], []
    for c in range(nb // DN_HEADS):
        sl = slice(c * DN_HEADS, (c + 1) * DN_HEADS)
        if known is not None and c > 0:
            state = _use_known(state, known[2][sl])
        starts.append(state)
        ws = _mm(wq[sl], state, 1, 0, *_DN_PREC["ws"])
        v_new = uw[sl, :, :d] - ws[:, :n]
        os_.append(ws[:, n:] + _mm(qk[sl], v_new, 1, 0, *_DN_PREC["ov"]))
        state = state * e_last[sl] + _mm(k_dec[sl], v_new, 0, 0, *_DN_PREC["st"])
    return jnp.concatenate(os_, axis=0), state, (pre, uw, jnp.concatenate(starts, axis=0))


def _gated_norm(o, dz, nw):
    parts = []
    for h in range(DN_HEADS):
        oh = o[:, h * DN_HEAD_DIM:(h + 1) * DN_HEAD_DIM]
        parts.append(oh * lax.rsqrt(jnp.mean(oh * oh, axis=-1, keepdims=True) + NORM_EPS) * nw)
    return jnp.concatenate(parts, axis=1) * _silu(dz)


def _dn_block_args(gc, q_ref, k_ref, v_ref, gb_ref):
    qs, ks, vs, gs, bs = [], [], [], [], []
    for cc in range(gc):
        r = slice(cc * CHUNK, (cc + 1) * CHUNK)
        gbv = gb_ref[r, :]
        for h in range(DN_HEADS):
            cols = slice(h * DN_HEAD_DIM, (h + 1) * DN_HEAD_DIM)
            qs.append(q_ref[r, cols])
            ks.append(k_ref[r, cols])
            vs.append(v_ref[r, cols])
            gs.append(gbv[:, DN_HEADS + h:DN_HEADS + h + 1])
            bs.append(gbv[:, h:h + 1])
    return tuple(jnp.stack(t, axis=0) for t in (qs, ks, vs, gs, bs))


def _dn_scan_fwd(qn, kn, vv, gb, gc):
    s = qn.shape[0]
    nchunk = s // CHUNK
    rows = gc * CHUNK

    def body(q_ref, k_ref, v_ref, gb_ref, y_ref, ss_ref, pre_ref, uw_ref, state):
        @pl.when(pl.program_id(0) == 0)
        def _():
            state[...] = jnp.zeros_like(state)
        q, k, v, gcol, bcol = _dn_block_args(gc, q_ref, k_ref, v_ref, gb_ref)
        y, new, (pre, uw, starts) = _dn_block(q, k, v, gcol, bcol, state[...])
        state[...] = new
        ss_ref[...] = starts
        pre_ref[...] = pre
        uw_ref[...] = uw
        for cc in range(gc):
            for h in range(DN_HEADS):
                y_ref[cc * CHUNK:(cc + 1) * CHUNK, h * DN_HEAD_DIM:(h + 1) * DN_HEAD_DIM] = y[cc * DN_HEADS + h]

    tile = pl.BlockSpec((rows, D_DN), lambda i: (i, 0))
    return pl.pallas_call(
        body, name="dn_scan_fwd", grid=(nchunk // gc,),
        out_shape=(jax.ShapeDtypeStruct((s, D_DN), F32),
                   jax.ShapeDtypeStruct((nchunk * DN_HEADS, DN_HEAD_DIM, DN_HEAD_DIM), F32),
                   jax.ShapeDtypeStruct((nchunk * DN_HEADS, 2 * CHUNK, 2 * CHUNK), F32),
                   jax.ShapeDtypeStruct((nchunk * DN_HEADS, CHUNK, 2 * DN_HEAD_DIM), F32)),
        in_specs=[tile, tile, tile, pl.BlockSpec((rows, 128), lambda i: (i, 0))],
        out_specs=(tile, pl.BlockSpec((gc * DN_HEADS, DN_HEAD_DIM, DN_HEAD_DIM), lambda i: (i, 0, 0)),
                   pl.BlockSpec((gc * DN_HEADS, 2 * CHUNK, 2 * CHUNK), lambda i: (i, 0, 0)),
                   pl.BlockSpec((gc * DN_HEADS, CHUNK, 2 * DN_HEAD_DIM), lambda i: (i, 0, 0))),
        scratch_shapes=[pltpu.VMEM((DN_HEADS, DN_HEAD_DIM, DN_HEAD_DIM), F32)],
        compiler_params=_cp(("arbitrary",)),
    )(qn, kn, vv, gb)


def _out_proj_loss(y_pool, o_dn, proj_main, dn_norm_w, x2, tgt, w_out_full, fnw, tm):
    s = x2.shape[0]

    def body(yp_ref, o_ref, dz_ref, nw_ref, x_ref, t_ref, wo_ref, fw_ref,
             dh_ref, dyp_ref, do_ref, ddz_ref, gwo_ref, gfw_ref, loss_ref, dnw_ref):
        @pl.when(pl.program_id(0) == 0)
        def _():
            gwo_ref[...] = jnp.zeros_like(gwo_ref)
            gfw_ref[...] = jnp.zeros_like(gfw_ref)
            loss_ref[...] = jnp.zeros_like(loss_ref)
            dnw_ref[...] = jnp.zeros_like(dnw_ref)
        y_dn, gate_vjp = jax.vjp(_gated_norm, o_ref[...], dz_ref[...], nw_ref[...])
        y = jnp.concatenate([yp_ref[...], y_dn], axis=1).astype(BF16)
        wo = wo_ref[...]
        h = x_ref[...] + jnp.dot(y, wo, preferred_element_type=F32)
        hn, r = _rms_hat(h)
        fw = fw_ref[...]
        err = hn * fw - t_ref[...]
        loss_ref[...] += 0.5 * jnp.sum(jnp.sum(err * err, axis=-1, keepdims=True) / D_MODEL, axis=0, keepdims=True)
        dout = err / D_MODEL
        gfw_ref[...] += jnp.sum(dout * hn, axis=0, keepdims=True)
        dhn = dout * fw
        dh = r * (dhn - hn * jnp.mean(dhn * hn, axis=-1, keepdims=True))
        dh_ref[...] = dh
        dhb = dh.astype(BF16)
        dy = _dot_nt_bf(dhb, wo)
        dyp_ref[...] = dy[:, :D_POOL]
        do, ddz, dnw = gate_vjp(dy[:, D_POOL:])
        do_ref[...] = do
        ddz_ref[...] = ddz
        dnw_ref[...] += dnw
        gwo_ref[...] += _dot_tn_bf(y, dhb)

    half = pl.BlockSpec((tm, D_POOL), lambda i: (i, 0))
    full = pl.BlockSpec((tm, D_MODEL), lambda i: (i, 0))
    lrow = pl.BlockSpec((1, 128), lambda i: (0, 0))
    return pl.pallas_call(
        body, name="out_proj_loss", grid=(s // tm,),
        out_shape=(jax.ShapeDtypeStruct((s, D_MODEL), F32), jax.ShapeDtypeStruct((s, D_POOL), F32),
                   jax.ShapeDtypeStruct((s, D_DN), F32), jax.ShapeDtypeStruct((s, D_DN), F32),
                   jax.ShapeDtypeStruct((D_MODEL, D_MODEL), F32),
                   jax.ShapeDtypeStruct((1, D_MODEL), F32), jax.ShapeDtypeStruct((1, 128), F32),
                   jax.ShapeDtypeStruct((1, 128), F32)),
        in_specs=[half, half, pl.BlockSpec((tm, D_DN), lambda i: (i, 5)), lrow, full, full,
                  pl.BlockSpec((D_MODEL, D_MODEL), lambda i: (0, 0)), pl.BlockSpec((1, D_MODEL), lambda i: (0, 0))],
        out_specs=(full, half, half, half, pl.BlockSpec((D_MODEL, D_MODEL), lambda i: (0, 0)),
                   pl.BlockSpec((1, D_MODEL), lambda i: (0, 0)), lrow, lrow),
        compiler_params=_cp(("arbitrary",)),
    )(y_pool, o_dn, proj_main, dn_norm_w, x2, tgt, w_out_full, fnw)


def _dn_scan_bwd(qn, kn, vv, gb, states, pre, uw, do_dn, p_out, gc):
    s = qn.shape[0]
    nchunk = s // CHUNK
    nstep = nchunk // gc
    rows = gc * CHUNK

    def body(q_ref, k_ref, v_ref, gb_ref, ss_ref, pre_ref, uw_ref, dy_ref, po_ref,
             dq_ref, dk_ref, dv_ref, dgb_ref, gro_ref, dstate, po_send, po_recv, rs_send, rs_recv):
        @pl.when(pl.program_id(0) == 0)
        def _():
            dstate[...] = jnp.zeros_like(dstate)
            po_send[...] = po_ref[...].astype(BF16)
            _direct_scatter_start(po_send, po_recv, rs_send, rs_recv)

        @pl.when(pl.program_id(0) == nstep - 1)
        def _():
            _direct_scatter_wait(po_send, po_recv, rs_send, rs_recv)
            total = po_ref[_dev_index(*_mesh_pos())]
            for k in range(N_DEV - 1):
                total = total + po_recv[k].astype(F32)
            gro_ref[...] = total
        lane = lax.broadcasted_iota(jnp.int32, (CHUNK, 128), 1)
        q, k, v, gcol, bcol = _dn_block_args(gc, q_ref, k_ref, v_ref, gb_ref)
        dy = jnp.stack([dy_ref[cc * CHUNK:(cc + 1) * CHUNK, h * DN_HEAD_DIM:(h + 1) * DN_HEAD_DIM]
                        for cc in range(gc) for h in range(DN_HEADS)], axis=0)
        known = (pre_ref[...], uw_ref[...], ss_ref[...])
        _, vjp = jax.vjp(lambda *a: _dn_block(*a, known=known)[:2], q, k, v, gcol, bcol, ss_ref[:DN_HEADS])
        dq, dk, dv, dg, db, dst = vjp((dy, dstate[...]))
        dstate[...] = dst
        for cc in range(gc):
            r = slice(cc * CHUNK, (cc + 1) * CHUNK)
            dgb = jnp.zeros((CHUNK, 128), F32)
            for h in range(DN_HEADS):
                b = cc * DN_HEADS + h
                cols = slice(h * DN_HEAD_DIM, (h + 1) * DN_HEAD_DIM)
                for ref, val in zip((dq_ref, dk_ref, dv_ref), (dq, dk, dv)):
                    ref[r, cols] = val[b]
                dgb = dgb + jnp.where(lane == h, db[b], 0.0) + jnp.where(lane == DN_HEADS + h, dg[b], 0.0)
            dgb_ref[r, :] = dgb

    rev = lambda i: (nstep - 1 - i, 0)
    tile = pl.BlockSpec((rows, D_DN), rev)
    lanes = pl.BlockSpec((rows, 128), rev)
    return pl.pallas_call(
        body, name="dn_scan_bwd", grid=(nstep,),
        out_shape=(jax.ShapeDtypeStruct((s, D_DN), F32),) * 3
        + (jax.ShapeDtypeStruct((s, 128), F32), jax.ShapeDtypeStruct(p_out.shape[1:], F32)),
        in_specs=[tile, tile, tile, lanes,
                  pl.BlockSpec((gc * DN_HEADS, DN_HEAD_DIM, DN_HEAD_DIM), lambda i: (nstep - 1 - i, 0, 0)),
                  pl.BlockSpec((gc * DN_HEADS, 2 * CHUNK, 2 * CHUNK), lambda i: (nstep - 1 - i, 0, 0)),
                  pl.BlockSpec((gc * DN_HEADS, CHUNK, 2 * DN_HEAD_DIM), lambda i: (nstep - 1 - i, 0, 0)), tile,
                  pl.BlockSpec(memory_space=pltpu.VMEM)],
        out_specs=(tile, tile, tile, lanes, pl.BlockSpec(memory_space=pltpu.VMEM)),
        scratch_shapes=[pltpu.VMEM((DN_HEADS, DN_HEAD_DIM, DN_HEAD_DIM), F32),
                        pltpu.VMEM(p_out.shape, BF16), pltpu.VMEM((N_DEV - 1,) + p_out.shape[1:], BF16),
                        pltpu.SemaphoreType.DMA((7,)), pltpu.SemaphoreType.DMA((7,))],
        compiler_params=_cp(("arbitrary",)),
    )(qn, kn, vv, gb, states, pre, uw, do_dn, p_out)


def _back(proj_main, proj_ba, dyp, dqn, dkn, dvv, dgb, ddz, x2, dh, norm_w, pool_w, pool_scale, conv_full,
          alog_lane, dtb_lane, wt_full, tm):
    s = x2.shape[0]
    nstep = s // tm
    per = tm // HALO

    def body(u_ref, z_ref, q_ref, k_ref, v_ref, up_ref, qp_ref, kp_ref, vp_ref, ba_ref,
             dyp_ref, dqn_ref, dkn_ref, dvv_ref, dgb_ref, ddz_ref, x_ref, dh_ref,
             nw_ref, pw_ref, ps_ref, cw_ref, al_ref, db_ref, wt_hbm,
             gx_ref, p_hbm, gnw_ref, dpw_ref, dps_ref, dcw_ref, dal_ref, ddb_ref,
             wt_vmem, acc, blk, head_dc, head_dw, sem, osem):
        j = pl.program_id(0)
        i = nstep - 1 - j

        @pl.when(j == 0)
        def _():
            cp = pltpu.make_async_copy(wt_hbm, wt_vmem, sem)
            cp.start()
            acc[...] = jnp.zeros_like(acc)
            for ref in (gnw_ref, dpw_ref, dps_ref, dcw_ref, dal_ref, ddb_ref, head_dc, head_dw):
                ref[...] = jnp.zeros_like(ref)
            cp.wait()

        u = u_ref[...]
        z = z_ref[...]
        dy = dyp_ref[...]
        ps = ps_ref[...]
        mixes = _pool_mix(u, _prev_tail(up_ref, i), i, tm)
        sg = jax.nn.sigmoid(z)
        sz = z * sg
        dsz = sg * (1.0 + z * (1.0 - sg))
        dzs, dwins = [], []
        for gi, w in enumerate(POOL_WINDOWS):
            cols = slice(gi * POOL_GROUP, (gi + 1) * POOL_GROUP)
            mixw = _dot_bf(mixes[gi], pw_ref[gi])
            dmixw = dy[:, cols] * ps[:, cols] * sz[:, cols]
            dps_ref[:, cols] += jnp.sum(dy[:, cols] * mixw * sz[:, cols], axis=0, keepdims=True)
            dzs.append(dy[:, cols] * mixw * ps[:, cols] * dsz[:, cols])
            dpw_ref[gi] += _dot_tn_bf(mixes[gi], dmixw)
            dwins.append(_dot_nt_bf(dmixw, pw_ref[gi]) / _pool_counts(i, tm, w))
        dzp = jnp.concatenate(dzs, axis=1)
        dw = jnp.concatenate(dwins, axis=1)
        win = jnp.concatenate([dw, head_dw[...]], axis=0)
        m = win.shape[0]
        dups = []
        for gi, w in enumerate(POOL_WINDOWS):
            win = win + pltpu.roll(win, m - w // 2, 0)
            cols = slice(gi * POOL_GROUP, (gi + 1) * POOL_GROUP)
            dups.append(win[:tm, :POOL_GROUP] - dw[:, cols] * _pool_counts(i, tm, w))
            if gi + 1 < len(POOL_WINDOWS):
                win = win[:, POOL_GROUP:]
        dup = jnp.concatenate(dups, axis=1)
        head_dw[...] = dw[:HALO, :]

        curs = (q_ref[...], k_ref[...], v_ref[...])
        tails = (_prev_tail(qp_ref, i), _prev_tail(kp_ref, i), _prev_tail(vp_ref, i))
        taps = [_conv_taps(curs[c], tails[c]) for c in range(3)]
        ys = [_conv_of_taps(taps[c], cw_ref[:, c * D_DN:(c + 1) * D_DN]) for c in range(3)]
        _, vjp = jax.vjp(_post_conv, *ys)
        dys = vjp((dqn_ref[...], dkn_ref[...], dvv_ref[...]))
        dxs = []
        for c, dyc in enumerate(dys):
            cols = slice(c * D_DN, (c + 1) * D_DN)
            w4 = cw_ref[:, cols]
            for sft in range(CONV_WIDTH):
                row = CONV_WIDTH - 1 - sft
                dcw_ref[row:row + 1, cols] += jnp.sum(dyc * taps[c][sft], axis=0, keepdims=True)
            head = head_dc[:, cols]
            dx = dyc * w4[CONV_WIDTH - 1:CONV_WIDTH, :]
            for sft in range(1, CONV_WIDTH):
                dx = dx + _shift_up(dyc, head, sft) * w4[CONV_WIDTH - 1 - sft:CONV_WIDTH - sft, :]
            dxs.append(dx)
            head_dc[:, cols] = dyc[:HALO, :]
        _, gvjp = jax.vjp(_gates, ba_ref[...], al_ref[...], db_ref[...])
        dba, dal, ddb = gvjp(dgb_ref[...])
        dal_ref[...] += dal
        ddb_ref[...] += ddb

        dbab = dba.astype(BF16)
        xhat, r = _rms_hat(x_ref[...])
        nw = nw_ref[...]
        n = (xhat * nw).astype(BF16)
        acc[D_MAIN:, :] += _dot_tn_bf(dbab, n)
        dn = jnp.dot(dbab, wt_vmem[D_MAIN:, :], preferred_element_type=F32)
        for cb, d in enumerate((dup, dzp, dxs[0], dxs[1], dxs[2], ddz_ref[...])):
            rows = slice(cb * D_POOL, (cb + 1) * D_POOL)
            dpart = d.astype(BF16)
            acc[rows, :] += _dot_tn_bf(dpart, n)
            dn = dn + jnp.dot(dpart, wt_vmem[rows, :], preferred_element_type=F32)
        gnw_ref[...] += jnp.sum(dn * xhat, axis=0, keepdims=True)
        dxh = dn * nw
        gx_ref[...] = dh_ref[...] + r * (dxh - xhat * jnp.mean(dxh * xhat, axis=-1, keepdims=True))

        @pl.when(j == nstep - 1)
        def _():
            def out(d):
                return pltpu.make_async_copy(blk.at[d % 2], p_hbm.at[d], osem.at[d % 2])
            for d in range(N_DEV):
                if d >= 2:
                    out(d - 2).wait()
                blk[d % 2] = acc[W_IN_SHARD * d:W_IN_SHARD * (d + 1), :]
                out(d).start()
            out(N_DEV - 2).wait()
            out(N_DEV - 1).wait()

    def col(c):
        return pl.BlockSpec((tm, D_POOL), lambda j: (nstep - 1 - j, c))

    def halo(c):
        return pl.BlockSpec((HALO, D_POOL), lambda j: (jnp.maximum((nstep - 1 - j) * per - 1, 0), c))

    rev = lambda j: (nstep - 1 - j, 0)
    part = pl.BlockSpec((tm, D_POOL), rev)
    lanes = pl.BlockSpec((tm, 128), rev)
    full = pl.BlockSpec((tm, D_MODEL), rev)
    row = pl.BlockSpec((1, D_MODEL), lambda j: (0, 0))
    lrow = pl.BlockSpec((1, 128), lambda j: (0, 0))
    pw = pl.BlockSpec((4, POOL_GROUP, POOL_GROUP), lambda j: (0, 0, 0))
    psp = pl.BlockSpec((1, D_POOL), lambda j: (0, 0))
    cw = pl.BlockSpec((CONV_WIDTH, 3 * D_DN), lambda j: (0, 0))
    return pl.pallas_call(
        body, name="back", grid=(nstep,),
        out_shape=(jax.ShapeDtypeStruct((s, D_MODEL), F32),
                   jax.ShapeDtypeStruct((N_DEV, W_IN_SHARD, D_MODEL), F32), jax.ShapeDtypeStruct((1, D_MODEL), F32),
                   jax.ShapeDtypeStruct((4, POOL_GROUP, POOL_GROUP), F32), jax.ShapeDtypeStruct((1, D_POOL), F32),
                   jax.ShapeDtypeStruct((CONV_WIDTH, 3 * D_DN), F32),
                   jax.ShapeDtypeStruct((1, 128), F32), jax.ShapeDtypeStruct((1, 128), F32)),
        in_specs=[col(0), col(1), col(2), col(3), col(4), halo(0), halo(2), halo(3), halo(4), lanes,
                  part, part, part, part, lanes, part, full, full,
                  row, pw, psp, cw, lrow, lrow, pl.BlockSpec(memory_space=pl.ANY)],
        out_specs=(full, pl.BlockSpec(memory_space=pl.ANY), row, pw, psp, cw, lrow, lrow),
        scratch_shapes=[pltpu.VMEM((D_IN_PAD, D_MODEL), BF16), pltpu.VMEM((D_IN_PAD, D_MODEL), F32),
                        pltpu.VMEM((2, W_IN_SHARD, D_MODEL), F32),
                        pltpu.VMEM((HALO, 3 * D_DN), F32), pltpu.VMEM((HALO, D_POOL), F32),
                        pltpu.SemaphoreType.DMA, pltpu.SemaphoreType.DMA((2,))],
        compiler_params=_cp(("arbitrary",)),
    )(proj_main, proj_main, proj_main, proj_main, proj_main, proj_main, proj_main, proj_main, proj_main, proj_ba,
      dyp, dqn, dkn, dvv, dgb, ddz, x2, dh, norm_w, pool_w, pool_scale, conv_full, alog_lane, dtb_lane, wt_full)


def _adamw_math(w, g, m, v):
    m = ADAM_B1 * m + (1.0 - ADAM_B1) * g
    v = ADAM_B2 * v + (1.0 - ADAM_B2) * (g * g)
    m_hat = m / (1.0 - ADAM_B1 ** ADAM_STEP)
    v_hat = v / (1.0 - ADAM_B2 ** ADAM_STEP)
    delta = -ADAM_LR * (m_hat / (jnp.sqrt(v_hat) + ADAM_EPS) + ADAM_WD * w)
    return delta, m, v


def _adamw(tiled, whole, gath_a, gath_b, pool, rows, lanes, nstep):
    kw, nrow = len(whole), len(rows)
    ins = list(tiled) + [a for p in whole for a in p] + [gath_a, gath_b] + list(pool) + [a for wmv in rows for a in wmv]

    def body(*refs):
        in_refs, outs = refs[:len(ins)], refs[len(ins):]

        def update(w, g, m, v, o):
            dl, nm, nv = _adamw_math(w, g, m, v)
            for ref, val in zip(o, (g, dl, nm, nv)):
                ref[...] = val

        update(*(r[...] for r in in_refs[:4]), outs[:4])

        @pl.when(pl.program_id(0) == 0)
        def _():
            for p in range(1, kw + 1):
                update(*(r[...] for r in in_refs[4 * p:4 * p + 4]), outs[4 * p:4 * p + 4])
            ga_ref, gb_ref = in_refs[4 * kw + 4:4 * kw + 6]
            rep_in, rep_out = in_refs[4 * kw + 6:], outs[4 * kw + 4:]

            def total(ref):
                g = ref[0]
                for d in range(1, N_DEV):
                    g = g + ref[d]
                return g

            gs = [total(ga_ref)]
            gb = total(gb_ref)
            for r in range(nrow):
                n = rep_in[3 * (r + 1)].shape[1]
                head = gb if lanes[r] == 0 else pltpu.roll(gb[:, :128], 128 - lanes[r], 1)
                gs.append(head[r:r + 1, :n])
            for r, g in enumerate(gs):
                w, m, v = (ref[...] for ref in rep_in[3 * r:3 * r + 3])
                update(w, g, m, v, rep_out[4 * r:4 * r + 4])
            rep_out[4 * (nrow + 1)][...] = gb[nrow:nrow + 1, 0:1]

    tile = pl.BlockSpec((tiled[0].shape[0] // nstep, tiled[0].shape[1]), lambda i: (i, 0))

    def full(a):
        return pl.BlockSpec(a.shape, lambda i, nd=a.ndim: (0,) * nd)

    firsts = [p[0] for p in whole] + [pool[0]] + [wmv[0] for wmv in rows]
    out_shape = ([jax.ShapeDtypeStruct(tiled[0].shape, F32)] * 4
                 + [jax.ShapeDtypeStruct(w.shape, F32) for w in firsts for _ in range(4)]
                 + [jax.ShapeDtypeStruct((1, 1), F32)])
    res = pl.pallas_call(
        body, name="adamw", grid=(nstep,),
        in_specs=[tile] * 4 + [full(a) for a in ins[4:]],
        out_specs=tuple([tile] * 4 + [full(o) for o in out_shape[4:]]),
        out_shape=tuple(out_shape),
        compiler_params=_cp(("arbitrary",)),
    )(*ins)
    groups = [tuple(res[4 * k:4 * k + 4]) for k in range(kw + nrow + 2)]
    return groups[0], groups[1:kw + 1], groups[kw + 1:], res[-1]


_ROW_ORDER = ("norm_w", "final_norm_w", "pool_scale", "dn_norm_w", "a_log", "dt_bias")
_ROW_LANE = (0, 0, 0, 0, DN_HEADS, DN_HEADS)


def _lane_row(vec4, start):
    return jnp.pad(vec4.reshape(-1), (start, 128 - start - vec4.size)).reshape(1, 128)


def kernel(x, norm_w, w_in, pool_w, pool_scale, conv_w, a_log, dt_bias, dn_norm_w, w_out, final_norm_w, loss_target, m_norm_w, m_w_in, m_pool_w, m_pool_scale, m_conv_w, m_a_log, m_dt_bias, m_dn_norm_w, m_w_out, m_final_norm_w, v_norm_w, v_w_in, v_pool_w, v_pool_scale, v_conv_w, v_a_log, v_dt_bias, v_dn_norm_w, v_w_out, v_final_norm_w):
    s = x.shape[1]
    tm = min(512, s)
    tmb = min(256, s)
    x2 = x[0]
    tgt = loss_target[0]
    def to_flat(a):
        return a[0].reshape(FLAT_ROWS, 128, W_IN_SHARD).transpose(2, 0, 1).reshape(W_IN_SHARD * FLAT_ROWS, 128)

    def from_flat(f):
        return f.reshape(W_IN_SHARD, FLAT_ROWS, 128).transpose(1, 2, 0).reshape(1, D_MODEL, W_IN_SHARD)

    wf, m_wf, v_wf = to_flat(w_in), to_flat(m_w_in), to_flat(v_w_in)

    g_in, g_conv, n_all = _gather_weights(wf, conv_w[0], x2, norm_w, tm)
    conv_full = g_conv.transpose(1, 0, 2).reshape(CONV_WIDTH, 3 * D_DN)
    alog_lane = _lane_row(a_log, DN_HEADS)
    dtb_lane = _lane_row(dt_bias, DN_HEADS)
    fnw = final_norm_w.reshape(1, D_MODEL)

    proj_main, proj_ba, y_pool, qn, kn, vv, gb, g_out, wt_full = _front(
        n_all, g_in, pool_w[0], pool_scale, conv_full, alog_lane, dtb_lane, w_out[0], tm)
    w_out_full = g_out.reshape(D_MODEL, D_MODEL)
    o_dn, states, dn_pre, dn_uw = _dn_scan_fwd(qn, kn, vv, gb, DN_CHUNKS_PER_STEP)

    dh, dyp, do_dn, ddz, g_wout, g_fnw, loss_part, g_dnw = _out_proj_loss(
        y_pool, o_dn, proj_main, dn_norm_w, x2, tgt, w_out_full, fnw, tm)
    p_out = g_wout.reshape(N_DEV, D_MODEL // N_DEV, D_MODEL)
    dqn, dkn, dvv, dgb, gr_out = _dn_scan_bwd(qn, kn, vv, gb, states, dn_pre, dn_uw, do_dn, p_out, DN_CHUNKS_PER_STEP)
    grad_x2, p_in, g_nw, g_pw, g_ps, g_conv_full, g_al, g_db = _back(
        proj_main, proj_ba, dyp, dqn, dkn, dvv, dgb, ddz, x2, dh, norm_w, pool_w[0], pool_scale, conv_full,
        alog_lane, dtb_lane, wt_full, tmb)

    p_conv = g_conv_full.reshape(CONV_WIDTH, N_DEV, 3 * D_DN // N_DEV).transpose(1, 0, 2)
    pack_a = g_pw.reshape(4 * POOL_GROUP, POOL_GROUP)
    gr_in, gr_conv, gath_a, gath_b = _reduce_grads((p_in, p_conv), (wf, conv_w[0]), pack_a,
                                                   [g_nw, g_fnw, g_ps, g_dnw, g_al, g_db, loss_part])

    flat = lambda a: a.reshape(4 * POOL_GROUP, POOL_GROUP)
    row = lambda a: a.reshape(1, -1)
    vecs = {"norm_w": (norm_w, m_norm_w, v_norm_w), "final_norm_w": (final_norm_w, m_final_norm_w, v_final_norm_w),
            "pool_scale": (pool_scale, m_pool_scale, v_pool_scale), "dn_norm_w": (dn_norm_w, m_dn_norm_w, v_dn_norm_w),
            "a_log": (a_log, m_a_log, v_a_log), "dt_bias": (dt_bias, m_dt_bias, v_dt_bias)}
    r_in, (r_out, r_conv), res, loss = _adamw(
        (wf, gr_in, m_wf, v_wf),
        [(w_out[0], gr_out, m_w_out[0], v_w_out[0]), (conv_w[0], gr_conv, m_conv_w[0], v_conv_w[0])],
        gath_a, gath_b, (flat(pool_w), flat(m_pool_w), flat(v_pool_w)),
        [tuple(row(a) for a in vecs[nm]) for nm in _ROW_ORDER], _ROW_LANE, ADAMW_STEPS)
    r_pool = res[0]
    r_vec = dict(zip(_ROW_ORDER, res[1:]))

    def group(k):
        vec = lambda nm: r_vec[nm][k].reshape(vecs[nm][0].shape)
        return (vec("norm_w"), from_flat(r_in[k]), r_pool[k].reshape(pool_w.shape), vec("pool_scale"), r_conv[k][None],
                vec("a_log"), vec("dt_bias"), vec("dn_norm_w"), r_out[k][None], vec("final_norm_w"))

    return (loss[0, 0], grad_x2[None], *group(0), *group(1), *group(2), *group(3))
```

```python
import functools

import jax
import jax.numpy as jnp
from jax import lax
from jax.experimental import pallas as pl
from jax.experimental.pallas import tpu as pltpu

F32 = jnp.float32
BF16 = jnp.bfloat16
HI = lax.Precision.HIGHEST
MESH = pl.DeviceIdType.MESH

D_MODEL = 1024
D_POOL = 512
D_DN = 512
POOL_WINDOWS = (2, 4, 8, 16)
POOL_GROUP = 128
DN_HEADS = 4
DN_HEAD_DIM = 128
CONV_WIDTH = 4
CHUNK = 64
NORM_EPS = 1e-6
D_IN = 3080
D_MAIN = 3072
FLAT_ROWS = D_MODEL // 128
D_IN_PAD = D_MAIN + 128
N_DEV = 8
W_IN_SHARD = D_IN // N_DEV
CONV_SHARD = 3 * D_DN // N_DEV
HALO = 16
DN_CHUNKS_PER_STEP = 8
ADAMW_STEPS = 5

ADAM_LR = 0.001
ADAM_B1 = 0.9
ADAM_B2 = 0.999
ADAM_EPS = 1e-08
ADAM_WD = 0.01
ADAM_STEP = 10

VMEM_LIMIT = 56 * 1024 * 1024
def _cp(sem=None, vmem=VMEM_LIMIT):
    kw = {"vmem_limit_bytes": vmem}
    if sem is not None:
        kw["dimension_semantics"] = sem
    return pltpu.CompilerParams(**kw)


def _dot_bf(a, b):
    return jnp.dot(a.astype(BF16), b.astype(BF16), preferred_element_type=F32)


def _dot_nt_bf(a, b):
    return lax.dot_general(a.astype(BF16), b.astype(BF16), (((1,), (1,)), ((), ())), preferred_element_type=F32)


def _dot_tn_bf(a, b):
    return lax.dot_general(a.astype(BF16), b.astype(BF16), (((0,), (0,)), ((), ())), preferred_element_type=F32)


def _mm_raw(a, b, ca, cb, prec):
    off = a.ndim - 2
    dn = (((ca + off,), (cb + off,)), ((0,), (0,)) if off else ((), ()))
    if prec == "hi":
        return lax.dot_general(a, b, dn, precision=HI, preferred_element_type=F32)
    ah, bh = a.astype(BF16), b.astype(BF16)
    out = lax.dot_general(ah, bh, dn, preferred_element_type=F32)
    if prec == "x3":
        al = (a - ah.astype(F32)).astype(BF16)
        bl = (b - bh.astype(F32)).astype(BF16)
        out = out + lax.dot_general(ah, bl, dn, preferred_element_type=F32)
        out = out + lax.dot_general(al, bh, dn, preferred_element_type=F32)
    return out


@functools.partial(jax.custom_vjp, nondiff_argnums=(2, 3, 4, 5))
def _mm(a, b, ca, cb, prec, bprec):
    return _mm_raw(a, b, ca, cb, prec)


def _mm_fwd(a, b, ca, cb, prec, bprec):
    return _mm_raw(a, b, ca, cb, prec), (a, b)


def _mm_bwd(ca, cb, prec, bprec, res, dc):
    a, b = res
    da = _mm_raw(dc, b, 1, 1 - cb, bprec) if ca == 1 else _mm_raw(b, dc, 1 - cb, 1, bprec)
    db = _mm_raw(a, dc, 1 - ca, 0, bprec) if cb == 0 else _mm_raw(dc, a, 0, 1 - ca, bprec)
    return da, db


_mm.defvjp(_mm_fwd, _mm_bwd)


@functools.partial(jax.custom_vjp, nondiff_argnums=(1, 2))
def _tri_inv(a, prec, bprec):
    n = a.shape[-1]
    ii = lax.broadcasted_iota(jnp.int32, (n, n), 0)
    jj = lax.broadcasted_iota(jnp.int32, (n, n), 1)
    p = (ii == jj).astype(F32) - a
    b = _mm_raw(a, a, 1, 0, prec)
    for _ in range(4):
        pb = _mm_raw(jnp.concatenate([p, b], axis=-2), b, 1, 0, prec)
        p = p + pb[..., :n, :]
        b = pb[..., n:, :]
    return p + _mm_raw(p, b, 1, 0, prec)


def _tri_inv_fwd(a, prec, bprec):
    t = _tri_inv(a, prec, bprec)
    return t, t


def _tri_inv_bwd(prec, bprec, t, dt):
    return (-_mm_raw(_mm_raw(t, dt, 0, 0, bprec), t, 1, 1, bprec),)


_tri_inv.defvjp(_tri_inv_fwd, _tri_inv_bwd)

@functools.partial(jax.custom_vjp, nondiff_argnums=(3, 4, 5))
def _mm_known(a, b, out, ca, cb, bprec):
    return out


def _mm_known_fwd(a, b, out, ca, cb, bprec):
    return out, (a, b)


def _mm_known_bwd(ca, cb, bprec, res, dc):
    return _mm_bwd(ca, cb, None, bprec, res, dc) + (jnp.zeros_like(dc),)


_mm_known.defvjp(_mm_known_fwd, _mm_known_bwd)


@jax.custom_vjp
def _use_known(x, known):
    return known


_use_known.defvjp(lambda x, known: (known, None), lambda _, g: (g, jnp.zeros_like(g)))


@functools.partial(jax.custom_vjp, nondiff_argnums=(2,))
def _tri_inv_known(a, t, bprec):
    return t


def _tri_inv_known_fwd(a, t, bprec):
    return t, t


def _tri_inv_known_bwd(bprec, t, dt):
    return _tri_inv_bwd(None, bprec, t, dt) + (jnp.zeros_like(dt),)


_tri_inv_known.defvjp(_tri_inv_known_fwd, _tri_inv_known_bwd)

_DN_PREC = {"akq": ("bf16", "bf16"), "inv": ("bf16", "bf16"), "uw": ("bf16", "bf16"), "ws": ("bf16", "bf16"),
            "ov": ("bf16", "bf16"), "st": ("bf16", "bf16")}


def _silu(x):
    return x * jax.nn.sigmoid(x)


def _softplus(x):
    pos = x > 0.0
    return jnp.where(pos, x, 0.0) + jnp.log1p(jnp.exp(jnp.where(pos, -x, x)))


def _mesh_pos():
    return lax.axis_index("x"), lax.axis_index("y"), lax.axis_index("c")


def _dev_index(x, y, c):
    return 4 * x + 2 * y + c


def _relay_order():
    x, y, c = _mesh_pos()
    n1 = (x + (1 - c) * (1 - 2 * x), y + c * (1 - 2 * y))
    n2 = (x + c * (1 - 2 * x), y + (1 - c) * (1 - 2 * y))
    return (x, y, c), (x, y, 1 - c), n1, n2, (1 - x, 1 - y)


def _all_gather_blocks(outs, send_sems, recv_sems, meanwhile=None, own=None):
    me, sibling, n1, n2, diag = _relay_order()
    c = me[2]

    def copy(a, k, block, to, src=None):
        rows = outs[a].at[_dev_index(*block)]
        return pltpu.make_async_remote_copy(src_ref=rows if src is None else src, dst_ref=rows,
                                            send_sem=send_sems.at[a, k], recv_sem=recv_sems.at[a, k],
                                            device_id=to, device_id_type=MESH)

    n = len(outs)
    started = []

    def start(cp):
        cp.start()
        started.append(cp)

    for a in range(n):
        mine = None if own is None else own[a]
        start(copy(a, 1, me, (*n1, c), mine))
        start(copy(a, 2, me, (*n2, c), mine))
        start(copy(a, 0, me, sibling, mine))
    if meanwhile is not None:
        meanwhile()
    for a in range(n):
        copy(a, 1, (*n1, c), me).wait_recv()
        start(copy(a, 3, (*n1, c), (*n2, c)))
        start(copy(a, 4, (*n1, c), sibling))
    for a in range(n):
        copy(a, 2, (*n2, c), me).wait_recv()
        start(copy(a, 5, (*n2, c), sibling))
        copy(a, 3, (*diag, c), me).wait_recv()
        start(copy(a, 6, (*diag, c), sibling))
    for a in range(n):
        copy(a, 0, sibling, me).wait_recv()
        copy(a, 4, (*n2, 1 - c), me).wait_recv()
        copy(a, 5, (*n1, 1 - c), me).wait_recv()
        copy(a, 6, (*diag, 1 - c), me).wait_recv()
    for cp in started:
        cp.wait_send()


def _peer_relations():
    x, y, c = _mesh_pos()
    flips = [(fx, fy, fc) for fx in (0, 1) for fy in (0, 1) for fc in (0, 1)][1:]
    peers = [(1 - x if fx else x, 1 - y if fy else y, 1 - c if fc else c) for fx, fy, fc in flips]
    return (x, y, c), peers


def _direct_gather_start(out_ref, send_sems, recv_sems):
    me, peers = _peer_relations()
    rows = out_ref.at[_dev_index(*me)]
    for k, peer in enumerate(peers):
        pltpu.make_async_remote_copy(src_ref=rows, dst_ref=rows, send_sem=send_sems.at[k], recv_sem=recv_sems.at[k],
                                     device_id=peer, device_id_type=MESH).start()


def _direct_gather_wait(out_ref, send_sems, recv_sems):
    me, peers = _peer_relations()
    for k, peer in enumerate(peers):
        rows = out_ref.at[_dev_index(*peer)]
        cp = pltpu.make_async_remote_copy(src_ref=rows, dst_ref=rows, send_sem=send_sems.at[k],
                                          recv_sem=recv_sems.at[k], device_id=peer, device_id_type=MESH)
        cp.wait_recv()
        cp.wait_send()


def _direct_scatter_start(send_ref, recv_ref, send_sems, recv_sems):
    me, peers = _peer_relations()
    for k, peer in enumerate(peers):
        pltpu.make_async_remote_copy(src_ref=send_ref.at[_dev_index(*peer)], dst_ref=recv_ref.at[k],
                                     send_sem=send_sems.at[k], recv_sem=recv_sems.at[k],
                                     device_id=peer, device_id_type=MESH).start()


def _direct_scatter_wait(send_ref, recv_ref, send_sems, recv_sems):
    me, peers = _peer_relations()
    for k, peer in enumerate(peers):
        cp = pltpu.make_async_remote_copy(src_ref=send_ref.at[_dev_index(*peer)], dst_ref=recv_ref.at[k],
                                          send_sem=send_sems.at[k], recv_sem=recv_sems.at[k],
                                          device_id=peer, device_id_type=MESH)
        cp.wait_recv()
        cp.wait_send()


def _gather_weights(w_in_flat, conv_blk, x2, norm_w, tn):
    nt = x2.shape[0] // tn

    def body(win_ref, conv_ref, x_hbm, nw_ref, gin_hbm, gconv_ref, n_hbm, own_ref, xbuf, nbuf, send_sems, recv_sems,
             xsem, nsem, own_sem):
        x, y, c = _mesh_pos()
        me = _dev_index(x, y, c)
        for j in range(FLAT_ROWS):
            own_ref[:, 128 * j:128 * (j + 1)] = win_ref[pl.ds(j, W_IN_SHARD, stride=FLAT_ROWS), :].astype(BF16)
        keep_own = pltpu.make_async_copy(own_ref, gin_hbm.at[me], own_sem.at[0])
        keep_own.start()
        gconv_ref[me] = conv_ref[...]

        def norm_x():
            def load(t):
                return pltpu.make_async_copy(x_hbm.at[pl.ds(t * tn, tn), :], xbuf.at[t % 2], xsem.at[t % 2])

            def store(t):
                return pltpu.make_async_copy(nbuf.at[t % 2], n_hbm.at[pl.ds(t * tn, tn), :], nsem.at[t % 2])

            load(0).start()
            for t in range(nt):
                if t + 1 < nt:
                    load(t + 1).start()
                load(t).wait()
                if t >= 2:
                    store(t - 2).wait()
                xhat, _ = _rms_hat(xbuf[t % 2])
                nbuf[t % 2] = (xhat * nw_ref[...]).astype(BF16)
                store(t).start()
            for t in range(max(nt - 2, 0), nt):
                store(t).wait()

        _all_gather_blocks((gin_hbm, gconv_ref), send_sems, recv_sems, meanwhile=norm_x, own=(own_ref, None))
        keep_own.wait()

    vm = pl.BlockSpec(memory_space=pltpu.VMEM)
    hbm = pl.BlockSpec(memory_space=pl.ANY)
    return pl.pallas_call(
        body, name="gather_weights",
        out_shape=(jax.ShapeDtypeStruct((N_DEV, W_IN_SHARD, D_MODEL), BF16),
                   jax.ShapeDtypeStruct((N_DEV,) + conv_blk.shape, F32),
                   jax.ShapeDtypeStruct(x2.shape, BF16)),
        in_specs=[vm, vm, hbm, vm], out_specs=(hbm, vm, hbm),
        scratch_shapes=[pltpu.VMEM((W_IN_SHARD, D_MODEL), BF16),
                        pltpu.VMEM((2, tn, D_MODEL), F32), pltpu.VMEM((2, tn, D_MODEL), BF16),
                        pltpu.SemaphoreType.DMA((2, 7)), pltpu.SemaphoreType.DMA((2, 7)),
                        pltpu.SemaphoreType.DMA((2,)), pltpu.SemaphoreType.DMA((2,)), pltpu.SemaphoreType.DMA((1,))],
        compiler_params=_cp(),
    )(w_in_flat, conv_blk, x2, norm_w)


def _reduce_grads(big, like, pack_a, rows):
    nb = len(big)

    def body(*refs):
        nr = len(rows)
        srcs, pa_ref, row_refs = refs[:nb], refs[nb], refs[nb + 1:nb + 1 + nr]
        outs, (ga_ref, gb_ref) = refs[nb + 1 + nr:2 * nb + 1 + nr], refs[2 * nb + 1 + nr:2 * nb + 3 + nr]
        scr = refs[2 * nb + 3 + nr:]
        r1s, r2s, sbs, sts = (scr[k * nb:(k + 1) * nb] for k in range(4))
        s1_send, s1_recv, s2_send, s2_recv, ag_send, ag_recv, st_sem = scr[4 * nb:]
        x, y, c = _mesh_pos()
        me = (x, y, c)
        sibling = (x, y, 1 - c)

        ga_ref[_dev_index(*me)] = pa_ref[...]
        gb_ref[_dev_index(*me)] = jnp.zeros(gb_ref.shape[1:], F32)
        for r, ref in enumerate(row_refs):
            gb_ref[_dev_index(*me), r:r + 1, :ref.shape[1]] = ref[...]

        _, _, n1, n2, diag = _relay_order()
        order = ((diag, 3), (n1, 1 + c), (n2, 2 - c), ((x, y), 0))

        def p1(a, i, to):
            chip, slot = order[i]
            return pltpu.make_async_remote_copy(
                src_ref=srcs[a].at[_dev_index(*chip, 1 - c)], dst_ref=r1s[a].at[slot],
                send_sem=s1_send.at[a, slot], recv_sem=s1_recv.at[a, slot], device_id=to, device_id_type=MESH)

        def p1_landed(a, slot):
            return pltpu.make_async_remote_copy(
                src_ref=r1s[a].at[slot], dst_ref=r1s[a].at[slot], send_sem=s1_send.at[a, slot],
                recv_sem=s1_recv.at[a, slot], device_id=me, device_id_type=MESH)

        def p2(a, k, to):
            return pltpu.make_async_remote_copy(
                src_ref=sbs[a].at[k], dst_ref=r2s[a].at[k],
                send_sem=s2_send.at[a, k], recv_sem=s2_recv.at[a, k], device_id=to, device_id_type=MESH)

        def stage(a, i):
            return pltpu.make_async_copy(srcs[a].at[_dev_index(*order[i][0], c)], sts[a].at[i % 2], st_sem.at[a, i % 2])

        sends = [p1(a, i, sibling) for a in range(nb) for i in (0, 2, 1, 3)]
        for cp in sends:
            cp.start()

        def step_1():
            for a in range(nb):
                stage(a, 0).start()
                for i, (_, slot) in enumerate(order):
                    if i + 1 < len(order):
                        stage(a, i + 1).start()
                    stage(a, i).wait()
                    p1_landed(a, slot).wait_recv()
                    chip_sum = r1s[a][slot] + sts[a][i % 2]
                    if i < 2:
                        sbs[a][i] = chip_sum.astype(BF16)
                        sends.append(p2(a, i, (*n1, c)))
                        sends[-1].start()
                    else:
                        r1s[a][slot] = chip_sum

        _all_gather_blocks((ga_ref, gb_ref), ag_send, ag_recv, meanwhile=step_1)
        for a in range(nb):
            p2(a, 0, me).wait_recv()
            sbs[a][2] = (r1s[a][2 - c] + r2s[a][0].astype(F32)).astype(BF16)
            sends.append(p2(a, 2, (*n2, c)))
            sends[-1].start()
        for a in range(nb):
            p2(a, 1, me).wait_recv()
            p2(a, 2, me).wait_recv()
            total = (r1s[a][0] + r2s[a][1].astype(F32)) + r2s[a][2].astype(F32)
            if outs[a].shape == total.shape:
                outs[a][...] = total
            else:
                r1s[a][0] = total
                for j in range(FLAT_ROWS):
                    outs[a][pl.ds(j, W_IN_SHARD, stride=FLAT_ROWS), :] = r1s[a][0, :, 128 * j:128 * (j + 1)]
        for cp in sends:
            cp.wait_send()

    vm = pl.BlockSpec(memory_space=pltpu.VMEM)
    hbm = pl.BlockSpec(memory_space=pl.ANY)
    blk = [p.shape[1:] for p in big]
    scratch = ([pltpu.VMEM((4,) + b, F32) for b in blk] + [pltpu.VMEM((3,) + b, BF16) for b in blk]
               + [pltpu.VMEM((3,) + b, BF16) for b in blk] + [pltpu.VMEM((2,) + b, F32) for b in blk]
               + [pltpu.SemaphoreType.DMA((nb, 4)), pltpu.SemaphoreType.DMA((nb, 4)),
                  pltpu.SemaphoreType.DMA((nb, 3)), pltpu.SemaphoreType.DMA((nb, 3)),
                  pltpu.SemaphoreType.DMA((2, 7)), pltpu.SemaphoreType.DMA((2, 7)),
                  pltpu.SemaphoreType.DMA((nb, 2))])
    return pl.pallas_call(
        body, name="reduce_grads",
        out_shape=tuple(jax.ShapeDtypeStruct(w.shape, F32) for w in like)
        + (jax.ShapeDtypeStruct((N_DEV,) + pack_a.shape, F32), jax.ShapeDtypeStruct((N_DEV, 8, D_MODEL), F32)),
        in_specs=[hbm] * nb + [vm] * (1 + len(rows)), out_specs=(vm,) * (nb + 2),
        scratch_shapes=scratch,
        compiler_params=_cp(),
    )(*big, pack_a, *rows)


def _rms_hat(xf):
    r = lax.rsqrt(jnp.mean(xf * xf, axis=-1, keepdims=True) + NORM_EPS)
    return xf * r, r


def _shift_up(cur, next_head, s):
    ext = jnp.concatenate([cur, next_head], axis=0)
    n = ext.shape[0]
    return pltpu.roll(ext, n - s, 0)[:cur.shape[0], :]


def _pool_counts(i, tp, w):
    t = i * tp + lax.broadcasted_iota(jnp.int32, (tp, 1), 0)
    return jnp.minimum(t + 1, w).astype(F32)


def _pool_mix(u, u_prev_tail, i, tp):
    win = jnp.concatenate([u_prev_tail, u], axis=0)
    mixes = []
    for gi, w in enumerate(POOL_WINDOWS):
        win = win + pltpu.roll(win, w // 2, 0)
        cols = slice(gi * POOL_GROUP, (gi + 1) * POOL_GROUP)
        mixes.append(win[HALO:, :POOL_GROUP] / _pool_counts(i, tp, w) - u[:, cols])
        if gi + 1 < len(POOL_WINDOWS):
            win = win[:, POOL_GROUP:]
    return mixes


def _prev_tail(ref, i):
    return jnp.where(i > 0, ref[...], 0.0)


def _conv_taps(cur, prev_tail):
    ext = jnp.concatenate([prev_tail, cur], axis=0)
    return [cur] + [pltpu.roll(ext, sft, 0)[HALO:, :] for sft in range(1, CONV_WIDTH)]


def _conv_of_taps(taps, w4):
    y = taps[0] * w4[CONV_WIDTH - 1:CONV_WIDTH, :]
    for sft in range(1, CONV_WIDTH):
        y = y + taps[sft] * w4[CONV_WIDTH - 1 - sft:CONV_WIDTH - sft, :]
    return y


def _conv_fwd(cur, prev_tail, w4):
    ext = jnp.concatenate([prev_tail, cur], axis=0)
    y = ext * w4[CONV_WIDTH - 1:CONV_WIDTH, :]
    for sft in range(1, CONV_WIDTH):
        y = y + pltpu.roll(ext, sft, 0) * w4[CONV_WIDTH - 1 - sft:CONV_WIDTH - sft, :]
    return y[HALO:, :]


def _l2n_heads(t):
    parts = []
    for h in range(DN_HEADS):
        th = t[:, h * DN_HEAD_DIM:(h + 1) * DN_HEAD_DIM]
        parts.append(th * lax.rsqrt(jnp.sum(th * th, axis=-1, keepdims=True) + NORM_EPS))
    return jnp.concatenate(parts, axis=1)


def _post_conv(yq, yk, yv):
    return _l2n_heads(_silu(yq)), _l2n_heads(_silu(yk)), _silu(yv)


def _gates(ba, alog_lane, dtb_lane):
    lane = lax.broadcasted_iota(jnp.int32, ba.shape, 1)
    beta = jax.nn.sigmoid(ba)
    g = -jnp.exp(alog_lane) * _softplus(ba + dtb_lane)
    return jnp.where(lane < DN_HEADS, beta, jnp.where(lane < 2 * DN_HEADS, g, 0.0))


def _front(n_all, g_in, pool_w, pool_scale, g_conv, alog_lane, dtb_lane, w_out_blk, tm):
    s = n_all.shape[0]

    def body(n_ref, pw_ref, ps_ref, gc_ref, al_ref, db_ref, wo_ref, g_hbm,
             pm_ref, pb_ref, yp_ref, qn_ref, kn_ref, vv_ref, gb_ref, gwo_hbm, wt_hbm,
             g_vmem, wt_vmem, tail_u, tail_qkv, gwo_ref, cw_ref, sem, g_sems, wo_send, wo_recv):
        i = pl.program_id(0)

        @pl.when(i == 0)
        def _():
            gwo_ref[_dev_index(*_mesh_pos())] = wo_ref[...].astype(BF16)
            _direct_gather_start(gwo_ref, wo_send, wo_recv)
            loads = [pltpu.make_async_copy(g_hbm.at[d], g_vmem.at[d], g_sems.at[d]) for d in range(N_DEV)]
            for cp in loads:
                cp.start()
            wt_vmem[D_MAIN:, :] = jnp.zeros((D_IN_PAD - D_MAIN, D_MODEL), BF16)
            tail_u[...] = jnp.zeros_like(tail_u)
            tail_qkv[...] = jnp.zeros_like(tail_qkv)
            for d in range(N_DEV):
                cw_ref[:, CONV_SHARD * d:CONV_SHARD * (d + 1)] = gc_ref[d]
            for d, cp in enumerate(loads):
                cp.wait()
                wt_vmem[W_IN_SHARD * d:W_IN_SHARD * (d + 1), :] = g_vmem[d]
            pltpu.make_async_copy(wt_vmem, wt_hbm, sem).start()
        n = n_ref[...]
        pm_ref[...] = _dot_nt_bf(n, wt_vmem[:D_MAIN, :])
        pb = _dot_nt_bf(n, wt_vmem[D_MAIN:, :])
        pb_ref[...] = pb
        u = pm_ref[:, :D_POOL]
        mixes = _pool_mix(u, tail_u[...], i, tm)
        tail_u[...] = u[tm - HALO:, :]
        gate = ps_ref[...] * _silu(pm_ref[:, D_POOL:2 * D_POOL])
        for gi in range(4):
            cols = slice(gi * POOL_GROUP, (gi + 1) * POOL_GROUP)
            yp_ref[:, cols] = _dot_bf(mixes[gi], pw_ref[gi]) * gate[:, cols]
        ys = []
        for c in range(3):
            cols = slice(c * D_DN, (c + 1) * D_DN)
            cur = pm_ref[:, 2 * D_POOL + c * D_DN:2 * D_POOL + (c + 1) * D_DN]
            ys.append(_conv_fwd(cur, tail_qkv[:, cols], cw_ref[:, cols]))
            tail_qkv[:, cols] = cur[tm - HALO:, :]
        qn, kn, vv = _post_conv(*ys)
        qn_ref[...] = qn
        kn_ref[...] = kn
        vv_ref[...] = vv
        gb_ref[...] = _gates(pb, al_ref[...], db_ref[...])

        @pl.when(i == s // tm - 1)
        def _():
            pltpu.make_async_copy(wt_vmem, wt_hbm, sem).wait()
            _direct_gather_wait(gwo_ref, wo_send, wo_recv)
            out = pltpu.make_async_copy(gwo_ref, gwo_hbm, sem)
            out.start()
            out.wait()

    tile = pl.BlockSpec((tm, D_DN), lambda i: (i, 0))
    lanes = pl.BlockSpec((tm, 128), lambda i: (i, 0))
    row = pl.BlockSpec((1, 128), lambda i: (0, 0))
    return pl.pallas_call(
        body, name="front", grid=(s // tm,),
        out_shape=(jax.ShapeDtypeStruct((s, D_MAIN), F32), jax.ShapeDtypeStruct((s, 128), F32),
                   jax.ShapeDtypeStruct((s, D_POOL), F32), jax.ShapeDtypeStruct((s, D_DN), F32),
                   jax.ShapeDtypeStruct((s, D_DN), F32), jax.ShapeDtypeStruct((s, D_DN), F32),
                   jax.ShapeDtypeStruct((s, 128), F32), jax.ShapeDtypeStruct((N_DEV,) + w_out_blk.shape, BF16),
                   jax.ShapeDtypeStruct((D_IN_PAD, D_MODEL), BF16)),
        in_specs=[pl.BlockSpec((tm, D_MODEL), lambda i: (i, 0)),
                  pl.BlockSpec((4, POOL_GROUP, POOL_GROUP), lambda i: (0, 0, 0)),
                  pl.BlockSpec((1, D_POOL), lambda i: (0, 0)),
                  pl.BlockSpec((N_DEV, CONV_WIDTH, CONV_SHARD), lambda i: (0, 0, 0)), row, row,
                  pl.BlockSpec(memory_space=pltpu.VMEM), pl.BlockSpec(memory_space=pl.ANY)],
        out_specs=(pl.BlockSpec((tm, D_MAIN), lambda i: (i, 0)), lanes, tile, tile, tile, tile, lanes,
                   pl.BlockSpec(memory_space=pl.ANY), pl.BlockSpec(memory_space=pl.ANY)),
        scratch_shapes=[pltpu.VMEM((N_DEV, W_IN_SHARD, D_MODEL), BF16), pltpu.VMEM((D_IN_PAD, D_MODEL), BF16),
                        pltpu.VMEM((HALO, D_POOL), F32), pltpu.VMEM((HALO, 3 * D_DN), F32),
                        pltpu.VMEM((N_DEV,) + w_out_blk.shape, BF16), pltpu.VMEM((CONV_WIDTH, 3 * D_DN), F32),
                        pltpu.SemaphoreType.DMA, pltpu.SemaphoreType.DMA((N_DEV,)),
                        pltpu.SemaphoreType.DMA((7,)), pltpu.SemaphoreType.DMA((7,))],
        compiler_params=_cp(("arbitrary",)),
    )(n_all, pool_w, pool_scale, g_conv, alog_lane, dtb_lane, w_out_blk, g_in)


def _dn_block(q, k, v, gcol, bcol, state, known=None):
    nb, n, d = q.shape
    ii = lax.broadcasted_iota(jnp.int32, (n, n), 0)
    jj = lax.broadcasted_iota(jnp.int32, (n, n), 1)
    lower = ii >= jj
    eye = (ii == jj).astype(F32)
    g_row = jnp.sum(eye * gcol, axis=1, keepdims=True)
    gc_col = jnp.sum(jnp.where(lower, g_row, 0.0), axis=2, keepdims=True)
    gc_row = jnp.sum(eye * gc_col, axis=1, keepdims=True)
    decay = jnp.where(lower, jnp.exp(jnp.where(lower, gc_col - gc_row, 0.0)), 0.0)
    kb = k * bcol
    vb = v * bcol
    qs = q * (DN_HEAD_DIM ** -0.5)
    egc = jnp.exp(gc_col)
    kq = jnp.concatenate([kb, qs], axis=1)
    vk = jnp.concatenate([vb, kb * egc], axis=2)
    if known is None:
        akq = _mm(kq, k, 1, 1, *_DN_PREC["akq"])
    else:
        akq = _mm_known(kq, k, known[0][:, :, :n].astype(F32), 1, 1, _DN_PREC["akq"][1])
    a = jnp.where(ii > jj, akq[:, :n] * decay, 0.0)
    qk = akq[:, n:] * decay
    if known is None:
        t = _tri_inv(a, *_DN_PREC["inv"])
        uw = _mm(t, vk, 1, 0, *_DN_PREC["uw"])
    else:
        t = _tri_inv_known(a, known[0][:, :n, n:].astype(F32), _DN_PREC["inv"][1])
        uw = _mm_known(t, vk, known[1], 1, 0, _DN_PREC["uw"][1])
    pre = jnp.concatenate([akq, jnp.concatenate([t, jnp.zeros_like(t)], axis=1)], axis=2)
    wq = jnp.concatenate([uw[:, :, d:], qs * egc], axis=1)
    g_last = gc_col[:, n - 1:n, :]
    k_dec = k * jnp.exp(g_last - gc_col)
    e_last = jnp.exp(g_last)
    os_, starts = [], []
    for c in range(nb // DN_HEADS):
        sl = slice(c * DN_HEADS, (c + 1) * DN_HEADS)
        if known is not None and c > 0:
            state = _use_known(state, known[2][sl])
        starts.append(state)
        ws = _mm(wq[sl], state, 1, 0, *_DN_PREC["ws"])
        v_new = uw[sl, :, :d] - ws[:, :n]
        os_.append(ws[:, n:] + _mm(qk[sl], v_new, 1, 0, *_DN_PREC["ov"]))
        state = state * e_last[sl] + _mm(k_dec[sl], v_new, 0, 0, *_DN_PREC["st"])
    return jnp.concatenate(os_, axis=0), state, (pre, uw, jnp.concatenate(starts, axis=0))


def _gated_norm(o, dz, nw):
    parts = []
    for h in range(DN_HEADS):
        oh = o[:, h * DN_HEAD_DIM:(h + 1) * DN_HEAD_DIM]
        parts.append(oh * lax.rsqrt(jnp.mean(oh * oh, axis=-1, keepdims=True) + NORM_EPS) * nw)
    return jnp.concatenate(parts, axis=1) * _silu(dz)


def _dn_block_args(gc, q_ref, k_ref, v_ref, gb_ref):
    qs, ks, vs, gs, bs = [], [], [], [], []
    for cc in range(gc):
        r = slice(cc * CHUNK, (cc + 1) * CHUNK)
        gbv = gb_ref[r, :]
        for h in range(DN_HEADS):
            cols = slice(h * DN_HEAD_DIM, (h + 1) * DN_HEAD_DIM)
            qs.append(q_ref[r, cols])
            ks.append(k_ref[r, cols])
            vs.append(v_ref[r, cols])
            gs.append(gbv[:, DN_HEADS + h:DN_HEADS + h + 1])
            bs.append(gbv[:, h:h + 1])
    return tuple(jnp.stack(t, axis=0) for t in (qs, ks, vs, gs, bs))


def _dn_scan_fwd(qn, kn, vv, gb, gc):
    s = qn.shape[0]
    nchunk = s // CHUNK
    rows = gc * CHUNK

    def body(q_ref, k_ref, v_ref, gb_ref, y_ref, ss_ref, pre_ref, uw_ref, state):
        @pl.when(pl.program_id(0) == 0)
        def _():
            state[...] = jnp.zeros_like(state)
        q, k, v, gcol, bcol = _dn_block_args(gc, q_ref, k_ref, v_ref, gb_ref)
        y, new, (pre, uw, starts) = _dn_block(q, k, v, gcol, bcol, state[...])
        state[...] = new
        ss_ref[...] = starts
        pre_ref[...] = pre
        uw_ref[...] = uw
        for cc in range(gc):
            for h in range(DN_HEADS):
                y_ref[cc * CHUNK:(cc + 1) * CHUNK, h * DN_HEAD_DIM:(h + 1) * DN_HEAD_DIM] = y[cc * DN_HEADS + h]

    tile = pl.BlockSpec((rows, D_DN), lambda i: (i, 0))
    return pl.pallas_call(
        body, name="dn_scan_fwd", grid=(nchunk // gc,),
        out_shape=(jax.ShapeDtypeStruct((s, D_DN), F32),
                   jax.ShapeDtypeStruct((nchunk * DN_HEADS, DN_HEAD_DIM, DN_HEAD_DIM), F32),
                   jax.ShapeDtypeStruct((nchunk * DN_HEADS, 2 * CHUNK, 2 * CHUNK), F32),
                   jax.ShapeDtypeStruct((nchunk * DN_HEADS, CHUNK, 2 * DN_HEAD_DIM), F32)),
        in_specs=[tile, tile, tile, pl.BlockSpec((rows, 128), lambda i: (i, 0))],
        out_specs=(tile, pl.BlockSpec((gc * DN_HEADS, DN_HEAD_DIM, DN_HEAD_DIM), lambda i: (i, 0, 0)),
                   pl.BlockSpec((gc * DN_HEADS, 2 * CHUNK, 2 * CHUNK), lambda i: (i, 0, 0)),
                   pl.BlockSpec((gc * DN_HEADS, CHUNK, 2 * DN_HEAD_DIM), lambda i: (i, 0, 0))),
        scratch_shapes=[pltpu.VMEM((DN_HEADS, DN_HEAD_DIM, DN_HEAD_DIM), F32)],
        compiler_params=_cp(("arbitrary",)),
    )(qn, kn, vv, gb)


def _out_proj_loss(y_pool, o_dn, proj_main, dn_norm_w, x2, tgt, w_out_full, fnw, tm):
    s = x2.shape[0]

    def body(yp_ref, o_ref, dz_ref, nw_ref, x_ref, t_ref, wo_ref, fw_ref,
             dh_ref, dyp_ref, do_ref, ddz_ref, gwo_ref, gfw_ref, loss_ref, dnw_ref):
        @pl.when(pl.program_id(0) == 0)
        def _():
            gwo_ref[...] = jnp.zeros_like(gwo_ref)
            gfw_ref[...] = jnp.zeros_like(gfw_ref)
            loss_ref[...] = jnp.zeros_like(loss_ref)
            dnw_ref[...] = jnp.zeros_like(dnw_ref)
        y_dn, gate_vjp = jax.vjp(_gated_norm, o_ref[...], dz_ref[...], nw_ref[...])
        y = jnp.concatenate([yp_ref[...], y_dn], axis=1).astype(BF16)
        wo = wo_ref[...]
        h = x_ref[...] + jnp.dot(y, wo, preferred_element_type=F32)
        hn, r = _rms_hat(h)
        fw = fw_ref[...]
        err = hn * fw - t_ref[...]
        loss_ref[...] += 0.5 * jnp.sum(jnp.sum(err * err, axis=-1, keepdims=True) / D_MODEL, axis=0, keepdims=True)
        dout = err / D_MODEL
        gfw_ref[...] += jnp.sum(dout * hn, axis=0, keepdims=True)
        dhn = dout * fw
        dh = r * (dhn - hn * jnp.mean(dhn * hn, axis=-1, keepdims=True))
        dh_ref[...] = dh
        dhb = dh.astype(BF16)
        dy = _dot_nt_bf(dhb, wo)
        dyp_ref[...] = dy[:, :D_POOL]
        do, ddz, dnw = gate_vjp(dy[:, D_POOL:])
        do_ref[...] = do
        ddz_ref[...] = ddz
        dnw_ref[...] += dnw
        gwo_ref[...] += _dot_tn_bf(y, dhb)

    half = pl.BlockSpec((tm, D_POOL), lambda i: (i, 0))
    full = pl.BlockSpec((tm, D_MODEL), lambda i: (i, 0))
    lrow = pl.BlockSpec((1, 128), lambda i: (0, 0))
    return pl.pallas_call(
        body, name="out_proj_loss", grid=(s // tm,),
        out_shape=(jax.ShapeDtypeStruct((s, D_MODEL), F32), jax.ShapeDtypeStruct((s, D_POOL), F32),
                   jax.ShapeDtypeStruct((s, D_DN), F32), jax.ShapeDtypeStruct((s, D_DN), F32),
                   jax.ShapeDtypeStruct((D_MODEL, D_MODEL), F32),
                   jax.ShapeDtypeStruct((1, D_MODEL), F32), jax.ShapeDtypeStruct((1, 128), F32),
                   jax.ShapeDtypeStruct((1, 128), F32)),
        in_specs=[half, half, pl.BlockSpec((tm, D_DN), lambda i: (i, 5)), lrow, full, full,
                  pl.BlockSpec((D_MODEL, D_MODEL), lambda i: (0, 0)), pl.BlockSpec((1, D_MODEL), lambda i: (0, 0))],
        out_specs=(full, half, half, half, pl.BlockSpec((D_MODEL, D_MODEL), lambda i: (0, 0)),
                   pl.BlockSpec((1, D_MODEL), lambda i: (0, 0)), lrow, lrow),
        compiler_params=_cp(("arbitrary",)),
    )(y_pool, o_dn, proj_main, dn_norm_w, x2, tgt, w_out_full, fnw)


def _dn_scan_bwd(qn, kn, vv, gb, states, pre, uw, do_dn, p_out, gc):
    s = qn.shape[0]
    nchunk = s // CHUNK
    nstep = nchunk // gc
    rows = gc * CHUNK

    def body(q_ref, k_ref, v_ref, gb_ref, ss_ref, pre_ref, uw_ref, dy_ref, po_ref,
             dq_ref, dk_ref, dv_ref, dgb_ref, gro_ref, dstate, po_send, po_recv, rs_send, rs_recv):
        @pl.when(pl.program_id(0) == 0)
        def _():
            dstate[...] = jnp.zeros_like(dstate)
            po_send[...] = po_ref[...].astype(BF16)
            _direct_scatter_start(po_send, po_recv, rs_send, rs_recv)

        @pl.when(pl.program_id(0) == nstep - 1)
        def _():
            _direct_scatter_wait(po_send, po_recv, rs_send, rs_recv)
            total = po_ref[_dev_index(*_mesh_pos())]
            for k in range(N_DEV - 1):
                total = total + po_recv[k].astype(F32)
            gro_ref[...] = total
        lane = lax.broadcasted_iota(jnp.int32, (CHUNK, 128), 1)
        q, k, v, gcol, bcol = _dn_block_args(gc, q_ref, k_ref, v_ref, gb_ref)
        dy = jnp.stack([dy_ref[cc * CHUNK:(cc + 1) * CHUNK, h * DN_HEAD_DIM:(h + 1) * DN_HEAD_DIM]
                        for cc in range(gc) for h in range(DN_HEADS)], axis=0)
        known = (pre_ref[...], uw_ref[...], ss_ref[...])
        _, vjp = jax.vjp(lambda *a: _dn_block(*a, known=known)[:2], q, k, v, gcol, bcol, ss_ref[:DN_HEADS])
        dq, dk, dv, dg, db, dst = vjp((dy, dstate[...]))
        dstate[...] = dst
        for cc in range(gc):
            r = slice(cc * CHUNK, (cc + 1) * CHUNK)
            dgb = jnp.zeros((CHUNK, 128), F32)
            for h in range(DN_HEADS):
                b = cc * DN_HEADS + h
                cols = slice(h * DN_HEAD_DIM, (h + 1) * DN_HEAD_DIM)
                for ref, val in zip((dq_ref, dk_ref, dv_ref), (dq, dk, dv)):
                    ref[r, cols] = val[b]
                dgb = dgb + jnp.where(lane == h, db[b], 0.0) + jnp.where(lane == DN_HEADS + h, dg[b], 0.0)
            dgb_ref[r, :] = dgb

    rev = lambda i: (nstep - 1 - i, 0)
    tile = pl.BlockSpec((rows, D_DN), rev)
    lanes = pl.BlockSpec((rows, 128), rev)
    return pl.pallas_call(
        body, name="dn_scan_bwd", grid=(nstep,),
        out_shape=(jax.ShapeDtypeStruct((s, D_DN), F32),) * 3
        + (jax.ShapeDtypeStruct((s, 128), F32), jax.ShapeDtypeStruct(p_out.shape[1:], F32)),
        in_specs=[tile, tile, tile, lanes,
                  pl.BlockSpec((gc * DN_HEADS, DN_HEAD_DIM, DN_HEAD_DIM), lambda i: (nstep - 1 - i, 0, 0)),
                  pl.BlockSpec((gc * DN_HEADS, 2 * CHUNK, 2 * CHUNK), lambda i: (nstep - 1 - i, 0, 0)),
                  pl.BlockSpec((gc * DN_HEADS, CHUNK, 2 * DN_HEAD_DIM), lambda i: (nstep - 1 - i, 0, 0)), tile,
                  pl.BlockSpec(memory_space=pltpu.VMEM)],
        out_specs=(tile, tile, tile, lanes, pl.BlockSpec(memory_space=pltpu.VMEM)),
        scratch_shapes=[pltpu.VMEM((DN_HEADS, DN_HEAD_DIM, DN_HEAD_DIM), F32),
                        pltpu.VMEM(p_out.shape, BF16), pltpu.VMEM((N_DEV - 1,) + p_out.shape[1:], BF16),
                        pltpu.SemaphoreType.DMA((7,)), pltpu.SemaphoreType.DMA((7,))],
        compiler_params=_cp(("arbitrary",)),
    )(qn, kn, vv, gb, states, pre, uw, do_dn, p_out)


def _back(proj_main, proj_ba, dyp, dqn, dkn, dvv, dgb, ddz, x2, dh, norm_w, pool_w, pool_scale, g_conv,
          alog_lane, dtb_lane, wt_full, tm):
    s = x2.shape[0]
    nstep = s // tm
    per = tm // HALO

    def body(u_ref, z_ref, q_ref, k_ref, v_ref, up_ref, qp_ref, kp_ref, vp_ref, ba_ref,
             dyp_ref, dqn_ref, dkn_ref, dvv_ref, dgb_ref, ddz_ref, x_ref, dh_ref,
             nw_ref, pw_ref, ps_ref, gc_ref, al_ref, db_ref, wt_hbm,
             gx_ref, p_hbm, gnw_ref, dpw_ref, dps_ref, pc_ref, dal_ref, ddb_ref,
             wt_vmem, acc, blk, head_dc, head_dw, cw_ref, dcw_ref, sem, osem):
        j = pl.program_id(0)
        i = nstep - 1 - j

        @pl.when(j == 0)
        def _():
            cp = pltpu.make_async_copy(wt_hbm, wt_vmem, sem)
            cp.start()
            acc[...] = jnp.zeros_like(acc)
            for ref in (gnw_ref, dpw_ref, dps_ref, dcw_ref, dal_ref, ddb_ref, head_dc, head_dw):
                ref[...] = jnp.zeros_like(ref)
            for d in range(N_DEV):
                cw_ref[:, CONV_SHARD * d:CONV_SHARD * (d + 1)] = gc_ref[d]
            cp.wait()

        u = u_ref[...]
        z = z_ref[...]
        dy = dyp_ref[...]
        ps = ps_ref[...]
        mixes = _pool_mix(u, _prev_tail(up_ref, i), i, tm)
        sg = jax.nn.sigmoid(z)
        sz = z * sg
        dsz = sg * (1.0 + z * (1.0 - sg))
        dzs, dwins = [], []
        for gi, w in enumerate(POOL_WINDOWS):
            cols = slice(gi * POOL_GROUP, (gi + 1) * POOL_GROUP)
            mixw = _dot_bf(mixes[gi], pw_ref[gi])
            dmixw = dy[:, cols] * ps[:, cols] * sz[:, cols]
            dps_ref[:, cols] += jnp.sum(dy[:, cols] * mixw * sz[:, cols], axis=0, keepdims=True)
            dzs.append(dy[:, cols] * mixw * ps[:, cols] * dsz[:, cols])
            dpw_ref[gi] += _dot_tn_bf(mixes[gi], dmixw)
            dwins.append(_dot_nt_bf(dmixw, pw_ref[gi]) / _pool_counts(i, tm, w))
        dzp = jnp.concatenate(dzs, axis=1)
        dw = jnp.concatenate(dwins, axis=1)
        win = jnp.concatenate([dw, head_dw[...]], axis=0)
        m = win.shape[0]
        dups = []
        for gi, w in enumerate(POOL_WINDOWS):
            win = win + pltpu.roll(win, m - w // 2, 0)
            cols = slice(gi * POOL_GROUP, (gi + 1) * POOL_GROUP)
            dups.append(win[:tm, :POOL_GROUP] - dw[:, cols] * _pool_counts(i, tm, w))
            if gi + 1 < len(POOL_WINDOWS):
                win = win[:, POOL_GROUP:]
        dup = jnp.concatenate(dups, axis=1)
        head_dw[...] = dw[:HALO, :]

        curs = (q_ref[...], k_ref[...], v_ref[...])
        tails = (_prev_tail(qp_ref, i), _prev_tail(kp_ref, i), _prev_tail(vp_ref, i))
        taps = [_conv_taps(curs[c], tails[c]) for c in range(3)]
        ys = [_conv_of_taps(taps[c], cw_ref[:, c * D_DN:(c + 1) * D_DN]) for c in range(3)]
        _, vjp = jax.vjp(_post_conv, *ys)
        dys = vjp((dqn_ref[...], dkn_ref[...], dvv_ref[...]))
        dxs = []
        for c, dyc in enumerate(dys):
            cols = slice(c * D_DN, (c + 1) * D_DN)
            w4 = cw_ref[:, cols]
            for sft in range(CONV_WIDTH):
                row = CONV_WIDTH - 1 - sft
                dcw_ref[row:row + 1, cols] += jnp.sum(dyc * taps[c][sft], axis=0, keepdims=True)
            head = head_dc[:, cols]
            dx = dyc * w4[CONV_WIDTH - 1:CONV_WIDTH, :]
            for sft in range(1, CONV_WIDTH):
                dx = dx + _shift_up(dyc, head, sft) * w4[CONV_WIDTH - 1 - sft:CONV_WIDTH - sft, :]
            dxs.append(dx)
            head_dc[:, cols] = dyc[:HALO, :]
        _, gvjp = jax.vjp(_gates, ba_ref[...], al_ref[...], db_ref[...])
        dba, dal, ddb = gvjp(dgb_ref[...])
        dal_ref[...] += dal
        ddb_ref[...] += ddb

        dbab = dba.astype(BF16)
        xhat, r = _rms_hat(x_ref[...])
        nw = nw_ref[...]
        n = (xhat * nw).astype(BF16)
        acc[D_MAIN:, :] += _dot_tn_bf(dbab, n)
        dn = jnp.dot(dbab, wt_vmem[D_MAIN:, :], preferred_element_type=F32)
        for cb, d in enumerate((dup, dzp, dxs[0], dxs[1], dxs[2], ddz_ref[...])):
            rows = slice(cb * D_POOL, (cb + 1) * D_POOL)
            dpart = d.astype(BF16)
            acc[rows, :] += _dot_tn_bf(dpart, n)
            dn = dn + jnp.dot(dpart, wt_vmem[rows, :], preferred_element_type=F32)
        gnw_ref[...] += jnp.sum(dn * xhat, axis=0, keepdims=True)
        dxh = dn * nw
        gx_ref[...] = dh_ref[...] + r * (dxh - xhat * jnp.mean(dxh * xhat, axis=-1, keepdims=True))

        @pl.when(j == nstep - 1)
        def _():
            for d in range(N_DEV):
                pc_ref[d] = dcw_ref[:, CONV_SHARD * d:CONV_SHARD * (d + 1)]

            def out(d):
                return pltpu.make_async_copy(blk.at[d % 2], p_hbm.at[d], osem.at[d % 2])
            for d in range(N_DEV):
                if d >= 2:
                    out(d - 2).wait()
                blk[d % 2] = acc[W_IN_SHARD * d:W_IN_SHARD * (d + 1), :]
                out(d).start()
            out(N_DEV - 2).wait()
            out(N_DEV - 1).wait()

    def col(c):
        return pl.BlockSpec((tm, D_POOL), lambda j: (nstep - 1 - j, c))

    def halo(c):
        return pl.BlockSpec((HALO, D_POOL), lambda j: (jnp.maximum((nstep - 1 - j) * per - 1, 0), c))

    rev = lambda j: (nstep - 1 - j, 0)
    part = pl.BlockSpec((tm, D_POOL), rev)
    lanes = pl.BlockSpec((tm, 128), rev)
    full = pl.BlockSpec((tm, D_MODEL), rev)
    row = pl.BlockSpec((1, D_MODEL), lambda j: (0, 0))
    lrow = pl.BlockSpec((1, 128), lambda j: (0, 0))
    pw = pl.BlockSpec((4, POOL_GROUP, POOL_GROUP), lambda j: (0, 0, 0))
    psp = pl.BlockSpec((1, D_POOL), lambda j: (0, 0))
    cw = pl.BlockSpec((N_DEV, CONV_WIDTH, CONV_SHARD), lambda j: (0, 0, 0))
    return pl.pallas_call(
        body, name="back", grid=(nstep,),
        out_shape=(jax.ShapeDtypeStruct((s, D_MODEL), F32),
                   jax.ShapeDtypeStruct((N_DEV, W_IN_SHARD, D_MODEL), F32), jax.ShapeDtypeStruct((1, D_MODEL), F32),
                   jax.ShapeDtypeStruct((4, POOL_GROUP, POOL_GROUP), F32), jax.ShapeDtypeStruct((1, D_POOL), F32),
                   jax.ShapeDtypeStruct((N_DEV, CONV_WIDTH, CONV_SHARD), F32),
                   jax.ShapeDtypeStruct((1, 128), F32), jax.ShapeDtypeStruct((1, 128), F32)),
        in_specs=[col(0), col(1), col(2), col(3), col(4), halo(0), halo(2), halo(3), halo(4), lanes,
                  part, part, part, part, lanes, part, full, full,
                  row, pw, psp, cw, lrow, lrow, pl.BlockSpec(memory_space=pl.ANY)],
        out_specs=(full, pl.BlockSpec(memory_space=pl.ANY), row, pw, psp, cw, lrow, lrow),
        scratch_shapes=[pltpu.VMEM((D_IN_PAD, D_MODEL), BF16), pltpu.VMEM((D_IN_PAD, D_MODEL), F32),
                        pltpu.VMEM((2, W_IN_SHARD, D_MODEL), F32),
                        pltpu.VMEM((HALO, 3 * D_DN), F32), pltpu.VMEM((HALO, D_POOL), F32),
                        pltpu.VMEM((CONV_WIDTH, 3 * D_DN), F32), pltpu.VMEM((CONV_WIDTH, 3 * D_DN), F32),
                        pltpu.SemaphoreType.DMA, pltpu.SemaphoreType.DMA((2,))],
        compiler_params=_cp(("arbitrary",)),
    )(proj_main, proj_main, proj_main, proj_main, proj_main, proj_main, proj_main, proj_main, proj_main, proj_ba,
      dyp, dqn, dkn, dvv, dgb, ddz, x2, dh, norm_w, pool_w, pool_scale, g_conv, alog_lane, dtb_lane, wt_full)


def _adamw_math(w, g, m, v):
    m = ADAM_B1 * m + (1.0 - ADAM_B1) * g
    v = ADAM_B2 * v + (1.0 - ADAM_B2) * (g * g)
    m_hat = m / (1.0 - ADAM_B1 ** ADAM_STEP)
    v_hat = v / (1.0 - ADAM_B2 ** ADAM_STEP)
    delta = -ADAM_LR * (m_hat / (jnp.sqrt(v_hat) + ADAM_EPS) + ADAM_WD * w)
    return delta, m, v


def _adamw(tiled, whole, gath_a, gath_b, pool, rows, lanes, nstep):
    kw, nrow = len(whole), len(rows)
    ins = list(tiled) + [a for p in whole for a in p] + [gath_a, gath_b] + list(pool) + [a for wmv in rows for a in wmv]

    def body(*refs):
        in_refs, outs = refs[:len(ins)], refs[len(ins):]

        def update(w, g, m, v, o):
            dl, nm, nv = _adamw_math(w, g, m, v)
            for ref, val in zip(o, (g, dl, nm, nv)):
                ref[...] = val

        update(*(r[...] for r in in_refs[:4]), outs[:4])

        @pl.when(pl.program_id(0) == 0)
        def _():
            for p in range(1, kw + 1):
                update(*(r[...] for r in in_refs[4 * p:4 * p + 4]), outs[4 * p:4 * p + 4])
            ga_ref, gb_ref = in_refs[4 * kw + 4:4 * kw + 6]
            rep_in, rep_out = in_refs[4 * kw + 6:], outs[4 * kw + 4:]

            def total(ref):
                g = ref[0]
                for d in range(1, N_DEV):
                    g = g + ref[d]
                return g

            gs = [total(ga_ref)]
            gb = total(gb_ref)
            for r in range(nrow):
                n = rep_in[3 * (r + 1)].shape[1]
                head = gb if lanes[r] == 0 else pltpu.roll(gb[:, :128], 128 - lanes[r], 1)
                gs.append(head[r:r + 1, :n])
            for r, g in enumerate(gs):
                w, m, v = (ref[...] for ref in rep_in[3 * r:3 * r + 3])
                update(w, g, m, v, rep_out[4 * r:4 * r + 4])
            rep_out[4 * (nrow + 1)][...] = gb[nrow:nrow + 1, 0:1]

    tile = pl.BlockSpec((tiled[0].shape[0] // nstep, tiled[0].shape[1]), lambda i: (i, 0))

    def full(a):
        return pl.BlockSpec(a.shape, lambda i, nd=a.ndim: (0,) * nd)

    firsts = [p[0] for p in whole] + [pool[0]] + [wmv[0] for wmv in rows]
    out_shape = ([jax.ShapeDtypeStruct(tiled[0].shape, F32)] * 4
                 + [jax.ShapeDtypeStruct(w.shape, F32) for w in firsts for _ in range(4)]
                 + [jax.ShapeDtypeStruct((1, 1), F32)])
    res = pl.pallas_call(
        body, name="adamw", grid=(nstep,),
        in_specs=[tile] * 4 + [full(a) for a in ins[4:]],
        out_specs=tuple([tile] * 4 + [full(o) for o in out_shape[4:]]),
        out_shape=tuple(out_shape),
        compiler_params=_cp(("arbitrary",)),
    )(*ins)
    groups = [tuple(res[4 * k:4 * k + 4]) for k in range(kw + nrow + 2)]
    return groups[0], groups[1:kw + 1], groups[kw + 1:], res[-1]


_ROW_ORDER = ("norm_w", "final_norm_w", "pool_scale", "dn_norm_w", "a_log", "dt_bias")
_ROW_LANE = (0, 0, 0, 0, DN_HEADS, DN_HEADS)


def _lane_row(vec4, start):
    return jnp.pad(vec4.reshape(-1), (start, 128 - start - vec4.size)).reshape(1, 128)


def kernel(x, norm_w, w_in, pool_w, pool_scale, conv_w, a_log, dt_bias, dn_norm_w, w_out, final_norm_w, loss_target, m_norm_w, m_w_in, m_pool_w, m_pool_scale, m_conv_w, m_a_log, m_dt_bias, m_dn_norm_w, m_w_out, m_final_norm_w, v_norm_w, v_w_in, v_pool_w, v_pool_scale, v_conv_w, v_a_log, v_dt_bias, v_dn_norm_w, v_w_out, v_final_norm_w):
    s = x.shape[1]
    tm = min(512, s)
    tmb = min(256, s)
    x2 = x[0]
    tgt = loss_target[0]
    def to_flat(a):
        return a[0].reshape(FLAT_ROWS, 128, W_IN_SHARD).transpose(2, 0, 1).reshape(W_IN_SHARD * FLAT_ROWS, 128)

    def from_flat(f):
        return f.reshape(W_IN_SHARD, FLAT_ROWS, 128).transpose(1, 2, 0).reshape(1, D_MODEL, W_IN_SHARD)

    wf, m_wf, v_wf = to_flat(w_in), to_flat(m_w_in), to_flat(v_w_in)

    g_in, g_conv, n_all = _gather_weights(wf, conv_w[0], x2, norm_w, tm)
    alog_lane = _lane_row(a_log, DN_HEADS)
    dtb_lane = _lane_row(dt_bias, DN_HEADS)
    fnw = final_norm_w.reshape(1, D_MODEL)

    proj_main, proj_ba, y_pool, qn, kn, vv, gb, g_out, wt_full = _front(
        n_all, g_in, pool_w[0], pool_scale, g_conv, alog_lane, dtb_lane, w_out[0], tm)
    w_out_full = g_out.reshape(D_MODEL, D_MODEL)
    o_dn, states, dn_pre, dn_uw = _dn_scan_fwd(qn, kn, vv, gb, DN_CHUNKS_PER_STEP)

    dh, dyp, do_dn, ddz, g_wout, g_fnw, loss_part, g_dnw = _out_proj_loss(
        y_pool, o_dn, proj_main, dn_norm_w, x2, tgt, w_out_full, fnw, tm)
    p_out = g_wout.reshape(N_DEV, D_MODEL // N_DEV, D_MODEL)
    dqn, dkn, dvv, dgb, gr_out = _dn_scan_bwd(qn, kn, vv, gb, states, dn_pre, dn_uw, do_dn, p_out, DN_CHUNKS_PER_STEP)
    grad_x2, p_in, g_nw, g_pw, g_ps, p_conv, g_al, g_db = _back(
        proj_main, proj_ba, dyp, dqn, dkn, dvv, dgb, ddz, x2, dh, norm_w, pool_w[0], pool_scale, g_conv,
        alog_lane, dtb_lane, wt_full, tmb)

    pack_a = g_pw.reshape(4 * POOL_GROUP, POOL_GROUP)
    gr_in, gr_conv, gath_a, gath_b = _reduce_grads((p_in, p_conv), (wf, conv_w[0]), pack_a,
                                                   [g_nw, g_fnw, g_ps, g_dnw, g_al, g_db, loss_part])

    flat = lambda a: a.reshape(4 * POOL_GROUP, POOL_GROUP)
    row = lambda a: a.reshape(1, -1)
    vecs = {"norm_w": (norm_w, m_norm_w, v_norm_w), "final_norm_w": (final_norm_w, m_final_norm_w, v_final_norm_w),
            "pool_scale": (pool_scale, m_pool_scale, v_pool_scale), "dn_norm_w": (dn_norm_w, m_dn_norm_w, v_dn_norm_w),
            "a_log": (a_log, m_a_log, v_a_log), "dt_bias": (dt_bias, m_dt_bias, v_dt_bias)}
    r_in, (r_out, r_conv), res, loss = _adamw(
        (wf, gr_in, m_wf, v_wf),
        [(w_out[0], gr_out, m_w_out[0], v_w_out[0]), (conv_w[0], gr_conv, m_conv_w[0], v_conv_w[0])],
        gath_a, gath_b, (flat(pool_w), flat(m_pool_w), flat(v_pool_w)),
        [tuple(row(a) for a in vecs[nm]) for nm in _ROW_ORDER], _ROW_LANE, ADAMW_STEPS)
    r_pool = res[0]
    r_vec = dict(zip(_ROW_ORDER, res[1:]))

    def group(k):
        vec = lambda nm: r_vec[nm][k].reshape(vecs[nm][0].shape)
        return (vec("norm_w"), from_flat(r_in[k]), r_pool[k].reshape(pool_w.shape), vec("pool_scale"), r_conv[k][None],
                vec("a_log"), vec("dt_bias"), vec("dn_norm_w"), r_out[k][None], vec("final_norm_w"))

    return (loss[0, 0], grad_x2[None], *group(0), *group(1), *group(2), *group(3))
```

```python
import functools

import jax
import jax.numpy as jnp
from jax import lax
from jax.experimental import pallas as pl
from jax.experimental.pallas import tpu as pltpu

F32 = jnp.float32
BF16 = jnp.bfloat16
HI = lax.Precision.HIGHEST
MESH = pl.DeviceIdType.MESH

D_MODEL = 1024
D_POOL = 512
D_DN = 512
POOL_WINDOWS = (2, 4, 8, 16)
POOL_GROUP = 128
DN_HEADS = 4
DN_HEAD_DIM = 128
CONV_WIDTH = 4
CHUNK = 64
NORM_EPS = 1e-6
D_IN = 3080
D_MAIN = 3072
FLAT_ROWS = D_MODEL // 128
D_IN_PAD = D_MAIN + 128
N_DEV = 8
W_IN_SHARD = D_IN // N_DEV
CONV_SHARD = 3 * D_DN // N_DEV
HALO = 16
DN_CHUNKS_PER_STEP = 8
ADAMW_STEPS = 5

ADAM_LR = 0.001
ADAM_B1 = 0.9
ADAM_B2 = 0.999
ADAM_EPS = 1e-08
ADAM_WD = 0.01
ADAM_STEP = 10

VMEM_LIMIT = 56 * 1024 * 1024
def _cp(sem=None, vmem=VMEM_LIMIT):
    kw = {"vmem_limit_bytes": vmem}
    if sem is not None:
        kw["dimension_semantics"] = sem
    return pltpu.CompilerParams(**kw)


def _dot_bf(a, b):
    return jnp.dot(a.astype(BF16), b.astype(BF16), preferred_element_type=F32)


def _dot_nt_bf(a, b):
    return lax.dot_general(a.astype(BF16), b.astype(BF16), (((1,), (1,)), ((), ())), preferred_element_type=F32)


def _dot_tn_bf(a, b):
    return lax.dot_general(a.astype(BF16), b.astype(BF16), (((0,), (0,)), ((), ())), preferred_element_type=F32)


def _mm_raw(a, b, ca, cb, prec):
    off = a.ndim - 2
    dn = (((ca + off,), (cb + off,)), ((0,), (0,)) if off else ((), ()))
    if prec == "hi":
        return lax.dot_general(a, b, dn, precision=HI, preferred_element_type=F32)
    ah, bh = a.astype(BF16), b.astype(BF16)
    out = lax.dot_general(ah, bh, dn, preferred_element_type=F32)
    if prec == "x3":
        al = (a - ah.astype(F32)).astype(BF16)
        bl = (b - bh.astype(F32)).astype(BF16)
        out = out + lax.dot_general(ah, bl, dn, preferred_element_type=F32)
        out = out + lax.dot_general(al, bh, dn, preferred_element_type=F32)
    return out


@functools.partial(jax.custom_vjp, nondiff_argnums=(2, 3, 4, 5))
def _mm(a, b, ca, cb, prec, bprec):
    return _mm_raw(a, b, ca, cb, prec)


def _mm_fwd(a, b, ca, cb, prec, bprec):
    return _mm_raw(a, b, ca, cb, prec), (a, b)


def _mm_bwd(ca, cb, prec, bprec, res, dc):
    a, b = res
    da = _mm_raw(dc, b, 1, 1 - cb, bprec) if ca == 1 else _mm_raw(b, dc, 1 - cb, 1, bprec)
    db = _mm_raw(a, dc, 1 - ca, 0, bprec) if cb == 0 else _mm_raw(dc, a, 0, 1 - ca, bprec)
    return da, db


_mm.defvjp(_mm_fwd, _mm_bwd)


@functools.partial(jax.custom_vjp, nondiff_argnums=(1, 2))
def _tri_inv(a, prec, bprec):
    n = a.shape[-1]
    ii = lax.broadcasted_iota(jnp.int32, (n, n), 0)
    jj = lax.broadcasted_iota(jnp.int32, (n, n), 1)
    p = (ii == jj).astype(F32) - a
    b = _mm_raw(a, a, 1, 0, prec)
    for _ in range(4):
        pb = _mm_raw(jnp.concatenate([p, b], axis=-2), b, 1, 0, prec)
        p = p + pb[..., :n, :]
        b = pb[..., n:, :]
    return p + _mm_raw(p, b, 1, 0, prec)


def _tri_inv_fwd(a, prec, bprec):
    t = _tri_inv(a, prec, bprec)
    return t, t


def _tri_inv_bwd(prec, bprec, t, dt):
    return (-_mm_raw(_mm_raw(t, dt, 0, 0, bprec), t, 1, 1, bprec),)


_tri_inv.defvjp(_tri_inv_fwd, _tri_inv_bwd)

@functools.partial(jax.custom_vjp, nondiff_argnums=(3, 4, 5))
def _mm_known(a, b, out, ca, cb, bprec):
    return out


def _mm_known_fwd(a, b, out, ca, cb, bprec):
    return out, (a, b)


def _mm_known_bwd(ca, cb, bprec, res, dc):
    return _mm_bwd(ca, cb, None, bprec, res, dc) + (jnp.zeros_like(dc),)


_mm_known.defvjp(_mm_known_fwd, _mm_known_bwd)


@jax.custom_vjp
def _use_known(x, known):
    return known


_use_known.defvjp(lambda x, known: (known, None), lambda _, g: (g, jnp.zeros_like(g)))


@functools.partial(jax.custom_vjp, nondiff_argnums=(2,))
def _tri_inv_known(a, t, bprec):
    return t


def _tri_inv_known_fwd(a, t, bprec):
    return t, t


def _tri_inv_known_bwd(bprec, t, dt):
    return _tri_inv_bwd(None, bprec, t, dt) + (jnp.zeros_like(dt),)


_tri_inv_known.defvjp(_tri_inv_known_fwd, _tri_inv_known_bwd)

_DN_PREC = {"akq": ("bf16", "bf16"), "inv": ("bf16", "bf16"), "uw": ("bf16", "bf16"), "ws": ("bf16", "bf16"),
            "ov": ("bf16", "bf16"), "st": ("bf16", "bf16")}


def _silu(x):
    return x * jax.nn.sigmoid(x)


def _softplus(x):
    pos = x > 0.0
    return jnp.where(pos, x, 0.0) + jnp.log1p(jnp.exp(jnp.where(pos, -x, x)))


def _mesh_pos():
    return lax.axis_index("x"), lax.axis_index("y"), lax.axis_index("c")


def _dev_index(x, y, c):
    return 4 * x + 2 * y + c


def _relay_order():
    x, y, c = _mesh_pos()
    n1 = (x + (1 - c) * (1 - 2 * x), y + c * (1 - 2 * y))
    n2 = (x + c * (1 - 2 * x), y + (1 - c) * (1 - 2 * y))
    return (x, y, c), (x, y, 1 - c), n1, n2, (1 - x, 1 - y)


def _all_gather_blocks(outs, send_sems, recv_sems, meanwhile=None, own=None):
    me, sibling, n1, n2, diag = _relay_order()
    c = me[2]

    def copy(a, k, block, to, src=None):
        rows = outs[a].at[_dev_index(*block)]
        return pltpu.make_async_remote_copy(src_ref=rows if src is None else src, dst_ref=rows,
                                            send_sem=send_sems.at[a, k], recv_sem=recv_sems.at[a, k],
                                            device_id=to, device_id_type=MESH)

    n = len(outs)
    started = []

    def start(cp):
        cp.start()
        started.append(cp)

    for a in range(n):
        mine = None if own is None else own[a]
        start(copy(a, 1, me, (*n1, c), mine))
        start(copy(a, 2, me, (*n2, c), mine))
        start(copy(a, 0, me, sibling, mine))
    if meanwhile is not None:
        meanwhile()
    for a in range(n):
        copy(a, 1, (*n1, c), me).wait_recv()
        start(copy(a, 3, (*n1, c), (*n2, c)))
        start(copy(a, 4, (*n1, c), sibling))
    for a in range(n):
        copy(a, 2, (*n2, c), me).wait_recv()
        start(copy(a, 5, (*n2, c), sibling))
        copy(a, 3, (*diag, c), me).wait_recv()
        start(copy(a, 6, (*diag, c), sibling))
    for a in range(n):
        copy(a, 0, sibling, me).wait_recv()
        copy(a, 4, (*n2, 1 - c), me).wait_recv()
        copy(a, 5, (*n1, 1 - c), me).wait_recv()
        copy(a, 6, (*diag, 1 - c), me).wait_recv()
    for cp in started:
        cp.wait_send()


def _peer_relations():
    x, y, c = _mesh_pos()
    flips = [(fx, fy, fc) for fx in (0, 1) for fy in (0, 1) for fc in (0, 1)][1:]
    peers = [(1 - x if fx else x, 1 - y if fy else y, 1 - c if fc else c) for fx, fy, fc in flips]
    return (x, y, c), peers


def _direct_gather_start(out_ref, send_sems, recv_sems):
    me, peers = _peer_relations()
    rows = out_ref.at[_dev_index(*me)]
    for k, peer in enumerate(peers):
        pltpu.make_async_remote_copy(src_ref=rows, dst_ref=rows, send_sem=send_sems.at[k], recv_sem=recv_sems.at[k],
                                     device_id=peer, device_id_type=MESH).start()


def _direct_gather_wait(out_ref, send_sems, recv_sems):
    me, peers = _peer_relations()
    for k, peer in enumerate(peers):
        rows = out_ref.at[_dev_index(*peer)]
        cp = pltpu.make_async_remote_copy(src_ref=rows, dst_ref=rows, send_sem=send_sems.at[k],
                                          recv_sem=recv_sems.at[k], device_id=peer, device_id_type=MESH)
        cp.wait_recv()
        cp.wait_send()


def _direct_scatter_start(send_ref, recv_ref, send_sems, recv_sems):
    me, peers = _peer_relations()
    for k, peer in enumerate(peers):
        pltpu.make_async_remote_copy(src_ref=send_ref.at[_dev_index(*peer)], dst_ref=recv_ref.at[k],
                                     send_sem=send_sems.at[k], recv_sem=recv_sems.at[k],
                                     device_id=peer, device_id_type=MESH).start()


def _direct_scatter_wait(send_ref, recv_ref, send_sems, recv_sems):
    me, peers = _peer_relations()
    for k, peer in enumerate(peers):
        cp = pltpu.make_async_remote_copy(src_ref=send_ref.at[_dev_index(*peer)], dst_ref=recv_ref.at[k],
                                          send_sem=send_sems.at[k], recv_sem=recv_sems.at[k],
                                          device_id=peer, device_id_type=MESH)
        cp.wait_recv()
        cp.wait_send()


def _gather_weights(w_in_flat, conv_blk, x2, norm_w, tn):
    nt = x2.shape[0] // tn

    def body(win_ref, conv_ref, x_hbm, nw_ref, gin_hbm, gconv_ref, n_hbm, own_ref, xbuf, nbuf, send_sems, recv_sems,
             xsem, nsem, own_sem):
        x, y, c = _mesh_pos()
        me = _dev_index(x, y, c)
        for j in range(FLAT_ROWS):
            own_ref[:, 128 * j:128 * (j + 1)] = win_ref[pl.ds(j, W_IN_SHARD, stride=FLAT_ROWS), :].astype(BF16)
        keep_own = pltpu.make_async_copy(own_ref, gin_hbm.at[me], own_sem.at[0])
        keep_own.start()
        gconv_ref[me] = conv_ref[...]

        def norm_x():
            def load(t):
                return pltpu.make_async_copy(x_hbm.at[pl.ds(t * tn, tn), :], xbuf.at[t % 2], xsem.at[t % 2])

            def store(t):
                return pltpu.make_async_copy(nbuf.at[t % 2], n_hbm.at[pl.ds(t * tn, tn), :], nsem.at[t % 2])

            load(0).start()
            for t in range(nt):
                if t + 1 < nt:
                    load(t + 1).start()
                load(t).wait()
                if t >= 2:
                    store(t - 2).wait()
                xhat, _ = _rms_hat(xbuf[t % 2])
                nbuf[t % 2] = (xhat * nw_ref[...]).astype(BF16)
                store(t).start()
            for t in range(max(nt - 2, 0), nt):
                store(t).wait()

        _all_gather_blocks((gin_hbm, gconv_ref), send_sems, recv_sems, meanwhile=norm_x, own=(own_ref, None))
        keep_own.wait()

    vm = pl.BlockSpec(memory_space=pltpu.VMEM)
    hbm = pl.BlockSpec(memory_space=pl.ANY)
    return pl.pallas_call(
        body, name="gather_weights",
        out_shape=(jax.ShapeDtypeStruct((N_DEV, W_IN_SHARD, D_MODEL), BF16),
                   jax.ShapeDtypeStruct((N_DEV,) + conv_blk.shape, F32),
                   jax.ShapeDtypeStruct(x2.shape, BF16)),
        in_specs=[vm, vm, hbm, vm], out_specs=(hbm, vm, hbm),
        scratch_shapes=[pltpu.VMEM((W_IN_SHARD, D_MODEL), BF16),
                        pltpu.VMEM((2, tn, D_MODEL), F32), pltpu.VMEM((2, tn, D_MODEL), BF16),
                        pltpu.SemaphoreType.DMA((2, 7)), pltpu.SemaphoreType.DMA((2, 7)),
                        pltpu.SemaphoreType.DMA((2,)), pltpu.SemaphoreType.DMA((2,)), pltpu.SemaphoreType.DMA((1,))],
        compiler_params=_cp(),
    )(w_in_flat, conv_blk, x2, norm_w)


def _reduce_grads(big, like, pack_a, rows):
    nb = len(big)

    def body(*refs):
        nr = len(rows)
        srcs, pa_ref, row_refs = refs[:nb], refs[nb], refs[nb + 1:nb + 1 + nr]
        outs, (ga_ref, gb_ref) = refs[nb + 1 + nr:2 * nb + 1 + nr], refs[2 * nb + 1 + nr:2 * nb + 3 + nr]
        scr = refs[2 * nb + 3 + nr:]
        r1s, r2s, sbs, sts = (scr[k * nb:(k + 1) * nb] for k in range(4))
        s1_send, s1_recv, s2_send, s2_recv, ag_send, ag_recv, st_sem = scr[4 * nb:]
        x, y, c = _mesh_pos()
        me = (x, y, c)
        sibling = (x, y, 1 - c)

        ga_ref[_dev_index(*me)] = pa_ref[...]
        gb_ref[_dev_index(*me)] = jnp.zeros(gb_ref.shape[1:], F32)
        for r, ref in enumerate(row_refs):
            gb_ref[_dev_index(*me), r:r + 1, :ref.shape[1]] = ref[...]

        _, _, n1, n2, diag = _relay_order()
        order = ((diag, 3), (n1, 1 + c), (n2, 2 - c), ((x, y), 0))

        def p1(a, i, to):
            chip, slot = order[i]
            return pltpu.make_async_remote_copy(
                src_ref=srcs[a].at[_dev_index(*chip, 1 - c)], dst_ref=r1s[a].at[slot],
                send_sem=s1_send.at[a, slot], recv_sem=s1_recv.at[a, slot], device_id=to, device_id_type=MESH)

        def p1_landed(a, slot):
            return pltpu.make_async_remote_copy(
                src_ref=r1s[a].at[slot], dst_ref=r1s[a].at[slot], send_sem=s1_send.at[a, slot],
                recv_sem=s1_recv.at[a, slot], device_id=me, device_id_type=MESH)

        def p2(a, k, to):
            return pltpu.make_async_remote_copy(
                src_ref=sbs[a].at[k], dst_ref=r2s[a].at[k],
                send_sem=s2_send.at[a, k], recv_sem=s2_recv.at[a, k], device_id=to, device_id_type=MESH)

        def stage(a, i):
            return pltpu.make_async_copy(srcs[a].at[_dev_index(*order[i][0], c)], sts[a].at[i % 2], st_sem.at[a, i % 2])

        sends = [p1(a, i, sibling) for a in range(nb) for i in (0, 2, 1, 3)]
        for cp in sends:
            cp.start()

        def step_1():
            for a in range(nb):
                stage(a, 0).start()
                for i, (_, slot) in enumerate(order):
                    if i + 1 < len(order):
                        stage(a, i + 1).start()
                    stage(a, i).wait()
                    p1_landed(a, slot).wait_recv()
                    chip_sum = r1s[a][slot] + sts[a][i % 2]
                    if i < 2:
                        sbs[a][i] = chip_sum.astype(BF16)
                        sends.append(p2(a, i, (*n1, c)))
                        sends[-1].start()
                    else:
                        r1s[a][slot] = chip_sum

        _all_gather_blocks((ga_ref, gb_ref), ag_send, ag_recv, meanwhile=step_1)
        for a in range(nb):
            p2(a, 0, me).wait_recv()
            sbs[a][2] = (r1s[a][2 - c] + r2s[a][0].astype(F32)).astype(BF16)
            sends.append(p2(a, 2, (*n2, c)))
            sends[-1].start()
        for a in range(nb):
            p2(a, 1, me).wait_recv()
            p2(a, 2, me).wait_recv()
            total = (r1s[a][0] + r2s[a][1].astype(F32)) + r2s[a][2].astype(F32)
            if outs[a].shape == total.shape:
                outs[a][...] = total
            else:
                r1s[a][0] = total
                for j in range(FLAT_ROWS):
                    outs[a][pl.ds(j, W_IN_SHARD, stride=FLAT_ROWS), :] = r1s[a][0, :, 128 * j:128 * (j + 1)]
        for cp in sends:
            cp.wait_send()

    vm = pl.BlockSpec(memory_space=pltpu.VMEM)
    hbm = pl.BlockSpec(memory_space=pl.ANY)
    blk = [p.shape[1:] for p in big]
    scratch = ([pltpu.VMEM((4,) + b, F32) for b in blk] + [pltpu.VMEM((3,) + b, BF16) for b in blk]
               + [pltpu.VMEM((3,) + b, BF16) for b in blk] + [pltpu.VMEM((2,) + b, F32) for b in blk]
               + [pltpu.SemaphoreType.DMA((nb, 4)), pltpu.SemaphoreType.DMA((nb, 4)),
                  pltpu.SemaphoreType.DMA((nb, 3)), pltpu.SemaphoreType.DMA((nb, 3)),
                  pltpu.SemaphoreType.DMA((2, 7)), pltpu.SemaphoreType.DMA((2, 7)),
                  pltpu.SemaphoreType.DMA((nb, 2))])
    return pl.pallas_call(
        body, name="reduce_grads",
        out_shape=tuple(jax.ShapeDtypeStruct(w.shape, F32) for w in like)
        + (jax.ShapeDtypeStruct((N_DEV,) + pack_a.shape, F32), jax.ShapeDtypeStruct((N_DEV, 8, D_MODEL), F32)),
        in_specs=[hbm] * nb + [vm] * (1 + len(rows)), out_specs=(vm,) * (nb + 2),
        scratch_shapes=scratch,
        compiler_params=_cp(),
    )(*big, pack_a, *rows)


def _rms_hat(xf):
    r = lax.rsqrt(jnp.mean(xf * xf, axis=-1, keepdims=True) + NORM_EPS)
    return xf * r, r


def _shift_up(cur, next_head, s):
    ext = jnp.concatenate([cur, next_head], axis=0)
    n = ext.shape[0]
    return pltpu.roll(ext, n - s, 0)[:cur.shape[0], :]


def _pool_counts(i, tp, w):
    t = i * tp + lax.broadcasted_iota(jnp.int32, (tp, 1), 0)
    return jnp.minimum(t + 1, w).astype(F32)


def _pool_mix(u, u_prev_tail, i, tp):
    win = jnp.concatenate([u_prev_tail, u], axis=0)
    mixes = []
    for gi, w in enumerate(POOL_WINDOWS):
        win = win + pltpu.roll(win, w // 2, 0)
        cols = slice(gi * POOL_GROUP, (gi + 1) * POOL_GROUP)
        mixes.append(win[HALO:, :POOL_GROUP] / _pool_counts(i, tp, w) - u[:, cols])
        if gi + 1 < len(POOL_WINDOWS):
            win = win[:, POOL_GROUP:]
    return mixes


def _prev_tail(ref, i):
    return jnp.where(i > 0, ref[...], 0.0)


def _conv_taps(cur, prev_tail):
    ext = jnp.concatenate([prev_tail, cur], axis=0)
    return [cur] + [pltpu.roll(ext, sft, 0)[HALO:, :] for sft in range(1, CONV_WIDTH)]


def _conv_of_taps(taps, w4):
    y = taps[0] * w4[CONV_WIDTH - 1:CONV_WIDTH, :]
    for sft in range(1, CONV_WIDTH):
        y = y + taps[sft] * w4[CONV_WIDTH - 1 - sft:CONV_WIDTH - sft, :]
    return y


def _conv_fwd(cur, prev_tail, w4):
    ext = jnp.concatenate([prev_tail, cur], axis=0)
    y = ext * w4[CONV_WIDTH - 1:CONV_WIDTH, :]
    for sft in range(1, CONV_WIDTH):
        y = y + pltpu.roll(ext, sft, 0) * w4[CONV_WIDTH - 1 - sft:CONV_WIDTH - sft, :]
    return y[HALO:, :]


def _l2n_heads(t):
    parts = []
    for h in range(DN_HEADS):
        th = t[:, h * DN_HEAD_DIM:(h + 1) * DN_HEAD_DIM]
        parts.append(th * lax.rsqrt(jnp.sum(th * th, axis=-1, keepdims=True) + NORM_EPS))
    return jnp.concatenate(parts, axis=1)


def _post_conv(yq, yk, yv):
    return _l2n_heads(_silu(yq)), _l2n_heads(_silu(yk)), _silu(yv)


def _gates(ba, alog_lane, dtb_lane):
    lane = lax.broadcasted_iota(jnp.int32, ba.shape, 1)
    beta = jax.nn.sigmoid(ba)
    g = -jnp.exp(alog_lane) * _softplus(ba + dtb_lane)
    return jnp.where(lane < DN_HEADS, beta, jnp.where(lane < 2 * DN_HEADS, g, 0.0))


def _front(n_all, g_in, pool_w, pool_scale, g_conv, alog_lane, dtb_lane, w_out_blk, tm):
    s = n_all.shape[0]

    def body(n_ref, pw_ref, ps_ref, gc_ref, al_ref, db_ref, wo_ref, g_hbm,
             pm_ref, pb_ref, yp_ref, qn_ref, kn_ref, vv_ref, gb_ref, gwo_hbm, wt_hbm,
             g_vmem, wt_vmem, tail_u, tail_qkv, gwo_ref, cw_ref, sem, g_sems, wo_send, wo_recv):
        i = pl.program_id(0)

        @pl.when(i == 0)
        def _():
            gwo_ref[_dev_index(*_mesh_pos())] = wo_ref[...].astype(BF16)
            _direct_gather_start(gwo_ref, wo_send, wo_recv)
            loads = [pltpu.make_async_copy(g_hbm.at[d], g_vmem.at[d], g_sems.at[d]) for d in range(N_DEV)]
            for cp in loads:
                cp.start()
            wt_vmem[D_MAIN:, :] = jnp.zeros((D_IN_PAD - D_MAIN, D_MODEL), BF16)
            tail_u[...] = jnp.zeros_like(tail_u)
            tail_qkv[...] = jnp.zeros_like(tail_qkv)
            for d in range(N_DEV):
                cw_ref[:, CONV_SHARD * d:CONV_SHARD * (d + 1)] = gc_ref[d]
            for d, cp in enumerate(loads):
                cp.wait()
                wt_vmem[W_IN_SHARD * d:W_IN_SHARD * (d + 1), :] = g_vmem[d]
            pltpu.make_async_copy(wt_vmem, wt_hbm, sem).start()
        n = n_ref[...]
        pm_ref[...] = _dot_nt_bf(n, wt_vmem[:D_MAIN, :])
        pb = _dot_nt_bf(n, wt_vmem[D_MAIN:, :])
        pb_ref[...] = pb
        u = pm_ref[:, :D_POOL]
        mixes = _pool_mix(u, tail_u[...], i, tm)
        tail_u[...] = u[tm - HALO:, :]
        gate = ps_ref[...] * _silu(pm_ref[:, D_POOL:2 * D_POOL])
        for gi in range(4):
            cols = slice(gi * POOL_GROUP, (gi + 1) * POOL_GROUP)
            yp_ref[:, cols] = _dot_bf(mixes[gi], pw_ref[gi]) * gate[:, cols]
        ys = []
        for c in range(3):
            cols = slice(c * D_DN, (c + 1) * D_DN)
            cur = pm_ref[:, 2 * D_POOL + c * D_DN:2 * D_POOL + (c + 1) * D_DN]
            ys.append(_conv_fwd(cur, tail_qkv[:, cols], cw_ref[:, cols]))
            tail_qkv[:, cols] = cur[tm - HALO:, :]
        qn, kn, vv = _post_conv(*ys)
        qn_ref[...] = qn
        kn_ref[...] = kn
        vv_ref[...] = vv
        gb_ref[...] = _gates(pb, al_ref[...], db_ref[...])

        @pl.when(i == s // tm - 1)
        def _():
            pltpu.make_async_copy(wt_vmem, wt_hbm, sem).wait()
            _direct_gather_wait(gwo_ref, wo_send, wo_recv)
            out = pltpu.make_async_copy(gwo_ref, gwo_hbm, sem)
            out.start()
            out.wait()

    tile = pl.BlockSpec((tm, D_DN), lambda i: (i, 0))
    lanes = pl.BlockSpec((tm, 128), lambda i: (i, 0))
    row = pl.BlockSpec((1, 128), lambda i: (0, 0))
    return pl.pallas_call(
        body, name="front", grid=(s // tm,),
        out_shape=(jax.ShapeDtypeStruct((s, D_MAIN), F32), jax.ShapeDtypeStruct((s, 128), F32),
                   jax.ShapeDtypeStruct((s, D_POOL), F32), jax.ShapeDtypeStruct((s, D_DN), F32),
                   jax.ShapeDtypeStruct((s, D_DN), F32), jax.ShapeDtypeStruct((s, D_DN), F32),
                   jax.ShapeDtypeStruct((s, 128), F32), jax.ShapeDtypeStruct((N_DEV,) + w_out_blk.shape, BF16),
                   jax.ShapeDtypeStruct((D_IN_PAD, D_MODEL), BF16)),
        in_specs=[pl.BlockSpec((tm, D_MODEL), lambda i: (i, 0)),
                  pl.BlockSpec((4, POOL_GROUP, POOL_GROUP), lambda i: (0, 0, 0)),
                  pl.BlockSpec((1, D_POOL), lambda i: (0, 0)),
                  pl.BlockSpec((N_DEV, CONV_WIDTH, CONV_SHARD), lambda i: (0, 0, 0)), row, row,
                  pl.BlockSpec(memory_space=pltpu.VMEM), pl.BlockSpec(memory_space=pl.ANY)],
        out_specs=(pl.BlockSpec((tm, D_MAIN), lambda i: (i, 0)), lanes, tile, tile, tile, tile, lanes,
                   pl.BlockSpec(memory_space=pl.ANY), pl.BlockSpec(memory_space=pl.ANY)),
        scratch_shapes=[pltpu.VMEM((N_DEV, W_IN_SHARD, D_MODEL), BF16), pltpu.VMEM((D_IN_PAD, D_MODEL), BF16),
                        pltpu.VMEM((HALO, D_POOL), F32), pltpu.VMEM((HALO, 3 * D_DN), F32),
                        pltpu.VMEM((N_DEV,) + w_out_blk.shape, BF16), pltpu.VMEM((CONV_WIDTH, 3 * D_DN), F32),
                        pltpu.SemaphoreType.DMA, pltpu.SemaphoreType.DMA((N_DEV,)),
                        pltpu.SemaphoreType.DMA((7,)), pltpu.SemaphoreType.DMA((7,))],
        compiler_params=_cp(("arbitrary",)),
    )(n_all, pool_w, pool_scale, g_conv, alog_lane, dtb_lane, w_out_blk, g_in)


def _dn_block(q, k, v, gcol, bcol, state, known=None):
    nb, n, d = q.shape
    ii = lax.broadcasted_iota(jnp.int32, (n, n), 0)
    jj = lax.broadcasted_iota(jnp.int32, (n, n), 1)
    lower = ii >= jj
    eye = (ii == jj).astype(F32)
    g_row = jnp.sum(eye * gcol, axis=1, keepdims=True)
    gc_col = jnp.sum(jnp.where(lower, g_row, 0.0), axis=2, keepdims=True)
    gc_row = jnp.sum(eye * gc_col, axis=1, keepdims=True)
    decay = jnp.where(lower, jnp.exp(jnp.where(lower, gc_col - gc_row, 0.0)), 0.0)
    kb = k * bcol
    vb = v * bcol
    qs = q * (DN_HEAD_DIM ** -0.5)
    egc = jnp.exp(gc_col)
    kq = jnp.concatenate([kb, qs], axis=1)
    vk = jnp.concatenate([vb, kb * egc], axis=2)
    if known is None:
        akq = _mm(kq, k, 1, 1, *_DN_PREC["akq"])
    else:
        akq = _mm_known(kq, k, known[0][:, :, :n].astype(F32), 1, 1, _DN_PREC["akq"][1])
    a = jnp.where(ii > jj, akq[:, :n] * decay, 0.0)
    qk = akq[:, n:] * decay
    if known is None:
        t = _tri_inv(a, *_DN_PREC["inv"])
        uw = _mm(t, vk, 1, 0, *_DN_PREC["uw"])
    else:
        t = _tri_inv_known(a, known[0][:, :n, n:].astype(F32), _DN_PREC["inv"][1])
        uw = _mm_known(t, vk, known[1], 1, 0, _DN_PREC["uw"][1])
    pre = jnp.concatenate([akq, jnp.concatenate([t, jnp.zeros_like(t)], axis=1)], axis=2)
    wq = jnp.concatenate([uw[:, :, d:], qs * egc], axis=1)
    g_last = gc_col[:, n - 1:n, :]
    k_dec = k * jnp.exp(g_last - gc_col)
    e_last = jnp.exp(g_last)
    os_, starts = [], []
    for c in range(nb // DN_HEADS):
        sl = slice(c * DN_HEADS, (c + 1) * DN_HEADS)
        if known is not None and c > 0:
            state = _use_known(state, known[2][sl])
        starts.append(state)
        ws = _mm(wq[sl], state, 1, 0, *_DN_PREC["ws"])
        v_new = uw[sl, :, :d] - ws[:, :n]
        os_.append(ws[:, n:] + _mm(qk[sl], v_new, 1, 0, *_DN_PREC["ov"]))
        state = state * e_last[sl] + _mm(k_dec[sl], v_new, 0, 0, *_DN_PREC["st"])
    return jnp.concatenate(os_, axis=0), state, (pre, uw, jnp.concatenate(starts, axis=0))


def _gated_norm(o, dz, nw):
    parts = []
    for h in range(DN_HEADS):
        oh = o[:, h * DN_HEAD_DIM:(h + 1) * DN_HEAD_DIM]
        parts.append(oh * lax.rsqrt(jnp.mean(oh * oh, axis=-1, keepdims=True) + NORM_EPS) * nw)
    return jnp.concatenate(parts, axis=1) * _silu(dz)


def _dn_block_args(gc, q_ref, k_ref, v_ref, gb_ref):
    qs, ks, vs, gs, bs = [], [], [], [], []
    for cc in range(gc):
        r = slice(cc * CHUNK, (cc + 1) * CHUNK)
        gbv = gb_ref[r, :]
        for h in range(DN_HEADS):
            cols = slice(h * DN_HEAD_DIM, (h + 1) * DN_HEAD_DIM)
            qs.append(q_ref[r, cols])
            ks.append(k_ref[r, cols])
            vs.append(v_ref[r, cols])
            gs.append(gbv[:, DN_HEADS + h:DN_HEADS + h + 1])
            bs.append(gbv[:, h:h + 1])
    return tuple(jnp.stack(t, axis=0) for t in (qs, ks, vs, gs, bs))


def _dn_scan_fwd(qn, kn, vv, gb, gc):
    s = qn.shape[0]
    nchunk = s // CHUNK
    rows = gc * CHUNK

    def body(q_ref, k_ref, v_ref, gb_ref, y_ref, ss_ref, pre_ref, uw_ref, state):
        @pl.when(pl.program_id(0) == 0)
        def _():
            state[...] = jnp.zeros_like(state)
        q, k, v, gcol, bcol = _dn_block_args(gc, q_ref, k_ref, v_ref, gb_ref)
        y, new, (pre, uw, starts) = _dn_block(q, k, v, gcol, bcol, state[...])
        state[...] = new
        ss_ref[...] = starts
        pre_ref[...] = pre
        uw_ref[...] = uw
        for cc in range(gc):
            for h in range(DN_HEADS):
                y_ref[cc * CHUNK:(cc + 1) * CHUNK, h * DN_HEAD_DIM:(h + 1) * DN_HEAD_DIM] = y[cc * DN_HEADS + h]

    tile = pl.BlockSpec((rows, D_DN), lambda i: (i, 0))
    return pl.pallas_call(
        body, name="dn_scan_fwd", grid=(nchunk // gc,),
        out_shape=(jax.ShapeDtypeStruct((s, D_DN), F32),
                   jax.ShapeDtypeStruct((nchunk * DN_HEADS, DN_HEAD_DIM, DN_HEAD_DIM), F32),
                   jax.ShapeDtypeStruct((nchunk * DN_HEADS, 2 * CHUNK, 2 * CHUNK), F32),
                   jax.ShapeDtypeStruct((nchunk * DN_HEADS, CHUNK, 2 * DN_HEAD_DIM), F32)),
        in_specs=[tile, tile, tile, pl.BlockSpec((rows, 128), lambda i: (i, 0))],
        out_specs=(tile, pl.BlockSpec((gc * DN_HEADS, DN_HEAD_DIM, DN_HEAD_DIM), lambda i: (i, 0, 0)),
                   pl.BlockSpec((gc * DN_HEADS, 2 * CHUNK, 2 * CHUNK), lambda i: (i, 0, 0)),
                   pl.BlockSpec((gc * DN_HEADS, CHUNK, 2 * DN_HEAD_DIM), lambda i: (i, 0, 0))),
        scratch_shapes=[pltpu.VMEM((DN_HEADS, DN_HEAD_DIM, DN_HEAD_DIM), F32)],
        compiler_params=_cp(("arbitrary",)),
    )(qn, kn, vv, gb)


def _out_proj_loss(y_pool, o_dn, proj_main, dn_norm_w, x2, tgt, w_out_full, fnw, tm):
    s = x2.shape[0]

    def body(yp_ref, o_ref, dz_ref, nw_ref, x_ref, t_ref, wo_ref, fw_ref,
             dh_ref, dyp_ref, do_ref, ddz_ref, gwo_ref, gfw_ref, loss_ref, dnw_ref):
        @pl.when(pl.program_id(0) == 0)
        def _():
            gwo_ref[...] = jnp.zeros_like(gwo_ref)
            gfw_ref[...] = jnp.zeros_like(gfw_ref)
            loss_ref[...] = jnp.zeros_like(loss_ref)
            dnw_ref[...] = jnp.zeros_like(dnw_ref)
        y_dn, gate_vjp = jax.vjp(_gated_norm, o_ref[...], dz_ref[...], nw_ref[...])
        y = jnp.concatenate([yp_ref[...], y_dn], axis=1).astype(BF16)
        wo = wo_ref[...]
        h = x_ref[...] + jnp.dot(y, wo, preferred_element_type=F32)
        hn, r = _rms_hat(h)
        fw = fw_ref[...]
        err = hn * fw - t_ref[...]
        loss_ref[...] += 0.5 * jnp.sum(jnp.sum(err * err, axis=-1, keepdims=True) / D_MODEL, axis=0, keepdims=True)
        dout = err / D_MODEL
        gfw_ref[...] += jnp.sum(dout * hn, axis=0, keepdims=True)
        dhn = dout * fw
        dh = r * (dhn - hn * jnp.mean(dhn * hn, axis=-1, keepdims=True))
        dh_ref[...] = dh
        dhb = dh.astype(BF16)
        dy = _dot_nt_bf(dhb, wo)
        dyp_ref[...] = dy[:, :D_POOL]
        do, ddz, dnw = gate_vjp(dy[:, D_POOL:])
        do_ref[...] = do
        ddz_ref[...] = ddz
        dnw_ref[...] += dnw
        gwo_ref[...] += _dot_tn_bf(y, dhb)

    half = pl.BlockSpec((tm, D_POOL), lambda i: (i, 0))
    full = pl.BlockSpec((tm, D_MODEL), lambda i: (i, 0))
    lrow = pl.BlockSpec((1, 128), lambda i: (0, 0))
    return pl.pallas_call(
        body, name="out_proj_loss", grid=(s // tm,),
        out_shape=(jax.ShapeDtypeStruct((s, D_MODEL), F32), jax.ShapeDtypeStruct((s, D_POOL), F32),
                   jax.ShapeDtypeStruct((s, D_DN), F32), jax.ShapeDtypeStruct((s, D_DN), F32),
                   jax.ShapeDtypeStruct((D_MODEL, D_MODEL), F32),
                   jax.ShapeDtypeStruct((1, D_MODEL), F32), jax.ShapeDtypeStruct((1, 128), F32),
                   jax.ShapeDtypeStruct((1, 128), F32)),
        in_specs=[half, half, pl.BlockSpec((tm, D_DN), lambda i: (i, 5)), lrow, full, full,
                  pl.BlockSpec((D_MODEL, D_MODEL), lambda i: (0, 0)), pl.BlockSpec((1, D_MODEL), lambda i: (0, 0))],
        out_specs=(full, half, half, half, pl.BlockSpec((D_MODEL, D_MODEL), lambda i: (0, 0)),
                   pl.BlockSpec((1, D_MODEL), lambda i: (0, 0)), lrow, lrow),
        compiler_params=_cp(("arbitrary",)),
    )(y_pool, o_dn, proj_main, dn_norm_w, x2, tgt, w_out_full, fnw)


def _dn_scan_bwd(qn, kn, vv, gb, states, pre, uw, do_dn, p_out, gc):
    s = qn.shape[0]
    nchunk = s // CHUNK
    nstep = nchunk // gc
    rows = gc * CHUNK

    def body(q_ref, k_ref, v_ref, gb_ref, ss_ref, pre_ref, uw_ref, dy_ref, po_ref,
             dq_ref, dk_ref, dv_ref, dgb_ref, gro_ref, dstate, po_send, po_recv, rs_send, rs_recv):
        @pl.when(pl.program_id(0) == 0)
        def _():
            dstate[...] = jnp.zeros_like(dstate)
            po_send[...] = po_ref[...].astype(BF16)
            _direct_scatter_start(po_send, po_recv, rs_send, rs_recv)

        @pl.when(pl.program_id(0) == nstep - 1)
        def _():
            _direct_scatter_wait(po_send, po_recv, rs_send, rs_recv)
            total = po_ref[_dev_index(*_mesh_pos())]
            for k in range(N_DEV - 1):
                total = total + po_recv[k].astype(F32)
            gro_ref[...] = total
        lane = lax.broadcasted_iota(jnp.int32, (CHUNK, 128), 1)
        q, k, v, gcol, bcol = _dn_block_args(gc, q_ref, k_ref, v_ref, gb_ref)
        dy = jnp.stack([dy_ref[cc * CHUNK:(cc + 1) * CHUNK, h * DN_HEAD_DIM:(h + 1) * DN_HEAD_DIM]
                        for cc in range(gc) for h in range(DN_HEADS)], axis=0)
        known = (pre_ref[...], uw_ref[...], ss_ref[...])
        _, vjp = jax.vjp(lambda *a: _dn_block(*a, known=known)[:2], q, k, v, gcol, bcol, ss_ref[:DN_HEADS])
        dq, dk, dv, dg, db, dst = vjp((dy, dstate[...]))
        dstate[...] = dst
        for cc in range(gc):
            r = slice(cc * CHUNK, (cc + 1) * CHUNK)
            dgb = jnp.zeros((CHUNK, 128), F32)
            for h in range(DN_HEADS):
                b = cc * DN_HEADS + h
                cols = slice(h * DN_HEAD_DIM, (h + 1) * DN_HEAD_DIM)
                for ref, val in zip((dq_ref, dk_ref, dv_ref), (dq, dk, dv)):
                    ref[r, cols] = val[b]
                dgb = dgb + jnp.where(lane == h, db[b], 0.0) + jnp.where(lane == DN_HEADS + h, dg[b], 0.0)
            dgb_ref[r, :] = dgb

    rev = lambda i: (nstep - 1 - i, 0)
    tile = pl.BlockSpec((rows, D_DN), rev)
    lanes = pl.BlockSpec((rows, 128), rev)
    return pl.pallas_call(
        body, name="dn_scan_bwd", grid=(nstep,),
        out_shape=(jax.ShapeDtypeStruct((s, D_DN), F32),) * 3
        + (jax.ShapeDtypeStruct((s, 128), F32), jax.ShapeDtypeStruct(p_out.shape[1:], F32)),
        in_specs=[tile, tile, tile, lanes,
                  pl.BlockSpec((gc * DN_HEADS, DN_HEAD_DIM, DN_HEAD_DIM), lambda i: (nstep - 1 - i, 0, 0)),
                  pl.BlockSpec((gc * DN_HEADS, 2 * CHUNK, 2 * CHUNK), lambda i: (nstep - 1 - i, 0, 0)),
                  pl.BlockSpec((gc * DN_HEADS, CHUNK, 2 * DN_HEAD_DIM), lambda i: (nstep - 1 - i, 0, 0)), tile,
                  pl.BlockSpec(memory_space=pltpu.VMEM)],
        out_specs=(tile, tile, tile, lanes, pl.BlockSpec(memory_space=pltpu.VMEM)),
        scratch_shapes=[pltpu.VMEM((DN_HEADS, DN_HEAD_DIM, DN_HEAD_DIM), F32),
                        pltpu.VMEM(p_out.shape, BF16), pltpu.VMEM((N_DEV - 1,) + p_out.shape[1:], BF16),
                        pltpu.SemaphoreType.DMA((7,)), pltpu.SemaphoreType.DMA((7,))],
        compiler_params=_cp(("arbitrary",)),
    )(qn, kn, vv, gb, states, pre, uw, do_dn, p_out)


def _back(proj_main, proj_ba, dyp, dqn, dkn, dvv, dgb, ddz, x2, dh, norm_w, pool_w, pool_scale, g_conv,
          alog_lane, dtb_lane, wt_full, tm):
    s = x2.shape[0]
    nstep = s // tm
    per = tm // HALO

    def body(u_ref, z_ref, q_ref, k_ref, v_ref, up_ref, qp_ref, kp_ref, vp_ref, ba_ref,
             dyp_ref, dqn_ref, dkn_ref, dvv_ref, dgb_ref, ddz_ref, x_ref, dh_ref,
             nw_ref, pw_ref, ps_ref, gc_ref, al_ref, db_ref, wt_hbm,
             gx_ref, p_hbm, gnw_ref, dpw_ref, dps_ref, pc_ref, dal_ref, ddb_ref,
             wt_vmem, acc, blk, head_dc, head_dw, cw_ref, dcw_ref, sem, osem):
        j = pl.program_id(0)
        i = nstep - 1 - j

        @pl.when(j == 0)
        def _():
            cp = pltpu.make_async_copy(wt_hbm, wt_vmem, sem)
            cp.start()
            acc[...] = jnp.zeros_like(acc)
            for ref in (gnw_ref, dpw_ref, dps_ref, dcw_ref, dal_ref, ddb_ref, head_dc, head_dw):
                ref[...] = jnp.zeros_like(ref)
            cp.wait()

        for d in range(N_DEV):
            cw_ref[:, CONV_SHARD * d:CONV_SHARD * (d + 1)] = gc_ref[d]

        u = u_ref[...]
        z = z_ref[...]
        dy = dyp_ref[...]
        ps = ps_ref[...]
        mixes = _pool_mix(u, _prev_tail(up_ref, i), i, tm)
        sg = jax.nn.sigmoid(z)
        sz = z * sg
        dsz = sg * (1.0 + z * (1.0 - sg))
        dzs, dwins = [], []
        for gi, w in enumerate(POOL_WINDOWS):
            cols = slice(gi * POOL_GROUP, (gi + 1) * POOL_GROUP)
            mixw = _dot_bf(mixes[gi], pw_ref[gi])
            dmixw = dy[:, cols] * ps[:, cols] * sz[:, cols]
            dps_ref[:, cols] += jnp.sum(dy[:, cols] * mixw * sz[:, cols], axis=0, keepdims=True)
            dzs.append(dy[:, cols] * mixw * ps[:, cols] * dsz[:, cols])
            dpw_ref[gi] += _dot_tn_bf(mixes[gi], dmixw)
            dwins.append(_dot_nt_bf(dmixw, pw_ref[gi]) / _pool_counts(i, tm, w))
        dzp = jnp.concatenate(dzs, axis=1)
        dw = jnp.concatenate(dwins, axis=1)
        win = jnp.concatenate([dw, head_dw[...]], axis=0)
        m = win.shape[0]
        dups = []
        for gi, w in enumerate(POOL_WINDOWS):
            win = win + pltpu.roll(win, m - w // 2, 0)
            cols = slice(gi * POOL_GROUP, (gi + 1) * POOL_GROUP)
            dups.append(win[:tm, :POOL_GROUP] - dw[:, cols] * _pool_counts(i, tm, w))
            if gi + 1 < len(POOL_WINDOWS):
                win = win[:, POOL_GROUP:]
        dup = jnp.concatenate(dups, axis=1)
        head_dw[...] = dw[:HALO, :]

        curs = (q_ref[...], k_ref[...], v_ref[...])
        tails = (_prev_tail(qp_ref, i), _prev_tail(kp_ref, i), _prev_tail(vp_ref, i))
        taps = [_conv_taps(curs[c], tails[c]) for c in range(3)]
        ys = [_conv_of_taps(taps[c], cw_ref[:, c * D_DN:(c + 1) * D_DN]) for c in range(3)]
        _, vjp = jax.vjp(_post_conv, *ys)
        dys = vjp((dqn_ref[...], dkn_ref[...], dvv_ref[...]))
        dxs = []
        for c, dyc in enumerate(dys):
            cols = slice(c * D_DN, (c + 1) * D_DN)
            w4 = cw_ref[:, cols]
            for sft in range(CONV_WIDTH):
                row = CONV_WIDTH - 1 - sft
                dcw_ref[row:row + 1, cols] += jnp.sum(dyc * taps[c][sft], axis=0, keepdims=True)
            head = head_dc[:, cols]
            dx = dyc * w4[CONV_WIDTH - 1:CONV_WIDTH, :]
            for sft in range(1, CONV_WIDTH):
                dx = dx + _shift_up(dyc, head, sft) * w4[CONV_WIDTH - 1 - sft:CONV_WIDTH - sft, :]
            dxs.append(dx)
            head_dc[:, cols] = dyc[:HALO, :]
        _, gvjp = jax.vjp(_gates, ba_ref[...], al_ref[...], db_ref[...])
        dba, dal, ddb = gvjp(dgb_ref[...])
        dal_ref[...] += dal
        ddb_ref[...] += ddb

        dbab = dba.astype(BF16)
        xhat, r = _rms_hat(x_ref[...])
        nw = nw_ref[...]
        n = (xhat * nw).astype(BF16)
        acc[D_MAIN:, :] += _dot_tn_bf(dbab, n)
        dn = jnp.dot(dbab, wt_vmem[D_MAIN:, :], preferred_element_type=F32)
        for cb, d in enumerate((dup, dzp, dxs[0], dxs[1], dxs[2], ddz_ref[...])):
            rows = slice(cb * D_POOL, (cb + 1) * D_POOL)
            dpart = d.astype(BF16)
            acc[rows, :] += _dot_tn_bf(dpart, n)
            dn = dn + jnp.dot(dpart, wt_vmem[rows, :], preferred_element_type=F32)
        gnw_ref[...] += jnp.sum(dn * xhat, axis=0, keepdims=True)
        dxh = dn * nw
        gx_ref[...] = dh_ref[...] + r * (dxh - xhat * jnp.mean(dxh * xhat, axis=-1, keepdims=True))

        @pl.when(j == nstep - 1)
        def _():
            for d in range(N_DEV):
                pc_ref[d] = dcw_ref[:, CONV_SHARD * d:CONV_SHARD * (d + 1)]

            def out(d):
                return pltpu.make_async_copy(blk.at[d % 2], p_hbm.at[d], osem.at[d % 2])
            for d in range(N_DEV):
                if d >= 2:
                    out(d - 2).wait()
                blk[d % 2] = acc[W_IN_SHARD * d:W_IN_SHARD * (d + 1), :]
                out(d).start()
            out(N_DEV - 2).wait()
            out(N_DEV - 1).wait()

    def col(c):
        return pl.BlockSpec((tm, D_POOL), lambda j: (nstep - 1 - j, c))

    def halo(c):
        return pl.BlockSpec((HALO, D_POOL), lambda j: (jnp.maximum((nstep - 1 - j) * per - 1, 0), c))

    rev = lambda j: (nstep - 1 - j, 0)
    part = pl.BlockSpec((tm, D_POOL), rev)
    lanes = pl.BlockSpec((tm, 128), rev)
    full = pl.BlockSpec((tm, D_MODEL), rev)
    row = pl.BlockSpec((1, D_MODEL), lambda j: (0, 0))
    lrow = pl.BlockSpec((1, 128), lambda j: (0, 0))
    pw = pl.BlockSpec((4, POOL_GROUP, POOL_GROUP), lambda j: (0, 0, 0))
    psp = pl.BlockSpec((1, D_POOL), lambda j: (0, 0))
    cw = pl.BlockSpec((N_DEV, CONV_WIDTH, CONV_SHARD), lambda j: (0, 0, 0))
    return pl.pallas_call(
        body, name="back", grid=(nstep,),
        out_shape=(jax.ShapeDtypeStruct((s, D_MODEL), F32),
                   jax.ShapeDtypeStruct((N_DEV, W_IN_SHARD, D_MODEL), F32), jax.ShapeDtypeStruct((1, D_MODEL), F32),
                   jax.ShapeDtypeStruct((4, POOL_GROUP, POOL_GROUP), F32), jax.ShapeDtypeStruct((1, D_POOL), F32),
                   jax.ShapeDtypeStruct((N_DEV, CONV_WIDTH, CONV_SHARD), F32),
                   jax.ShapeDtypeStruct((1, 128), F32), jax.ShapeDtypeStruct((1, 128), F32)),
        in_specs=[col(0), col(1), col(2), col(3), col(4), halo(0), halo(2), halo(3), halo(4), lanes,
                  part, part, part, part, lanes, part, full, full,
                  row, pw, psp, cw, lrow, lrow, pl.BlockSpec(memory_space=pl.ANY)],
        out_specs=(full, pl.BlockSpec(memory_space=pl.ANY), row, pw, psp, cw, lrow, lrow),
        scratch_shapes=[pltpu.VMEM((D_IN_PAD, D_MODEL), BF16), pltpu.VMEM((D_IN_PAD, D_MODEL), F32),
                        pltpu.VMEM((2, W_IN_SHARD, D_MODEL), F32),
                        pltpu.VMEM((HALO, 3 * D_DN), F32), pltpu.VMEM((HALO, D_POOL), F32),
                        pltpu.VMEM((CONV_WIDTH, 3 * D_DN), F32), pltpu.VMEM((CONV_WIDTH, 3 * D_DN), F32),
                        pltpu.SemaphoreType.DMA, pltpu.SemaphoreType.DMA((2,))],
        compiler_params=_cp(("arbitrary",)),
    )(proj_main, proj_main, proj_main, proj_main, proj_main, proj_main, proj_main, proj_main, proj_main, proj_ba,
      dyp, dqn, dkn, dvv, dgb, ddz, x2, dh, norm_w, pool_w, pool_scale, g_conv, alog_lane, dtb_lane, wt_full)


def _adamw_math(w, g, m, v):
    m = ADAM_B1 * m + (1.0 - ADAM_B1) * g
    v = ADAM_B2 * v + (1.0 - ADAM_B2) * (g * g)
    m_hat = m / (1.0 - ADAM_B1 ** ADAM_STEP)
    v_hat = v / (1.0 - ADAM_B2 ** ADAM_STEP)
    delta = -ADAM_LR * (m_hat / (jnp.sqrt(v_hat) + ADAM_EPS) + ADAM_WD * w)
    return delta, m, v


def _adamw(tiled, whole, gath_a, gath_b, pool, rows, lanes, nstep):
    kw, nrow = len(whole), len(rows)
    ins = list(tiled) + [a for p in whole for a in p] + [gath_a, gath_b] + list(pool) + [a for wmv in rows for a in wmv]

    def body(*refs):
        in_refs, outs = refs[:len(ins)], refs[len(ins):]

        def update(w, g, m, v, o):
            dl, nm, nv = _adamw_math(w, g, m, v)
            for ref, val in zip(o, (g, dl, nm, nv)):
                ref[...] = val

        update(*(r[...] for r in in_refs[:4]), outs[:4])

        @pl.when(pl.program_id(0) == 0)
        def _():
            for p in range(1, kw + 1):
                update(*(r[...] for r in in_refs[4 * p:4 * p + 4]), outs[4 * p:4 * p + 4])
            ga_ref, gb_ref = in_refs[4 * kw + 4:4 * kw + 6]
            rep_in, rep_out = in_refs[4 * kw + 6:], outs[4 * kw + 4:]

            def total(ref):
                g = ref[0]
                for d in range(1, N_DEV):
                    g = g + ref[d]
                return g

            gs = [total(ga_ref)]
            gb = total(gb_ref)
            for r in range(nrow):
                n = rep_in[3 * (r + 1)].shape[1]
                head = gb if lanes[r] == 0 else pltpu.roll(gb[:, :128], 128 - lanes[r], 1)
                gs.append(head[r:r + 1, :n])
            for r, g in enumerate(gs):
                w, m, v = (ref[...] for ref in rep_in[3 * r:3 * r + 3])
                update(w, g, m, v, rep_out[4 * r:4 * r + 4])
            rep_out[4 * (nrow + 1)][...] = gb[nrow:nrow + 1, 0:1]

    tile = pl.BlockSpec((tiled[0].shape[0] // nstep, tiled[0].shape[1]), lambda i: (i, 0))

    def full(a):
        return pl.BlockSpec(a.shape, lambda i, nd=a.ndim: (0,) * nd)

    firsts = [p[0] for p in whole] + [pool[0]] + [wmv[0] for wmv in rows]
    out_shape = ([jax.ShapeDtypeStruct(tiled[0].shape, F32)] * 4
                 + [jax.ShapeDtypeStruct(w.shape, F32) for w in firsts for _ in range(4)]
                 + [jax.ShapeDtypeStruct((1, 1), F32)])
    res = pl.pallas_call(
        body, name="adamw", grid=(nstep,),
        in_specs=[tile] * 4 + [full(a) for a in ins[4:]],
        out_specs=tuple([tile] * 4 + [full(o) for o in out_shape[4:]]),
        out_shape=tuple(out_shape),
        compiler_params=_cp(("arbitrary",)),
    )(*ins)
    groups = [tuple(res[4 * k:4 * k + 4]) for k in range(kw + nrow + 2)]
    return groups[0], groups[1:kw + 1], groups[kw + 1:], res[-1]


_ROW_ORDER = ("norm_w", "final_norm_w", "pool_scale", "dn_norm_w", "a_log", "dt_bias")
_ROW_LANE = (0, 0, 0, 0, DN_HEADS, DN_HEADS)


def _lane_row(vec4, start):
    return jnp.pad(vec4.reshape(-1), (start, 128 - start - vec4.size)).reshape(1, 128)


def kernel(x, norm_w, w_in, pool_w, pool_scale, conv_w, a_log, dt_bias, dn_norm_w, w_out, final_norm_w, loss_target, m_norm_w, m_w_in, m_pool_w, m_pool_scale, m_conv_w, m_a_log, m_dt_bias, m_dn_norm_w, m_w_out, m_final_norm_w, v_norm_w, v_w_in, v_pool_w, v_pool_scale, v_conv_w, v_a_log, v_dt_bias, v_dn_norm_w, v_w_out, v_final_norm_w):
    s = x.shape[1]
    tm = min(512, s)
    tmb = min(256, s)
    x2 = x[0]
    tgt = loss_target[0]
    def to_flat(a):
        return a[0].reshape(FLAT_ROWS, 128, W_IN_SHARD).transpose(2, 0, 1).reshape(W_IN_SHARD * FLAT_ROWS, 128)

    def from_flat(f):
        return f.reshape(W_IN_SHARD, FLAT_ROWS, 128).transpose(1, 2, 0).reshape(1, D_MODEL, W_IN_SHARD)

    wf, m_wf, v_wf = to_flat(w_in), to_flat(m_w_in), to_flat(v_w_in)

    g_in, g_conv, n_all = _gather_weights(wf, conv_w[0], x2, norm_w, tm)
    alog_lane = _lane_row(a_log, DN_HEADS)
    dtb_lane = _lane_row(dt_bias, DN_HEADS)
    fnw = final_norm_w.reshape(1, D_MODEL)

    proj_main, proj_ba, y_pool, qn, kn, vv, gb, g_out, wt_full = _front(
        n_all, g_in, pool_w[0], pool_scale, g_conv, alog_lane, dtb_lane, w_out[0], tm)
    w_out_full = g_out.reshape(D_MODEL, D_MODEL)
    o_dn, states, dn_pre, dn_uw = _dn_scan_fwd(qn, kn, vv, gb, DN_CHUNKS_PER_STEP)

    dh, dyp, do_dn, ddz, g_wout, g_fnw, loss_part, g_dnw = _out_proj_loss(
        y_pool, o_dn, proj_main, dn_norm_w, x2, tgt, w_out_full, fnw, tm)
    p_out = g_wout.reshape(N_DEV, D_MODEL // N_DEV, D_MODEL)
    dqn, dkn, dvv, dgb, gr_out = _dn_scan_bwd(qn, kn, vv, gb, states, dn_pre, dn_uw, do_dn, p_out, DN_CHUNKS_PER_STEP)
    grad_x2, p_in, g_nw, g_pw, g_ps, p_conv, g_al, g_db = _back(
        proj_main, proj_ba, dyp, dqn, dkn, dvv, dgb, ddz, x2, dh, norm_w, pool_w[0], pool_scale, g_conv,
        alog_lane, dtb_lane, wt_full, tmb)

    pack_a = g_pw.reshape(4 * POOL_GROUP, POOL_GROUP)
    gr_in, gr_conv, gath_a, gath_b = _reduce_grads((p_in, p_conv), (wf, conv_w[0]), pack_a,
                                                   [g_nw, g_fnw, g_ps, g_dnw, g_al, g_db, loss_part])

    flat = lambda a: a.reshape(4 * POOL_GROUP, POOL_GROUP)
    row = lambda a: a.reshape(1, -1)
    vecs = {"norm_w": (norm_w, m_norm_w, v_norm_w), "final_norm_w": (final_norm_w, m_final_norm_w, v_final_norm_w),
            "pool_scale": (pool_scale, m_pool_scale, v_pool_scale), "dn_norm_w": (dn_norm_w, m_dn_norm_w, v_dn_norm_w),
            "a_log": (a_log, m_a_log, v_a_log), "dt_bias": (dt_bias, m_dt_bias, v_dt_bias)}
    r_in, (r_out, r_conv), res, loss = _adamw(
        (wf, gr_in, m_wf, v_wf),
        [(w_out[0], gr_out, m_w_out[0], v_w_out[0]), (conv_w[0], gr_conv, m_conv_w[0], v_conv_w[0])],
        gath_a, gath_b, (flat(pool_w), flat(m_pool_w), flat(v_pool_w)),
        [tuple(row(a) for a in vecs[nm]) for nm in _ROW_ORDER], _ROW_LANE, ADAMW_STEPS)
    r_pool = res[0]
    r_vec = dict(zip(_ROW_ORDER, res[1:]))

    def group(k):
        vec = lambda nm: r_vec[nm][k].reshape(vecs[nm][0].shape)
        return (vec("norm_w"), from_flat(r_in[k]), r_pool[k].reshape(pool_w.shape), vec("pool_scale"), r_conv[k][None],
                vec("a_log"), vec("dt_bias"), vec("dn_norm_w"), r_out[k][None], vec("final_norm_w"))

    return (loss[0, 0], grad_x2[None], *group(0), *group(1), *group(2), *group(3))
```

```python
import functools

import jax
import jax.numpy as jnp
from jax import lax
from jax.experimental import pallas as pl
from jax.experimental.pallas import tpu as pltpu

F32 = jnp.float32
BF16 = jnp.bfloat16
HI = lax.Precision.HIGHEST
MESH = pl.DeviceIdType.MESH

D_MODEL = 1024
D_POOL = 512
D_DN = 512
POOL_WINDOWS = (2, 4, 8, 16)
POOL_GROUP = 128
DN_HEADS = 4
DN_HEAD_DIM = 128
CONV_WIDTH = 4
CHUNK = 64
NORM_EPS = 1e-6
D_IN = 3080
D_MAIN = 3072
FLAT_ROWS = D_MODEL // 128
D_IN_PAD = D_MAIN + 128
N_DEV = 8
W_IN_SHARD = D_IN // N_DEV
CONV_SHARD = 3 * D_DN // N_DEV
HALO = 16
DN_CHUNKS_PER_STEP = 8
ADAMW_STEPS = 7

ADAM_LR = 0.001
ADAM_B1 = 0.9
ADAM_B2 = 0.999
ADAM_EPS = 1e-08
ADAM_WD = 0.01
ADAM_STEP = 10

VMEM_LIMIT = 56 * 1024 * 1024
def _cp(sem=None, vmem=VMEM_LIMIT):
    kw = {"vmem_limit_bytes": vmem}
    if sem is not None:
        kw["dimension_semantics"] = sem
    return pltpu.CompilerParams(**kw)


def _dot_bf(a, b):
    return jnp.dot(a.astype(BF16), b.astype(BF16), preferred_element_type=F32)


def _dot_nt_bf(a, b):
    return lax.dot_general(a.astype(BF16), b.astype(BF16), (((1,), (1,)), ((), ())), preferred_element_type=F32)


def _dot_tn_bf(a, b):
    return lax.dot_general(a.astype(BF16), b.astype(BF16), (((0,), (0,)), ((), ())), preferred_element_type=F32)


def _mm_raw(a, b, ca, cb, prec):
    off = a.ndim - 2
    dn = (((ca + off,), (cb + off,)), ((0,), (0,)) if off else ((), ()))
    if prec == "hi":
        return lax.dot_general(a, b, dn, precision=HI, preferred_element_type=F32)
    ah, bh = a.astype(BF16), b.astype(BF16)
    out = lax.dot_general(ah, bh, dn, preferred_element_type=F32)
    if prec == "x3":
        al = (a - ah.astype(F32)).astype(BF16)
        bl = (b - bh.astype(F32)).astype(BF16)
        out = out + lax.dot_general(ah, bl, dn, preferred_element_type=F32)
        out = out + lax.dot_general(al, bh, dn, preferred_element_type=F32)
    return out


@functools.partial(jax.custom_vjp, nondiff_argnums=(2, 3, 4, 5))
def _mm(a, b, ca, cb, prec, bprec):
    return _mm_raw(a, b, ca, cb, prec)


def _mm_fwd(a, b, ca, cb, prec, bprec):
    return _mm_raw(a, b, ca, cb, prec), (a, b)


def _mm_bwd(ca, cb, prec, bprec, res, dc):
    a, b = res
    da = _mm_raw(dc, b, 1, 1 - cb, bprec) if ca == 1 else _mm_raw(b, dc, 1 - cb, 1, bprec)
    db = _mm_raw(a, dc, 1 - ca, 0, bprec) if cb == 0 else _mm_raw(dc, a, 0, 1 - ca, bprec)
    return da, db


_mm.defvjp(_mm_fwd, _mm_bwd)


@functools.partial(jax.custom_vjp, nondiff_argnums=(1, 2))
def _tri_inv(a, prec, bprec):
    n = a.shape[-1]
    ii = lax.broadcasted_iota(jnp.int32, (n, n), 0)
    jj = lax.broadcasted_iota(jnp.int32, (n, n), 1)
    p = (ii == jj).astype(F32) - a
    b = _mm_raw(a, a, 1, 0, prec)
    for _ in range(4):
        pb = _mm_raw(jnp.concatenate([p, b], axis=-2), b, 1, 0, prec)
        p = p + pb[..., :n, :]
        b = pb[..., n:, :]
    return p + _mm_raw(p, b, 1, 0, prec)


def _tri_inv_fwd(a, prec, bprec):
    t = _tri_inv(a, prec, bprec)
    return t, t


def _tri_inv_bwd(prec, bprec, t, dt):
    return (-_mm_raw(_mm_raw(t, dt, 0, 0, bprec), t, 1, 1, bprec),)


_tri_inv.defvjp(_tri_inv_fwd, _tri_inv_bwd)

@functools.partial(jax.custom_vjp, nondiff_argnums=(3, 4, 5))
def _mm_known(a, b, out, ca, cb, bprec):
    return out


def _mm_known_fwd(a, b, out, ca, cb, bprec):
    return out, (a, b)


def _mm_known_bwd(ca, cb, bprec, res, dc):
    return _mm_bwd(ca, cb, None, bprec, res, dc) + (jnp.zeros_like(dc),)


_mm_known.defvjp(_mm_known_fwd, _mm_known_bwd)


@jax.custom_vjp
def _use_known(x, known):
    return known


_use_known.defvjp(lambda x, known: (known, None), lambda _, g: (g, jnp.zeros_like(g)))


@functools.partial(jax.custom_vjp, nondiff_argnums=(2,))
def _tri_inv_known(a, t, bprec):
    return t


def _tri_inv_known_fwd(a, t, bprec):
    return t, t


def _tri_inv_known_bwd(bprec, t, dt):
    return _tri_inv_bwd(None, bprec, t, dt) + (jnp.zeros_like(dt),)


_tri_inv_known.defvjp(_tri_inv_known_fwd, _tri_inv_known_bwd)

_DN_PREC = {"akq": ("bf16", "bf16"), "inv": ("bf16", "bf16"), "uw": ("bf16", "bf16"), "ws": ("bf16", "bf16"),
            "ov": ("bf16", "bf16"), "st": ("bf16", "bf16")}


def _silu(x):
    return x * jax.nn.sigmoid(x)


def _softplus(x):
    pos = x > 0.0
    return jnp.where(pos, x, 0.0) + jnp.log1p(jnp.exp(jnp.where(pos, -x, x)))


def _mesh_pos():
    return lax.axis_index("x"), lax.axis_index("y"), lax.axis_index("c")


def _dev_index(x, y, c):
    return 4 * x + 2 * y + c


def _relay_order():
    x, y, c = _mesh_pos()
    n1 = (x + (1 - c) * (1 - 2 * x), y + c * (1 - 2 * y))
    n2 = (x + c * (1 - 2 * x), y + (1 - c) * (1 - 2 * y))
    return (x, y, c), (x, y, 1 - c), n1, n2, (1 - x, 1 - y)


def _all_gather_blocks(outs, send_sems, recv_sems, meanwhile=None, own=None):
    me, sibling, n1, n2, diag = _relay_order()
    c = me[2]

    def copy(a, k, block, to, src=None):
        rows = outs[a].at[_dev_index(*block)]
        return pltpu.make_async_remote_copy(src_ref=rows if src is None else src, dst_ref=rows,
                                            send_sem=send_sems.at[a, k], recv_sem=recv_sems.at[a, k],
                                            device_id=to, device_id_type=MESH)

    n = len(outs)
    started = []

    def start(cp):
        cp.start()
        started.append(cp)

    for a in range(n):
        mine = None if own is None else own[a]
        start(copy(a, 1, me, (*n1, c), mine))
        start(copy(a, 2, me, (*n2, c), mine))
        start(copy(a, 0, me, sibling, mine))
    if meanwhile is not None:
        meanwhile()
    for a in range(n):
        copy(a, 1, (*n1, c), me).wait_recv()
        start(copy(a, 3, (*n1, c), (*n2, c)))
        start(copy(a, 4, (*n1, c), sibling))
    for a in range(n):
        copy(a, 2, (*n2, c), me).wait_recv()
        start(copy(a, 5, (*n2, c), sibling))
        copy(a, 3, (*diag, c), me).wait_recv()
        start(copy(a, 6, (*diag, c), sibling))
    for a in range(n):
        copy(a, 0, sibling, me).wait_recv()
        copy(a, 4, (*n2, 1 - c), me).wait_recv()
        copy(a, 5, (*n1, 1 - c), me).wait_recv()
        copy(a, 6, (*diag, 1 - c), me).wait_recv()
    for cp in started:
        cp.wait_send()


def _peer_relations():
    x, y, c = _mesh_pos()
    flips = [(fx, fy, fc) for fx in (0, 1) for fy in (0, 1) for fc in (0, 1)][1:]
    peers = [(1 - x if fx else x, 1 - y if fy else y, 1 - c if fc else c) for fx, fy, fc in flips]
    return (x, y, c), peers


def _direct_gather_start(out_ref, send_sems, recv_sems):
    me, peers = _peer_relations()
    rows = out_ref.at[_dev_index(*me)]
    for k, peer in enumerate(peers):
        pltpu.make_async_remote_copy(src_ref=rows, dst_ref=rows, send_sem=send_sems.at[k], recv_sem=recv_sems.at[k],
                                     device_id=peer, device_id_type=MESH).start()


def _direct_gather_wait(out_ref, send_sems, recv_sems):
    me, peers = _peer_relations()
    for k, peer in enumerate(peers):
        rows = out_ref.at[_dev_index(*peer)]
        cp = pltpu.make_async_remote_copy(src_ref=rows, dst_ref=rows, send_sem=send_sems.at[k],
                                          recv_sem=recv_sems.at[k], device_id=peer, device_id_type=MESH)
        cp.wait_recv()
        cp.wait_send()


def _direct_scatter_start(send_ref, recv_ref, send_sems, recv_sems):
    me, peers = _peer_relations()
    for k, peer in enumerate(peers):
        pltpu.make_async_remote_copy(src_ref=send_ref.at[_dev_index(*peer)], dst_ref=recv_ref.at[k],
                                     send_sem=send_sems.at[k], recv_sem=recv_sems.at[k],
                                     device_id=peer, device_id_type=MESH).start()


def _direct_scatter_wait(send_ref, recv_ref, send_sems, recv_sems):
    me, peers = _peer_relations()
    for k, peer in enumerate(peers):
        cp = pltpu.make_async_remote_copy(src_ref=send_ref.at[_dev_index(*peer)], dst_ref=recv_ref.at[k],
                                          send_sem=send_sems.at[k], recv_sem=recv_sems.at[k],
                                          device_id=peer, device_id_type=MESH)
        cp.wait_recv()
        cp.wait_send()


def _gather_weights(w_in_flat, conv_blk, x2, norm_w, tn):
    nt = x2.shape[0] // tn

    def body(win_ref, conv_ref, x_hbm, nw_ref, gin_hbm, gconv_ref, n_hbm, own_ref, xbuf, nbuf, send_sems, recv_sems,
             xsem, nsem, own_sem):
        x, y, c = _mesh_pos()
        me = _dev_index(x, y, c)
        for j in range(FLAT_ROWS):
            own_ref[:, 128 * j:128 * (j + 1)] = win_ref[pl.ds(j, W_IN_SHARD, stride=FLAT_ROWS), :].astype(BF16)
        keep_own = pltpu.make_async_copy(own_ref, gin_hbm.at[me], own_sem.at[0])
        keep_own.start()
        gconv_ref[me] = conv_ref[...]

        def norm_x():
            def load(t):
                return pltpu.make_async_copy(x_hbm.at[pl.ds(t * tn, tn), :], xbuf.at[t % 2], xsem.at[t % 2])

            def store(t):
                return pltpu.make_async_copy(nbuf.at[t % 2], n_hbm.at[pl.ds(t * tn, tn), :], nsem.at[t % 2])

            load(0).start()
            for t in range(nt):
                if t + 1 < nt:
                    load(t + 1).start()
                load(t).wait()
                if t >= 2:
                    store(t - 2).wait()
                xhat, _ = _rms_hat(xbuf[t % 2])
                nbuf[t % 2] = (xhat * nw_ref[...]).astype(BF16)
                store(t).start()
            for t in range(max(nt - 2, 0), nt):
                store(t).wait()

        _all_gather_blocks((gin_hbm, gconv_ref), send_sems, recv_sems, meanwhile=norm_x, own=(own_ref, None))
        keep_own.wait()

    vm = pl.BlockSpec(memory_space=pltpu.VMEM)
    hbm = pl.BlockSpec(memory_space=pl.ANY)
    return pl.pallas_call(
        body, name="gather_weights",
        out_shape=(jax.ShapeDtypeStruct((N_DEV, W_IN_SHARD, D_MODEL), BF16),
                   jax.ShapeDtypeStruct((N_DEV,) + conv_blk.shape, F32),
                   jax.ShapeDtypeStruct(x2.shape, BF16)),
        in_specs=[vm, vm, hbm, vm], out_specs=(hbm, vm, hbm),
        scratch_shapes=[pltpu.VMEM((W_IN_SHARD, D_MODEL), BF16),
                        pltpu.VMEM((2, tn, D_MODEL), F32), pltpu.VMEM((2, tn, D_MODEL), BF16),
                        pltpu.SemaphoreType.DMA((2, 7)), pltpu.SemaphoreType.DMA((2, 7)),
                        pltpu.SemaphoreType.DMA((2,)), pltpu.SemaphoreType.DMA((2,)), pltpu.SemaphoreType.DMA((1,))],
        compiler_params=_cp(),
    )(w_in_flat, conv_blk, x2, norm_w)


def _reduce_grads(big, like, pack_a, rows):
    nb = len(big)

    def body(*refs):
        nr = len(rows)
        srcs, pa_ref, row_refs = refs[:nb], refs[nb], refs[nb + 1:nb + 1 + nr]
        outs, (ga_ref, gb_ref) = refs[nb + 1 + nr:2 * nb + 1 + nr], refs[2 * nb + 1 + nr:2 * nb + 3 + nr]
        scr = refs[2 * nb + 3 + nr:]
        r1s, r2s, sbs, sts = (scr[k * nb:(k + 1) * nb] for k in range(4))
        s1_send, s1_recv, s2_send, s2_recv, ag_send, ag_recv, st_sem = scr[4 * nb:]
        x, y, c = _mesh_pos()
        me = (x, y, c)
        sibling = (x, y, 1 - c)

        ga_ref[_dev_index(*me)] = pa_ref[...]
        gb_ref[_dev_index(*me)] = jnp.zeros(gb_ref.shape[1:], F32)
        for r, ref in enumerate(row_refs):
            gb_ref[_dev_index(*me), r:r + 1, :ref.shape[1]] = ref[...]

        _, _, n1, n2, diag = _relay_order()
        order = ((diag, 3), (n1, 1 + c), (n2, 2 - c), ((x, y), 0))

        def p1(a, i, to):
            chip, slot = order[i]
            return pltpu.make_async_remote_copy(
                src_ref=srcs[a].at[_dev_index(*chip, 1 - c)], dst_ref=r1s[a].at[slot],
                send_sem=s1_send.at[a, slot], recv_sem=s1_recv.at[a, slot], device_id=to, device_id_type=MESH)

        def p1_landed(a, slot):
            return pltpu.make_async_remote_copy(
                src_ref=r1s[a].at[slot], dst_ref=r1s[a].at[slot], send_sem=s1_send.at[a, slot],
                recv_sem=s1_recv.at[a, slot], device_id=me, device_id_type=MESH)

        def p2(a, k, to):
            return pltpu.make_async_remote_copy(
                src_ref=sbs[a].at[k], dst_ref=r2s[a].at[k],
                send_sem=s2_send.at[a, k], recv_sem=s2_recv.at[a, k], device_id=to, device_id_type=MESH)

        def stage(a, i):
            return pltpu.make_async_copy(srcs[a].at[_dev_index(*order[i][0], c)], sts[a].at[i % 2], st_sem.at[a, i % 2])

        sends = [p1(a, i, sibling) for a in range(nb) for i in (0, 2, 1, 3)]
        for cp in sends:
            cp.start()

        def step_1():
            for a in range(nb):
                stage(a, 0).start()
                for i, (_, slot) in enumerate(order):
                    if i + 1 < len(order):
                        stage(a, i + 1).start()
                    stage(a, i).wait()
                    p1_landed(a, slot).wait_recv()
                    chip_sum = r1s[a][slot] + sts[a][i % 2]
                    if i < 2:
                        sbs[a][i] = chip_sum.astype(BF16)
                        sends.append(p2(a, i, (*n1, c)))
                        sends[-1].start()
                    else:
                        r1s[a][slot] = chip_sum

        _all_gather_blocks((ga_ref, gb_ref), ag_send, ag_recv, meanwhile=step_1)
        for a in range(nb):
            p2(a, 0, me).wait_recv()
            sbs[a][2] = (r1s[a][2 - c] + r2s[a][0].astype(F32)).astype(BF16)
            sends.append(p2(a, 2, (*n2, c)))
            sends[-1].start()
        for a in range(nb):
            p2(a, 1, me).wait_recv()
            p2(a, 2, me).wait_recv()
            total = (r1s[a][0] + r2s[a][1].astype(F32)) + r2s[a][2].astype(F32)
            if outs[a].shape == total.shape:
                outs[a][...] = total
            else:
                r1s[a][0] = total
                for j in range(FLAT_ROWS):
                    outs[a][pl.ds(j, W_IN_SHARD, stride=FLAT_ROWS), :] = r1s[a][0, :, 128 * j:128 * (j + 1)]
        for cp in sends:
            cp.wait_send()

    vm = pl.BlockSpec(memory_space=pltpu.VMEM)
    hbm = pl.BlockSpec(memory_space=pl.ANY)
    blk = [p.shape[1:] for p in big]
    scratch = ([pltpu.VMEM((4,) + b, F32) for b in blk] + [pltpu.VMEM((3,) + b, BF16) for b in blk]
               + [pltpu.VMEM((3,) + b, BF16) for b in blk] + [pltpu.VMEM((2,) + b, F32) for b in blk]
               + [pltpu.SemaphoreType.DMA((nb, 4)), pltpu.SemaphoreType.DMA((nb, 4)),
                  pltpu.SemaphoreType.DMA((nb, 3)), pltpu.SemaphoreType.DMA((nb, 3)),
                  pltpu.SemaphoreType.DMA((2, 7)), pltpu.SemaphoreType.DMA((2, 7)),
                  pltpu.SemaphoreType.DMA((nb, 2))])
    return pl.pallas_call(
        body, name="reduce_grads",
        out_shape=tuple(jax.ShapeDtypeStruct(w.shape, F32) for w in like)
        + (jax.ShapeDtypeStruct((N_DEV,) + pack_a.shape, F32), jax.ShapeDtypeStruct((N_DEV, 8, D_MODEL), F32)),
        in_specs=[hbm] * nb + [vm] * (1 + len(rows)), out_specs=(vm,) * (nb + 2),
        scratch_shapes=scratch,
        compiler_params=_cp(),
    )(*big, pack_a, *rows)


def _rms_hat(xf):
    r = lax.rsqrt(jnp.mean(xf * xf, axis=-1, keepdims=True) + NORM_EPS)
    return xf * r, r


def _shift_up(cur, next_head, s):
    ext = jnp.concatenate([cur, next_head], axis=0)
    n = ext.shape[0]
    return pltpu.roll(ext, n - s, 0)[:cur.shape[0], :]


def _pool_counts(i, tp, w):
    t = i * tp + lax.broadcasted_iota(jnp.int32, (tp, 1), 0)
    return jnp.minimum(t + 1, w).astype(F32)


def _pool_mix(u, u_prev_tail, i, tp):
    win = jnp.concatenate([u_prev_tail, u], axis=0)
    mixes = []
    for gi, w in enumerate(POOL_WINDOWS):
        win = win + pltpu.roll(win, w // 2, 0)
        cols = slice(gi * POOL_GROUP, (gi + 1) * POOL_GROUP)
        mixes.append(win[HALO:, :POOL_GROUP] / _pool_counts(i, tp, w) - u[:, cols])
        if gi + 1 < len(POOL_WINDOWS):
            win = win[:, POOL_GROUP:]
    return mixes


def _prev_tail(ref, i):
    return jnp.where(i > 0, ref[...], 0.0)


def _conv_taps(cur, prev_tail):
    ext = jnp.concatenate([prev_tail, cur], axis=0)
    return [cur] + [pltpu.roll(ext, sft, 0)[HALO:, :] for sft in range(1, CONV_WIDTH)]


def _conv_of_taps(taps, w4):
    y = taps[0] * w4[CONV_WIDTH - 1:CONV_WIDTH, :]
    for sft in range(1, CONV_WIDTH):
        y = y + taps[sft] * w4[CONV_WIDTH - 1 - sft:CONV_WIDTH - sft, :]
    return y


def _conv_fwd(cur, prev_tail, w4):
    ext = jnp.concatenate([prev_tail, cur], axis=0)
    y = ext * w4[CONV_WIDTH - 1:CONV_WIDTH, :]
    for sft in range(1, CONV_WIDTH):
        y = y + pltpu.roll(ext, sft, 0) * w4[CONV_WIDTH - 1 - sft:CONV_WIDTH - sft, :]
    return y[HALO:, :]


def _l2n_heads(t):
    parts = []
    for h in range(DN_HEADS):
        th = t[:, h * DN_HEAD_DIM:(h + 1) * DN_HEAD_DIM]
        parts.append(th * lax.rsqrt(jnp.sum(th * th, axis=-1, keepdims=True) + NORM_EPS))
    return jnp.concatenate(parts, axis=1)


def _post_conv(yq, yk, yv):
    return _l2n_heads(_silu(yq)), _l2n_heads(_silu(yk)), _silu(yv)


def _gates(ba, alog_lane, dtb_lane):
    lane = lax.broadcasted_iota(jnp.int32, ba.shape, 1)
    beta = jax.nn.sigmoid(ba)
    g = -jnp.exp(alog_lane) * _softplus(ba + dtb_lane)
    return jnp.where(lane < DN_HEADS, beta, jnp.where(lane < 2 * DN_HEADS, g, 0.0))


def _front(n_all, g_in, pool_w, pool_scale, g_conv, alog_lane, dtb_lane, w_out_blk, tm):
    s = n_all.shape[0]

    def body(n_ref, pw_ref, ps_ref, gc_ref, al_ref, db_ref, wo_ref, g_hbm,
             pm_ref, pb_ref, yp_ref, qn_ref, kn_ref, vv_ref, gb_ref, gwo_hbm, wt_hbm,
             g_vmem, wt_vmem, tail_u, tail_qkv, gwo_ref, cw_ref, sem, g_sems, wo_send, wo_recv):
        i = pl.program_id(0)

        @pl.when(i == 0)
        def _():
            gwo_ref[_dev_index(*_mesh_pos())] = wo_ref[...].astype(BF16)
            _direct_gather_start(gwo_ref, wo_send, wo_recv)
            loads = [pltpu.make_async_copy(g_hbm.at[d], g_vmem.at[d], g_sems.at[d]) for d in range(N_DEV)]
            for cp in loads:
                cp.start()
            wt_vmem[D_MAIN:, :] = jnp.zeros((D_IN_PAD - D_MAIN, D_MODEL), BF16)
            tail_u[...] = jnp.zeros_like(tail_u)
            tail_qkv[...] = jnp.zeros_like(tail_qkv)
            for d in range(N_DEV):
                cw_ref[:, CONV_SHARD * d:CONV_SHARD * (d + 1)] = gc_ref[d]
            for d, cp in enumerate(loads):
                cp.wait()
                wt_vmem[W_IN_SHARD * d:W_IN_SHARD * (d + 1), :] = g_vmem[d]
            pltpu.make_async_copy(wt_vmem, wt_hbm, sem).start()
        n = n_ref[...]
        pm_ref[...] = _dot_nt_bf(n, wt_vmem[:D_MAIN, :])
        pb = _dot_nt_bf(n, wt_vmem[D_MAIN:, :])
        pb_ref[...] = pb
        u = pm_ref[:, :D_POOL]
        mixes = _pool_mix(u, tail_u[...], i, tm)
        tail_u[...] = u[tm - HALO:, :]
        gate = ps_ref[...] * _silu(pm_ref[:, D_POOL:2 * D_POOL])
        for gi in range(4):
            cols = slice(gi * POOL_GROUP, (gi + 1) * POOL_GROUP)
            yp_ref[:, cols] = _dot_bf(mixes[gi], pw_ref[gi]) * gate[:, cols]
        ys = []
        for c in range(3):
            cols = slice(c * D_DN, (c + 1) * D_DN)
            cur = pm_ref[:, 2 * D_POOL + c * D_DN:2 * D_POOL + (c + 1) * D_DN]
            ys.append(_conv_fwd(cur, tail_qkv[:, cols], cw_ref[:, cols]))
            tail_qkv[:, cols] = cur[tm - HALO:, :]
        qn, kn, vv = _post_conv(*ys)
        qn_ref[...] = qn
        kn_ref[...] = kn
        vv_ref[...] = vv
        gb_ref[...] = _gates(pb, al_ref[...], db_ref[...])

        @pl.when(i == s // tm - 1)
        def _():
            pltpu.make_async_copy(wt_vmem, wt_hbm, sem).wait()
            _direct_gather_wait(gwo_ref, wo_send, wo_recv)
            out = pltpu.make_async_copy(gwo_ref, gwo_hbm, sem)
            out.start()
            out.wait()

    tile = pl.BlockSpec((tm, D_DN), lambda i: (i, 0))
    lanes = pl.BlockSpec((tm, 128), lambda i: (i, 0))
    row = pl.BlockSpec((1, 128), lambda i: (0, 0))
    return pl.pallas_call(
        body, name="front", grid=(s // tm,),
        out_shape=(jax.ShapeDtypeStruct((s, D_MAIN), F32), jax.ShapeDtypeStruct((s, 128), F32),
                   jax.ShapeDtypeStruct((s, D_POOL), F32), jax.ShapeDtypeStruct((s, D_DN), F32),
                   jax.ShapeDtypeStruct((s, D_DN), F32), jax.ShapeDtypeStruct((s, D_DN), F32),
                   jax.ShapeDtypeStruct((s, 128), F32), jax.ShapeDtypeStruct((N_DEV,) + w_out_blk.shape, BF16),
                   jax.ShapeDtypeStruct((D_IN_PAD, D_MODEL), BF16)),
        in_specs=[pl.BlockSpec((tm, D_MODEL), lambda i: (i, 0)),
                  pl.BlockSpec((4, POOL_GROUP, POOL_GROUP), lambda i: (0, 0, 0)),
                  pl.BlockSpec((1, D_POOL), lambda i: (0, 0)),
                  pl.BlockSpec((N_DEV, CONV_WIDTH, CONV_SHARD), lambda i: (0, 0, 0)), row, row,
                  pl.BlockSpec(memory_space=pltpu.VMEM), pl.BlockSpec(memory_space=pl.ANY)],
        out_specs=(pl.BlockSpec((tm, D_MAIN), lambda i: (i, 0)), lanes, tile, tile, tile, tile, lanes,
                   pl.BlockSpec(memory_space=pl.ANY), pl.BlockSpec(memory_space=pl.ANY)),
        scratch_shapes=[pltpu.VMEM((N_DEV, W_IN_SHARD, D_MODEL), BF16), pltpu.VMEM((D_IN_PAD, D_MODEL), BF16),
                        pltpu.VMEM((HALO, D_POOL), F32), pltpu.VMEM((HALO, 3 * D_DN), F32),
                        pltpu.VMEM((N_DEV,) + w_out_blk.shape, BF16), pltpu.VMEM((CONV_WIDTH, 3 * D_DN), F32),
                        pltpu.SemaphoreType.DMA, pltpu.SemaphoreType.DMA((N_DEV,)),
                        pltpu.SemaphoreType.DMA((7,)), pltpu.SemaphoreType.DMA((7,))],
        compiler_params=_cp(("arbitrary",)),
    )(n_all, pool_w, pool_scale, g_conv, alog_lane, dtb_lane, w_out_blk, g_in)


def _dn_block(q, k, v, gcol, bcol, state, known=None):
    nb, n, d = q.shape
    ii = lax.broadcasted_iota(jnp.int32, (n, n), 0)
    jj = lax.broadcasted_iota(jnp.int32, (n, n), 1)
    lower = ii >= jj
    eye = (ii == jj).astype(F32)
    g_row = jnp.sum(eye * gcol, axis=1, keepdims=True)
    gc_col = jnp.sum(jnp.where(lower, g_row, 0.0), axis=2, keepdims=True)
    gc_row = jnp.sum(eye * gc_col, axis=1, keepdims=True)
    decay = jnp.where(lower, jnp.exp(jnp.where(lower, gc_col - gc_row, 0.0)), 0.0)
    kb = k * bcol
    vb = v * bcol
    qs = q * (DN_HEAD_DIM ** -0.5)
    egc = jnp.exp(gc_col)
    kq = jnp.concatenate([kb, qs], axis=1)
    vk = jnp.concatenate([vb, kb * egc], axis=2)
    if known is None:
        akq = _mm(kq, k, 1, 1, *_DN_PREC["akq"])
    else:
        akq = _mm_known(kq, k, known[0][:, :, :n].astype(F32), 1, 1, _DN_PREC["akq"][1])
    a = jnp.where(ii > jj, akq[:, :n] * decay, 0.0)
    qk = akq[:, n:] * decay
    if known is None:
        t = _tri_inv(a, *_DN_PREC["inv"])
        uw = _mm(t, vk, 1, 0, *_DN_PREC["uw"])
    else:
        t = _tri_inv_known(a, known[0][:, :n, n:].astype(F32), _DN_PREC["inv"][1])
        uw = _mm_known(t, vk, known[1], 1, 0, _DN_PREC["uw"][1])
    pre = jnp.concatenate([akq, jnp.concatenate([t, jnp.zeros_like(t)], axis=1)], axis=2)
    wq = jnp.concatenate([uw[:, :, d:], qs * egc], axis=1)
    g_last = gc_col[:, n - 1:n, :]
    k_dec = k * jnp.exp(g_last - gc_col)
    e_last = jnp.exp(g_last)
    os_, starts = [], []
    for c in range(nb // DN_HEADS):
        sl = slice(c * DN_HEADS, (c + 1) * DN_HEADS)
        if known is not None and c > 0:
            state = _use_known(state, known[2][sl])
        starts.append(state)
        ws = _mm(wq[sl], state, 1, 0, *_DN_PREC["ws"])
        v_new = uw[sl, :, :d] - ws[:, :n]
        os_.append(ws[:, n:] + _mm(qk[sl], v_new, 1, 0, *_DN_PREC["ov"]))
        state = state * e_last[sl] + _mm(k_dec[sl], v_new, 0, 0, *_DN_PREC["st"])
    return jnp.concatenate(os_, axis=0), state, (pre, uw, jnp.concatenate(starts, axis=0))


def _gated_norm(o, dz, nw):
    parts = []
    for h in range(DN_HEADS):
        oh = o[:, h * DN_HEAD_DIM:(h + 1) * DN_HEAD_DIM]
        parts.append(oh * lax.rsqrt(jnp.mean(oh * oh, axis=-1, keepdims=True) + NORM_EPS) * nw)
    return jnp.concatenate(parts, axis=1) * _silu(dz)


def _dn_block_args(gc, q_ref, k_ref, v_ref, gb_ref):
    qs, ks, vs, gs, bs = [], [], [], [], []
    for cc in range(gc):
        r = slice(cc * CHUNK, (cc + 1) * CHUNK)
        gbv = gb_ref[r, :]
        for h in range(DN_HEADS):
            cols = slice(h * DN_HEAD_DIM, (h + 1) * DN_HEAD_DIM)
            qs.append(q_ref[r, cols])
            ks.append(k_ref[r, cols])
            vs.append(v_ref[r, cols])
            gs.append(gbv[:, DN_HEADS + h:DN_HEADS + h + 1])
            bs.append(gbv[:, h:h + 1])
    return tuple(jnp.stack(t, axis=0) for t in (qs, ks, vs, gs, bs))


def _dn_scan_fwd(qn, kn, vv, gb, gc):
    s = qn.shape[0]
    nchunk = s // CHUNK
    rows = gc * CHUNK

    def body(q_ref, k_ref, v_ref, gb_ref, y_ref, ss_ref, pre_ref, uw_ref, state):
        @pl.when(pl.program_id(0) == 0)
        def _():
            state[...] = jnp.zeros_like(state)
        q, k, v, gcol, bcol = _dn_block_args(gc, q_ref, k_ref, v_ref, gb_ref)
        y, new, (pre, uw, starts) = _dn_block(q, k, v, gcol, bcol, state[...])
        state[...] = new
        ss_ref[...] = starts
        pre_ref[...] = pre
        uw_ref[...] = uw
        for cc in range(gc):
            for h in range(DN_HEADS):
                y_ref[cc * CHUNK:(cc + 1) * CHUNK, h * DN_HEAD_DIM:(h + 1) * DN_HEAD_DIM] = y[cc * DN_HEADS + h]

    tile = pl.BlockSpec((rows, D_DN), lambda i: (i, 0))
    return pl.pallas_call(
        body, name="dn_scan_fwd", grid=(nchunk // gc,),
        out_shape=(jax.ShapeDtypeStruct((s, D_DN), F32),
                   jax.ShapeDtypeStruct((nchunk * DN_HEADS, DN_HEAD_DIM, DN_HEAD_DIM), F32),
                   jax.ShapeDtypeStruct((nchunk * DN_HEADS, 2 * CHUNK, 2 * CHUNK), F32),
                   jax.ShapeDtypeStruct((nchunk * DN_HEADS, CHUNK, 2 * DN_HEAD_DIM), F32)),
        in_specs=[tile, tile, tile, pl.BlockSpec((rows, 128), lambda i: (i, 0))],
        out_specs=(tile, pl.BlockSpec((gc * DN_HEADS, DN_HEAD_DIM, DN_HEAD_DIM), lambda i: (i, 0, 0)),
                   pl.BlockSpec((gc * DN_HEADS, 2 * CHUNK, 2 * CHUNK), lambda i: (i, 0, 0)),
                   pl.BlockSpec((gc * DN_HEADS, CHUNK, 2 * DN_HEAD_DIM), lambda i: (i, 0, 0))),
        scratch_shapes=[pltpu.VMEM((DN_HEADS, DN_HEAD_DIM, DN_HEAD_DIM), F32)],
        compiler_params=_cp(("arbitrary",)),
    )(qn, kn, vv, gb)


def _out_proj_loss(y_pool, o_dn, proj_main, dn_norm_w, x2, tgt, w_out_full, fnw, tm):
    s = x2.shape[0]

    def body(yp_ref, o_ref, dz_ref, nw_ref, x_ref, t_ref, wo_ref, fw_ref,
             dh_ref, dyp_ref, do_ref, ddz_ref, gwo_ref, gfw_ref, loss_ref, dnw_ref):
        @pl.when(pl.program_id(0) == 0)
        def _():
            gwo_ref[...] = jnp.zeros_like(gwo_ref)
            gfw_ref[...] = jnp.zeros_like(gfw_ref)
            loss_ref[...] = jnp.zeros_like(loss_ref)
            dnw_ref[...] = jnp.zeros_like(dnw_ref)
        y_dn, gate_vjp = jax.vjp(_gated_norm, o_ref[...], dz_ref[...], nw_ref[...])
        y = jnp.concatenate([yp_ref[...], y_dn], axis=1).astype(BF16)
        wo = wo_ref[...]
        h = x_ref[...] + jnp.dot(y, wo, preferred_element_type=F32)
        hn, r = _rms_hat(h)
        fw = fw_ref[...]
        err = hn * fw - t_ref[...]
        loss_ref[...] += 0.5 * jnp.sum(jnp.sum(err * err, axis=-1, keepdims=True) / D_MODEL, axis=0, keepdims=True)
        dout = err / D_MODEL
        gfw_ref[...] += jnp.sum(dout * hn, axis=0, keepdims=True)
        dhn = dout * fw
        dh = r * (dhn - hn * jnp.mean(dhn * hn, axis=-1, keepdims=True))
        dh_ref[...] = dh
        dhb = dh.astype(BF16)
        dy = _dot_nt_bf(dhb, wo)
        dyp_ref[...] = dy[:, :D_POOL]
        do, ddz, dnw = gate_vjp(dy[:, D_POOL:])
        do_ref[...] = do
        ddz_ref[...] = ddz
        dnw_ref[...] += dnw
        gwo_ref[...] += _dot_tn_bf(y, dhb)

    half = pl.BlockSpec((tm, D_POOL), lambda i: (i, 0))
    full = pl.BlockSpec((tm, D_MODEL), lambda i: (i, 0))
    lrow = pl.BlockSpec((1, 128), lambda i: (0, 0))
    return pl.pallas_call(
        body, name="out_proj_loss", grid=(s // tm,),
        out_shape=(jax.ShapeDtypeStruct((s, D_MODEL), F32), jax.ShapeDtypeStruct((s, D_POOL), F32),
                   jax.ShapeDtypeStruct((s, D_DN), F32), jax.ShapeDtypeStruct((s, D_DN), F32),
                   jax.ShapeDtypeStruct((D_MODEL, D_MODEL), F32),
                   jax.ShapeDtypeStruct((1, D_MODEL), F32), jax.ShapeDtypeStruct((1, 128), F32),
                   jax.ShapeDtypeStruct((1, 128), F32)),
        in_specs=[half, half, pl.BlockSpec((tm, D_DN), lambda i: (i, 5)), lrow, full, full,
                  pl.BlockSpec((D_MODEL, D_MODEL), lambda i: (0, 0)), pl.BlockSpec((1, D_MODEL), lambda i: (0, 0))],
        out_specs=(full, half, half, half, pl.BlockSpec((D_MODEL, D_MODEL), lambda i: (0, 0)),
                   pl.BlockSpec((1, D_MODEL), lambda i: (0, 0)), lrow, lrow),
        compiler_params=_cp(("arbitrary",)),
    )(y_pool, o_dn, proj_main, dn_norm_w, x2, tgt, w_out_full, fnw)


def _dn_scan_bwd(qn, kn, vv, gb, states, pre, uw, do_dn, p_out, gc):
    s = qn.shape[0]
    nchunk = s // CHUNK
    nstep = nchunk // gc
    rows = gc * CHUNK

    def body(q_ref, k_ref, v_ref, gb_ref, ss_ref, pre_ref, uw_ref, dy_ref, po_ref,
             dq_ref, dk_ref, dv_ref, dgb_ref, gro_ref, dstate, po_send, po_recv, rs_send, rs_recv):
        @pl.when(pl.program_id(0) == 0)
        def _():
            dstate[...] = jnp.zeros_like(dstate)
            po_send[...] = po_ref[...].astype(BF16)
            _direct_scatter_start(po_send, po_recv, rs_send, rs_recv)

        @pl.when(pl.program_id(0) == nstep - 1)
        def _():
            _direct_scatter_wait(po_send, po_recv, rs_send, rs_recv)
            total = po_ref[_dev_index(*_mesh_pos())]
            for k in range(N_DEV - 1):
                total = total + po_recv[k].astype(F32)
            gro_ref[...] = total
        lane = lax.broadcasted_iota(jnp.int32, (CHUNK, 128), 1)
        q, k, v, gcol, bcol = _dn_block_args(gc, q_ref, k_ref, v_ref, gb_ref)
        dy = jnp.stack([dy_ref[cc * CHUNK:(cc + 1) * CHUNK, h * DN_HEAD_DIM:(h + 1) * DN_HEAD_DIM]
                        for cc in range(gc) for h in range(DN_HEADS)], axis=0)
        known = (pre_ref[...], uw_ref[...], ss_ref[...])
        _, vjp = jax.vjp(lambda *a: _dn_block(*a, known=known)[:2], q, k, v, gcol, bcol, ss_ref[:DN_HEADS])
        dq, dk, dv, dg, db, dst = vjp((dy, dstate[...]))
        dstate[...] = dst
        for cc in range(gc):
            r = slice(cc * CHUNK, (cc + 1) * CHUNK)
            dgb = jnp.zeros((CHUNK, 128), F32)
            for h in range(DN_HEADS):
                b = cc * DN_HEADS + h
                cols = slice(h * DN_HEAD_DIM, (h + 1) * DN_HEAD_DIM)
                for ref, val in zip((dq_ref, dk_ref, dv_ref), (dq, dk, dv)):
                    ref[r, cols] = val[b]
                dgb = dgb + jnp.where(lane == h, db[b], 0.0) + jnp.where(lane == DN_HEADS + h, dg[b], 0.0)
            dgb_ref[r, :] = dgb

    rev = lambda i: (nstep - 1 - i, 0)
    tile = pl.BlockSpec((rows, D_DN), rev)
    lanes = pl.BlockSpec((rows, 128), rev)
    return pl.pallas_call(
        body, name="dn_scan_bwd", grid=(nstep,),
        out_shape=(jax.ShapeDtypeStruct((s, D_DN), F32),) * 3
        + (jax.ShapeDtypeStruct((s, 128), F32), jax.ShapeDtypeStruct(p_out.shape[1:], F32)),
        in_specs=[tile, tile, tile, lanes,
                  pl.BlockSpec((gc * DN_HEADS, DN_HEAD_DIM, DN_HEAD_DIM), lambda i: (nstep - 1 - i, 0, 0)),
                  pl.BlockSpec((gc * DN_HEADS, 2 * CHUNK, 2 * CHUNK), lambda i: (nstep - 1 - i, 0, 0)),
                  pl.BlockSpec((gc * DN_HEADS, CHUNK, 2 * DN_HEAD_DIM), lambda i: (nstep - 1 - i, 0, 0)), tile,
                  pl.BlockSpec(memory_space=pltpu.VMEM)],
        out_specs=(tile, tile, tile, lanes, pl.BlockSpec(memory_space=pltpu.VMEM)),
        scratch_shapes=[pltpu.VMEM((DN_HEADS, DN_HEAD_DIM, DN_HEAD_DIM), F32),
                        pltpu.VMEM(p_out.shape, BF16), pltpu.VMEM((N_DEV - 1,) + p_out.shape[1:], BF16),
                        pltpu.SemaphoreType.DMA((7,)), pltpu.SemaphoreType.DMA((7,))],
        compiler_params=_cp(("arbitrary",)),
    )(qn, kn, vv, gb, states, pre, uw, do_dn, p_out)


def _back(proj_main, proj_ba, dyp, dqn, dkn, dvv, dgb, ddz, x2, dh, norm_w, pool_w, pool_scale, g_conv,
          alog_lane, dtb_lane, wt_full, tm):
    s = x2.shape[0]
    nstep = s // tm
    per = tm // HALO

    def body(u_ref, z_ref, q_ref, k_ref, v_ref, up_ref, qp_ref, kp_ref, vp_ref, ba_ref,
             dyp_ref, dqn_ref, dkn_ref, dvv_ref, dgb_ref, ddz_ref, x_ref, dh_ref,
             nw_ref, pw_ref, ps_ref, gc_ref, al_ref, db_ref, wt_hbm,
             gx_ref, p_hbm, gnw_ref, dpw_ref, dps_ref, pc_ref, dal_ref, ddb_ref,
             wt_vmem, acc, blk, head_dc, head_dw, cw_ref, dcw_ref, sem, osem):
        j = pl.program_id(0)
        i = nstep - 1 - j

        @pl.when(j == 0)
        def _():
            cp = pltpu.make_async_copy(wt_hbm, wt_vmem, sem)
            cp.start()
            acc[...] = jnp.zeros_like(acc)
            for ref in (gnw_ref, dpw_ref, dps_ref, dcw_ref, dal_ref, ddb_ref, head_dc, head_dw):
                ref[...] = jnp.zeros_like(ref)
            cp.wait()

        for d in range(N_DEV):
            cw_ref[:, CONV_SHARD * d:CONV_SHARD * (d + 1)] = gc_ref[d]

        u = u_ref[...]
        z = z_ref[...]
        dy = dyp_ref[...]
        ps = ps_ref[...]
        mixes = _pool_mix(u, _prev_tail(up_ref, i), i, tm)
        sg = jax.nn.sigmoid(z)
        sz = z * sg
        dsz = sg * (1.0 + z * (1.0 - sg))
        dzs, dwins = [], []
        for gi, w in enumerate(POOL_WINDOWS):
            cols = slice(gi * POOL_GROUP, (gi + 1) * POOL_GROUP)
            mixw = _dot_bf(mixes[gi], pw_ref[gi])
            dmixw = dy[:, cols] * ps[:, cols] * sz[:, cols]
            dps_ref[:, cols] += jnp.sum(dy[:, cols] * mixw * sz[:, cols], axis=0, keepdims=True)
            dzs.append(dy[:, cols] * mixw * ps[:, cols] * dsz[:, cols])
            dpw_ref[gi] += _dot_tn_bf(mixes[gi], dmixw)
            dwins.append(_dot_nt_bf(dmixw, pw_ref[gi]) / _pool_counts(i, tm, w))
        dzp = jnp.concatenate(dzs, axis=1)
        dw = jnp.concatenate(dwins, axis=1)
        win = jnp.concatenate([dw, head_dw[...]], axis=0)
        m = win.shape[0]
        dups = []
        for gi, w in enumerate(POOL_WINDOWS):
            win = win + pltpu.roll(win, m - w // 2, 0)
            cols = slice(gi * POOL_GROUP, (gi + 1) * POOL_GROUP)
            dups.append(win[:tm, :POOL_GROUP] - dw[:, cols] * _pool_counts(i, tm, w))
            if gi + 1 < len(POOL_WINDOWS):
                win = win[:, POOL_GROUP:]
        dup = jnp.concatenate(dups, axis=1)
        head_dw[...] = dw[:HALO, :]

        curs = (q_ref[...], k_ref[...], v_ref[...])
        tails = (_prev_tail(qp_ref, i), _prev_tail(kp_ref, i), _prev_tail(vp_ref, i))
        taps = [_conv_taps(curs[c], tails[c]) for c in range(3)]
        ys = [_conv_of_taps(taps[c], cw_ref[:, c * D_DN:(c + 1) * D_DN]) for c in range(3)]
        _, vjp = jax.vjp(_post_conv, *ys)
        dys = vjp((dqn_ref[...], dkn_ref[...], dvv_ref[...]))
        dxs = []
        for c, dyc in enumerate(dys):
            cols = slice(c * D_DN, (c + 1) * D_DN)
            w4 = cw_ref[:, cols]
            for sft in range(CONV_WIDTH):
                row = CONV_WIDTH - 1 - sft
                dcw_ref[row:row + 1, cols] += jnp.sum(dyc * taps[c][sft], axis=0, keepdims=True)
            head = head_dc[:, cols]
            dx = dyc * w4[CONV_WIDTH - 1:CONV_WIDTH, :]
            for sft in range(1, CONV_WIDTH):
                dx = dx + _shift_up(dyc, head, sft) * w4[CONV_WIDTH - 1 - sft:CONV_WIDTH - sft, :]
            dxs.append(dx)
            head_dc[:, cols] = dyc[:HALO, :]
        _, gvjp = jax.vjp(_gates, ba_ref[...], al_ref[...], db_ref[...])
        dba, dal, ddb = gvjp(dgb_ref[...])
        dal_ref[...] += dal
        ddb_ref[...] += ddb

        dbab = dba.astype(BF16)
        xhat, r = _rms_hat(x_ref[...])
        nw = nw_ref[...]
        n = (xhat * nw).astype(BF16)
        acc[D_MAIN:, :] += _dot_tn_bf(dbab, n)
        dn = jnp.dot(dbab, wt_vmem[D_MAIN:, :], preferred_element_type=F32)
        for cb, d in enumerate((dup, dzp, dxs[0], dxs[1], dxs[2], ddz_ref[...])):
            rows = slice(cb * D_POOL, (cb + 1) * D_POOL)
            dpart = d.astype(BF16)
            acc[rows, :] += _dot_tn_bf(dpart, n)
            dn = dn + jnp.dot(dpart, wt_vmem[rows, :], preferred_element_type=F32)
        gnw_ref[...] += jnp.sum(dn * xhat, axis=0, keepdims=True)
        dxh = dn * nw
        gx_ref[...] = dh_ref[...] + r * (dxh - xhat * jnp.mean(dxh * xhat, axis=-1, keepdims=True))

        @pl.when(j == nstep - 1)
        def _():
            for d in range(N_DEV):
                pc_ref[d] = dcw_ref[:, CONV_SHARD * d:CONV_SHARD * (d + 1)]

            def out(d):
                return pltpu.make_async_copy(blk.at[d % 2], p_hbm.at[d], osem.at[d % 2])
            for d in range(N_DEV):
                if d >= 2:
                    out(d - 2).wait()
                blk[d % 2] = acc[W_IN_SHARD * d:W_IN_SHARD * (d + 1), :]
                out(d).start()
            out(N_DEV - 2).wait()
            out(N_DEV - 1).wait()

    def col(c):
        return pl.BlockSpec((tm, D_POOL), lambda j: (nstep - 1 - j, c))

    def halo(c):
        return pl.BlockSpec((HALO, D_POOL), lambda j: (jnp.maximum((nstep - 1 - j) * per - 1, 0), c))

    rev = lambda j: (nstep - 1 - j, 0)
    part = pl.BlockSpec((tm, D_POOL), rev)
    lanes = pl.BlockSpec((tm, 128), rev)
    full = pl.BlockSpec((tm, D_MODEL), rev)
    row = pl.BlockSpec((1, D_MODEL), lambda j: (0, 0))
    lrow = pl.BlockSpec((1, 128), lambda j: (0, 0))
    pw = pl.BlockSpec((4, POOL_GROUP, POOL_GROUP), lambda j: (0, 0, 0))
    psp = pl.BlockSpec((1, D_POOL), lambda j: (0, 0))
    cw = pl.BlockSpec((N_DEV, CONV_WIDTH, CONV_SHARD), lambda j: (0, 0, 0))
    return pl.pallas_call(
        body, name="back", grid=(nstep,),
        out_shape=(jax.ShapeDtypeStruct((s, D_MODEL), F32),
                   jax.ShapeDtypeStruct((N_DEV, W_IN_SHARD, D_MODEL), F32), jax.ShapeDtypeStruct((1, D_MODEL), F32),
                   jax.ShapeDtypeStruct((4, POOL_GROUP, POOL_GROUP), F32), jax.ShapeDtypeStruct((1, D_POOL), F32),
                   jax.ShapeDtypeStruct((N_DEV, CONV_WIDTH, CONV_SHARD), F32),
                   jax.ShapeDtypeStruct((1, 128), F32), jax.ShapeDtypeStruct((1, 128), F32)),
        in_specs=[col(0), col(1), col(2), col(3), col(4), halo(0), halo(2), halo(3), halo(4), lanes,
                  part, part, part, part, lanes, part, full, full,
                  row, pw, psp, cw, lrow, lrow, pl.BlockSpec(memory_space=pl.ANY)],
        out_specs=(full, pl.BlockSpec(memory_space=pl.ANY), row, pw, psp, cw, lrow, lrow),
        scratch_shapes=[pltpu.VMEM((D_IN_PAD, D_MODEL), BF16), pltpu.VMEM((D_IN_PAD, D_MODEL), F32),
                        pltpu.VMEM((2, W_IN_SHARD, D_MODEL), F32),
                        pltpu.VMEM((HALO, 3 * D_DN), F32), pltpu.VMEM((HALO, D_POOL), F32),
                        pltpu.VMEM((CONV_WIDTH, 3 * D_DN), F32), pltpu.VMEM((CONV_WIDTH, 3 * D_DN), F32),
                        pltpu.SemaphoreType.DMA, pltpu.SemaphoreType.DMA((2,))],
        compiler_params=_cp(("arbitrary",)),
    )(proj_main, proj_main, proj_main, proj_main, proj_main, proj_main, proj_main, proj_main, proj_main, proj_ba,
      dyp, dqn, dkn, dvv, dgb, ddz, x2, dh, norm_w, pool_w, pool_scale, g_conv, alog_lane, dtb_lane, wt_full)


def _adamw_math(w, g, m, v):
    m = ADAM_B1 * m + (1.0 - ADAM_B1) * g
    v = ADAM_B2 * v + (1.0 - ADAM_B2) * (g * g)
    m_hat = m / (1.0 - ADAM_B1 ** ADAM_STEP)
    v_hat = v / (1.0 - ADAM_B2 ** ADAM_STEP)
    delta = -ADAM_LR * (m_hat / (jnp.sqrt(v_hat) + ADAM_EPS) + ADAM_WD * w)
    return delta, m, v


def _adamw(tiled, whole, gath_a, gath_b, pool, rows, lanes, nstep):
    kw, nrow = len(whole), len(rows)
    ins = list(tiled) + [a for p in whole for a in p] + [gath_a, gath_b] + list(pool) + [a for wmv in rows for a in wmv]

    def body(*refs):
        in_refs, outs = refs[:len(ins)], refs[len(ins):]

        def update(w, g, m, v, o):
            dl, nm, nv = _adamw_math(w, g, m, v)
            for ref, val in zip(o, (g, dl, nm, nv)):
                ref[...] = val

        update(*(r[...] for r in in_refs[:4]), outs[:4])

        @pl.when(pl.program_id(0) == 0)
        def _():
            for p in range(1, kw + 1):
                update(*(r[...] for r in in_refs[4 * p:4 * p + 4]), outs[4 * p:4 * p + 4])
            ga_ref, gb_ref = in_refs[4 * kw + 4:4 * kw + 6]
            rep_in, rep_out = in_refs[4 * kw + 6:], outs[4 * kw + 4:]

            def total(ref):
                g = ref[0]
                for d in range(1, N_DEV):
                    g = g + ref[d]
                return g

            gs = [total(ga_ref)]
            gb = total(gb_ref)
            for r in range(nrow):
                n = rep_in[3 * (r + 1)].shape[1]
                head = gb if lanes[r] == 0 else pltpu.roll(gb[:, :128], 128 - lanes[r], 1)
                gs.append(head[r:r + 1, :n])
            for r, g in enumerate(gs):
                w, m, v = (ref[...] for ref in rep_in[3 * r:3 * r + 3])
                update(w, g, m, v, rep_out[4 * r:4 * r + 4])
            rep_out[4 * (nrow + 1)][...] = gb[nrow:nrow + 1, 0:1]

    tile = pl.BlockSpec((tiled[0].shape[0] // nstep, tiled[0].shape[1]), lambda i: (i, 0))

    def full(a):
        return pl.BlockSpec(a.shape, lambda i, nd=a.ndim: (0,) * nd)

    firsts = [p[0] for p in whole] + [pool[0]] + [wmv[0] for wmv in rows]
    out_shape = ([jax.ShapeDtypeStruct(tiled[0].shape, F32)] * 4
                 + [jax.ShapeDtypeStruct(w.shape, F32) for w in firsts for _ in range(4)]
                 + [jax.ShapeDtypeStruct((1, 1), F32)])
    res = pl.pallas_call(
        body, name="adamw", grid=(nstep,),
        in_specs=[tile] * 4 + [full(a) for a in ins[4:]],
        out_specs=tuple([tile] * 4 + [full(o) for o in out_shape[4:]]),
        out_shape=tuple(out_shape),
        compiler_params=_cp(("arbitrary",)),
    )(*ins)
    groups = [tuple(res[4 * k:4 * k + 4]) for k in range(kw + nrow + 2)]
    return groups[0], groups[1:kw + 1], groups[kw + 1:], res[-1]


_ROW_ORDER = ("norm_w", "final_norm_w", "pool_scale", "dn_norm_w", "a_log", "dt_bias")
_ROW_LANE = (0, 0, 0, 0, DN_HEADS, DN_HEADS)


def _lane_row(vec4, start):
    return jnp.pad(vec4.reshape(-1), (start, 128 - start - vec4.size)).reshape(1, 128)


def kernel(x, norm_w, w_in, pool_w, pool_scale, conv_w, a_log, dt_bias, dn_norm_w, w_out, final_norm_w, loss_target, m_norm_w, m_w_in, m_pool_w, m_pool_scale, m_conv_w, m_a_log, m_dt_bias, m_dn_norm_w, m_w_out, m_final_norm_w, v_norm_w, v_w_in, v_pool_w, v_pool_scale, v_conv_w, v_a_log, v_dt_bias, v_dn_norm_w, v_w_out, v_final_norm_w):
    s = x.shape[1]
    tm = min(512, s)
    tmb = min(256, s)
    x2 = x[0]
    tgt = loss_target[0]
    def to_flat(a):
        return a[0].reshape(FLAT_ROWS, 128, W_IN_SHARD).transpose(2, 0, 1).reshape(W_IN_SHARD * FLAT_ROWS, 128)

    def from_flat(f):
        return f.reshape(W_IN_SHARD, FLAT_ROWS, 128).transpose(1, 2, 0).reshape(1, D_MODEL, W_IN_SHARD)

    wf, m_wf, v_wf = to_flat(w_in), to_flat(m_w_in), to_flat(v_w_in)

    g_in, g_conv, n_all = _gather_weights(wf, conv_w[0], x2, norm_w, tm)
    alog_lane = _lane_row(a_log, DN_HEADS)
    dtb_lane = _lane_row(dt_bias, DN_HEADS)
    fnw = final_norm_w.reshape(1, D_MODEL)

    proj_main, proj_ba, y_pool, qn, kn, vv, gb, g_out, wt_full = _front(
        n_all, g_in, pool_w[0], pool_scale, g_conv, alog_lane, dtb_lane, w_out[0], tm)
    w_out_full = g_out.reshape(D_MODEL, D_MODEL)
    o_dn, states, dn_pre, dn_uw = _dn_scan_fwd(qn, kn, vv, gb, DN_CHUNKS_PER_STEP)

    dh, dyp, do_dn, ddz, g_wout, g_fnw, loss_part, g_dnw = _out_proj_loss(
        y_pool, o_dn, proj_main, dn_norm_w, x2, tgt, w_out_full, fnw, tm)
    p_out = g_wout.reshape(N_DEV, D_MODEL // N_DEV, D_MODEL)
    dqn, dkn, dvv, dgb, gr_out = _dn_scan_bwd(qn, kn, vv, gb, states, dn_pre, dn_uw, do_dn, p_out, DN_CHUNKS_PER_STEP)
    grad_x2, p_in, g_nw, g_pw, g_ps, p_conv, g_al, g_db = _back(
        proj_main, proj_ba, dyp, dqn, dkn, dvv, dgb, ddz, x2, dh, norm_w, pool_w[0], pool_scale, g_conv,
        alog_lane, dtb_lane, wt_full, tmb)

    pack_a = g_pw.reshape(4 * POOL_GROUP, POOL_GROUP)
    gr_in, gr_conv, gath_a, gath_b = _reduce_grads((p_in, p_conv), (wf, conv_w[0]), pack_a,
                                                   [g_nw, g_fnw, g_ps, g_dnw, g_al, g_db, loss_part])

    flat = lambda a: a.reshape(4 * POOL_GROUP, POOL_GROUP)
    row = lambda a: a.reshape(1, -1)
    vecs = {"norm_w": (norm_w, m_norm_w, v_norm_w), "final_norm_w": (final_norm_w, m_final_norm_w, v_final_norm_w),
            "pool_scale": (pool_scale, m_pool_scale, v_pool_scale), "dn_norm_w": (dn_norm_w, m_dn_norm_w, v_dn_norm_w),
            "a_log": (a_log, m_a_log, v_a_log), "dt_bias": (dt_bias, m_dt_bias, v_dt_bias)}
    r_in, (r_out, r_conv), res, loss = _adamw(
        (wf, gr_in, m_wf, v_wf),
        [(w_out[0], gr_out, m_w_out[0], v_w_out[0]), (conv_w[0], gr_conv, m_conv_w[0], v_conv_w[0])],
        gath_a, gath_b, (flat(pool_w), flat(m_pool_w), flat(v_pool_w)),
        [tuple(row(a) for a in vecs[nm]) for nm in _ROW_ORDER], _ROW_LANE, ADAMW_STEPS)
    r_pool = res[0]
    r_vec = dict(zip(_ROW_ORDER, res[1:]))

    def group(k):
        vec = lambda nm: r_vec[nm][k].reshape(vecs[nm][0].shape)
        return (vec("norm_w"), from_flat(r_in[k]), r_pool[k].reshape(pool_w.shape), vec("pool_scale"), r_conv[k][None],
                vec("a_log"), vec("dt_bias"), vec("dn_norm_w"), r_out[k][None], vec("final_norm_w"))

    return (loss[0, 0], grad_x2[None], *group(0), *group(1), *group(2), *group(3))
```
